```python
import math
import jax, jax.numpy as jnp
from jax import lax
import numpy as np

D_MODEL = 1024
BATCH = 8
SEQ = 8192
DEPTH = 4

GRID_W = 64
ROPE_THETA = 10000.0
Q_BLOCK = 128
EPS = 1e-6

GQA_HEADS = 8
GQA_KV_HEADS = 2
GQA_GROUP = GQA_HEADS // GQA_KV_HEADS
GQA_HEAD_DIM = D_MODEL // 16
GQA_Q_W = GQA_HEADS * GQA_HEAD_DIM
GQA_KV_W = GQA_KV_HEADS * GQA_HEAD_DIM

MLA_HEADS = 8
MLA_NOPE_DIM = D_MODEL // 16
MLA_ROPE_DIM = D_MODEL // 32
MLA_V_DIM = D_MODEL // 16
MLA_QK_DIM = MLA_NOPE_DIM + MLA_ROPE_DIM
MLA_Q_RANK = (3 * D_MODEL) // 8
MLA_KV_RANK = D_MODEL // 4
MLA_OUT_W = MLA_HEADS * MLA_V_DIM

D_FF = 4 * D_MODEL

SPLIT_SIZES = (GQA_Q_W, GQA_KV_W, GQA_KV_W, MLA_Q_RANK, MLA_KV_RANK, MLA_ROPE_DIM, 2 * D_MODEL)
IN_W = sum(SPLIT_SIZES)
SPLIT_POINTS = [int(v) for v in np.cumsum(SPLIT_SIZES)[:-1]]

kernel_name = "hybrid_gqa_mla_sandwich_encoder"


def rmsnorm(x, g):
    xf = x.astype(jnp.float32)
    y = xf * lax.rsqrt(jnp.mean(xf * xf, axis=-1, keepdims=True) + EPS)
    return (y * g.astype(jnp.float32)).astype(x.dtype)


def axial_rope_tables(seq, rot_dim):
    rows = seq // GRID_W
    row = jnp.repeat(jnp.arange(rows, dtype=jnp.float32), GRID_W)
    col = jnp.tile(jnp.arange(GRID_W, dtype=jnp.float32), rows)
    half = rot_dim // 2
    inv = ROPE_THETA ** (-jnp.arange(0, half, 2, dtype=jnp.float32) / half)
    ar = row[:, None] * inv[None, :]
    ac = col[:, None] * inv[None, :]
    ang = jnp.concatenate([ar, ar, ac, ac], axis=-1)
    return jnp.cos(ang), jnp.sin(ang)


def apply_axial_rope(x, cos, sin):
    d = x.shape[-1]
    h = d // 2
    q = h // 2
    shape = (cos.shape[0],) + (1,) * (x.ndim - 3) + (d,)
    c = cos.reshape(shape).astype(x.dtype)
    s = sin.reshape(shape).astype(x.dtype)
    xr, xc = x[..., :h], x[..., h:]
    rot = lambda z: jnp.concatenate([-z[..., q:], z[..., :q]], axis=-1)
    x_rot = jnp.concatenate([rot(xr), rot(xc)], axis=-1)
    return x * c + x_rot * s


def blocked_attention(q, k, v, scale):
    b, s, hk, g, dk = q.shape
    dv = v.shape[-1]
    nb = s // Q_BLOCK
    qb = q.reshape(b, nb, Q_BLOCK, hk, g, dk).swapaxes(0, 1)

    def one_block(qblk):
        sc = jnp.einsum('bqhgd,bkhd->bhgqk', qblk, k).astype(jnp.float32) * scale
        p = jax.nn.softmax(sc, axis=-1).astype(v.dtype)
        return jnp.einsum('bhgqk,bkhd->bqhgd', p, v)

    out = lax.map(one_block, qb)
    return out.swapaxes(0, 1).reshape(b, s, hk * g * dv)


def token_mixer(u, w_in, b_gate, q_norm_g, k_norm_g, q_a_norm_g, kv_a_norm_g,
                w_q_up, w_kv_up, w_branch_a, w_branch_b, w_o,
                cos_a, sin_a, cos_b, sin_b):
    b, s, _ = u.shape
    z = u @ w_in
    qa, ka, va, cq, ckv, kr, gl = jnp.split(z, SPLIT_POINTS, axis=-1)

    qa = qa.reshape(b, s, GQA_HEADS, GQA_HEAD_DIM)
    ka = ka.reshape(b, s, GQA_KV_HEADS, GQA_HEAD_DIM)
    va = va.reshape(b, s, GQA_KV_HEADS, GQA_HEAD_DIM)
    qa = apply_axial_rope(rmsnorm(qa, q_norm_g), cos_a, sin_a)
    ka = apply_axial_rope(rmsnorm(ka, k_norm_g), cos_a, sin_a)
    qa = qa.reshape(b, s, GQA_KV_HEADS, GQA_GROUP, GQA_HEAD_DIM)
    ya = blocked_attention(qa, ka, va, 1.0 / math.sqrt(GQA_HEAD_DIM))

    qb = (rmsnorm(cq, q_a_norm_g) @ w_q_up).reshape(b, s, MLA_HEADS, MLA_QK_DIM)
    q_nope, q_rope = qb[..., :MLA_NOPE_DIM], qb[..., MLA_NOPE_DIM:]
    q_rope = apply_axial_rope(q_rope, cos_b, sin_b)
    kvb = (rmsnorm(ckv, kv_a_norm_g) @ w_kv_up).reshape(b, s, MLA_HEADS, MLA_NOPE_DIM + MLA_V_DIM)
    k_nope, vb = kvb[..., :MLA_NOPE_DIM], kvb[..., MLA_NOPE_DIM:]
    k_rope = apply_axial_rope(kr, cos_b, sin_b)
    k_rope = jnp.broadcast_to(k_rope[:, :, None, :], (b, s, MLA_HEADS, MLA_ROPE_DIM))
    qb = jnp.concatenate([q_nope, q_rope], axis=-1)[:, :, :, None, :]
    kb = jnp.concatenate([k_nope, k_rope], axis=-1)
    yb = blocked_attention(qb, kb, vb, 1.0 / math.sqrt(MLA_QK_DIM))

    gates = jax.nn.sigmoid((gl + b_gate).astype(jnp.float32)).astype(u.dtype)
    g_a, g_b = gates[..., :D_MODEL], gates[..., D_MODEL:]
    merged = g_a * (ya @ w_branch_a) + g_b * (yb @ w_branch_b)
    return merged @ w_o


def _fwd_setup_inputs(seed: int = 0) -> dict:
    key = jax.random.key(seed)
    ks = jax.random.split(key, 20)
    f32 = jnp.float32

    def w(k, fan_in, fan_out):
        return jax.random.normal(k, (DEPTH, fan_in, fan_out), f32) * fan_in ** -0.5

    def gain(k, n):
        return 1.0 + 0.05 * jax.random.normal(k, (DEPTH, n), f32)

    return {
        "x": jax.random.normal(ks[0], (BATCH, SEQ, D_MODEL), f32),
        "w_in": w(ks[1], D_MODEL, IN_W),
        "b_gate": 0.1 * jax.random.normal(ks[2], (DEPTH, 2 * D_MODEL), f32),
        "q_norm_g": gain(ks[3], GQA_HEAD_DIM),
        "k_norm_g": gain(ks[4], GQA_HEAD_DIM),
        "q_a_norm_g": gain(ks[5], MLA_Q_RANK),
        "kv_a_norm_g": gain(ks[6], MLA_KV_RANK),
        "w_q_up": w(ks[7], MLA_Q_RANK, MLA_HEADS * MLA_QK_DIM),
        "w_kv_up": w(ks[8], MLA_KV_RANK, MLA_HEADS * (MLA_NOPE_DIM + MLA_V_DIM)),
        "w_branch_a": w(ks[9], GQA_Q_W, D_MODEL),
        "w_branch_b": w(ks[10], MLA_OUT_W, D_MODEL),
        "w_o": w(ks[11], D_MODEL, D_MODEL),
        "w_ffn_up": w(ks[12], D_MODEL, D_FF),
        "w_ffn_down": w(ks[13], D_FF, D_MODEL),
        "pre_mix_g": gain(ks[14], D_MODEL),
        "post_mix_g": gain(ks[15], D_MODEL),
        "pre_ffn_g": gain(ks[16], D_MODEL),
        "post_ffn_g": gain(ks[17], D_MODEL),
    }


def _fwd_reference(x, w_in, b_gate, q_norm_g, k_norm_g, q_a_norm_g, kv_a_norm_g,
              w_q_up, w_kv_up, w_branch_a, w_branch_b, w_o, w_ffn_up, w_ffn_down,
              pre_mix_g, post_mix_g, pre_ffn_g, post_ffn_g):
    seq = x.shape[1]
    cos_a, sin_a = axial_rope_tables(seq, GQA_HEAD_DIM)
    cos_b, sin_b = axial_rope_tables(seq, MLA_ROPE_DIM)
    for l in range(DEPTH):
        u = rmsnorm(x, pre_mix_g[l])
        m = token_mixer(u, w_in[l], b_gate[l], q_norm_g[l], k_norm_g[l],
                        q_a_norm_g[l], kv_a_norm_g[l], w_q_up[l], w_kv_up[l],
                        w_branch_a[l], w_branch_b[l], w_o[l],
                        cos_a, sin_a, cos_b, sin_b)
        x = x + rmsnorm(m, post_mix_g[l])
        h = rmsnorm(x, pre_ffn_g[l]) @ w_ffn_up[l]
        f = jnp.square(jax.nn.relu(h)) @ w_ffn_down[l]
        x = x + rmsnorm(f, post_ffn_g[l])
    return x


import jax as _jax
import jax.numpy as _jnp

TWIN_FORMAT = 'train_step'
FWD_PARAMS = ['x', 'w_in', 'b_gate', 'q_norm_g', 'k_norm_g', 'q_a_norm_g', 'kv_a_norm_g', 'w_q_up', 'w_kv_up', 'w_branch_a', 'w_branch_b', 'w_o', 'w_ffn_up', 'w_ffn_down', 'pre_mix_g', 'post_mix_g', 'pre_ffn_g', 'post_ffn_g']
TWIN_WEIGHTS = ['w_in', 'b_gate', 'q_norm_g', 'k_norm_g', 'q_a_norm_g', 'kv_a_norm_g', 'w_q_up', 'w_kv_up', 'w_branch_a', 'w_branch_b', 'w_o', 'w_ffn_up', 'w_ffn_down', 'pre_mix_g', 'post_mix_g', 'pre_ffn_g', 'post_ffn_g']
TWIN_DIFF_INPUT = 'x'
TWIN_INPUTS = ['x', 'w_in', 'b_gate', 'q_norm_g', 'k_norm_g', 'q_a_norm_g', 'kv_a_norm_g', 'w_q_up', 'w_kv_up', 'w_branch_a', 'w_branch_b', 'w_o', 'w_ffn_up', 'w_ffn_down', 'pre_mix_g', 'post_mix_g', 'pre_ffn_g', 'post_ffn_g', 'loss_target', 'm_w_in', 'm_b_gate', 'm_q_norm_g', 'm_k_norm_g', 'm_q_a_norm_g', 'm_kv_a_norm_g', 'm_w_q_up', 'm_w_kv_up', 'm_w_branch_a', 'm_w_branch_b', 'm_w_o', 'm_w_ffn_up', 'm_w_ffn_down', 'm_pre_mix_g', 'm_post_mix_g', 'm_pre_ffn_g', 'm_post_ffn_g', 'v_w_in', 'v_b_gate', 'v_q_norm_g', 'v_k_norm_g', 'v_q_a_norm_g', 'v_kv_a_norm_g', 'v_w_q_up', 'v_w_kv_up', 'v_w_branch_a', 'v_w_branch_b', 'v_w_o', 'v_w_ffn_up', 'v_w_ffn_down', 'v_pre_mix_g', 'v_post_mix_g', 'v_pre_ffn_g', 'v_post_ffn_g']
TWIN_OUTPUTS = ['loss', 'grad_x', 'grad_w_in', 'grad_b_gate', 'grad_q_norm_g', 'grad_k_norm_g', 'grad_q_a_norm_g', 'grad_kv_a_norm_g', 'grad_w_q_up', 'grad_w_kv_up', 'grad_w_branch_a', 'grad_w_branch_b', 'grad_w_o', 'grad_w_ffn_up', 'grad_w_ffn_down', 'grad_pre_mix_g', 'grad_post_mix_g', 'grad_pre_ffn_g', 'grad_post_ffn_g', 'delta_w_in', 'delta_b_gate', 'delta_q_norm_g', 'delta_k_norm_g', 'delta_q_a_norm_g', 'delta_kv_a_norm_g', 'delta_w_q_up', 'delta_w_kv_up', 'delta_w_branch_a', 'delta_w_branch_b', 'delta_w_o', 'delta_w_ffn_up', 'delta_w_ffn_down', 'delta_pre_mix_g', 'delta_post_mix_g', 'delta_pre_ffn_g', 'delta_post_ffn_g', 'new_m_w_in', 'new_m_b_gate', 'new_m_q_norm_g', 'new_m_k_norm_g', 'new_m_q_a_norm_g', 'new_m_kv_a_norm_g', 'new_m_w_q_up', 'new_m_w_kv_up', 'new_m_w_branch_a', 'new_m_w_branch_b', 'new_m_w_o', 'new_m_w_ffn_up', 'new_m_w_ffn_down', 'new_m_pre_mix_g', 'new_m_post_mix_g', 'new_m_pre_ffn_g', 'new_m_post_ffn_g', 'new_v_w_in', 'new_v_b_gate', 'new_v_q_norm_g', 'new_v_k_norm_g', 'new_v_q_a_norm_g', 'new_v_kv_a_norm_g', 'new_v_w_q_up', 'new_v_w_kv_up', 'new_v_w_branch_a', 'new_v_w_branch_b', 'new_v_w_o', 'new_v_w_ffn_up', 'new_v_w_ffn_down', 'new_v_pre_mix_g', 'new_v_post_mix_g', 'new_v_pre_ffn_g', 'new_v_post_ffn_g']
TWIN_LEAF_KINDS = {'loss': 'loss', 'grad_x': 'grad_x', 'grad_w_in': 'grad_w', 'grad_b_gate': 'grad_w', 'grad_q_norm_g': 'grad_w', 'grad_k_norm_g': 'grad_w', 'grad_q_a_norm_g': 'grad_w', 'grad_kv_a_norm_g': 'grad_w', 'grad_w_q_up': 'grad_w', 'grad_w_kv_up': 'grad_w', 'grad_w_branch_a': 'grad_w', 'grad_w_branch_b': 'grad_w', 'grad_w_o': 'grad_w', 'grad_w_ffn_up': 'grad_w', 'grad_w_ffn_down': 'grad_w', 'grad_pre_mix_g': 'grad_w', 'grad_post_mix_g': 'grad_w', 'grad_pre_ffn_g': 'grad_w', 'grad_post_ffn_g': 'grad_w', 'delta_w_in': 'delta_w', 'delta_b_gate': 'delta_w', 'delta_q_norm_g': 'delta_w', 'delta_k_norm_g': 'delta_w', 'delta_q_a_norm_g': 'delta_w', 'delta_kv_a_norm_g': 'delta_w', 'delta_w_q_up': 'delta_w', 'delta_w_kv_up': 'delta_w', 'delta_w_branch_a': 'delta_w', 'delta_w_branch_b': 'delta_w', 'delta_w_o': 'delta_w', 'delta_w_ffn_up': 'delta_w', 'delta_w_ffn_down': 'delta_w', 'delta_pre_mix_g': 'delta_w', 'delta_post_mix_g': 'delta_w', 'delta_pre_ffn_g': 'delta_w', 'delta_post_ffn_g': 'delta_w', 'new_m_w_in': 'new_m', 'new_m_b_gate': 'new_m', 'new_m_q_norm_g': 'new_m', 'new_m_k_norm_g': 'new_m', 'new_m_q_a_norm_g': 'new_m', 'new_m_kv_a_norm_g': 'new_m', 'new_m_w_q_up': 'new_m', 'new_m_w_kv_up': 'new_m', 'new_m_w_branch_a': 'new_m', 'new_m_w_branch_b': 'new_m', 'new_m_w_o': 'new_m', 'new_m_w_ffn_up': 'new_m', 'new_m_w_ffn_down': 'new_m', 'new_m_pre_mix_g': 'new_m', 'new_m_post_mix_g': 'new_m', 'new_m_pre_ffn_g': 'new_m', 'new_m_post_ffn_g': 'new_m', 'new_v_w_in': 'new_v', 'new_v_b_gate': 'new_v', 'new_v_q_norm_g': 'new_v', 'new_v_k_norm_g': 'new_v', 'new_v_q_a_norm_g': 'new_v', 'new_v_kv_a_norm_g': 'new_v', 'new_v_w_q_up': 'new_v', 'new_v_w_kv_up': 'new_v', 'new_v_w_branch_a': 'new_v', 'new_v_w_branch_b': 'new_v', 'new_v_w_o': 'new_v', 'new_v_w_ffn_up': 'new_v', 'new_v_w_ffn_down': 'new_v', 'new_v_pre_mix_g': 'new_v', 'new_v_post_mix_g': 'new_v', 'new_v_pre_ffn_g': 'new_v', 'new_v_post_ffn_g': 'new_v'}


def _forward(args):
    return _fwd_reference(*[args[k] for k in FWD_PARAMS])


def _output_shape():
    def fwd():
        inp = _fwd_setup_inputs(0)
        return _fwd_reference(*[inp[k] for k in FWD_PARAMS])
    out = _jax.eval_shape(fwd)
    return out.shape, out.dtype

N_MICROBATCH = 1
ADAM_LR = 0.001
ADAM_B1 = 0.9
ADAM_B2 = 0.999
ADAM_EPS = 1e-08
ADAM_WD = 0.01
ADAM_STEP = 10
PER_EXAMPLE_BATCH_AXIS = {'x': 0, 'loss_target': 0}
SHARED_INPUTS = []
_WEIGHT_DTYPES = {'w_in': _jnp.float32, 'b_gate': _jnp.float32, 'q_norm_g': _jnp.float32, 'k_norm_g': _jnp.float32, 'q_a_norm_g': _jnp.float32, 'kv_a_norm_g': _jnp.float32, 'w_q_up': _jnp.float32, 'w_kv_up': _jnp.float32, 'w_branch_a': _jnp.float32, 'w_branch_b': _jnp.float32, 'w_o': _jnp.float32, 'w_ffn_up': _jnp.float32, 'w_ffn_down': _jnp.float32, 'pre_mix_g': _jnp.float32, 'post_mix_g': _jnp.float32, 'pre_ffn_g': _jnp.float32, 'post_ffn_g': _jnp.float32}
MOMENT_SCALE = {'w_in': 4.214110e+01, 'b_gate': 2.199936e+01, 'q_norm_g': 9.605569e+00, 'k_norm_g': 1.057521e+01, 'q_a_norm_g': 8.424270e+00, 'kv_a_norm_g': 1.079216e+02, 'w_q_up': 4.173266e+00, 'w_kv_up': 5.258044e+01, 'w_branch_a': 5.484780e+01, 'w_branch_b': 5.348318e+01, 'w_o': 7.683796e+01, 'w_ffn_up': 2.019185e+01, 'w_ffn_down': 7.247130e+01, 'pre_mix_g': 7.949552e+01, 'post_mix_g': 1.044690e+02, 'pre_ffn_g': 3.977833e+01, 'post_ffn_g': 9.799995e+01}


def _to_microbatches(a, axis):
    t = _jnp.moveaxis(a, axis, 0)
    t = t.reshape((N_MICROBATCH, t.shape[0] // N_MICROBATCH) + t.shape[1:])
    return _jnp.moveaxis(t, 1, axis + 1)


def setup_inputs(seed: int = 0) -> dict:
    inp = _fwd_setup_inputs(seed)
    key = _jax.random.fold_in(_jax.random.key(seed), 7919)
    shape, _ = _output_shape()
    out = dict(inp)
    out["loss_target"] = _jax.random.normal(_jax.random.fold_in(key, 0), shape, _jnp.float32)
    for i, name in enumerate(TWIN_WEIGHTS):
        w = inp[name].astype(_jnp.float32)
        if MOMENT_SCALE is None:
            s = _jnp.sqrt(_jnp.mean(_jnp.square(w)) + 1e-30)
        else:
            s = MOMENT_SCALE[name]
        km, kv = _jax.random.split(_jax.random.fold_in(key, i + 1))
        out[name] = w
        out["m_" + name] = s * _jax.random.normal(km, w.shape, _jnp.float32)
        out["v_" + name] = (s * s) * _jax.random.uniform(kv, w.shape, _jnp.float32, 0.5, 1.5)
    if N_MICROBATCH > 1:
        for name, axis in PER_EXAMPLE_BATCH_AXIS.items():
            out[name] = _to_microbatches(out[name], axis)
    return {'x': out['x'], 'w_in': out['w_in'], 'b_gate': out['b_gate'], 'q_norm_g': out['q_norm_g'], 'k_norm_g': out['k_norm_g'], 'q_a_norm_g': out['q_a_norm_g'], 'kv_a_norm_g': out['kv_a_norm_g'], 'w_q_up': out['w_q_up'], 'w_kv_up': out['w_kv_up'], 'w_branch_a': out['w_branch_a'], 'w_branch_b': out['w_branch_b'], 'w_o': out['w_o'], 'w_ffn_up': out['w_ffn_up'], 'w_ffn_down': out['w_ffn_down'], 'pre_mix_g': out['pre_mix_g'], 'post_mix_g': out['post_mix_g'], 'pre_ffn_g': out['pre_ffn_g'], 'post_ffn_g': out['post_ffn_g'], 'loss_target': out['loss_target'], 'm_w_in': out['m_w_in'], 'm_b_gate': out['m_b_gate'], 'm_q_norm_g': out['m_q_norm_g'], 'm_k_norm_g': out['m_k_norm_g'], 'm_q_a_norm_g': out['m_q_a_norm_g'], 'm_kv_a_norm_g': out['m_kv_a_norm_g'], 'm_w_q_up': out['m_w_q_up'], 'm_w_kv_up': out['m_w_kv_up'], 'm_w_branch_a': out['m_w_branch_a'], 'm_w_branch_b': out['m_w_branch_b'], 'm_w_o': out['m_w_o'], 'm_w_ffn_up': out['m_w_ffn_up'], 'm_w_ffn_down': out['m_w_ffn_down'], 'm_pre_mix_g': out['m_pre_mix_g'], 'm_post_mix_g': out['m_post_mix_g'], 'm_pre_ffn_g': out['m_pre_ffn_g'], 'm_post_ffn_g': out['m_post_ffn_g'], 'v_w_in': out['v_w_in'], 'v_b_gate': out['v_b_gate'], 'v_q_norm_g': out['v_q_norm_g'], 'v_k_norm_g': out['v_k_norm_g'], 'v_q_a_norm_g': out['v_q_a_norm_g'], 'v_kv_a_norm_g': out['v_kv_a_norm_g'], 'v_w_q_up': out['v_w_q_up'], 'v_w_kv_up': out['v_w_kv_up'], 'v_w_branch_a': out['v_w_branch_a'], 'v_w_branch_b': out['v_w_branch_b'], 'v_w_o': out['v_w_o'], 'v_w_ffn_up': out['v_w_ffn_up'], 'v_w_ffn_down': out['v_w_ffn_down'], 'v_pre_mix_g': out['v_pre_mix_g'], 'v_post_mix_g': out['v_post_mix_g'], 'v_pre_ffn_g': out['v_pre_ffn_g'], 'v_post_ffn_g': out['v_post_ffn_g']}


def _loss(weights, diff, rest, loss_target):
    with _jax.named_scope("forward"):
        args = {**rest, TWIN_DIFF_INPUT: diff, **{k: w.astype(_WEIGHT_DTYPES[k]) for k, w in weights.items()}}
        y = _forward(args)
    with _jax.named_scope("loss_head"):
        err = _jnp.square(y.astype(_jnp.float32) - loss_target)
        return 0.5 * _jnp.sum(_jnp.mean(err, axis=-1)) if err.ndim else 0.5 * err


def _adamw(w, g, m, v):
    m = ADAM_B1 * m + (1.0 - ADAM_B1) * g
    v = ADAM_B2 * v + (1.0 - ADAM_B2) * _jnp.square(g)
    m_hat = m / (1.0 - ADAM_B1 ** ADAM_STEP)
    v_hat = v / (1.0 - ADAM_B2 ** ADAM_STEP)
    delta = -ADAM_LR * (m_hat / (_jnp.sqrt(v_hat) + ADAM_EPS) + ADAM_WD * w)
    return delta, m, v


def reference(x, w_in, b_gate, q_norm_g, k_norm_g, q_a_norm_g, kv_a_norm_g, w_q_up, w_kv_up, w_branch_a, w_branch_b, w_o, w_ffn_up, w_ffn_down, pre_mix_g, post_mix_g, pre_ffn_g, post_ffn_g, loss_target, m_w_in, m_b_gate, m_q_norm_g, m_k_norm_g, m_q_a_norm_g, m_kv_a_norm_g, m_w_q_up, m_w_kv_up, m_w_branch_a, m_w_branch_b, m_w_o, m_w_ffn_up, m_w_ffn_down, m_pre_mix_g, m_post_mix_g, m_pre_ffn_g, m_post_ffn_g, v_w_in, v_b_gate, v_q_norm_g, v_k_norm_g, v_q_a_norm_g, v_kv_a_norm_g, v_w_q_up, v_w_kv_up, v_w_branch_a, v_w_branch_b, v_w_o, v_w_ffn_up, v_w_ffn_down, v_pre_mix_g, v_post_mix_g, v_pre_ffn_g, v_post_ffn_g):
    given = dict(x=x, w_in=w_in, b_gate=b_gate, q_norm_g=q_norm_g, k_norm_g=k_norm_g, q_a_norm_g=q_a_norm_g, kv_a_norm_g=kv_a_norm_g, w_q_up=w_q_up, w_kv_up=w_kv_up, w_branch_a=w_branch_a, w_branch_b=w_branch_b, w_o=w_o, w_ffn_up=w_ffn_up, w_ffn_down=w_ffn_down, pre_mix_g=pre_mix_g, post_mix_g=post_mix_g, pre_ffn_g=pre_ffn_g, post_ffn_g=post_ffn_g, loss_target=loss_target, m_w_in=m_w_in, m_b_gate=m_b_gate, m_q_norm_g=m_q_norm_g, m_k_norm_g=m_k_norm_g, m_q_a_norm_g=m_q_a_norm_g, m_kv_a_norm_g=m_kv_a_norm_g, m_w_q_up=m_w_q_up, m_w_kv_up=m_w_kv_up, m_w_branch_a=m_w_branch_a, m_w_branch_b=m_w_branch_b, m_w_o=m_w_o, m_w_ffn_up=m_w_ffn_up, m_w_ffn_down=m_w_ffn_down, m_pre_mix_g=m_pre_mix_g, m_post_mix_g=m_post_mix_g, m_pre_ffn_g=m_pre_ffn_g, m_post_ffn_g=m_post_ffn_g, v_w_in=v_w_in, v_b_gate=v_b_gate, v_q_norm_g=v_q_norm_g, v_k_norm_g=v_k_norm_g, v_q_a_norm_g=v_q_a_norm_g, v_kv_a_norm_g=v_kv_a_norm_g, v_w_q_up=v_w_q_up, v_w_kv_up=v_w_kv_up, v_w_branch_a=v_w_branch_a, v_w_branch_b=v_w_branch_b, v_w_o=v_w_o, v_w_ffn_up=v_w_ffn_up, v_w_ffn_down=v_w_ffn_down, v_pre_mix_g=v_pre_mix_g, v_post_mix_g=v_post_mix_g, v_pre_ffn_g=v_pre_ffn_g, v_post_ffn_g=v_post_ffn_g)
    weights = {n: given[n] for n in TWIN_WEIGHTS}
    shared = {n: given[n] for n in SHARED_INPUTS}
    per_example = {n: given[n] for n in ['x']}
    grad_fn = _jax.value_and_grad(_loss, argnums=(0, 1))

    def one_microbatch(ex, loss_target):
        ex = dict(ex)
        diff = ex.pop(TWIN_DIFF_INPUT)
        return grad_fn(weights, diff, {**shared, **ex}, loss_target)

    if N_MICROBATCH == 1:
        loss, (grad_w, grad_x) = one_microbatch(per_example, given["loss_target"])
    else:
        def body(carry, xs):
            loss_sum, grad_sum = carry
            l_k, (gw_k, gx_k) = one_microbatch(xs[0], xs[1])
            with _jax.named_scope("update"):
                return (loss_sum + l_k, _jax.tree.map(_jnp.add, grad_sum, gw_k)), gx_k

        init = (_jnp.zeros((), _jnp.float32), _jax.tree.map(_jnp.zeros_like, weights))
        (loss, grad_w), grad_x = _jax.lax.scan(body, init, (per_example, given["loss_target"]))
    with _jax.named_scope("update"):
        delta_w, new_m, new_v = {}, {}, {}
        for n in TWIN_WEIGHTS:
            delta_w[n], new_m[n], new_v[n] = _adamw(weights[n], grad_w[n], given["m_" + n], given["v_" + n])
    return (loss, grad_x, *[grad_w[n] for n in TWIN_WEIGHTS], *[delta_w[n] for n in TWIN_WEIGHTS],
            *[new_m[n] for n in TWIN_WEIGHTS], *[new_v[n] for n in TWIN_WEIGHTS])
```

```python
import functools
import math

import jax
import jax.numpy as jnp
import numpy as np
from jax import lax
from jax.experimental import pallas as pl
from jax.experimental.pallas import tpu as pltpu

F32 = jnp.float32
BF16 = jnp.bfloat16

D_MODEL = 1024
GRID_W = 64
ROPE_THETA = 10000.0
EPS = 1e-6
GQA_HEADS = 8
GQA_KV_HEADS = 2
GQA_GROUP = GQA_HEADS // GQA_KV_HEADS
HEAD_DIM = 64
MLA_HEADS = 8
MLA_ROPE_DIM = 32
MLA_QK_DIM = 96
MLA_Q_RANK = 384
MLA_KV_RANK = 256
D_FF = 4 * D_MODEL
GQA_SCALE = 1.0 / math.sqrt(HEAD_DIM)
MLA_SCALE = 1.0 / math.sqrt(MLA_QK_DIM)

ADAM_LR = 0.001
ADAM_B1 = 0.9
ADAM_B2 = 0.999
ADAM_EPS = 1e-08
ADAM_WD = 0.01
ADAM_STEP = 10

N_DEV = 8
LANES = 128
SUBLANES = 8
VMEM_LIMIT = 48 * 1024 * 1024

Z_QA, Z_KA, Z_VA, Z_CQ, Z_CKV, Z_KR, Z_GATE = 0, 512, 640, 768, 1152, 1408, 1536
Z_ATT_W = 1536
Z_W = 3584
KR_LANE0 = 64

WEIGHT_NAMES = ("w_in", "b_gate", "q_norm_g", "k_norm_g", "q_a_norm_g", "kv_a_norm_g", "w_q_up", "w_kv_up",
                "w_branch_a", "w_branch_b", "w_o", "w_ffn_up", "w_ffn_down", "pre_mix_g", "post_mix_g",
                "pre_ffn_g", "post_ffn_g")
SHARD_AXIS = {"w_in": 2, "w_q_up": 2, "w_kv_up": 2, "w_branch_a": 2, "w_branch_b": 2, "w_o": 1, "w_ffn_up": 2,
              "w_ffn_down": 1}
BIG_NAMES = tuple(n for n in WEIGHT_NAMES if n in SHARD_AXIS)
SMALL_NAMES = tuple(n for n in WEIGHT_NAMES if n not in SHARD_AXIS)
PACK_ROW_TILE = 1024


def _params(*semantics):
    return pltpu.CompilerParams(dimension_semantics=semantics, vmem_limit_bytes=VMEM_LIMIT)


def _tile(n, pref):
    if n <= pref:
        return n
    t = (pref // LANES) * LANES
    while n % t:
        t -= LANES
    return t


def _fold8(t):
    return t.reshape(t.shape[0] // SUBLANES, SUBLANES, t.shape[1]).sum(axis=0)


_DIMS = {"nn": ((1,), (0,)), "nt": ((1,), (1,)), "tn": ((0,), (0,))}


def _matmul(a, b, mode, name):
    if mode == "nn":
        (m, k), n = a.shape, b.shape[1]
    elif mode == "nt":
        (m, k), n = a.shape, b.shape[0]
    else:
        (k, m), n = a.shape, b.shape[1]
    tm, tn = _tile(m, 512), _tile(n, 512)
    tk = _tile(k, 512 if mode == "tn" else 1024)
    nk = k // tk
    dims = (_DIMS[mode], ((), ()))

    def body(a_ref, b_ref, o_ref, acc_ref):
        prod = lax.dot_general(a_ref[...], b_ref[...], dims, preferred_element_type=F32)
        if nk == 1:
            o_ref[...] = prod
        else:
            kk = pl.program_id(2)

            @pl.when(kk == 0)
            def _():
                acc_ref[...] = prod

            @pl.when(kk > 0)
            def _():
                acc_ref[...] += prod

            @pl.when(kk == nk - 1)
            def _():
                o_ref[...] = acc_ref[...]

    if mode == "tn":
        a_spec = pl.BlockSpec((tk, tm), lambda i, j, kk: (kk, i))
    else:
        a_spec = pl.BlockSpec((tm, tk), lambda i, j, kk: (i, kk))
    if mode == "nt":
        b_spec = pl.BlockSpec((tn, tk), lambda i, j, kk: (j, kk))
    else:
        b_spec = pl.BlockSpec((tk, tn), lambda i, j, kk: (kk, j))
    return pl.pallas_call(
        body,
        name=name,
        grid=(m // tm, n // tn, nk),
        in_specs=[a_spec, b_spec],
        out_specs=pl.BlockSpec((tm, tn), lambda i, j, kk: (i, j)),
        out_shape=jax.ShapeDtypeStruct((m, n), F32),
        scratch_shapes=[pltpu.VMEM((tm, tn), F32)],
        compiler_params=_params("parallel", "parallel", "arbitrary"),
    )(a, b)


def _rinv(x):
    return lax.rsqrt(jnp.mean(x * x, axis=-1, keepdims=True) + EPS)


def _rms_bwd_rows(x, g, dy):
    r = _rinv(x)
    xh = x * r
    dxh = dy * g
    dx = r * (dxh - xh * jnp.mean(dxh * xh, axis=-1, keepdims=True))
    return dx, dy * xh


def _row_spec(tm, c):
    return pl.BlockSpec((tm, c), lambda i: (i, 0))


def _vec_spec(c):
    return pl.BlockSpec((1, c), lambda i: (0, 0))


def _acc_spec(c):
    return pl.BlockSpec((SUBLANES, c), lambda i: (0, 0))


def _rms_fwd(x, g):
    t, d = x.shape
    tm = _tile(t, 512)

    def body(x_ref, g_ref, o_ref):
        xv = x_ref[...]
        o_ref[...] = (xv * _rinv(xv) * g_ref[...]).astype(BF16)

    return pl.pallas_call(
        body, name="rms_fwd", grid=(t // tm,),
        in_specs=[_row_spec(tm, d), _vec_spec(d)], out_specs=_row_spec(tm, d),
        out_shape=jax.ShapeDtypeStruct((t, d), BF16), compiler_params=_params("parallel"),
    )(x, g.reshape(1, d))


def _rms_bwd(x, g, dres, dy):
    t, d = x.shape
    tm = _tile(t, 512)

    def body(x_ref, g_ref, dres_ref, dy_ref, dx_ref, dg_ref):
        dx, dgc = _rms_bwd_rows(x_ref[...], g_ref[...], dy_ref[...])
        dx_ref[...] = dres_ref[...] + dx

        @pl.when(pl.program_id(0) == 0)
        def _():
            dg_ref[...] = jnp.zeros_like(dg_ref)

        dg_ref[...] += _fold8(dgc)

    return pl.pallas_call(
        body, name="rms_bwd", grid=(t // tm,),
        in_specs=[_row_spec(tm, d), _vec_spec(d), _row_spec(tm, d), _row_spec(tm, d)],
        out_specs=[_row_spec(tm, d), _acc_spec(d)],
        out_shape=[jax.ShapeDtypeStruct((t, d), F32), jax.ShapeDtypeStruct((SUBLANES, d), F32)],
        compiler_params=_params("arbitrary"),
    )(x, g.reshape(1, d), dres, dy)


def _res_norm_fwd(x, m, g_post, g_next):
    t, d = x.shape
    tm = _tile(t, 512)

    def body(x_ref, m_ref, gp_ref, gn_ref, x2_ref, u2_ref):
        mv = m_ref[...]
        x2 = x_ref[...] + mv * _rinv(mv) * gp_ref[...]
        x2_ref[...] = x2
        u2_ref[...] = (x2 * _rinv(x2) * gn_ref[...]).astype(BF16)

    return pl.pallas_call(
        body, name="res_norm_fwd", grid=(t // tm,),
        in_specs=[_row_spec(tm, d), _row_spec(tm, d), _vec_spec(d), _vec_spec(d)],
        out_specs=[_row_spec(tm, d), _row_spec(tm, d)],
        out_shape=[jax.ShapeDtypeStruct((t, d), F32), jax.ShapeDtypeStruct((t, d), BF16)],
        compiler_params=_params("parallel"),
    )(x, m, g_post.reshape(1, d), g_next.reshape(1, d))


def _res_norm_bwd(x2, m, g_post, g_next, dx2_in, du2):
    t, d = x2.shape
    tm = _tile(t, 512)

    def body(x2_ref, m_ref, gp_ref, gn_ref, dx2in_ref, du2_ref, dx2_ref, dm_ref, dgp_ref, dgn_ref):
        dxn, dgn_c = _rms_bwd_rows(x2_ref[...], gn_ref[...], du2_ref[...])
        dx2 = dx2in_ref[...] + dxn
        dx2_ref[...] = dx2
        dm, dgp_c = _rms_bwd_rows(m_ref[...], gp_ref[...], dx2)
        dm_ref[...] = dm.astype(BF16)

        @pl.when(pl.program_id(0) == 0)
        def _():
            dgp_ref[...] = jnp.zeros_like(dgp_ref)
            dgn_ref[...] = jnp.zeros_like(dgn_ref)

        dgp_ref[...] += _fold8(dgp_c)
        dgn_ref[...] += _fold8(dgn_c)

    return pl.pallas_call(
        body, name="res_norm_bwd", grid=(t // tm,),
        in_specs=[_row_spec(tm, d), _row_spec(tm, d), _vec_spec(d), _vec_spec(d), _row_spec(tm, d), _row_spec(tm, d)],
        out_specs=[_row_spec(tm, d), _row_spec(tm, d), _acc_spec(d), _acc_spec(d)],
        out_shape=[jax.ShapeDtypeStruct((t, d), F32), jax.ShapeDtypeStruct((t, d), BF16),
                   jax.ShapeDtypeStruct((SUBLANES, d), F32), jax.ShapeDtypeStruct((SUBLANES, d), F32)],
        compiler_params=_params("arbitrary"),
    )(x2, m, g_post.reshape(1, d), g_next.reshape(1, d), dx2_in, du2)


def _rope_tables(t):
    rows = t // GRID_W
    row = jnp.repeat(jnp.arange(rows, dtype=F32), GRID_W)
    col = jnp.tile(jnp.arange(GRID_W, dtype=F32), rows)

    def tab(rot_dim):
        half = rot_dim // 2
        inv = ROPE_THETA ** (-jnp.arange(0, half, 2, dtype=F32) / half)
        ar = row[:, None] * inv[None, :]
        ac = col[:, None] * inv[None, :]
        ang = jnp.concatenate([ar, ar, ac, ac], axis=-1)
        q = half // 2
        sign = np.tile(np.concatenate([-np.ones(q, np.float32), np.ones(q, np.float32)]), 2)
        return jnp.cos(ang), jnp.sin(ang) * sign[None, :]

    ca, sa = tab(HEAD_DIM)
    cb, sb = tab(MLA_ROPE_DIM)
    one = jnp.ones((t, 1), F32)
    cos_b = jnp.concatenate([one * jnp.ones((1, KR_LANE0), F32), cb, one * jnp.ones((1, 32), F32)], axis=-1)
    sin_b = jnp.concatenate([jnp.zeros((t, KR_LANE0), F32), sb, jnp.zeros((t, 32), F32)], axis=-1)
    return jnp.tile(ca, (1, GQA_HEADS)), jnp.tile(sa, (1, GQA_HEADS)), cos_b, sin_b


def _swap_halves(x, sh):
    lane = lax.broadcasted_iota(jnp.int32, x.shape, 1)
    up = pltpu.roll(x, LANES - sh, 1)
    dn = pltpu.roll(x, sh, 1)
    return jnp.where((lane & (2 * sh - 1)) < sh, up, dn)


def _rope(x, cos, sin_s, sh):
    return x * cos + _swap_halves(x, sh) * sin_s


def _rope_bwd(dy, cos, sin_s, sh):
    return dy * cos + _swap_halves(dy * sin_s, sh)


def _lo_mask(shape):
    return lax.broadcasted_iota(jnp.int32, shape, 1) < HEAD_DIM


def _half_mean(t, lo):
    s_lo = jnp.sum(jnp.where(lo, t, 0.0), axis=-1, keepdims=True)
    s_hi = jnp.sum(jnp.where(lo, 0.0, t), axis=-1, keepdims=True)
    return jnp.where(lo, s_lo, s_hi) * (1.0 / HEAD_DIM)


def _head_norm(x, g2):
    lo = _lo_mask(x.shape)
    r = lax.rsqrt(_half_mean(x * x, lo) + EPS)
    return x * r * g2


def _head_norm_bwd(x, g2, dy):
    lo = _lo_mask(x.shape)
    r = lax.rsqrt(_half_mean(x * x, lo) + EPS)
    xh = x * r
    dxh = dy * g2
    dx = r * (dxh - xh * _half_mean(dxh * xh, lo))
    return dx, dy * xh


def _prep_a_fwd(z, gq2, gk2, gqa, gkva, cos_a, sin_a, cos_b, sin_b):
    t = z.shape[0]
    tm = _tile(t, 256)

    def body(z_ref, gq_ref, gk_ref, gqa_ref, gkva_ref, ca_ref, sa_ref, cb_ref, sb_ref,
             qa_ref, ka_ref, va_ref, cqn_ref, ckvn_ref, krr_ref):
        for j in range(4):
            cols = slice(LANES * j, LANES * (j + 1))
            y = _rope(_head_norm(z_ref[:, cols], gq_ref[...]), ca_ref[:, cols], sa_ref[:, cols], 16)
            qa_ref[:, cols] = (y * GQA_SCALE).astype(BF16)
        y = _rope(_head_norm(z_ref[:, Z_KA:Z_VA], gk_ref[...]), ca_ref[:, :LANES], sa_ref[:, :LANES], 16)
        ka_ref[...] = y.astype(BF16)
        va_ref[...] = z_ref[:, Z_VA:Z_CQ].astype(BF16)
        cq = z_ref[:, Z_CQ:Z_CKV]
        cqn_ref[...] = (cq * _rinv(cq) * gqa_ref[...]).astype(BF16)
        ckv = z_ref[:, Z_CKV:Z_KR]
        ckvn_ref[...] = (ckv * _rinv(ckv) * gkva_ref[...]).astype(BF16)
        krr_ref[...] = _rope(z_ref[:, Z_KR:Z_GATE], cb_ref[...], sb_ref[...], 8)

    return pl.pallas_call(
        body, name="prep_a_fwd", grid=(t // tm,),
        in_specs=[_row_spec(tm, Z_ATT_W), _vec_spec(LANES), _vec_spec(LANES), _vec_spec(MLA_Q_RANK),
                  _vec_spec(MLA_KV_RANK), _row_spec(tm, 512), _row_spec(tm, 512), _row_spec(tm, LANES),
                  _row_spec(tm, LANES)],
        out_specs=[_row_spec(tm, 512), _row_spec(tm, LANES), _row_spec(tm, LANES), _row_spec(tm, MLA_Q_RANK),
                   _row_spec(tm, MLA_KV_RANK), _row_spec(tm, LANES)],
        out_shape=[jax.ShapeDtypeStruct((t, 512), BF16), jax.ShapeDtypeStruct((t, LANES), BF16),
                   jax.ShapeDtypeStruct((t, LANES), BF16), jax.ShapeDtypeStruct((t, MLA_Q_RANK), BF16),
                   jax.ShapeDtypeStruct((t, MLA_KV_RANK), BF16), jax.ShapeDtypeStruct((t, LANES), F32)],
        compiler_params=_params("parallel"),
    )(z, gq2, gk2, gqa, gkva, cos_a, sin_a, cos_b, sin_b)


def _prep_a_bwd(z, dqa, dka4, dva4, dcqn, dckvn, dkr, dzga, dzgb, gq2, gk2, gqa, gkva, cos_a, sin_a):
    t = z.shape[0]
    tm = _tile(t, 256)

    def body(z_ref, dqa_ref, dka_ref, dva_ref, dcqn_ref, dckvn_ref, dkr_ref, dzga_ref, dzgb_ref, gq_ref, gk_ref,
             gqa_ref, gkva_ref, ca_ref, sa_ref, dz_ref, dgq_ref, dgk_ref, dgqa_ref, dgkva_ref):
        @pl.when(pl.program_id(0) == 0)
        def _():
            dgq_ref[...] = jnp.zeros_like(dgq_ref)
            dgk_ref[...] = jnp.zeros_like(dgk_ref)
            dgqa_ref[...] = jnp.zeros_like(dgqa_ref)
            dgkva_ref[...] = jnp.zeros_like(dgkva_ref)

        dgq = jnp.zeros((SUBLANES, LANES), F32)
        for j in range(4):
            cols = slice(LANES * j, LANES * (j + 1))
            dy = _rope_bwd(dqa_ref[:, cols] * GQA_SCALE, ca_ref[:, cols], sa_ref[:, cols], 16)
            dx, dgc = _head_norm_bwd(z_ref[:, cols], gq_ref[...], dy)
            dz_ref[:, cols] = dx.astype(BF16)
            dgq = dgq + _fold8(dgc)
        dgq_ref[...] += dgq
        dk = dka_ref[0] + dka_ref[1] + dka_ref[2] + dka_ref[3]
        dy = _rope_bwd(dk, ca_ref[:, :LANES], sa_ref[:, :LANES], 16)
        dx, dgc = _head_norm_bwd(z_ref[:, Z_KA:Z_VA], gk_ref[...], dy)
        dz_ref[:, Z_KA:Z_VA] = dx.astype(BF16)
        dgk_ref[...] += _fold8(dgc)
        dz_ref[:, Z_VA:Z_CQ] = (dva_ref[0] + dva_ref[1] + dva_ref[2] + dva_ref[3]).astype(BF16)
        dx, dgc = _rms_bwd_rows(z_ref[:, Z_CQ:Z_CKV], gqa_ref[...], dcqn_ref[...])
        dz_ref[:, Z_CQ:Z_CKV] = dx.astype(BF16)
        dgqa_ref[...] += _fold8(dgc)
        dx, dgc = _rms_bwd_rows(z_ref[:, Z_CKV:Z_KR], gkva_ref[...], dckvn_ref[...])
        dz_ref[:, Z_CKV:Z_KR] = dx.astype(BF16)
        dgkva_ref[...] += _fold8(dgc)
        dz_ref[:, Z_KR:Z_GATE] = dkr_ref[...].astype(BF16)
        dz_ref[:, Z_GATE:Z_GATE + D_MODEL] = dzga_ref[...]
        dz_ref[:, Z_GATE + D_MODEL:Z_W] = dzgb_ref[...]

    part = pl.BlockSpec((4, tm, LANES), lambda i: (0, i, 0))
    return pl.pallas_call(
        body, name="prep_a_bwd", grid=(t // tm,),
        in_specs=[_row_spec(tm, Z_ATT_W), _row_spec(tm, 512), part, part, _row_spec(tm, MLA_Q_RANK),
                  _row_spec(tm, MLA_KV_RANK), _row_spec(tm, LANES), _row_spec(tm, D_MODEL), _row_spec(tm, D_MODEL),
                  _vec_spec(LANES),
                  _vec_spec(LANES), _vec_spec(MLA_Q_RANK), _vec_spec(MLA_KV_RANK), _row_spec(tm, 512),
                  _row_spec(tm, 512)],
        out_specs=[_row_spec(tm, Z_W), _acc_spec(LANES), _acc_spec(LANES), _acc_spec(MLA_Q_RANK),
                   _acc_spec(MLA_KV_RANK)],
        out_shape=[jax.ShapeDtypeStruct((t, Z_W), BF16), jax.ShapeDtypeStruct((SUBLANES, LANES), F32),
                   jax.ShapeDtypeStruct((SUBLANES, LANES), F32), jax.ShapeDtypeStruct((SUBLANES, MLA_Q_RANK), F32),
                   jax.ShapeDtypeStruct((SUBLANES, MLA_KV_RANK), F32)],
        compiler_params=_params("arbitrary"),
    )(z, dqa, dka4, dva4, dcqn, dckvn, dkr, dzga, dzgb, gq2, gk2, gqa, gkva, cos_a, sin_a)


def _prep_b_fwd(qb, kvb, krr, cos_b, sin_b):
    t = qb.shape[0]
    tm = _tile(t, 256)

    def body(qb_ref, kvb_ref, krr_ref, cb_ref, sb_ref, q_ref, k_ref, v_ref):
        for h in range(MLA_HEADS):
            cols = slice(LANES * h, LANES * (h + 1))
            q_ref[:, cols] = (_rope(qb_ref[:, cols], cb_ref[...], sb_ref[...], 8) * MLA_SCALE).astype(BF16)
            k_ref[:, cols] = (kvb_ref[:, cols] + krr_ref[...]).astype(BF16)
        v_ref[...] = kvb_ref[:, 1024:1536].astype(BF16)

    return pl.pallas_call(
        body, name="prep_b_fwd", grid=(t // tm,),
        in_specs=[_row_spec(tm, 1024), _row_spec(tm, 1536), _row_spec(tm, LANES), _row_spec(tm, LANES),
                  _row_spec(tm, LANES)],
        out_specs=[_row_spec(tm, 1024), _row_spec(tm, 1024), _row_spec(tm, 512)],
        out_shape=[jax.ShapeDtypeStruct((t, 1024), BF16), jax.ShapeDtypeStruct((t, 1024), BF16),
                   jax.ShapeDtypeStruct((t, 512), BF16)],
        compiler_params=_params("parallel"),
    )(qb, kvb, krr, cos_b, sin_b)


def _prep_b_bwd(dq, dk, dv, cos_b, sin_b):
    t = dq.shape[0]
    tm = _tile(t, 256)

    def body(dq_ref, dk_ref, dv_ref, cb_ref, sb_ref, dqb_ref, dkvb_ref, dkr_ref):
        dkr = jnp.zeros((tm, LANES), F32)
        for h in range(MLA_HEADS):
            cols = slice(LANES * h, LANES * (h + 1))
            dqb_ref[:, cols] = _rope_bwd(dq_ref[:, cols] * MLA_SCALE, cb_ref[...], sb_ref[...], 8).astype(BF16)
            dkh = dk_ref[:, cols]
            dkvb_ref[:, cols] = dkh.astype(BF16)
            dkr = dkr + dkh
        dkvb_ref[:, 1024:1536] = dv_ref[...].astype(BF16)
        dkr_ref[...] = _rope_bwd(dkr, cb_ref[...], sb_ref[...], 8)

    return pl.pallas_call(
        body, name="prep_b_bwd", grid=(t // tm,),
        in_specs=[_row_spec(tm, 1024), _row_spec(tm, 1024), _row_spec(tm, 512), _row_spec(tm, LANES),
                  _row_spec(tm, LANES)],
        out_specs=[_row_spec(tm, 1024), _row_spec(tm, 1536), _row_spec(tm, LANES)],
        out_shape=[jax.ShapeDtypeStruct((t, 1024), BF16), jax.ShapeDtypeStruct((t, 1536), BF16),
                   jax.ShapeDtypeStruct((t, LANES), F32)],
        compiler_params=_params("parallel"),
    )(dq, dk, dv, cos_b, sin_b)


_NT = (((1,), (1,)), ((), ()))
_NN = (((1,), (0,)), ((), ()))


def _head_operands(qv, kv, i, shared_k):
    if shared_k:
        lo = _lo_mask(qv.shape)
        keep = lo if i == 0 else jnp.logical_not(lo)
        return jnp.where(keep, qv, jnp.zeros_like(qv)), kv
    cols = slice(LANES * i, LANES * (i + 1))
    return qv[:, cols], kv[:, cols]


def _attn_specs(shared_k, tq, tk, q_of, k_of):
    wq = LANES if shared_k else 2 * LANES
    q_spec = pl.BlockSpec((tq, wq), lambda *g: (q_of(*g), g[0]))
    if shared_k:
        k_spec = pl.BlockSpec((tk, LANES), lambda *g: (k_of(*g), 0))
        v_spec = pl.BlockSpec((tk, LANES), lambda *g: (k_of(*g), 0))
    else:
        k_spec = pl.BlockSpec((tk, wq), lambda *g: (k_of(*g), g[0]))
        v_spec = pl.BlockSpec((tk, LANES), lambda *g: (k_of(*g), g[0]))
    return wq, q_spec, k_spec, v_spec


def _attn_fwd(q, k, v, shared_k, name):
    t = q.shape[0]
    tq = tk = _tile(t, 512)
    nq, nk = t // tq, t // tk
    wq, q_spec, k_spec, v_spec = _attn_specs(shared_k, tq, tk, lambda p, i, j: i, lambda p, i, j: j)
    groups = q.shape[1] // wq

    def body(q_ref, k_ref, v_ref, o_ref, lse_ref, m_s, l_s, acc_s):
        kb = pl.program_id(2)

        @pl.when(kb == 0)
        def _():
            m_s[...] = jnp.full_like(m_s, -jnp.inf)
            l_s[...] = jnp.zeros_like(l_s)
            acc_s[...] = jnp.zeros_like(acc_s)

        qv, kv, vv = q_ref[...], k_ref[...], v_ref[...]
        for i in range(2):
            qi, ki = _head_operands(qv, kv, i, shared_k)
            s = lax.dot_general(qi, ki, _NT, preferred_element_type=F32)
            m_prev = m_s[i]
            m_new = jnp.maximum(m_prev, jnp.max(s, axis=-1, keepdims=True))
            alpha = jnp.exp(m_prev - m_new)
            p = jnp.exp(s - m_new)
            l_s[i] = alpha * l_s[i] + jnp.sum(p, axis=-1, keepdims=True)
            acc_s[i] = alpha * acc_s[i] + lax.dot_general(p.astype(BF16), vv, _NN, preferred_element_type=F32)
            m_s[i] = m_new

        @pl.when(kb == nk - 1)
        def _():
            o0 = acc_s[0] / l_s[0]
            o1 = acc_s[1] / l_s[1]
            o_ref[...] = jnp.where(_lo_mask(o0.shape), o0, o1).astype(BF16)
            lse_ref[0] = m_s[0] + jnp.log(l_s[0])
            lse_ref[1] = m_s[1] + jnp.log(l_s[1])

    return pl.pallas_call(
        body, name=name, grid=(groups, nq, nk),
        in_specs=[q_spec, k_spec, v_spec],
        out_specs=[pl.BlockSpec((tq, LANES), lambda p, i, j: (i, p)),
                   pl.BlockSpec((2, tq, 1), lambda p, i, j: (p, i, 0))],
        out_shape=[jax.ShapeDtypeStruct((t, LANES * groups), BF16),
                   jax.ShapeDtypeStruct((2 * groups, t, 1), F32)],
        scratch_shapes=[pltpu.VMEM((2, tq, 1), F32), pltpu.VMEM((2, tq, 1), F32), pltpu.VMEM((2, tq, LANES), F32)],
        compiler_params=_params("parallel", "parallel", "arbitrary"),
    )(q, k, v)


def _attn_delta(do, o):
    t, w = do.shape
    tm = _tile(t, 512)
    groups = w // LANES

    def body(do_ref, o_ref, delta_ref, dob_ref):
        dov = do_ref[...]
        dob_ref[...] = dov.astype(BF16)
        prod = dov * o_ref[...].astype(F32)
        for g in range(groups):
            x = prod[:, LANES * g:LANES * (g + 1)]
            lo = _lo_mask(x.shape)
            delta_ref[2 * g] = jnp.sum(jnp.where(lo, x, 0.0), axis=-1, keepdims=True)
            delta_ref[2 * g + 1] = jnp.sum(jnp.where(lo, 0.0, x), axis=-1, keepdims=True)

    return pl.pallas_call(
        body, name="attn_delta", grid=(t // tm,),
        in_specs=[_row_spec(tm, w), _row_spec(tm, w)],
        out_specs=[pl.BlockSpec((2 * groups, tm, 1), lambda i: (0, i, 0)), _row_spec(tm, w)],
        out_shape=[jax.ShapeDtypeStruct((2 * groups, t, 1), F32), jax.ShapeDtypeStruct((t, w), BF16)],
        compiler_params=_params("parallel"),
    )(do, o)


def _attn_bwd(q, k, v, do, lse, delta, shared_k, name):
    t = q.shape[0]
    tq = tk = _tile(t, 512)
    nq, nk = t // tq, t // tk
    wq, q_spec, k_spec, v_spec = _attn_specs(shared_k, tq, tk, lambda p, j, i: i, lambda p, j, i: j)
    groups = q.shape[1] // wq

    def body(q_ref, k_ref, v_ref, do_ref, lse_ref, delta_ref, dq_ref, dk_ref, dv_ref, dk_s, dv_s):
        kb, qb = pl.program_id(1), pl.program_id(2)

        @pl.when(qb == 0)
        def _():
            dk_s[...] = jnp.zeros_like(dk_s)
            dv_s[...] = jnp.zeros_like(dv_s)

        qv, kv, vv, dov = q_ref[...], k_ref[...], v_ref[...], do_ref[...]
        lo = _lo_mask(dov.shape)
        dq_parts = []
        for i in range(2):
            qi, ki = _head_operands(qv, kv, i, shared_k)
            keep = lo if i == 0 else jnp.logical_not(lo)
            doi = jnp.where(keep, dov, jnp.zeros_like(dov))
            s = lax.dot_general(qi, ki, _NT, preferred_element_type=F32)
            p = jnp.exp(s - lse_ref[i])
            dp = lax.dot_general(doi, vv, _NT, preferred_element_type=F32)
            ds = p * (dp - delta_ref[i])
            pt = p.T.astype(BF16)
            dst = ds.T.astype(BF16)
            dv_s[...] += lax.dot_general(pt, doi, _NN, preferred_element_type=F32)
            dk_i = lax.dot_general(dst, qi, _NN, preferred_element_type=F32)
            if shared_k:
                dk_s[...] += dk_i
            else:
                dk_s[:, LANES * i:LANES * (i + 1)] += dk_i
            dq_parts.append(lax.dot_general(ds.astype(BF16), ki, _NN, preferred_element_type=F32))
        rows = pl.ds(pl.multiple_of(qb * tq, tq), tq)
        if shared_k:
            tiles = [(slice(0, LANES), jnp.where(lo, dq_parts[0], dq_parts[1]))]
        else:
            tiles = [(slice(0, LANES), dq_parts[0]), (slice(LANES, 2 * LANES), dq_parts[1])]
        for cols, val in tiles:
            @pl.when(kb == 0)
            def _(cols=cols, val=val):
                dq_ref[rows, cols] = val

            @pl.when(kb > 0)
            def _(cols=cols, val=val):
                dq_ref[rows, cols] += val

        @pl.when(qb == nq - 1)
        def _():
            if shared_k:
                dk_ref[0] = dk_s[...]
                dv_ref[0] = dv_s[...]
            else:
                dk_ref[...] = dk_s[...]
                dv_ref[...] = dv_s[...]

    stat_spec = pl.BlockSpec((2, tq, 1), lambda p, j, i: (p, i, 0))
    do_spec = pl.BlockSpec((tq, LANES), lambda p, j, i: (i, p))
    dq_spec = pl.BlockSpec((t, wq), lambda p, j, i: (0, p))
    if shared_k:
        dk_spec = pl.BlockSpec((1, tk, LANES), lambda p, j, i: (p, j, 0))
        dv_spec = dk_spec
        dk_shape = jax.ShapeDtypeStruct((groups, t, LANES), F32)
        dv_shape = dk_shape
    else:
        dk_spec = pl.BlockSpec((tk, wq), lambda p, j, i: (j, p))
        dv_spec = pl.BlockSpec((tk, LANES), lambda p, j, i: (j, p))
        dk_shape = jax.ShapeDtypeStruct((t, wq * groups), F32)
        dv_shape = jax.ShapeDtypeStruct((t, LANES * groups), F32)
    return pl.pallas_call(
        body, name=name, grid=(groups, nk, nq),
        in_specs=[q_spec, k_spec, v_spec, do_spec, stat_spec, stat_spec],
        out_specs=[dq_spec, dk_spec, dv_spec],
        out_shape=[jax.ShapeDtypeStruct((t, wq * groups), F32), dk_shape, dv_shape],
        scratch_shapes=[pltpu.VMEM((tk, wq), F32), pltpu.VMEM((tk, LANES), F32)],
        compiler_params=_params("parallel", "arbitrary", "arbitrary"),
    )(q, k, v, do, lse, delta)


_MERGE_W = 512
_GATE_BLK0 = Z_GATE // _MERGE_W


def _merge_fwd(z, b_gate, ta, tb):
    t = z.shape[0]
    tm = _tile(t, 512)
    w = _MERGE_W
    nj = D_MODEL // w

    def body(za_ref, zb_ref, ba_ref, bb_ref, ta_ref, tb_ref, o_ref):
        ga = jax.nn.sigmoid(za_ref[...] + ba_ref[...])
        gb = jax.nn.sigmoid(zb_ref[...] + bb_ref[...])
        o_ref[...] = (ga * ta_ref[...] + gb * tb_ref[...]).astype(BF16)

    return pl.pallas_call(
        body, name="merge_fwd", grid=(t // tm, nj),
        in_specs=[pl.BlockSpec((tm, w), lambda i, j: (i, _GATE_BLK0 + j)),
                  pl.BlockSpec((tm, w), lambda i, j: (i, _GATE_BLK0 + nj + j)),
                  pl.BlockSpec((1, w), lambda i, j: (0, j)),
                  pl.BlockSpec((1, w), lambda i, j: (0, nj + j)),
                  pl.BlockSpec((tm, w), lambda i, j: (i, j)),
                  pl.BlockSpec((tm, w), lambda i, j: (i, j))],
        out_specs=pl.BlockSpec((tm, w), lambda i, j: (i, j)),
        out_shape=jax.ShapeDtypeStruct((t, D_MODEL), BF16),
        compiler_params=_params("parallel", "parallel"),
    )(z, z, b_gate, b_gate, ta, tb)


def _merge_bwd(dmg, z, b_gate, ta, tb):
    t = z.shape[0]
    tm = _tile(t, 512)
    w = _MERGE_W
    nj = D_MODEL // w

    def body(dm_ref, za_ref, zb_ref, ba_ref, bb_ref, ta_ref, tb_ref, dta_ref, dtb_ref, dza_ref, dzb_ref,
             dba_ref, dbb_ref):
        dm = dm_ref[...]
        ga = jax.nn.sigmoid(za_ref[...] + ba_ref[...])
        gb = jax.nn.sigmoid(zb_ref[...] + bb_ref[...])
        dta_ref[...] = (dm * ga).astype(BF16)
        dtb_ref[...] = (dm * gb).astype(BF16)
        dza = dm * ta_ref[...] * ga * (1.0 - ga)
        dzb = dm * tb_ref[...] * gb * (1.0 - gb)
        dza_ref[...] = dza.astype(BF16)
        dzb_ref[...] = dzb.astype(BF16)

        @pl.when(pl.program_id(1) == 0)
        def _():
            dba_ref[...] = jnp.zeros_like(dba_ref)
            dbb_ref[...] = jnp.zeros_like(dbb_ref)

        dba_ref[...] += _fold8(dza)
        dbb_ref[...] += _fold8(dzb)

    blk = pl.BlockSpec((tm, w), lambda j, i: (i, j))
    acc = pl.BlockSpec((SUBLANES, w), lambda j, i: (0, j))
    return pl.pallas_call(
        body, name="merge_bwd", grid=(nj, t // tm),
        in_specs=[blk,
                  pl.BlockSpec((tm, w), lambda j, i: (i, _GATE_BLK0 + j)),
                  pl.BlockSpec((tm, w), lambda j, i: (i, _GATE_BLK0 + nj + j)),
                  pl.BlockSpec((1, w), lambda j, i: (0, j)),
                  pl.BlockSpec((1, w), lambda j, i: (0, nj + j)),
                  blk, blk],
        out_specs=[blk, blk, blk, blk, acc, acc],
        out_shape=[jax.ShapeDtypeStruct((t, D_MODEL), BF16)] * 4 + [jax.ShapeDtypeStruct((SUBLANES, D_MODEL), F32)] * 2,
        compiler_params=_params("parallel", "arbitrary"),
    )(dmg, z, z, b_gate, b_gate, ta, tb)


def _relu2_fwd(h):
    t, f = h.shape
    tm, tn = _tile(t, 512), _tile(f, 1024)

    def body(h_ref, a_ref):
        r = jnp.maximum(h_ref[...], 0.0)
        a_ref[...] = (r * r).astype(BF16)

    spec = pl.BlockSpec((tm, tn), lambda i, j: (i, j))
    return pl.pallas_call(
        body, name="relu2_fwd", grid=(t // tm, f // tn), in_specs=[spec], out_specs=spec,
        out_shape=jax.ShapeDtypeStruct((t, f), BF16), compiler_params=_params("parallel", "parallel"),
    )(h)


def _relu2_bwd(da, h):
    t, f = h.shape
    tm, tn = _tile(t, 512), _tile(f, 1024)

    def body(da_ref, h_ref, dh_ref):
        dh_ref[...] = (da_ref[...] * (2.0 * jnp.maximum(h_ref[...], 0.0))).astype(BF16)

    spec = pl.BlockSpec((tm, tn), lambda i, j: (i, j))
    return pl.pallas_call(
        body, name="relu2_bwd", grid=(t // tm, f // tn), in_specs=[spec, spec], out_specs=spec,
        out_shape=jax.ShapeDtypeStruct((t, f), BF16), compiler_params=_params("parallel", "parallel"),
    )(da, h)


def _loss_grad(y, target):
    t, d = y.shape
    tm = _tile(t, 512)

    def body(y_ref, t_ref, dy_ref, acc_ref):
        err = y_ref[...] - t_ref[...]
        dy_ref[...] = err * (1.0 / d)
        e8 = _fold8(err * err)
        part = e8[:, 0:LANES]
        for c in range(1, d // LANES):
            part = part + e8[:, LANES * c:LANES * (c + 1)]

        @pl.when(pl.program_id(0) == 0)
        def _():
            acc_ref[...] = jnp.zeros_like(acc_ref)

        acc_ref[...] += part

    return pl.pallas_call(
        body, name="loss_grad", grid=(t // tm,),
        in_specs=[_row_spec(tm, d), _row_spec(tm, d)],
        out_specs=[_row_spec(tm, d), _acc_spec(LANES)],
        out_shape=[jax.ShapeDtypeStruct((t, d), F32), jax.ShapeDtypeStruct((SUBLANES, LANES), F32)],
        compiler_params=_params("arbitrary"),
    )(y, target)


_MESH_ID = pl.DeviceIdType.MESH
_ANY = pl.BlockSpec(memory_space=pl.ANY)


def _all_gather(x):
    r, c_ = x.shape

    def body(x_ref, out_ref, send_sems, recv_sems, local_sem):
        mx, my, mc = lax.axis_index("x"), lax.axis_index("y"), lax.axis_index("c")
        me, sibling = (mx, my, mc), (mx, my, 1 - mc)
        chips = [(1 - mx, my), (mx, 1 - my), (1 - mx, 1 - my)]

        def slot(px, py, pc):
            return out_ref.at[4 * px + 2 * py + pc]

        def copy(sem, block, to, src=None):
            return pltpu.make_async_remote_copy(
                src_ref=slot(*block) if src is None else src, dst_ref=slot(*block),
                send_sem=send_sems.at[sem], recv_sem=recv_sems.at[sem], device_id=to, device_id_type=_MESH_ID)

        mine = pltpu.make_async_copy(x_ref, slot(*me), local_sem)
        mine.start()
        first = [copy(0, me, sibling, src=x_ref)]
        first += [copy(1 + j, me, (*chip, mc), src=x_ref) for j, chip in enumerate(chips)]
        for cp in first:
            cp.start()
        passed = [copy(4 + j, (*chip, mc), sibling) for j, chip in enumerate(chips)]
        for j, chip in enumerate(chips):
            copy(1 + j, (*chip, mc), me).wait_recv()
            passed[j].start()
        copy(0, sibling, me).wait_recv()
        for j, chip in enumerate(chips):
            copy(4 + j, (*chip, 1 - mc), me).wait_recv()
        for cp in first + passed:
            cp.wait_send()
        mine.wait()

    return pl.pallas_call(
        body, name="weight_all_gather",
        out_shape=jax.ShapeDtypeStruct((N_DEV, r, c_), x.dtype),
        in_specs=[_ANY], out_specs=_ANY,
        scratch_shapes=[pltpu.SemaphoreType.DMA((7,)), pltpu.SemaphoreType.DMA((7,)), pltpu.SemaphoreType.DMA],
    )(x)


def _all_to_all(send):
    _, r, c_ = send.shape

    def body(s_ref, r_ref, send_sems, recv_sems, local_sem):
        mx, my, mc = lax.axis_index("x"), lax.axis_index("y"), lax.axis_index("c")
        me = 4 * mx + 2 * my + mc
        local = pltpu.make_async_copy(s_ref.at[me], r_ref.at[me], local_sem)
        local.start()
        copies = []
        for rel in range(1, N_DEV):
            px = 1 - mx if rel & 4 else mx
            py = 1 - my if rel & 2 else my
            pc = 1 - mc if rel & 1 else mc
            peer = 4 * px + 2 * py + pc
            cp = pltpu.make_async_remote_copy(
                src_ref=s_ref.at[peer], dst_ref=r_ref.at[me], send_sem=send_sems.at[rel - 1],
                recv_sem=recv_sems.at[rel - 1], device_id=(px, py, pc), device_id_type=_MESH_ID)
            cp.start()
            copies.append(cp)
        for cp in copies:
            cp.wait_send()
            cp.wait_recv()
        local.wait()

    return pl.pallas_call(
        body, name="grad_all_to_all",
        out_shape=jax.ShapeDtypeStruct(send.shape, send.dtype),
        in_specs=[_ANY], out_specs=_ANY,
        scratch_shapes=[pltpu.SemaphoreType.DMA((7,)), pltpu.SemaphoreType.DMA((7,)), pltpu.SemaphoreType.DMA],
    )(send)


def _adamw(recv, w, m, v):
    r = w.shape[0]
    tr = _tile(r, PACK_ROW_TILE)

    def body(g_ref, w_ref, m_ref, v_ref, go_ref, d_ref, mo_ref, vo_ref):
        g = g_ref[0]
        for s in range(1, N_DEV):
            g = g + g_ref[s]
        go_ref[...] = g
        mn = ADAM_B1 * m_ref[...] + (1.0 - ADAM_B1) * g
        vn = ADAM_B2 * v_ref[...] + (1.0 - ADAM_B2) * (g * g)
        mo_ref[...] = mn
        vo_ref[...] = vn
        m_hat = mn / (1.0 - ADAM_B1 ** ADAM_STEP)
        v_hat = vn / (1.0 - ADAM_B2 ** ADAM_STEP)
        d_ref[...] = -ADAM_LR * (m_hat / (jnp.sqrt(v_hat) + ADAM_EPS) + ADAM_WD * w_ref[...])

    spec = pl.BlockSpec((tr, LANES), lambda i: (i, 0))
    out = jax.ShapeDtypeStruct((r, LANES), F32)
    return pl.pallas_call(
        body, name="grad_sum_adamw", grid=(r // tr,),
        in_specs=[pl.BlockSpec((N_DEV, tr, LANES), lambda i: (0, i, 0)), spec, spec, spec],
        out_specs=[spec, spec, spec, spec], out_shape=[out, out, out, out],
        compiler_params=_params("parallel"),
    )(recv, w, m, v)


def _pad_cols(a, before, after):
    parts = []
    if before:
        parts.append(jnp.zeros(a.shape[:-1] + (before,), a.dtype))
    parts.append(a)
    if after:
        parts.append(jnp.zeros(a.shape[:-1] + (after,), a.dtype))
    return jnp.concatenate(parts, axis=-1)


def _q_head_pairs(a, axis):
    shp = a.shape
    a = a.reshape(shp[:axis] + (GQA_KV_HEADS, GQA_GROUP, HEAD_DIM) + shp[axis + 1:])
    a = jnp.swapaxes(a, axis, axis + 1)
    return a.reshape(shp)


def _q_head_unpairs(a, axis):
    shp = a.shape
    a = a.reshape(shp[:axis] + (GQA_GROUP, GQA_KV_HEADS, HEAD_DIM) + shp[axis + 1:])
    a = jnp.swapaxes(a, axis, axis + 1)
    return a.reshape(shp)


def _layout_weights(w):
    w_in = w["w_in"]
    lead = w_in.shape[:-1]
    w_in_p = jnp.concatenate([
        _q_head_pairs(w_in[..., 0:512], w_in.ndim - 1),
        w_in[..., 512:1408],
        _pad_cols(w_in[..., 1408:1440], KR_LANE0, LANES - KR_LANE0 - MLA_ROPE_DIM),
        w_in[..., 1440:],
    ], axis=-1)
    wq = w["w_q_up"]
    wq_p = _pad_cols(wq.reshape(wq.shape[:-1] + (MLA_HEADS, MLA_QK_DIM)), 0, LANES - MLA_QK_DIM)
    wq_p = wq_p.reshape(wq.shape[:-1] + (MLA_HEADS * LANES,))
    wkv = w["w_kv_up"]
    wkv4 = wkv.reshape(wkv.shape[:-1] + (MLA_HEADS, 2 * HEAD_DIM))
    wk_p = _pad_cols(wkv4[..., :HEAD_DIM], 0, LANES - HEAD_DIM).reshape(wkv.shape[:-1] + (MLA_HEADS * LANES,))
    wv_p = wkv4[..., HEAD_DIM:].reshape(wkv.shape[:-1] + (MLA_HEADS * HEAD_DIM,))
    del lead
    return {
        "w_in": w_in_p, "w_q_up": wq_p, "w_kv_up": jnp.concatenate([wk_p, wv_p], axis=-1),
        "w_branch_a": _q_head_pairs(w["w_branch_a"], w["w_branch_a"].ndim - 2), "w_branch_b": w["w_branch_b"],
        "w_o": w["w_o"], "w_ffn_up": w["w_ffn_up"], "w_ffn_down": w["w_ffn_down"],
    }


def _unlayout_grads(g):
    gi = g["w_in"]
    kr0 = Z_KR + KR_LANE0
    g_in = jnp.concatenate([
        _q_head_unpairs(gi[..., 0:512], gi.ndim - 1), gi[..., 512:1408], gi[..., kr0:kr0 + MLA_ROPE_DIM],
        gi[..., Z_GATE:],
    ], axis=-1)
    gq = g["w_q_up"]
    gq = gq.reshape(gq.shape[:-1] + (MLA_HEADS, LANES))[..., :MLA_QK_DIM]
    gq = gq.reshape(gq.shape[:-2] + (MLA_HEADS * MLA_QK_DIM,))
    gkv = g["w_kv_up"]
    gk = gkv[..., :MLA_HEADS * LANES].reshape(gkv.shape[:-1] + (MLA_HEADS, LANES))[..., :HEAD_DIM]
    gv = gkv[..., MLA_HEADS * LANES:].reshape(gkv.shape[:-1] + (MLA_HEADS, HEAD_DIM))
    gkv = jnp.concatenate([gk, gv], axis=-1).reshape(gkv.shape[:-1] + (MLA_HEADS * 2 * HEAD_DIM,))
    return {
        "w_in": g_in, "w_q_up": gq, "w_kv_up": gkv,
        "w_branch_a": _q_head_unpairs(g["w_branch_a"], g["w_branch_a"].ndim - 2), "w_branch_b": g["w_branch_b"],
        "w_o": g["w_o"], "w_ffn_up": g["w_ffn_up"], "w_ffn_down": g["w_ffn_down"],
    }


def _pack_rows(flat_parts, lead=()):
    n = sum(p.shape[-1] for p in flat_parts)
    per = PACK_ROW_TILE * LANES
    pad = (-n) % per
    if pad:
        flat_parts = list(flat_parts) + [jnp.zeros(lead + (pad,), flat_parts[0].dtype)]
    flat = jnp.concatenate(flat_parts, axis=-1)
    return flat.reshape(lead + ((n + pad) // LANES, LANES))


def _unpack_rows(packed, shapes):
    flat = packed.reshape(-1)
    out, off = [], 0
    for shp in shapes:
        n = int(np.prod(shp))
        out.append(flat[off:off + n].reshape(shp))
        off += n
    return out


def _shards_of(full, axis):
    shp = full.shape
    cut = shp[:axis] + (N_DEV, shp[axis] // N_DEV) + shp[axis + 1:]
    return jnp.moveaxis(full.reshape(cut), axis, 0).reshape(N_DEV, -1)


def _from_shards(flat, shard_shape, axis):
    parts = jnp.moveaxis(flat.reshape((N_DEV,) + tuple(shard_shape)), 0, axis)
    full = list(shard_shape)
    full[axis] *= N_DEV
    return parts.reshape(full)


def _layer_fwd(x, u, lw, tabs):
    cos_a, sin_a, cos_b, sin_b = tabs
    z = _matmul(u, lw["w_in"], "nn", "mm_in")
    qa, ka, va, cqn, ckvn, krr = _prep_a_fwd(z, lw["gq2"], lw["gk2"], lw["gqa"], lw["gkva"], cos_a, sin_a, cos_b, sin_b)
    qb = _matmul(cqn, lw["w_q_up"], "nn", "mm_q_up")
    kvb = _matmul(ckvn, lw["w_kv_up"], "nn", "mm_kv_up")
    q_b, k_b, v_b = _prep_b_fwd(qb, kvb, krr, cos_b, sin_b)
    ya, lse_a = _attn_fwd(qa, ka, va, True, "gqa_fwd")
    yb, lse_b = _attn_fwd(q_b, k_b, v_b, False, "mla_fwd")
    ta = _matmul(ya, lw["w_branch_a"], "nn", "mm_branch_a")
    tb = _matmul(yb, lw["w_branch_b"], "nn", "mm_branch_b")
    merged = _merge_fwd(z, lw["b_gate"], ta, tb)
    m = _matmul(merged, lw["w_o"], "nn", "mm_o")
    x2, u2 = _res_norm_fwd(x, m, lw["post_mix_g"], lw["pre_ffn_g"])
    h = _matmul(u2, lw["w_ffn_up"], "nn", "mm_ffn_up")
    a = _relu2_fwd(h)
    f = _matmul(a, lw["w_ffn_down"], "nn", "mm_ffn_down")
    x3, u_next = _res_norm_fwd(x2, f, lw["post_ffn_g"], lw["next_pre_mix_g"])
    saved = dict(u=u, z=z, qa=qa, ka=ka, va=va, cqn=cqn, ckvn=ckvn, q_b=q_b, k_b=k_b, v_b=v_b, ya=ya, yb=yb,
                 lse_a=lse_a, lse_b=lse_b, ta=ta, tb=tb, merged=merged, m=m, x2=x2, u2=u2, h=h, a=a, f=f, x3=x3)
    return x3, u_next, saved


def _layer_bwd(dx3, du_next, lw, sv, tabs):
    cos_a, sin_a, cos_b, sin_b = tabs
    g = {}
    dx3, df, dg4, dg1n = _res_norm_bwd(sv["x3"], sv["f"], lw["post_ffn_g"], lw["next_pre_mix_g"], dx3, du_next)
    g["post_ffn_g"], g["next_pre_mix_g"] = dg4, dg1n
    da = _matmul(df, lw["w_ffn_down"], "nt", "mm_d_a")
    g["w_ffn_down"] = _matmul(sv["a"], df, "tn", "mm_dw_ffn_down")
    dh = _relu2_bwd(da, sv["h"])
    du2 = _matmul(dh, lw["w_ffn_up"], "nt", "mm_d_u2")
    g["w_ffn_up"] = _matmul(sv["u2"], dh, "tn", "mm_dw_ffn_up")
    dx2, dm, dg2, dg3 = _res_norm_bwd(sv["x2"], sv["m"], lw["post_mix_g"], lw["pre_ffn_g"], dx3, du2)
    g["post_mix_g"], g["pre_ffn_g"] = dg2, dg3
    dmg = _matmul(dm, lw["w_o"], "nt", "mm_d_merged")
    g["w_o"] = _matmul(sv["merged"], dm, "tn", "mm_dw_o")
    dta, dtb, dzg_a, dzg_b, db_a, db_b = _merge_bwd(dmg, sv["z"], lw["b_gate"], sv["ta"], sv["tb"])
    g["b_gate"] = jnp.concatenate([db_a, db_b], axis=-1)
    dya = _matmul(dta, lw["w_branch_a"], "nt", "mm_d_ya")
    g["w_branch_a"] = _matmul(sv["ya"], dta, "tn", "mm_dw_branch_a")
    dyb = _matmul(dtb, lw["w_branch_b"], "nt", "mm_d_yb")
    g["w_branch_b"] = _matmul(sv["yb"], dtb, "tn", "mm_dw_branch_b")
    delta_a, dya16 = _attn_delta(dya, sv["ya"])
    delta_b, dyb16 = _attn_delta(dyb, sv["yb"])
    dqa, dka4, dva4 = _attn_bwd(sv["qa"], sv["ka"], sv["va"], dya16, sv["lse_a"], delta_a, True, "gqa_bwd")
    dq_b, dk_b, dv_b = _attn_bwd(sv["q_b"], sv["k_b"], sv["v_b"], dyb16, sv["lse_b"], delta_b, False, "mla_bwd")
    dqb, dkvb, dkr = _prep_b_bwd(dq_b, dk_b, dv_b, cos_b, sin_b)
    dcqn = _matmul(dqb, lw["w_q_up"], "nt", "mm_d_cqn")
    g["w_q_up"] = _matmul(sv["cqn"], dqb, "tn", "mm_dw_q_up")
    dckvn = _matmul(dkvb, lw["w_kv_up"], "nt", "mm_d_ckvn")
    g["w_kv_up"] = _matmul(sv["ckvn"], dkvb, "tn", "mm_dw_kv_up")
    dz, dgq, dgk, dgqa, dgkva = _prep_a_bwd(sv["z"], dqa, dka4, dva4, dcqn, dckvn, dkr, dzg_a, dzg_b, lw["gq2"],
                                            lw["gk2"], lw["gqa"], lw["gkva"], cos_a, sin_a)
    g["q_norm_g"], g["k_norm_g"], g["q_a_norm_g"], g["kv_a_norm_g"] = dgq, dgk, dgqa, dgkva
    du = _matmul(dz, lw["w_in"], "nt", "mm_d_u")
    g["w_in"] = _matmul(sv["u"], dz, "tn", "mm_dw_in")
    return dx2, du, g


def kernel(x, w_in, b_gate, q_norm_g, k_norm_g, q_a_norm_g, kv_a_norm_g, w_q_up, w_kv_up, w_branch_a, w_branch_b, w_o, w_ffn_up, w_ffn_down, pre_mix_g, post_mix_g, pre_ffn_g, post_ffn_g, loss_target, m_w_in, m_b_gate, m_q_norm_g, m_k_norm_g, m_q_a_norm_g, m_kv_a_norm_g, m_w_q_up, m_w_kv_up, m_w_branch_a, m_w_branch_b, m_w_o, m_w_ffn_up, m_w_ffn_down, m_pre_mix_g, m_post_mix_g, m_pre_ffn_g, m_post_ffn_g, v_w_in, v_b_gate, v_q_norm_g, v_k_norm_g, v_q_a_norm_g, v_kv_a_norm_g, v_w_q_up, v_w_kv_up, v_w_branch_a, v_w_branch_b, v_w_o, v_w_ffn_up, v_w_ffn_down, v_pre_mix_g, v_post_mix_g, v_pre_ffn_g, v_post_ffn_g):
    weights = dict(zip(WEIGHT_NAMES, (w_in, b_gate, q_norm_g, k_norm_g, q_a_norm_g, kv_a_norm_g, w_q_up, w_kv_up,
                                      w_branch_a, w_branch_b, w_o, w_ffn_up, w_ffn_down, pre_mix_g, post_mix_g,
                                      pre_ffn_g, post_ffn_g)))
    mom_m = dict(zip(WEIGHT_NAMES, (m_w_in, m_b_gate, m_q_norm_g, m_k_norm_g, m_q_a_norm_g, m_kv_a_norm_g, m_w_q_up,
                                    m_w_kv_up, m_w_branch_a, m_w_branch_b, m_w_o, m_w_ffn_up, m_w_ffn_down,
                                    m_pre_mix_g, m_post_mix_g, m_pre_ffn_g, m_post_ffn_g)))
    mom_v = dict(zip(WEIGHT_NAMES, (v_w_in, v_b_gate, v_q_norm_g, v_k_norm_g, v_q_a_norm_g, v_kv_a_norm_g, v_w_q_up,
                                    v_w_kv_up, v_w_branch_a, v_w_branch_b, v_w_o, v_w_ffn_up, v_w_ffn_down,
                                    v_pre_mix_g, v_post_mix_g, v_pre_ffn_g, v_post_ffn_g)))
    assert x.shape[0] == 1 and x.shape[2] == D_MODEL, x.shape
    n_layers = w_in.shape[0]
    t = x.shape[1]
    x0 = x.reshape(t, D_MODEL)
    target = loss_target.reshape(t, D_MODEL)
    shard_shapes = {n: weights[n].shape for n in BIG_NAMES}
    small_shapes = [weights[n].shape for n in SMALL_NAMES]

    packed = _pack_rows([weights[n].astype(BF16).reshape(-1) for n in BIG_NAMES])
    gathered = _all_gather(packed).reshape(N_DEV, -1)
    full, off = {}, 0
    for n in BIG_NAMES:
        cnt = int(np.prod(shard_shapes[n]))
        full[n] = _from_shards(gathered[:, off:off + cnt], shard_shapes[n], SHARD_AXIS[n])
        off += cnt
    lw_all = _layout_weights(full)
    lw_all["b_gate"] = b_gate.reshape(n_layers, 1, 2 * D_MODEL)
    lw_all["gq2"] = jnp.tile(q_norm_g, (1, 2)).reshape(n_layers, 1, LANES)
    lw_all["gk2"] = jnp.tile(k_norm_g, (1, 2)).reshape(n_layers, 1, LANES)
    lw_all["gqa"] = q_a_norm_g.reshape(n_layers, 1, MLA_Q_RANK)
    lw_all["gkva"] = kv_a_norm_g.reshape(n_layers, 1, MLA_KV_RANK)
    for n in ("post_mix_g", "pre_ffn_g", "post_ffn_g"):
        lw_all[n] = weights[n]
    lw_all["next_pre_mix_g"] = jnp.roll(pre_mix_g, -1, axis=0)

    tabs = _rope_tables(t)
    u0 = _rms_fwd(x0, pre_mix_g[0])

    def fwd_step(carry, lw):
        xc, uc = carry
        x3, u_next, saved = _layer_fwd(xc, uc, lw, tabs)
        return (x3, u_next), saved

    (y, _), saved_all = lax.scan(fwd_step, (x0, u0), lw_all)
    dy, loss_acc = _loss_grad(y, target)
    loss = lax.psum(0.5 * jnp.sum(loss_acc) / D_MODEL, ("x", "y", "c"))

    def bwd_step(carry, xs):
        dxc, duc = carry
        lw, sv = xs
        dx, du, g = _layer_bwd(dxc, duc, lw, sv, tabs)
        return (dx, du), g

    (dx0, du0), grads = lax.scan(bwd_step, (dy, jnp.zeros((t, D_MODEL), F32)), (lw_all, saved_all), reverse=True)
    grad_x, dg1_first = _rms_bwd(x0, pre_mix_g[0], dx0, du0)

    big_grads = _unlayout_grads({n: grads[n] for n in BIG_NAMES})
    fold = lambda a: a.sum(axis=1)
    dgq = fold(grads["q_norm_g"]).reshape(n_layers, 2, HEAD_DIM).sum(axis=1)
    dgk = fold(grads["k_norm_g"]).reshape(n_layers, 2, HEAD_DIM).sum(axis=1)
    dg1 = jnp.concatenate([fold(dg1_first[None]), fold(grads["next_pre_mix_g"])[:-1]], axis=0)
    small_grads = {
        "b_gate": fold(grads["b_gate"]), "q_norm_g": dgq, "k_norm_g": dgk, "q_a_norm_g": fold(grads["q_a_norm_g"]),
        "kv_a_norm_g": fold(grads["kv_a_norm_g"]), "pre_mix_g": dg1, "post_mix_g": fold(grads["post_mix_g"]),
        "pre_ffn_g": fold(grads["pre_ffn_g"]), "post_ffn_g": fold(grads["post_ffn_g"]),
    }
    small_flat = jnp.concatenate([small_grads[n].reshape(-1) for n in SMALL_NAMES])
    send_parts = [_shards_of(big_grads[n], SHARD_AXIS[n]) for n in BIG_NAMES]
    send_parts.append(jnp.broadcast_to(small_flat[None], (N_DEV, small_flat.shape[0])))
    send = _pack_rows(send_parts, lead=(N_DEV,))
    recv = _all_to_all(send)

    def pack_state(d):
        return _pack_rows([d[n].reshape(-1) for n in BIG_NAMES] + [d[n].reshape(-1) for n in SMALL_NAMES])

    g_p, d_p, m_p, v_p = _adamw(recv, pack_state(weights), pack_state(mom_m), pack_state(mom_v))
    shapes = [shard_shapes[n] for n in BIG_NAMES] + small_shapes
    order = BIG_NAMES + SMALL_NAMES
    outs = []
    for packed_out in (g_p, d_p, m_p, v_p):
        by_name = dict(zip(order, _unpack_rows(packed_out, shapes)))
        outs.extend(by_name[n] for n in WEIGHT_NAMES)
    return (loss, grad_x.reshape(x.shape), *outs)
```

```python
import functools
import math

import jax
import jax.numpy as jnp
import numpy as np
from jax import lax
from jax.experimental import pallas as pl
from jax.experimental.pallas import tpu as pltpu

F32 = jnp.float32
BF16 = jnp.bfloat16

D_MODEL = 1024
GRID_W = 64
ROPE_THETA = 10000.0
EPS = 1e-6
GQA_HEADS = 8
GQA_KV_HEADS = 2
GQA_GROUP = GQA_HEADS // GQA_KV_HEADS
HEAD_DIM = 64
MLA_HEADS = 8
MLA_ROPE_DIM = 32
MLA_QK_DIM = 96
MLA_Q_RANK = 384
MLA_KV_RANK = 256
D_FF = 4 * D_MODEL
GQA_SCALE = 1.0 / math.sqrt(HEAD_DIM)
MLA_SCALE = 1.0 / math.sqrt(MLA_QK_DIM)
LOG2E = math.log2(math.e)
LN2 = math.log(2.0)

ADAM_LR = 0.001
ADAM_B1 = 0.9
ADAM_B2 = 0.999
ADAM_EPS = 1e-08
ADAM_WD = 0.01
ADAM_STEP = 10

N_DEV = 8
LANES = 128
SUBLANES = 8
VMEM_LIMIT = 48 * 1024 * 1024

Z_QA, Z_KA, Z_VA, Z_CQ, Z_CKV, Z_KR, Z_GATE = 0, 512, 640, 768, 1152, 1408, 1536
Z_ATT_W = 1536
Z_W = 3584
KR_LANE0 = 64

WEIGHT_NAMES = ("w_in", "b_gate", "q_norm_g", "k_norm_g", "q_a_norm_g", "kv_a_norm_g", "w_q_up", "w_kv_up",
                "w_branch_a", "w_branch_b", "w_o", "w_ffn_up", "w_ffn_down", "pre_mix_g", "post_mix_g",
                "pre_ffn_g", "post_ffn_g")
SHARD_AXIS = {"w_in": 2, "w_q_up": 2, "w_kv_up": 2, "w_branch_a": 2, "w_branch_b": 2, "w_o": 1, "w_ffn_up": 2,
              "w_ffn_down": 1}
BIG_NAMES = tuple(n for n in WEIGHT_NAMES if n in SHARD_AXIS)
SMALL_NAMES = tuple(n for n in WEIGHT_NAMES if n not in SHARD_AXIS)
PACK_ROW_TILE = 1024


def _params(*semantics):
    return pltpu.CompilerParams(dimension_semantics=semantics, vmem_limit_bytes=VMEM_LIMIT)


def _tile(n, pref):
    if n <= pref:
        return n
    t = (pref // LANES) * LANES
    while n % t:
        t -= LANES
    return t


def _fold8(t):
    return t.reshape(t.shape[0] // SUBLANES, SUBLANES, t.shape[1]).sum(axis=0)


_DIMS = {"nn": ((1,), (0,)), "nt": ((1,), (1,)), "tn": ((0,), (0,))}


def _matmul(a, b, mode, name):
    if mode == "nn":
        (m, k), n = a.shape, b.shape[1]
    elif mode == "nt":
        (m, k), n = a.shape, b.shape[0]
    else:
        (k, m), n = a.shape, b.shape[1]
    tm, tn = _tile(m, 512), _tile(n, 512)
    tk = _tile(k, 512 if mode == "tn" else 1024)
    nk = k // tk
    dims = (_DIMS[mode], ((), ()))

    def body(a_ref, b_ref, o_ref, acc_ref):
        prod = lax.dot_general(a_ref[...], b_ref[...], dims, preferred_element_type=F32)
        if nk == 1:
            o_ref[...] = prod
        else:
            kk = pl.program_id(2)

            @pl.when(kk == 0)
            def _():
                acc_ref[...] = prod

            @pl.when(kk > 0)
            def _():
                acc_ref[...] += prod

            @pl.when(kk == nk - 1)
            def _():
                o_ref[...] = acc_ref[...]

    if mode == "tn":
        a_spec = pl.BlockSpec((tk, tm), lambda i, j, kk: (kk, i))
    else:
        a_spec = pl.BlockSpec((tm, tk), lambda i, j, kk: (i, kk))
    if mode == "nt":
        b_spec = pl.BlockSpec((tn, tk), lambda i, j, kk: (j, kk))
    else:
        b_spec = pl.BlockSpec((tk, tn), lambda i, j, kk: (kk, j))
    return pl.pallas_call(
        body,
        name=name,
        grid=(m // tm, n // tn, nk),
        in_specs=[a_spec, b_spec],
        out_specs=pl.BlockSpec((tm, tn), lambda i, j, kk: (i, j)),
        out_shape=jax.ShapeDtypeStruct((m, n), F32),
        scratch_shapes=[pltpu.VMEM((tm, tn), F32)],
        compiler_params=_params("parallel", "parallel", "arbitrary"),
    )(a, b)


def _rinv(x):
    return lax.rsqrt(jnp.mean(x * x, axis=-1, keepdims=True) + EPS)


def _rms_bwd_rows(x, g, dy):
    r = _rinv(x)
    xh = x * r
    dxh = dy * g
    dx = r * (dxh - xh * jnp.mean(dxh * xh, axis=-1, keepdims=True))
    return dx, dy * xh


def _row_spec(tm, c):
    return pl.BlockSpec((tm, c), lambda i: (i, 0))


def _vec_spec(c):
    return pl.BlockSpec((1, c), lambda i: (0, 0))


def _acc_spec(c):
    return pl.BlockSpec((SUBLANES, c), lambda i: (0, 0))


def _rms_fwd(x, g):
    t, d = x.shape
    tm = _tile(t, 512)

    def body(x_ref, g_ref, o_ref):
        xv = x_ref[...]
        o_ref[...] = (xv * _rinv(xv) * g_ref[...]).astype(BF16)

    return pl.pallas_call(
        body, name="rms_fwd", grid=(t // tm,),
        in_specs=[_row_spec(tm, d), _vec_spec(d)], out_specs=_row_spec(tm, d),
        out_shape=jax.ShapeDtypeStruct((t, d), BF16), compiler_params=_params("parallel"),
    )(x, g.reshape(1, d))


def _rms_bwd(x, g, dres, dy):
    t, d = x.shape
    tm = _tile(t, 512)

    def body(x_ref, g_ref, dres_ref, dy_ref, dx_ref, dg_ref):
        dx, dgc = _rms_bwd_rows(x_ref[...], g_ref[...], dy_ref[...])
        dx_ref[...] = dres_ref[...] + dx

        @pl.when(pl.program_id(0) == 0)
        def _():
            dg_ref[...] = jnp.zeros_like(dg_ref)

        dg_ref[...] += _fold8(dgc)

    return pl.pallas_call(
        body, name="rms_bwd", grid=(t // tm,),
        in_specs=[_row_spec(tm, d), _vec_spec(d), _row_spec(tm, d), _row_spec(tm, d)],
        out_specs=[_row_spec(tm, d), _acc_spec(d)],
        out_shape=[jax.ShapeDtypeStruct((t, d), F32), jax.ShapeDtypeStruct((SUBLANES, d), F32)],
        compiler_params=_params("arbitrary"),
    )(x, g.reshape(1, d), dres, dy)


def _res_norm_fwd(x, m, g_post, g_next):
    t, d = x.shape
    tm = _tile(t, 512)

    def body(x_ref, m_ref, gp_ref, gn_ref, x2_ref, u2_ref):
        mv = m_ref[...]
        x2 = x_ref[...] + mv * _rinv(mv) * gp_ref[...]
        x2_ref[...] = x2
        u2_ref[...] = (x2 * _rinv(x2) * gn_ref[...]).astype(BF16)

    return pl.pallas_call(
        body, name="res_norm_fwd", grid=(t // tm,),
        in_specs=[_row_spec(tm, d), _row_spec(tm, d), _vec_spec(d), _vec_spec(d)],
        out_specs=[_row_spec(tm, d), _row_spec(tm, d)],
        out_shape=[jax.ShapeDtypeStruct((t, d), F32), jax.ShapeDtypeStruct((t, d), BF16)],
        compiler_params=_params("parallel"),
    )(x, m, g_post.reshape(1, d), g_next.reshape(1, d))


def _res_norm_bwd(x2, m, g_post, g_next, dx2_in, du2):
    t, d = x2.shape
    tm = _tile(t, 512)

    def body(x2_ref, m_ref, gp_ref, gn_ref, dx2in_ref, du2_ref, dx2_ref, dm_ref, dgp_ref, dgn_ref):
        dxn, dgn_c = _rms_bwd_rows(x2_ref[...], gn_ref[...], du2_ref[...])
        dx2 = dx2in_ref[...] + dxn
        dx2_ref[...] = dx2
        dm, dgp_c = _rms_bwd_rows(m_ref[...], gp_ref[...], dx2)
        dm_ref[...] = dm.astype(BF16)

        @pl.when(pl.program_id(0) == 0)
        def _():
            dgp_ref[...] = jnp.zeros_like(dgp_ref)
            dgn_ref[...] = jnp.zeros_like(dgn_ref)

        dgp_ref[...] += _fold8(dgp_c)
        dgn_ref[...] += _fold8(dgn_c)

    return pl.pallas_call(
        body, name="res_norm_bwd", grid=(t // tm,),
        in_specs=[_row_spec(tm, d), _row_spec(tm, d), _vec_spec(d), _vec_spec(d), _row_spec(tm, d), _row_spec(tm, d)],
        out_specs=[_row_spec(tm, d), _row_spec(tm, d), _acc_spec(d), _acc_spec(d)],
        out_shape=[jax.ShapeDtypeStruct((t, d), F32), jax.ShapeDtypeStruct((t, d), BF16),
                   jax.ShapeDtypeStruct((SUBLANES, d), F32), jax.ShapeDtypeStruct((SUBLANES, d), F32)],
        compiler_params=_params("arbitrary"),
    )(x2, m, g_post.reshape(1, d), g_next.reshape(1, d), dx2_in, du2)


def _rope_tables(t):
    rows = t // GRID_W
    row = jnp.repeat(jnp.arange(rows, dtype=F32), GRID_W)
    col = jnp.tile(jnp.arange(GRID_W, dtype=F32), rows)

    def tab(rot_dim):
        half = rot_dim // 2
        inv = ROPE_THETA ** (-jnp.arange(0, half, 2, dtype=F32) / half)
        ar = row[:, None] * inv[None, :]
        ac = col[:, None] * inv[None, :]
        ang = jnp.concatenate([ar, ar, ac, ac], axis=-1)
        q = half // 2
        sign = np.tile(np.concatenate([-np.ones(q, np.float32), np.ones(q, np.float32)]), 2)
        return jnp.cos(ang), jnp.sin(ang) * sign[None, :]

    ca, sa = tab(HEAD_DIM)
    cb, sb = tab(MLA_ROPE_DIM)
    one = jnp.ones((t, 1), F32)
    cos_b = jnp.concatenate([one * jnp.ones((1, KR_LANE0), F32), cb, one * jnp.ones((1, 32), F32)], axis=-1)
    sin_b = jnp.concatenate([jnp.zeros((t, KR_LANE0), F32), sb, jnp.zeros((t, 32), F32)], axis=-1)
    return jnp.tile(ca, (1, GQA_HEADS)), jnp.tile(sa, (1, GQA_HEADS)), cos_b, sin_b


def _swap_halves(x, sh):
    lane = lax.broadcasted_iota(jnp.int32, x.shape, 1)
    up = pltpu.roll(x, LANES - sh, 1)
    dn = pltpu.roll(x, sh, 1)
    return jnp.where((lane & (2 * sh - 1)) < sh, up, dn)


def _rope(x, cos, sin_s, sh):
    return x * cos + _swap_halves(x, sh) * sin_s


def _rope_bwd(dy, cos, sin_s, sh):
    return dy * cos + _swap_halves(dy * sin_s, sh)


def _lo_mask(shape):
    return lax.broadcasted_iota(jnp.int32, shape, 1) < HEAD_DIM


def _half_mean(t, lo):
    s_lo = jnp.sum(jnp.where(lo, t, 0.0), axis=-1, keepdims=True)
    s_hi = jnp.sum(jnp.where(lo, 0.0, t), axis=-1, keepdims=True)
    return jnp.where(lo, s_lo, s_hi) * (1.0 / HEAD_DIM)


def _head_norm(x, g2):
    lo = _lo_mask(x.shape)
    r = lax.rsqrt(_half_mean(x * x, lo) + EPS)
    return x * r * g2


def _head_norm_bwd(x, g2, dy):
    lo = _lo_mask(x.shape)
    r = lax.rsqrt(_half_mean(x * x, lo) + EPS)
    xh = x * r
    dxh = dy * g2
    dx = r * (dxh - xh * _half_mean(dxh * xh, lo))
    return dx, dy * xh


def _prep_a_fwd(z, gq2, gk2, gqa, gkva, cos_a, sin_a, cos_b, sin_b):
    t = z.shape[0]
    tm = _tile(t, 256)

    def body(z_ref, gq_ref, gk_ref, gqa_ref, gkva_ref, ca_ref, sa_ref, cb_ref, sb_ref,
             qa_ref, ka_ref, va_ref, cqn_ref, ckvn_ref, krr_ref):
        for j in range(4):
            cols = slice(LANES * j, LANES * (j + 1))
            y = _rope(_head_norm(z_ref[:, cols], gq_ref[...]), ca_ref[:, cols], sa_ref[:, cols], 16)
            qa_ref[:, cols] = (y * (GQA_SCALE * LOG2E)).astype(BF16)
        y = _rope(_head_norm(z_ref[:, Z_KA:Z_VA], gk_ref[...]), ca_ref[:, :LANES], sa_ref[:, :LANES], 16)
        ka_ref[...] = y.astype(BF16)
        va_ref[...] = z_ref[:, Z_VA:Z_CQ].astype(BF16)
        cq = z_ref[:, Z_CQ:Z_CKV]
        cqn_ref[...] = (cq * _rinv(cq) * gqa_ref[...]).astype(BF16)
        ckv = z_ref[:, Z_CKV:Z_KR]
        ckvn_ref[...] = (ckv * _rinv(ckv) * gkva_ref[...]).astype(BF16)
        krr_ref[...] = _rope(z_ref[:, Z_KR:Z_GATE], cb_ref[...], sb_ref[...], 8)

    return pl.pallas_call(
        body, name="prep_a_fwd", grid=(t // tm,),
        in_specs=[_row_spec(tm, Z_ATT_W), _vec_spec(LANES), _vec_spec(LANES), _vec_spec(MLA_Q_RANK),
                  _vec_spec(MLA_KV_RANK), _row_spec(tm, 512), _row_spec(tm, 512), _row_spec(tm, LANES),
                  _row_spec(tm, LANES)],
        out_specs=[_row_spec(tm, 512), _row_spec(tm, LANES), _row_spec(tm, LANES), _row_spec(tm, MLA_Q_RANK),
                   _row_spec(tm, MLA_KV_RANK), _row_spec(tm, LANES)],
        out_shape=[jax.ShapeDtypeStruct((t, 512), BF16), jax.ShapeDtypeStruct((t, LANES), BF16),
                   jax.ShapeDtypeStruct((t, LANES), BF16), jax.ShapeDtypeStruct((t, MLA_Q_RANK), BF16),
                   jax.ShapeDtypeStruct((t, MLA_KV_RANK), BF16), jax.ShapeDtypeStruct((t, LANES), F32)],
        compiler_params=_params("parallel"),
    )(z, gq2, gk2, gqa, gkva, cos_a, sin_a, cos_b, sin_b)


def _prep_a_bwd(z, dqa, dka4, dva4, dcqn, dckvn, dkr, dzga, dzgb, gq2, gk2, gqa, gkva, cos_a, sin_a):
    t = z.shape[0]
    tm = _tile(t, 256)

    def body(z_ref, dqa_ref, dka_ref, dva_ref, dcqn_ref, dckvn_ref, dkr_ref, dzga_ref, dzgb_ref, gq_ref, gk_ref,
             gqa_ref, gkva_ref, ca_ref, sa_ref, dz_ref, dgq_ref, dgk_ref, dgqa_ref, dgkva_ref):
        @pl.when(pl.program_id(0) == 0)
        def _():
            dgq_ref[...] = jnp.zeros_like(dgq_ref)
            dgk_ref[...] = jnp.zeros_like(dgk_ref)
            dgqa_ref[...] = jnp.zeros_like(dgqa_ref)
            dgkva_ref[...] = jnp.zeros_like(dgkva_ref)

        dgq = jnp.zeros((SUBLANES, LANES), F32)
        for j in range(4):
            cols = slice(LANES * j, LANES * (j + 1))
            dy = _rope_bwd(dqa_ref[:, cols] * GQA_SCALE, ca_ref[:, cols], sa_ref[:, cols], 16)
            dx, dgc = _head_norm_bwd(z_ref[:, cols], gq_ref[...], dy)
            dz_ref[:, cols] = dx.astype(BF16)
            dgq = dgq + _fold8(dgc)
        dgq_ref[...] += dgq
        dk = (dka_ref[0] + dka_ref[1] + dka_ref[2] + dka_ref[3]) * LN2
        dy = _rope_bwd(dk, ca_ref[:, :LANES], sa_ref[:, :LANES], 16)
        dx, dgc = _head_norm_bwd(z_ref[:, Z_KA:Z_VA], gk_ref[...], dy)
        dz_ref[:, Z_KA:Z_VA] = dx.astype(BF16)
        dgk_ref[...] += _fold8(dgc)
        dz_ref[:, Z_VA:Z_CQ] = (dva_ref[0] + dva_ref[1] + dva_ref[2] + dva_ref[3]).astype(BF16)
        dx, dgc = _rms_bwd_rows(z_ref[:, Z_CQ:Z_CKV], gqa_ref[...], dcqn_ref[...])
        dz_ref[:, Z_CQ:Z_CKV] = dx.astype(BF16)
        dgqa_ref[...] += _fold8(dgc)
        dx, dgc = _rms_bwd_rows(z_ref[:, Z_CKV:Z_KR], gkva_ref[...], dckvn_ref[...])
        dz_ref[:, Z_CKV:Z_KR] = dx.astype(BF16)
        dgkva_ref[...] += _fold8(dgc)
        dz_ref[:, Z_KR:Z_GATE] = dkr_ref[...].astype(BF16)
        dz_ref[:, Z_GATE:Z_GATE + D_MODEL] = dzga_ref[...]
        dz_ref[:, Z_GATE + D_MODEL:Z_W] = dzgb_ref[...]

    part = pl.BlockSpec((4, tm, LANES), lambda i: (0, i, 0))
    return pl.pallas_call(
        body, name="prep_a_bwd", grid=(t // tm,),
        in_specs=[_row_spec(tm, Z_ATT_W), _row_spec(tm, 512), part, part, _row_spec(tm, MLA_Q_RANK),
                  _row_spec(tm, MLA_KV_RANK), _row_spec(tm, LANES), _row_spec(tm, D_MODEL), _row_spec(tm, D_MODEL),
                  _vec_spec(LANES),
                  _vec_spec(LANES), _vec_spec(MLA_Q_RANK), _vec_spec(MLA_KV_RANK), _row_spec(tm, 512),
                  _row_spec(tm, 512)],
        out_specs=[_row_spec(tm, Z_W), _acc_spec(LANES), _acc_spec(LANES), _acc_spec(MLA_Q_RANK),
                   _acc_spec(MLA_KV_RANK)],
        out_shape=[jax.ShapeDtypeStruct((t, Z_W), BF16), jax.ShapeDtypeStruct((SUBLANES, LANES), F32),
                   jax.ShapeDtypeStruct((SUBLANES, LANES), F32), jax.ShapeDtypeStruct((SUBLANES, MLA_Q_RANK), F32),
                   jax.ShapeDtypeStruct((SUBLANES, MLA_KV_RANK), F32)],
        compiler_params=_params("arbitrary"),
    )(z, dqa, dka4, dva4, dcqn, dckvn, dkr, dzga, dzgb, gq2, gk2, gqa, gkva, cos_a, sin_a)


def _prep_b_fwd(qb, kvb, krr, cos_b, sin_b):
    t = qb.shape[0]
    tm = _tile(t, 256)

    def body(qb_ref, kvb_ref, krr_ref, cb_ref, sb_ref, q_ref, k_ref, v_ref):
        for h in range(MLA_HEADS):
            cols = slice(LANES * h, LANES * (h + 1))
            q_ref[:, cols] = (_rope(qb_ref[:, cols], cb_ref[...], sb_ref[...], 8) * (MLA_SCALE * LOG2E)).astype(BF16)
            k_ref[:, cols] = (kvb_ref[:, cols] + krr_ref[...]).astype(BF16)
        v_ref[...] = kvb_ref[:, 1024:1536].astype(BF16)

    return pl.pallas_call(
        body, name="prep_b_fwd", grid=(t // tm,),
        in_specs=[_row_spec(tm, 1024), _row_spec(tm, 1536), _row_spec(tm, LANES), _row_spec(tm, LANES),
                  _row_spec(tm, LANES)],
        out_specs=[_row_spec(tm, 1024), _row_spec(tm, 1024), _row_spec(tm, 512)],
        out_shape=[jax.ShapeDtypeStruct((t, 1024), BF16), jax.ShapeDtypeStruct((t, 1024), BF16),
                   jax.ShapeDtypeStruct((t, 512), BF16)],
        compiler_params=_params("parallel"),
    )(qb, kvb, krr, cos_b, sin_b)


def _prep_b_bwd(dq, dk, dv, cos_b, sin_b):
    t = dq.shape[0]
    tm = _tile(t, 256)

    def body(dq_ref, dk_ref, dv_ref, cb_ref, sb_ref, dqb_ref, dkvb_ref, dkr_ref):
        dkr = jnp.zeros((tm, LANES), F32)
        for h in range(MLA_HEADS):
            cols = slice(LANES * h, LANES * (h + 1))
            dqb_ref[:, cols] = _rope_bwd(dq_ref[:, cols] * MLA_SCALE, cb_ref[...], sb_ref[...], 8).astype(BF16)
            dkh = dk_ref[:, cols] * LN2
            dkvb_ref[:, cols] = dkh.astype(BF16)
            dkr = dkr + dkh
        dkvb_ref[:, 1024:1536] = dv_ref[...].astype(BF16)
        dkr_ref[...] = _rope_bwd(dkr, cb_ref[...], sb_ref[...], 8)

    return pl.pallas_call(
        body, name="prep_b_bwd", grid=(t // tm,),
        in_specs=[_row_spec(tm, 1024), _row_spec(tm, 1024), _row_spec(tm, 512), _row_spec(tm, LANES),
                  _row_spec(tm, LANES)],
        out_specs=[_row_spec(tm, 1024), _row_spec(tm, 1536), _row_spec(tm, LANES)],
        out_shape=[jax.ShapeDtypeStruct((t, 1024), BF16), jax.ShapeDtypeStruct((t, 1536), BF16),
                   jax.ShapeDtypeStruct((t, LANES), F32)],
        compiler_params=_params("parallel"),
    )(dq, dk, dv, cos_b, sin_b)


_NT = (((1,), (1,)), ((), ()))
_NN = (((1,), (0,)), ((), ()))
_TN = (((0,), (0,)), ((), ()))


def _head_operands(qv, kv, i, shared_k):
    if shared_k:
        lo = _lo_mask(qv.shape)
        keep = lo if i == 0 else jnp.logical_not(lo)
        return jnp.where(keep, qv, jnp.zeros_like(qv)), kv
    cols = slice(LANES * i, LANES * (i + 1))
    return qv[:, cols], kv[:, cols]


def _attn_specs(shared_k, tq, tk, q_of, k_of):
    wq = LANES if shared_k else 2 * LANES
    q_spec = pl.BlockSpec((tq, wq), lambda *g: (q_of(*g), g[0]))
    if shared_k:
        k_spec = pl.BlockSpec((tk, LANES), lambda *g: (k_of(*g), 0))
        v_spec = pl.BlockSpec((tk, LANES), lambda *g: (k_of(*g), 0))
    else:
        k_spec = pl.BlockSpec((tk, wq), lambda *g: (k_of(*g), g[0]))
        v_spec = pl.BlockSpec((tk, LANES), lambda *g: (k_of(*g), g[0]))
    return wq, q_spec, k_spec, v_spec


def _attn_fwd(q, k, v, shared_k, name):
    t = q.shape[0]
    tq = tk = _tile(t, 512)
    nq, nk = t // tq, t // tk
    wq, q_spec, k_spec, v_spec = _attn_specs(shared_k, tq, tk, lambda p, i, j: i, lambda p, i, j: j)
    groups = q.shape[1] // wq

    def body(q_ref, k_ref, v_ref, o_ref, lse_ref, m_s, l_s, acc_s):
        kb = pl.program_id(2)

        @pl.when(kb == 0)
        def _():
            m_s[...] = jnp.full_like(m_s, -jnp.inf)
            l_s[...] = jnp.zeros_like(l_s)
            acc_s[...] = jnp.zeros_like(acc_s)

        qv, kv, vv = q_ref[...], k_ref[...], v_ref[...]
        for i in range(2):
            qi, ki = _head_operands(qv, kv, i, shared_k)
            st = lax.dot_general(ki, qi, _NT, preferred_element_type=F32)
            m_prev = m_s[i]
            m_new = jnp.maximum(m_prev, jnp.max(st, axis=0, keepdims=True))
            alpha = jnp.exp2(m_prev - m_new)
            pt = jnp.exp2(st - m_new)
            l_s[i] = alpha * l_s[i] + jnp.sum(pt, axis=0, keepdims=True)
            acc_s[i] = alpha * acc_s[i] + lax.dot_general(vv, pt.astype(BF16), _TN, preferred_element_type=F32)
            m_s[i] = m_new

        @pl.when(kb == nk - 1)
        def _():
            o0 = acc_s[0] / l_s[0]
            o1 = acc_s[1] / l_s[1]
            row_lo = lax.broadcasted_iota(jnp.int32, o0.shape, 0) < HEAD_DIM
            o_ref[...] = jnp.where(row_lo, o0, o1).T.astype(BF16)
            lse_ref[0] = m_s[0] + jnp.log2(l_s[0])
            lse_ref[1] = m_s[1] + jnp.log2(l_s[1])

    return pl.pallas_call(
        body, name=name, grid=(groups, nq, nk),
        in_specs=[q_spec, k_spec, v_spec],
        out_specs=[pl.BlockSpec((tq, LANES), lambda p, i, j: (i, p)),
                   pl.BlockSpec((2, 1, tq), lambda p, i, j: (p, 0, i))],
        out_shape=[jax.ShapeDtypeStruct((t, LANES * groups), BF16),
                   jax.ShapeDtypeStruct((2 * groups, 1, t), F32)],
        scratch_shapes=[pltpu.VMEM((2, 1, tq), F32), pltpu.VMEM((2, 1, tq), F32), pltpu.VMEM((2, LANES, tq), F32)],
        compiler_params=_params("parallel", "parallel", "arbitrary"),
    )(q, k, v)


def _attn_delta(do, o):
    t, w = do.shape
    tm = _tile(t, 512)
    groups = w // LANES

    def body(do_ref, o_ref, delta_ref, dob_ref):
        dov = do_ref[...]
        dob_ref[...] = dov.astype(BF16)
        prod = dov * o_ref[...].astype(F32)
        lane_lo = lax.broadcasted_iota(jnp.int32, (SUBLANES, LANES), 1) < HEAD_DIM
        masks = (lane_lo.astype(BF16), jnp.logical_not(lane_lo).astype(BF16))
        for g in range(groups):
            x = prod[:, LANES * g:LANES * (g + 1)]
            hi = x.astype(BF16)
            mid = (x - hi.astype(F32)).astype(BF16)
            for i in range(2):
                r = (lax.dot_general(masks[i], hi, _NT, preferred_element_type=F32)
                     + lax.dot_general(masks[i], mid, _NT, preferred_element_type=F32))
                delta_ref[2 * g + i] = r[0:1, :]

    return pl.pallas_call(
        body, name="attn_delta", grid=(t // tm,),
        in_specs=[_row_spec(tm, w), _row_spec(tm, w)],
        out_specs=[pl.BlockSpec((2 * groups, 1, tm), lambda i: (0, 0, i)), _row_spec(tm, w)],
        out_shape=[jax.ShapeDtypeStruct((2 * groups, 1, t), F32), jax.ShapeDtypeStruct((t, w), BF16)],
        compiler_params=_params("parallel"),
    )(do, o)


def _attn_bwd(q, k, v, do, lse, delta, shared_k, name):
    t = q.shape[0]
    tq = tk = _tile(t, 512)
    nq, nk = t // tq, t // tk
    wq, q_spec, k_spec, v_spec = _attn_specs(shared_k, tq, tk, lambda p, j, i: i, lambda p, j, i: j)
    groups = q.shape[1] // wq

    def body(q_ref, k_ref, v_ref, do_ref, lse_ref, delta_ref, dq_ref, dk_ref, dv_ref, dk_s, dv_s):
        kb, qb = pl.program_id(1), pl.program_id(2)

        @pl.when(qb == 0)
        def _():
            dk_s[...] = jnp.zeros_like(dk_s)
            dv_s[...] = jnp.zeros_like(dv_s)

        qv, kv, vv, dov = q_ref[...], k_ref[...], v_ref[...], do_ref[...]
        lo = _lo_mask(dov.shape)
        dq_parts = []
        for i in range(2):
            qi, ki = _head_operands(qv, kv, i, shared_k)
            keep = lo if i == 0 else jnp.logical_not(lo)
            doi = jnp.where(keep, dov, jnp.zeros_like(dov))
            st = lax.dot_general(ki, qi, _NT, preferred_element_type=F32)
            pt = jnp.exp2(st - lse_ref[i])
            dpt = lax.dot_general(vv, doi, _NT, preferred_element_type=F32)
            dst = (pt * (dpt - delta_ref[i])).astype(BF16)
            dv_s[...] += lax.dot_general(pt.astype(BF16), doi, _NN, preferred_element_type=F32)
            dk_i = lax.dot_general(dst, qi, _NN, preferred_element_type=F32)
            if shared_k:
                dk_s[...] += dk_i
            else:
                dk_s[:, LANES * i:LANES * (i + 1)] += dk_i
            dq_parts.append(lax.dot_general(dst, ki, _TN, preferred_element_type=F32))
        rows = pl.ds(pl.multiple_of(qb * tq, tq), tq)
        if shared_k:
            tiles = [(slice(0, LANES), jnp.where(lo, dq_parts[0], dq_parts[1]))]
        else:
            tiles = [(slice(0, LANES), dq_parts[0]), (slice(LANES, 2 * LANES), dq_parts[1])]
        for cols, val in tiles:
            @pl.when(kb == 0)
            def _(cols=cols, val=val):
                dq_ref[rows, cols] = val

            @pl.when(kb > 0)
            def _(cols=cols, val=val):
                dq_ref[rows, cols] += val

        @pl.when(qb == nq - 1)
        def _():
            if shared_k:
                dk_ref[0] = dk_s[...]
                dv_ref[0] = dv_s[...]
            else:
                dk_ref[...] = dk_s[...]
                dv_ref[...] = dv_s[...]

    stat_spec = pl.BlockSpec((2, 1, tq), lambda p, j, i: (p, 0, i))
    do_spec = pl.BlockSpec((tq, LANES), lambda p, j, i: (i, p))
    dq_spec = pl.BlockSpec((t, wq), lambda p, j, i: (0, p))
    if shared_k:
        dk_spec = pl.BlockSpec((1, tk, LANES), lambda p, j, i: (p, j, 0))
        dv_spec = dk_spec
        dk_shape = jax.ShapeDtypeStruct((groups, t, LANES), F32)
        dv_shape = dk_shape
    else:
        dk_spec = pl.BlockSpec((tk, wq), lambda p, j, i: (j, p))
        dv_spec = pl.BlockSpec((tk, LANES), lambda p, j, i: (j, p))
        dk_shape = jax.ShapeDtypeStruct((t, wq * groups), F32)
        dv_shape = jax.ShapeDtypeStruct((t, LANES * groups), F32)
    return pl.pallas_call(
        body, name=name, grid=(groups, nk, nq),
        in_specs=[q_spec, k_spec, v_spec, do_spec, stat_spec, stat_spec],
        out_specs=[dq_spec, dk_spec, dv_spec],
        out_shape=[jax.ShapeDtypeStruct((t, wq * groups), F32), dk_shape, dv_shape],
        scratch_shapes=[pltpu.VMEM((tk, wq), F32), pltpu.VMEM((tk, LANES), F32)],
        compiler_params=_params("parallel", "arbitrary", "arbitrary"),
    )(q, k, v, do, lse, delta)


_MERGE_W = 512
_GATE_BLK0 = Z_GATE // _MERGE_W


def _merge_fwd(z, b_gate, ta, tb):
    t = z.shape[0]
    tm = _tile(t, 512)
    w = _MERGE_W
    nj = D_MODEL // w

    def body(za_ref, zb_ref, ba_ref, bb_ref, ta_ref, tb_ref, o_ref):
        ga = jax.nn.sigmoid(za_ref[...] + ba_ref[...])
        gb = jax.nn.sigmoid(zb_ref[...] + bb_ref[...])
        o_ref[...] = (ga * ta_ref[...] + gb * tb_ref[...]).astype(BF16)

    return pl.pallas_call(
        body, name="merge_fwd", grid=(t // tm, nj),
        in_specs=[pl.BlockSpec((tm, w), lambda i, j: (i, _GATE_BLK0 + j)),
                  pl.BlockSpec((tm, w), lambda i, j: (i, _GATE_BLK0 + nj + j)),
                  pl.BlockSpec((1, w), lambda i, j: (0, j)),
                  pl.BlockSpec((1, w), lambda i, j: (0, nj + j)),
                  pl.BlockSpec((tm, w), lambda i, j: (i, j)),
                  pl.BlockSpec((tm, w), lambda i, j: (i, j))],
        out_specs=pl.BlockSpec((tm, w), lambda i, j: (i, j)),
        out_shape=jax.ShapeDtypeStruct((t, D_MODEL), BF16),
        compiler_params=_params("parallel", "parallel"),
    )(z, z, b_gate, b_gate, ta, tb)


def _merge_bwd(dmg, z, b_gate, ta, tb):
    t = z.shape[0]
    tm = _tile(t, 512)
    w = _MERGE_W
    nj = D_MODEL // w

    def body(dm_ref, za_ref, zb_ref, ba_ref, bb_ref, ta_ref, tb_ref, dta_ref, dtb_ref, dza_ref, dzb_ref,
             dba_ref, dbb_ref):
        dm = dm_ref[...]
        ga = jax.nn.sigmoid(za_ref[...] + ba_ref[...])
        gb = jax.nn.sigmoid(zb_ref[...] + bb_ref[...])
        dta_ref[...] = (dm * ga).astype(BF16)
        dtb_ref[...] = (dm * gb).astype(BF16)
        dza = dm * ta_ref[...] * ga * (1.0 - ga)
        dzb = dm * tb_ref[...] * gb * (1.0 - gb)
        dza_ref[...] = dza.astype(BF16)
        dzb_ref[...] = dzb.astype(BF16)

        @pl.when(pl.program_id(1) == 0)
        def _():
            dba_ref[...] = jnp.zeros_like(dba_ref)
            dbb_ref[...] = jnp.zeros_like(dbb_ref)

        dba_ref[...] += _fold8(dza)
        dbb_ref[...] += _fold8(dzb)

    blk = pl.BlockSpec((tm, w), lambda j, i: (i, j))
    acc = pl.BlockSpec((SUBLANES, w), lambda j, i: (0, j))
    return pl.pallas_call(
        body, name="merge_bwd", grid=(nj, t // tm),
        in_specs=[blk,
                  pl.BlockSpec((tm, w), lambda j, i: (i, _GATE_BLK0 + j)),
                  pl.BlockSpec((tm, w), lambda j, i: (i, _GATE_BLK0 + nj + j)),
                  pl.BlockSpec((1, w), lambda j, i: (0, j)),
                  pl.BlockSpec((1, w), lambda j, i: (0, nj + j)),
                  blk, blk],
        out_specs=[blk, blk, blk, blk, acc, acc],
        out_shape=[jax.ShapeDtypeStruct((t, D_MODEL), BF16)] * 4 + [jax.ShapeDtypeStruct((SUBLANES, D_MODEL), F32)] * 2,
        compiler_params=_params("parallel", "arbitrary"),
    )(dmg, z, z, b_gate, b_gate, ta, tb)


def _relu2_fwd(h):
    t, f = h.shape
    tm, tn = _tile(t, 512), _tile(f, 1024)

    def body(h_ref, a_ref):
        r = jnp.maximum(h_ref[...], 0.0)
        a_ref[...] = (r * r).astype(BF16)

    spec = pl.BlockSpec((tm, tn), lambda i, j: (i, j))
    return pl.pallas_call(
        body, name="relu2_fwd", grid=(t // tm, f // tn), in_specs=[spec], out_specs=spec,
        out_shape=jax.ShapeDtypeStruct((t, f), BF16), compiler_params=_params("parallel", "parallel"),
    )(h)


def _relu2_bwd(da, h):
    t, f = h.shape
    tm, tn = _tile(t, 512), _tile(f, 1024)

    def body(da_ref, h_ref, dh_ref):
        dh_ref[...] = (da_ref[...] * (2.0 * jnp.maximum(h_ref[...], 0.0))).astype(BF16)

    spec = pl.BlockSpec((tm, tn), lambda i, j: (i, j))
    return pl.pallas_call(
        body, name="relu2_bwd", grid=(t // tm, f // tn), in_specs=[spec, spec], out_specs=spec,
        out_shape=jax.ShapeDtypeStruct((t, f), BF16), compiler_params=_params("parallel", "parallel"),
    )(da, h)


def _loss_grad(y, target):
    t, d = y.shape
    tm = _tile(t, 512)

    def body(y_ref, t_ref, dy_ref, acc_ref):
        err = y_ref[...] - t_ref[...]
        dy_ref[...] = err * (1.0 / d)
        e8 = _fold8(err * err)
        part = e8[:, 0:LANES]
        for c in range(1, d // LANES):
            part = part + e8[:, LANES * c:LANES * (c + 1)]

        @pl.when(pl.program_id(0) == 0)
        def _():
            acc_ref[...] = jnp.zeros_like(acc_ref)

        acc_ref[...] += part

    return pl.pallas_call(
        body, name="loss_grad", grid=(t // tm,),
        in_specs=[_row_spec(tm, d), _row_spec(tm, d)],
        out_specs=[_row_spec(tm, d), _acc_spec(LANES)],
        out_shape=[jax.ShapeDtypeStruct((t, d), F32), jax.ShapeDtypeStruct((SUBLANES, LANES), F32)],
        compiler_params=_params("arbitrary"),
    )(y, target)


_MESH_ID = pl.DeviceIdType.MESH
_ANY = pl.BlockSpec(memory_space=pl.ANY)


def _all_gather(x):
    r, c_ = x.shape

    def body(x_ref, out_ref, send_sems, recv_sems, local_sem):
        mx, my, mc = lax.axis_index("x"), lax.axis_index("y"), lax.axis_index("c")
        me, sibling = (mx, my, mc), (mx, my, 1 - mc)
        chips = [(1 - mx, my), (mx, 1 - my), (1 - mx, 1 - my)]

        def slot(px, py, pc):
            return out_ref.at[4 * px + 2 * py + pc]

        def copy(sem, block, to, src=None):
            return pltpu.make_async_remote_copy(
                src_ref=slot(*block) if src is None else src, dst_ref=slot(*block),
                send_sem=send_sems.at[sem], recv_sem=recv_sems.at[sem], device_id=to, device_id_type=_MESH_ID)

        mine = pltpu.make_async_copy(x_ref, slot(*me), local_sem)
        mine.start()
        first = [copy(0, me, sibling, src=x_ref)]
        first += [copy(1 + j, me, (*chip, mc), src=x_ref) for j, chip in enumerate(chips)]
        for cp in first:
            cp.start()
        passed = [copy(4 + j, (*chip, mc), sibling) for j, chip in enumerate(chips)]
        for j, chip in enumerate(chips):
            copy(1 + j, (*chip, mc), me).wait_recv()
            passed[j].start()
        copy(0, sibling, me).wait_recv()
        for j, chip in enumerate(chips):
            copy(4 + j, (*chip, 1 - mc), me).wait_recv()
        for cp in first + passed:
            cp.wait_send()
        mine.wait()

    return pl.pallas_call(
        body, name="weight_all_gather",
        out_shape=jax.ShapeDtypeStruct((N_DEV, r, c_), x.dtype),
        in_specs=[_ANY], out_specs=_ANY,
        scratch_shapes=[pltpu.SemaphoreType.DMA((7,)), pltpu.SemaphoreType.DMA((7,)), pltpu.SemaphoreType.DMA],
    )(x)


def _all_to_all(send):
    _, r, c_ = send.shape

    def body(s_ref, r_ref, send_sems, recv_sems, local_sem):
        mx, my, mc = lax.axis_index("x"), lax.axis_index("y"), lax.axis_index("c")
        me = 4 * mx + 2 * my + mc
        local = pltpu.make_async_copy(s_ref.at[me], r_ref.at[me], local_sem)
        local.start()
        copies = []
        for rel in range(1, N_DEV):
            px = 1 - mx if rel & 4 else mx
            py = 1 - my if rel & 2 else my
            pc = 1 - mc if rel & 1 else mc
            peer = 4 * px + 2 * py + pc
            cp = pltpu.make_async_remote_copy(
                src_ref=s_ref.at[peer], dst_ref=r_ref.at[me], send_sem=send_sems.at[rel - 1],
                recv_sem=recv_sems.at[rel - 1], device_id=(px, py, pc), device_id_type=_MESH_ID)
            cp.start()
            copies.append(cp)
        for cp in copies:
            cp.wait_send()
            cp.wait_recv()
        local.wait()

    return pl.pallas_call(
        body, name="grad_all_to_all",
        out_shape=jax.ShapeDtypeStruct(send.shape, send.dtype),
        in_specs=[_ANY], out_specs=_ANY,
        scratch_shapes=[pltpu.SemaphoreType.DMA((7,)), pltpu.SemaphoreType.DMA((7,)), pltpu.SemaphoreType.DMA],
    )(send)


def _adamw(recv, w, m, v):
    r = w.shape[0]
    tr = _tile(r, PACK_ROW_TILE)

    def body(g_ref, w_ref, m_ref, v_ref, go_ref, d_ref, mo_ref, vo_ref):
        g = g_ref[0]
        for s in range(1, N_DEV):
            g = g + g_ref[s]
        go_ref[...] = g
        mn = ADAM_B1 * m_ref[...] + (1.0 - ADAM_B1) * g
        vn = ADAM_B2 * v_ref[...] + (1.0 - ADAM_B2) * (g * g)
        mo_ref[...] = mn
        vo_ref[...] = vn
        m_hat = mn / (1.0 - ADAM_B1 ** ADAM_STEP)
        v_hat = vn / (1.0 - ADAM_B2 ** ADAM_STEP)
        d_ref[...] = -ADAM_LR * (m_hat / (jnp.sqrt(v_hat) + ADAM_EPS) + ADAM_WD * w_ref[...])

    spec = pl.BlockSpec((tr, LANES), lambda i: (i, 0))
    out = jax.ShapeDtypeStruct((r, LANES), F32)
    return pl.pallas_call(
        body, name="grad_sum_adamw", grid=(r // tr,),
        in_specs=[pl.BlockSpec((N_DEV, tr, LANES), lambda i: (0, i, 0)), spec, spec, spec],
        out_specs=[spec, spec, spec, spec], out_shape=[out, out, out, out],
        compiler_params=_params("parallel"),
    )(recv, w, m, v)


def _pad_cols(a, before, after):
    parts = []
    if before:
        parts.append(jnp.zeros(a.shape[:-1] + (before,), a.dtype))
    parts.append(a)
    if after:
        parts.append(jnp.zeros(a.shape[:-1] + (after,), a.dtype))
    return jnp.concatenate(parts, axis=-1)


def _q_head_pairs(a, axis):
    shp = a.shape
    a = a.reshape(shp[:axis] + (GQA_KV_HEADS, GQA_GROUP, HEAD_DIM) + shp[axis + 1:])
    a = jnp.swapaxes(a, axis, axis + 1)
    return a.reshape(shp)


def _q_head_unpairs(a, axis):
    shp = a.shape
    a = a.reshape(shp[:axis] + (GQA_GROUP, GQA_KV_HEADS, HEAD_DIM) + shp[axis + 1:])
    a = jnp.swapaxes(a, axis, axis + 1)
    return a.reshape(shp)


def _layout_weights(w):
    w_in = w["w_in"]
    lead = w_in.shape[:-1]
    w_in_p = jnp.concatenate([
        _q_head_pairs(w_in[..., 0:512], w_in.ndim - 1),
        w_in[..., 512:1408],
        _pad_cols(w_in[..., 1408:1440], KR_LANE0, LANES - KR_LANE0 - MLA_ROPE_DIM),
        w_in[..., 1440:],
    ], axis=-1)
    wq = w["w_q_up"]
    wq_p = _pad_cols(wq.reshape(wq.shape[:-1] + (MLA_HEADS, MLA_QK_DIM)), 0, LANES - MLA_QK_DIM)
    wq_p = wq_p.reshape(wq.shape[:-1] + (MLA_HEADS * LANES,))
    wkv = w["w_kv_up"]
    wkv4 = wkv.reshape(wkv.shape[:-1] + (MLA_HEADS, 2 * HEAD_DIM))
    wk_p = _pad_cols(wkv4[..., :HEAD_DIM], 0, LANES - HEAD_DIM).reshape(wkv.shape[:-1] + (MLA_HEADS * LANES,))
    wv_p = wkv4[..., HEAD_DIM:].reshape(wkv.shape[:-1] + (MLA_HEADS * HEAD_DIM,))
    del lead
    return {
        "w_in": w_in_p, "w_q_up": wq_p, "w_kv_up": jnp.concatenate([wk_p, wv_p], axis=-1),
        "w_branch_a": _q_head_pairs(w["w_branch_a"], w["w_branch_a"].ndim - 2), "w_branch_b": w["w_branch_b"],
        "w_o": w["w_o"], "w_ffn_up": w["w_ffn_up"], "w_ffn_down": w["w_ffn_down"],
    }


def _unlayout_grads(g):
    gi = g["w_in"]
    kr0 = Z_KR + KR_LANE0
    g_in = jnp.concatenate([
        _q_head_unpairs(gi[..., 0:512], gi.ndim - 1), gi[..., 512:1408], gi[..., kr0:kr0 + MLA_ROPE_DIM],
        gi[..., Z_GATE:],
    ], axis=-1)
    gq = g["w_q_up"]
    gq = gq.reshape(gq.shape[:-1] + (MLA_HEADS, LANES))[..., :MLA_QK_DIM]
    gq = gq.reshape(gq.shape[:-2] + (MLA_HEADS * MLA_QK_DIM,))
    gkv = g["w_kv_up"]
    gk = gkv[..., :MLA_HEADS * LANES].reshape(gkv.shape[:-1] + (MLA_HEADS, LANES))[..., :HEAD_DIM]
    gv = gkv[..., MLA_HEADS * LANES:].reshape(gkv.shape[:-1] + (MLA_HEADS, HEAD_DIM))
    gkv = jnp.concatenate([gk, gv], axis=-1).reshape(gkv.shape[:-1] + (MLA_HEADS * 2 * HEAD_DIM,))
    return {
        "w_in": g_in, "w_q_up": gq, "w_kv_up": gkv,
        "w_branch_a": _q_head_unpairs(g["w_branch_a"], g["w_branch_a"].ndim - 2), "w_branch_b": g["w_branch_b"],
        "w_o": g["w_o"], "w_ffn_up": g["w_ffn_up"], "w_ffn_down": g["w_ffn_down"],
    }


def _pack_rows(flat_parts, lead=()):
    n = sum(p.shape[-1] for p in flat_parts)
    per = PACK_ROW_TILE * LANES
    pad = (-n) % per
    if pad:
        flat_parts = list(flat_parts) + [jnp.zeros(lead + (pad,), flat_parts[0].dtype)]
    flat = jnp.concatenate(flat_parts, axis=-1)
    return flat.reshape(lead + ((n + pad) // LANES, LANES))


def _unpack_rows(packed, shapes):
    flat = packed.reshape(-1)
    out, off = [], 0
    for shp in shapes:
        n = int(np.prod(shp))
        out.append(flat[off:off + n].reshape(shp))
        off += n
    return out


def _shards_of(full, axis):
    shp = full.shape
    cut = shp[:axis] + (N_DEV, shp[axis] // N_DEV) + shp[axis + 1:]
    return jnp.moveaxis(full.reshape(cut), axis, 0).reshape(N_DEV, -1)


def _from_shards(flat, shard_shape, axis):
    parts = jnp.moveaxis(flat.reshape((N_DEV,) + tuple(shard_shape)), 0, axis)
    full = list(shard_shape)
    full[axis] *= N_DEV
    return parts.reshape(full)


def _layer_fwd(x, u, lw, tabs):
    cos_a, sin_a, cos_b, sin_b = tabs
    z = _matmul(u, lw["w_in"], "nn", "mm_in")
    qa, ka, va, cqn, ckvn, krr = _prep_a_fwd(z, lw["gq2"], lw["gk2"], lw["gqa"], lw["gkva"], cos_a, sin_a, cos_b, sin_b)
    qb = _matmul(cqn, lw["w_q_up"], "nn", "mm_q_up")
    kvb = _matmul(ckvn, lw["w_kv_up"], "nn", "mm_kv_up")
    q_b, k_b, v_b = _prep_b_fwd(qb, kvb, krr, cos_b, sin_b)
    ya, lse_a = _attn_fwd(qa, ka, va, True, "gqa_fwd")
    yb, lse_b = _attn_fwd(q_b, k_b, v_b, False, "mla_fwd")
    ta = _matmul(ya, lw["w_branch_a"], "nn", "mm_branch_a")
    tb = _matmul(yb, lw["w_branch_b"], "nn", "mm_branch_b")
    merged = _merge_fwd(z, lw["b_gate"], ta, tb)
    m = _matmul(merged, lw["w_o"], "nn", "mm_o")
    x2, u2 = _res_norm_fwd(x, m, lw["post_mix_g"], lw["pre_ffn_g"])
    h = _matmul(u2, lw["w_ffn_up"], "nn", "mm_ffn_up")
    a = _relu2_fwd(h)
    f = _matmul(a, lw["w_ffn_down"], "nn", "mm_ffn_down")
    x3, u_next = _res_norm_fwd(x2, f, lw["post_ffn_g"], lw["next_pre_mix_g"])
    saved = dict(u=u, z=z, qa=qa, ka=ka, va=va, cqn=cqn, ckvn=ckvn, q_b=q_b, k_b=k_b, v_b=v_b, ya=ya, yb=yb,
                 lse_a=lse_a, lse_b=lse_b, ta=ta, tb=tb, merged=merged, m=m, x2=x2, u2=u2, h=h, a=a, f=f, x3=x3)
    return x3, u_next, saved


def _layer_bwd(dx3, du_next, lw, sv, tabs):
    cos_a, sin_a, cos_b, sin_b = tabs
    g = {}
    dx3, df, dg4, dg1n = _res_norm_bwd(sv["x3"], sv["f"], lw["post_ffn_g"], lw["next_pre_mix_g"], dx3, du_next)
    g["post_ffn_g"], g["next_pre_mix_g"] = dg4, dg1n
    da = _matmul(df, lw["w_ffn_down"], "nt", "mm_d_a")
    g["w_ffn_down"] = _matmul(sv["a"], df, "tn", "mm_dw_ffn_down")
    dh = _relu2_bwd(da, sv["h"])
    du2 = _matmul(dh, lw["w_ffn_up"], "nt", "mm_d_u2")
    g["w_ffn_up"] = _matmul(sv["u2"], dh, "tn", "mm_dw_ffn_up")
    dx2, dm, dg2, dg3 = _res_norm_bwd(sv["x2"], sv["m"], lw["post_mix_g"], lw["pre_ffn_g"], dx3, du2)
    g["post_mix_g"], g["pre_ffn_g"] = dg2, dg3
    dmg = _matmul(dm, lw["w_o"], "nt", "mm_d_merged")
    g["w_o"] = _matmul(sv["merged"], dm, "tn", "mm_dw_o")
    dta, dtb, dzg_a, dzg_b, db_a, db_b = _merge_bwd(dmg, sv["z"], lw["b_gate"], sv["ta"], sv["tb"])
    g["b_gate"] = jnp.concatenate([db_a, db_b], axis=-1)
    dya = _matmul(dta, lw["w_branch_a"], "nt", "mm_d_ya")
    g["w_branch_a"] = _matmul(sv["ya"], dta, "tn", "mm_dw_branch_a")
    dyb = _matmul(dtb, lw["w_branch_b"], "nt", "mm_d_yb")
    g["w_branch_b"] = _matmul(sv["yb"], dtb, "tn", "mm_dw_branch_b")
    delta_a, dya16 = _attn_delta(dya, sv["ya"])
    delta_b, dyb16 = _attn_delta(dyb, sv["yb"])
    dqa, dka4, dva4 = _attn_bwd(sv["qa"], sv["ka"], sv["va"], dya16, sv["lse_a"], delta_a, True, "gqa_bwd")
    dq_b, dk_b, dv_b = _attn_bwd(sv["q_b"], sv["k_b"], sv["v_b"], dyb16, sv["lse_b"], delta_b, False, "mla_bwd")
    dqb, dkvb, dkr = _prep_b_bwd(dq_b, dk_b, dv_b, cos_b, sin_b)
    dcqn = _matmul(dqb, lw["w_q_up"], "nt", "mm_d_cqn")
    g["w_q_up"] = _matmul(sv["cqn"], dqb, "tn", "mm_dw_q_up")
    dckvn = _matmul(dkvb, lw["w_kv_up"], "nt", "mm_d_ckvn")
    g["w_kv_up"] = _matmul(sv["ckvn"], dkvb, "tn", "mm_dw_kv_up")
    dz, dgq, dgk, dgqa, dgkva = _prep_a_bwd(sv["z"], dqa, dka4, dva4, dcqn, dckvn, dkr, dzg_a, dzg_b, lw["gq2"],
                                            lw["gk2"], lw["gqa"], lw["gkva"], cos_a, sin_a)
    g["q_norm_g"], g["k_norm_g"], g["q_a_norm_g"], g["kv_a_norm_g"] = dgq, dgk, dgqa, dgkva
    du = _matmul(dz, lw["w_in"], "nt", "mm_d_u")
    g["w_in"] = _matmul(sv["u"], dz, "tn", "mm_dw_in")
    return dx2, du, g


def kernel(x, w_in, b_gate, q_norm_g, k_norm_g, q_a_norm_g, kv_a_norm_g, w_q_up, w_kv_up, w_branch_a, w_branch_b, w_o, w_ffn_up, w_ffn_down, pre_mix_g, post_mix_g, pre_ffn_g, post_ffn_g, loss_target, m_w_in, m_b_gate, m_q_norm_g, m_k_norm_g, m_q_a_norm_g, m_kv_a_norm_g, m_w_q_up, m_w_kv_up, m_w_branch_a, m_w_branch_b, m_w_o, m_w_ffn_up, m_w_ffn_down, m_pre_mix_g, m_post_mix_g, m_pre_ffn_g, m_post_ffn_g, v_w_in, v_b_gate, v_q_norm_g, v_k_norm_g, v_q_a_norm_g, v_kv_a_norm_g, v_w_q_up, v_w_kv_up, v_w_branch_a, v_w_branch_b, v_w_o, v_w_ffn_up, v_w_ffn_down, v_pre_mix_g, v_post_mix_g, v_pre_ffn_g, v_post_ffn_g):
    weights = dict(zip(WEIGHT_NAMES, (w_in, b_gate, q_norm_g, k_norm_g, q_a_norm_g, kv_a_norm_g, w_q_up, w_kv_up,
                                      w_branch_a, w_branch_b, w_o, w_ffn_up, w_ffn_down, pre_mix_g, post_mix_g,
                                      pre_ffn_g, post_ffn_g)))
    mom_m = dict(zip(WEIGHT_NAMES, (m_w_in, m_b_gate, m_q_norm_g, m_k_norm_g, m_q_a_norm_g, m_kv_a_norm_g, m_w_q_up,
                                    m_w_kv_up, m_w_branch_a, m_w_branch_b, m_w_o, m_w_ffn_up, m_w_ffn_down,
                                    m_pre_mix_g, m_post_mix_g, m_pre_ffn_g, m_post_ffn_g)))
    mom_v = dict(zip(WEIGHT_NAMES, (v_w_in, v_b_gate, v_q_norm_g, v_k_norm_g, v_q_a_norm_g, v_kv_a_norm_g, v_w_q_up,
                                    v_w_kv_up, v_w_branch_a, v_w_branch_b, v_w_o, v_w_ffn_up, v_w_ffn_down,
                                    v_pre_mix_g, v_post_mix_g, v_pre_ffn_g, v_post_ffn_g)))
    assert x.shape[0] == 1 and x.shape[2] == D_MODEL, x.shape
    n_layers = w_in.shape[0]
    t = x.shape[1]
    x0 = x.reshape(t, D_MODEL)
    target = loss_target.reshape(t, D_MODEL)
    shard_shapes = {n: weights[n].shape for n in BIG_NAMES}
    small_shapes = [weights[n].shape for n in SMALL_NAMES]

    packed = _pack_rows([weights[n].astype(BF16).reshape(-1) for n in BIG_NAMES])
    gathered = _all_gather(packed).reshape(N_DEV, -1)
    full, off = {}, 0
    for n in BIG_NAMES:
        cnt = int(np.prod(shard_shapes[n]))
        full[n] = _from_shards(gathered[:, off:off + cnt], shard_shapes[n], SHARD_AXIS[n])
        off += cnt
    lw_all = _layout_weights(full)
    lw_all["b_gate"] = b_gate.reshape(n_layers, 1, 2 * D_MODEL)
    lw_all["gq2"] = jnp.tile(q_norm_g, (1, 2)).reshape(n_layers, 1, LANES)
    lw_all["gk2"] = jnp.tile(k_norm_g, (1, 2)).reshape(n_layers, 1, LANES)
    lw_all["gqa"] = q_a_norm_g.reshape(n_layers, 1, MLA_Q_RANK)
    lw_all["gkva"] = kv_a_norm_g.reshape(n_layers, 1, MLA_KV_RANK)
    for n in ("post_mix_g", "pre_ffn_g", "post_ffn_g"):
        lw_all[n] = weights[n]
    lw_all["next_pre_mix_g"] = jnp.roll(pre_mix_g, -1, axis=0)

    tabs = _rope_tables(t)
    u0 = _rms_fwd(x0, pre_mix_g[0])

    def fwd_step(carry, lw):
        xc, uc = carry
        x3, u_next, saved = _layer_fwd(xc, uc, lw, tabs)
        return (x3, u_next), saved

    (y, _), saved_all = lax.scan(fwd_step, (x0, u0), lw_all)
    dy, loss_acc = _loss_grad(y, target)
    loss = lax.psum(0.5 * jnp.sum(loss_acc) / D_MODEL, ("x", "y", "c"))

    def bwd_step(carry, xs):
        dxc, duc = carry
        lw, sv = xs
        dx, du, g = _layer_bwd(dxc, duc, lw, sv, tabs)
        return (dx, du), g

    (dx0, du0), grads = lax.scan(bwd_step, (dy, jnp.zeros((t, D_MODEL), F32)), (lw_all, saved_all), reverse=True)
    grad_x, dg1_first = _rms_bwd(x0, pre_mix_g[0], dx0, du0)

    big_grads = _unlayout_grads({n: grads[n] for n in BIG_NAMES})
    fold = lambda a: a.sum(axis=1)
    dgq = fold(grads["q_norm_g"]).reshape(n_layers, 2, HEAD_DIM).sum(axis=1)
    dgk = fold(grads["k_norm_g"]).reshape(n_layers, 2, HEAD_DIM).sum(axis=1)
    dg1 = jnp.concatenate([fold(dg1_first[None]), fold(grads["next_pre_mix_g"])[:-1]], axis=0)
    small_grads = {
        "b_gate": fold(grads["b_gate"]), "q_norm_g": dgq, "k_norm_g": dgk, "q_a_norm_g": fold(grads["q_a_norm_g"]),
        "kv_a_norm_g": fold(grads["kv_a_norm_g"]), "pre_mix_g": dg1, "post_mix_g": fold(grads["post_mix_g"]),
        "pre_ffn_g": fold(grads["pre_ffn_g"]), "post_ffn_g": fold(grads["post_ffn_g"]),
    }
    small_flat = jnp.concatenate([small_grads[n].reshape(-1) for n in SMALL_NAMES])
    send_parts = [_shards_of(big_grads[n], SHARD_AXIS[n]) for n in BIG_NAMES]
    send_parts.append(jnp.broadcast_to(small_flat[None], (N_DEV, small_flat.shape[0])))
    send = _pack_rows(send_parts, lead=(N_DEV,))
    recv = _all_to_all(send)

    def pack_state(d):
        return _pack_rows([d[n].reshape(-1) for n in BIG_NAMES] + [d[n].reshape(-1) for n in SMALL_NAMES])

    g_p, d_p, m_p, v_p = _adamw(recv, pack_state(weights), pack_state(mom_m), pack_state(mom_v))
    shapes = [shard_shapes[n] for n in BIG_NAMES] + small_shapes
    order = BIG_NAMES + SMALL_NAMES
    outs = []
    for packed_out in (g_p, d_p, m_p, v_p):
        by_name = dict(zip(order, _unpack_rows(packed_out, shapes)))
        outs.extend(by_name[n] for n in WEIGHT_NAMES)
    return (loss, grad_x.reshape(x.shape), *outs)
```

```python
import functools
import math

import jax
import jax.numpy as jnp
import numpy as np
from jax import lax
from jax.experimental import pallas as pl
from jax.experimental.pallas import tpu as pltpu

F32 = jnp.float32
BF16 = jnp.bfloat16

D_MODEL = 1024
GRID_W = 64
ROPE_THETA = 10000.0
EPS = 1e-6
GQA_HEADS = 8
GQA_KV_HEADS = 2
GQA_GROUP = GQA_HEADS // GQA_KV_HEADS
HEAD_DIM = 64
MLA_HEADS = 8
MLA_ROPE_DIM = 32
MLA_QK_DIM = 96
MLA_Q_RANK = 384
MLA_KV_RANK = 256
D_FF = 4 * D_MODEL
GQA_SCALE = 1.0 / math.sqrt(HEAD_DIM)
MLA_SCALE = 1.0 / math.sqrt(MLA_QK_DIM)
LOG2E = math.log2(math.e)
LN2 = math.log(2.0)

ADAM_LR = 0.001
ADAM_B1 = 0.9
ADAM_B2 = 0.999
ADAM_EPS = 1e-08
ADAM_WD = 0.01
ADAM_STEP = 10

N_DEV = 8
LANES = 128
SUBLANES = 8
VMEM_LIMIT = 48 * 1024 * 1024

Z_QA, Z_KA, Z_VA, Z_CQ, Z_CKV, Z_KR, Z_GATE = 0, 512, 640, 768, 1152, 1408, 1536
Z_ATT_W = 1536
Z_W = 3584
KR_LANE0 = 64

WEIGHT_NAMES = ("w_in", "b_gate", "q_norm_g", "k_norm_g", "q_a_norm_g", "kv_a_norm_g", "w_q_up", "w_kv_up",
                "w_branch_a", "w_branch_b", "w_o", "w_ffn_up", "w_ffn_down", "pre_mix_g", "post_mix_g",
                "pre_ffn_g", "post_ffn_g")
SHARD_AXIS = {"w_in": 2, "w_q_up": 2, "w_kv_up": 2, "w_branch_a": 2, "w_branch_b": 2, "w_o": 1, "w_ffn_up": 2,
              "w_ffn_down": 1}
BIG_NAMES = tuple(n for n in WEIGHT_NAMES if n in SHARD_AXIS)
SMALL_NAMES = tuple(n for n in WEIGHT_NAMES if n not in SHARD_AXIS)
PACK_ROW_TILE = 1024
MM_TILE = 1024
ATTN_TQ = 1024
ATTN_TK = 512


def _params(*semantics):
    return pltpu.CompilerParams(dimension_semantics=semantics, vmem_limit_bytes=VMEM_LIMIT)


def _tile(n, pref):
    if n <= pref:
        return n
    t = (pref // LANES) * LANES
    while n % t:
        t -= LANES
    return t


def _fold8(t):
    return t.reshape(t.shape[0] // SUBLANES, SUBLANES, t.shape[1]).sum(axis=0)


_DIMS = {"nn": ((1,), (0,)), "nt": ((1,), (1,)), "tn": ((0,), (0,))}


def _matmul(a, b, mode, name):
    if mode == "nn":
        (m, k), n = a.shape, b.shape[1]
    elif mode == "nt":
        (m, k), n = a.shape, b.shape[0]
    else:
        (k, m), n = a.shape, b.shape[1]
    tm, tn, tk = _tile(m, MM_TILE), _tile(n, MM_TILE), _tile(k, MM_TILE)
    nk = k // tk
    dims = (_DIMS[mode], ((), ()))

    def body(a_ref, b_ref, o_ref, acc_ref):
        prod = lax.dot_general(a_ref[...], b_ref[...], dims, preferred_element_type=F32)
        if nk == 1:
            o_ref[...] = prod
        else:
            kk = pl.program_id(2)

            @pl.when(kk == 0)
            def _():
                acc_ref[...] = prod

            @pl.when(kk > 0)
            def _():
                acc_ref[...] += prod

            @pl.when(kk == nk - 1)
            def _():
                o_ref[...] = acc_ref[...]

    if mode == "tn":
        a_spec = pl.BlockSpec((tk, tm), lambda i, j, kk: (kk, i))
    else:
        a_spec = pl.BlockSpec((tm, tk), lambda i, j, kk: (i, kk))
    if mode == "nt":
        b_spec = pl.BlockSpec((tn, tk), lambda i, j, kk: (j, kk))
    else:
        b_spec = pl.BlockSpec((tk, tn), lambda i, j, kk: (kk, j))
    return pl.pallas_call(
        body,
        name=name,
        grid=(m // tm, n // tn, nk),
        in_specs=[a_spec, b_spec],
        out_specs=pl.BlockSpec((tm, tn), lambda i, j, kk: (i, j)),
        out_shape=jax.ShapeDtypeStruct((m, n), F32),
        scratch_shapes=[pltpu.VMEM((tm, tn), F32)],
        compiler_params=_params("parallel", "parallel", "arbitrary"),
    )(a, b)


def _rinv(x):
    return lax.rsqrt(jnp.mean(x * x, axis=-1, keepdims=True) + EPS)


def _rms_bwd_rows(x, g, dy):
    r = _rinv(x)
    xh = x * r
    dxh = dy * g
    dx = r * (dxh - xh * jnp.mean(dxh * xh, axis=-1, keepdims=True))
    return dx, dy * xh


def _row_spec(tm, c):
    return pl.BlockSpec((tm, c), lambda i: (i, 0))


def _vec_spec(c):
    return pl.BlockSpec((1, c), lambda i: (0, 0))


def _acc_spec(c):
    return pl.BlockSpec((SUBLANES, c), lambda i: (0, 0))


def _rms_fwd(x, g):
    t, d = x.shape
    tm = _tile(t, 512)

    def body(x_ref, g_ref, o_ref):
        xv = x_ref[...]
        o_ref[...] = (xv * _rinv(xv) * g_ref[...]).astype(BF16)

    return pl.pallas_call(
        body, name="rms_fwd", grid=(t // tm,),
        in_specs=[_row_spec(tm, d), _vec_spec(d)], out_specs=_row_spec(tm, d),
        out_shape=jax.ShapeDtypeStruct((t, d), BF16), compiler_params=_params("parallel"),
    )(x, g.reshape(1, d))


def _rms_bwd(x, g, dres, dy):
    t, d = x.shape
    tm = _tile(t, 512)

    def body(x_ref, g_ref, dres_ref, dy_ref, dx_ref, dg_ref):
        dx, dgc = _rms_bwd_rows(x_ref[...], g_ref[...], dy_ref[...])
        dx_ref[...] = dres_ref[...] + dx

        @pl.when(pl.program_id(0) == 0)
        def _():
            dg_ref[...] = jnp.zeros_like(dg_ref)

        dg_ref[...] += _fold8(dgc)

    return pl.pallas_call(
        body, name="rms_bwd", grid=(t // tm,),
        in_specs=[_row_spec(tm, d), _vec_spec(d), _row_spec(tm, d), _row_spec(tm, d)],
        out_specs=[_row_spec(tm, d), _acc_spec(d)],
        out_shape=[jax.ShapeDtypeStruct((t, d), F32), jax.ShapeDtypeStruct((SUBLANES, d), F32)],
        compiler_params=_params("arbitrary"),
    )(x, g.reshape(1, d), dres, dy)


def _res_norm_fwd(x, m, g_post, g_next):
    t, d = x.shape
    tm = _tile(t, 512)

    def body(x_ref, m_ref, gp_ref, gn_ref, x2_ref, u2_ref):
        mv = m_ref[...]
        x2 = x_ref[...] + mv * _rinv(mv) * gp_ref[...]
        x2_ref[...] = x2
        u2_ref[...] = (x2 * _rinv(x2) * gn_ref[...]).astype(BF16)

    return pl.pallas_call(
        body, name="res_norm_fwd", grid=(t // tm,),
        in_specs=[_row_spec(tm, d), _row_spec(tm, d), _vec_spec(d), _vec_spec(d)],
        out_specs=[_row_spec(tm, d), _row_spec(tm, d)],
        out_shape=[jax.ShapeDtypeStruct((t, d), F32), jax.ShapeDtypeStruct((t, d), BF16)],
        compiler_params=_params("parallel"),
    )(x, m, g_post.reshape(1, d), g_next.reshape(1, d))


def _res_norm_bwd(x2, m, g_post, g_next, dx2_in, du2):
    t, d = x2.shape
    tm = _tile(t, 512)

    def body(x2_ref, m_ref, gp_ref, gn_ref, dx2in_ref, du2_ref, dx2_ref, dm_ref, dgp_ref, dgn_ref):
        dxn, dgn_c = _rms_bwd_rows(x2_ref[...], gn_ref[...], du2_ref[...])
        dx2 = dx2in_ref[...] + dxn
        dx2_ref[...] = dx2
        dm, dgp_c = _rms_bwd_rows(m_ref[...], gp_ref[...], dx2)
        dm_ref[...] = dm.astype(BF16)

        @pl.when(pl.program_id(0) == 0)
        def _():
            dgp_ref[...] = jnp.zeros_like(dgp_ref)
            dgn_ref[...] = jnp.zeros_like(dgn_ref)

        dgp_ref[...] += _fold8(dgp_c)
        dgn_ref[...] += _fold8(dgn_c)

    return pl.pallas_call(
        body, name="res_norm_bwd", grid=(t // tm,),
        in_specs=[_row_spec(tm, d), _row_spec(tm, d), _vec_spec(d), _vec_spec(d), _row_spec(tm, d), _row_spec(tm, d)],
        out_specs=[_row_spec(tm, d), _row_spec(tm, d), _acc_spec(d), _acc_spec(d)],
        out_shape=[jax.ShapeDtypeStruct((t, d), F32), jax.ShapeDtypeStruct((t, d), BF16),
                   jax.ShapeDtypeStruct((SUBLANES, d), F32), jax.ShapeDtypeStruct((SUBLANES, d), F32)],
        compiler_params=_params("arbitrary"),
    )(x2, m, g_post.reshape(1, d), g_next.reshape(1, d), dx2_in, du2)


def _rope_tables(t):
    rows = t // GRID_W
    row = jnp.repeat(jnp.arange(rows, dtype=F32), GRID_W)
    col = jnp.tile(jnp.arange(GRID_W, dtype=F32), rows)

    def tab(rot_dim):
        half = rot_dim // 2
        inv = ROPE_THETA ** (-jnp.arange(0, half, 2, dtype=F32) / half)
        ar = row[:, None] * inv[None, :]
        ac = col[:, None] * inv[None, :]
        ang = jnp.concatenate([ar, ar, ac, ac], axis=-1)
        q = half // 2
        sign = np.tile(np.concatenate([-np.ones(q, np.float32), np.ones(q, np.float32)]), 2)
        return jnp.cos(ang), jnp.sin(ang) * sign[None, :]

    ca, sa = tab(HEAD_DIM)
    cb, sb = tab(MLA_ROPE_DIM)
    one = jnp.ones((t, 1), F32)
    cos_b = jnp.concatenate([one * jnp.ones((1, KR_LANE0), F32), cb, one * jnp.ones((1, 32), F32)], axis=-1)
    sin_b = jnp.concatenate([jnp.zeros((t, KR_LANE0), F32), sb, jnp.zeros((t, 32), F32)], axis=-1)
    return jnp.tile(ca, (1, GQA_HEADS)), jnp.tile(sa, (1, GQA_HEADS)), cos_b, sin_b


def _swap_halves(x, sh):
    lane = lax.broadcasted_iota(jnp.int32, x.shape, 1)
    up = pltpu.roll(x, LANES - sh, 1)
    dn = pltpu.roll(x, sh, 1)
    return jnp.where((lane & (2 * sh - 1)) < sh, up, dn)


def _rope(x, cos, sin_s, sh):
    return x * cos + _swap_halves(x, sh) * sin_s


def _rope_bwd(dy, cos, sin_s, sh):
    return dy * cos + _swap_halves(dy * sin_s, sh)


def _lo_mask(shape):
    return lax.broadcasted_iota(jnp.int32, shape, 1) < HEAD_DIM


def _half_mean(t, lo):
    s_lo = jnp.sum(jnp.where(lo, t, 0.0), axis=-1, keepdims=True)
    s_hi = jnp.sum(jnp.where(lo, 0.0, t), axis=-1, keepdims=True)
    return jnp.where(lo, s_lo, s_hi) * (1.0 / HEAD_DIM)


def _head_norm(x, g2):
    lo = _lo_mask(x.shape)
    r = lax.rsqrt(_half_mean(x * x, lo) + EPS)
    return x * r * g2


def _head_norm_bwd(x, g2, dy):
    lo = _lo_mask(x.shape)
    r = lax.rsqrt(_half_mean(x * x, lo) + EPS)
    xh = x * r
    dxh = dy * g2
    dx = r * (dxh - xh * _half_mean(dxh * xh, lo))
    return dx, dy * xh


def _prep_a_fwd(z, gq2, gk2, gqa, gkva, cos_a, sin_a, cos_b, sin_b):
    t = z.shape[0]
    tm = _tile(t, 256)

    def body(z_ref, gq_ref, gk_ref, gqa_ref, gkva_ref, ca_ref, sa_ref, cb_ref, sb_ref,
             qa_ref, ka_ref, va_ref, cqn_ref, ckvn_ref, krr_ref):
        for j in range(4):
            cols = slice(LANES * j, LANES * (j + 1))
            y = _rope(_head_norm(z_ref[:, cols], gq_ref[...]), ca_ref[:, cols], sa_ref[:, cols], 16)
            qa_ref[:, cols] = (y * (GQA_SCALE * LOG2E)).astype(BF16)
        y = _rope(_head_norm(z_ref[:, Z_KA:Z_VA], gk_ref[...]), ca_ref[:, :LANES], sa_ref[:, :LANES], 16)
        ka_ref[...] = y.astype(BF16)
        va_ref[...] = z_ref[:, Z_VA:Z_CQ].astype(BF16)
        cq = z_ref[:, Z_CQ:Z_CKV]
        cqn_ref[...] = (cq * _rinv(cq) * gqa_ref[...]).astype(BF16)
        ckv = z_ref[:, Z_CKV:Z_KR]
        ckvn_ref[...] = (ckv * _rinv(ckv) * gkva_ref[...]).astype(BF16)
        krr_ref[...] = _rope(z_ref[:, Z_KR:Z_GATE], cb_ref[...], sb_ref[...], 8)

    return pl.pallas_call(
        body, name="prep_a_fwd", grid=(t // tm,),
        in_specs=[_row_spec(tm, Z_ATT_W), _vec_spec(LANES), _vec_spec(LANES), _vec_spec(MLA_Q_RANK),
                  _vec_spec(MLA_KV_RANK), _row_spec(tm, 512), _row_spec(tm, 512), _row_spec(tm, LANES),
                  _row_spec(tm, LANES)],
        out_specs=[_row_spec(tm, 512), _row_spec(tm, LANES), _row_spec(tm, LANES), _row_spec(tm, MLA_Q_RANK),
                   _row_spec(tm, MLA_KV_RANK), _row_spec(tm, LANES)],
        out_shape=[jax.ShapeDtypeStruct((t, 512), BF16), jax.ShapeDtypeStruct((t, LANES), BF16),
                   jax.ShapeDtypeStruct((t, LANES), BF16), jax.ShapeDtypeStruct((t, MLA_Q_RANK), BF16),
                   jax.ShapeDtypeStruct((t, MLA_KV_RANK), BF16), jax.ShapeDtypeStruct((t, LANES), F32)],
        compiler_params=_params("parallel"),
    )(z, gq2, gk2, gqa, gkva, cos_a, sin_a, cos_b, sin_b)


def _prep_a_bwd(z, dqa, dka4, dva4, dcqn, dckvn, dkr, dzga, dzgb, gq2, gk2, gqa, gkva, cos_a, sin_a):
    t = z.shape[0]
    tm = _tile(t, 256)

    def body(z_ref, dqa_ref, dka_ref, dva_ref, dcqn_ref, dckvn_ref, dkr_ref, dzga_ref, dzgb_ref, gq_ref, gk_ref,
             gqa_ref, gkva_ref, ca_ref, sa_ref, dz_ref, dgq_ref, dgk_ref, dgqa_ref, dgkva_ref):
        @pl.when(pl.program_id(0) == 0)
        def _():
            dgq_ref[...] = jnp.zeros_like(dgq_ref)
            dgk_ref[...] = jnp.zeros_like(dgk_ref)
            dgqa_ref[...] = jnp.zeros_like(dgqa_ref)
            dgkva_ref[...] = jnp.zeros_like(dgkva_ref)

        dgq = jnp.zeros((SUBLANES, LANES), F32)
        for j in range(4):
            cols = slice(LANES * j, LANES * (j + 1))
            dy = _rope_bwd(dqa_ref[:, cols] * GQA_SCALE, ca_ref[:, cols], sa_ref[:, cols], 16)
            dx, dgc = _head_norm_bwd(z_ref[:, cols], gq_ref[...], dy)
            dz_ref[:, cols] = dx.astype(BF16)
            dgq = dgq + _fold8(dgc)
        dgq_ref[...] += dgq
        dk = (dka_ref[0] + dka_ref[1] + dka_ref[2] + dka_ref[3]) * LN2
        dy = _rope_bwd(dk, ca_ref[:, :LANES], sa_ref[:, :LANES], 16)
        dx, dgc = _head_norm_bwd(z_ref[:, Z_KA:Z_VA], gk_ref[...], dy)
        dz_ref[:, Z_KA:Z_VA] = dx.astype(BF16)
        dgk_ref[...] += _fold8(dgc)
        dz_ref[:, Z_VA:Z_CQ] = (dva_ref[0] + dva_ref[1] + dva_ref[2] + dva_ref[3]).astype(BF16)
        dx, dgc = _rms_bwd_rows(z_ref[:, Z_CQ:Z_CKV], gqa_ref[...], dcqn_ref[...])
        dz_ref[:, Z_CQ:Z_CKV] = dx.astype(BF16)
        dgqa_ref[...] += _fold8(dgc)
        dx, dgc = _rms_bwd_rows(z_ref[:, Z_CKV:Z_KR], gkva_ref[...], dckvn_ref[...])
        dz_ref[:, Z_CKV:Z_KR] = dx.astype(BF16)
        dgkva_ref[...] += _fold8(dgc)
        dz_ref[:, Z_KR:Z_GATE] = dkr_ref[...].astype(BF16)
        dz_ref[:, Z_GATE:Z_GATE + D_MODEL] = dzga_ref[...]
        dz_ref[:, Z_GATE + D_MODEL:Z_W] = dzgb_ref[...]

    part = pl.BlockSpec((4, tm, LANES), lambda i: (0, i, 0))
    return pl.pallas_call(
        body, name="prep_a_bwd", grid=(t // tm,),
        in_specs=[_row_spec(tm, Z_ATT_W), _row_spec(tm, 512), part, part, _row_spec(tm, MLA_Q_RANK),
                  _row_spec(tm, MLA_KV_RANK), _row_spec(tm, LANES), _row_spec(tm, D_MODEL), _row_spec(tm, D_MODEL),
                  _vec_spec(LANES),
                  _vec_spec(LANES), _vec_spec(MLA_Q_RANK), _vec_spec(MLA_KV_RANK), _row_spec(tm, 512),
                  _row_spec(tm, 512)],
        out_specs=[_row_spec(tm, Z_W), _acc_spec(LANES), _acc_spec(LANES), _acc_spec(MLA_Q_RANK),
                   _acc_spec(MLA_KV_RANK)],
        out_shape=[jax.ShapeDtypeStruct((t, Z_W), BF16), jax.ShapeDtypeStruct((SUBLANES, LANES), F32),
                   jax.ShapeDtypeStruct((SUBLANES, LANES), F32), jax.ShapeDtypeStruct((SUBLANES, MLA_Q_RANK), F32),
                   jax.ShapeDtypeStruct((SUBLANES, MLA_KV_RANK), F32)],
        compiler_params=_params("arbitrary"),
    )(z, dqa, dka4, dva4, dcqn, dckvn, dkr, dzga, dzgb, gq2, gk2, gqa, gkva, cos_a, sin_a)


def _prep_b_fwd(qb, kvb, krr, cos_b, sin_b):
    t = qb.shape[0]
    tm = _tile(t, 256)

    def body(qb_ref, kvb_ref, krr_ref, cb_ref, sb_ref, q_ref, k_ref, v_ref):
        for h in range(MLA_HEADS):
            cols = slice(LANES * h, LANES * (h + 1))
            q_ref[:, cols] = (_rope(qb_ref[:, cols], cb_ref[...], sb_ref[...], 8) * (MLA_SCALE * LOG2E)).astype(BF16)
            k_ref[:, cols] = (kvb_ref[:, cols] + krr_ref[...]).astype(BF16)
        v_ref[...] = kvb_ref[:, 1024:1536].astype(BF16)

    return pl.pallas_call(
        body, name="prep_b_fwd", grid=(t // tm,),
        in_specs=[_row_spec(tm, 1024), _row_spec(tm, 1536), _row_spec(tm, LANES), _row_spec(tm, LANES),
                  _row_spec(tm, LANES)],
        out_specs=[_row_spec(tm, 1024), _row_spec(tm, 1024), _row_spec(tm, 512)],
        out_shape=[jax.ShapeDtypeStruct((t, 1024), BF16), jax.ShapeDtypeStruct((t, 1024), BF16),
                   jax.ShapeDtypeStruct((t, 512), BF16)],
        compiler_params=_params("parallel"),
    )(qb, kvb, krr, cos_b, sin_b)


def _prep_b_bwd(dq, dk, dv, cos_b, sin_b):
    t = dq.shape[0]
    tm = _tile(t, 256)

    def body(dq_ref, dk_ref, dv_ref, cb_ref, sb_ref, dqb_ref, dkvb_ref, dkr_ref):
        dkr = jnp.zeros((tm, LANES), F32)
        for h in range(MLA_HEADS):
            cols = slice(LANES * h, LANES * (h + 1))
            dqb_ref[:, cols] = _rope_bwd(dq_ref[:, cols] * MLA_SCALE, cb_ref[...], sb_ref[...], 8).astype(BF16)
            dkh = dk_ref[:, cols] * LN2
            dkvb_ref[:, cols] = dkh.astype(BF16)
            dkr = dkr + dkh
        dkvb_ref[:, 1024:1536] = dv_ref[...].astype(BF16)
        dkr_ref[...] = _rope_bwd(dkr, cb_ref[...], sb_ref[...], 8)

    return pl.pallas_call(
        body, name="prep_b_bwd", grid=(t // tm,),
        in_specs=[_row_spec(tm, 1024), _row_spec(tm, 1024), _row_spec(tm, 512), _row_spec(tm, LANES),
                  _row_spec(tm, LANES)],
        out_specs=[_row_spec(tm, 1024), _row_spec(tm, 1536), _row_spec(tm, LANES)],
        out_shape=[jax.ShapeDtypeStruct((t, 1024), BF16), jax.ShapeDtypeStruct((t, 1536), BF16),
                   jax.ShapeDtypeStruct((t, LANES), F32)],
        compiler_params=_params("parallel"),
    )(dq, dk, dv, cos_b, sin_b)


_NT = (((1,), (1,)), ((), ()))
_NN = (((1,), (0,)), ((), ()))
_TN = (((0,), (0,)), ((), ()))


def _head_operands(qv, kv, i, shared_k):
    if shared_k:
        lo = _lo_mask(qv.shape)
        keep = lo if i == 0 else jnp.logical_not(lo)
        return jnp.where(keep, qv, jnp.zeros_like(qv)), kv
    cols = slice(LANES * i, LANES * (i + 1))
    return qv[:, cols], kv[:, cols]


def _attn_specs(shared_k, tq, tk, q_of, k_of):
    wq = LANES if shared_k else 2 * LANES
    q_spec = pl.BlockSpec((tq, wq), lambda *g: (q_of(*g), g[0]))
    if shared_k:
        k_spec = pl.BlockSpec((tk, LANES), lambda *g: (k_of(*g), 0))
        v_spec = pl.BlockSpec((tk, LANES), lambda *g: (k_of(*g), 0))
    else:
        k_spec = pl.BlockSpec((tk, wq), lambda *g: (k_of(*g), g[0]))
        v_spec = pl.BlockSpec((tk, LANES), lambda *g: (k_of(*g), g[0]))
    return wq, q_spec, k_spec, v_spec


def _attn_fwd(q, k, v, shared_k, name):
    t = q.shape[0]
    tq, tk = _tile(t, ATTN_TQ), _tile(t, ATTN_TK)
    nq, nk = t // tq, t // tk
    wq, q_spec, k_spec, v_spec = _attn_specs(shared_k, tq, tk, lambda p, i, j: i, lambda p, i, j: j)
    groups = q.shape[1] // wq

    def body(q_ref, k_ref, v_ref, o_ref, lse_ref, m_s, l_s, acc_s):
        kb = pl.program_id(2)

        @pl.when(kb == 0)
        def _():
            m_s[...] = jnp.full_like(m_s, -jnp.inf)
            l_s[...] = jnp.zeros_like(l_s)
            acc_s[...] = jnp.zeros_like(acc_s)

        qv, kv, vv = q_ref[...], k_ref[...], v_ref[...]
        for i in range(2):
            qi, ki = _head_operands(qv, kv, i, shared_k)
            st = lax.dot_general(ki, qi, _NT, preferred_element_type=F32)
            m_prev = m_s[i]
            m_new = jnp.maximum(m_prev, jnp.max(st, axis=0, keepdims=True))
            alpha = jnp.exp2(m_prev - m_new)
            pt = jnp.exp2(st - m_new)
            l_s[i] = alpha * l_s[i] + jnp.sum(pt, axis=0, keepdims=True)
            acc_s[i] = alpha * acc_s[i] + lax.dot_general(vv, pt.astype(BF16), _TN, preferred_element_type=F32)
            m_s[i] = m_new

        @pl.when(kb == nk - 1)
        def _():
            o0 = acc_s[0] / l_s[0]
            o1 = acc_s[1] / l_s[1]
            row_lo = lax.broadcasted_iota(jnp.int32, o0.shape, 0) < HEAD_DIM
            o_ref[...] = jnp.where(row_lo, o0, o1).T.astype(BF16)
            lse_ref[0] = m_s[0] + jnp.log2(l_s[0])
            lse_ref[1] = m_s[1] + jnp.log2(l_s[1])

    return pl.pallas_call(
        body, name=name, grid=(groups, nq, nk),
        in_specs=[q_spec, k_spec, v_spec],
        out_specs=[pl.BlockSpec((tq, LANES), lambda p, i, j: (i, p)),
                   pl.BlockSpec((2, 1, tq), lambda p, i, j: (p, 0, i))],
        out_shape=[jax.ShapeDtypeStruct((t, LANES * groups), BF16),
                   jax.ShapeDtypeStruct((2 * groups, 1, t), F32)],
        scratch_shapes=[pltpu.VMEM((2, 1, tq), F32), pltpu.VMEM((2, 1, tq), F32), pltpu.VMEM((2, LANES, tq), F32)],
        compiler_params=_params("parallel", "parallel", "arbitrary"),
    )(q, k, v)


def _attn_delta(do, o):
    t, w = do.shape
    tm = _tile(t, 512)
    groups = w // LANES

    def body(do_ref, o_ref, delta_ref, dob_ref):
        dov = do_ref[...]
        dob_ref[...] = dov.astype(BF16)
        prod = dov * o_ref[...].astype(F32)
        lane_lo = lax.broadcasted_iota(jnp.int32, (SUBLANES, LANES), 1) < HEAD_DIM
        masks = (lane_lo.astype(BF16), jnp.logical_not(lane_lo).astype(BF16))
        for g in range(groups):
            x = prod[:, LANES * g:LANES * (g + 1)]
            hi = x.astype(BF16)
            mid = (x - hi.astype(F32)).astype(BF16)
            for i in range(2):
                r = (lax.dot_general(masks[i], hi, _NT, preferred_element_type=F32)
                     + lax.dot_general(masks[i], mid, _NT, preferred_element_type=F32))
                delta_ref[2 * g + i] = r[0:1, :]

    return pl.pallas_call(
        body, name="attn_delta", grid=(t // tm,),
        in_specs=[_row_spec(tm, w), _row_spec(tm, w)],
        out_specs=[pl.BlockSpec((2 * groups, 1, tm), lambda i: (0, 0, i)), _row_spec(tm, w)],
        out_shape=[jax.ShapeDtypeStruct((2 * groups, 1, t), F32), jax.ShapeDtypeStruct((t, w), BF16)],
        compiler_params=_params("parallel"),
    )(do, o)


def _attn_bwd(q, k, v, do, lse, delta, shared_k, name):
    t = q.shape[0]
    tq, tk = _tile(t, ATTN_TQ), _tile(t, ATTN_TK)
    nq, nk = t // tq, t // tk
    wq, q_spec, k_spec, v_spec = _attn_specs(shared_k, tq, tk, lambda p, j, i: i, lambda p, j, i: j)
    groups = q.shape[1] // wq

    def body(q_ref, k_ref, v_ref, do_ref, lse_ref, delta_ref, dq_ref, dk_ref, dv_ref, dk_s, dv_s):
        kb, qb = pl.program_id(1), pl.program_id(2)

        @pl.when(qb == 0)
        def _():
            dk_s[...] = jnp.zeros_like(dk_s)
            dv_s[...] = jnp.zeros_like(dv_s)

        qv, kv, vv, dov = q_ref[...], k_ref[...], v_ref[...], do_ref[...]
        lo = _lo_mask(dov.shape)
        dq_parts = []
        for i in range(2):
            qi, ki = _head_operands(qv, kv, i, shared_k)
            keep = lo if i == 0 else jnp.logical_not(lo)
            doi = jnp.where(keep, dov, jnp.zeros_like(dov))
            st = lax.dot_general(ki, qi, _NT, preferred_element_type=F32)
            pt = jnp.exp2(st - lse_ref[i])
            dpt = lax.dot_general(vv, doi, _NT, preferred_element_type=F32)
            dst = (pt * (dpt - delta_ref[i])).astype(BF16)
            dv_s[...] += lax.dot_general(pt.astype(BF16), doi, _NN, preferred_element_type=F32)
            dk_i = lax.dot_general(dst, qi, _NN, preferred_element_type=F32)
            if shared_k:
                dk_s[...] += dk_i
            else:
                dk_s[:, LANES * i:LANES * (i + 1)] += dk_i
            dq_parts.append(lax.dot_general(dst, ki, _TN, preferred_element_type=F32))
        rows = pl.ds(pl.multiple_of(qb * tq, tq), tq)
        if shared_k:
            tiles = [(slice(0, LANES), jnp.where(lo, dq_parts[0], dq_parts[1]))]
        else:
            tiles = [(slice(0, LANES), dq_parts[0]), (slice(LANES, 2 * LANES), dq_parts[1])]
        for cols, val in tiles:
            @pl.when(kb == 0)
            def _(cols=cols, val=val):
                dq_ref[rows, cols] = val

            @pl.when(kb > 0)
            def _(cols=cols, val=val):
                dq_ref[rows, cols] += val

        @pl.when(qb == nq - 1)
        def _():
            if shared_k:
                dk_ref[0] = dk_s[...]
                dv_ref[0] = dv_s[...]
            else:
                dk_ref[...] = dk_s[...]
                dv_ref[...] = dv_s[...]

    stat_spec = pl.BlockSpec((2, 1, tq), lambda p, j, i: (p, 0, i))
    do_spec = pl.BlockSpec((tq, LANES), lambda p, j, i: (i, p))
    dq_spec = pl.BlockSpec((t, wq), lambda p, j, i: (0, p))
    if shared_k:
        dk_spec = pl.BlockSpec((1, tk, LANES), lambda p, j, i: (p, j, 0))
        dv_spec = dk_spec
        dk_shape = jax.ShapeDtypeStruct((groups, t, LANES), F32)
        dv_shape = dk_shape
    else:
        dk_spec = pl.BlockSpec((tk, wq), lambda p, j, i: (j, p))
        dv_spec = pl.BlockSpec((tk, LANES), lambda p, j, i: (j, p))
        dk_shape = jax.ShapeDtypeStruct((t, wq * groups), F32)
        dv_shape = jax.ShapeDtypeStruct((t, LANES * groups), F32)
    return pl.pallas_call(
        body, name=name, grid=(groups, nk, nq),
        in_specs=[q_spec, k_spec, v_spec, do_spec, stat_spec, stat_spec],
        out_specs=[dq_spec, dk_spec, dv_spec],
        out_shape=[jax.ShapeDtypeStruct((t, wq * groups), F32), dk_shape, dv_shape],
        scratch_shapes=[pltpu.VMEM((tk, wq), F32), pltpu.VMEM((tk, LANES), F32)],
        compiler_params=_params("parallel", "arbitrary", "arbitrary"),
    )(q, k, v, do, lse, delta)


_MERGE_W = 512
_GATE_BLK0 = Z_GATE // _MERGE_W


def _merge_fwd(z, b_gate, ta, tb):
    t = z.shape[0]
    tm = _tile(t, 512)
    w = _MERGE_W
    nj = D_MODEL // w

    def body(za_ref, zb_ref, ba_ref, bb_ref, ta_ref, tb_ref, o_ref):
        ga = jax.nn.sigmoid(za_ref[...] + ba_ref[...])
        gb = jax.nn.sigmoid(zb_ref[...] + bb_ref[...])
        o_ref[...] = (ga * ta_ref[...] + gb * tb_ref[...]).astype(BF16)

    return pl.pallas_call(
        body, name="merge_fwd", grid=(t // tm, nj),
        in_specs=[pl.BlockSpec((tm, w), lambda i, j: (i, _GATE_BLK0 + j)),
                  pl.BlockSpec((tm, w), lambda i, j: (i, _GATE_BLK0 + nj + j)),
                  pl.BlockSpec((1, w), lambda i, j: (0, j)),
                  pl.BlockSpec((1, w), lambda i, j: (0, nj + j)),
                  pl.BlockSpec((tm, w), lambda i, j: (i, j)),
                  pl.BlockSpec((tm, w), lambda i, j: (i, j))],
        out_specs=pl.BlockSpec((tm, w), lambda i, j: (i, j)),
        out_shape=jax.ShapeDtypeStruct((t, D_MODEL), BF16),
        compiler_params=_params("parallel", "parallel"),
    )(z, z, b_gate, b_gate, ta, tb)


def _merge_bwd(dmg, z, b_gate, ta, tb):
    t = z.shape[0]
    tm = _tile(t, 512)
    w = _MERGE_W
    nj = D_MODEL // w

    def body(dm_ref, za_ref, zb_ref, ba_ref, bb_ref, ta_ref, tb_ref, dta_ref, dtb_ref, dza_ref, dzb_ref,
             dba_ref, dbb_ref):
        dm = dm_ref[...]
        ga = jax.nn.sigmoid(za_ref[...] + ba_ref[...])
        gb = jax.nn.sigmoid(zb_ref[...] + bb_ref[...])
        dta_ref[...] = (dm * ga).astype(BF16)
        dtb_ref[...] = (dm * gb).astype(BF16)
        dza = dm * ta_ref[...] * ga * (1.0 - ga)
        dzb = dm * tb_ref[...] * gb * (1.0 - gb)
        dza_ref[...] = dza.astype(BF16)
        dzb_ref[...] = dzb.astype(BF16)

        @pl.when(pl.program_id(1) == 0)
        def _():
            dba_ref[...] = jnp.zeros_like(dba_ref)
            dbb_ref[...] = jnp.zeros_like(dbb_ref)

        dba_ref[...] += _fold8(dza)
        dbb_ref[...] += _fold8(dzb)

    blk = pl.BlockSpec((tm, w), lambda j, i: (i, j))
    acc = pl.BlockSpec((SUBLANES, w), lambda j, i: (0, j))
    return pl.pallas_call(
        body, name="merge_bwd", grid=(nj, t // tm),
        in_specs=[blk,
                  pl.BlockSpec((tm, w), lambda j, i: (i, _GATE_BLK0 + j)),
                  pl.BlockSpec((tm, w), lambda j, i: (i, _GATE_BLK0 + nj + j)),
                  pl.BlockSpec((1, w), lambda j, i: (0, j)),
                  pl.BlockSpec((1, w), lambda j, i: (0, nj + j)),
                  blk, blk],
        out_specs=[blk, blk, blk, blk, acc, acc],
        out_shape=[jax.ShapeDtypeStruct((t, D_MODEL), BF16)] * 4 + [jax.ShapeDtypeStruct((SUBLANES, D_MODEL), F32)] * 2,
        compiler_params=_params("parallel", "arbitrary"),
    )(dmg, z, z, b_gate, b_gate, ta, tb)


def _relu2_fwd(h):
    t, f = h.shape
    tm, tn = _tile(t, 512), _tile(f, 1024)

    def body(h_ref, a_ref):
        r = jnp.maximum(h_ref[...], 0.0)
        a_ref[...] = (r * r).astype(BF16)

    spec = pl.BlockSpec((tm, tn), lambda i, j: (i, j))
    return pl.pallas_call(
        body, name="relu2_fwd", grid=(t // tm, f // tn), in_specs=[spec], out_specs=spec,
        out_shape=jax.ShapeDtypeStruct((t, f), BF16), compiler_params=_params("parallel", "parallel"),
    )(h)


def _relu2_bwd(da, h):
    t, f = h.shape
    tm, tn = _tile(t, 512), _tile(f, 1024)

    def body(da_ref, h_ref, dh_ref):
        dh_ref[...] = (da_ref[...] * (2.0 * jnp.maximum(h_ref[...], 0.0))).astype(BF16)

    spec = pl.BlockSpec((tm, tn), lambda i, j: (i, j))
    return pl.pallas_call(
        body, name="relu2_bwd", grid=(t // tm, f // tn), in_specs=[spec, spec], out_specs=spec,
        out_shape=jax.ShapeDtypeStruct((t, f), BF16), compiler_params=_params("parallel", "parallel"),
    )(da, h)


def _loss_grad(y, target):
    t, d = y.shape
    tm = _tile(t, 512)

    def body(y_ref, t_ref, dy_ref, acc_ref):
        err = y_ref[...] - t_ref[...]
        dy_ref[...] = err * (1.0 / d)
        e8 = _fold8(err * err)
        part = e8[:, 0:LANES]
        for c in range(1, d // LANES):
            part = part + e8[:, LANES * c:LANES * (c + 1)]

        @pl.when(pl.program_id(0) == 0)
        def _():
            acc_ref[...] = jnp.zeros_like(acc_ref)

        acc_ref[...] += part

    return pl.pallas_call(
        body, name="loss_grad", grid=(t // tm,),
        in_specs=[_row_spec(tm, d), _row_spec(tm, d)],
        out_specs=[_row_spec(tm, d), _acc_spec(LANES)],
        out_shape=[jax.ShapeDtypeStruct((t, d), F32), jax.ShapeDtypeStruct((SUBLANES, LANES), F32)],
        compiler_params=_params("arbitrary"),
    )(y, target)


_MESH_ID = pl.DeviceIdType.MESH
_ANY = pl.BlockSpec(memory_space=pl.ANY)


def _all_gather(x):
    r, c_ = x.shape

    def body(x_ref, out_ref, send_sems, recv_sems, local_sem):
        mx, my, mc = lax.axis_index("x"), lax.axis_index("y"), lax.axis_index("c")
        me, sibling = (mx, my, mc), (mx, my, 1 - mc)
        chips = [(1 - mx, my), (mx, 1 - my), (1 - mx, 1 - my)]

        def slot(px, py, pc):
            return out_ref.at[4 * px + 2 * py + pc]

        def copy(sem, block, to, src=None):
            return pltpu.make_async_remote_copy(
                src_ref=slot(*block) if src is None else src, dst_ref=slot(*block),
                send_sem=send_sems.at[sem], recv_sem=recv_sems.at[sem], device_id=to, device_id_type=_MESH_ID)

        mine = pltpu.make_async_copy(x_ref, slot(*me), local_sem)
        mine.start()
        first = [copy(0, me, sibling, src=x_ref)]
        first += [copy(1 + j, me, (*chip, mc), src=x_ref) for j, chip in enumerate(chips)]
        for cp in first:
            cp.start()
        passed = [copy(4 + j, (*chip, mc), sibling) for j, chip in enumerate(chips)]
        for j, chip in enumerate(chips):
            copy(1 + j, (*chip, mc), me).wait_recv()
            passed[j].start()
        copy(0, sibling, me).wait_recv()
        for j, chip in enumerate(chips):
            copy(4 + j, (*chip, 1 - mc), me).wait_recv()
        for cp in first + passed:
            cp.wait_send()
        mine.wait()

    return pl.pallas_call(
        body, name="weight_all_gather",
        out_shape=jax.ShapeDtypeStruct((N_DEV, r, c_), x.dtype),
        in_specs=[_ANY], out_specs=_ANY,
        scratch_shapes=[pltpu.SemaphoreType.DMA((7,)), pltpu.SemaphoreType.DMA((7,)), pltpu.SemaphoreType.DMA],
    )(x)


def _all_to_all(send):
    _, r, c_ = send.shape

    def body(s_ref, r_ref, send_sems, recv_sems, local_sem):
        mx, my, mc = lax.axis_index("x"), lax.axis_index("y"), lax.axis_index("c")
        me = 4 * mx + 2 * my + mc
        local = pltpu.make_async_copy(s_ref.at[me], r_ref.at[me], local_sem)
        local.start()
        copies = []
        for rel in range(1, N_DEV):
            px = 1 - mx if rel & 4 else mx
            py = 1 - my if rel & 2 else my
            pc = 1 - mc if rel & 1 else mc
            peer = 4 * px + 2 * py + pc
            cp = pltpu.make_async_remote_copy(
                src_ref=s_ref.at[peer], dst_ref=r_ref.at[me], send_sem=send_sems.at[rel - 1],
                recv_sem=recv_sems.at[rel - 1], device_id=(px, py, pc), device_id_type=_MESH_ID)
            cp.start()
            copies.append(cp)
        for cp in copies:
            cp.wait_send()
            cp.wait_recv()
        local.wait()

    return pl.pallas_call(
        body, name="grad_all_to_all",
        out_shape=jax.ShapeDtypeStruct(send.shape, send.dtype),
        in_specs=[_ANY], out_specs=_ANY,
        scratch_shapes=[pltpu.SemaphoreType.DMA((7,)), pltpu.SemaphoreType.DMA((7,)), pltpu.SemaphoreType.DMA],
    )(send)


def _adamw(recv, w, m, v):
    r = w.shape[0]
    tr = _tile(r, PACK_ROW_TILE)

    def body(g_ref, w_ref, m_ref, v_ref, go_ref, d_ref, mo_ref, vo_ref):
        g = g_ref[0]
        for s in range(1, N_DEV):
            g = g + g_ref[s]
        go_ref[...] = g
        mn = ADAM_B1 * m_ref[...] + (1.0 - ADAM_B1) * g
        vn = ADAM_B2 * v_ref[...] + (1.0 - ADAM_B2) * (g * g)
        mo_ref[...] = mn
        vo_ref[...] = vn
        m_hat = mn / (1.0 - ADAM_B1 ** ADAM_STEP)
        v_hat = vn / (1.0 - ADAM_B2 ** ADAM_STEP)
        d_ref[...] = -ADAM_LR * (m_hat / (jnp.sqrt(v_hat) + ADAM_EPS) + ADAM_WD * w_ref[...])

    spec = pl.BlockSpec((tr, LANES), lambda i: (i, 0))
    out = jax.ShapeDtypeStruct((r, LANES), F32)
    return pl.pallas_call(
        body, name="grad_sum_adamw", grid=(r // tr,),
        in_specs=[pl.BlockSpec((N_DEV, tr, LANES), lambda i: (0, i, 0)), spec, spec, spec],
        out_specs=[spec, spec, spec, spec], out_shape=[out, out, out, out],
        compiler_params=_params("parallel"),
    )(recv, w, m, v)


def _pad_cols(a, before, after):
    parts = []
    if before:
        parts.append(jnp.zeros(a.shape[:-1] + (before,), a.dtype))
    parts.append(a)
    if after:
        parts.append(jnp.zeros(a.shape[:-1] + (after,), a.dtype))
    return jnp.concatenate(parts, axis=-1)


def _q_head_pairs(a, axis):
    shp = a.shape
    a = a.reshape(shp[:axis] + (GQA_KV_HEADS, GQA_GROUP, HEAD_DIM) + shp[axis + 1:])
    a = jnp.swapaxes(a, axis, axis + 1)
    return a.reshape(shp)


def _q_head_unpairs(a, axis):
    shp = a.shape
    a = a.reshape(shp[:axis] + (GQA_GROUP, GQA_KV_HEADS, HEAD_DIM) + shp[axis + 1:])
    a = jnp.swapaxes(a, axis, axis + 1)
    return a.reshape(shp)


def _layout_weights(w):
    w_in = w["w_in"]
    lead = w_in.shape[:-1]
    w_in_p = jnp.concatenate([
        _q_head_pairs(w_in[..., 0:512], w_in.ndim - 1),
        w_in[..., 512:1408],
        _pad_cols(w_in[..., 1408:1440], KR_LANE0, LANES - KR_LANE0 - MLA_ROPE_DIM),
        w_in[..., 1440:],
    ], axis=-1)
    wq = w["w_q_up"]
    wq_p = _pad_cols(wq.reshape(wq.shape[:-1] + (MLA_HEADS, MLA_QK_DIM)), 0, LANES - MLA_QK_DIM)
    wq_p = wq_p.reshape(wq.shape[:-1] + (MLA_HEADS * LANES,))
    wkv = w["w_kv_up"]
    wkv4 = wkv.reshape(wkv.shape[:-1] + (MLA_HEADS, 2 * HEAD_DIM))
    wk_p = _pad_cols(wkv4[..., :HEAD_DIM], 0, LANES - HEAD_DIM).reshape(wkv.shape[:-1] + (MLA_HEADS * LANES,))
    wv_p = wkv4[..., HEAD_DIM:].reshape(wkv.shape[:-1] + (MLA_HEADS * HEAD_DIM,))
    del lead
    return {
        "w_in": w_in_p, "w_q_up": wq_p, "w_kv_up": jnp.concatenate([wk_p, wv_p], axis=-1),
        "w_branch_a": _q_head_pairs(w["w_branch_a"], w["w_branch_a"].ndim - 2), "w_branch_b": w["w_branch_b"],
        "w_o": w["w_o"], "w_ffn_up": w["w_ffn_up"], "w_ffn_down": w["w_ffn_down"],
    }


def _unlayout_grads(g):
    gi = g["w_in"]
    kr0 = Z_KR + KR_LANE0
    g_in = jnp.concatenate([
        _q_head_unpairs(gi[..., 0:512], gi.ndim - 1), gi[..., 512:1408], gi[..., kr0:kr0 + MLA_ROPE_DIM],
        gi[..., Z_GATE:],
    ], axis=-1)
    gq = g["w_q_up"]
    gq = gq.reshape(gq.shape[:-1] + (MLA_HEADS, LANES))[..., :MLA_QK_DIM]
    gq = gq.reshape(gq.shape[:-2] + (MLA_HEADS * MLA_QK_DIM,))
    gkv = g["w_kv_up"]
    gk = gkv[..., :MLA_HEADS * LANES].reshape(gkv.shape[:-1] + (MLA_HEADS, LANES))[..., :HEAD_DIM]
    gv = gkv[..., MLA_HEADS * LANES:].reshape(gkv.shape[:-1] + (MLA_HEADS, HEAD_DIM))
    gkv = jnp.concatenate([gk, gv], axis=-1).reshape(gkv.shape[:-1] + (MLA_HEADS * 2 * HEAD_DIM,))
    return {
        "w_in": g_in, "w_q_up": gq, "w_kv_up": gkv,
        "w_branch_a": _q_head_unpairs(g["w_branch_a"], g["w_branch_a"].ndim - 2), "w_branch_b": g["w_branch_b"],
        "w_o": g["w_o"], "w_ffn_up": g["w_ffn_up"], "w_ffn_down": g["w_ffn_down"],
    }


def _pack_rows(flat_parts, lead=()):
    n = sum(p.shape[-1] for p in flat_parts)
    per = PACK_ROW_TILE * LANES
    pad = (-n) % per
    if pad:
        flat_parts = list(flat_parts) + [jnp.zeros(lead + (pad,), flat_parts[0].dtype)]
    flat = jnp.concatenate(flat_parts, axis=-1)
    return flat.reshape(lead + ((n + pad) // LANES, LANES))


def _unpack_rows(packed, shapes):
    flat = packed.reshape(-1)
    out, off = [], 0
    for shp in shapes:
        n = int(np.prod(shp))
        out.append(flat[off:off + n].reshape(shp))
        off += n
    return out


def _shards_of(full, axis):
    shp = full.shape
    cut = shp[:axis] + (N_DEV, shp[axis] // N_DEV) + shp[axis + 1:]
    return jnp.moveaxis(full.reshape(cut), axis, 0).reshape(N_DEV, -1)


def _from_shards(flat, shard_shape, axis):
    parts = jnp.moveaxis(flat.reshape((N_DEV,) + tuple(shard_shape)), 0, axis)
    full = list(shard_shape)
    full[axis] *= N_DEV
    return parts.reshape(full)


def _layer_fwd(x, u, lw, tabs):
    cos_a, sin_a, cos_b, sin_b = tabs
    z = _matmul(u, lw["w_in"], "nn", "mm_in")
    qa, ka, va, cqn, ckvn, krr = _prep_a_fwd(z, lw["gq2"], lw["gk2"], lw["gqa"], lw["gkva"], cos_a, sin_a, cos_b, sin_b)
    qb = _matmul(cqn, lw["w_q_up"], "nn", "mm_q_up")
    kvb = _matmul(ckvn, lw["w_kv_up"], "nn", "mm_kv_up")
    q_b, k_b, v_b = _prep_b_fwd(qb, kvb, krr, cos_b, sin_b)
    ya, lse_a = _attn_fwd(qa, ka, va, True, "gqa_fwd")
    yb, lse_b = _attn_fwd(q_b, k_b, v_b, False, "mla_fwd")
    ta = _matmul(ya, lw["w_branch_a"], "nn", "mm_branch_a")
    tb = _matmul(yb, lw["w_branch_b"], "nn", "mm_branch_b")
    merged = _merge_fwd(z, lw["b_gate"], ta, tb)
    m = _matmul(merged, lw["w_o"], "nn", "mm_o")
    x2, u2 = _res_norm_fwd(x, m, lw["post_mix_g"], lw["pre_ffn_g"])
    h = _matmul(u2, lw["w_ffn_up"], "nn", "mm_ffn_up")
    a = _relu2_fwd(h)
    f = _matmul(a, lw["w_ffn_down"], "nn", "mm_ffn_down")
    x3, u_next = _res_norm_fwd(x2, f, lw["post_ffn_g"], lw["next_pre_mix_g"])
    saved = dict(u=u, z=z, qa=qa, ka=ka, va=va, cqn=cqn, ckvn=ckvn, q_b=q_b, k_b=k_b, v_b=v_b, ya=ya, yb=yb,
                 lse_a=lse_a, lse_b=lse_b, ta=ta, tb=tb, merged=merged, m=m, x2=x2, u2=u2, h=h, a=a, f=f, x3=x3)
    return x3, u_next, saved


def _layer_bwd(dx3, du_next, lw, sv, tabs):
    cos_a, sin_a, cos_b, sin_b = tabs
    g = {}
    dx3, df, dg4, dg1n = _res_norm_bwd(sv["x3"], sv["f"], lw["post_ffn_g"], lw["next_pre_mix_g"], dx3, du_next)
    g["post_ffn_g"], g["next_pre_mix_g"] = dg4, dg1n
    da = _matmul(df, lw["w_ffn_down"], "nt", "mm_d_a")
    g["w_ffn_down"] = _matmul(sv["a"], df, "tn", "mm_dw_ffn_down")
    dh = _relu2_bwd(da, sv["h"])
    du2 = _matmul(dh, lw["w_ffn_up"], "nt", "mm_d_u2")
    g["w_ffn_up"] = _matmul(sv["u2"], dh, "tn", "mm_dw_ffn_up")
    dx2, dm, dg2, dg3 = _res_norm_bwd(sv["x2"], sv["m"], lw["post_mix_g"], lw["pre_ffn_g"], dx3, du2)
    g["post_mix_g"], g["pre_ffn_g"] = dg2, dg3
    dmg = _matmul(dm, lw["w_o"], "nt", "mm_d_merged")
    g["w_o"] = _matmul(sv["merged"], dm, "tn", "mm_dw_o")
    dta, dtb, dzg_a, dzg_b, db_a, db_b = _merge_bwd(dmg, sv["z"], lw["b_gate"], sv["ta"], sv["tb"])
    g["b_gate"] = jnp.concatenate([db_a, db_b], axis=-1)
    dya = _matmul(dta, lw["w_branch_a"], "nt", "mm_d_ya")
    g["w_branch_a"] = _matmul(sv["ya"], dta, "tn", "mm_dw_branch_a")
    dyb = _matmul(dtb, lw["w_branch_b"], "nt", "mm_d_yb")
    g["w_branch_b"] = _matmul(sv["yb"], dtb, "tn", "mm_dw_branch_b")
    delta_a, dya16 = _attn_delta(dya, sv["ya"])
    delta_b, dyb16 = _attn_delta(dyb, sv["yb"])
    dqa, dka4, dva4 = _attn_bwd(sv["qa"], sv["ka"], sv["va"], dya16, sv["lse_a"], delta_a, True, "gqa_bwd")
    dq_b, dk_b, dv_b = _attn_bwd(sv["q_b"], sv["k_b"], sv["v_b"], dyb16, sv["lse_b"], delta_b, False, "mla_bwd")
    dqb, dkvb, dkr = _prep_b_bwd(dq_b, dk_b, dv_b, cos_b, sin_b)
    dcqn = _matmul(dqb, lw["w_q_up"], "nt", "mm_d_cqn")
    g["w_q_up"] = _matmul(sv["cqn"], dqb, "tn", "mm_dw_q_up")
    dckvn = _matmul(dkvb, lw["w_kv_up"], "nt", "mm_d_ckvn")
    g["w_kv_up"] = _matmul(sv["ckvn"], dkvb, "tn", "mm_dw_kv_up")
    dz, dgq, dgk, dgqa, dgkva = _prep_a_bwd(sv["z"], dqa, dka4, dva4, dcqn, dckvn, dkr, dzg_a, dzg_b, lw["gq2"],
                                            lw["gk2"], lw["gqa"], lw["gkva"], cos_a, sin_a)
    g["q_norm_g"], g["k_norm_g"], g["q_a_norm_g"], g["kv_a_norm_g"] = dgq, dgk, dgqa, dgkva
    du = _matmul(dz, lw["w_in"], "nt", "mm_d_u")
    g["w_in"] = _matmul(sv["u"], dz, "tn", "mm_dw_in")
    return dx2, du, g


def kernel(x, w_in, b_gate, q_norm_g, k_norm_g, q_a_norm_g, kv_a_norm_g, w_q_up, w_kv_up, w_branch_a, w_branch_b, w_o, w_ffn_up, w_ffn_down, pre_mix_g, post_mix_g, pre_ffn_g, post_ffn_g, loss_target, m_w_in, m_b_gate, m_q_norm_g, m_k_norm_g, m_q_a_norm_g, m_kv_a_norm_g, m_w_q_up, m_w_kv_up, m_w_branch_a, m_w_branch_b, m_w_o, m_w_ffn_up, m_w_ffn_down, m_pre_mix_g, m_post_mix_g, m_pre_ffn_g, m_post_ffn_g, v_w_in, v_b_gate, v_q_norm_g, v_k_norm_g, v_q_a_norm_g, v_kv_a_norm_g, v_w_q_up, v_w_kv_up, v_w_branch_a, v_w_branch_b, v_w_o, v_w_ffn_up, v_w_ffn_down, v_pre_mix_g, v_post_mix_g, v_pre_ffn_g, v_post_ffn_g):
    weights = dict(zip(WEIGHT_NAMES, (w_in, b_gate, q_norm_g, k_norm_g, q_a_norm_g, kv_a_norm_g, w_q_up, w_kv_up,
                                      w_branch_a, w_branch_b, w_o, w_ffn_up, w_ffn_down, pre_mix_g, post_mix_g,
                                      pre_ffn_g, post_ffn_g)))
    mom_m = dict(zip(WEIGHT_NAMES, (m_w_in, m_b_gate, m_q_norm_g, m_k_norm_g, m_q_a_norm_g, m_kv_a_norm_g, m_w_q_up,
                                    m_w_kv_up, m_w_branch_a, m_w_branch_b, m_w_o, m_w_ffn_up, m_w_ffn_down,
                                    m_pre_mix_g, m_post_mix_g, m_pre_ffn_g, m_post_ffn_g)))
    mom_v = dict(zip(WEIGHT_NAMES, (v_w_in, v_b_gate, v_q_norm_g, v_k_norm_g, v_q_a_norm_g, v_kv_a_norm_g, v_w_q_up,
                                    v_w_kv_up, v_w_branch_a, v_w_branch_b, v_w_o, v_w_ffn_up, v_w_ffn_down,
                                    v_pre_mix_g, v_post_mix_g, v_pre_ffn_g, v_post_ffn_g)))
    assert x.shape[0] == 1 and x.shape[2] == D_MODEL, x.shape
    n_layers = w_in.shape[0]
    t = x.shape[1]
    x0 = x.reshape(t, D_MODEL)
    target = loss_target.reshape(t, D_MODEL)
    shard_shapes = {n: weights[n].shape for n in BIG_NAMES}
    small_shapes = [weights[n].shape for n in SMALL_NAMES]

    packed = _pack_rows([weights[n].astype(BF16).reshape(-1) for n in BIG_NAMES])
    gathered = _all_gather(packed).reshape(N_DEV, -1)
    full, off = {}, 0
    for n in BIG_NAMES:
        cnt = int(np.prod(shard_shapes[n]))
        full[n] = _from_shards(gathered[:, off:off + cnt], shard_shapes[n], SHARD_AXIS[n])
        off += cnt
    lw_all = _layout_weights(full)
    lw_all["b_gate"] = b_gate.reshape(n_layers, 1, 2 * D_MODEL)
    lw_all["gq2"] = jnp.tile(q_norm_g, (1, 2)).reshape(n_layers, 1, LANES)
    lw_all["gk2"] = jnp.tile(k_norm_g, (1, 2)).reshape(n_layers, 1, LANES)
    lw_all["gqa"] = q_a_norm_g.reshape(n_layers, 1, MLA_Q_RANK)
    lw_all["gkva"] = kv_a_norm_g.reshape(n_layers, 1, MLA_KV_RANK)
    for n in ("post_mix_g", "pre_ffn_g", "post_ffn_g"):
        lw_all[n] = weights[n]
    lw_all["next_pre_mix_g"] = jnp.roll(pre_mix_g, -1, axis=0)

    tabs = _rope_tables(t)
    u0 = _rms_fwd(x0, pre_mix_g[0])

    layer_w = [{n: a[l] for n, a in lw_all.items()} for l in range(n_layers)]
    xc, uc, saved = x0, u0, []
    for l in range(n_layers):
        xc, uc, sv = _layer_fwd(xc, uc, layer_w[l], tabs)
        saved.append(sv)
    dy, loss_acc = _loss_grad(xc, target)
    loss = lax.psum(0.5 * jnp.sum(loss_acc) / D_MODEL, ("x", "y", "c"))

    dx0, du0, layer_g = dy, jnp.zeros((t, D_MODEL), F32), [None] * n_layers
    for l in reversed(range(n_layers)):
        dx0, du0, layer_g[l] = _layer_bwd(dx0, du0, layer_w[l], saved[l], tabs)
    grads = {n: jnp.stack([g[n] for g in layer_g]) for n in layer_g[0]}
    grad_x, dg1_first = _rms_bwd(x0, pre_mix_g[0], dx0, du0)

    big_grads = _unlayout_grads({n: grads[n] for n in BIG_NAMES})
    fold = lambda a: a.sum(axis=1)
    dgq = fold(grads["q_norm_g"]).reshape(n_layers, 2, HEAD_DIM).sum(axis=1)
    dgk = fold(grads["k_norm_g"]).reshape(n_layers, 2, HEAD_DIM).sum(axis=1)
    dg1 = jnp.concatenate([fold(dg1_first[None]), fold(grads["next_pre_mix_g"])[:-1]], axis=0)
    small_grads = {
        "b_gate": fold(grads["b_gate"]), "q_norm_g": dgq, "k_norm_g": dgk, "q_a_norm_g": fold(grads["q_a_norm_g"]),
        "kv_a_norm_g": fold(grads["kv_a_norm_g"]), "pre_mix_g": dg1, "post_mix_g": fold(grads["post_mix_g"]),
        "pre_ffn_g": fold(grads["pre_ffn_g"]), "post_ffn_g": fold(grads["post_ffn_g"]),
    }
    small_flat = jnp.concatenate([small_grads[n].reshape(-1) for n in SMALL_NAMES])
    send_parts = [_shards_of(big_grads[n], SHARD_AXIS[n]) for n in BIG_NAMES]
    send_parts.append(jnp.broadcast_to(small_flat[None], (N_DEV, small_flat.shape[0])))
    send = _pack_rows(send_parts, lead=(N_DEV,))
    recv = _all_to_all(send)

    def pack_state(d):
        return _pack_rows([d[n].reshape(-1) for n in BIG_NAMES] + [d[n].reshape(-1) for n in SMALL_NAMES])

    g_p, d_p, m_p, v_p = _adamw(recv, pack_state(weights), pack_state(mom_m), pack_state(mom_v))
    shapes = [shard_shapes[n] for n in BIG_NAMES] + small_shapes
    order = BIG_NAMES + SMALL_NAMES
    outs = []
    for packed_out in (g_p, d_p, m_p, v_p):
        by_name = dict(zip(order, _unpack_rows(packed_out, shapes)))
        outs.extend(by_name[n] for n in WEIGHT_NAMES)
    return (loss, grad_x.reshape(x.shape), *outs)
```

```python
import functools
import math

import jax
import jax.numpy as jnp
import numpy as np
from jax import lax
from jax.experimental import pallas as pl
from jax.experimental.pallas import tpu as pltpu

F32 = jnp.float32
BF16 = jnp.bfloat16

D_MODEL = 1024
GRID_W = 64
ROPE_THETA = 10000.0
EPS = 1e-6
GQA_HEADS = 8
GQA_KV_HEADS = 2
GQA_GROUP = GQA_HEADS // GQA_KV_HEADS
HEAD_DIM = 64
MLA_HEADS = 8
MLA_ROPE_DIM = 32
MLA_QK_DIM = 96
MLA_Q_RANK = 384
MLA_KV_RANK = 256
D_FF = 4 * D_MODEL
GQA_SCALE = 1.0 / math.sqrt(HEAD_DIM)
MLA_SCALE = 1.0 / math.sqrt(MLA_QK_DIM)
LOG2E = math.log2(math.e)
LN2 = math.log(2.0)

ADAM_LR = 0.001
ADAM_B1 = 0.9
ADAM_B2 = 0.999
ADAM_EPS = 1e-08
ADAM_WD = 0.01
ADAM_STEP = 10

N_DEV = 8
LANES = 128
SUBLANES = 8
VMEM_LIMIT = 48 * 1024 * 1024

Z_QA, Z_KA, Z_VA, Z_CQ, Z_CKV, Z_KR, Z_GATE = 0, 512, 640, 768, 1152, 1408, 1536
Z_ATT_W = 1536
Z_W = 3584
KR_LANE0 = 64

WEIGHT_NAMES = ("w_in", "b_gate", "q_norm_g", "k_norm_g", "q_a_norm_g", "kv_a_norm_g", "w_q_up", "w_kv_up",
                "w_branch_a", "w_branch_b", "w_o", "w_ffn_up", "w_ffn_down", "pre_mix_g", "post_mix_g",
                "pre_ffn_g", "post_ffn_g")
SHARD_AXIS = {"w_in": 2, "w_q_up": 2, "w_kv_up": 2, "w_branch_a": 2, "w_branch_b": 2, "w_o": 1, "w_ffn_up": 2,
              "w_ffn_down": 1}
BIG_NAMES = tuple(n for n in WEIGHT_NAMES if n in SHARD_AXIS)
SMALL_NAMES = tuple(n for n in WEIGHT_NAMES if n not in SHARD_AXIS)
ADAM_BLOCK_ELEMS = 256 * 1024
MM_TILE = 1024
ATTN_TQ = 1024
ATTN_TK = 512


def _params(*semantics):
    return pltpu.CompilerParams(dimension_semantics=semantics, vmem_limit_bytes=VMEM_LIMIT)


def _tile(n, pref):
    if n <= pref:
        return n
    t = (pref // LANES) * LANES
    while n % t:
        t -= LANES
    return t


def _fold8(t):
    return t.reshape(t.shape[0] // SUBLANES, SUBLANES, t.shape[1]).sum(axis=0)


_DIMS = {"nn": ((1,), (0,)), "nt": ((1,), (1,)), "tn": ((0,), (0,))}


def _matmul(a, b, mode, name):
    if mode == "nn":
        (m, k), n = a.shape, b.shape[1]
    elif mode == "nt":
        (m, k), n = a.shape, b.shape[0]
    else:
        (k, m), n = a.shape, b.shape[1]
    tm, tn, tk = _tile(m, MM_TILE), _tile(n, MM_TILE), _tile(k, MM_TILE)
    nk = k // tk
    dims = (_DIMS[mode], ((), ()))

    def body(a_ref, b_ref, o_ref, acc_ref):
        prod = lax.dot_general(a_ref[...], b_ref[...], dims, preferred_element_type=F32)
        if nk == 1:
            o_ref[...] = prod
        else:
            kk = pl.program_id(2)

            @pl.when(kk == 0)
            def _():
                acc_ref[...] = prod

            @pl.when(kk > 0)
            def _():
                acc_ref[...] += prod

            @pl.when(kk == nk - 1)
            def _():
                o_ref[...] = acc_ref[...]

    if mode == "tn":
        a_spec = pl.BlockSpec((tk, tm), lambda i, j, kk: (kk, i))
    else:
        a_spec = pl.BlockSpec((tm, tk), lambda i, j, kk: (i, kk))
    if mode == "nt":
        b_spec = pl.BlockSpec((tn, tk), lambda i, j, kk: (j, kk))
    else:
        b_spec = pl.BlockSpec((tk, tn), lambda i, j, kk: (kk, j))
    return pl.pallas_call(
        body,
        name=name,
        grid=(m // tm, n // tn, nk),
        in_specs=[a_spec, b_spec],
        out_specs=pl.BlockSpec((tm, tn), lambda i, j, kk: (i, j)),
        out_shape=jax.ShapeDtypeStruct((m, n), F32),
        scratch_shapes=[pltpu.VMEM((tm, tn), F32)],
        compiler_params=_params("parallel", "parallel", "arbitrary"),
    )(a, b)


def _rinv(x):
    return lax.rsqrt(jnp.mean(x * x, axis=-1, keepdims=True) + EPS)


def _rms_bwd_rows(x, g, dy):
    r = _rinv(x)
    xh = x * r
    dxh = dy * g
    dx = r * (dxh - xh * jnp.mean(dxh * xh, axis=-1, keepdims=True))
    return dx, dy * xh


def _row_spec(tm, c):
    return pl.BlockSpec((tm, c), lambda i: (i, 0))


def _vec_spec(c):
    return pl.BlockSpec((1, c), lambda i: (0, 0))


def _acc_spec(c):
    return pl.BlockSpec((SUBLANES, c), lambda i: (0, 0))


def _rms_fwd(x, g):
    t, d = x.shape
    tm = _tile(t, 512)

    def body(x_ref, g_ref, o_ref):
        xv = x_ref[...]
        o_ref[...] = (xv * _rinv(xv) * g_ref[...]).astype(BF16)

    return pl.pallas_call(
        body, name="rms_fwd", grid=(t // tm,),
        in_specs=[_row_spec(tm, d), _vec_spec(d)], out_specs=_row_spec(tm, d),
        out_shape=jax.ShapeDtypeStruct((t, d), BF16), compiler_params=_params("parallel"),
    )(x, g.reshape(1, d))


def _rms_bwd(x, g, dres, dy):
    t, d = x.shape
    tm = _tile(t, 512)

    def body(x_ref, g_ref, dres_ref, dy_ref, dx_ref, dg_ref):
        dx, dgc = _rms_bwd_rows(x_ref[...], g_ref[...], dy_ref[...])
        dx_ref[...] = dres_ref[...] + dx

        @pl.when(pl.program_id(0) == 0)
        def _():
            dg_ref[...] = jnp.zeros_like(dg_ref)

        dg_ref[...] += _fold8(dgc)

    return pl.pallas_call(
        body, name="rms_bwd", grid=(t // tm,),
        in_specs=[_row_spec(tm, d), _vec_spec(d), _row_spec(tm, d), _row_spec(tm, d)],
        out_specs=[_row_spec(tm, d), _acc_spec(d)],
        out_shape=[jax.ShapeDtypeStruct((t, d), F32), jax.ShapeDtypeStruct((SUBLANES, d), F32)],
        compiler_params=_params("arbitrary"),
    )(x, g.reshape(1, d), dres, dy)


def _res_norm_fwd(x, m, g_post, g_next):
    t, d = x.shape
    tm = _tile(t, 512)

    def body(x_ref, m_ref, gp_ref, gn_ref, x2_ref, u2_ref):
        mv = m_ref[...]
        x2 = x_ref[...] + mv * _rinv(mv) * gp_ref[...]
        x2_ref[...] = x2
        u2_ref[...] = (x2 * _rinv(x2) * gn_ref[...]).astype(BF16)

    return pl.pallas_call(
        body, name="res_norm_fwd", grid=(t // tm,),
        in_specs=[_row_spec(tm, d), _row_spec(tm, d), _vec_spec(d), _vec_spec(d)],
        out_specs=[_row_spec(tm, d), _row_spec(tm, d)],
        out_shape=[jax.ShapeDtypeStruct((t, d), F32), jax.ShapeDtypeStruct((t, d), BF16)],
        compiler_params=_params("parallel"),
    )(x, m, g_post.reshape(1, d), g_next.reshape(1, d))


def _res_norm_bwd(x2, m, g_post, g_next, dx2_in, du2):
    t, d = x2.shape
    tm = _tile(t, 512)

    def body(x2_ref, m_ref, gp_ref, gn_ref, dx2in_ref, du2_ref, dx2_ref, dm_ref, dgp_ref, dgn_ref):
        dxn, dgn_c = _rms_bwd_rows(x2_ref[...], gn_ref[...], du2_ref[...])
        dx2 = dx2in_ref[...] + dxn
        dx2_ref[...] = dx2
        dm, dgp_c = _rms_bwd_rows(m_ref[...], gp_ref[...], dx2)
        dm_ref[...] = dm.astype(BF16)

        @pl.when(pl.program_id(0) == 0)
        def _():
            dgp_ref[...] = jnp.zeros_like(dgp_ref)
            dgn_ref[...] = jnp.zeros_like(dgn_ref)

        dgp_ref[...] += _fold8(dgp_c)
        dgn_ref[...] += _fold8(dgn_c)

    return pl.pallas_call(
        body, name="res_norm_bwd", grid=(t // tm,),
        in_specs=[_row_spec(tm, d), _row_spec(tm, d), _vec_spec(d), _vec_spec(d), _row_spec(tm, d), _row_spec(tm, d)],
        out_specs=[_row_spec(tm, d), _row_spec(tm, d), _acc_spec(d), _acc_spec(d)],
        out_shape=[jax.ShapeDtypeStruct((t, d), F32), jax.ShapeDtypeStruct((t, d), BF16),
                   jax.ShapeDtypeStruct((SUBLANES, d), F32), jax.ShapeDtypeStruct((SUBLANES, d), F32)],
        compiler_params=_params("arbitrary"),
    )(x2, m, g_post.reshape(1, d), g_next.reshape(1, d), dx2_in, du2)


def _rope_tables(t):
    rows = t // GRID_W
    row = jnp.repeat(jnp.arange(rows, dtype=F32), GRID_W)
    col = jnp.tile(jnp.arange(GRID_W, dtype=F32), rows)

    def tab(rot_dim):
        half = rot_dim // 2
        inv = ROPE_THETA ** (-jnp.arange(0, half, 2, dtype=F32) / half)
        ar = row[:, None] * inv[None, :]
        ac = col[:, None] * inv[None, :]
        ang = jnp.concatenate([ar, ar, ac, ac], axis=-1)
        q = half // 2
        sign = np.tile(np.concatenate([-np.ones(q, np.float32), np.ones(q, np.float32)]), 2)
        return jnp.cos(ang), jnp.sin(ang) * sign[None, :]

    ca, sa = tab(HEAD_DIM)
    cb, sb = tab(MLA_ROPE_DIM)
    one = jnp.ones((t, 1), F32)
    cos_b = jnp.concatenate([one * jnp.ones((1, KR_LANE0), F32), cb, one * jnp.ones((1, 32), F32)], axis=-1)
    sin_b = jnp.concatenate([jnp.zeros((t, KR_LANE0), F32), sb, jnp.zeros((t, 32), F32)], axis=-1)
    return jnp.tile(ca, (1, GQA_HEADS)), jnp.tile(sa, (1, GQA_HEADS)), cos_b, sin_b


def _swap_halves(x, sh):
    lane = lax.broadcasted_iota(jnp.int32, x.shape, 1)
    up = pltpu.roll(x, LANES - sh, 1)
    dn = pltpu.roll(x, sh, 1)
    return jnp.where((lane & (2 * sh - 1)) < sh, up, dn)


def _rope(x, cos, sin_s, sh):
    return x * cos + _swap_halves(x, sh) * sin_s


def _rope_bwd(dy, cos, sin_s, sh):
    return dy * cos + _swap_halves(dy * sin_s, sh)


def _lo_mask(shape):
    return lax.broadcasted_iota(jnp.int32, shape, 1) < HEAD_DIM


def _half_mean(t, lo):
    s_lo = jnp.sum(jnp.where(lo, t, 0.0), axis=-1, keepdims=True)
    s_hi = jnp.sum(jnp.where(lo, 0.0, t), axis=-1, keepdims=True)
    return jnp.where(lo, s_lo, s_hi) * (1.0 / HEAD_DIM)


def _head_norm(x, g2):
    lo = _lo_mask(x.shape)
    r = lax.rsqrt(_half_mean(x * x, lo) + EPS)
    return x * r * g2


def _head_norm_bwd(x, g2, dy):
    lo = _lo_mask(x.shape)
    r = lax.rsqrt(_half_mean(x * x, lo) + EPS)
    xh = x * r
    dxh = dy * g2
    dx = r * (dxh - xh * _half_mean(dxh * xh, lo))
    return dx, dy * xh


def _prep_a_fwd(z, gq2, gk2, gqa, gkva, cos_a, sin_a, cos_b, sin_b):
    t = z.shape[0]
    tm = _tile(t, 256)

    def body(z_ref, gq_ref, gk_ref, gqa_ref, gkva_ref, ca_ref, sa_ref, cb_ref, sb_ref,
             qa_ref, ka_ref, va_ref, cqn_ref, ckvn_ref, krr_ref):
        for j in range(4):
            cols = slice(LANES * j, LANES * (j + 1))
            y = _rope(_head_norm(z_ref[:, cols], gq_ref[...]), ca_ref[:, cols], sa_ref[:, cols], 16)
            qa_ref[:, cols] = (y * (GQA_SCALE * LOG2E)).astype(BF16)
        y = _rope(_head_norm(z_ref[:, Z_KA:Z_VA], gk_ref[...]), ca_ref[:, :LANES], sa_ref[:, :LANES], 16)
        ka_ref[...] = y.astype(BF16)
        va_ref[...] = z_ref[:, Z_VA:Z_CQ].astype(BF16)
        cq = z_ref[:, Z_CQ:Z_CKV]
        cqn_ref[...] = (cq * _rinv(cq) * gqa_ref[...]).astype(BF16)
        ckv = z_ref[:, Z_CKV:Z_KR]
        ckvn_ref[...] = (ckv * _rinv(ckv) * gkva_ref[...]).astype(BF16)
        krr_ref[...] = _rope(z_ref[:, Z_KR:Z_GATE], cb_ref[...], sb_ref[...], 8)

    return pl.pallas_call(
        body, name="prep_a_fwd", grid=(t // tm,),
        in_specs=[_row_spec(tm, Z_ATT_W), _vec_spec(LANES), _vec_spec(LANES), _vec_spec(MLA_Q_RANK),
                  _vec_spec(MLA_KV_RANK), _row_spec(tm, 512), _row_spec(tm, 512), _row_spec(tm, LANES),
                  _row_spec(tm, LANES)],
        out_specs=[_row_spec(tm, 512), _row_spec(tm, LANES), _row_spec(tm, LANES), _row_spec(tm, MLA_Q_RANK),
                   _row_spec(tm, MLA_KV_RANK), _row_spec(tm, LANES)],
        out_shape=[jax.ShapeDtypeStruct((t, 512), BF16), jax.ShapeDtypeStruct((t, LANES), BF16),
                   jax.ShapeDtypeStruct((t, LANES), BF16), jax.ShapeDtypeStruct((t, MLA_Q_RANK), BF16),
                   jax.ShapeDtypeStruct((t, MLA_KV_RANK), BF16), jax.ShapeDtypeStruct((t, LANES), F32)],
        compiler_params=_params("parallel"),
    )(z, gq2, gk2, gqa, gkva, cos_a, sin_a, cos_b, sin_b)


def _prep_a_bwd(z, dqa, dka4, dva4, dcqn, dckvn, dkr, dzga, dzgb, gq2, gk2, gqa, gkva, cos_a, sin_a):
    t = z.shape[0]
    tm = _tile(t, 256)

    def body(z_ref, dqa_ref, dka_ref, dva_ref, dcqn_ref, dckvn_ref, dkr_ref, dzga_ref, dzgb_ref, gq_ref, gk_ref,
             gqa_ref, gkva_ref, ca_ref, sa_ref, dz_ref, dgq_ref, dgk_ref, dgqa_ref, dgkva_ref):
        @pl.when(pl.program_id(0) == 0)
        def _():
            dgq_ref[...] = jnp.zeros_like(dgq_ref)
            dgk_ref[...] = jnp.zeros_like(dgk_ref)
            dgqa_ref[...] = jnp.zeros_like(dgqa_ref)
            dgkva_ref[...] = jnp.zeros_like(dgkva_ref)

        dgq = jnp.zeros((SUBLANES, LANES), F32)
        for j in range(4):
            cols = slice(LANES * j, LANES * (j + 1))
            dy = _rope_bwd(dqa_ref[:, cols] * GQA_SCALE, ca_ref[:, cols], sa_ref[:, cols], 16)
            dx, dgc = _head_norm_bwd(z_ref[:, cols], gq_ref[...], dy)
            dz_ref[:, cols] = dx.astype(BF16)
            dgq = dgq + _fold8(dgc)
        dgq_ref[...] += dgq
        dk = (dka_ref[0] + dka_ref[1] + dka_ref[2] + dka_ref[3]) * LN2
        dy = _rope_bwd(dk, ca_ref[:, :LANES], sa_ref[:, :LANES], 16)
        dx, dgc = _head_norm_bwd(z_ref[:, Z_KA:Z_VA], gk_ref[...], dy)
        dz_ref[:, Z_KA:Z_VA] = dx.astype(BF16)
        dgk_ref[...] += _fold8(dgc)
        dz_ref[:, Z_VA:Z_CQ] = (dva_ref[0] + dva_ref[1] + dva_ref[2] + dva_ref[3]).astype(BF16)
        dx, dgc = _rms_bwd_rows(z_ref[:, Z_CQ:Z_CKV], gqa_ref[...], dcqn_ref[...])
        dz_ref[:, Z_CQ:Z_CKV] = dx.astype(BF16)
        dgqa_ref[...] += _fold8(dgc)
        dx, dgc = _rms_bwd_rows(z_ref[:, Z_CKV:Z_KR], gkva_ref[...], dckvn_ref[...])
        dz_ref[:, Z_CKV:Z_KR] = dx.astype(BF16)
        dgkva_ref[...] += _fold8(dgc)
        dz_ref[:, Z_KR:Z_GATE] = dkr_ref[...].astype(BF16)
        dz_ref[:, Z_GATE:Z_GATE + D_MODEL] = dzga_ref[...]
        dz_ref[:, Z_GATE + D_MODEL:Z_W] = dzgb_ref[...]

    part = pl.BlockSpec((4, tm, LANES), lambda i: (0, i, 0))
    return pl.pallas_call(
        body, name="prep_a_bwd", grid=(t // tm,),
        in_specs=[_row_spec(tm, Z_ATT_W), _row_spec(tm, 512), part, part, _row_spec(tm, MLA_Q_RANK),
                  _row_spec(tm, MLA_KV_RANK), _row_spec(tm, LANES), _row_spec(tm, D_MODEL), _row_spec(tm, D_MODEL),
                  _vec_spec(LANES),
                  _vec_spec(LANES), _vec_spec(MLA_Q_RANK), _vec_spec(MLA_KV_RANK), _row_spec(tm, 512),
                  _row_spec(tm, 512)],
        out_specs=[_row_spec(tm, Z_W), _acc_spec(LANES), _acc_spec(LANES), _acc_spec(MLA_Q_RANK),
                   _acc_spec(MLA_KV_RANK)],
        out_shape=[jax.ShapeDtypeStruct((t, Z_W), BF16), jax.ShapeDtypeStruct((SUBLANES, LANES), F32),
                   jax.ShapeDtypeStruct((SUBLANES, LANES), F32), jax.ShapeDtypeStruct((SUBLANES, MLA_Q_RANK), F32),
                   jax.ShapeDtypeStruct((SUBLANES, MLA_KV_RANK), F32)],
        compiler_params=_params("arbitrary"),
    )(z, dqa, dka4, dva4, dcqn, dckvn, dkr, dzga, dzgb, gq2, gk2, gqa, gkva, cos_a, sin_a)


def _prep_b_fwd(qb, kvb, krr, cos_b, sin_b):
    t = qb.shape[0]
    tm = _tile(t, 256)

    def body(qb_ref, kvb_ref, krr_ref, cb_ref, sb_ref, q_ref, k_ref, v_ref):
        for h in range(MLA_HEADS):
            cols = slice(LANES * h, LANES * (h + 1))
            q_ref[:, cols] = (_rope(qb_ref[:, cols], cb_ref[...], sb_ref[...], 8) * (MLA_SCALE * LOG2E)).astype(BF16)
            k_ref[:, cols] = (kvb_ref[:, cols] + krr_ref[...]).astype(BF16)
        v_ref[...] = kvb_ref[:, 1024:1536].astype(BF16)

    return pl.pallas_call(
        body, name="prep_b_fwd", grid=(t // tm,),
        in_specs=[_row_spec(tm, 1024), _row_spec(tm, 1536), _row_spec(tm, LANES), _row_spec(tm, LANES),
                  _row_spec(tm, LANES)],
        out_specs=[_row_spec(tm, 1024), _row_spec(tm, 1024), _row_spec(tm, 512)],
        out_shape=[jax.ShapeDtypeStruct((t, 1024), BF16), jax.ShapeDtypeStruct((t, 1024), BF16),
                   jax.ShapeDtypeStruct((t, 512), BF16)],
        compiler_params=_params("parallel"),
    )(qb, kvb, krr, cos_b, sin_b)


def _prep_b_bwd(dq, dk, dv, cos_b, sin_b):
    t = dq.shape[0]
    tm = _tile(t, 256)

    def body(dq_ref, dk_ref, dv_ref, cb_ref, sb_ref, dqb_ref, dkvb_ref, dkr_ref):
        dkr = jnp.zeros((tm, LANES), F32)
        for h in range(MLA_HEADS):
            cols = slice(LANES * h, LANES * (h + 1))
            dqb_ref[:, cols] = _rope_bwd(dq_ref[:, cols] * MLA_SCALE, cb_ref[...], sb_ref[...], 8).astype(BF16)
            dkh = dk_ref[:, cols] * LN2
            dkvb_ref[:, cols] = dkh.astype(BF16)
            dkr = dkr + dkh
        dkvb_ref[:, 1024:1536] = dv_ref[...].astype(BF16)
        dkr_ref[...] = _rope_bwd(dkr, cb_ref[...], sb_ref[...], 8)

    return pl.pallas_call(
        body, name="prep_b_bwd", grid=(t // tm,),
        in_specs=[_row_spec(tm, 1024), _row_spec(tm, 1024), _row_spec(tm, 512), _row_spec(tm, LANES),
                  _row_spec(tm, LANES)],
        out_specs=[_row_spec(tm, 1024), _row_spec(tm, 1536), _row_spec(tm, LANES)],
        out_shape=[jax.ShapeDtypeStruct((t, 1024), BF16), jax.ShapeDtypeStruct((t, 1536), BF16),
                   jax.ShapeDtypeStruct((t, LANES), F32)],
        compiler_params=_params("parallel"),
    )(dq, dk, dv, cos_b, sin_b)


_NT = (((1,), (1,)), ((), ()))
_NN = (((1,), (0,)), ((), ()))
_TN = (((0,), (0,)), ((), ()))


def _head_operands(qv, kv, i, shared_k):
    if shared_k:
        lo = _lo_mask(qv.shape)
        keep = lo if i == 0 else jnp.logical_not(lo)
        return jnp.where(keep, qv, jnp.zeros_like(qv)), kv
    cols = slice(LANES * i, LANES * (i + 1))
    return qv[:, cols], kv[:, cols]


def _attn_specs(shared_k, tq, tk, q_of, k_of):
    wq = LANES if shared_k else 2 * LANES
    q_spec = pl.BlockSpec((tq, wq), lambda *g: (q_of(*g), g[0]))
    if shared_k:
        k_spec = pl.BlockSpec((tk, LANES), lambda *g: (k_of(*g), 0))
        v_spec = pl.BlockSpec((tk, LANES), lambda *g: (k_of(*g), 0))
    else:
        k_spec = pl.BlockSpec((tk, wq), lambda *g: (k_of(*g), g[0]))
        v_spec = pl.BlockSpec((tk, LANES), lambda *g: (k_of(*g), g[0]))
    return wq, q_spec, k_spec, v_spec


def _attn_fwd(q, k, v, shared_k, name):
    t = q.shape[0]
    tq, tk = _tile(t, ATTN_TQ), _tile(t, ATTN_TK)
    nq, nk = t // tq, t // tk
    wq, q_spec, k_spec, v_spec = _attn_specs(shared_k, tq, tk, lambda p, i, j: i, lambda p, i, j: j)
    groups = q.shape[1] // wq
    chunk = _tile(tq, 2 * LANES)

    def body(q_ref, k_ref, v_ref, o_ref, lse_ref, m_s, l_s, acc_s, alpha_s, s_s, p_s):
        kb = pl.program_id(2)

        @pl.when(kb == 0)
        def _():
            m_s[...] = jnp.full_like(m_s, -jnp.inf)
            l_s[...] = jnp.zeros_like(l_s)
            acc_s[...] = jnp.zeros_like(acc_s)

        qv, kv, vv = q_ref[...], k_ref[...], v_ref[...]
        for i in range(2):
            qi, ki = _head_operands(qv, kv, i, shared_k)
            s_s[i] = lax.dot_general(ki, qi, _NT, preferred_element_type=F32)
        for i in range(2):
            for c in range(tq // chunk):
                cols = slice(c * chunk, (c + 1) * chunk)
                m_prev = m_s[i, :, cols]
                m_new = jnp.maximum(m_prev, jnp.max(s_s[i, :, cols], axis=0, keepdims=True))
                alpha = jnp.exp2(m_prev - m_new)
                pt = jnp.exp2(s_s[i, :, cols] - m_new)
                l_s[i, :, cols] = alpha * l_s[i, :, cols] + jnp.sum(pt, axis=0, keepdims=True)
                m_s[i, :, cols] = m_new
                alpha_s[i, :, cols] = alpha
                p_s[i, :, cols] = pt.astype(BF16)
        for i in range(2):
            acc_s[i] = alpha_s[i] * acc_s[i] + lax.dot_general(vv, p_s[i], _TN, preferred_element_type=F32)

        @pl.when(kb == nk - 1)
        def _():
            o0 = acc_s[0] / l_s[0]
            o1 = acc_s[1] / l_s[1]
            row_lo = lax.broadcasted_iota(jnp.int32, o0.shape, 0) < HEAD_DIM
            o_ref[...] = jnp.where(row_lo, o0, o1).T.astype(BF16)
            lse_ref[0] = m_s[0] + jnp.log2(l_s[0])
            lse_ref[1] = m_s[1] + jnp.log2(l_s[1])

    return pl.pallas_call(
        body, name=name, grid=(groups, nq, nk),
        in_specs=[q_spec, k_spec, v_spec],
        out_specs=[pl.BlockSpec((tq, LANES), lambda p, i, j: (i, p)),
                   pl.BlockSpec((2, 1, tq), lambda p, i, j: (p, 0, i))],
        out_shape=[jax.ShapeDtypeStruct((t, LANES * groups), BF16),
                   jax.ShapeDtypeStruct((2 * groups, 1, t), F32)],
        scratch_shapes=[pltpu.VMEM((2, 1, tq), F32), pltpu.VMEM((2, 1, tq), F32), pltpu.VMEM((2, LANES, tq), F32),
                        pltpu.VMEM((2, 1, tq), F32), pltpu.VMEM((2, tk, tq), F32), pltpu.VMEM((2, tk, tq), BF16)],
        compiler_params=_params("parallel", "parallel", "arbitrary"),
    )(q, k, v)


def _attn_delta(do, o):
    t, w = do.shape
    tm = _tile(t, 512)
    groups = w // LANES

    def body(do_ref, o_ref, delta_ref, dob_ref):
        dov = do_ref[...]
        dob_ref[...] = dov.astype(BF16)
        prod = dov * o_ref[...].astype(F32)
        lane_lo = lax.broadcasted_iota(jnp.int32, (SUBLANES, LANES), 1) < HEAD_DIM
        masks = (lane_lo.astype(BF16), jnp.logical_not(lane_lo).astype(BF16))
        for g in range(groups):
            x = prod[:, LANES * g:LANES * (g + 1)]
            hi = x.astype(BF16)
            mid = (x - hi.astype(F32)).astype(BF16)
            for i in range(2):
                r = (lax.dot_general(masks[i], hi, _NT, preferred_element_type=F32)
                     + lax.dot_general(masks[i], mid, _NT, preferred_element_type=F32))
                delta_ref[2 * g + i] = r[0:1, :]

    return pl.pallas_call(
        body, name="attn_delta", grid=(t // tm,),
        in_specs=[_row_spec(tm, w), _row_spec(tm, w)],
        out_specs=[pl.BlockSpec((2 * groups, 1, tm), lambda i: (0, 0, i)), _row_spec(tm, w)],
        out_shape=[jax.ShapeDtypeStruct((2 * groups, 1, t), F32), jax.ShapeDtypeStruct((t, w), BF16)],
        compiler_params=_params("parallel"),
    )(do, o)


def _attn_bwd(q, k, v, do, lse, delta, shared_k, name):
    t = q.shape[0]
    tq, tk = _tile(t, ATTN_TQ), _tile(t, ATTN_TK)
    nq, nk = t // tq, t // tk
    wq, q_spec, k_spec, v_spec = _attn_specs(shared_k, tq, tk, lambda p, j, i: i, lambda p, j, i: j)
    groups = q.shape[1] // wq

    def body(q_ref, k_ref, v_ref, do_ref, lse_ref, delta_ref, dq_ref, dk_ref, dv_ref, dk_s, dv_s):
        kb, qb = pl.program_id(1), pl.program_id(2)

        @pl.when(qb == 0)
        def _():
            dk_s[...] = jnp.zeros_like(dk_s)
            dv_s[...] = jnp.zeros_like(dv_s)

        qv, kv, vv, dov = q_ref[...], k_ref[...], v_ref[...], do_ref[...]
        lo = _lo_mask(dov.shape)
        dq_parts = []
        for i in range(2):
            qi, ki = _head_operands(qv, kv, i, shared_k)
            keep = lo if i == 0 else jnp.logical_not(lo)
            doi = jnp.where(keep, dov, jnp.zeros_like(dov))
            st = lax.dot_general(ki, qi, _NT, preferred_element_type=F32)
            pt = jnp.exp2(st - lse_ref[i])
            dpt = lax.dot_general(vv, doi, _NT, preferred_element_type=F32)
            dst = (pt * (dpt - delta_ref[i])).astype(BF16)
            dv_s[...] += lax.dot_general(pt.astype(BF16), doi, _NN, preferred_element_type=F32)
            dk_i = lax.dot_general(dst, qi, _NN, preferred_element_type=F32)
            if shared_k:
                dk_s[...] += dk_i
            else:
                dk_s[:, LANES * i:LANES * (i + 1)] += dk_i
            dq_parts.append(lax.dot_general(dst, ki, _TN, preferred_element_type=F32))
        rows = pl.ds(pl.multiple_of(qb * tq, tq), tq)
        if shared_k:
            tiles = [(slice(0, LANES), jnp.where(lo, dq_parts[0], dq_parts[1]))]
        else:
            tiles = [(slice(0, LANES), dq_parts[0]), (slice(LANES, 2 * LANES), dq_parts[1])]
        for cols, val in tiles:
            @pl.when(kb == 0)
            def _(cols=cols, val=val):
                dq_ref[rows, cols] = val

            @pl.when(kb > 0)
            def _(cols=cols, val=val):
                dq_ref[rows, cols] += val

        @pl.when(qb == nq - 1)
        def _():
            if shared_k:
                dk_ref[0] = dk_s[...]
                dv_ref[0] = dv_s[...]
            else:
                dk_ref[...] = dk_s[...]
                dv_ref[...] = dv_s[...]

    stat_spec = pl.BlockSpec((2, 1, tq), lambda p, j, i: (p, 0, i))
    do_spec = pl.BlockSpec((tq, LANES), lambda p, j, i: (i, p))
    dq_spec = pl.BlockSpec((t, wq), lambda p, j, i: (0, p))
    if shared_k:
        dk_spec = pl.BlockSpec((1, tk, LANES), lambda p, j, i: (p, j, 0))
        dv_spec = dk_spec
        dk_shape = jax.ShapeDtypeStruct((groups, t, LANES), F32)
        dv_shape = dk_shape
    else:
        dk_spec = pl.BlockSpec((tk, wq), lambda p, j, i: (j, p))
        dv_spec = pl.BlockSpec((tk, LANES), lambda p, j, i: (j, p))
        dk_shape = jax.ShapeDtypeStruct((t, wq * groups), F32)
        dv_shape = jax.ShapeDtypeStruct((t, LANES * groups), F32)
    return pl.pallas_call(
        body, name=name, grid=(groups, nk, nq),
        in_specs=[q_spec, k_spec, v_spec, do_spec, stat_spec, stat_spec],
        out_specs=[dq_spec, dk_spec, dv_spec],
        out_shape=[jax.ShapeDtypeStruct((t, wq * groups), F32), dk_shape, dv_shape],
        scratch_shapes=[pltpu.VMEM((tk, wq), F32), pltpu.VMEM((tk, LANES), F32)],
        compiler_params=_params("parallel", "arbitrary", "arbitrary"),
    )(q, k, v, do, lse, delta)


_MERGE_W = 512
_GATE_BLK0 = Z_GATE // _MERGE_W


def _merge_fwd(z, b_gate, ta, tb):
    t = z.shape[0]
    tm = _tile(t, 512)
    w = _MERGE_W
    nj = D_MODEL // w

    def body(za_ref, zb_ref, ba_ref, bb_ref, ta_ref, tb_ref, o_ref):
        ga = jax.nn.sigmoid(za_ref[...] + ba_ref[...])
        gb = jax.nn.sigmoid(zb_ref[...] + bb_ref[...])
        o_ref[...] = (ga * ta_ref[...] + gb * tb_ref[...]).astype(BF16)

    return pl.pallas_call(
        body, name="merge_fwd", grid=(t // tm, nj),
        in_specs=[pl.BlockSpec((tm, w), lambda i, j: (i, _GATE_BLK0 + j)),
                  pl.BlockSpec((tm, w), lambda i, j: (i, _GATE_BLK0 + nj + j)),
                  pl.BlockSpec((1, w), lambda i, j: (0, j)),
                  pl.BlockSpec((1, w), lambda i, j: (0, nj + j)),
                  pl.BlockSpec((tm, w), lambda i, j: (i, j)),
                  pl.BlockSpec((tm, w), lambda i, j: (i, j))],
        out_specs=pl.BlockSpec((tm, w), lambda i, j: (i, j)),
        out_shape=jax.ShapeDtypeStruct((t, D_MODEL), BF16),
        compiler_params=_params("parallel", "parallel"),
    )(z, z, b_gate, b_gate, ta, tb)


def _merge_bwd(dmg, z, b_gate, ta, tb):
    t = z.shape[0]
    tm = _tile(t, 512)
    w = _MERGE_W
    nj = D_MODEL // w

    def body(dm_ref, za_ref, zb_ref, ba_ref, bb_ref, ta_ref, tb_ref, dta_ref, dtb_ref, dza_ref, dzb_ref,
             dba_ref, dbb_ref):
        dm = dm_ref[...]
        ga = jax.nn.sigmoid(za_ref[...] + ba_ref[...])
        gb = jax.nn.sigmoid(zb_ref[...] + bb_ref[...])
        dta_ref[...] = (dm * ga).astype(BF16)
        dtb_ref[...] = (dm * gb).astype(BF16)
        dza = dm * ta_ref[...] * ga * (1.0 - ga)
        dzb = dm * tb_ref[...] * gb * (1.0 - gb)
        dza_ref[...] = dza.astype(BF16)
        dzb_ref[...] = dzb.astype(BF16)

        @pl.when(pl.program_id(1) == 0)
        def _():
            dba_ref[...] = jnp.zeros_like(dba_ref)
            dbb_ref[...] = jnp.zeros_like(dbb_ref)

        dba_ref[...] += _fold8(dza)
        dbb_ref[...] += _fold8(dzb)

    blk = pl.BlockSpec((tm, w), lambda j, i: (i, j))
    acc = pl.BlockSpec((SUBLANES, w), lambda j, i: (0, j))
    return pl.pallas_call(
        body, name="merge_bwd", grid=(nj, t // tm),
        in_specs=[blk,
                  pl.BlockSpec((tm, w), lambda j, i: (i, _GATE_BLK0 + j)),
                  pl.BlockSpec((tm, w), lambda j, i: (i, _GATE_BLK0 + nj + j)),
                  pl.BlockSpec((1, w), lambda j, i: (0, j)),
                  pl.BlockSpec((1, w), lambda j, i: (0, nj + j)),
                  blk, blk],
        out_specs=[blk, blk, blk, blk, acc, acc],
        out_shape=[jax.ShapeDtypeStruct((t, D_MODEL), BF16)] * 4 + [jax.ShapeDtypeStruct((SUBLANES, D_MODEL), F32)] * 2,
        compiler_params=_params("parallel", "arbitrary"),
    )(dmg, z, z, b_gate, b_gate, ta, tb)


def _relu2_fwd(h):
    t, f = h.shape
    tm, tn = _tile(t, 512), _tile(f, 1024)

    def body(h_ref, a_ref):
        r = jnp.maximum(h_ref[...], 0.0)
        a_ref[...] = (r * r).astype(BF16)

    spec = pl.BlockSpec((tm, tn), lambda i, j: (i, j))
    return pl.pallas_call(
        body, name="relu2_fwd", grid=(t // tm, f // tn), in_specs=[spec], out_specs=spec,
        out_shape=jax.ShapeDtypeStruct((t, f), BF16), compiler_params=_params("parallel", "parallel"),
    )(h)


def _relu2_bwd(da, h):
    t, f = h.shape
    tm, tn = _tile(t, 512), _tile(f, 1024)

    def body(da_ref, h_ref, dh_ref):
        dh_ref[...] = (da_ref[...] * (2.0 * jnp.maximum(h_ref[...], 0.0))).astype(BF16)

    spec = pl.BlockSpec((tm, tn), lambda i, j: (i, j))
    return pl.pallas_call(
        body, name="relu2_bwd", grid=(t // tm, f // tn), in_specs=[spec, spec], out_specs=spec,
        out_shape=jax.ShapeDtypeStruct((t, f), BF16), compiler_params=_params("parallel", "parallel"),
    )(da, h)


def _loss_grad(y, target):
    t, d = y.shape
    tm = _tile(t, 512)

    def body(y_ref, t_ref, dy_ref, acc_ref):
        err = y_ref[...] - t_ref[...]
        dy_ref[...] = err * (1.0 / d)
        e8 = _fold8(err * err)
        part = e8[:, 0:LANES]
        for c in range(1, d // LANES):
            part = part + e8[:, LANES * c:LANES * (c + 1)]

        @pl.when(pl.program_id(0) == 0)
        def _():
            acc_ref[...] = jnp.zeros_like(acc_ref)

        acc_ref[...] += part

    return pl.pallas_call(
        body, name="loss_grad", grid=(t // tm,),
        in_specs=[_row_spec(tm, d), _row_spec(tm, d)],
        out_specs=[_row_spec(tm, d), _acc_spec(LANES)],
        out_shape=[jax.ShapeDtypeStruct((t, d), F32), jax.ShapeDtypeStruct((SUBLANES, LANES), F32)],
        compiler_params=_params("arbitrary"),
    )(y, target)


_MESH_ID = pl.DeviceIdType.MESH
_ANY = pl.BlockSpec(memory_space=pl.ANY)


def _all_gather(arrays):
    n = len(arrays)

    def body(*refs):
        x_refs, out_refs = refs[:n], refs[n:2 * n]
        send_sems, recv_sems, local_sems = refs[2 * n:]
        mx, my, mc = lax.axis_index("x"), lax.axis_index("y"), lax.axis_index("c")
        me, sibling = (mx, my, mc), (mx, my, 1 - mc)
        chips = [(1 - mx, my), (mx, 1 - my), (1 - mx, 1 - my)]

        def slot(a, px, py, pc):
            return out_refs[a].at[4 * px + 2 * py + pc]

        def copy(a, sem, block, to, src=None):
            return pltpu.make_async_remote_copy(
                src_ref=slot(a, *block) if src is None else src, dst_ref=slot(a, *block),
                send_sem=send_sems.at[a, sem], recv_sem=recv_sems.at[a, sem], device_id=to, device_id_type=_MESH_ID)

        mine = [pltpu.make_async_copy(x_refs[a], slot(a, *me), local_sems.at[a]) for a in range(n)]
        first = []
        for a in range(n):
            mine[a].start()
            first.append(copy(a, 0, me, sibling, src=x_refs[a]))
            first += [copy(a, 1 + j, me, (*chip, mc), src=x_refs[a]) for j, chip in enumerate(chips)]
        for cp in first:
            cp.start()
        passed = []
        for a in range(n):
            for j, chip in enumerate(chips):
                copy(a, 1 + j, (*chip, mc), me).wait_recv()
                passed.append(copy(a, 4 + j, (*chip, mc), sibling))
                passed[-1].start()
        for a in range(n):
            copy(a, 0, sibling, me).wait_recv()
            for j, chip in enumerate(chips):
                copy(a, 4 + j, (*chip, 1 - mc), me).wait_recv()
        for cp in first + passed:
            cp.wait_send()
        for cp in mine:
            cp.wait()

    return pl.pallas_call(
        body, name="weight_all_gather",
        out_shape=[jax.ShapeDtypeStruct((N_DEV,) + a.shape, a.dtype) for a in arrays],
        in_specs=[_ANY] * n, out_specs=[_ANY] * n,
        scratch_shapes=[pltpu.SemaphoreType.DMA((n, 7)), pltpu.SemaphoreType.DMA((n, 7)),
                        pltpu.SemaphoreType.DMA((n,))],
    )(*arrays)


def _all_to_all(sends):
    n = len(sends)

    def body(*refs):
        s_refs, r_refs = refs[:n], refs[n:2 * n]
        send_sems, recv_sems, local_sems = refs[2 * n:]
        mx, my, mc = lax.axis_index("x"), lax.axis_index("y"), lax.axis_index("c")
        me = 4 * mx + 2 * my + mc
        local = [pltpu.make_async_copy(s_refs[a].at[me], r_refs[a].at[me], local_sems.at[a]) for a in range(n)]
        copies = []
        for a in range(n):
            local[a].start()
            for rel in range(1, N_DEV):
                px = 1 - mx if rel & 4 else mx
                py = 1 - my if rel & 2 else my
                pc = 1 - mc if rel & 1 else mc
                peer = 4 * px + 2 * py + pc
                cp = pltpu.make_async_remote_copy(
                    src_ref=s_refs[a].at[peer], dst_ref=r_refs[a].at[me], send_sem=send_sems.at[a, rel - 1],
                    recv_sem=recv_sems.at[a, rel - 1], device_id=(px, py, pc), device_id_type=_MESH_ID)
                cp.start()
                copies.append(cp)
        for cp in copies:
            cp.wait_send()
            cp.wait_recv()
        for cp in local:
            cp.wait()

    return pl.pallas_call(
        body, name="grad_all_to_all",
        out_shape=[jax.ShapeDtypeStruct(s.shape, s.dtype) for s in sends],
        in_specs=[_ANY] * n, out_specs=[_ANY] * n,
        scratch_shapes=[pltpu.SemaphoreType.DMA((n, 7)), pltpu.SemaphoreType.DMA((n, 7)),
                        pltpu.SemaphoreType.DMA((n,))],
    )(*sends)


def _adamw(recv, w, m, v):
    r, c_ = w.shape
    tr = min(r, ADAM_BLOCK_ELEMS // (pl.cdiv(c_, LANES) * LANES))
    while r % tr:
        tr -= SUBLANES

    def body(g_ref, w_ref, m_ref, v_ref, go_ref, d_ref, mo_ref, vo_ref):
        g = g_ref[0]
        for s in range(1, N_DEV):
            g = g + g_ref[s]
        go_ref[...] = g
        mn = ADAM_B1 * m_ref[...] + (1.0 - ADAM_B1) * g
        vn = ADAM_B2 * v_ref[...] + (1.0 - ADAM_B2) * (g * g)
        mo_ref[...] = mn
        vo_ref[...] = vn
        m_hat = mn / (1.0 - ADAM_B1 ** ADAM_STEP)
        v_hat = vn / (1.0 - ADAM_B2 ** ADAM_STEP)
        d_ref[...] = -ADAM_LR * (m_hat / (jnp.sqrt(v_hat) + ADAM_EPS) + ADAM_WD * w_ref[...])

    spec = pl.BlockSpec((tr, c_), lambda i: (i, 0))
    out = jax.ShapeDtypeStruct((r, c_), F32)
    return pl.pallas_call(
        body, name="grad_sum_adamw", grid=(r // tr,),
        in_specs=[pl.BlockSpec((N_DEV, tr, c_), lambda i: (0, i, 0)), spec, spec, spec],
        out_specs=[spec, spec, spec, spec], out_shape=[out, out, out, out],
        compiler_params=_params("parallel"),
    )(recv, w, m, v)


def _pad_cols(a, before, after):
    parts = []
    if before:
        parts.append(jnp.zeros(a.shape[:-1] + (before,), a.dtype))
    parts.append(a)
    if after:
        parts.append(jnp.zeros(a.shape[:-1] + (after,), a.dtype))
    return jnp.concatenate(parts, axis=-1)


def _q_head_pairs(a, axis):
    shp = a.shape
    a = a.reshape(shp[:axis] + (GQA_KV_HEADS, GQA_GROUP, HEAD_DIM) + shp[axis + 1:])
    a = jnp.swapaxes(a, axis, axis + 1)
    return a.reshape(shp)


def _q_head_unpairs(a, axis):
    shp = a.shape
    a = a.reshape(shp[:axis] + (GQA_GROUP, GQA_KV_HEADS, HEAD_DIM) + shp[axis + 1:])
    a = jnp.swapaxes(a, axis, axis + 1)
    return a.reshape(shp)


def _layout_weights(w):
    w_in = w["w_in"]
    lead = w_in.shape[:-1]
    w_in_p = jnp.concatenate([
        _q_head_pairs(w_in[..., 0:512], w_in.ndim - 1),
        w_in[..., 512:1408],
        _pad_cols(w_in[..., 1408:1440], KR_LANE0, LANES - KR_LANE0 - MLA_ROPE_DIM),
        w_in[..., 1440:],
    ], axis=-1)
    wq = w["w_q_up"]
    wq_p = _pad_cols(wq.reshape(wq.shape[:-1] + (MLA_HEADS, MLA_QK_DIM)), 0, LANES - MLA_QK_DIM)
    wq_p = wq_p.reshape(wq.shape[:-1] + (MLA_HEADS * LANES,))
    wkv = w["w_kv_up"]
    wkv4 = wkv.reshape(wkv.shape[:-1] + (MLA_HEADS, 2 * HEAD_DIM))
    wk_p = _pad_cols(wkv4[..., :HEAD_DIM], 0, LANES - HEAD_DIM).reshape(wkv.shape[:-1] + (MLA_HEADS * LANES,))
    wv_p = wkv4[..., HEAD_DIM:].reshape(wkv.shape[:-1] + (MLA_HEADS * HEAD_DIM,))
    del lead
    return {
        "w_in": w_in_p, "w_q_up": wq_p, "w_kv_up": jnp.concatenate([wk_p, wv_p], axis=-1),
        "w_branch_a": _q_head_pairs(w["w_branch_a"], w["w_branch_a"].ndim - 2), "w_branch_b": w["w_branch_b"],
        "w_o": w["w_o"], "w_ffn_up": w["w_ffn_up"], "w_ffn_down": w["w_ffn_down"],
    }


def _unlayout_grads(g):
    gi = g["w_in"]
    kr0 = Z_KR + KR_LANE0
    g_in = jnp.concatenate([
        _q_head_unpairs(gi[..., 0:512], gi.ndim - 1), gi[..., 512:1408], gi[..., kr0:kr0 + MLA_ROPE_DIM],
        gi[..., Z_GATE:],
    ], axis=-1)
    gq = g["w_q_up"]
    gq = gq.reshape(gq.shape[:-1] + (MLA_HEADS, LANES))[..., :MLA_QK_DIM]
    gq = gq.reshape(gq.shape[:-2] + (MLA_HEADS * MLA_QK_DIM,))
    gkv = g["w_kv_up"]
    gk = gkv[..., :MLA_HEADS * LANES].reshape(gkv.shape[:-1] + (MLA_HEADS, LANES))[..., :HEAD_DIM]
    gv = gkv[..., MLA_HEADS * LANES:].reshape(gkv.shape[:-1] + (MLA_HEADS, HEAD_DIM))
    gkv = jnp.concatenate([gk, gv], axis=-1).reshape(gkv.shape[:-1] + (MLA_HEADS * 2 * HEAD_DIM,))
    return {
        "w_in": g_in, "w_q_up": gq, "w_kv_up": gkv,
        "w_branch_a": _q_head_unpairs(g["w_branch_a"], g["w_branch_a"].ndim - 2), "w_branch_b": g["w_branch_b"],
        "w_o": g["w_o"], "w_ffn_up": g["w_ffn_up"], "w_ffn_down": g["w_ffn_down"],
    }


def _pack_small(parts):
    flat = jnp.concatenate([p.reshape(-1) for p in parts])
    pad = (-flat.shape[0]) % (SUBLANES * LANES)
    if pad:
        flat = jnp.concatenate([flat, jnp.zeros((pad,), flat.dtype)])
    return flat.reshape(-1, LANES)


def _unpack_small(packed, shapes):
    flat = packed.reshape(-1)
    out, off = [], 0
    for shp in shapes:
        n = int(np.prod(shp))
        out.append(flat[off:off + n].reshape(shp))
        off += n
    return out


def _shards_of(full, axis):
    shp = full.shape
    cut = shp[:axis] + (N_DEV, shp[axis] // N_DEV) + shp[axis + 1:]
    return jnp.moveaxis(full.reshape(cut), axis, 0)


def _from_shards(shards, axis):
    full = list(shards.shape[1:])
    full[axis] *= N_DEV
    return jnp.moveaxis(shards, 0, axis).reshape(full)


def _rows2d(a):
    return a.reshape(-1, a.shape[-1])


def _layer_fwd(x, u, lw, tabs):
    cos_a, sin_a, cos_b, sin_b = tabs
    z = _matmul(u, lw["w_in"], "nn", "mm_in")
    qa, ka, va, cqn, ckvn, krr = _prep_a_fwd(z, lw["gq2"], lw["gk2"], lw["gqa"], lw["gkva"], cos_a, sin_a, cos_b, sin_b)
    qb = _matmul(cqn, lw["w_q_up"], "nn", "mm_q_up")
    kvb = _matmul(ckvn, lw["w_kv_up"], "nn", "mm_kv_up")
    q_b, k_b, v_b = _prep_b_fwd(qb, kvb, krr, cos_b, sin_b)
    ya, lse_a = _attn_fwd(qa, ka, va, True, "gqa_fwd")
    yb, lse_b = _attn_fwd(q_b, k_b, v_b, False, "mla_fwd")
    ta = _matmul(ya, lw["w_branch_a"], "nn", "mm_branch_a")
    tb = _matmul(yb, lw["w_branch_b"], "nn", "mm_branch_b")
    merged = _merge_fwd(z, lw["b_gate"], ta, tb)
    m = _matmul(merged, lw["w_o"], "nn", "mm_o")
    x2, u2 = _res_norm_fwd(x, m, lw["post_mix_g"], lw["pre_ffn_g"])
    h = _matmul(u2, lw["w_ffn_up"], "nn", "mm_ffn_up")
    a = _relu2_fwd(h)
    f = _matmul(a, lw["w_ffn_down"], "nn", "mm_ffn_down")
    x3, u_next = _res_norm_fwd(x2, f, lw["post_ffn_g"], lw["next_pre_mix_g"])
    saved = dict(u=u, z=z, qa=qa, ka=ka, va=va, cqn=cqn, ckvn=ckvn, q_b=q_b, k_b=k_b, v_b=v_b, ya=ya, yb=yb,
                 lse_a=lse_a, lse_b=lse_b, ta=ta, tb=tb, merged=merged, m=m, x2=x2, u2=u2, h=h, a=a, f=f, x3=x3)
    return x3, u_next, saved


def _layer_bwd(dx3, du_next, lw, sv, tabs):
    cos_a, sin_a, cos_b, sin_b = tabs
    g = {}
    dx3, df, dg4, dg1n = _res_norm_bwd(sv["x3"], sv["f"], lw["post_ffn_g"], lw["next_pre_mix_g"], dx3, du_next)
    g["post_ffn_g"], g["next_pre_mix_g"] = dg4, dg1n
    da = _matmul(df, lw["w_ffn_down"], "nt", "mm_d_a")
    g["w_ffn_down"] = _matmul(sv["a"], df, "tn", "mm_dw_ffn_down")
    dh = _relu2_bwd(da, sv["h"])
    du2 = _matmul(dh, lw["w_ffn_up"], "nt", "mm_d_u2")
    g["w_ffn_up"] = _matmul(sv["u2"], dh, "tn", "mm_dw_ffn_up")
    dx2, dm, dg2, dg3 = _res_norm_bwd(sv["x2"], sv["m"], lw["post_mix_g"], lw["pre_ffn_g"], dx3, du2)
    g["post_mix_g"], g["pre_ffn_g"] = dg2, dg3
    dmg = _matmul(dm, lw["w_o"], "nt", "mm_d_merged")
    g["w_o"] = _matmul(sv["merged"], dm, "tn", "mm_dw_o")
    dta, dtb, dzg_a, dzg_b, db_a, db_b = _merge_bwd(dmg, sv["z"], lw["b_gate"], sv["ta"], sv["tb"])
    g["b_gate"] = jnp.concatenate([db_a, db_b], axis=-1)
    dya = _matmul(dta, lw["w_branch_a"], "nt", "mm_d_ya")
    g["w_branch_a"] = _matmul(sv["ya"], dta, "tn", "mm_dw_branch_a")
    dyb = _matmul(dtb, lw["w_branch_b"], "nt", "mm_d_yb")
    g["w_branch_b"] = _matmul(sv["yb"], dtb, "tn", "mm_dw_branch_b")
    delta_a, dya16 = _attn_delta(dya, sv["ya"])
    delta_b, dyb16 = _attn_delta(dyb, sv["yb"])
    dqa, dka4, dva4 = _attn_bwd(sv["qa"], sv["ka"], sv["va"], dya16, sv["lse_a"], delta_a, True, "gqa_bwd")
    dq_b, dk_b, dv_b = _attn_bwd(sv["q_b"], sv["k_b"], sv["v_b"], dyb16, sv["lse_b"], delta_b, False, "mla_bwd")
    dqb, dkvb, dkr = _prep_b_bwd(dq_b, dk_b, dv_b, cos_b, sin_b)
    dcqn = _matmul(dqb, lw["w_q_up"], "nt", "mm_d_cqn")
    g["w_q_up"] = _matmul(sv["cqn"], dqb, "tn", "mm_dw_q_up")
    dckvn = _matmul(dkvb, lw["w_kv_up"], "nt", "mm_d_ckvn")
    g["w_kv_up"] = _matmul(sv["ckvn"], dkvb, "tn", "mm_dw_kv_up")
    dz, dgq, dgk, dgqa, dgkva = _prep_a_bwd(sv["z"], dqa, dka4, dva4, dcqn, dckvn, dkr, dzg_a, dzg_b, lw["gq2"],
                                            lw["gk2"], lw["gqa"], lw["gkva"], cos_a, sin_a)
    g["q_norm_g"], g["k_norm_g"], g["q_a_norm_g"], g["kv_a_norm_g"] = dgq, dgk, dgqa, dgkva
    du = _matmul(dz, lw["w_in"], "nt", "mm_d_u")
    g["w_in"] = _matmul(sv["u"], dz, "tn", "mm_dw_in")
    return dx2, du, g


def kernel(x, w_in, b_gate, q_norm_g, k_norm_g, q_a_norm_g, kv_a_norm_g, w_q_up, w_kv_up, w_branch_a, w_branch_b, w_o, w_ffn_up, w_ffn_down, pre_mix_g, post_mix_g, pre_ffn_g, post_ffn_g, loss_target, m_w_in, m_b_gate, m_q_norm_g, m_k_norm_g, m_q_a_norm_g, m_kv_a_norm_g, m_w_q_up, m_w_kv_up, m_w_branch_a, m_w_branch_b, m_w_o, m_w_ffn_up, m_w_ffn_down, m_pre_mix_g, m_post_mix_g, m_pre_ffn_g, m_post_ffn_g, v_w_in, v_b_gate, v_q_norm_g, v_k_norm_g, v_q_a_norm_g, v_kv_a_norm_g, v_w_q_up, v_w_kv_up, v_w_branch_a, v_w_branch_b, v_w_o, v_w_ffn_up, v_w_ffn_down, v_pre_mix_g, v_post_mix_g, v_pre_ffn_g, v_post_ffn_g):
    weights = dict(zip(WEIGHT_NAMES, (w_in, b_gate, q_norm_g, k_norm_g, q_a_norm_g, kv_a_norm_g, w_q_up, w_kv_up,
                                      w_branch_a, w_branch_b, w_o, w_ffn_up, w_ffn_down, pre_mix_g, post_mix_g,
                                      pre_ffn_g, post_ffn_g)))
    mom_m = dict(zip(WEIGHT_NAMES, (m_w_in, m_b_gate, m_q_norm_g, m_k_norm_g, m_q_a_norm_g, m_kv_a_norm_g, m_w_q_up,
                                    m_w_kv_up, m_w_branch_a, m_w_branch_b, m_w_o, m_w_ffn_up, m_w_ffn_down,
                                    m_pre_mix_g, m_post_mix_g, m_pre_ffn_g, m_post_ffn_g)))
    mom_v = dict(zip(WEIGHT_NAMES, (v_w_in, v_b_gate, v_q_norm_g, v_k_norm_g, v_q_a_norm_g, v_kv_a_norm_g, v_w_q_up,
                                    v_w_kv_up, v_w_branch_a, v_w_branch_b, v_w_o, v_w_ffn_up, v_w_ffn_down,
                                    v_pre_mix_g, v_post_mix_g, v_pre_ffn_g, v_post_ffn_g)))
    assert x.shape[0] == 1 and x.shape[2] == D_MODEL, x.shape
    n_layers = w_in.shape[0]
    t = x.shape[1]
    x0 = x.reshape(t, D_MODEL)
    target = loss_target.reshape(t, D_MODEL)
    shard_shapes = {n: weights[n].shape for n in BIG_NAMES}
    small_shapes = [weights[n].shape for n in SMALL_NAMES]

    gathered = _all_gather([weights[n].astype(BF16) for n in BIG_NAMES])
    full = {n: _from_shards(g, SHARD_AXIS[n]) for n, g in zip(BIG_NAMES, gathered)}
    lw_all = _layout_weights(full)
    lw_all["b_gate"] = b_gate.reshape(n_layers, 1, 2 * D_MODEL)
    lw_all["gq2"] = jnp.tile(q_norm_g, (1, 2)).reshape(n_layers, 1, LANES)
    lw_all["gk2"] = jnp.tile(k_norm_g, (1, 2)).reshape(n_layers, 1, LANES)
    lw_all["gqa"] = q_a_norm_g.reshape(n_layers, 1, MLA_Q_RANK)
    lw_all["gkva"] = kv_a_norm_g.reshape(n_layers, 1, MLA_KV_RANK)
    for n in ("post_mix_g", "pre_ffn_g", "post_ffn_g"):
        lw_all[n] = weights[n]
    lw_all["next_pre_mix_g"] = jnp.roll(pre_mix_g, -1, axis=0)

    tabs = _rope_tables(t)
    u0 = _rms_fwd(x0, pre_mix_g[0])

    layer_w = [{n: a[l] for n, a in lw_all.items()} for l in range(n_layers)]
    xc, uc, saved = x0, u0, []
    for l in range(n_layers):
        xc, uc, sv = _layer_fwd(xc, uc, layer_w[l], tabs)
        saved.append(sv)
    dy, loss_acc = _loss_grad(xc, target)
    loss = lax.psum(0.5 * jnp.sum(loss_acc) / D_MODEL, ("x", "y", "c"))

    dx0, du0, layer_g = dy, jnp.zeros((t, D_MODEL), F32), [None] * n_layers
    for l in reversed(range(n_layers)):
        dx0, du0, layer_g[l] = _layer_bwd(dx0, du0, layer_w[l], saved[l], tabs)
    grads = {n: jnp.stack([g[n] for g in layer_g]) for n in layer_g[0]}
    grad_x, dg1_first = _rms_bwd(x0, pre_mix_g[0], dx0, du0)

    big_grads = _unlayout_grads({n: grads[n] for n in BIG_NAMES})
    fold = lambda a: a.sum(axis=1)
    dgq = fold(grads["q_norm_g"]).reshape(n_layers, 2, HEAD_DIM).sum(axis=1)
    dgk = fold(grads["k_norm_g"]).reshape(n_layers, 2, HEAD_DIM).sum(axis=1)
    dg1 = jnp.concatenate([fold(dg1_first[None]), fold(grads["next_pre_mix_g"])[:-1]], axis=0)
    small_grads = {
        "b_gate": fold(grads["b_gate"]), "q_norm_g": dgq, "k_norm_g": dgk, "q_a_norm_g": fold(grads["q_a_norm_g"]),
        "kv_a_norm_g": fold(grads["kv_a_norm_g"]), "pre_mix_g": dg1, "post_mix_g": fold(grads["post_mix_g"]),
        "pre_ffn_g": fold(grads["pre_ffn_g"]), "post_ffn_g": fold(grads["post_ffn_g"]),
    }
    small_packed = _pack_small([small_grads[n] for n in SMALL_NAMES])
    sends = [_shards_of(big_grads[n], SHARD_AXIS[n]) for n in BIG_NAMES]
    sends.append(jnp.broadcast_to(small_packed[None], (N_DEV,) + small_packed.shape))
    recvs = _all_to_all(sends)

    results = {}
    for n, recv in zip(BIG_NAMES, recvs):
        res = _adamw(recv.reshape((N_DEV,) + _rows2d(weights[n]).shape), _rows2d(weights[n]), _rows2d(mom_m[n]),
                     _rows2d(mom_v[n]))
        results[n] = [r.reshape(shard_shapes[n]) for r in res]
    res = _adamw(recvs[-1], *[_pack_small([d[n] for n in SMALL_NAMES]) for d in (weights, mom_m, mom_v)])
    for kind, packed_out in enumerate(res):
        for n, val in zip(SMALL_NAMES, _unpack_small(packed_out, small_shapes)):
            results.setdefault(n, [None] * 4)[kind] = val
    outs = [results[n][kind] for kind in range(4) for n in WEIGHT_NAMES]
    return (loss, grad_x.reshape(x.shape), *outs)
```

```python
import functools
import math

import jax
import jax.numpy as jnp
import numpy as np
from jax import lax
from jax.experimental import pallas as pl
from jax.experimental.pallas import tpu as pltpu

F32 = jnp.float32
BF16 = jnp.bfloat16

D_MODEL = 1024
GRID_W = 64
ROPE_THETA = 10000.0
EPS = 1e-6
GQA_HEADS = 8
GQA_KV_HEADS = 2
GQA_GROUP = GQA_HEADS // GQA_KV_HEADS
HEAD_DIM = 64
MLA_HEADS = 8
MLA_ROPE_DIM = 32
MLA_QK_DIM = 96
MLA_Q_RANK = 384
MLA_KV_RANK = 256
D_FF = 4 * D_MODEL
GQA_SCALE = 1.0 / math.sqrt(HEAD_DIM)
MLA_SCALE = 1.0 / math.sqrt(MLA_QK_DIM)
LOG2E = math.log2(math.e)
LN2 = math.log(2.0)

ADAM_LR = 0.001
ADAM_B1 = 0.9
ADAM_B2 = 0.999
ADAM_EPS = 1e-08
ADAM_WD = 0.01
ADAM_STEP = 10

N_DEV = 8
LANES = 128
SUBLANES = 8
VMEM_LIMIT = 48 * 1024 * 1024

Z_QA, Z_KA, Z_VA, Z_CQ, Z_CKV, Z_KR, Z_GATE = 0, 512, 640, 768, 1152, 1408, 1536
Z_ATT_W = 1536
Z_W = 3584
KR_LANE0 = 64

WEIGHT_NAMES = ("w_in", "b_gate", "q_norm_g", "k_norm_g", "q_a_norm_g", "kv_a_norm_g", "w_q_up", "w_kv_up",
                "w_branch_a", "w_branch_b", "w_o", "w_ffn_up", "w_ffn_down", "pre_mix_g", "post_mix_g",
                "pre_ffn_g", "post_ffn_g")
SHARD_AXIS = {"w_in": 2, "w_q_up": 2, "w_kv_up": 2, "w_branch_a": 2, "w_branch_b": 2, "w_o": 1, "w_ffn_up": 2,
              "w_ffn_down": 1}
BIG_NAMES = tuple(n for n in WEIGHT_NAMES if n in SHARD_AXIS)
SMALL_NAMES = tuple(n for n in WEIGHT_NAMES if n not in SHARD_AXIS)
ADAM_BLOCK_ELEMS = 256 * 1024
MM_TILE = 1024
ATTN_TQ = 1024
ATTN_TK = 512


def _params(*semantics):
    return pltpu.CompilerParams(dimension_semantics=semantics, vmem_limit_bytes=VMEM_LIMIT)


def _tile(n, pref):
    if n <= pref:
        return n
    t = (pref // LANES) * LANES
    while n % t:
        t -= LANES
    return t


def _fold8(t):
    return t.reshape(t.shape[0] // SUBLANES, SUBLANES, t.shape[1]).sum(axis=0)


_DIMS = {"nn": ((1,), (0,)), "nt": ((1,), (1,)), "tn": ((0,), (0,))}


def _matmul(a, b, mode, name):
    if mode == "nn":
        (m, k), n = a.shape, b.shape[1]
    elif mode == "nt":
        (m, k), n = a.shape, b.shape[0]
    else:
        (k, m), n = a.shape, b.shape[1]
    tm, tn, tk = _tile(m, MM_TILE), _tile(n, MM_TILE), _tile(k, MM_TILE)
    nk = k // tk
    dims = (_DIMS[mode], ((), ()))

    def body(a_ref, b_ref, o_ref, acc_ref):
        prod = lax.dot_general(a_ref[...], b_ref[...], dims, preferred_element_type=F32)
        if nk == 1:
            o_ref[...] = prod
        else:
            kk = pl.program_id(2)

            @pl.when(kk == 0)
            def _():
                acc_ref[...] = prod

            @pl.when(kk > 0)
            def _():
                acc_ref[...] += prod

            @pl.when(kk == nk - 1)
            def _():
                o_ref[...] = acc_ref[...]

    if mode == "tn":
        a_spec = pl.BlockSpec((tk, tm), lambda i, j, kk: (kk, i))
    else:
        a_spec = pl.BlockSpec((tm, tk), lambda i, j, kk: (i, kk))
    if mode == "nt":
        b_spec = pl.BlockSpec((tn, tk), lambda i, j, kk: (j, kk))
    else:
        b_spec = pl.BlockSpec((tk, tn), lambda i, j, kk: (kk, j))
    return pl.pallas_call(
        body,
        name=name,
        grid=(m // tm, n // tn, nk),
        in_specs=[a_spec, b_spec],
        out_specs=pl.BlockSpec((tm, tn), lambda i, j, kk: (i, j)),
        out_shape=jax.ShapeDtypeStruct((m, n), F32),
        scratch_shapes=[pltpu.VMEM((tm, tn), F32)],
        compiler_params=_params("parallel", "parallel", "arbitrary"),
    )(a, b)


def _rinv(x):
    return lax.rsqrt(jnp.mean(x * x, axis=-1, keepdims=True) + EPS)


def _rms_bwd_rows(x, g, dy):
    r = _rinv(x)
    xh = x * r
    dxh = dy * g
    dx = r * (dxh - xh * jnp.mean(dxh * xh, axis=-1, keepdims=True))
    return dx, dy * xh


def _row_spec(tm, c):
    return pl.BlockSpec((tm, c), lambda i: (i, 0))


def _vec_spec(c):
    return pl.BlockSpec((1, c), lambda i: (0, 0))


def _acc_spec(c):
    return pl.BlockSpec((SUBLANES, c), lambda i: (0, 0))


def _rms_fwd(x, g):
    t, d = x.shape
    tm = _tile(t, 512)

    def body(x_ref, g_ref, o_ref):
        xv = x_ref[...]
        o_ref[...] = (xv * _rinv(xv) * g_ref[...]).astype(BF16)

    return pl.pallas_call(
        body, name="rms_fwd", grid=(t // tm,),
        in_specs=[_row_spec(tm, d), _vec_spec(d)], out_specs=_row_spec(tm, d),
        out_shape=jax.ShapeDtypeStruct((t, d), BF16), compiler_params=_params("parallel"),
    )(x, g.reshape(1, d))


def _rms_bwd(x, g, dres, dy):
    t, d = x.shape
    tm = _tile(t, 512)

    def body(x_ref, g_ref, dres_ref, dy_ref, dx_ref, dg_ref):
        dx, dgc = _rms_bwd_rows(x_ref[...], g_ref[...], dy_ref[...])
        dx_ref[...] = dres_ref[...] + dx

        @pl.when(pl.program_id(0) == 0)
        def _():
            dg_ref[...] = jnp.zeros_like(dg_ref)

        dg_ref[...] += _fold8(dgc)

    return pl.pallas_call(
        body, name="rms_bwd", grid=(t // tm,),
        in_specs=[_row_spec(tm, d), _vec_spec(d), _row_spec(tm, d), _row_spec(tm, d)],
        out_specs=[_row_spec(tm, d), _acc_spec(d)],
        out_shape=[jax.ShapeDtypeStruct((t, d), F32), jax.ShapeDtypeStruct((SUBLANES, d), F32)],
        compiler_params=_params("arbitrary"),
    )(x, g.reshape(1, d), dres, dy)


def _res_norm_fwd(x, m, g_post, g_next):
    t, d = x.shape
    tm = _tile(t, 512)

    def body(x_ref, m_ref, gp_ref, gn_ref, x2_ref, u2_ref):
        mv = m_ref[...]
        x2 = x_ref[...] + mv * _rinv(mv) * gp_ref[...]
        x2_ref[...] = x2
        u2_ref[...] = (x2 * _rinv(x2) * gn_ref[...]).astype(BF16)

    return pl.pallas_call(
        body, name="res_norm_fwd", grid=(t // tm,),
        in_specs=[_row_spec(tm, d), _row_spec(tm, d), _vec_spec(d), _vec_spec(d)],
        out_specs=[_row_spec(tm, d), _row_spec(tm, d)],
        out_shape=[jax.ShapeDtypeStruct((t, d), F32), jax.ShapeDtypeStruct((t, d), BF16)],
        compiler_params=_params("parallel"),
    )(x, m, g_post.reshape(1, d), g_next.reshape(1, d))


def _res_norm_bwd(x2, m, g_post, g_next, dx2_in, du2):
    t, d = x2.shape
    tm = _tile(t, 512)

    def body(x2_ref, m_ref, gp_ref, gn_ref, dx2in_ref, du2_ref, dx2_ref, dm_ref, dgp_ref, dgn_ref):
        dxn, dgn_c = _rms_bwd_rows(x2_ref[...], gn_ref[...], du2_ref[...])
        dx2 = dx2in_ref[...] + dxn
        dx2_ref[...] = dx2
        dm, dgp_c = _rms_bwd_rows(m_ref[...], gp_ref[...], dx2)
        dm_ref[...] = dm.astype(BF16)

        @pl.when(pl.program_id(0) == 0)
        def _():
            dgp_ref[...] = jnp.zeros_like(dgp_ref)
            dgn_ref[...] = jnp.zeros_like(dgn_ref)

        dgp_ref[...] += _fold8(dgp_c)
        dgn_ref[...] += _fold8(dgn_c)

    return pl.pallas_call(
        body, name="res_norm_bwd", grid=(t // tm,),
        in_specs=[_row_spec(tm, d), _row_spec(tm, d), _vec_spec(d), _vec_spec(d), _row_spec(tm, d), _row_spec(tm, d)],
        out_specs=[_row_spec(tm, d), _row_spec(tm, d), _acc_spec(d), _acc_spec(d)],
        out_shape=[jax.ShapeDtypeStruct((t, d), F32), jax.ShapeDtypeStruct((t, d), BF16),
                   jax.ShapeDtypeStruct((SUBLANES, d), F32), jax.ShapeDtypeStruct((SUBLANES, d), F32)],
        compiler_params=_params("arbitrary"),
    )(x2, m, g_post.reshape(1, d), g_next.reshape(1, d), dx2_in, du2)


def _rope_tables(t):
    rows = t // GRID_W
    row = jnp.repeat(jnp.arange(rows, dtype=F32), GRID_W)
    col = jnp.tile(jnp.arange(GRID_W, dtype=F32), rows)

    def tab(rot_dim):
        half = rot_dim // 2
        inv = ROPE_THETA ** (-jnp.arange(0, half, 2, dtype=F32) / half)
        ar = row[:, None] * inv[None, :]
        ac = col[:, None] * inv[None, :]
        ang = jnp.concatenate([ar, ar, ac, ac], axis=-1)
        q = half // 2
        sign = np.tile(np.concatenate([-np.ones(q, np.float32), np.ones(q, np.float32)]), 2)
        return jnp.cos(ang), jnp.sin(ang) * sign[None, :]

    ca, sa = tab(HEAD_DIM)
    cb, sb = tab(MLA_ROPE_DIM)
    one = jnp.ones((t, 1), F32)
    cos_b = jnp.concatenate([one * jnp.ones((1, KR_LANE0), F32), cb, one * jnp.ones((1, 32), F32)], axis=-1)
    sin_b = jnp.concatenate([jnp.zeros((t, KR_LANE0), F32), sb, jnp.zeros((t, 32), F32)], axis=-1)
    return jnp.tile(ca, (1, GQA_HEADS)), jnp.tile(sa, (1, GQA_HEADS)), cos_b, sin_b


def _swap_halves(x, sh):
    lane = lax.broadcasted_iota(jnp.int32, x.shape, 1)
    up = pltpu.roll(x, LANES - sh, 1)
    dn = pltpu.roll(x, sh, 1)
    return jnp.where((lane & (2 * sh - 1)) < sh, up, dn)


def _rope(x, cos, sin_s, sh):
    return x * cos + _swap_halves(x, sh) * sin_s


def _rope_bwd(dy, cos, sin_s, sh):
    return dy * cos + _swap_halves(dy * sin_s, sh)


def _lo_mask(shape):
    return lax.broadcasted_iota(jnp.int32, shape, 1) < HEAD_DIM


def _half_mean(t, lo):
    s_lo = jnp.sum(jnp.where(lo, t, 0.0), axis=-1, keepdims=True)
    s_hi = jnp.sum(jnp.where(lo, 0.0, t), axis=-1, keepdims=True)
    return jnp.where(lo, s_lo, s_hi) * (1.0 / HEAD_DIM)


def _head_norm(x, g2):
    lo = _lo_mask(x.shape)
    r = lax.rsqrt(_half_mean(x * x, lo) + EPS)
    return x * r * g2


def _head_norm_bwd(x, g2, dy):
    lo = _lo_mask(x.shape)
    r = lax.rsqrt(_half_mean(x * x, lo) + EPS)
    xh = x * r
    dxh = dy * g2
    dx = r * (dxh - xh * _half_mean(dxh * xh, lo))
    return dx, dy * xh


def _prep_a_fwd(z, gq2, gk2, gqa, gkva, cos_a, sin_a, cos_b, sin_b):
    t = z.shape[0]
    tm = _tile(t, 256)

    def body(z_ref, gq_ref, gk_ref, gqa_ref, gkva_ref, ca_ref, sa_ref, cb_ref, sb_ref,
             qa_ref, ka_ref, va_ref, cqn_ref, ckvn_ref, krr_ref):
        for j in range(4):
            cols = slice(LANES * j, LANES * (j + 1))
            y = _rope(_head_norm(z_ref[:, cols], gq_ref[...]), ca_ref[:, cols], sa_ref[:, cols], 16)
            qa_ref[:, cols] = (y * (GQA_SCALE * LOG2E)).astype(BF16)
        y = _rope(_head_norm(z_ref[:, Z_KA:Z_VA], gk_ref[...]), ca_ref[:, :LANES], sa_ref[:, :LANES], 16)
        ka_ref[...] = y.astype(BF16)
        va_ref[...] = z_ref[:, Z_VA:Z_CQ].astype(BF16)
        cq = z_ref[:, Z_CQ:Z_CKV]
        cqn_ref[...] = (cq * _rinv(cq) * gqa_ref[...]).astype(BF16)
        ckv = z_ref[:, Z_CKV:Z_KR]
        ckvn_ref[...] = (ckv * _rinv(ckv) * gkva_ref[...]).astype(BF16)
        krr_ref[...] = _rope(z_ref[:, Z_KR:Z_GATE], cb_ref[...], sb_ref[...], 8)

    return pl.pallas_call(
        body, name="prep_a_fwd", grid=(t // tm,),
        in_specs=[_row_spec(tm, Z_ATT_W), _vec_spec(LANES), _vec_spec(LANES), _vec_spec(MLA_Q_RANK),
                  _vec_spec(MLA_KV_RANK), _row_spec(tm, 512), _row_spec(tm, 512), _row_spec(tm, LANES),
                  _row_spec(tm, LANES)],
        out_specs=[_row_spec(tm, 512), _row_spec(tm, LANES), _row_spec(tm, LANES), _row_spec(tm, MLA_Q_RANK),
                   _row_spec(tm, MLA_KV_RANK), _row_spec(tm, LANES)],
        out_shape=[jax.ShapeDtypeStruct((t, 512), BF16), jax.ShapeDtypeStruct((t, LANES), BF16),
                   jax.ShapeDtypeStruct((t, LANES), BF16), jax.ShapeDtypeStruct((t, MLA_Q_RANK), BF16),
                   jax.ShapeDtypeStruct((t, MLA_KV_RANK), BF16), jax.ShapeDtypeStruct((t, LANES), F32)],
        compiler_params=_params("parallel"),
    )(z, gq2, gk2, gqa, gkva, cos_a, sin_a, cos_b, sin_b)


def _prep_a_bwd(z, dqa, dka4, dva4, dcqn, dckvn, dkr, dzga, dzgb, gq2, gk2, gqa, gkva, cos_a, sin_a):
    t = z.shape[0]
    tm = _tile(t, 256)

    def body(z_ref, dqa_ref, dka_ref, dva_ref, dcqn_ref, dckvn_ref, dkr_ref, dzga_ref, dzgb_ref, gq_ref, gk_ref,
             gqa_ref, gkva_ref, ca_ref, sa_ref, dz_ref, dgq_ref, dgk_ref, dgqa_ref, dgkva_ref):
        @pl.when(pl.program_id(0) == 0)
        def _():
            dgq_ref[...] = jnp.zeros_like(dgq_ref)
            dgk_ref[...] = jnp.zeros_like(dgk_ref)
            dgqa_ref[...] = jnp.zeros_like(dgqa_ref)
            dgkva_ref[...] = jnp.zeros_like(dgkva_ref)

        dgq = jnp.zeros((SUBLANES, LANES), F32)
        for j in range(4):
            cols = slice(LANES * j, LANES * (j + 1))
            dy = _rope_bwd(dqa_ref[:, cols] * GQA_SCALE, ca_ref[:, cols], sa_ref[:, cols], 16)
            dx, dgc = _head_norm_bwd(z_ref[:, cols], gq_ref[...], dy)
            dz_ref[:, cols] = dx.astype(BF16)
            dgq = dgq + _fold8(dgc)
        dgq_ref[...] += dgq
        dk = (dka_ref[0] + dka_ref[1] + dka_ref[2] + dka_ref[3]).T * LN2
        dy = _rope_bwd(dk, ca_ref[:, :LANES], sa_ref[:, :LANES], 16)
        dx, dgc = _head_norm_bwd(z_ref[:, Z_KA:Z_VA], gk_ref[...], dy)
        dz_ref[:, Z_KA:Z_VA] = dx.astype(BF16)
        dgk_ref[...] += _fold8(dgc)
        dz_ref[:, Z_VA:Z_CQ] = (dva_ref[0] + dva_ref[1] + dva_ref[2] + dva_ref[3]).T.astype(BF16)
        dx, dgc = _rms_bwd_rows(z_ref[:, Z_CQ:Z_CKV], gqa_ref[...], dcqn_ref[...])
        dz_ref[:, Z_CQ:Z_CKV] = dx.astype(BF16)
        dgqa_ref[...] += _fold8(dgc)
        dx, dgc = _rms_bwd_rows(z_ref[:, Z_CKV:Z_KR], gkva_ref[...], dckvn_ref[...])
        dz_ref[:, Z_CKV:Z_KR] = dx.astype(BF16)
        dgkva_ref[...] += _fold8(dgc)
        dz_ref[:, Z_KR:Z_GATE] = dkr_ref[...].astype(BF16)
        dz_ref[:, Z_GATE:Z_GATE + D_MODEL] = dzga_ref[...]
        dz_ref[:, Z_GATE + D_MODEL:Z_W] = dzgb_ref[...]

    part = pl.BlockSpec((4, LANES, tm), lambda i: (0, 0, i))
    return pl.pallas_call(
        body, name="prep_a_bwd", grid=(t // tm,),
        in_specs=[_row_spec(tm, Z_ATT_W), _row_spec(tm, 512), part, part, _row_spec(tm, MLA_Q_RANK),
                  _row_spec(tm, MLA_KV_RANK), _row_spec(tm, LANES), _row_spec(tm, D_MODEL), _row_spec(tm, D_MODEL),
                  _vec_spec(LANES),
                  _vec_spec(LANES), _vec_spec(MLA_Q_RANK), _vec_spec(MLA_KV_RANK), _row_spec(tm, 512),
                  _row_spec(tm, 512)],
        out_specs=[_row_spec(tm, Z_W), _acc_spec(LANES), _acc_spec(LANES), _acc_spec(MLA_Q_RANK),
                   _acc_spec(MLA_KV_RANK)],
        out_shape=[jax.ShapeDtypeStruct((t, Z_W), BF16), jax.ShapeDtypeStruct((SUBLANES, LANES), F32),
                   jax.ShapeDtypeStruct((SUBLANES, LANES), F32), jax.ShapeDtypeStruct((SUBLANES, MLA_Q_RANK), F32),
                   jax.ShapeDtypeStruct((SUBLANES, MLA_KV_RANK), F32)],
        compiler_params=_params("arbitrary"),
    )(z, dqa, dka4, dva4, dcqn, dckvn, dkr, dzga, dzgb, gq2, gk2, gqa, gkva, cos_a, sin_a)


def _prep_b_fwd(qb, kvb, krr, cos_b, sin_b):
    t = qb.shape[0]
    tm = _tile(t, 256)

    def body(qb_ref, kvb_ref, krr_ref, cb_ref, sb_ref, q_ref, k_ref, v_ref):
        for h in range(MLA_HEADS):
            cols = slice(LANES * h, LANES * (h + 1))
            q_ref[:, cols] = (_rope(qb_ref[:, cols], cb_ref[...], sb_ref[...], 8) * (MLA_SCALE * LOG2E)).astype(BF16)
            k_ref[:, cols] = (kvb_ref[:, cols] + krr_ref[...]).astype(BF16)
        v_ref[...] = kvb_ref[:, 1024:1536].astype(BF16)

    return pl.pallas_call(
        body, name="prep_b_fwd", grid=(t // tm,),
        in_specs=[_row_spec(tm, 1024), _row_spec(tm, 1536), _row_spec(tm, LANES), _row_spec(tm, LANES),
                  _row_spec(tm, LANES)],
        out_specs=[_row_spec(tm, 1024), _row_spec(tm, 1024), _row_spec(tm, 512)],
        out_shape=[jax.ShapeDtypeStruct((t, 1024), BF16), jax.ShapeDtypeStruct((t, 1024), BF16),
                   jax.ShapeDtypeStruct((t, 512), BF16)],
        compiler_params=_params("parallel"),
    )(qb, kvb, krr, cos_b, sin_b)


def _prep_b_bwd(dq, dk, dv, cos_b, sin_b):
    t = dq.shape[0]
    tm = _tile(t, 256)

    def body(dq_ref, dk_ref, dv_ref, cb_ref, sb_ref, dqb_ref, dkvb_ref, dkr_ref):
        dkr = jnp.zeros((tm, LANES), F32)
        for h in range(MLA_HEADS):
            cols = slice(LANES * h, LANES * (h + 1))
            dqb_ref[:, cols] = _rope_bwd(dq_ref[:, cols] * MLA_SCALE, cb_ref[...], sb_ref[...], 8).astype(BF16)
            dkh = dk_ref[cols, :].T * LN2
            dkvb_ref[:, cols] = dkh.astype(BF16)
            dkr = dkr + dkh
        for j in range(MLA_HEADS // 2):
            dkvb_ref[:, 1024 + LANES * j:1024 + LANES * (j + 1)] = dv_ref[LANES * j:LANES * (j + 1), :].T.astype(BF16)
        dkr_ref[...] = _rope_bwd(dkr, cb_ref[...], sb_ref[...], 8)

    return pl.pallas_call(
        body, name="prep_b_bwd", grid=(t // tm,),
        in_specs=[_row_spec(tm, 1024), pl.BlockSpec((1024, tm), lambda i: (0, i)),
                  pl.BlockSpec((512, tm), lambda i: (0, i)), _row_spec(tm, LANES),
                  _row_spec(tm, LANES)],
        out_specs=[_row_spec(tm, 1024), _row_spec(tm, 1536), _row_spec(tm, LANES)],
        out_shape=[jax.ShapeDtypeStruct((t, 1024), BF16), jax.ShapeDtypeStruct((t, 1536), BF16),
                   jax.ShapeDtypeStruct((t, LANES), F32)],
        compiler_params=_params("parallel"),
    )(dq, dk, dv, cos_b, sin_b)


_NT = (((1,), (1,)), ((), ()))
_NN = (((1,), (0,)), ((), ()))
_TN = (((0,), (0,)), ((), ()))


def _head_operands(qv, kv, i, shared_k):
    if shared_k:
        lo = _lo_mask(qv.shape)
        keep = lo if i == 0 else jnp.logical_not(lo)
        return jnp.where(keep, qv, jnp.zeros_like(qv)), kv
    cols = slice(LANES * i, LANES * (i + 1))
    return qv[:, cols], kv[:, cols]


def _attn_specs(shared_k, tq, tk, q_of, k_of):
    wq = LANES if shared_k else 2 * LANES
    q_spec = pl.BlockSpec((tq, wq), lambda *g: (q_of(*g), g[0]))
    if shared_k:
        k_spec = pl.BlockSpec((tk, LANES), lambda *g: (k_of(*g), 0))
        v_spec = pl.BlockSpec((tk, LANES), lambda *g: (k_of(*g), 0))
    else:
        k_spec = pl.BlockSpec((tk, wq), lambda *g: (k_of(*g), g[0]))
        v_spec = pl.BlockSpec((tk, LANES), lambda *g: (k_of(*g), g[0]))
    return wq, q_spec, k_spec, v_spec


def _attn_fwd(q, k, v, shared_k, name):
    t = q.shape[0]
    tq, tk = _tile(t, ATTN_TQ), _tile(t, ATTN_TK)
    nq, nk = t // tq, t // tk
    wq, q_spec, k_spec, v_spec = _attn_specs(shared_k, tq, tk, lambda p, i, j: i, lambda p, i, j: j)
    groups = q.shape[1] // wq
    chunk = _tile(tq, 2 * LANES)

    def body(q_ref, k_ref, v_ref, o_ref, lse_ref, m_s, l_s, acc_s, alpha_s, s_s, p_s):
        kb = pl.program_id(2)

        @pl.when(kb == 0)
        def _():
            m_s[...] = jnp.full_like(m_s, -jnp.inf)
            l_s[...] = jnp.zeros_like(l_s)
            acc_s[...] = jnp.zeros_like(acc_s)

        qv, kv, vv = q_ref[...], k_ref[...], v_ref[...]
        for i in range(2):
            qi, ki = _head_operands(qv, kv, i, shared_k)
            s_s[i] = lax.dot_general(ki, qi, _NT, preferred_element_type=F32)
        for i in range(2):
            for c in range(tq // chunk):
                cols = slice(c * chunk, (c + 1) * chunk)
                m_prev = m_s[i, :, cols]
                m_new = jnp.maximum(m_prev, jnp.max(s_s[i, :, cols], axis=0, keepdims=True))
                alpha = jnp.exp2(m_prev - m_new)
                pt = jnp.exp2(s_s[i, :, cols] - m_new)
                l_s[i, :, cols] = alpha * l_s[i, :, cols] + jnp.sum(pt, axis=0, keepdims=True)
                m_s[i, :, cols] = m_new
                alpha_s[i, :, cols] = alpha
                p_s[i, :, cols] = pt.astype(BF16)
        for i in range(2):
            acc_s[i] = alpha_s[i] * acc_s[i] + lax.dot_general(vv, p_s[i], _TN, preferred_element_type=F32)

        @pl.when(kb == nk - 1)
        def _():
            o0 = acc_s[0] / l_s[0]
            o1 = acc_s[1] / l_s[1]
            row_lo = lax.broadcasted_iota(jnp.int32, o0.shape, 0) < HEAD_DIM
            o_ref[...] = jnp.where(row_lo, o0, o1).T.astype(BF16)
            lse_ref[0] = m_s[0] + jnp.log2(l_s[0])
            lse_ref[1] = m_s[1] + jnp.log2(l_s[1])

    return pl.pallas_call(
        body, name=name, grid=(groups, nq, nk),
        in_specs=[q_spec, k_spec, v_spec],
        out_specs=[pl.BlockSpec((tq, LANES), lambda p, i, j: (i, p)),
                   pl.BlockSpec((2, 1, tq), lambda p, i, j: (p, 0, i))],
        out_shape=[jax.ShapeDtypeStruct((t, LANES * groups), BF16),
                   jax.ShapeDtypeStruct((2 * groups, 1, t), F32)],
        scratch_shapes=[pltpu.VMEM((2, 1, tq), F32), pltpu.VMEM((2, 1, tq), F32), pltpu.VMEM((2, LANES, tq), F32),
                        pltpu.VMEM((2, 1, tq), F32), pltpu.VMEM((2, tk, tq), F32), pltpu.VMEM((2, tk, tq), BF16)],
        compiler_params=_params("parallel", "parallel", "arbitrary"),
    )(q, k, v)


def _attn_stats(do, o, lse):
    t, w = do.shape
    tm = _tile(t, 512)
    groups = w // LANES

    def body(do_ref, o_ref, lse_ref, delta_ref, lser_ref, dob_ref):
        dov = do_ref[...]
        dob_ref[...] = dov.astype(BF16)
        prod = dov * o_ref[...].astype(F32)
        for g in range(groups):
            x = prod[:, LANES * g:LANES * (g + 1)]
            lo = _lo_mask(x.shape)
            d0 = jnp.sum(jnp.where(lo, x, 0.0), axis=-1, keepdims=True)
            d1 = jnp.sum(jnp.where(lo, 0.0, x), axis=-1, keepdims=True)
            delta_ref[2 * g] = jnp.broadcast_to(d0, (tm, LANES))
            delta_ref[2 * g + 1] = jnp.broadcast_to(d1, (tm, LANES))
        for h in range(2 * groups):
            lser_ref[h] = jnp.broadcast_to(lse_ref[h], (LANES, tm)).T

    rep_spec = pl.BlockSpec((2 * groups, tm, LANES), lambda i: (0, i, 0))
    rep_shape = jax.ShapeDtypeStruct((2 * groups, t, LANES), F32)
    return pl.pallas_call(
        body, name="attn_stats", grid=(t // tm,),
        in_specs=[_row_spec(tm, w), _row_spec(tm, w), pl.BlockSpec((2 * groups, 1, tm), lambda i: (0, 0, i))],
        out_specs=[rep_spec, rep_spec, _row_spec(tm, w)],
        out_shape=[rep_shape, rep_shape, jax.ShapeDtypeStruct((t, w), BF16)],
        compiler_params=_params("parallel"),
    )(do, o, lse)


def _attn_bwd(q, k, v, do, lse, delta, shared_k, name):
    t = q.shape[0]
    tq, tk = _tile(t, ATTN_TQ), _tile(t, ATTN_TK)
    nq, nk = t // tq, t // tk
    wq, q_spec, k_spec, v_spec = _attn_specs(shared_k, tq, tk, lambda p, j, i: i, lambda p, j, i: j)
    groups = q.shape[1] // wq

    def body(q_ref, k_ref, v_ref, do_ref, lse_ref, delta_ref, dq_ref, dk_ref, dv_ref, dk_s, dv_s, s_s, dp_s, p_s,
             ds_s):
        kb, qb = pl.program_id(1), pl.program_id(2)

        @pl.when(qb == 0)
        def _():
            dk_s[...] = jnp.zeros_like(dk_s)
            dv_s[...] = jnp.zeros_like(dv_s)

        qv, kv, vv, dov = q_ref[...], k_ref[...], v_ref[...], do_ref[...]
        lo = _lo_mask(dov.shape)
        heads = []
        for i in range(2):
            qi, ki = _head_operands(qv, kv, i, shared_k)
            keep = lo if i == 0 else jnp.logical_not(lo)
            doi = jnp.where(keep, dov, jnp.zeros_like(dov))
            heads.append((qi, ki, doi))
            s_s[i] = lax.dot_general(qi, ki, _NT, preferred_element_type=F32)
            dp_s[i] = lax.dot_general(doi, vv, _NT, preferred_element_type=F32)
        for i in range(2):
            lse_i, delta_i = lse_ref[i], delta_ref[i]
            for c in range(tk // LANES):
                cols = slice(c * LANES, (c + 1) * LANES)
                p = jnp.exp2(s_s[i, :, cols] - lse_i)
                p_s[i, :, cols] = p.astype(BF16)
                ds_s[i, :, cols] = (p * (dp_s[i, :, cols] - delta_i)).astype(BF16)
        dq_parts = []
        for i in range(2):
            qi, ki, doi = heads[i]
            dv_s[...] += lax.dot_general(doi, p_s[i], _TN, preferred_element_type=F32)
            dk_i = lax.dot_general(qi, ds_s[i], _TN, preferred_element_type=F32)
            if shared_k:
                dk_s[...] += dk_i
            else:
                dk_s[LANES * i:LANES * (i + 1), :] += dk_i
            dq_parts.append(lax.dot_general(ds_s[i], ki, _NN, preferred_element_type=F32))
        rows = pl.ds(pl.multiple_of(qb * tq, tq), tq)
        if shared_k:
            tiles = [(slice(0, LANES), jnp.where(lo, dq_parts[0], dq_parts[1]))]
        else:
            tiles = [(slice(0, LANES), dq_parts[0]), (slice(LANES, 2 * LANES), dq_parts[1])]
        for cols, val in tiles:
            @pl.when(kb == 0)
            def _(cols=cols, val=val):
                dq_ref[rows, cols] = val

            @pl.when(kb > 0)
            def _(cols=cols, val=val):
                dq_ref[rows, cols] += val

        @pl.when(qb == nq - 1)
        def _():
            if shared_k:
                dk_ref[0] = dk_s[...]
                dv_ref[0] = dv_s[...]
            else:
                dk_ref[...] = dk_s[...]
                dv_ref[...] = dv_s[...]

    stat_spec = pl.BlockSpec((2, tq, LANES), lambda p, j, i: (p, i, 0))
    do_spec = pl.BlockSpec((tq, LANES), lambda p, j, i: (i, p))
    dq_spec = pl.BlockSpec((t, wq), lambda p, j, i: (0, p))
    if shared_k:
        dk_spec = pl.BlockSpec((1, LANES, tk), lambda p, j, i: (p, 0, j))
        dv_spec = dk_spec
        dk_shape = jax.ShapeDtypeStruct((groups, LANES, t), F32)
        dv_shape = dk_shape
    else:
        dk_spec = pl.BlockSpec((wq, tk), lambda p, j, i: (p, j))
        dv_spec = pl.BlockSpec((LANES, tk), lambda p, j, i: (p, j))
        dk_shape = jax.ShapeDtypeStruct((wq * groups, t), F32)
        dv_shape = jax.ShapeDtypeStruct((LANES * groups, t), F32)
    return pl.pallas_call(
        body, name=name, grid=(groups, nk, nq),
        in_specs=[q_spec, k_spec, v_spec, do_spec, stat_spec, stat_spec],
        out_specs=[dq_spec, dk_spec, dv_spec],
        out_shape=[jax.ShapeDtypeStruct((t, wq * groups), F32), dk_shape, dv_shape],
        scratch_shapes=[pltpu.VMEM((wq, tk), F32), pltpu.VMEM((LANES, tk), F32), pltpu.VMEM((2, tq, tk), F32),
                        pltpu.VMEM((2, tq, tk), F32), pltpu.VMEM((2, tq, tk), BF16), pltpu.VMEM((2, tq, tk), BF16)],
        compiler_params=_params("parallel", "arbitrary", "arbitrary"),
    )(q, k, v, do, lse, delta)


_MERGE_W = 512
_GATE_BLK0 = Z_GATE // _MERGE_W


def _merge_fwd(z, b_gate, ta, tb):
    t = z.shape[0]
    tm = _tile(t, 512)
    w = _MERGE_W
    nj = D_MODEL // w

    def body(za_ref, zb_ref, ba_ref, bb_ref, ta_ref, tb_ref, o_ref):
        ga = jax.nn.sigmoid(za_ref[...] + ba_ref[...])
        gb = jax.nn.sigmoid(zb_ref[...] + bb_ref[...])
        o_ref[...] = (ga * ta_ref[...] + gb * tb_ref[...]).astype(BF16)

    return pl.pallas_call(
        body, name="merge_fwd", grid=(t // tm, nj),
        in_specs=[pl.BlockSpec((tm, w), lambda i, j: (i, _GATE_BLK0 + j)),
                  pl.BlockSpec((tm, w), lambda i, j: (i, _GATE_BLK0 + nj + j)),
                  pl.BlockSpec((1, w), lambda i, j: (0, j)),
                  pl.BlockSpec((1, w), lambda i, j: (0, nj + j)),
                  pl.BlockSpec((tm, w), lambda i, j: (i, j)),
                  pl.BlockSpec((tm, w), lambda i, j: (i, j))],
        out_specs=pl.BlockSpec((tm, w), lambda i, j: (i, j)),
        out_shape=jax.ShapeDtypeStruct((t, D_MODEL), BF16),
        compiler_params=_params("parallel", "parallel"),
    )(z, z, b_gate, b_gate, ta, tb)


def _merge_bwd(dmg, z, b_gate, ta, tb):
    t = z.shape[0]
    tm = _tile(t, 512)
    w = _MERGE_W
    nj = D_MODEL // w

    def body(dm_ref, za_ref, zb_ref, ba_ref, bb_ref, ta_ref, tb_ref, dta_ref, dtb_ref, dza_ref, dzb_ref,
             dba_ref, dbb_ref):
        dm = dm_ref[...]
        ga = jax.nn.sigmoid(za_ref[...] + ba_ref[...])
        gb = jax.nn.sigmoid(zb_ref[...] + bb_ref[...])
        dta_ref[...] = (dm * ga).astype(BF16)
        dtb_ref[...] = (dm * gb).astype(BF16)
        dza = dm * ta_ref[...] * ga * (1.0 - ga)
        dzb = dm * tb_ref[...] * gb * (1.0 - gb)
        dza_ref[...] = dza.astype(BF16)
        dzb_ref[...] = dzb.astype(BF16)

        @pl.when(pl.program_id(1) == 0)
        def _():
            dba_ref[...] = jnp.zeros_like(dba_ref)
            dbb_ref[...] = jnp.zeros_like(dbb_ref)

        dba_ref[...] += _fold8(dza)
        dbb_ref[...] += _fold8(dzb)

    blk = pl.BlockSpec((tm, w), lambda j, i: (i, j))
    acc = pl.BlockSpec((SUBLANES, w), lambda j, i: (0, j))
    return pl.pallas_call(
        body, name="merge_bwd", grid=(nj, t // tm),
        in_specs=[blk,
                  pl.BlockSpec((tm, w), lambda j, i: (i, _GATE_BLK0 + j)),
                  pl.BlockSpec((tm, w), lambda j, i: (i, _GATE_BLK0 + nj + j)),
                  pl.BlockSpec((1, w), lambda j, i: (0, j)),
                  pl.BlockSpec((1, w), lambda j, i: (0, nj + j)),
                  blk, blk],
        out_specs=[blk, blk, blk, blk, acc, acc],
        out_shape=[jax.ShapeDtypeStruct((t, D_MODEL), BF16)] * 4 + [jax.ShapeDtypeStruct((SUBLANES, D_MODEL), F32)] * 2,
        compiler_params=_params("parallel", "arbitrary"),
    )(dmg, z, z, b_gate, b_gate, ta, tb)


def _relu2_fwd(h):
    t, f = h.shape
    tm, tn = _tile(t, 512), _tile(f, 1024)

    def body(h_ref, a_ref):
        r = jnp.maximum(h_ref[...], 0.0)
        a_ref[...] = (r * r).astype(BF16)

    spec = pl.BlockSpec((tm, tn), lambda i, j: (i, j))
    return pl.pallas_call(
        body, name="relu2_fwd", grid=(t // tm, f // tn), in_specs=[spec], out_specs=spec,
        out_shape=jax.ShapeDtypeStruct((t, f), BF16), compiler_params=_params("parallel", "parallel"),
    )(h)


def _relu2_bwd(da, h):
    t, f = h.shape
    tm, tn = _tile(t, 512), _tile(f, 1024)

    def body(da_ref, h_ref, dh_ref):
        dh_ref[...] = (da_ref[...] * (2.0 * jnp.maximum(h_ref[...], 0.0))).astype(BF16)

    spec = pl.BlockSpec((tm, tn), lambda i, j: (i, j))
    return pl.pallas_call(
        body, name="relu2_bwd", grid=(t // tm, f // tn), in_specs=[spec, spec], out_specs=spec,
        out_shape=jax.ShapeDtypeStruct((t, f), BF16), compiler_params=_params("parallel", "parallel"),
    )(da, h)


def _loss_grad(y, target):
    t, d = y.shape
    tm = _tile(t, 512)

    def body(y_ref, t_ref, dy_ref, acc_ref):
        err = y_ref[...] - t_ref[...]
        dy_ref[...] = err * (1.0 / d)
        e8 = _fold8(err * err)
        part = e8[:, 0:LANES]
        for c in range(1, d // LANES):
            part = part + e8[:, LANES * c:LANES * (c + 1)]

        @pl.when(pl.program_id(0) == 0)
        def _():
            acc_ref[...] = jnp.zeros_like(acc_ref)

        acc_ref[...] += part

    return pl.pallas_call(
        body, name="loss_grad", grid=(t // tm,),
        in_specs=[_row_spec(tm, d), _row_spec(tm, d)],
        out_specs=[_row_spec(tm, d), _acc_spec(LANES)],
        out_shape=[jax.ShapeDtypeStruct((t, d), F32), jax.ShapeDtypeStruct((SUBLANES, LANES), F32)],
        compiler_params=_params("arbitrary"),
    )(y, target)


_MESH_ID = pl.DeviceIdType.MESH
_ANY = pl.BlockSpec(memory_space=pl.ANY)


def _all_gather(arrays):
    n = len(arrays)

    def body(*refs):
        x_refs, out_refs = refs[:n], refs[n:2 * n]
        send_sems, recv_sems, local_sems = refs[2 * n:]
        mx, my, mc = lax.axis_index("x"), lax.axis_index("y"), lax.axis_index("c")
        me, sibling = (mx, my, mc), (mx, my, 1 - mc)
        chips = [(1 - mx, my), (mx, 1 - my), (1 - mx, 1 - my)]

        def slot(a, px, py, pc):
            return out_refs[a].at[4 * px + 2 * py + pc]

        def copy(a, sem, block, to, src=None):
            return pltpu.make_async_remote_copy(
                src_ref=slot(a, *block) if src is None else src, dst_ref=slot(a, *block),
                send_sem=send_sems.at[a, sem], recv_sem=recv_sems.at[a, sem], device_id=to, device_id_type=_MESH_ID)

        mine = [pltpu.make_async_copy(x_refs[a], slot(a, *me), local_sems.at[a]) for a in range(n)]
        first = []
        for a in range(n):
            mine[a].start()
            first.append(copy(a, 0, me, sibling, src=x_refs[a]))
            first += [copy(a, 1 + j, me, (*chip, mc), src=x_refs[a]) for j, chip in enumerate(chips)]
        for cp in first:
            cp.start()
        passed = []
        for a in range(n):
            for j, chip in enumerate(chips):
                copy(a, 1 + j, (*chip, mc), me).wait_recv()
                passed.append(copy(a, 4 + j, (*chip, mc), sibling))
                passed[-1].start()
        for a in range(n):
            copy(a, 0, sibling, me).wait_recv()
            for j, chip in enumerate(chips):
                copy(a, 4 + j, (*chip, 1 - mc), me).wait_recv()
        for cp in first + passed:
            cp.wait_send()
        for cp in mine:
            cp.wait()

    return pl.pallas_call(
        body, name="weight_all_gather",
        out_shape=[jax.ShapeDtypeStruct((N_DEV,) + a.shape, a.dtype) for a in arrays],
        in_specs=[_ANY] * n, out_specs=[_ANY] * n,
        scratch_shapes=[pltpu.SemaphoreType.DMA((n, 7)), pltpu.SemaphoreType.DMA((n, 7)),
                        pltpu.SemaphoreType.DMA((n,))],
    )(*arrays)


def _all_to_all(sends):
    n = len(sends)

    def body(*refs):
        s_refs, r_refs = refs[:n], refs[n:2 * n]
        send_sems, recv_sems, local_sems = refs[2 * n:]
        mx, my, mc = lax.axis_index("x"), lax.axis_index("y"), lax.axis_index("c")
        me = 4 * mx + 2 * my + mc
        local = [pltpu.make_async_copy(s_refs[a].at[me], r_refs[a].at[me], local_sems.at[a]) for a in range(n)]
        copies = []
        for a in range(n):
            local[a].start()
            for rel in range(1, N_DEV):
                px = 1 - mx if rel & 4 else mx
                py = 1 - my if rel & 2 else my
                pc = 1 - mc if rel & 1 else mc
                peer = 4 * px + 2 * py + pc
                cp = pltpu.make_async_remote_copy(
                    src_ref=s_refs[a].at[peer], dst_ref=r_refs[a].at[me], send_sem=send_sems.at[a, rel - 1],
                    recv_sem=recv_sems.at[a, rel - 1], device_id=(px, py, pc), device_id_type=_MESH_ID)
                cp.start()
                copies.append(cp)
        for cp in copies:
            cp.wait_send()
            cp.wait_recv()
        for cp in local:
            cp.wait()

    return pl.pallas_call(
        body, name="grad_all_to_all",
        out_shape=[jax.ShapeDtypeStruct(s.shape, s.dtype) for s in sends],
        in_specs=[_ANY] * n, out_specs=[_ANY] * n,
        scratch_shapes=[pltpu.SemaphoreType.DMA((n, 7)), pltpu.SemaphoreType.DMA((n, 7)),
                        pltpu.SemaphoreType.DMA((n,))],
    )(*sends)


def _adamw(recv, w, m, v):
    r, c_ = w.shape
    tr = min(r, ADAM_BLOCK_ELEMS // (pl.cdiv(c_, LANES) * LANES))
    while r % tr:
        tr -= SUBLANES

    def body(g_ref, w_ref, m_ref, v_ref, go_ref, d_ref, mo_ref, vo_ref):
        g = g_ref[0]
        for s in range(1, N_DEV):
            g = g + g_ref[s]
        go_ref[...] = g
        mn = ADAM_B1 * m_ref[...] + (1.0 - ADAM_B1) * g
        vn = ADAM_B2 * v_ref[...] + (1.0 - ADAM_B2) * (g * g)
        mo_ref[...] = mn
        vo_ref[...] = vn
        m_hat = mn / (1.0 - ADAM_B1 ** ADAM_STEP)
        v_hat = vn / (1.0 - ADAM_B2 ** ADAM_STEP)
        d_ref[...] = -ADAM_LR * (m_hat / (jnp.sqrt(v_hat) + ADAM_EPS) + ADAM_WD * w_ref[...])

    spec = pl.BlockSpec((tr, c_), lambda i: (i, 0))
    out = jax.ShapeDtypeStruct((r, c_), F32)
    return pl.pallas_call(
        body, name="grad_sum_adamw", grid=(r // tr,),
        in_specs=[pl.BlockSpec((N_DEV, tr, c_), lambda i: (0, i, 0)), spec, spec, spec],
        out_specs=[spec, spec, spec, spec], out_shape=[out, out, out, out],
        compiler_params=_params("parallel"),
    )(recv, w, m, v)


def _pad_cols(a, before, after):
    parts = []
    if before:
        parts.append(jnp.zeros(a.shape[:-1] + (before,), a.dtype))
    parts.append(a)
    if after:
        parts.append(jnp.zeros(a.shape[:-1] + (after,), a.dtype))
    return jnp.concatenate(parts, axis=-1)


def _q_head_pairs(a, axis):
    shp = a.shape
    a = a.reshape(shp[:axis] + (GQA_KV_HEADS, GQA_GROUP, HEAD_DIM) + shp[axis + 1:])
    a = jnp.swapaxes(a, axis, axis + 1)
    return a.reshape(shp)


def _q_head_unpairs(a, axis):
    shp = a.shape
    a = a.reshape(shp[:axis] + (GQA_GROUP, GQA_KV_HEADS, HEAD_DIM) + shp[axis + 1:])
    a = jnp.swapaxes(a, axis, axis + 1)
    return a.reshape(shp)


def _layout_weights(w):
    w_in = w["w_in"]
    lead = w_in.shape[:-1]
    w_in_p = jnp.concatenate([
        _q_head_pairs(w_in[..., 0:512], w_in.ndim - 1),
        w_in[..., 512:1408],
        _pad_cols(w_in[..., 1408:1440], KR_LANE0, LANES - KR_LANE0 - MLA_ROPE_DIM),
        w_in[..., 1440:],
    ], axis=-1)
    wq = w["w_q_up"]
    wq_p = _pad_cols(wq.reshape(wq.shape[:-1] + (MLA_HEADS, MLA_QK_DIM)), 0, LANES - MLA_QK_DIM)
    wq_p = wq_p.reshape(wq.shape[:-1] + (MLA_HEADS * LANES,))
    wkv = w["w_kv_up"]
    wkv4 = wkv.reshape(wkv.shape[:-1] + (MLA_HEADS, 2 * HEAD_DIM))
    wk_p = _pad_cols(wkv4[..., :HEAD_DIM], 0, LANES - HEAD_DIM).reshape(wkv.shape[:-1] + (MLA_HEADS * LANES,))
    wv_p = wkv4[..., HEAD_DIM:].reshape(wkv.shape[:-1] + (MLA_HEADS * HEAD_DIM,))
    del lead
    return {
        "w_in": w_in_p, "w_q_up": wq_p, "w_kv_up": jnp.concatenate([wk_p, wv_p], axis=-1),
        "w_branch_a": _q_head_pairs(w["w_branch_a"], w["w_branch_a"].ndim - 2), "w_branch_b": w["w_branch_b"],
        "w_o": w["w_o"], "w_ffn_up": w["w_ffn_up"], "w_ffn_down": w["w_ffn_down"],
    }


def _unlayout_grads(g):
    gi = g["w_in"]
    kr0 = Z_KR + KR_LANE0
    g_in = jnp.concatenate([
        _q_head_unpairs(gi[..., 0:512], gi.ndim - 1), gi[..., 512:1408], gi[..., kr0:kr0 + MLA_ROPE_DIM],
        gi[..., Z_GATE:],
    ], axis=-1)
    gq = g["w_q_up"]
    gq = gq.reshape(gq.shape[:-1] + (MLA_HEADS, LANES))[..., :MLA_QK_DIM]
    gq = gq.reshape(gq.shape[:-2] + (MLA_HEADS * MLA_QK_DIM,))
    gkv = g["w_kv_up"]
    gk = gkv[..., :MLA_HEADS * LANES].reshape(gkv.shape[:-1] + (MLA_HEADS, LANES))[..., :HEAD_DIM]
    gv = gkv[..., MLA_HEADS * LANES:].reshape(gkv.shape[:-1] + (MLA_HEADS, HEAD_DIM))
    gkv = jnp.concatenate([gk, gv], axis=-1).reshape(gkv.shape[:-1] + (MLA_HEADS * 2 * HEAD_DIM,))
    return {
        "w_in": g_in, "w_q_up": gq, "w_kv_up": gkv,
        "w_branch_a": _q_head_unpairs(g["w_branch_a"], g["w_branch_a"].ndim - 2), "w_branch_b": g["w_branch_b"],
        "w_o": g["w_o"], "w_ffn_up": g["w_ffn_up"], "w_ffn_down": g["w_ffn_down"],
    }


def _pack_small(parts):
    flat = jnp.concatenate([p.reshape(-1) for p in parts])
    pad = (-flat.shape[0]) % (SUBLANES * LANES)
    if pad:
        flat = jnp.concatenate([flat, jnp.zeros((pad,), flat.dtype)])
    return flat.reshape(-1, LANES)


def _unpack_small(packed, shapes):
    flat = packed.reshape(-1)
    out, off = [], 0
    for shp in shapes:
        n = int(np.prod(shp))
        out.append(flat[off:off + n].reshape(shp))
        off += n
    return out


def _shards_of(full, axis):
    shp = full.shape
    cut = shp[:axis] + (N_DEV, shp[axis] // N_DEV) + shp[axis + 1:]
    return jnp.moveaxis(full.reshape(cut), axis, 0)


def _from_shards(shards, axis):
    full = list(shards.shape[1:])
    full[axis] *= N_DEV
    return jnp.moveaxis(shards, 0, axis).reshape(full)


def _rows2d(a):
    return a.reshape(-1, a.shape[-1])


def _layer_fwd(x, u, lw, tabs):
    cos_a, sin_a, cos_b, sin_b = tabs
    z = _matmul(u, lw["w_in"], "nn", "mm_in")
    qa, ka, va, cqn, ckvn, krr = _prep_a_fwd(z, lw["gq2"], lw["gk2"], lw["gqa"], lw["gkva"], cos_a, sin_a, cos_b, sin_b)
    qb = _matmul(cqn, lw["w_q_up"], "nn", "mm_q_up")
    kvb = _matmul(ckvn, lw["w_kv_up"], "nn", "mm_kv_up")
    q_b, k_b, v_b = _prep_b_fwd(qb, kvb, krr, cos_b, sin_b)
    ya, lse_a = _attn_fwd(qa, ka, va, True, "gqa_fwd")
    yb, lse_b = _attn_fwd(q_b, k_b, v_b, False, "mla_fwd")
    ta = _matmul(ya, lw["w_branch_a"], "nn", "mm_branch_a")
    tb = _matmul(yb, lw["w_branch_b"], "nn", "mm_branch_b")
    merged = _merge_fwd(z, lw["b_gate"], ta, tb)
    m = _matmul(merged, lw["w_o"], "nn", "mm_o")
    x2, u2 = _res_norm_fwd(x, m, lw["post_mix_g"], lw["pre_ffn_g"])
    h = _matmul(u2, lw["w_ffn_up"], "nn", "mm_ffn_up")
    a = _relu2_fwd(h)
    f = _matmul(a, lw["w_ffn_down"], "nn", "mm_ffn_down")
    x3, u_next = _res_norm_fwd(x2, f, lw["post_ffn_g"], lw["next_pre_mix_g"])
    saved = dict(u=u, z=z, qa=qa, ka=ka, va=va, cqn=cqn, ckvn=ckvn, q_b=q_b, k_b=k_b, v_b=v_b, ya=ya, yb=yb,
                 lse_a=lse_a, lse_b=lse_b, ta=ta, tb=tb, merged=merged, m=m, x2=x2, u2=u2, h=h, a=a, f=f, x3=x3)
    return x3, u_next, saved


def _layer_bwd(dx3, du_next, lw, sv, tabs):
    cos_a, sin_a, cos_b, sin_b = tabs
    g = {}
    dx3, df, dg4, dg1n = _res_norm_bwd(sv["x3"], sv["f"], lw["post_ffn_g"], lw["next_pre_mix_g"], dx3, du_next)
    g["post_ffn_g"], g["next_pre_mix_g"] = dg4, dg1n
    da = _matmul(df, lw["w_ffn_down"], "nt", "mm_d_a")
    g["w_ffn_down"] = _matmul(sv["a"], df, "tn", "mm_dw_ffn_down")
    dh = _relu2_bwd(da, sv["h"])
    du2 = _matmul(dh, lw["w_ffn_up"], "nt", "mm_d_u2")
    g["w_ffn_up"] = _matmul(sv["u2"], dh, "tn", "mm_dw_ffn_up")
    dx2, dm, dg2, dg3 = _res_norm_bwd(sv["x2"], sv["m"], lw["post_mix_g"], lw["pre_ffn_g"], dx3, du2)
    g["post_mix_g"], g["pre_ffn_g"] = dg2, dg3
    dmg = _matmul(dm, lw["w_o"], "nt", "mm_d_merged")
    g["w_o"] = _matmul(sv["merged"], dm, "tn", "mm_dw_o")
    dta, dtb, dzg_a, dzg_b, db_a, db_b = _merge_bwd(dmg, sv["z"], lw["b_gate"], sv["ta"], sv["tb"])
    g["b_gate"] = jnp.concatenate([db_a, db_b], axis=-1)
    dya = _matmul(dta, lw["w_branch_a"], "nt", "mm_d_ya")
    g["w_branch_a"] = _matmul(sv["ya"], dta, "tn", "mm_dw_branch_a")
    dyb = _matmul(dtb, lw["w_branch_b"], "nt", "mm_d_yb")
    g["w_branch_b"] = _matmul(sv["yb"], dtb, "tn", "mm_dw_branch_b")
    delta_a, lse_a, dya16 = _attn_stats(dya, sv["ya"], sv["lse_a"])
    delta_b, lse_b, dyb16 = _attn_stats(dyb, sv["yb"], sv["lse_b"])
    dqa, dka4, dva4 = _attn_bwd(sv["qa"], sv["ka"], sv["va"], dya16, lse_a, delta_a, True, "gqa_bwd")
    dq_b, dk_b, dv_b = _attn_bwd(sv["q_b"], sv["k_b"], sv["v_b"], dyb16, lse_b, delta_b, False, "mla_bwd")
    dqb, dkvb, dkr = _prep_b_bwd(dq_b, dk_b, dv_b, cos_b, sin_b)
    dcqn = _matmul(dqb, lw["w_q_up"], "nt", "mm_d_cqn")
    g["w_q_up"] = _matmul(sv["cqn"], dqb, "tn", "mm_dw_q_up")
    dckvn = _matmul(dkvb, lw["w_kv_up"], "nt", "mm_d_ckvn")
    g["w_kv_up"] = _matmul(sv["ckvn"], dkvb, "tn", "mm_dw_kv_up")
    dz, dgq, dgk, dgqa, dgkva = _prep_a_bwd(sv["z"], dqa, dka4, dva4, dcqn, dckvn, dkr, dzg_a, dzg_b, lw["gq2"],
                                            lw["gk2"], lw["gqa"], lw["gkva"], cos_a, sin_a)
    g["q_norm_g"], g["k_norm_g"], g["q_a_norm_g"], g["kv_a_norm_g"] = dgq, dgk, dgqa, dgkva
    du = _matmul(dz, lw["w_in"], "nt", "mm_d_u")
    g["w_in"] = _matmul(sv["u"], dz, "tn", "mm_dw_in")
    return dx2, du, g


def kernel(x, w_in, b_gate, q_norm_g, k_norm_g, q_a_norm_g, kv_a_norm_g, w_q_up, w_kv_up, w_branch_a, w_branch_b, w_o, w_ffn_up, w_ffn_down, pre_mix_g, post_mix_g, pre_ffn_g, post_ffn_g, loss_target, m_w_in, m_b_gate, m_q_norm_g, m_k_norm_g, m_q_a_norm_g, m_kv_a_norm_g, m_w_q_up, m_w_kv_up, m_w_branch_a, m_w_branch_b, m_w_o, m_w_ffn_up, m_w_ffn_down, m_pre_mix_g, m_post_mix_g, m_pre_ffn_g, m_post_ffn_g, v_w_in, v_b_gate, v_q_norm_g, v_k_norm_g, v_q_a_norm_g, v_kv_a_norm_g, v_w_q_up, v_w_kv_up, v_w_branch_a, v_w_branch_b, v_w_o, v_w_ffn_up, v_w_ffn_down, v_pre_mix_g, v_post_mix_g, v_pre_ffn_g, v_post_ffn_g):
    weights = dict(zip(WEIGHT_NAMES, (w_in, b_gate, q_norm_g, k_norm_g, q_a_norm_g, kv_a_norm_g, w_q_up, w_kv_up,
                                      w_branch_a, w_branch_b, w_o, w_ffn_up, w_ffn_down, pre_mix_g, post_mix_g,
                                      pre_ffn_g, post_ffn_g)))
    mom_m = dict(zip(WEIGHT_NAMES, (m_w_in, m_b_gate, m_q_norm_g, m_k_norm_g, m_q_a_norm_g, m_kv_a_norm_g, m_w_q_up,
                                    m_w_kv_up, m_w_branch_a, m_w_branch_b, m_w_o, m_w_ffn_up, m_w_ffn_down,
                                    m_pre_mix_g, m_post_mix_g, m_pre_ffn_g, m_post_ffn_g)))
    mom_v = dict(zip(WEIGHT_NAMES, (v_w_in, v_b_gate, v_q_norm_g, v_k_norm_g, v_q_a_norm_g, v_kv_a_norm_g, v_w_q_up,
                                    v_w_kv_up, v_w_branch_a, v_w_branch_b, v_w_o, v_w_ffn_up, v_w_ffn_down,
                                    v_pre_mix_g, v_post_mix_g, v_pre_ffn_g, v_post_ffn_g)))
    assert x.shape[0] == 1 and x.shape[2] == D_MODEL, x.shape
    n_layers = w_in.shape[0]
    t = x.shape[1]
    x0 = x.reshape(t, D_MODEL)
    target = loss_target.reshape(t, D_MODEL)
    shard_shapes = {n: weights[n].shape for n in BIG_NAMES}
    small_shapes = [weights[n].shape for n in SMALL_NAMES]

    gathered = _all_gather([weights[n].astype(BF16) for n in BIG_NAMES])
    full = {n: _from_shards(g, SHARD_AXIS[n]) for n, g in zip(BIG_NAMES, gathered)}
    lw_all = _layout_weights(full)
    lw_all["b_gate"] = b_gate.reshape(n_layers, 1, 2 * D_MODEL)
    lw_all["gq2"] = jnp.tile(q_norm_g, (1, 2)).reshape(n_layers, 1, LANES)
    lw_all["gk2"] = jnp.tile(k_norm_g, (1, 2)).reshape(n_layers, 1, LANES)
    lw_all["gqa"] = q_a_norm_g.reshape(n_layers, 1, MLA_Q_RANK)
    lw_all["gkva"] = kv_a_norm_g.reshape(n_layers, 1, MLA_KV_RANK)
    for n in ("post_mix_g", "pre_ffn_g", "post_ffn_g"):
        lw_all[n] = weights[n]
    lw_all["next_pre_mix_g"] = jnp.roll(pre_mix_g, -1, axis=0)

    tabs = _rope_tables(t)
    u0 = _rms_fwd(x0, pre_mix_g[0])

    layer_w = [{n: a[l] for n, a in lw_all.items()} for l in range(n_layers)]
    xc, uc, saved = x0, u0, []
    for l in range(n_layers):
        xc, uc, sv = _layer_fwd(xc, uc, layer_w[l], tabs)
        saved.append(sv)
    dy, loss_acc = _loss_grad(xc, target)
    loss = lax.psum(0.5 * jnp.sum(loss_acc) / D_MODEL, ("x", "y", "c"))

    dx0, du0, layer_g = dy, jnp.zeros((t, D_MODEL), F32), [None] * n_layers
    for l in reversed(range(n_layers)):
        dx0, du0, layer_g[l] = _layer_bwd(dx0, du0, layer_w[l], saved[l], tabs)
    grads = {n: jnp.stack([g[n] for g in layer_g]) for n in layer_g[0]}
    grad_x, dg1_first = _rms_bwd(x0, pre_mix_g[0], dx0, du0)

    big_grads = _unlayout_grads({n: grads[n] for n in BIG_NAMES})
    fold = lambda a: a.sum(axis=1)
    dgq = fold(grads["q_norm_g"]).reshape(n_layers, 2, HEAD_DIM).sum(axis=1)
    dgk = fold(grads["k_norm_g"]).reshape(n_layers, 2, HEAD_DIM).sum(axis=1)
    dg1 = jnp.concatenate([fold(dg1_first[None]), fold(grads["next_pre_mix_g"])[:-1]], axis=0)
    small_grads = {
        "b_gate": fold(grads["b_gate"]), "q_norm_g": dgq, "k_norm_g": dgk, "q_a_norm_g": fold(grads["q_a_norm_g"]),
        "kv_a_norm_g": fold(grads["kv_a_norm_g"]), "pre_mix_g": dg1, "post_mix_g": fold(grads["post_mix_g"]),
        "pre_ffn_g": fold(grads["pre_ffn_g"]), "post_ffn_g": fold(grads["post_ffn_g"]),
    }
    small_packed = _pack_small([small_grads[n] for n in SMALL_NAMES])
    sends = [_shards_of(big_grads[n], SHARD_AXIS[n]) for n in BIG_NAMES]
    sends.append(jnp.broadcast_to(small_packed[None], (N_DEV,) + small_packed.shape))
    recvs = _all_to_all(sends)

    results = {}
    for n, recv in zip(BIG_NAMES, recvs):
        res = _adamw(recv.reshape((N_DEV,) + _rows2d(weights[n]).shape), _rows2d(weights[n]), _rows2d(mom_m[n]),
                     _rows2d(mom_v[n]))
        results[n] = [r.reshape(shard_shapes[n]) for r in res]
    res = _adamw(recvs[-1], *[_pack_small([d[n] for n in SMALL_NAMES]) for d in (weights, mom_m, mom_v)])
    for kind, packed_out in enumerate(res):
        for n, val in zip(SMALL_NAMES, _unpack_small(packed_out, small_shapes)):
            results.setdefault(n, [None] * 4)[kind] = val
    outs = [results[n][kind] for kind in range(4) for n in WEIGHT_NAMES]
    return (loss, grad_x.reshape(x.shape), *outs)
```

```python
import functools
import math

import jax
import jax.numpy as jnp
import numpy as np
from jax import lax
from jax.experimental import pallas as pl
from jax.experimental.pallas import tpu as pltpu

F32 = jnp.float32
BF16 = jnp.bfloat16

D_MODEL = 1024
GRID_W = 64
ROPE_THETA = 10000.0
EPS = 1e-6
GQA_HEADS = 8
GQA_KV_HEADS = 2
GQA_GROUP = GQA_HEADS // GQA_KV_HEADS
HEAD_DIM = 64
MLA_HEADS = 8
MLA_ROPE_DIM = 32
MLA_QK_DIM = 96
MLA_Q_RANK = 384
MLA_KV_RANK = 256
D_FF = 4 * D_MODEL
GQA_SCALE = 1.0 / math.sqrt(HEAD_DIM)
MLA_SCALE = 1.0 / math.sqrt(MLA_QK_DIM)
LOG2E = math.log2(math.e)
LN2 = math.log(2.0)

ADAM_LR = 0.001
ADAM_B1 = 0.9
ADAM_B2 = 0.999
ADAM_EPS = 1e-08
ADAM_WD = 0.01
ADAM_STEP = 10

N_DEV = 8
LANES = 128
SUBLANES = 8
VMEM_LIMIT = 48 * 1024 * 1024

Z_QA, Z_KA, Z_VA, Z_CQ, Z_CKV, Z_KR, Z_GATE = 0, 512, 640, 768, 1152, 1408, 1536
Z_ATT_W = 1536
Z_W = 3584
KR_LANE0 = 64

WEIGHT_NAMES = ("w_in", "b_gate", "q_norm_g", "k_norm_g", "q_a_norm_g", "kv_a_norm_g", "w_q_up", "w_kv_up",
                "w_branch_a", "w_branch_b", "w_o", "w_ffn_up", "w_ffn_down", "pre_mix_g", "post_mix_g",
                "pre_ffn_g", "post_ffn_g")
SHARD_AXIS = {"w_in": 2, "w_q_up": 2, "w_kv_up": 2, "w_branch_a": 2, "w_branch_b": 2, "w_o": 1, "w_ffn_up": 2,
              "w_ffn_down": 1}
BIG_NAMES = tuple(n for n in WEIGHT_NAMES if n in SHARD_AXIS)
SMALL_NAMES = tuple(n for n in WEIGHT_NAMES if n not in SHARD_AXIS)
ADAM_BLOCK_ELEMS = 256 * 1024
MM_TILE = 1024
ATTN_TQ = 1024
ATTN_TK = 512


def _params(*semantics):
    return pltpu.CompilerParams(dimension_semantics=semantics, vmem_limit_bytes=VMEM_LIMIT)


def _tile(n, pref):
    if n <= pref:
        return n
    t = (pref // LANES) * LANES
    while n % t:
        t -= LANES
    return t


def _fold8(t):
    return t.reshape(t.shape[0] // SUBLANES, SUBLANES, t.shape[1]).sum(axis=0)


_DIMS = {"nn": ((1,), (0,)), "nt": ((1,), (1,)), "tn": ((0,), (0,))}


def _matmul(a, b, mode, name):
    if mode == "nn":
        (m, k), n = a.shape, b.shape[1]
    elif mode == "nt":
        (m, k), n = a.shape, b.shape[0]
    else:
        (k, m), n = a.shape, b.shape[1]
    tm, tn, tk = _tile(m, MM_TILE), _tile(n, MM_TILE), _tile(k, MM_TILE)
    nk = k // tk
    dims = (_DIMS[mode], ((), ()))

    def body(a_ref, b_ref, o_ref, acc_ref):
        prod = lax.dot_general(a_ref[...], b_ref[...], dims, preferred_element_type=F32)
        if nk == 1:
            o_ref[...] = prod
        else:
            kk = pl.program_id(2)

            @pl.when(kk == 0)
            def _():
                acc_ref[...] = prod

            @pl.when(kk > 0)
            def _():
                acc_ref[...] += prod

            @pl.when(kk == nk - 1)
            def _():
                o_ref[...] = acc_ref[...]

    if mode == "tn":
        a_spec = pl.BlockSpec((tk, tm), lambda i, j, kk: (kk, i))
    else:
        a_spec = pl.BlockSpec((tm, tk), lambda i, j, kk: (i, kk))
    if mode == "nt":
        b_spec = pl.BlockSpec((tn, tk), lambda i, j, kk: (j, kk))
    else:
        b_spec = pl.BlockSpec((tk, tn), lambda i, j, kk: (kk, j))
    return pl.pallas_call(
        body,
        name=name,
        grid=(m // tm, n // tn, nk),
        in_specs=[a_spec, b_spec],
        out_specs=pl.BlockSpec((tm, tn), lambda i, j, kk: (i, j)),
        out_shape=jax.ShapeDtypeStruct((m, n), F32),
        scratch_shapes=[pltpu.VMEM((tm, tn), F32)],
        compiler_params=_params("parallel", "parallel", "arbitrary"),
    )(a, b)


def _rinv(x):
    return lax.rsqrt(jnp.mean(x * x, axis=-1, keepdims=True) + EPS)


def _rms_bwd_rows(x, g, dy):
    r = _rinv(x)
    xh = x * r
    dxh = dy * g
    dx = r * (dxh - xh * jnp.mean(dxh * xh, axis=-1, keepdims=True))
    return dx, dy * xh


def _row_spec(tm, c):
    return pl.BlockSpec((tm, c), lambda i: (i, 0))


def _vec_spec(c):
    return pl.BlockSpec((1, c), lambda i: (0, 0))


def _acc_spec(c):
    return pl.BlockSpec((SUBLANES, c), lambda i: (0, 0))


def _rms_fwd(x, g):
    t, d = x.shape
    tm = _tile(t, 512)

    def body(x_ref, g_ref, o_ref):
        xv = x_ref[...]
        o_ref[...] = (xv * _rinv(xv) * g_ref[...]).astype(BF16)

    return pl.pallas_call(
        body, name="rms_fwd", grid=(t // tm,),
        in_specs=[_row_spec(tm, d), _vec_spec(d)], out_specs=_row_spec(tm, d),
        out_shape=jax.ShapeDtypeStruct((t, d), BF16), compiler_params=_params("parallel"),
    )(x, g.reshape(1, d))


def _rms_bwd(x, g, dres, dy):
    t, d = x.shape
    tm = _tile(t, 512)

    def body(x_ref, g_ref, dres_ref, dy_ref, dx_ref, dg_ref):
        dx, dgc = _rms_bwd_rows(x_ref[...], g_ref[...], dy_ref[...])
        dx_ref[...] = dres_ref[...] + dx

        @pl.when(pl.program_id(0) == 0)
        def _():
            dg_ref[...] = jnp.zeros_like(dg_ref)

        dg_ref[...] += _fold8(dgc)

    return pl.pallas_call(
        body, name="rms_bwd", grid=(t // tm,),
        in_specs=[_row_spec(tm, d), _vec_spec(d), _row_spec(tm, d), _row_spec(tm, d)],
        out_specs=[_row_spec(tm, d), _acc_spec(d)],
        out_shape=[jax.ShapeDtypeStruct((t, d), F32), jax.ShapeDtypeStruct((SUBLANES, d), F32)],
        compiler_params=_params("arbitrary"),
    )(x, g.reshape(1, d), dres, dy)


def _res_norm_fwd(x, m, g_post, g_next):
    t, d = x.shape
    tm = _tile(t, 512)

    def body(x_ref, m_ref, gp_ref, gn_ref, x2_ref, u2_ref):
        mv = m_ref[...]
        x2 = x_ref[...] + mv * _rinv(mv) * gp_ref[...]
        x2_ref[...] = x2
        u2_ref[...] = (x2 * _rinv(x2) * gn_ref[...]).astype(BF16)

    return pl.pallas_call(
        body, name="res_norm_fwd", grid=(t // tm,),
        in_specs=[_row_spec(tm, d), _row_spec(tm, d), _vec_spec(d), _vec_spec(d)],
        out_specs=[_row_spec(tm, d), _row_spec(tm, d)],
        out_shape=[jax.ShapeDtypeStruct((t, d), F32), jax.ShapeDtypeStruct((t, d), BF16)],
        compiler_params=_params("parallel"),
    )(x, m, g_post.reshape(1, d), g_next.reshape(1, d))


def _res_norm_bwd(x2, m, g_post, g_next, dx2_in, du2):
    t, d = x2.shape
    tm = _tile(t, 512)

    def body(x2_ref, m_ref, gp_ref, gn_ref, dx2in_ref, du2_ref, dx2_ref, dm_ref, dgp_ref, dgn_ref):
        dxn, dgn_c = _rms_bwd_rows(x2_ref[...], gn_ref[...], du2_ref[...])
        dx2 = dx2in_ref[...] + dxn
        dx2_ref[...] = dx2
        dm, dgp_c = _rms_bwd_rows(m_ref[...], gp_ref[...], dx2)
        dm_ref[...] = dm.astype(BF16)

        @pl.when(pl.program_id(0) == 0)
        def _():
            dgp_ref[...] = jnp.zeros_like(dgp_ref)
            dgn_ref[...] = jnp.zeros_like(dgn_ref)

        dgp_ref[...] += _fold8(dgp_c)
        dgn_ref[...] += _fold8(dgn_c)

    return pl.pallas_call(
        body, name="res_norm_bwd", grid=(t // tm,),
        in_specs=[_row_spec(tm, d), _row_spec(tm, d), _vec_spec(d), _vec_spec(d), _row_spec(tm, d), _row_spec(tm, d)],
        out_specs=[_row_spec(tm, d), _row_spec(tm, d), _acc_spec(d), _acc_spec(d)],
        out_shape=[jax.ShapeDtypeStruct((t, d), F32), jax.ShapeDtypeStruct((t, d), BF16),
                   jax.ShapeDtypeStruct((SUBLANES, d), F32), jax.ShapeDtypeStruct((SUBLANES, d), F32)],
        compiler_params=_params("arbitrary"),
    )(x2, m, g_post.reshape(1, d), g_next.reshape(1, d), dx2_in, du2)


def _rope_tables(t):
    rows = t // GRID_W
    row = jnp.repeat(jnp.arange(rows, dtype=F32), GRID_W)
    col = jnp.tile(jnp.arange(GRID_W, dtype=F32), rows)

    def tab(rot_dim):
        half = rot_dim // 2
        inv = ROPE_THETA ** (-jnp.arange(0, half, 2, dtype=F32) / half)
        ar = row[:, None] * inv[None, :]
        ac = col[:, None] * inv[None, :]
        ang = jnp.concatenate([ar, ar, ac, ac], axis=-1)
        q = half // 2
        sign = np.tile(np.concatenate([-np.ones(q, np.float32), np.ones(q, np.float32)]), 2)
        return jnp.cos(ang), jnp.sin(ang) * sign[None, :]

    ca, sa = tab(HEAD_DIM)
    cb, sb = tab(MLA_ROPE_DIM)
    one = jnp.ones((t, 1), F32)
    cos_b = jnp.concatenate([one * jnp.ones((1, KR_LANE0), F32), cb, one * jnp.ones((1, 32), F32)], axis=-1)
    sin_b = jnp.concatenate([jnp.zeros((t, KR_LANE0), F32), sb, jnp.zeros((t, 32), F32)], axis=-1)
    return jnp.tile(ca, (1, GQA_HEADS)), jnp.tile(sa, (1, GQA_HEADS)), cos_b, sin_b


def _swap_halves(x, sh):
    lane = lax.broadcasted_iota(jnp.int32, x.shape, 1)
    up = pltpu.roll(x, LANES - sh, 1)
    dn = pltpu.roll(x, sh, 1)
    return jnp.where((lane & (2 * sh - 1)) < sh, up, dn)


def _rope(x, cos, sin_s, sh):
    return x * cos + _swap_halves(x, sh) * sin_s


def _rope_bwd(dy, cos, sin_s, sh):
    return dy * cos + _swap_halves(dy * sin_s, sh)


def _lo_mask(shape):
    return lax.broadcasted_iota(jnp.int32, shape, 1) < HEAD_DIM


def _half_mean(t, lo):
    s_lo = jnp.sum(jnp.where(lo, t, 0.0), axis=-1, keepdims=True)
    s_hi = jnp.sum(jnp.where(lo, 0.0, t), axis=-1, keepdims=True)
    return jnp.where(lo, s_lo, s_hi) * (1.0 / HEAD_DIM)


def _head_norm(x, g2):
    lo = _lo_mask(x.shape)
    r = lax.rsqrt(_half_mean(x * x, lo) + EPS)
    return x * r * g2


def _head_norm_bwd(x, g2, dy):
    lo = _lo_mask(x.shape)
    r = lax.rsqrt(_half_mean(x * x, lo) + EPS)
    xh = x * r
    dxh = dy * g2
    dx = r * (dxh - xh * _half_mean(dxh * xh, lo))
    return dx, dy * xh


def _prep_a_fwd(z, gq2, gk2, gqa, gkva, cos_a, sin_a, cos_b, sin_b):
    t = z.shape[0]
    tm = _tile(t, 256)

    def body(z_ref, gq_ref, gk_ref, gqa_ref, gkva_ref, ca_ref, sa_ref, cb_ref, sb_ref,
             qa_ref, ka_ref, va_ref, cqn_ref, ckvn_ref, krr_ref):
        for j in range(4):
            cols = slice(LANES * j, LANES * (j + 1))
            y = _rope(_head_norm(z_ref[:, cols], gq_ref[...]), ca_ref[:, cols], sa_ref[:, cols], 16)
            qa_ref[:, cols] = (y * (GQA_SCALE * LOG2E)).astype(BF16)
        y = _rope(_head_norm(z_ref[:, Z_KA:Z_VA], gk_ref[...]), ca_ref[:, :LANES], sa_ref[:, :LANES], 16)
        ka_ref[...] = y.astype(BF16)
        va_ref[...] = z_ref[:, Z_VA:Z_CQ].astype(BF16)
        cq = z_ref[:, Z_CQ:Z_CKV]
        cqn_ref[...] = (cq * _rinv(cq) * gqa_ref[...]).astype(BF16)
        ckv = z_ref[:, Z_CKV:Z_KR]
        ckvn_ref[...] = (ckv * _rinv(ckv) * gkva_ref[...]).astype(BF16)
        krr_ref[...] = _rope(z_ref[:, Z_KR:Z_GATE], cb_ref[...], sb_ref[...], 8)

    return pl.pallas_call(
        body, name="prep_a_fwd", grid=(t // tm,),
        in_specs=[_row_spec(tm, Z_ATT_W), _vec_spec(LANES), _vec_spec(LANES), _vec_spec(MLA_Q_RANK),
                  _vec_spec(MLA_KV_RANK), _row_spec(tm, 512), _row_spec(tm, 512), _row_spec(tm, LANES),
                  _row_spec(tm, LANES)],
        out_specs=[_row_spec(tm, 512), _row_spec(tm, LANES), _row_spec(tm, LANES), _row_spec(tm, MLA_Q_RANK),
                   _row_spec(tm, MLA_KV_RANK), _row_spec(tm, LANES)],
        out_shape=[jax.ShapeDtypeStruct((t, 512), BF16), jax.ShapeDtypeStruct((t, LANES), BF16),
                   jax.ShapeDtypeStruct((t, LANES), BF16), jax.ShapeDtypeStruct((t, MLA_Q_RANK), BF16),
                   jax.ShapeDtypeStruct((t, MLA_KV_RANK), BF16), jax.ShapeDtypeStruct((t, LANES), F32)],
        compiler_params=_params("parallel"),
    )(z, gq2, gk2, gqa, gkva, cos_a, sin_a, cos_b, sin_b)


def _prep_a_bwd(z, dqa, dka4, dva4, dcqn, dckvn, dkr, dzga, dzgb, gq2, gk2, gqa, gkva, cos_a, sin_a):
    t = z.shape[0]
    tm = _tile(t, 256)

    def body(z_ref, dqa_ref, dka_ref, dva_ref, dcqn_ref, dckvn_ref, dkr_ref, dzga_ref, dzgb_ref, gq_ref, gk_ref,
             gqa_ref, gkva_ref, ca_ref, sa_ref, dz_ref, dgq_ref, dgk_ref, dgqa_ref, dgkva_ref):
        @pl.when(pl.program_id(0) == 0)
        def _():
            dgq_ref[...] = jnp.zeros_like(dgq_ref)
            dgk_ref[...] = jnp.zeros_like(dgk_ref)
            dgqa_ref[...] = jnp.zeros_like(dgqa_ref)
            dgkva_ref[...] = jnp.zeros_like(dgkva_ref)

        dgq = jnp.zeros((SUBLANES, LANES), F32)
        for j in range(4):
            cols = slice(LANES * j, LANES * (j + 1))
            dy = _rope_bwd(dqa_ref[:, cols] * GQA_SCALE, ca_ref[:, cols], sa_ref[:, cols], 16)
            dx, dgc = _head_norm_bwd(z_ref[:, cols], gq_ref[...], dy)
            dz_ref[:, cols] = dx.astype(BF16)
            dgq = dgq + _fold8(dgc)
        dgq_ref[...] += dgq
        dk = (dka_ref[0] + dka_ref[1] + dka_ref[2] + dka_ref[3]).T * LN2
        dy = _rope_bwd(dk, ca_ref[:, :LANES], sa_ref[:, :LANES], 16)
        dx, dgc = _head_norm_bwd(z_ref[:, Z_KA:Z_VA], gk_ref[...], dy)
        dz_ref[:, Z_KA:Z_VA] = dx.astype(BF16)
        dgk_ref[...] += _fold8(dgc)
        dz_ref[:, Z_VA:Z_CQ] = (dva_ref[0] + dva_ref[1] + dva_ref[2] + dva_ref[3]).T.astype(BF16)
        dx, dgc = _rms_bwd_rows(z_ref[:, Z_CQ:Z_CKV], gqa_ref[...], dcqn_ref[...])
        dz_ref[:, Z_CQ:Z_CKV] = dx.astype(BF16)
        dgqa_ref[...] += _fold8(dgc)
        dx, dgc = _rms_bwd_rows(z_ref[:, Z_CKV:Z_KR], gkva_ref[...], dckvn_ref[...])
        dz_ref[:, Z_CKV:Z_KR] = dx.astype(BF16)
        dgkva_ref[...] += _fold8(dgc)
        dz_ref[:, Z_KR:Z_GATE] = dkr_ref[...].astype(BF16)
        dz_ref[:, Z_GATE:Z_GATE + D_MODEL] = dzga_ref[...]
        dz_ref[:, Z_GATE + D_MODEL:Z_W] = dzgb_ref[...]

    part = pl.BlockSpec((4, LANES, tm), lambda i: (0, 0, i))
    return pl.pallas_call(
        body, name="prep_a_bwd", grid=(t // tm,),
        in_specs=[_row_spec(tm, Z_ATT_W), _row_spec(tm, 512), part, part, _row_spec(tm, MLA_Q_RANK),
                  _row_spec(tm, MLA_KV_RANK), _row_spec(tm, LANES), _row_spec(tm, D_MODEL), _row_spec(tm, D_MODEL),
                  _vec_spec(LANES),
                  _vec_spec(LANES), _vec_spec(MLA_Q_RANK), _vec_spec(MLA_KV_RANK), _row_spec(tm, 512),
                  _row_spec(tm, 512)],
        out_specs=[_row_spec(tm, Z_W), _acc_spec(LANES), _acc_spec(LANES), _acc_spec(MLA_Q_RANK),
                   _acc_spec(MLA_KV_RANK)],
        out_shape=[jax.ShapeDtypeStruct((t, Z_W), BF16), jax.ShapeDtypeStruct((SUBLANES, LANES), F32),
                   jax.ShapeDtypeStruct((SUBLANES, LANES), F32), jax.ShapeDtypeStruct((SUBLANES, MLA_Q_RANK), F32),
                   jax.ShapeDtypeStruct((SUBLANES, MLA_KV_RANK), F32)],
        compiler_params=_params("arbitrary"),
    )(z, dqa, dka4, dva4, dcqn, dckvn, dkr, dzga, dzgb, gq2, gk2, gqa, gkva, cos_a, sin_a)


def _prep_b_fwd(qb, kvb, krr, cos_b, sin_b):
    t = qb.shape[0]
    tm = _tile(t, 256)

    def body(qb_ref, kvb_ref, krr_ref, cb_ref, sb_ref, q_ref, k_ref, v_ref):
        for h in range(MLA_HEADS):
            cols = slice(LANES * h, LANES * (h + 1))
            q_ref[:, cols] = (_rope(qb_ref[:, cols], cb_ref[...], sb_ref[...], 8) * (MLA_SCALE * LOG2E)).astype(BF16)
            k_ref[:, cols] = (kvb_ref[:, cols] + krr_ref[...]).astype(BF16)
        v_ref[...] = kvb_ref[:, 1024:1536].astype(BF16)

    return pl.pallas_call(
        body, name="prep_b_fwd", grid=(t // tm,),
        in_specs=[_row_spec(tm, 1024), _row_spec(tm, 1536), _row_spec(tm, LANES), _row_spec(tm, LANES),
                  _row_spec(tm, LANES)],
        out_specs=[_row_spec(tm, 1024), _row_spec(tm, 1024), _row_spec(tm, 512)],
        out_shape=[jax.ShapeDtypeStruct((t, 1024), BF16), jax.ShapeDtypeStruct((t, 1024), BF16),
                   jax.ShapeDtypeStruct((t, 512), BF16)],
        compiler_params=_params("parallel"),
    )(qb, kvb, krr, cos_b, sin_b)


def _prep_b_bwd(dq, dk, dv, cos_b, sin_b):
    t = dq.shape[0]
    tm = _tile(t, 256)

    def body(dq_ref, dk_ref, dv_ref, cb_ref, sb_ref, dqb_ref, dkvb_ref, dkr_ref):
        dkr = jnp.zeros((tm, LANES), F32)
        for h in range(MLA_HEADS):
            cols = slice(LANES * h, LANES * (h + 1))
            dqb_ref[:, cols] = _rope_bwd(dq_ref[:, cols] * MLA_SCALE, cb_ref[...], sb_ref[...], 8).astype(BF16)
            dkh = dk_ref[cols, :].T * LN2
            dkvb_ref[:, cols] = dkh.astype(BF16)
            dkr = dkr + dkh
        for j in range(MLA_HEADS // 2):
            dkvb_ref[:, 1024 + LANES * j:1024 + LANES * (j + 1)] = dv_ref[LANES * j:LANES * (j + 1), :].T.astype(BF16)
        dkr_ref[...] = _rope_bwd(dkr, cb_ref[...], sb_ref[...], 8)

    return pl.pallas_call(
        body, name="prep_b_bwd", grid=(t // tm,),
        in_specs=[_row_spec(tm, 1024), pl.BlockSpec((1024, tm), lambda i: (0, i)),
                  pl.BlockSpec((512, tm), lambda i: (0, i)), _row_spec(tm, LANES),
                  _row_spec(tm, LANES)],
        out_specs=[_row_spec(tm, 1024), _row_spec(tm, 1536), _row_spec(tm, LANES)],
        out_shape=[jax.ShapeDtypeStruct((t, 1024), BF16), jax.ShapeDtypeStruct((t, 1536), BF16),
                   jax.ShapeDtypeStruct((t, LANES), F32)],
        compiler_params=_params("parallel"),
    )(dq, dk, dv, cos_b, sin_b)


_NT = (((1,), (1,)), ((), ()))
_NN = (((1,), (0,)), ((), ()))
_TN = (((0,), (0,)), ((), ()))


def _head_operands(qv, kv, i, shared_k):
    if shared_k:
        lo = _lo_mask(qv.shape)
        keep = lo if i == 0 else jnp.logical_not(lo)
        return jnp.where(keep, qv, jnp.zeros_like(qv)), kv
    cols = slice(LANES * i, LANES * (i + 1))
    return qv[:, cols], kv[:, cols]


def _attn_specs(shared_k, tq, tk, q_of, k_of):
    wq = LANES if shared_k else 2 * LANES
    q_spec = pl.BlockSpec((tq, wq), lambda *g: (q_of(*g), g[0]))
    if shared_k:
        k_spec = pl.BlockSpec((tk, LANES), lambda *g: (k_of(*g), 0))
        v_spec = pl.BlockSpec((tk, LANES), lambda *g: (k_of(*g), 0))
    else:
        k_spec = pl.BlockSpec((tk, wq), lambda *g: (k_of(*g), g[0]))
        v_spec = pl.BlockSpec((tk, LANES), lambda *g: (k_of(*g), g[0]))
    return wq, q_spec, k_spec, v_spec


def _attn_fwd(q, k, v, shared_k, name):
    t = q.shape[0]
    tq, tk = _tile(t, ATTN_TQ), _tile(t, ATTN_TK)
    nq, nk = t // tq, t // tk
    wq, q_spec, k_spec, v_spec = _attn_specs(shared_k, tq, tk, lambda p, i, j: i, lambda p, i, j: j)
    groups = q.shape[1] // wq
    chunk = _tile(tq, 2 * LANES)

    def body(q_ref, k_ref, v_ref, o_ref, lse_ref, m_s, l_s, acc_s, alpha_s, s_s, p_s):
        kb = pl.program_id(2)

        @pl.when(kb == 0)
        def _():
            m_s[...] = jnp.full_like(m_s, -jnp.inf)
            l_s[...] = jnp.zeros_like(l_s)
            acc_s[...] = jnp.zeros_like(acc_s)

        qv, kv, vv = q_ref[...], k_ref[...], v_ref[...]
        for i in range(2):
            qi, ki = _head_operands(qv, kv, i, shared_k)
            s_s[i] = lax.dot_general(ki, qi, _NT, preferred_element_type=F32)
        for i in range(2):
            for c in range(tq // chunk):
                cols = slice(c * chunk, (c + 1) * chunk)
                m_prev = m_s[i, :, cols]
                m_new = jnp.maximum(m_prev, jnp.max(s_s[i, :, cols], axis=0, keepdims=True))
                alpha = jnp.exp2(m_prev - m_new)
                pt = jnp.exp2(s_s[i, :, cols] - m_new)
                l_s[i, :, cols] = alpha * l_s[i, :, cols] + jnp.sum(pt, axis=0, keepdims=True)
                m_s[i, :, cols] = m_new
                alpha_s[i, :, cols] = alpha
                p_s[i, :, cols] = pt.astype(BF16)
        for i in range(2):
            acc_s[i] = alpha_s[i] * acc_s[i] + lax.dot_general(vv, p_s[i], _TN, preferred_element_type=F32)

        @pl.when(kb == nk - 1)
        def _():
            o0 = acc_s[0] / l_s[0]
            o1 = acc_s[1] / l_s[1]
            row_lo = lax.broadcasted_iota(jnp.int32, o0.shape, 0) < HEAD_DIM
            o_ref[...] = jnp.where(row_lo, o0, o1).T.astype(BF16)
            lse_ref[0] = m_s[0] + jnp.log2(l_s[0])
            lse_ref[1] = m_s[1] + jnp.log2(l_s[1])

    return pl.pallas_call(
        body, name=name, grid=(groups, nq, nk),
        in_specs=[q_spec, k_spec, v_spec],
        out_specs=[pl.BlockSpec((tq, LANES), lambda p, i, j: (i, p)),
                   pl.BlockSpec((2, 1, tq), lambda p, i, j: (p, 0, i))],
        out_shape=[jax.ShapeDtypeStruct((t, LANES * groups), BF16),
                   jax.ShapeDtypeStruct((2 * groups, 1, t), F32)],
        scratch_shapes=[pltpu.VMEM((2, 1, tq), F32), pltpu.VMEM((2, 1, tq), F32), pltpu.VMEM((2, LANES, tq), F32),
                        pltpu.VMEM((2, 1, tq), F32), pltpu.VMEM((2, tk, tq), F32), pltpu.VMEM((2, tk, tq), BF16)],
        compiler_params=_params("parallel", "parallel", "arbitrary"),
    )(q, k, v)


def _attn_stats(do, o, lse):
    t, w = do.shape
    tm = _tile(t, 512)
    groups = w // LANES

    def body(do_ref, o_ref, lse_ref, delta_ref, lser_ref, dob_ref):
        dov = do_ref[...]
        dob_ref[...] = dov.astype(BF16)
        prod = dov * o_ref[...].astype(F32)
        for g in range(groups):
            x = prod[:, LANES * g:LANES * (g + 1)]
            lo = _lo_mask(x.shape)
            d0 = jnp.sum(jnp.where(lo, x, 0.0), axis=-1, keepdims=True)
            d1 = jnp.sum(jnp.where(lo, 0.0, x), axis=-1, keepdims=True)
            delta_ref[2 * g] = jnp.broadcast_to(d0, (tm, LANES))
            delta_ref[2 * g + 1] = jnp.broadcast_to(d1, (tm, LANES))
        for h in range(2 * groups):
            lser_ref[h] = jnp.broadcast_to(lse_ref[h], (LANES, tm)).T

    rep_spec = pl.BlockSpec((2 * groups, tm, LANES), lambda i: (0, i, 0))
    rep_shape = jax.ShapeDtypeStruct((2 * groups, t, LANES), F32)
    return pl.pallas_call(
        body, name="attn_stats", grid=(t // tm,),
        in_specs=[_row_spec(tm, w), _row_spec(tm, w), pl.BlockSpec((2 * groups, 1, tm), lambda i: (0, 0, i))],
        out_specs=[rep_spec, rep_spec, _row_spec(tm, w)],
        out_shape=[rep_shape, rep_shape, jax.ShapeDtypeStruct((t, w), BF16)],
        compiler_params=_params("parallel"),
    )(do, o, lse)


def _attn_bwd(q, k, v, do, lse, delta, shared_k, name):
    t = q.shape[0]
    tq, tk = _tile(t, ATTN_TQ), _tile(t, ATTN_TK)
    nq, nk = t // tq, t // tk
    wq, q_spec, k_spec, v_spec = _attn_specs(shared_k, tq, tk, lambda p, j, i: i, lambda p, j, i: j)
    groups = q.shape[1] // wq

    def body(q_ref, k_ref, v_ref, do_ref, lse_ref, delta_ref, dq_ref, dk_ref, dv_ref, dk_s, dv_s, s_s, dp_s, p_s,
             ds_s):
        kb, qb = pl.program_id(1), pl.program_id(2)

        @pl.when(qb == 0)
        def _():
            dk_s[...] = jnp.zeros_like(dk_s)
            dv_s[...] = jnp.zeros_like(dv_s)

        qv, kv, vv, dov = q_ref[...], k_ref[...], v_ref[...], do_ref[...]
        lo = _lo_mask(dov.shape)
        heads = []
        for i in range(2):
            qi, ki = _head_operands(qv, kv, i, shared_k)
            keep = lo if i == 0 else jnp.logical_not(lo)
            doi = jnp.where(keep, dov, jnp.zeros_like(dov))
            heads.append((qi, ki, doi))
            s_s[i] = lax.dot_general(qi, ki, _NT, preferred_element_type=F32)
            dp_s[i] = lax.dot_general(doi, vv, _NT, preferred_element_type=F32)
        for i in range(2):
            lse_i, delta_i = lse_ref[i], delta_ref[i]
            for c in range(tk // LANES):
                cols = slice(c * LANES, (c + 1) * LANES)
                p = jnp.exp2(s_s[i, :, cols] - lse_i)
                p_s[i, :, cols] = p.astype(BF16)
                ds_s[i, :, cols] = (p * (dp_s[i, :, cols] - delta_i)).astype(BF16)
        dq_parts = []
        for i in range(2):
            qi, ki, doi = heads[i]
            dv_s[...] += lax.dot_general(doi, p_s[i], _TN, preferred_element_type=F32)
            dk_i = lax.dot_general(qi, ds_s[i], _TN, preferred_element_type=F32)
            if shared_k:
                dk_s[...] += dk_i
            else:
                dk_s[LANES * i:LANES * (i + 1), :] += dk_i
            dq_parts.append(lax.dot_general(ds_s[i], ki, _NN, preferred_element_type=F32))
        rows = pl.ds(pl.multiple_of(qb * tq, tq), tq)
        if shared_k:
            tiles = [(slice(0, LANES), jnp.where(lo, dq_parts[0], dq_parts[1]))]
        else:
            tiles = [(slice(0, LANES), dq_parts[0]), (slice(LANES, 2 * LANES), dq_parts[1])]
        for cols, val in tiles:
            @pl.when(kb == 0)
            def _(cols=cols, val=val):
                dq_ref[rows, cols] = val

            @pl.when(kb > 0)
            def _(cols=cols, val=val):
                dq_ref[rows, cols] += val

        @pl.when(qb == nq - 1)
        def _():
            if shared_k:
                dk_ref[0] = dk_s[...]
                dv_ref[0] = dv_s[...]
            else:
                dk_ref[...] = dk_s[...]
                dv_ref[...] = dv_s[...]

    stat_spec = pl.BlockSpec((2, tq, LANES), lambda p, j, i: (p, i, 0))
    do_spec = pl.BlockSpec((tq, LANES), lambda p, j, i: (i, p))
    dq_spec = pl.BlockSpec((t, wq), lambda p, j, i: (0, p))
    if shared_k:
        dk_spec = pl.BlockSpec((1, LANES, tk), lambda p, j, i: (p, 0, j))
        dv_spec = dk_spec
        dk_shape = jax.ShapeDtypeStruct((groups, LANES, t), F32)
        dv_shape = dk_shape
    else:
        dk_spec = pl.BlockSpec((wq, tk), lambda p, j, i: (p, j))
        dv_spec = pl.BlockSpec((LANES, tk), lambda p, j, i: (p, j))
        dk_shape = jax.ShapeDtypeStruct((wq * groups, t), F32)
        dv_shape = jax.ShapeDtypeStruct((LANES * groups, t), F32)
    return pl.pallas_call(
        body, name=name, grid=(groups, nk, nq),
        in_specs=[q_spec, k_spec, v_spec, do_spec, stat_spec, stat_spec],
        out_specs=[dq_spec, dk_spec, dv_spec],
        out_shape=[jax.ShapeDtypeStruct((t, wq * groups), F32), dk_shape, dv_shape],
        scratch_shapes=[pltpu.VMEM((wq, tk), F32), pltpu.VMEM((LANES, tk), F32), pltpu.VMEM((2, tq, tk), F32),
                        pltpu.VMEM((2, tq, tk), F32), pltpu.VMEM((2, tq, tk), BF16), pltpu.VMEM((2, tq, tk), BF16)],
        compiler_params=_params("parallel", "arbitrary", "arbitrary"),
    )(q, k, v, do, lse, delta)


_MERGE_W = 512
_GATE_BLK0 = Z_GATE // _MERGE_W


def _merge_fwd(z, b_gate, ta, tb):
    t = z.shape[0]
    tm = _tile(t, 512)
    w = _MERGE_W
    nj = D_MODEL // w

    def body(za_ref, zb_ref, ba_ref, bb_ref, ta_ref, tb_ref, o_ref):
        ga = jax.nn.sigmoid(za_ref[...] + ba_ref[...])
        gb = jax.nn.sigmoid(zb_ref[...] + bb_ref[...])
        o_ref[...] = (ga * ta_ref[...] + gb * tb_ref[...]).astype(BF16)

    return pl.pallas_call(
        body, name="merge_fwd", grid=(t // tm, nj),
        in_specs=[pl.BlockSpec((tm, w), lambda i, j: (i, _GATE_BLK0 + j)),
                  pl.BlockSpec((tm, w), lambda i, j: (i, _GATE_BLK0 + nj + j)),
                  pl.BlockSpec((1, w), lambda i, j: (0, j)),
                  pl.BlockSpec((1, w), lambda i, j: (0, nj + j)),
                  pl.BlockSpec((tm, w), lambda i, j: (i, j)),
                  pl.BlockSpec((tm, w), lambda i, j: (i, j))],
        out_specs=pl.BlockSpec((tm, w), lambda i, j: (i, j)),
        out_shape=jax.ShapeDtypeStruct((t, D_MODEL), BF16),
        compiler_params=_params("parallel", "parallel"),
    )(z, z, b_gate, b_gate, ta, tb)


def _merge_bwd(dmg, z, b_gate, ta, tb):
    t = z.shape[0]
    tm = _tile(t, 512)
    w = _MERGE_W
    nj = D_MODEL // w

    def body(dm_ref, za_ref, zb_ref, ba_ref, bb_ref, ta_ref, tb_ref, dta_ref, dtb_ref, dza_ref, dzb_ref,
             dba_ref, dbb_ref):
        dm = dm_ref[...]
        ga = jax.nn.sigmoid(za_ref[...] + ba_ref[...])
        gb = jax.nn.sigmoid(zb_ref[...] + bb_ref[...])
        dta_ref[...] = (dm * ga).astype(BF16)
        dtb_ref[...] = (dm * gb).astype(BF16)
        dza = dm * ta_ref[...] * ga * (1.0 - ga)
        dzb = dm * tb_ref[...] * gb * (1.0 - gb)
        dza_ref[...] = dza.astype(BF16)
        dzb_ref[...] = dzb.astype(BF16)

        @pl.when(pl.program_id(1) == 0)
        def _():
            dba_ref[...] = jnp.zeros_like(dba_ref)
            dbb_ref[...] = jnp.zeros_like(dbb_ref)

        dba_ref[...] += _fold8(dza)
        dbb_ref[...] += _fold8(dzb)

    blk = pl.BlockSpec((tm, w), lambda j, i: (i, j))
    acc = pl.BlockSpec((SUBLANES, w), lambda j, i: (0, j))
    return pl.pallas_call(
        body, name="merge_bwd", grid=(nj, t // tm),
        in_specs=[blk,
                  pl.BlockSpec((tm, w), lambda j, i: (i, _GATE_BLK0 + j)),
                  pl.BlockSpec((tm, w), lambda j, i: (i, _GATE_BLK0 + nj + j)),
                  pl.BlockSpec((1, w), lambda j, i: (0, j)),
                  pl.BlockSpec((1, w), lambda j, i: (0, nj + j)),
                  blk, blk],
        out_specs=[blk, blk, blk, blk, acc, acc],
        out_shape=[jax.ShapeDtypeStruct((t, D_MODEL), BF16)] * 4 + [jax.ShapeDtypeStruct((SUBLANES, D_MODEL), F32)] * 2,
        compiler_params=_params("parallel", "arbitrary"),
    )(dmg, z, z, b_gate, b_gate, ta, tb)


def _relu2_fwd(h):
    t, f = h.shape
    tm, tn = _tile(t, 512), _tile(f, 1024)

    def body(h_ref, a_ref):
        r = jnp.maximum(h_ref[...], 0.0)
        a_ref[...] = (r * r).astype(BF16)

    spec = pl.BlockSpec((tm, tn), lambda i, j: (i, j))
    return pl.pallas_call(
        body, name="relu2_fwd", grid=(t // tm, f // tn), in_specs=[spec], out_specs=spec,
        out_shape=jax.ShapeDtypeStruct((t, f), BF16), compiler_params=_params("parallel", "parallel"),
    )(h)


def _relu2_bwd(da, h):
    t, f = h.shape
    tm, tn = _tile(t, 512), _tile(f, 1024)

    def body(da_ref, h_ref, dh_ref):
        dh_ref[...] = (da_ref[...] * (2.0 * jnp.maximum(h_ref[...], 0.0))).astype(BF16)

    spec = pl.BlockSpec((tm, tn), lambda i, j: (i, j))
    return pl.pallas_call(
        body, name="relu2_bwd", grid=(t // tm, f // tn), in_specs=[spec, spec], out_specs=spec,
        out_shape=jax.ShapeDtypeStruct((t, f), BF16), compiler_params=_params("parallel", "parallel"),
    )(da, h)


def _loss_grad(y, target):
    t, d = y.shape
    tm = _tile(t, 512)

    def body(y_ref, t_ref, dy_ref, acc_ref):
        err = y_ref[...] - t_ref[...]
        dy_ref[...] = err * (1.0 / d)
        e8 = _fold8(err * err)
        part = e8[:, 0:LANES]
        for c in range(1, d // LANES):
            part = part + e8[:, LANES * c:LANES * (c + 1)]

        @pl.when(pl.program_id(0) == 0)
        def _():
            acc_ref[...] = jnp.zeros_like(acc_ref)

        acc_ref[...] += part

    return pl.pallas_call(
        body, name="loss_grad", grid=(t // tm,),
        in_specs=[_row_spec(tm, d), _row_spec(tm, d)],
        out_specs=[_row_spec(tm, d), _acc_spec(LANES)],
        out_shape=[jax.ShapeDtypeStruct((t, d), F32), jax.ShapeDtypeStruct((SUBLANES, LANES), F32)],
        compiler_params=_params("arbitrary"),
    )(y, target)


_MESH_ID = pl.DeviceIdType.MESH
_ANY = pl.BlockSpec(memory_space=pl.ANY)


def _all_gather(arrays):
    n = len(arrays)

    def body(*refs):
        x_refs, out_refs = refs[:n], refs[n:2 * n]
        send_sems, recv_sems, local_sems = refs[2 * n:]
        mx, my, mc = lax.axis_index("x"), lax.axis_index("y"), lax.axis_index("c")
        me, sibling = (mx, my, mc), (mx, my, 1 - mc)
        chips = [(1 - mx, my), (mx, 1 - my), (1 - mx, 1 - my)]

        def slot(a, px, py, pc):
            return out_refs[a].at[4 * px + 2 * py + pc]

        def copy(a, sem, block, to, src=None):
            return pltpu.make_async_remote_copy(
                src_ref=slot(a, *block) if src is None else src, dst_ref=slot(a, *block),
                send_sem=send_sems.at[a, sem], recv_sem=recv_sems.at[a, sem], device_id=to, device_id_type=_MESH_ID)

        mine = [pltpu.make_async_copy(x_refs[a], slot(a, *me), local_sems.at[a]) for a in range(n)]
        first = []
        for a in range(n):
            mine[a].start()
            first.append(copy(a, 0, me, sibling, src=x_refs[a]))
            first += [copy(a, 1 + j, me, (*chip, mc), src=x_refs[a]) for j, chip in enumerate(chips)]
        for cp in first:
            cp.start()
        passed = []
        for a in range(n):
            for j, chip in enumerate(chips):
                copy(a, 1 + j, (*chip, mc), me).wait_recv()
                passed.append(copy(a, 4 + j, (*chip, mc), sibling))
                passed[-1].start()
        for a in range(n):
            copy(a, 0, sibling, me).wait_recv()
            for j, chip in enumerate(chips):
                copy(a, 4 + j, (*chip, 1 - mc), me).wait_recv()
        for cp in first + passed:
            cp.wait_send()
        for cp in mine:
            cp.wait()

    return pl.pallas_call(
        body, name="weight_all_gather",
        out_shape=[jax.ShapeDtypeStruct((N_DEV,) + a.shape, a.dtype) for a in arrays],
        in_specs=[_ANY] * n, out_specs=[_ANY] * n,
        scratch_shapes=[pltpu.SemaphoreType.DMA((n, 7)), pltpu.SemaphoreType.DMA((n, 7)),
                        pltpu.SemaphoreType.DMA((n,))],
    )(*arrays)


def _pair_exchange(sends):
    n = len(sends)

    def body(*refs):
        s_refs, r_refs = refs[:n], refs[n:2 * n]
        send_sems, recv_sems = refs[2 * n:]
        mx, my, mc = lax.axis_index("x"), lax.axis_index("y"), lax.axis_index("c")
        copies = []
        for a in range(n):
            for ch in range(4):
                cp = pltpu.make_async_remote_copy(
                    src_ref=s_refs[a].at[2 * ch + (1 - mc)], dst_ref=r_refs[a].at[ch], send_sem=send_sems.at[a, ch],
                    recv_sem=recv_sems.at[a, ch], device_id=(mx, my, 1 - mc), device_id_type=_MESH_ID)
                cp.start()
                copies.append(cp)
        for cp in copies:
            cp.wait_send()
            cp.wait_recv()

    return pl.pallas_call(
        body, name="grad_pair_exchange",
        out_shape=[jax.ShapeDtypeStruct((4,) + s.shape[1:], s.dtype) for s in sends],
        in_specs=[_ANY] * n, out_specs=[_ANY] * n,
        scratch_shapes=[pltpu.SemaphoreType.DMA((n, 4)), pltpu.SemaphoreType.DMA((n, 4))],
    )(*sends)


def _pair_add(send, half, core):
    _, r, c_ = send.shape
    tr = _row_tile(r, c_)

    def body(core_ref, s_ref, h_ref, o_ref):
        del core_ref
        o_ref[...] = (s_ref[...] + h_ref[...]).astype(BF16)

    blk = pl.BlockSpec((1, tr, c_), lambda ch, i, core_ref: (ch, i, 0))
    return pl.pallas_call(
        body, name="grad_pair_add",
        grid_spec=pltpu.PrefetchScalarGridSpec(
            num_scalar_prefetch=1, grid=(4, r // tr),
            in_specs=[pl.BlockSpec((1, tr, c_), lambda ch, i, core_ref: (2 * ch + core_ref[0], i, 0)), blk],
            out_specs=blk),
        out_shape=jax.ShapeDtypeStruct((4, r, c_), BF16),
        compiler_params=_params("parallel", "parallel"),
    )(core, send, half)


def _chip_exchange(parts):
    n = len(parts)

    def body(*refs):
        p_refs, r_refs = refs[:n], refs[n:2 * n]
        send_sems, recv_sems, local_sems = refs[2 * n:]
        mx, my, mc = lax.axis_index("x"), lax.axis_index("y"), lax.axis_index("c")
        mine = 2 * mx + my
        local = [pltpu.make_async_copy(p_refs[a].at[mine], r_refs[a].at[mine], local_sems.at[a]) for a in range(n)]
        copies = []
        for a in range(n):
            local[a].start()
            for rel in range(1, 4):
                px = 1 - mx if rel & 2 else mx
                py = 1 - my if rel & 1 else my
                cp = pltpu.make_async_remote_copy(
                    src_ref=p_refs[a].at[2 * px + py], dst_ref=r_refs[a].at[mine], send_sem=send_sems.at[a, rel - 1],
                    recv_sem=recv_sems.at[a, rel - 1], device_id=(px, py, mc), device_id_type=_MESH_ID)
                cp.start()
                copies.append(cp)
        for cp in copies:
            cp.wait_send()
            cp.wait_recv()
        for cp in local:
            cp.wait()

    return pl.pallas_call(
        body, name="grad_chip_exchange",
        out_shape=[jax.ShapeDtypeStruct(p.shape, p.dtype) for p in parts],
        in_specs=[_ANY] * n, out_specs=[_ANY] * n,
        scratch_shapes=[pltpu.SemaphoreType.DMA((n, 3)), pltpu.SemaphoreType.DMA((n, 3)),
                        pltpu.SemaphoreType.DMA((n,))],
    )(*parts)


def _row_tile(r, c_):
    tr = min(r, ADAM_BLOCK_ELEMS // (pl.cdiv(c_, LANES) * LANES))
    while r % tr:
        tr -= SUBLANES
    return tr


def _adamw(recv, w, m, v):
    r, c_ = w.shape
    tr = _row_tile(r, c_)
    n_src = recv.shape[0]

    def body(g_ref, w_ref, m_ref, v_ref, go_ref, d_ref, mo_ref, vo_ref):
        g = g_ref[0].astype(F32)
        for s in range(1, n_src):
            g = g + g_ref[s].astype(F32)
        go_ref[...] = g
        mn = ADAM_B1 * m_ref[...] + (1.0 - ADAM_B1) * g
        vn = ADAM_B2 * v_ref[...] + (1.0 - ADAM_B2) * (g * g)
        mo_ref[...] = mn
        vo_ref[...] = vn
        m_hat = mn / (1.0 - ADAM_B1 ** ADAM_STEP)
        v_hat = vn / (1.0 - ADAM_B2 ** ADAM_STEP)
        d_ref[...] = -ADAM_LR * (m_hat / (jnp.sqrt(v_hat) + ADAM_EPS) + ADAM_WD * w_ref[...])

    spec = pl.BlockSpec((tr, c_), lambda i: (i, 0))
    out = jax.ShapeDtypeStruct((r, c_), F32)
    return pl.pallas_call(
        body, name="grad_sum_adamw", grid=(r // tr,),
        in_specs=[pl.BlockSpec((n_src, tr, c_), lambda i: (0, i, 0)), spec, spec, spec],
        out_specs=[spec, spec, spec, spec], out_shape=[out, out, out, out],
        compiler_params=_params("parallel"),
    )(recv, w, m, v)


def _pad_cols(a, before, after):
    parts = []
    if before:
        parts.append(jnp.zeros(a.shape[:-1] + (before,), a.dtype))
    parts.append(a)
    if after:
        parts.append(jnp.zeros(a.shape[:-1] + (after,), a.dtype))
    return jnp.concatenate(parts, axis=-1)


def _q_head_pairs(a, axis):
    shp = a.shape
    a = a.reshape(shp[:axis] + (GQA_KV_HEADS, GQA_GROUP, HEAD_DIM) + shp[axis + 1:])
    a = jnp.swapaxes(a, axis, axis + 1)
    return a.reshape(shp)


def _q_head_unpairs(a, axis):
    shp = a.shape
    a = a.reshape(shp[:axis] + (GQA_GROUP, GQA_KV_HEADS, HEAD_DIM) + shp[axis + 1:])
    a = jnp.swapaxes(a, axis, axis + 1)
    return a.reshape(shp)


def _layout_weights(w):
    w_in = w["w_in"]
    lead = w_in.shape[:-1]
    w_in_p = jnp.concatenate([
        _q_head_pairs(w_in[..., 0:512], w_in.ndim - 1),
        w_in[..., 512:1408],
        _pad_cols(w_in[..., 1408:1440], KR_LANE0, LANES - KR_LANE0 - MLA_ROPE_DIM),
        w_in[..., 1440:],
    ], axis=-1)
    wq = w["w_q_up"]
    wq_p = _pad_cols(wq.reshape(wq.shape[:-1] + (MLA_HEADS, MLA_QK_DIM)), 0, LANES - MLA_QK_DIM)
    wq_p = wq_p.reshape(wq.shape[:-1] + (MLA_HEADS * LANES,))
    wkv = w["w_kv_up"]
    wkv4 = wkv.reshape(wkv.shape[:-1] + (MLA_HEADS, 2 * HEAD_DIM))
    wk_p = _pad_cols(wkv4[..., :HEAD_DIM], 0, LANES - HEAD_DIM).reshape(wkv.shape[:-1] + (MLA_HEADS * LANES,))
    wv_p = wkv4[..., HEAD_DIM:].reshape(wkv.shape[:-1] + (MLA_HEADS * HEAD_DIM,))
    del lead
    return {
        "w_in": w_in_p, "w_q_up": wq_p, "w_kv_up": jnp.concatenate([wk_p, wv_p], axis=-1),
        "w_branch_a": _q_head_pairs(w["w_branch_a"], w["w_branch_a"].ndim - 2), "w_branch_b": w["w_branch_b"],
        "w_o": w["w_o"], "w_ffn_up": w["w_ffn_up"], "w_ffn_down": w["w_ffn_down"],
    }


def _unlayout_grads(g):
    gi = g["w_in"]
    kr0 = Z_KR + KR_LANE0
    g_in = jnp.concatenate([
        _q_head_unpairs(gi[..., 0:512], gi.ndim - 1), gi[..., 512:1408], gi[..., kr0:kr0 + MLA_ROPE_DIM],
        gi[..., Z_GATE:],
    ], axis=-1)
    gq = g["w_q_up"]
    gq = gq.reshape(gq.shape[:-1] + (MLA_HEADS, LANES))[..., :MLA_QK_DIM]
    gq = gq.reshape(gq.shape[:-2] + (MLA_HEADS * MLA_QK_DIM,))
    gkv = g["w_kv_up"]
    gk = gkv[..., :MLA_HEADS * LANES].reshape(gkv.shape[:-1] + (MLA_HEADS, LANES))[..., :HEAD_DIM]
    gv = gkv[..., MLA_HEADS * LANES:].reshape(gkv.shape[:-1] + (MLA_HEADS, HEAD_DIM))
    gkv = jnp.concatenate([gk, gv], axis=-1).reshape(gkv.shape[:-1] + (MLA_HEADS * 2 * HEAD_DIM,))
    return {
        "w_in": g_in, "w_q_up": gq, "w_kv_up": gkv,
        "w_branch_a": _q_head_unpairs(g["w_branch_a"], g["w_branch_a"].ndim - 2), "w_branch_b": g["w_branch_b"],
        "w_o": g["w_o"], "w_ffn_up": g["w_ffn_up"], "w_ffn_down": g["w_ffn_down"],
    }


def _pack_small(parts):
    flat = jnp.concatenate([p.reshape(-1) for p in parts])
    pad = (-flat.shape[0]) % (SUBLANES * LANES)
    if pad:
        flat = jnp.concatenate([flat, jnp.zeros((pad,), flat.dtype)])
    return flat.reshape(-1, LANES)


def _unpack_small(packed, shapes):
    flat = packed.reshape(-1)
    out, off = [], 0
    for shp in shapes:
        n = int(np.prod(shp))
        out.append(flat[off:off + n].reshape(shp))
        off += n
    return out


def _shards_of(full, axis):
    shp = full.shape
    cut = shp[:axis] + (N_DEV, shp[axis] // N_DEV) + shp[axis + 1:]
    return jnp.moveaxis(full.reshape(cut), axis, 0)


def _from_shards(shards, axis):
    full = list(shards.shape[1:])
    full[axis] *= N_DEV
    return jnp.moveaxis(shards, 0, axis).reshape(full)


def _rows2d(a):
    return a.reshape(-1, a.shape[-1])


def _layer_fwd(x, u, lw, tabs):
    cos_a, sin_a, cos_b, sin_b = tabs
    z = _matmul(u, lw["w_in"], "nn", "mm_in")
    qa, ka, va, cqn, ckvn, krr = _prep_a_fwd(z, lw["gq2"], lw["gk2"], lw["gqa"], lw["gkva"], cos_a, sin_a, cos_b, sin_b)
    qb = _matmul(cqn, lw["w_q_up"], "nn", "mm_q_up")
    kvb = _matmul(ckvn, lw["w_kv_up"], "nn", "mm_kv_up")
    q_b, k_b, v_b = _prep_b_fwd(qb, kvb, krr, cos_b, sin_b)
    ya, lse_a = _attn_fwd(qa, ka, va, True, "gqa_fwd")
    yb, lse_b = _attn_fwd(q_b, k_b, v_b, False, "mla_fwd")
    ta = _matmul(ya, lw["w_branch_a"], "nn", "mm_branch_a")
    tb = _matmul(yb, lw["w_branch_b"], "nn", "mm_branch_b")
    merged = _merge_fwd(z, lw["b_gate"], ta, tb)
    m = _matmul(merged, lw["w_o"], "nn", "mm_o")
    x2, u2 = _res_norm_fwd(x, m, lw["post_mix_g"], lw["pre_ffn_g"])
    h = _matmul(u2, lw["w_ffn_up"], "nn", "mm_ffn_up")
    a = _relu2_fwd(h)
    f = _matmul(a, lw["w_ffn_down"], "nn", "mm_ffn_down")
    x3, u_next = _res_norm_fwd(x2, f, lw["post_ffn_g"], lw["next_pre_mix_g"])
    saved = dict(u=u, z=z, qa=qa, ka=ka, va=va, cqn=cqn, ckvn=ckvn, q_b=q_b, k_b=k_b, v_b=v_b, ya=ya, yb=yb,
                 lse_a=lse_a, lse_b=lse_b, ta=ta, tb=tb, merged=merged, m=m, x2=x2, u2=u2, h=h, a=a, f=f, x3=x3)
    return x3, u_next, saved


def _layer_bwd(dx3, du_next, lw, sv, tabs):
    cos_a, sin_a, cos_b, sin_b = tabs
    g = {}
    dx3, df, dg4, dg1n = _res_norm_bwd(sv["x3"], sv["f"], lw["post_ffn_g"], lw["next_pre_mix_g"], dx3, du_next)
    g["post_ffn_g"], g["next_pre_mix_g"] = dg4, dg1n
    da = _matmul(df, lw["w_ffn_down"], "nt", "mm_d_a")
    g["w_ffn_down"] = _matmul(sv["a"], df, "tn", "mm_dw_ffn_down")
    dh = _relu2_bwd(da, sv["h"])
    du2 = _matmul(dh, lw["w_ffn_up"], "nt", "mm_d_u2")
    g["w_ffn_up"] = _matmul(sv["u2"], dh, "tn", "mm_dw_ffn_up")
    dx2, dm, dg2, dg3 = _res_norm_bwd(sv["x2"], sv["m"], lw["post_mix_g"], lw["pre_ffn_g"], dx3, du2)
    g["post_mix_g"], g["pre_ffn_g"] = dg2, dg3
    dmg = _matmul(dm, lw["w_o"], "nt", "mm_d_merged")
    g["w_o"] = _matmul(sv["merged"], dm, "tn", "mm_dw_o")
    dta, dtb, dzg_a, dzg_b, db_a, db_b = _merge_bwd(dmg, sv["z"], lw["b_gate"], sv["ta"], sv["tb"])
    g["b_gate"] = jnp.concatenate([db_a, db_b], axis=-1)
    dya = _matmul(dta, lw["w_branch_a"], "nt", "mm_d_ya")
    g["w_branch_a"] = _matmul(sv["ya"], dta, "tn", "mm_dw_branch_a")
    dyb = _matmul(dtb, lw["w_branch_b"], "nt", "mm_d_yb")
    g["w_branch_b"] = _matmul(sv["yb"], dtb, "tn", "mm_dw_branch_b")
    delta_a, lse_a, dya16 = _attn_stats(dya, sv["ya"], sv["lse_a"])
    delta_b, lse_b, dyb16 = _attn_stats(dyb, sv["yb"], sv["lse_b"])
    dqa, dka4, dva4 = _attn_bwd(sv["qa"], sv["ka"], sv["va"], dya16, lse_a, delta_a, True, "gqa_bwd")
    dq_b, dk_b, dv_b = _attn_bwd(sv["q_b"], sv["k_b"], sv["v_b"], dyb16, lse_b, delta_b, False, "mla_bwd")
    dqb, dkvb, dkr = _prep_b_bwd(dq_b, dk_b, dv_b, cos_b, sin_b)
    dcqn = _matmul(dqb, lw["w_q_up"], "nt", "mm_d_cqn")
    g["w_q_up"] = _matmul(sv["cqn"], dqb, "tn", "mm_dw_q_up")
    dckvn = _matmul(dkvb, lw["w_kv_up"], "nt", "mm_d_ckvn")
    g["w_kv_up"] = _matmul(sv["ckvn"], dkvb, "tn", "mm_dw_kv_up")
    dz, dgq, dgk, dgqa, dgkva = _prep_a_bwd(sv["z"], dqa, dka4, dva4, dcqn, dckvn, dkr, dzg_a, dzg_b, lw["gq2"],
                                            lw["gk2"], lw["gqa"], lw["gkva"], cos_a, sin_a)
    g["q_norm_g"], g["k_norm_g"], g["q_a_norm_g"], g["kv_a_norm_g"] = dgq, dgk, dgqa, dgkva
    du = _matmul(dz, lw["w_in"], "nt", "mm_d_u")
    g["w_in"] = _matmul(sv["u"], dz, "tn", "mm_dw_in")
    return dx2, du, g


def kernel(x, w_in, b_gate, q_norm_g, k_norm_g, q_a_norm_g, kv_a_norm_g, w_q_up, w_kv_up, w_branch_a, w_branch_b, w_o, w_ffn_up, w_ffn_down, pre_mix_g, post_mix_g, pre_ffn_g, post_ffn_g, loss_target, m_w_in, m_b_gate, m_q_norm_g, m_k_norm_g, m_q_a_norm_g, m_kv_a_norm_g, m_w_q_up, m_w_kv_up, m_w_branch_a, m_w_branch_b, m_w_o, m_w_ffn_up, m_w_ffn_down, m_pre_mix_g, m_post_mix_g, m_pre_ffn_g, m_post_ffn_g, v_w_in, v_b_gate, v_q_norm_g, v_k_norm_g, v_q_a_norm_g, v_kv_a_norm_g, v_w_q_up, v_w_kv_up, v_w_branch_a, v_w_branch_b, v_w_o, v_w_ffn_up, v_w_ffn_down, v_pre_mix_g, v_post_mix_g, v_pre_ffn_g, v_post_ffn_g):
    weights = dict(zip(WEIGHT_NAMES, (w_in, b_gate, q_norm_g, k_norm_g, q_a_norm_g, kv_a_norm_g, w_q_up, w_kv_up,
                                      w_branch_a, w_branch_b, w_o, w_ffn_up, w_ffn_down, pre_mix_g, post_mix_g,
                                      pre_ffn_g, post_ffn_g)))
    mom_m = dict(zip(WEIGHT_NAMES, (m_w_in, m_b_gate, m_q_norm_g, m_k_norm_g, m_q_a_norm_g, m_kv_a_norm_g, m_w_q_up,
                                    m_w_kv_up, m_w_branch_a, m_w_branch_b, m_w_o, m_w_ffn_up, m_w_ffn_down,
                                    m_pre_mix_g, m_post_mix_g, m_pre_ffn_g, m_post_ffn_g)))
    mom_v = dict(zip(WEIGHT_NAMES, (v_w_in, v_b_gate, v_q_norm_g, v_k_norm_g, v_q_a_norm_g, v_kv_a_norm_g, v_w_q_up,
                                    v_w_kv_up, v_w_branch_a, v_w_branch_b, v_w_o, v_w_ffn_up, v_w_ffn_down,
                                    v_pre_mix_g, v_post_mix_g, v_pre_ffn_g, v_post_ffn_g)))
    assert x.shape[0] == 1 and x.shape[2] == D_MODEL, x.shape
    n_layers = w_in.shape[0]
    t = x.shape[1]
    x0 = x.reshape(t, D_MODEL)
    target = loss_target.reshape(t, D_MODEL)
    shard_shapes = {n: weights[n].shape for n in BIG_NAMES}
    small_shapes = [weights[n].shape for n in SMALL_NAMES]

    gathered = _all_gather([weights[n].astype(BF16) for n in BIG_NAMES])
    full = {n: _from_shards(g, SHARD_AXIS[n]) for n, g in zip(BIG_NAMES, gathered)}
    lw_all = _layout_weights(full)
    lw_all["b_gate"] = b_gate.reshape(n_layers, 1, 2 * D_MODEL)
    lw_all["gq2"] = jnp.tile(q_norm_g, (1, 2)).reshape(n_layers, 1, LANES)
    lw_all["gk2"] = jnp.tile(k_norm_g, (1, 2)).reshape(n_layers, 1, LANES)
    lw_all["gqa"] = q_a_norm_g.reshape(n_layers, 1, MLA_Q_RANK)
    lw_all["gkva"] = kv_a_norm_g.reshape(n_layers, 1, MLA_KV_RANK)
    for n in ("post_mix_g", "pre_ffn_g", "post_ffn_g"):
        lw_all[n] = weights[n]
    lw_all["next_pre_mix_g"] = jnp.roll(pre_mix_g, -1, axis=0)

    tabs = _rope_tables(t)
    u0 = _rms_fwd(x0, pre_mix_g[0])

    layer_w = [{n: a[l] for n, a in lw_all.items()} for l in range(n_layers)]
    xc, uc, saved = x0, u0, []
    for l in range(n_layers):
        xc, uc, sv = _layer_fwd(xc, uc, layer_w[l], tabs)
        saved.append(sv)
    dy, loss_acc = _loss_grad(xc, target)
    loss = lax.psum(0.5 * jnp.sum(loss_acc) / D_MODEL, ("x", "y", "c"))

    dx0, du0, layer_g = dy, jnp.zeros((t, D_MODEL), F32), [None] * n_layers
    for l in reversed(range(n_layers)):
        dx0, du0, layer_g[l] = _layer_bwd(dx0, du0, layer_w[l], saved[l], tabs)
    grads = {n: jnp.stack([g[n] for g in layer_g]) for n in layer_g[0]}
    grad_x, dg1_first = _rms_bwd(x0, pre_mix_g[0], dx0, du0)

    big_grads = _unlayout_grads({n: grads[n] for n in BIG_NAMES})
    fold = lambda a: a.sum(axis=1)
    dgq = fold(grads["q_norm_g"]).reshape(n_layers, 2, HEAD_DIM).sum(axis=1)
    dgk = fold(grads["k_norm_g"]).reshape(n_layers, 2, HEAD_DIM).sum(axis=1)
    dg1 = jnp.concatenate([fold(dg1_first[None]), fold(grads["next_pre_mix_g"])[:-1]], axis=0)
    small_grads = {
        "b_gate": fold(grads["b_gate"]), "q_norm_g": dgq, "k_norm_g": dgk, "q_a_norm_g": fold(grads["q_a_norm_g"]),
        "kv_a_norm_g": fold(grads["kv_a_norm_g"]), "pre_mix_g": dg1, "post_mix_g": fold(grads["post_mix_g"]),
        "pre_ffn_g": fold(grads["pre_ffn_g"]), "post_ffn_g": fold(grads["post_ffn_g"]),
    }
    small_packed = _pack_small([small_grads[n] for n in SMALL_NAMES])
    sends = [_shards_of(big_grads[n], SHARD_AXIS[n]).reshape((N_DEV,) + _rows2d(weights[n]).shape)
             for n in BIG_NAMES]
    sends.append(jnp.broadcast_to(small_packed[None], (N_DEV,) + small_packed.shape))
    halves = _pair_exchange(sends)
    core = lax.axis_index("c").astype(jnp.int32).reshape(1)
    recvs = _chip_exchange([_pair_add(s, h, core) for s, h in zip(sends, halves)])

    results = {}
    for n, recv in zip(BIG_NAMES, recvs):
        res = _adamw(recv, _rows2d(weights[n]), _rows2d(mom_m[n]), _rows2d(mom_v[n]))
        results[n] = [r.reshape(shard_shapes[n]) for r in res]
    res = _adamw(recvs[-1], *[_pack_small([d[n] for n in SMALL_NAMES]) for d in (weights, mom_m, mom_v)])
    for kind, packed_out in enumerate(res):
        for n, val in zip(SMALL_NAMES, _unpack_small(packed_out, small_shapes)):
            results.setdefault(n, [None] * 4)[kind] = val
    outs = [results[n][kind] for kind in range(4) for n in WEIGHT_NAMES]
    return (loss, grad_x.reshape(x.shape), *outs)
```

```python
import functools
import math

import jax
import jax.numpy as jnp
import numpy as np
from jax import lax
from jax.experimental import pallas as pl
from jax.experimental.pallas import tpu as pltpu

F32 = jnp.float32
BF16 = jnp.bfloat16

D_MODEL = 1024
GRID_W = 64
ROPE_THETA = 10000.0
EPS = 1e-6
GQA_HEADS = 8
GQA_KV_HEADS = 2
GQA_GROUP = GQA_HEADS // GQA_KV_HEADS
HEAD_DIM = 64
MLA_HEADS = 8
MLA_ROPE_DIM = 32
MLA_QK_DIM = 96
MLA_Q_RANK = 384
MLA_KV_RANK = 256
D_FF = 4 * D_MODEL
GQA_SCALE = 1.0 / math.sqrt(HEAD_DIM)
MLA_SCALE = 1.0 / math.sqrt(MLA_QK_DIM)
LOG2E = math.log2(math.e)
LN2 = math.log(2.0)

ADAM_LR = 0.001
ADAM_B1 = 0.9
ADAM_B2 = 0.999
ADAM_EPS = 1e-08
ADAM_WD = 0.01
ADAM_STEP = 10

N_DEV = 8
LANES = 128
SUBLANES = 8
VMEM_LIMIT = 48 * 1024 * 1024

Z_QA, Z_KA, Z_VA, Z_CQ, Z_CKV, Z_KR, Z_GATE = 0, 512, 640, 768, 1152, 1408, 1536
Z_ATT_W = 1536
Z_W = 3584
KR_LANE0 = 64

WEIGHT_NAMES = ("w_in", "b_gate", "q_norm_g", "k_norm_g", "q_a_norm_g", "kv_a_norm_g", "w_q_up", "w_kv_up",
                "w_branch_a", "w_branch_b", "w_o", "w_ffn_up", "w_ffn_down", "pre_mix_g", "post_mix_g",
                "pre_ffn_g", "post_ffn_g")
SHARD_AXIS = {"w_in": 2, "w_q_up": 2, "w_kv_up": 2, "w_branch_a": 2, "w_branch_b": 2, "w_o": 1, "w_ffn_up": 2,
              "w_ffn_down": 1}
BIG_NAMES = tuple(n for n in WEIGHT_NAMES if n in SHARD_AXIS)
SMALL_NAMES = tuple(n for n in WEIGHT_NAMES if n not in SHARD_AXIS)
ADAM_BLOCK_ELEMS = 256 * 1024
MM_TILE = 1024
MM_TILE_K = 2048
ATTN_TQ = 1024
ATTN_TK = 1024


def _params(*semantics):
    return pltpu.CompilerParams(dimension_semantics=semantics, vmem_limit_bytes=VMEM_LIMIT)


def _tile(n, pref):
    if n <= pref:
        return n
    t = (pref // LANES) * LANES
    while n % t:
        t -= LANES
    return t


def _fold8(t):
    return t.reshape(t.shape[0] // SUBLANES, SUBLANES, t.shape[1]).sum(axis=0)


_DIMS = {"nn": ((1,), (0,)), "nt": ((1,), (1,)), "tn": ((0,), (0,))}


def _matmul(a, b, mode, name, post=None, h=None):
    if mode == "nn":
        (m, k), n = a.shape, b.shape[1]
    elif mode == "nt":
        (m, k), n = a.shape, b.shape[0]
    else:
        (k, m), n = a.shape, b.shape[1]
    tm, tn, tk = _tile(m, MM_TILE), _tile(n, MM_TILE), _tile(k, MM_TILE_K)
    nk = k // tk
    dims = (_DIMS[mode], ((), ()))
    n_in = 3 if post == "relu2_bwd" else 2
    n_out = 2 if post == "relu2" else 1

    def body(*refs):
        a_ref, b_ref = refs[:2]
        o_refs, acc_ref = refs[n_in:n_in + n_out], refs[-1]

        def finish(val):
            if post == "relu2":
                o_refs[0][...] = val
                r = jnp.maximum(val, 0.0)
                o_refs[1][...] = (r * r).astype(BF16)
            elif post == "relu2_bwd":
                o_refs[0][...] = (val * (2.0 * jnp.maximum(refs[2][...], 0.0))).astype(BF16)
            else:
                o_refs[0][...] = val

        prod = lax.dot_general(a_ref[...], b_ref[...], dims, preferred_element_type=F32)
        if nk == 1:
            finish(prod)
        else:
            kk = pl.program_id(2)

            @pl.when(kk == 0)
            def _():
                acc_ref[...] = prod

            @pl.when(kk > 0)
            def _():
                acc_ref[...] += prod

            @pl.when(kk == nk - 1)
            def _():
                finish(acc_ref[...])

    if mode == "tn":
        a_spec = pl.BlockSpec((tk, tm), lambda i, j, kk: (kk, i))
    else:
        a_spec = pl.BlockSpec((tm, tk), lambda i, j, kk: (i, kk))
    if mode == "nt":
        b_spec = pl.BlockSpec((tn, tk), lambda i, j, kk: (j, kk))
    else:
        b_spec = pl.BlockSpec((tk, tn), lambda i, j, kk: (kk, j))
    o_spec = pl.BlockSpec((tm, tn), lambda i, j, kk: (i, j))
    f32_out, bf16_out = jax.ShapeDtypeStruct((m, n), F32), jax.ShapeDtypeStruct((m, n), BF16)
    out_shape = {None: f32_out, "relu2": [f32_out, bf16_out], "relu2_bwd": bf16_out}[post]
    return pl.pallas_call(
        body,
        name=name,
        grid=(m // tm, n // tn, nk),
        in_specs=[a_spec, b_spec] + ([o_spec] if post == "relu2_bwd" else []),
        out_specs=[o_spec, o_spec] if post == "relu2" else o_spec,
        out_shape=out_shape,
        scratch_shapes=[pltpu.VMEM((tm, tn), F32)],
        compiler_params=_params("parallel", "parallel", "arbitrary"),
    )(*((a, b, h) if post == "relu2_bwd" else (a, b)))


def _rinv(x):
    return lax.rsqrt(jnp.mean(x * x, axis=-1, keepdims=True) + EPS)


def _rms_bwd_rows(x, g, dy):
    r = _rinv(x)
    xh = x * r
    dxh = dy * g
    dx = r * (dxh - xh * jnp.mean(dxh * xh, axis=-1, keepdims=True))
    return dx, dy * xh


def _row_spec(tm, c):
    return pl.BlockSpec((tm, c), lambda i: (i, 0))


def _vec_spec(c):
    return pl.BlockSpec((1, c), lambda i: (0, 0))


def _acc_spec(c):
    return pl.BlockSpec((SUBLANES, c), lambda i: (0, 0))


def _rms_fwd(x, g):
    t, d = x.shape
    tm = _tile(t, 512)

    def body(x_ref, g_ref, o_ref):
        xv = x_ref[...]
        o_ref[...] = (xv * _rinv(xv) * g_ref[...]).astype(BF16)

    return pl.pallas_call(
        body, name="rms_fwd", grid=(t // tm,),
        in_specs=[_row_spec(tm, d), _vec_spec(d)], out_specs=_row_spec(tm, d),
        out_shape=jax.ShapeDtypeStruct((t, d), BF16), compiler_params=_params("parallel"),
    )(x, g.reshape(1, d))


def _rms_bwd(x, g, dres, dy):
    t, d = x.shape
    tm = _tile(t, 512)

    def body(x_ref, g_ref, dres_ref, dy_ref, dx_ref, dg_ref):
        dx, dgc = _rms_bwd_rows(x_ref[...], g_ref[...], dy_ref[...])
        dx_ref[...] = dres_ref[...] + dx

        @pl.when(pl.program_id(0) == 0)
        def _():
            dg_ref[...] = jnp.zeros_like(dg_ref)

        dg_ref[...] += _fold8(dgc)

    return pl.pallas_call(
        body, name="rms_bwd", grid=(t // tm,),
        in_specs=[_row_spec(tm, d), _vec_spec(d), _row_spec(tm, d), _row_spec(tm, d)],
        out_specs=[_row_spec(tm, d), _acc_spec(d)],
        out_shape=[jax.ShapeDtypeStruct((t, d), F32), jax.ShapeDtypeStruct((SUBLANES, d), F32)],
        compiler_params=_params("arbitrary"),
    )(x, g.reshape(1, d), dres, dy)


def _res_norm_fwd(x, m, g_post, g_next):
    t, d = x.shape
    tm = _tile(t, 512)

    def body(x_ref, m_ref, gp_ref, gn_ref, x2_ref, u2_ref):
        mv = m_ref[...]
        x2 = x_ref[...] + mv * _rinv(mv) * gp_ref[...]
        x2_ref[...] = x2
        u2_ref[...] = (x2 * _rinv(x2) * gn_ref[...]).astype(BF16)

    return pl.pallas_call(
        body, name="res_norm_fwd", grid=(t // tm,),
        in_specs=[_row_spec(tm, d), _row_spec(tm, d), _vec_spec(d), _vec_spec(d)],
        out_specs=[_row_spec(tm, d), _row_spec(tm, d)],
        out_shape=[jax.ShapeDtypeStruct((t, d), F32), jax.ShapeDtypeStruct((t, d), BF16)],
        compiler_params=_params("parallel"),
    )(x, m, g_post.reshape(1, d), g_next.reshape(1, d))


def _res_norm_bwd(x2, m, g_post, g_next, dx2_in, du2):
    t, d = x2.shape
    tm = _tile(t, 512)

    def body(x2_ref, m_ref, gp_ref, gn_ref, dx2in_ref, du2_ref, dx2_ref, dm_ref, dgp_ref, dgn_ref):
        dxn, dgn_c = _rms_bwd_rows(x2_ref[...], gn_ref[...], du2_ref[...])
        dx2 = dx2in_ref[...] + dxn
        dx2_ref[...] = dx2
        dm, dgp_c = _rms_bwd_rows(m_ref[...], gp_ref[...], dx2)
        dm_ref[...] = dm.astype(BF16)

        @pl.when(pl.program_id(0) == 0)
        def _():
            dgp_ref[...] = jnp.zeros_like(dgp_ref)
            dgn_ref[...] = jnp.zeros_like(dgn_ref)

        dgp_ref[...] += _fold8(dgp_c)
        dgn_ref[...] += _fold8(dgn_c)

    return pl.pallas_call(
        body, name="res_norm_bwd", grid=(t // tm,),
        in_specs=[_row_spec(tm, d), _row_spec(tm, d), _vec_spec(d), _vec_spec(d), _row_spec(tm, d), _row_spec(tm, d)],
        out_specs=[_row_spec(tm, d), _row_spec(tm, d), _acc_spec(d), _acc_spec(d)],
        out_shape=[jax.ShapeDtypeStruct((t, d), F32), jax.ShapeDtypeStruct((t, d), BF16),
                   jax.ShapeDtypeStruct((SUBLANES, d), F32), jax.ShapeDtypeStruct((SUBLANES, d), F32)],
        compiler_params=_params("arbitrary"),
    )(x2, m, g_post.reshape(1, d), g_next.reshape(1, d), dx2_in, du2)


def _rope_tables(t):
    rows = t // GRID_W
    row = jnp.repeat(jnp.arange(rows, dtype=F32), GRID_W)
    col = jnp.tile(jnp.arange(GRID_W, dtype=F32), rows)

    def tab(rot_dim):
        half = rot_dim // 2
        inv = ROPE_THETA ** (-jnp.arange(0, half, 2, dtype=F32) / half)
        ar = row[:, None] * inv[None, :]
        ac = col[:, None] * inv[None, :]
        ang = jnp.concatenate([ar, ar, ac, ac], axis=-1)
        q = half // 2
        sign = np.tile(np.concatenate([-np.ones(q, np.float32), np.ones(q, np.float32)]), 2)
        return jnp.cos(ang), jnp.sin(ang) * sign[None, :]

    ca, sa = tab(HEAD_DIM)
    cb, sb = tab(MLA_ROPE_DIM)
    one = jnp.ones((t, 1), F32)
    cos_b = jnp.concatenate([one * jnp.ones((1, KR_LANE0), F32), cb, one * jnp.ones((1, 32), F32)], axis=-1)
    sin_b = jnp.concatenate([jnp.zeros((t, KR_LANE0), F32), sb, jnp.zeros((t, 32), F32)], axis=-1)
    return jnp.tile(ca, (1, GQA_HEADS)), jnp.tile(sa, (1, GQA_HEADS)), cos_b, sin_b


def _swap_halves(x, sh):
    lane = lax.broadcasted_iota(jnp.int32, x.shape, 1)
    up = pltpu.roll(x, LANES - sh, 1)
    dn = pltpu.roll(x, sh, 1)
    return jnp.where((lane & (2 * sh - 1)) < sh, up, dn)


def _rope(x, cos, sin_s, sh):
    return x * cos + _swap_halves(x, sh) * sin_s


def _rope_bwd(dy, cos, sin_s, sh):
    return dy * cos + _swap_halves(dy * sin_s, sh)


def _lo_mask(shape):
    return lax.broadcasted_iota(jnp.int32, shape, 1) < HEAD_DIM


def _half_mean(t, lo):
    s_lo = jnp.sum(jnp.where(lo, t, 0.0), axis=-1, keepdims=True)
    s_hi = jnp.sum(jnp.where(lo, 0.0, t), axis=-1, keepdims=True)
    return jnp.where(lo, s_lo, s_hi) * (1.0 / HEAD_DIM)


def _head_norm(x, g2):
    lo = _lo_mask(x.shape)
    r = lax.rsqrt(_half_mean(x * x, lo) + EPS)
    return x * r * g2


def _head_norm_bwd(x, g2, dy):
    lo = _lo_mask(x.shape)
    r = lax.rsqrt(_half_mean(x * x, lo) + EPS)
    xh = x * r
    dxh = dy * g2
    dx = r * (dxh - xh * _half_mean(dxh * xh, lo))
    return dx, dy * xh


def _prep_a_fwd(z, gq2, gk2, gqa, gkva, cos_a, sin_a, cos_b, sin_b):
    t = z.shape[0]
    tm = _tile(t, 256)

    def body(z_ref, gq_ref, gk_ref, gqa_ref, gkva_ref, ca_ref, sa_ref, cb_ref, sb_ref,
             qa_ref, ka_ref, va_ref, cqn_ref, ckvn_ref, krr_ref):
        for j in range(4):
            cols = slice(LANES * j, LANES * (j + 1))
            y = _rope(_head_norm(z_ref[:, cols], gq_ref[...]), ca_ref[:, cols], sa_ref[:, cols], 16)
            qa_ref[:, cols] = (y * (GQA_SCALE * LOG2E)).astype(BF16)
        y = _rope(_head_norm(z_ref[:, Z_KA:Z_VA], gk_ref[...]), ca_ref[:, :LANES], sa_ref[:, :LANES], 16)
        ka_ref[...] = y.astype(BF16)
        va_ref[...] = z_ref[:, Z_VA:Z_CQ].astype(BF16)
        cq = z_ref[:, Z_CQ:Z_CKV]
        cqn_ref[...] = (cq * _rinv(cq) * gqa_ref[...]).astype(BF16)
        ckv = z_ref[:, Z_CKV:Z_KR]
        ckvn_ref[...] = (ckv * _rinv(ckv) * gkva_ref[...]).astype(BF16)
        krr_ref[...] = _rope(z_ref[:, Z_KR:Z_GATE], cb_ref[...], sb_ref[...], 8)

    return pl.pallas_call(
        body, name="prep_a_fwd", grid=(t // tm,),
        in_specs=[_row_spec(tm, Z_ATT_W), _vec_spec(LANES), _vec_spec(LANES), _vec_spec(MLA_Q_RANK),
                  _vec_spec(MLA_KV_RANK), _row_spec(tm, 512), _row_spec(tm, 512), _row_spec(tm, LANES),
                  _row_spec(tm, LANES)],
        out_specs=[_row_spec(tm, 512), _row_spec(tm, LANES), _row_spec(tm, LANES), _row_spec(tm, MLA_Q_RANK),
                   _row_spec(tm, MLA_KV_RANK), _row_spec(tm, LANES)],
        out_shape=[jax.ShapeDtypeStruct((t, 512), BF16), jax.ShapeDtypeStruct((t, LANES), BF16),
                   jax.ShapeDtypeStruct((t, LANES), BF16), jax.ShapeDtypeStruct((t, MLA_Q_RANK), BF16),
                   jax.ShapeDtypeStruct((t, MLA_KV_RANK), BF16), jax.ShapeDtypeStruct((t, LANES), F32)],
        compiler_params=_params("parallel"),
    )(z, gq2, gk2, gqa, gkva, cos_a, sin_a, cos_b, sin_b)


def _prep_a_bwd(z, dqa, dka4, dva4, dcqn, dckvn, dkr, dzga, dzgb, gq2, gk2, gqa, gkva, cos_a, sin_a):
    t = z.shape[0]
    tm = _tile(t, 256)

    def body(z_ref, dqa_ref, dka_ref, dva_ref, dcqn_ref, dckvn_ref, dkr_ref, dzga_ref, dzgb_ref, gq_ref, gk_ref,
             gqa_ref, gkva_ref, ca_ref, sa_ref, dz_ref, dgq_ref, dgk_ref, dgqa_ref, dgkva_ref):
        @pl.when(pl.program_id(0) == 0)
        def _():
            dgq_ref[...] = jnp.zeros_like(dgq_ref)
            dgk_ref[...] = jnp.zeros_like(dgk_ref)
            dgqa_ref[...] = jnp.zeros_like(dgqa_ref)
            dgkva_ref[...] = jnp.zeros_like(dgkva_ref)

        dgq = jnp.zeros((SUBLANES, LANES), F32)
        for j in range(4):
            cols = slice(LANES * j, LANES * (j + 1))
            dy = _rope_bwd(dqa_ref[:, cols] * GQA_SCALE, ca_ref[:, cols], sa_ref[:, cols], 16)
            dx, dgc = _head_norm_bwd(z_ref[:, cols], gq_ref[...], dy)
            dz_ref[:, cols] = dx.astype(BF16)
            dgq = dgq + _fold8(dgc)
        dgq_ref[...] += dgq
        dk = (dka_ref[0] + dka_ref[1] + dka_ref[2] + dka_ref[3]).T * LN2
        dy = _rope_bwd(dk, ca_ref[:, :LANES], sa_ref[:, :LANES], 16)
        dx, dgc = _head_norm_bwd(z_ref[:, Z_KA:Z_VA], gk_ref[...], dy)
        dz_ref[:, Z_KA:Z_VA] = dx.astype(BF16)
        dgk_ref[...] += _fold8(dgc)
        dz_ref[:, Z_VA:Z_CQ] = (dva_ref[0] + dva_ref[1] + dva_ref[2] + dva_ref[3]).T.astype(BF16)
        dx, dgc = _rms_bwd_rows(z_ref[:, Z_CQ:Z_CKV], gqa_ref[...], dcqn_ref[...])
        dz_ref[:, Z_CQ:Z_CKV] = dx.astype(BF16)
        dgqa_ref[...] += _fold8(dgc)
        dx, dgc = _rms_bwd_rows(z_ref[:, Z_CKV:Z_KR], gkva_ref[...], dckvn_ref[...])
        dz_ref[:, Z_CKV:Z_KR] = dx.astype(BF16)
        dgkva_ref[...] += _fold8(dgc)
        dz_ref[:, Z_KR:Z_GATE] = dkr_ref[...].astype(BF16)
        dz_ref[:, Z_GATE:Z_GATE + D_MODEL] = dzga_ref[...]
        dz_ref[:, Z_GATE + D_MODEL:Z_W] = dzgb_ref[...]

    part = pl.BlockSpec((4, LANES, tm), lambda i: (0, 0, i))
    return pl.pallas_call(
        body, name="prep_a_bwd", grid=(t // tm,),
        in_specs=[_row_spec(tm, Z_ATT_W), _row_spec(tm, 512), part, part, _row_spec(tm, MLA_Q_RANK),
                  _row_spec(tm, MLA_KV_RANK), _row_spec(tm, LANES), _row_spec(tm, D_MODEL), _row_spec(tm, D_MODEL),
                  _vec_spec(LANES),
                  _vec_spec(LANES), _vec_spec(MLA_Q_RANK), _vec_spec(MLA_KV_RANK), _row_spec(tm, 512),
                  _row_spec(tm, 512)],
        out_specs=[_row_spec(tm, Z_W), _acc_spec(LANES), _acc_spec(LANES), _acc_spec(MLA_Q_RANK),
                   _acc_spec(MLA_KV_RANK)],
        out_shape=[jax.ShapeDtypeStruct((t, Z_W), BF16), jax.ShapeDtypeStruct((SUBLANES, LANES), F32),
                   jax.ShapeDtypeStruct((SUBLANES, LANES), F32), jax.ShapeDtypeStruct((SUBLANES, MLA_Q_RANK), F32),
                   jax.ShapeDtypeStruct((SUBLANES, MLA_KV_RANK), F32)],
        compiler_params=_params("arbitrary"),
    )(z, dqa, dka4, dva4, dcqn, dckvn, dkr, dzga, dzgb, gq2, gk2, gqa, gkva, cos_a, sin_a)


def _prep_b_fwd(qb, kvb, krr, cos_b, sin_b):
    t = qb.shape[0]
    tm = _tile(t, 256)

    def body(qb_ref, kvb_ref, krr_ref, cb_ref, sb_ref, q_ref, k_ref, v_ref):
        for h in range(MLA_HEADS):
            cols = slice(LANES * h, LANES * (h + 1))
            q_ref[:, cols] = (_rope(qb_ref[:, cols], cb_ref[...], sb_ref[...], 8) * (MLA_SCALE * LOG2E)).astype(BF16)
            k_ref[:, cols] = (kvb_ref[:, cols] + krr_ref[...]).astype(BF16)
        v_ref[...] = kvb_ref[:, 1024:1536].astype(BF16)

    return pl.pallas_call(
        body, name="prep_b_fwd", grid=(t // tm,),
        in_specs=[_row_spec(tm, 1024), _row_spec(tm, 1536), _row_spec(tm, LANES), _row_spec(tm, LANES),
                  _row_spec(tm, LANES)],
        out_specs=[_row_spec(tm, 1024), _row_spec(tm, 1024), _row_spec(tm, 512)],
        out_shape=[jax.ShapeDtypeStruct((t, 1024), BF16), jax.ShapeDtypeStruct((t, 1024), BF16),
                   jax.ShapeDtypeStruct((t, 512), BF16)],
        compiler_params=_params("parallel"),
    )(qb, kvb, krr, cos_b, sin_b)


def _prep_b_bwd(dq, dk, dv, cos_b, sin_b):
    t = dq.shape[0]
    tm = _tile(t, 256)

    def body(dq_ref, dk_ref, dv_ref, cb_ref, sb_ref, dqb_ref, dkvb_ref, dkr_ref):
        dkr = jnp.zeros((tm, LANES), F32)
        for h in range(MLA_HEADS):
            cols = slice(LANES * h, LANES * (h + 1))
            dqb_ref[:, cols] = _rope_bwd(dq_ref[:, cols] * MLA_SCALE, cb_ref[...], sb_ref[...], 8).astype(BF16)
            dkh = dk_ref[cols, :].T * LN2
            dkvb_ref[:, cols] = dkh.astype(BF16)
            dkr = dkr + dkh
        for j in range(MLA_HEADS // 2):
            dkvb_ref[:, 1024 + LANES * j:1024 + LANES * (j + 1)] = dv_ref[LANES * j:LANES * (j + 1), :].T.astype(BF16)
        dkr_ref[...] = _rope_bwd(dkr, cb_ref[...], sb_ref[...], 8)

    return pl.pallas_call(
        body, name="prep_b_bwd", grid=(t // tm,),
        in_specs=[_row_spec(tm, 1024), pl.BlockSpec((1024, tm), lambda i: (0, i)),
                  pl.BlockSpec((512, tm), lambda i: (0, i)), _row_spec(tm, LANES),
                  _row_spec(tm, LANES)],
        out_specs=[_row_spec(tm, 1024), _row_spec(tm, 1536), _row_spec(tm, LANES)],
        out_shape=[jax.ShapeDtypeStruct((t, 1024), BF16), jax.ShapeDtypeStruct((t, 1536), BF16),
                   jax.ShapeDtypeStruct((t, LANES), F32)],
        compiler_params=_params("parallel"),
    )(dq, dk, dv, cos_b, sin_b)


_NT = (((1,), (1,)), ((), ()))
_NN = (((1,), (0,)), ((), ()))
_TN = (((0,), (0,)), ((), ()))


def _head_operands(qv, kv, i, shared_k):
    if shared_k:
        lo = _lo_mask(qv.shape)
        keep = lo if i == 0 else jnp.logical_not(lo)
        return jnp.where(keep, qv, jnp.zeros_like(qv)), kv
    cols = slice(LANES * i, LANES * (i + 1))
    return qv[:, cols], kv[:, cols]


def _attn_specs(shared_k, tq, tk, q_of, k_of):
    wq = LANES if shared_k else 2 * LANES
    q_spec = pl.BlockSpec((tq, wq), lambda *g: (q_of(*g), g[0]))
    if shared_k:
        k_spec = pl.BlockSpec((tk, LANES), lambda *g: (k_of(*g), 0))
        v_spec = pl.BlockSpec((tk, LANES), lambda *g: (k_of(*g), 0))
    else:
        k_spec = pl.BlockSpec((tk, wq), lambda *g: (k_of(*g), g[0]))
        v_spec = pl.BlockSpec((tk, LANES), lambda *g: (k_of(*g), g[0]))
    return wq, q_spec, k_spec, v_spec


def _attn_fwd(q, k, v, shared_k, name):
    t = q.shape[0]
    tq, tk = _tile(t, ATTN_TQ), _tile(t, ATTN_TK)
    nq, nk = t // tq, t // tk
    wq, q_spec, k_spec, v_spec = _attn_specs(shared_k, tq, tk, lambda p, i, j: i, lambda p, i, j: j)
    groups = q.shape[1] // wq
    chunk = _tile(tq, 2 * LANES)

    def body(q_ref, k_ref, v_ref, o_ref, lse_ref, m_s, l_s, acc_s, alpha_s, s_s, p_s):
        kb = pl.program_id(2)

        @pl.when(kb == 0)
        def _():
            m_s[...] = jnp.full_like(m_s, -jnp.inf)
            l_s[...] = jnp.zeros_like(l_s)
            acc_s[...] = jnp.zeros_like(acc_s)

        qv, kv, vv = q_ref[...], k_ref[...], v_ref[...]
        for i in range(2):
            qi, ki = _head_operands(qv, kv, i, shared_k)
            s_s[i] = lax.dot_general(ki, qi, _NT, preferred_element_type=F32)
        for i in range(2):
            for c in range(tq // chunk):
                cols = slice(c * chunk, (c + 1) * chunk)
                m_prev = m_s[i, :, cols]
                m_new = jnp.maximum(m_prev, jnp.max(s_s[i, :, cols], axis=0, keepdims=True))
                alpha = jnp.exp2(m_prev - m_new)
                pt = jnp.exp2(s_s[i, :, cols] - m_new)
                l_s[i, :, cols] = alpha * l_s[i, :, cols] + jnp.sum(pt, axis=0, keepdims=True)
                m_s[i, :, cols] = m_new
                alpha_s[i, :, cols] = alpha
                p_s[i, :, cols] = pt.astype(BF16)
        for i in range(2):
            acc_s[i] = alpha_s[i] * acc_s[i] + lax.dot_general(vv, p_s[i], _TN, preferred_element_type=F32)

        @pl.when(kb == nk - 1)
        def _():
            o0 = acc_s[0] / l_s[0]
            o1 = acc_s[1] / l_s[1]
            row_lo = lax.broadcasted_iota(jnp.int32, o0.shape, 0) < HEAD_DIM
            o_ref[...] = jnp.where(row_lo, o0, o1).T.astype(BF16)
            lse_ref[0] = m_s[0] + jnp.log2(l_s[0])
            lse_ref[1] = m_s[1] + jnp.log2(l_s[1])

    return pl.pallas_call(
        body, name=name, grid=(groups, nq, nk),
        in_specs=[q_spec, k_spec, v_spec],
        out_specs=[pl.BlockSpec((tq, LANES), lambda p, i, j: (i, p)),
                   pl.BlockSpec((2, 1, tq), lambda p, i, j: (p, 0, i))],
        out_shape=[jax.ShapeDtypeStruct((t, LANES * groups), BF16),
                   jax.ShapeDtypeStruct((2 * groups, 1, t), F32)],
        scratch_shapes=[pltpu.VMEM((2, 1, tq), F32), pltpu.VMEM((2, 1, tq), F32), pltpu.VMEM((2, LANES, tq), F32),
                        pltpu.VMEM((2, 1, tq), F32), pltpu.VMEM((2, tk, tq), F32), pltpu.VMEM((2, tk, tq), BF16)],
        compiler_params=_params("parallel", "parallel", "arbitrary"),
    )(q, k, v)


def _attn_stats(do, o, lse):
    t, w = do.shape
    tm = _tile(t, 512)
    groups = w // LANES

    def body(do_ref, o_ref, lse_ref, delta_ref, lser_ref, dob_ref):
        dov = do_ref[...]
        dob_ref[...] = dov.astype(BF16)
        prod = dov * o_ref[...].astype(F32)
        for g in range(groups):
            x = prod[:, LANES * g:LANES * (g + 1)]
            lo = _lo_mask(x.shape)
            d0 = jnp.sum(jnp.where(lo, x, 0.0), axis=-1, keepdims=True)
            d1 = jnp.sum(jnp.where(lo, 0.0, x), axis=-1, keepdims=True)
            delta_ref[2 * g] = jnp.broadcast_to(d0, (tm, LANES))
            delta_ref[2 * g + 1] = jnp.broadcast_to(d1, (tm, LANES))
        for h in range(2 * groups):
            lser_ref[h] = jnp.broadcast_to(lse_ref[h], (LANES, tm)).T

    rep_spec = pl.BlockSpec((2 * groups, tm, LANES), lambda i: (0, i, 0))
    rep_shape = jax.ShapeDtypeStruct((2 * groups, t, LANES), F32)
    return pl.pallas_call(
        body, name="attn_stats", grid=(t // tm,),
        in_specs=[_row_spec(tm, w), _row_spec(tm, w), pl.BlockSpec((2 * groups, 1, tm), lambda i: (0, 0, i))],
        out_specs=[rep_spec, rep_spec, _row_spec(tm, w)],
        out_shape=[rep_shape, rep_shape, jax.ShapeDtypeStruct((t, w), BF16)],
        compiler_params=_params("parallel"),
    )(do, o, lse)


def _attn_bwd(q, k, v, do, lse, delta, shared_k, name):
    t = q.shape[0]
    tq, tk = _tile(t, ATTN_TQ), _tile(t, ATTN_TK)
    nq, nk = t // tq, t // tk
    wq, q_spec, k_spec, v_spec = _attn_specs(shared_k, tq, tk, lambda p, j, i: i, lambda p, j, i: j)
    groups = q.shape[1] // wq

    def body(q_ref, k_ref, v_ref, do_ref, lse_ref, delta_ref, dq_ref, dk_ref, dv_ref, dk_s, dv_s, s_s, dp_s, p_s,
             ds_s):
        kb, qb = pl.program_id(1), pl.program_id(2)

        @pl.when(qb == 0)
        def _():
            dk_s[...] = jnp.zeros_like(dk_s)
            dv_s[...] = jnp.zeros_like(dv_s)

        qv, kv, vv, dov = q_ref[...], k_ref[...], v_ref[...], do_ref[...]
        lo = _lo_mask(dov.shape)
        heads = []
        for i in range(2):
            qi, ki = _head_operands(qv, kv, i, shared_k)
            keep = lo if i == 0 else jnp.logical_not(lo)
            doi = jnp.where(keep, dov, jnp.zeros_like(dov))
            heads.append((qi, ki, doi))
            s_s[i] = lax.dot_general(qi, ki, _NT, preferred_element_type=F32)
            dp_s[i] = lax.dot_general(doi, vv, _NT, preferred_element_type=F32)
        for i in range(2):
            lse_i, delta_i = lse_ref[i], delta_ref[i]
            for c in range(tk // LANES):
                cols = slice(c * LANES, (c + 1) * LANES)
                p = jnp.exp2(s_s[i, :, cols] - lse_i)
                p_s[i, :, cols] = p.astype(BF16)
                ds_s[i, :, cols] = (p * (dp_s[i, :, cols] - delta_i)).astype(BF16)
        dq_parts = []
        for i in range(2):
            qi, ki, doi = heads[i]
            dv_s[...] += lax.dot_general(doi, p_s[i], _TN, preferred_element_type=F32)
            dk_i = lax.dot_general(qi, ds_s[i], _TN, preferred_element_type=F32)
            if shared_k:
                dk_s[...] += dk_i
            else:
                dk_s[LANES * i:LANES * (i + 1), :] += dk_i
            dq_parts.append(lax.dot_general(ds_s[i], ki, _NN, preferred_element_type=F32))
        rows = pl.ds(pl.multiple_of(qb * tq, tq), tq)
        if shared_k:
            tiles = [(slice(0, LANES), jnp.where(lo, dq_parts[0], dq_parts[1]))]
        else:
            tiles = [(slice(0, LANES), dq_parts[0]), (slice(LANES, 2 * LANES), dq_parts[1])]
        for cols, val in tiles:
            @pl.when(kb == 0)
            def _(cols=cols, val=val):
                dq_ref[rows, cols] = val

            @pl.when(kb > 0)
            def _(cols=cols, val=val):
                dq_ref[rows, cols] += val

        @pl.when(qb == nq - 1)
        def _():
            if shared_k:
                dk_ref[0] = dk_s[...]
                dv_ref[0] = dv_s[...]
            else:
                dk_ref[...] = dk_s[...]
                dv_ref[...] = dv_s[...]

    stat_spec = pl.BlockSpec((2, tq, LANES), lambda p, j, i: (p, i, 0))
    do_spec = pl.BlockSpec((tq, LANES), lambda p, j, i: (i, p))
    dq_spec = pl.BlockSpec((t, wq), lambda p, j, i: (0, p))
    if shared_k:
        dk_spec = pl.BlockSpec((1, LANES, tk), lambda p, j, i: (p, 0, j))
        dv_spec = dk_spec
        dk_shape = jax.ShapeDtypeStruct((groups, LANES, t), F32)
        dv_shape = dk_shape
    else:
        dk_spec = pl.BlockSpec((wq, tk), lambda p, j, i: (p, j))
        dv_spec = pl.BlockSpec((LANES, tk), lambda p, j, i: (p, j))
        dk_shape = jax.ShapeDtypeStruct((wq * groups, t), F32)
        dv_shape = jax.ShapeDtypeStruct((LANES * groups, t), F32)
    return pl.pallas_call(
        body, name=name, grid=(groups, nk, nq),
        in_specs=[q_spec, k_spec, v_spec, do_spec, stat_spec, stat_spec],
        out_specs=[dq_spec, dk_spec, dv_spec],
        out_shape=[jax.ShapeDtypeStruct((t, wq * groups), F32), dk_shape, dv_shape],
        scratch_shapes=[pltpu.VMEM((wq, tk), F32), pltpu.VMEM((LANES, tk), F32), pltpu.VMEM((2, tq, tk), F32),
                        pltpu.VMEM((2, tq, tk), F32), pltpu.VMEM((2, tq, tk), BF16), pltpu.VMEM((2, tq, tk), BF16)],
        compiler_params=_params("parallel", "arbitrary", "arbitrary"),
    )(q, k, v, do, lse, delta)


_MERGE_W = 512
_GATE_BLK0 = Z_GATE // _MERGE_W


def _merge_fwd(z, b_gate, ta, tb):
    t = z.shape[0]
    tm = _tile(t, 512)
    w = _MERGE_W
    nj = D_MODEL // w

    def body(za_ref, zb_ref, ba_ref, bb_ref, ta_ref, tb_ref, o_ref):
        ga = jax.nn.sigmoid(za_ref[...] + ba_ref[...])
        gb = jax.nn.sigmoid(zb_ref[...] + bb_ref[...])
        o_ref[...] = (ga * ta_ref[...] + gb * tb_ref[...]).astype(BF16)

    return pl.pallas_call(
        body, name="merge_fwd", grid=(t // tm, nj),
        in_specs=[pl.BlockSpec((tm, w), lambda i, j: (i, _GATE_BLK0 + j)),
                  pl.BlockSpec((tm, w), lambda i, j: (i, _GATE_BLK0 + nj + j)),
                  pl.BlockSpec((1, w), lambda i, j: (0, j)),
                  pl.BlockSpec((1, w), lambda i, j: (0, nj + j)),
                  pl.BlockSpec((tm, w), lambda i, j: (i, j)),
                  pl.BlockSpec((tm, w), lambda i, j: (i, j))],
        out_specs=pl.BlockSpec((tm, w), lambda i, j: (i, j)),
        out_shape=jax.ShapeDtypeStruct((t, D_MODEL), BF16),
        compiler_params=_params("parallel", "parallel"),
    )(z, z, b_gate, b_gate, ta, tb)


def _merge_bwd(dmg, z, b_gate, ta, tb):
    t = z.shape[0]
    tm = _tile(t, 512)
    w = _MERGE_W
    nj = D_MODEL // w

    def body(dm_ref, za_ref, zb_ref, ba_ref, bb_ref, ta_ref, tb_ref, dta_ref, dtb_ref, dza_ref, dzb_ref,
             dba_ref, dbb_ref):
        dm = dm_ref[...]
        ga = jax.nn.sigmoid(za_ref[...] + ba_ref[...])
        gb = jax.nn.sigmoid(zb_ref[...] + bb_ref[...])
        dta_ref[...] = (dm * ga).astype(BF16)
        dtb_ref[...] = (dm * gb).astype(BF16)
        dza = dm * ta_ref[...] * ga * (1.0 - ga)
        dzb = dm * tb_ref[...] * gb * (1.0 - gb)
        dza_ref[...] = dza.astype(BF16)
        dzb_ref[...] = dzb.astype(BF16)

        @pl.when(pl.program_id(1) == 0)
        def _():
            dba_ref[...] = jnp.zeros_like(dba_ref)
            dbb_ref[...] = jnp.zeros_like(dbb_ref)

        dba_ref[...] += _fold8(dza)
        dbb_ref[...] += _fold8(dzb)

    blk = pl.BlockSpec((tm, w), lambda j, i: (i, j))
    acc = pl.BlockSpec((SUBLANES, w), lambda j, i: (0, j))
    return pl.pallas_call(
        body, name="merge_bwd", grid=(nj, t // tm),
        in_specs=[blk,
                  pl.BlockSpec((tm, w), lambda j, i: (i, _GATE_BLK0 + j)),
                  pl.BlockSpec((tm, w), lambda j, i: (i, _GATE_BLK0 + nj + j)),
                  pl.BlockSpec((1, w), lambda j, i: (0, j)),
                  pl.BlockSpec((1, w), lambda j, i: (0, nj + j)),
                  blk, blk],
        out_specs=[blk, blk, blk, blk, acc, acc],
        out_shape=[jax.ShapeDtypeStruct((t, D_MODEL), BF16)] * 4 + [jax.ShapeDtypeStruct((SUBLANES, D_MODEL), F32)] * 2,
        compiler_params=_params("parallel", "arbitrary"),
    )(dmg, z, z, b_gate, b_gate, ta, tb)


def _loss_grad(y, target):
    t, d = y.shape
    tm = _tile(t, 512)

    def body(y_ref, t_ref, dy_ref, acc_ref):
        err = y_ref[...] - t_ref[...]
        dy_ref[...] = err * (1.0 / d)
        e8 = _fold8(err * err)
        part = e8[:, 0:LANES]
        for c in range(1, d // LANES):
            part = part + e8[:, LANES * c:LANES * (c + 1)]

        @pl.when(pl.program_id(0) == 0)
        def _():
            acc_ref[...] = jnp.zeros_like(acc_ref)

        acc_ref[...] += part

    return pl.pallas_call(
        body, name="loss_grad", grid=(t // tm,),
        in_specs=[_row_spec(tm, d), _row_spec(tm, d)],
        out_specs=[_row_spec(tm, d), _acc_spec(LANES)],
        out_shape=[jax.ShapeDtypeStruct((t, d), F32), jax.ShapeDtypeStruct((SUBLANES, LANES), F32)],
        compiler_params=_params("arbitrary"),
    )(y, target)


_MESH_ID = pl.DeviceIdType.MESH
_ANY = pl.BlockSpec(memory_space=pl.ANY)


def _all_gather(arrays):
    n = len(arrays)

    def body(*refs):
        x_refs, out_refs = refs[:n], refs[n:2 * n]
        send_sems, recv_sems, local_sems = refs[2 * n:]
        mx, my, mc = lax.axis_index("x"), lax.axis_index("y"), lax.axis_index("c")
        me, sibling = (mx, my, mc), (mx, my, 1 - mc)
        chips = [(1 - mx, my), (mx, 1 - my), (1 - mx, 1 - my)]

        def slot(a, px, py, pc):
            return out_refs[a].at[4 * px + 2 * py + pc]

        def copy(a, sem, block, to, src=None):
            return pltpu.make_async_remote_copy(
                src_ref=slot(a, *block) if src is None else src, dst_ref=slot(a, *block),
                send_sem=send_sems.at[a, sem], recv_sem=recv_sems.at[a, sem], device_id=to, device_id_type=_MESH_ID)

        mine = [pltpu.make_async_copy(x_refs[a], slot(a, *me), local_sems.at[a]) for a in range(n)]
        first = []
        for a in range(n):
            mine[a].start()
            first.append(copy(a, 0, me, sibling, src=x_refs[a]))
            first += [copy(a, 1 + j, me, (*chip, mc), src=x_refs[a]) for j, chip in enumerate(chips)]
        for cp in first:
            cp.start()
        passed = []
        for a in range(n):
            for j, chip in enumerate(chips):
                copy(a, 1 + j, (*chip, mc), me).wait_recv()
                passed.append(copy(a, 4 + j, (*chip, mc), sibling))
                passed[-1].start()
        for a in range(n):
            copy(a, 0, sibling, me).wait_recv()
            for j, chip in enumerate(chips):
                copy(a, 4 + j, (*chip, 1 - mc), me).wait_recv()
        for cp in first + passed:
            cp.wait_send()
        for cp in mine:
            cp.wait()

    return pl.pallas_call(
        body, name="weight_all_gather",
        out_shape=[jax.ShapeDtypeStruct((N_DEV,) + a.shape, a.dtype) for a in arrays],
        in_specs=[_ANY] * n, out_specs=[_ANY] * n,
        scratch_shapes=[pltpu.SemaphoreType.DMA((n, 7)), pltpu.SemaphoreType.DMA((n, 7)),
                        pltpu.SemaphoreType.DMA((n,))],
    )(*arrays)


def _pair_exchange(sends):
    n = len(sends)

    def body(*refs):
        s_refs, r_refs = refs[:n], refs[n:2 * n]
        send_sems, recv_sems = refs[2 * n:]
        mx, my, mc = lax.axis_index("x"), lax.axis_index("y"), lax.axis_index("c")
        copies = []
        for a in range(n):
            for ch in range(4):
                cp = pltpu.make_async_remote_copy(
                    src_ref=s_refs[a].at[2 * ch + (1 - mc)], dst_ref=r_refs[a].at[ch], send_sem=send_sems.at[a, ch],
                    recv_sem=recv_sems.at[a, ch], device_id=(mx, my, 1 - mc), device_id_type=_MESH_ID)
                cp.start()
                copies.append(cp)
        for cp in copies:
            cp.wait_send()
            cp.wait_recv()

    return pl.pallas_call(
        body, name="grad_pair_exchange",
        out_shape=[jax.ShapeDtypeStruct((4,) + s.shape[1:], s.dtype) for s in sends],
        in_specs=[_ANY] * n, out_specs=[_ANY] * n,
        scratch_shapes=[pltpu.SemaphoreType.DMA((n, 4)), pltpu.SemaphoreType.DMA((n, 4))],
    )(*sends)


def _pair_add(send, half, core):
    _, r, c_ = send.shape
    tr = _row_tile(r, c_)

    def body(core_ref, s_ref, h_ref, o_ref):
        del core_ref
        o_ref[...] = (s_ref[...] + h_ref[...]).astype(BF16)

    blk = pl.BlockSpec((1, tr, c_), lambda ch, i, core_ref: (ch, i, 0))
    return pl.pallas_call(
        body, name="grad_pair_add",
        grid_spec=pltpu.PrefetchScalarGridSpec(
            num_scalar_prefetch=1, grid=(4, r // tr),
            in_specs=[pl.BlockSpec((1, tr, c_), lambda ch, i, core_ref: (2 * ch + core_ref[0], i, 0)), blk],
            out_specs=blk),
        out_shape=jax.ShapeDtypeStruct((4, r, c_), BF16),
        compiler_params=_params("parallel", "parallel"),
    )(core, send, half)


def _chip_exchange(parts):
    n = len(parts)

    def body(*refs):
        p_refs, r_refs = refs[:n], refs[n:2 * n]
        send_sems, recv_sems, local_sems = refs[2 * n:]
        mx, my, mc = lax.axis_index("x"), lax.axis_index("y"), lax.axis_index("c")
        mine = 2 * mx + my
        local = [pltpu.make_async_copy(p_refs[a].at[mine], r_refs[a].at[mine], local_sems.at[a]) for a in range(n)]
        copies = []
        for a in range(n):
            local[a].start()
            for rel in range(1, 4):
                px = 1 - mx if rel & 2 else mx
                py = 1 - my if rel & 1 else my
                cp = pltpu.make_async_remote_copy(
                    src_ref=p_refs[a].at[2 * px + py], dst_ref=r_refs[a].at[mine], send_sem=send_sems.at[a, rel - 1],
                    recv_sem=recv_sems.at[a, rel - 1], device_id=(px, py, mc), device_id_type=_MESH_ID)
                cp.start()
                copies.append(cp)
        for cp in copies:
            cp.wait_send()
            cp.wait_recv()
        for cp in local:
            cp.wait()

    return pl.pallas_call(
        body, name="grad_chip_exchange",
        out_shape=[jax.ShapeDtypeStruct(p.shape, p.dtype) for p in parts],
        in_specs=[_ANY] * n, out_specs=[_ANY] * n,
        scratch_shapes=[pltpu.SemaphoreType.DMA((n, 3)), pltpu.SemaphoreType.DMA((n, 3)),
                        pltpu.SemaphoreType.DMA((n,))],
    )(*parts)


def _row_tile(r, c_):
    tr = min(r, ADAM_BLOCK_ELEMS // (pl.cdiv(c_, LANES) * LANES))
    while r % tr:
        tr -= SUBLANES
    return tr


def _adamw(recv, w, m, v):
    r, c_ = w.shape
    tr = _row_tile(r, c_)
    n_src = recv.shape[0]

    def body(g_ref, w_ref, m_ref, v_ref, go_ref, d_ref, mo_ref, vo_ref):
        g = g_ref[0].astype(F32)
        for s in range(1, n_src):
            g = g + g_ref[s].astype(F32)
        go_ref[...] = g
        mn = ADAM_B1 * m_ref[...] + (1.0 - ADAM_B1) * g
        vn = ADAM_B2 * v_ref[...] + (1.0 - ADAM_B2) * (g * g)
        mo_ref[...] = mn
        vo_ref[...] = vn
        m_hat = mn / (1.0 - ADAM_B1 ** ADAM_STEP)
        v_hat = vn / (1.0 - ADAM_B2 ** ADAM_STEP)
        d_ref[...] = -ADAM_LR * (m_hat / (jnp.sqrt(v_hat) + ADAM_EPS) + ADAM_WD * w_ref[...])

    spec = pl.BlockSpec((tr, c_), lambda i: (i, 0))
    out = jax.ShapeDtypeStruct((r, c_), F32)
    return pl.pallas_call(
        body, name="grad_sum_adamw", grid=(r // tr,),
        in_specs=[pl.BlockSpec((n_src, tr, c_), lambda i: (0, i, 0)), spec, spec, spec],
        out_specs=[spec, spec, spec, spec], out_shape=[out, out, out, out],
        compiler_params=_params("parallel"),
    )(recv, w, m, v)


def _pad_cols(a, before, after):
    parts = []
    if before:
        parts.append(jnp.zeros(a.shape[:-1] + (before,), a.dtype))
    parts.append(a)
    if after:
        parts.append(jnp.zeros(a.shape[:-1] + (after,), a.dtype))
    return jnp.concatenate(parts, axis=-1)


def _q_head_pairs(a, axis):
    shp = a.shape
    a = a.reshape(shp[:axis] + (GQA_KV_HEADS, GQA_GROUP, HEAD_DIM) + shp[axis + 1:])
    a = jnp.swapaxes(a, axis, axis + 1)
    return a.reshape(shp)


def _q_head_unpairs(a, axis):
    shp = a.shape
    a = a.reshape(shp[:axis] + (GQA_GROUP, GQA_KV_HEADS, HEAD_DIM) + shp[axis + 1:])
    a = jnp.swapaxes(a, axis, axis + 1)
    return a.reshape(shp)


def _layout_weights(w):
    w_in = w["w_in"]
    lead = w_in.shape[:-1]
    w_in_p = jnp.concatenate([
        _q_head_pairs(w_in[..., 0:512], w_in.ndim - 1),
        w_in[..., 512:1408],
        _pad_cols(w_in[..., 1408:1440], KR_LANE0, LANES - KR_LANE0 - MLA_ROPE_DIM),
        w_in[..., 1440:],
    ], axis=-1)
    wq = w["w_q_up"]
    wq_p = _pad_cols(wq.reshape(wq.shape[:-1] + (MLA_HEADS, MLA_QK_DIM)), 0, LANES - MLA_QK_DIM)
    wq_p = wq_p.reshape(wq.shape[:-1] + (MLA_HEADS * LANES,))
    wkv = w["w_kv_up"]
    wkv4 = wkv.reshape(wkv.shape[:-1] + (MLA_HEADS, 2 * HEAD_DIM))
    wk_p = _pad_cols(wkv4[..., :HEAD_DIM], 0, LANES - HEAD_DIM).reshape(wkv.shape[:-1] + (MLA_HEADS * LANES,))
    wv_p = wkv4[..., HEAD_DIM:].reshape(wkv.shape[:-1] + (MLA_HEADS * HEAD_DIM,))
    del lead
    return {
        "w_in": w_in_p, "w_q_up": wq_p, "w_kv_up": jnp.concatenate([wk_p, wv_p], axis=-1),
        "w_branch_a": _q_head_pairs(w["w_branch_a"], w["w_branch_a"].ndim - 2), "w_branch_b": w["w_branch_b"],
        "w_o": w["w_o"], "w_ffn_up": w["w_ffn_up"], "w_ffn_down": w["w_ffn_down"],
    }


def _unlayout_grads(g):
    gi = g["w_in"]
    kr0 = Z_KR + KR_LANE0
    g_in = jnp.concatenate([
        _q_head_unpairs(gi[..., 0:512], gi.ndim - 1), gi[..., 512:1408], gi[..., kr0:kr0 + MLA_ROPE_DIM],
        gi[..., Z_GATE:],
    ], axis=-1)
    gq = g["w_q_up"]
    gq = gq.reshape(gq.shape[:-1] + (MLA_HEADS, LANES))[..., :MLA_QK_DIM]
    gq = gq.reshape(gq.shape[:-2] + (MLA_HEADS * MLA_QK_DIM,))
    gkv = g["w_kv_up"]
    gk = gkv[..., :MLA_HEADS * LANES].reshape(gkv.shape[:-1] + (MLA_HEADS, LANES))[..., :HEAD_DIM]
    gv = gkv[..., MLA_HEADS * LANES:].reshape(gkv.shape[:-1] + (MLA_HEADS, HEAD_DIM))
    gkv = jnp.concatenate([gk, gv], axis=-1).reshape(gkv.shape[:-1] + (MLA_HEADS * 2 * HEAD_DIM,))
    return {
        "w_in": g_in, "w_q_up": gq, "w_kv_up": gkv,
        "w_branch_a": _q_head_unpairs(g["w_branch_a"], g["w_branch_a"].ndim - 2), "w_branch_b": g["w_branch_b"],
        "w_o": g["w_o"], "w_ffn_up": g["w_ffn_up"], "w_ffn_down": g["w_ffn_down"],
    }


def _pack_small(parts):
    flat = jnp.concatenate([p.reshape(-1) for p in parts])
    pad = (-flat.shape[0]) % (SUBLANES * LANES)
    if pad:
        flat = jnp.concatenate([flat, jnp.zeros((pad,), flat.dtype)])
    return flat.reshape(-1, LANES)


def _unpack_small(packed, shapes):
    flat = packed.reshape(-1)
    out, off = [], 0
    for shp in shapes:
        n = int(np.prod(shp))
        out.append(flat[off:off + n].reshape(shp))
        off += n
    return out


def _shards_of(full, axis):
    shp = full.shape
    cut = shp[:axis] + (N_DEV, shp[axis] // N_DEV) + shp[axis + 1:]
    return jnp.moveaxis(full.reshape(cut), axis, 0)


def _from_shards(shards, axis):
    full = list(shards.shape[1:])
    full[axis] *= N_DEV
    return jnp.moveaxis(shards, 0, axis).reshape(full)


def _rows2d(a):
    return a.reshape(-1, a.shape[-1])


def _layer_fwd(x, u, lw, tabs):
    cos_a, sin_a, cos_b, sin_b = tabs
    z = _matmul(u, lw["w_in"], "nn", "mm_in")
    qa, ka, va, cqn, ckvn, krr = _prep_a_fwd(z, lw["gq2"], lw["gk2"], lw["gqa"], lw["gkva"], cos_a, sin_a, cos_b, sin_b)
    qb = _matmul(cqn, lw["w_q_up"], "nn", "mm_q_up")
    kvb = _matmul(ckvn, lw["w_kv_up"], "nn", "mm_kv_up")
    q_b, k_b, v_b = _prep_b_fwd(qb, kvb, krr, cos_b, sin_b)
    ya, lse_a = _attn_fwd(qa, ka, va, True, "gqa_fwd")
    yb, lse_b = _attn_fwd(q_b, k_b, v_b, False, "mla_fwd")
    ta = _matmul(ya, lw["w_branch_a"], "nn", "mm_branch_a")
    tb = _matmul(yb, lw["w_branch_b"], "nn", "mm_branch_b")
    merged = _merge_fwd(z, lw["b_gate"], ta, tb)
    m = _matmul(merged, lw["w_o"], "nn", "mm_o")
    x2, u2 = _res_norm_fwd(x, m, lw["post_mix_g"], lw["pre_ffn_g"])
    h, a = _matmul(u2, lw["w_ffn_up"], "nn", "mm_ffn_up", post="relu2")
    f = _matmul(a, lw["w_ffn_down"], "nn", "mm_ffn_down")
    x3, u_next = _res_norm_fwd(x2, f, lw["post_ffn_g"], lw["next_pre_mix_g"])
    saved = dict(u=u, z=z, qa=qa, ka=ka, va=va, cqn=cqn, ckvn=ckvn, q_b=q_b, k_b=k_b, v_b=v_b, ya=ya, yb=yb,
                 lse_a=lse_a, lse_b=lse_b, ta=ta, tb=tb, merged=merged, m=m, x2=x2, u2=u2, h=h, a=a, f=f, x3=x3)
    return x3, u_next, saved


def _layer_bwd(dx3, du_next, lw, sv, tabs):
    cos_a, sin_a, cos_b, sin_b = tabs
    g = {}
    dx3, df, dg4, dg1n = _res_norm_bwd(sv["x3"], sv["f"], lw["post_ffn_g"], lw["next_pre_mix_g"], dx3, du_next)
    g["post_ffn_g"], g["next_pre_mix_g"] = dg4, dg1n
    dh = _matmul(df, lw["w_ffn_down"], "nt", "mm_d_h", post="relu2_bwd", h=sv["h"])
    g["w_ffn_down"] = _matmul(sv["a"], df, "tn", "mm_dw_ffn_down")
    du2 = _matmul(dh, lw["w_ffn_up"], "nt", "mm_d_u2")
    g["w_ffn_up"] = _matmul(sv["u2"], dh, "tn", "mm_dw_ffn_up")
    dx2, dm, dg2, dg3 = _res_norm_bwd(sv["x2"], sv["m"], lw["post_mix_g"], lw["pre_ffn_g"], dx3, du2)
    g["post_mix_g"], g["pre_ffn_g"] = dg2, dg3
    dmg = _matmul(dm, lw["w_o"], "nt", "mm_d_merged")
    g["w_o"] = _matmul(sv["merged"], dm, "tn", "mm_dw_o")
    dta, dtb, dzg_a, dzg_b, db_a, db_b = _merge_bwd(dmg, sv["z"], lw["b_gate"], sv["ta"], sv["tb"])
    g["b_gate"] = jnp.concatenate([db_a, db_b], axis=-1)
    dya = _matmul(dta, lw["w_branch_a"], "nt", "mm_d_ya")
    g["w_branch_a"] = _matmul(sv["ya"], dta, "tn", "mm_dw_branch_a")
    dyb = _matmul(dtb, lw["w_branch_b"], "nt", "mm_d_yb")
    g["w_branch_b"] = _matmul(sv["yb"], dtb, "tn", "mm_dw_branch_b")
    delta_a, lse_a, dya16 = _attn_stats(dya, sv["ya"], sv["lse_a"])
    delta_b, lse_b, dyb16 = _attn_stats(dyb, sv["yb"], sv["lse_b"])
    dqa, dka4, dva4 = _attn_bwd(sv["qa"], sv["ka"], sv["va"], dya16, lse_a, delta_a, True, "gqa_bwd")
    dq_b, dk_b, dv_b = _attn_bwd(sv["q_b"], sv["k_b"], sv["v_b"], dyb16, lse_b, delta_b, False, "mla_bwd")
    dqb, dkvb, dkr = _prep_b_bwd(dq_b, dk_b, dv_b, cos_b, sin_b)
    dcqn = _matmul(dqb, lw["w_q_up"], "nt", "mm_d_cqn")
    g["w_q_up"] = _matmul(sv["cqn"], dqb, "tn", "mm_dw_q_up")
    dckvn = _matmul(dkvb, lw["w_kv_up"], "nt", "mm_d_ckvn")
    g["w_kv_up"] = _matmul(sv["ckvn"], dkvb, "tn", "mm_dw_kv_up")
    dz, dgq, dgk, dgqa, dgkva = _prep_a_bwd(sv["z"], dqa, dka4, dva4, dcqn, dckvn, dkr, dzg_a, dzg_b, lw["gq2"],
                                            lw["gk2"], lw["gqa"], lw["gkva"], cos_a, sin_a)
    g["q_norm_g"], g["k_norm_g"], g["q_a_norm_g"], g["kv_a_norm_g"] = dgq, dgk, dgqa, dgkva
    du = _matmul(dz, lw["w_in"], "nt", "mm_d_u")
    g["w_in"] = _matmul(sv["u"], dz, "tn", "mm_dw_in")
    return dx2, du, g


def kernel(x, w_in, b_gate, q_norm_g, k_norm_g, q_a_norm_g, kv_a_norm_g, w_q_up, w_kv_up, w_branch_a, w_branch_b, w_o, w_ffn_up, w_ffn_down, pre_mix_g, post_mix_g, pre_ffn_g, post_ffn_g, loss_target, m_w_in, m_b_gate, m_q_norm_g, m_k_norm_g, m_q_a_norm_g, m_kv_a_norm_g, m_w_q_up, m_w_kv_up, m_w_branch_a, m_w_branch_b, m_w_o, m_w_ffn_up, m_w_ffn_down, m_pre_mix_g, m_post_mix_g, m_pre_ffn_g, m_post_ffn_g, v_w_in, v_b_gate, v_q_norm_g, v_k_norm_g, v_q_a_norm_g, v_kv_a_norm_g, v_w_q_up, v_w_kv_up, v_w_branch_a, v_w_branch_b, v_w_o, v_w_ffn_up, v_w_ffn_down, v_pre_mix_g, v_post_mix_g, v_pre_ffn_g, v_post_ffn_g):
    weights = dict(zip(WEIGHT_NAMES, (w_in, b_gate, q_norm_g, k_norm_g, q_a_norm_g, kv_a_norm_g, w_q_up, w_kv_up,
                                      w_branch_a, w_branch_b, w_o, w_ffn_up, w_ffn_down, pre_mix_g, post_mix_g,
                                      pre_ffn_g, post_ffn_g)))
    mom_m = dict(zip(WEIGHT_NAMES, (m_w_in, m_b_gate, m_q_norm_g, m_k_norm_g, m_q_a_norm_g, m_kv_a_norm_g, m_w_q_up,
                                    m_w_kv_up, m_w_branch_a, m_w_branch_b, m_w_o, m_w_ffn_up, m_w_ffn_down,
                                    m_pre_mix_g, m_post_mix_g, m_pre_ffn_g, m_post_ffn_g)))
    mom_v = dict(zip(WEIGHT_NAMES, (v_w_in, v_b_gate, v_q_norm_g, v_k_norm_g, v_q_a_norm_g, v_kv_a_norm_g, v_w_q_up,
                                    v_w_kv_up, v_w_branch_a, v_w_branch_b, v_w_o, v_w_ffn_up, v_w_ffn_down,
                                    v_pre_mix_g, v_post_mix_g, v_pre_ffn_g, v_post_ffn_g)))
    assert x.shape[0] == 1 and x.shape[2] == D_MODEL, x.shape
    n_layers = w_in.shape[0]
    t = x.shape[1]
    x0 = x.reshape(t, D_MODEL)
    target = loss_target.reshape(t, D_MODEL)
    shard_shapes = {n: weights[n].shape for n in BIG_NAMES}
    small_shapes = [weights[n].shape for n in SMALL_NAMES]

    gathered = _all_gather([weights[n].astype(BF16) for n in BIG_NAMES])
    full = {n: _from_shards(g, SHARD_AXIS[n]) for n, g in zip(BIG_NAMES, gathered)}
    lw_all = _layout_weights(full)
    lw_all["b_gate"] = b_gate.reshape(n_layers, 1, 2 * D_MODEL)
    lw_all["gq2"] = jnp.tile(q_norm_g, (1, 2)).reshape(n_layers, 1, LANES)
    lw_all["gk2"] = jnp.tile(k_norm_g, (1, 2)).reshape(n_layers, 1, LANES)
    lw_all["gqa"] = q_a_norm_g.reshape(n_layers, 1, MLA_Q_RANK)
    lw_all["gkva"] = kv_a_norm_g.reshape(n_layers, 1, MLA_KV_RANK)
    for n in ("post_mix_g", "pre_ffn_g", "post_ffn_g"):
        lw_all[n] = weights[n]
    lw_all["next_pre_mix_g"] = jnp.roll(pre_mix_g, -1, axis=0)

    tabs = _rope_tables(t)
    u0 = _rms_fwd(x0, pre_mix_g[0])

    layer_w = [{n: a[l] for n, a in lw_all.items()} for l in range(n_layers)]
    xc, uc, saved = x0, u0, []
    for l in range(n_layers):
        xc, uc, sv = _layer_fwd(xc, uc, layer_w[l], tabs)
        saved.append(sv)
    dy, loss_acc = _loss_grad(xc, target)
    loss = lax.psum(0.5 * jnp.sum(loss_acc) / D_MODEL, ("x", "y", "c"))

    dx0, du0, layer_g = dy, jnp.zeros((t, D_MODEL), F32), [None] * n_layers
    for l in reversed(range(n_layers)):
        dx0, du0, layer_g[l] = _layer_bwd(dx0, du0, layer_w[l], saved[l], tabs)
    grads = {n: jnp.stack([g[n] for g in layer_g]) for n in layer_g[0]}
    grad_x, dg1_first = _rms_bwd(x0, pre_mix_g[0], dx0, du0)

    big_grads = _unlayout_grads({n: grads[n] for n in BIG_NAMES})
    fold = lambda a: a.sum(axis=1)
    dgq = fold(grads["q_norm_g"]).reshape(n_layers, 2, HEAD_DIM).sum(axis=1)
    dgk = fold(grads["k_norm_g"]).reshape(n_layers, 2, HEAD_DIM).sum(axis=1)
    dg1 = jnp.concatenate([fold(dg1_first[None]), fold(grads["next_pre_mix_g"])[:-1]], axis=0)
    small_grads = {
        "b_gate": fold(grads["b_gate"]), "q_norm_g": dgq, "k_norm_g": dgk, "q_a_norm_g": fold(grads["q_a_norm_g"]),
        "kv_a_norm_g": fold(grads["kv_a_norm_g"]), "pre_mix_g": dg1, "post_mix_g": fold(grads["post_mix_g"]),
        "pre_ffn_g": fold(grads["pre_ffn_g"]), "post_ffn_g": fold(grads["post_ffn_g"]),
    }
    small_packed = _pack_small([small_grads[n] for n in SMALL_NAMES])
    sends = [_shards_of(big_grads[n], SHARD_AXIS[n]).reshape((N_DEV,) + _rows2d(weights[n]).shape)
             for n in BIG_NAMES]
    sends.append(jnp.broadcast_to(small_packed[None], (N_DEV,) + small_packed.shape))
    halves = _pair_exchange(sends)
    core = lax.axis_index("c").astype(jnp.int32).reshape(1)
    recvs = _chip_exchange([_pair_add(s, h, core) for s, h in zip(sends, halves)])

    results = {}
    for n, recv in zip(BIG_NAMES, recvs):
        res = _adamw(recv, _rows2d(weights[n]), _rows2d(mom_m[n]), _rows2d(mom_v[n]))
        results[n] = [r.reshape(shard_shapes[n]) for r in res]
    res = _adamw(recvs[-1], *[_pack_small([d[n] for n in SMALL_NAMES]) for d in (weights, mom_m, mom_v)])
    for kind, packed_out in enumerate(res):
        for n, val in zip(SMALL_NAMES, _unpack_small(packed_out, small_shapes)):
            results.setdefault(n, [None] * 4)[kind] = val
    outs = [results[n][kind] for kind in range(4) for n in WEIGHT_NAMES]
    return (loss, grad_x.reshape(x.shape), *outs)
```

```python
import functools
import math

import jax
import jax.numpy as jnp
import numpy as np
from jax import lax
from jax.experimental import pallas as pl
from jax.experimental.pallas import tpu as pltpu

F32 = jnp.float32
BF16 = jnp.bfloat16

D_MODEL = 1024
GRID_W = 64
ROPE_THETA = 10000.0
EPS = 1e-6
GQA_HEADS = 8
GQA_KV_HEADS = 2
GQA_GROUP = GQA_HEADS // GQA_KV_HEADS
HEAD_DIM = 64
MLA_HEADS = 8
MLA_ROPE_DIM = 32
MLA_QK_DIM = 96
MLA_Q_RANK = 384
MLA_KV_RANK = 256
D_FF = 4 * D_MODEL
GQA_SCALE = 1.0 / math.sqrt(HEAD_DIM)
MLA_SCALE = 1.0 / math.sqrt(MLA_QK_DIM)
LOG2E = math.log2(math.e)
LN2 = math.log(2.0)

ADAM_LR = 0.001
ADAM_B1 = 0.9
ADAM_B2 = 0.999
ADAM_EPS = 1e-08
ADAM_WD = 0.01
ADAM_STEP = 10

N_DEV = 8
LANES = 128
SUBLANES = 8
VMEM_LIMIT = 48 * 1024 * 1024

Z_QA, Z_KA, Z_VA, Z_CQ, Z_CKV, Z_KR, Z_GATE = 0, 512, 640, 768, 1152, 1408, 1536
Z_ATT_W = 1536
Z_W = 3584
KR_LANE0 = 64

WEIGHT_NAMES = ("w_in", "b_gate", "q_norm_g", "k_norm_g", "q_a_norm_g", "kv_a_norm_g", "w_q_up", "w_kv_up",
                "w_branch_a", "w_branch_b", "w_o", "w_ffn_up", "w_ffn_down", "pre_mix_g", "post_mix_g",
                "pre_ffn_g", "post_ffn_g")
SHARD_AXIS = {"w_in": 2, "w_q_up": 2, "w_kv_up": 2, "w_branch_a": 2, "w_branch_b": 2, "w_o": 1, "w_ffn_up": 2,
              "w_ffn_down": 1}
BIG_NAMES = tuple(n for n in WEIGHT_NAMES if n in SHARD_AXIS)
SMALL_NAMES = tuple(n for n in WEIGHT_NAMES if n not in SHARD_AXIS)
ADAM_BLOCK_ELEMS = 256 * 1024
MM_TILE = 1024
MM_TILE_K = 2048
ATTN_TQ = 1024
ATTN_TK = 1024


def _params(*semantics):
    return pltpu.CompilerParams(dimension_semantics=semantics, vmem_limit_bytes=VMEM_LIMIT)


def _tile(n, pref):
    if n <= pref:
        return n
    t = (pref // LANES) * LANES
    while n % t:
        t -= LANES
    return t


def _fold8(t):
    return t.reshape(t.shape[0] // SUBLANES, SUBLANES, t.shape[1]).sum(axis=0)


_DIMS = {"nn": ((1,), (0,)), "nt": ((1,), (1,)), "tn": ((0,), (0,))}


def _matmul(a, b, mode, name, post=None, h=None):
    out_dt = F32 if mode == "tn" else BF16
    if mode == "nn":
        (m, k), n = a.shape, b.shape[1]
    elif mode == "nt":
        (m, k), n = a.shape, b.shape[0]
    else:
        (k, m), n = a.shape, b.shape[1]
    tm, tn, tk = _tile(m, MM_TILE), _tile(n, MM_TILE), _tile(k, MM_TILE_K)
    nk = k // tk
    dims = (_DIMS[mode], ((), ()))
    n_in = 3 if post == "relu2_bwd" else 2
    n_out = 2 if post == "relu2" else 1

    def body(*refs):
        a_ref, b_ref = refs[:2]
        o_refs, acc_ref = refs[n_in:n_in + n_out], refs[-1]

        def finish(val):
            if post == "relu2":
                o_refs[0][...] = val.astype(out_dt)
                r = jnp.maximum(val, 0.0)
                o_refs[1][...] = (r * r).astype(BF16)
            elif post == "relu2_bwd":
                o_refs[0][...] = (val * (2.0 * jnp.maximum(refs[2][...].astype(F32), 0.0))).astype(BF16)
            else:
                o_refs[0][...] = val.astype(out_dt)

        prod = lax.dot_general(a_ref[...], b_ref[...], dims, preferred_element_type=F32)
        if nk == 1:
            finish(prod)
        else:
            kk = pl.program_id(2)

            @pl.when(kk == 0)
            def _():
                acc_ref[...] = prod

            @pl.when(kk > 0)
            def _():
                acc_ref[...] += prod

            @pl.when(kk == nk - 1)
            def _():
                finish(acc_ref[...])

    if mode == "tn":
        a_spec = pl.BlockSpec((tk, tm), lambda i, j, kk: (kk, i))
    else:
        a_spec = pl.BlockSpec((tm, tk), lambda i, j, kk: (i, kk))
    if mode == "nt":
        b_spec = pl.BlockSpec((tn, tk), lambda i, j, kk: (j, kk))
    else:
        b_spec = pl.BlockSpec((tk, tn), lambda i, j, kk: (kk, j))
    o_spec = pl.BlockSpec((tm, tn), lambda i, j, kk: (i, j))
    main_out, bf16_out = jax.ShapeDtypeStruct((m, n), out_dt), jax.ShapeDtypeStruct((m, n), BF16)
    out_shape = {None: main_out, "relu2": [main_out, bf16_out], "relu2_bwd": bf16_out}[post]
    return pl.pallas_call(
        body,
        name=name,
        grid=(m // tm, n // tn, nk),
        in_specs=[a_spec, b_spec] + ([o_spec] if post == "relu2_bwd" else []),
        out_specs=[o_spec, o_spec] if post == "relu2" else o_spec,
        out_shape=out_shape,
        scratch_shapes=[pltpu.VMEM((tm, tn), F32)],
        compiler_params=_params("parallel", "parallel", "arbitrary"),
    )(*((a, b, h) if post == "relu2_bwd" else (a, b)))


def _rinv(x):
    return lax.rsqrt(jnp.mean(x * x, axis=-1, keepdims=True) + EPS)


def _rms_bwd_rows(x, g, dy):
    r = _rinv(x)
    xh = x * r
    dxh = dy * g
    dx = r * (dxh - xh * jnp.mean(dxh * xh, axis=-1, keepdims=True))
    return dx, dy * xh


def _row_spec(tm, c):
    return pl.BlockSpec((tm, c), lambda i: (i, 0))


def _vec_spec(c):
    return pl.BlockSpec((1, c), lambda i: (0, 0))


def _acc_spec(c):
    return pl.BlockSpec((SUBLANES, c), lambda i: (0, 0))


def _rms_fwd(x, g):
    t, d = x.shape
    tm = _tile(t, 512)

    def body(x_ref, g_ref, o_ref):
        xv = x_ref[...]
        o_ref[...] = (xv * _rinv(xv) * g_ref[...]).astype(BF16)

    return pl.pallas_call(
        body, name="rms_fwd", grid=(t // tm,),
        in_specs=[_row_spec(tm, d), _vec_spec(d)], out_specs=_row_spec(tm, d),
        out_shape=jax.ShapeDtypeStruct((t, d), BF16), compiler_params=_params("parallel"),
    )(x, g.reshape(1, d))


def _rms_bwd(x, g, dres, dy):
    t, d = x.shape
    tm = _tile(t, 512)

    def body(x_ref, g_ref, dres_ref, dy_ref, dx_ref, dg_ref):
        dx, dgc = _rms_bwd_rows(x_ref[...], g_ref[...], dy_ref[...].astype(F32))
        dx_ref[...] = dres_ref[...] + dx

        @pl.when(pl.program_id(0) == 0)
        def _():
            dg_ref[...] = jnp.zeros_like(dg_ref)

        dg_ref[...] += _fold8(dgc)

    return pl.pallas_call(
        body, name="rms_bwd", grid=(t // tm,),
        in_specs=[_row_spec(tm, d), _vec_spec(d), _row_spec(tm, d), _row_spec(tm, d)],
        out_specs=[_row_spec(tm, d), _acc_spec(d)],
        out_shape=[jax.ShapeDtypeStruct((t, d), F32), jax.ShapeDtypeStruct((SUBLANES, d), F32)],
        compiler_params=_params("arbitrary"),
    )(x, g.reshape(1, d), dres, dy)


def _res_norm_fwd(x, m, g_post, g_next):
    t, d = x.shape
    tm = _tile(t, 512)

    def body(x_ref, m_ref, gp_ref, gn_ref, x2_ref, u2_ref):
        mv = m_ref[...].astype(F32)
        x2 = x_ref[...] + mv * _rinv(mv) * gp_ref[...]
        x2_ref[...] = x2
        u2_ref[...] = (x2 * _rinv(x2) * gn_ref[...]).astype(BF16)

    return pl.pallas_call(
        body, name="res_norm_fwd", grid=(t // tm,),
        in_specs=[_row_spec(tm, d), _row_spec(tm, d), _vec_spec(d), _vec_spec(d)],
        out_specs=[_row_spec(tm, d), _row_spec(tm, d)],
        out_shape=[jax.ShapeDtypeStruct((t, d), F32), jax.ShapeDtypeStruct((t, d), BF16)],
        compiler_params=_params("parallel"),
    )(x, m, g_post.reshape(1, d), g_next.reshape(1, d))


def _res_norm_bwd(x2, m, g_post, g_next, dx2_in, du2):
    t, d = x2.shape
    tm = _tile(t, 512)

    def body(x2_ref, m_ref, gp_ref, gn_ref, dx2in_ref, du2_ref, dx2_ref, dm_ref, dgp_ref, dgn_ref):
        dxn, dgn_c = _rms_bwd_rows(x2_ref[...], gn_ref[...], du2_ref[...].astype(F32))
        dx2 = dx2in_ref[...] + dxn
        dx2_ref[...] = dx2
        dm, dgp_c = _rms_bwd_rows(m_ref[...].astype(F32), gp_ref[...], dx2)
        dm_ref[...] = dm.astype(BF16)

        @pl.when(pl.program_id(0) == 0)
        def _():
            dgp_ref[...] = jnp.zeros_like(dgp_ref)
            dgn_ref[...] = jnp.zeros_like(dgn_ref)

        dgp_ref[...] += _fold8(dgp_c)
        dgn_ref[...] += _fold8(dgn_c)

    return pl.pallas_call(
        body, name="res_norm_bwd", grid=(t // tm,),
        in_specs=[_row_spec(tm, d), _row_spec(tm, d), _vec_spec(d), _vec_spec(d), _row_spec(tm, d), _row_spec(tm, d)],
        out_specs=[_row_spec(tm, d), _row_spec(tm, d), _acc_spec(d), _acc_spec(d)],
        out_shape=[jax.ShapeDtypeStruct((t, d), F32), jax.ShapeDtypeStruct((t, d), BF16),
                   jax.ShapeDtypeStruct((SUBLANES, d), F32), jax.ShapeDtypeStruct((SUBLANES, d), F32)],
        compiler_params=_params("arbitrary"),
    )(x2, m, g_post.reshape(1, d), g_next.reshape(1, d), dx2_in, du2)


def _rope_tables(t):
    rows = t // GRID_W
    row = jnp.repeat(jnp.arange(rows, dtype=F32), GRID_W)
    col = jnp.tile(jnp.arange(GRID_W, dtype=F32), rows)

    def tab(rot_dim):
        half = rot_dim // 2
        inv = ROPE_THETA ** (-jnp.arange(0, half, 2, dtype=F32) / half)
        ar = row[:, None] * inv[None, :]
        ac = col[:, None] * inv[None, :]
        ang = jnp.concatenate([ar, ar, ac, ac], axis=-1)
        q = half // 2
        sign = np.tile(np.concatenate([-np.ones(q, np.float32), np.ones(q, np.float32)]), 2)
        return jnp.cos(ang), jnp.sin(ang) * sign[None, :]

    ca, sa = tab(HEAD_DIM)
    cb, sb = tab(MLA_ROPE_DIM)
    one = jnp.ones((t, 1), F32)
    cos_b = jnp.concatenate([one * jnp.ones((1, KR_LANE0), F32), cb, one * jnp.ones((1, 32), F32)], axis=-1)
    sin_b = jnp.concatenate([jnp.zeros((t, KR_LANE0), F32), sb, jnp.zeros((t, 32), F32)], axis=-1)
    return jnp.tile(ca, (1, GQA_HEADS)), jnp.tile(sa, (1, GQA_HEADS)), cos_b, sin_b


def _swap_halves(x, sh):
    lane = lax.broadcasted_iota(jnp.int32, x.shape, 1)
    up = pltpu.roll(x, LANES - sh, 1)
    dn = pltpu.roll(x, sh, 1)
    return jnp.where((lane & (2 * sh - 1)) < sh, up, dn)


def _rope(x, cos, sin_s, sh):
    return x * cos + _swap_halves(x, sh) * sin_s


def _rope_bwd(dy, cos, sin_s, sh):
    return dy * cos + _swap_halves(dy * sin_s, sh)


def _lo_mask(shape):
    return lax.broadcasted_iota(jnp.int32, shape, 1) < HEAD_DIM


def _half_mean(t, lo):
    s_lo = jnp.sum(jnp.where(lo, t, 0.0), axis=-1, keepdims=True)
    s_hi = jnp.sum(jnp.where(lo, 0.0, t), axis=-1, keepdims=True)
    return jnp.where(lo, s_lo, s_hi) * (1.0 / HEAD_DIM)


def _head_norm(x, g2):
    lo = _lo_mask(x.shape)
    r = lax.rsqrt(_half_mean(x * x, lo) + EPS)
    return x * r * g2


def _head_norm_bwd(x, g2, dy):
    lo = _lo_mask(x.shape)
    r = lax.rsqrt(_half_mean(x * x, lo) + EPS)
    xh = x * r
    dxh = dy * g2
    dx = r * (dxh - xh * _half_mean(dxh * xh, lo))
    return dx, dy * xh


def _prep_a_fwd(z, gq2, gk2, gqa, gkva, cos_a, sin_a, cos_b, sin_b):
    t = z.shape[0]
    tm = _tile(t, 256)

    def body(z_ref, gq_ref, gk_ref, gqa_ref, gkva_ref, ca_ref, sa_ref, cb_ref, sb_ref,
             qa_ref, ka_ref, va_ref, cqn_ref, ckvn_ref, krr_ref):
        def zf(lo, hi):
            return z_ref[:, lo:hi].astype(F32)

        for j in range(4):
            cols = slice(LANES * j, LANES * (j + 1))
            y = _rope(_head_norm(zf(LANES * j, LANES * (j + 1)), gq_ref[...]), ca_ref[:, cols], sa_ref[:, cols], 16)
            qa_ref[:, cols] = (y * (GQA_SCALE * LOG2E)).astype(BF16)
        y = _rope(_head_norm(zf(Z_KA, Z_VA), gk_ref[...]), ca_ref[:, :LANES], sa_ref[:, :LANES], 16)
        ka_ref[...] = y.astype(BF16)
        va_ref[...] = z_ref[:, Z_VA:Z_CQ].astype(BF16)
        cq = zf(Z_CQ, Z_CKV)
        cqn_ref[...] = (cq * _rinv(cq) * gqa_ref[...]).astype(BF16)
        ckv = zf(Z_CKV, Z_KR)
        ckvn_ref[...] = (ckv * _rinv(ckv) * gkva_ref[...]).astype(BF16)
        krr_ref[...] = _rope(zf(Z_KR, Z_GATE), cb_ref[...], sb_ref[...], 8)

    return pl.pallas_call(
        body, name="prep_a_fwd", grid=(t // tm,),
        in_specs=[_row_spec(tm, Z_ATT_W), _vec_spec(LANES), _vec_spec(LANES), _vec_spec(MLA_Q_RANK),
                  _vec_spec(MLA_KV_RANK), _row_spec(tm, 512), _row_spec(tm, 512), _row_spec(tm, LANES),
                  _row_spec(tm, LANES)],
        out_specs=[_row_spec(tm, 512), _row_spec(tm, LANES), _row_spec(tm, LANES), _row_spec(tm, MLA_Q_RANK),
                   _row_spec(tm, MLA_KV_RANK), _row_spec(tm, LANES)],
        out_shape=[jax.ShapeDtypeStruct((t, 512), BF16), jax.ShapeDtypeStruct((t, LANES), BF16),
                   jax.ShapeDtypeStruct((t, LANES), BF16), jax.ShapeDtypeStruct((t, MLA_Q_RANK), BF16),
                   jax.ShapeDtypeStruct((t, MLA_KV_RANK), BF16), jax.ShapeDtypeStruct((t, LANES), F32)],
        compiler_params=_params("parallel"),
    )(z, gq2, gk2, gqa, gkva, cos_a, sin_a, cos_b, sin_b)


def _prep_a_bwd(z, dqa, dka4, dva4, dcqn, dckvn, dkr, dzga, dzgb, gq2, gk2, gqa, gkva, cos_a, sin_a):
    t = z.shape[0]
    tm = _tile(t, 256)

    def body(z_ref, dqa_ref, dka_ref, dva_ref, dcqn_ref, dckvn_ref, dkr_ref, dzga_ref, dzgb_ref, gq_ref, gk_ref,
             gqa_ref, gkva_ref, ca_ref, sa_ref, dz_ref, dgq_ref, dgk_ref, dgqa_ref, dgkva_ref):
        @pl.when(pl.program_id(0) == 0)
        def _():
            dgq_ref[...] = jnp.zeros_like(dgq_ref)
            dgk_ref[...] = jnp.zeros_like(dgk_ref)
            dgqa_ref[...] = jnp.zeros_like(dgqa_ref)
            dgkva_ref[...] = jnp.zeros_like(dgkva_ref)

        def zf(lo, hi):
            return z_ref[:, lo:hi].astype(F32)

        dgq = jnp.zeros((SUBLANES, LANES), F32)
        for j in range(4):
            cols = slice(LANES * j, LANES * (j + 1))
            dy = _rope_bwd(dqa_ref[:, cols] * GQA_SCALE, ca_ref[:, cols], sa_ref[:, cols], 16)
            dx, dgc = _head_norm_bwd(zf(LANES * j, LANES * (j + 1)), gq_ref[...], dy)
            dz_ref[:, cols] = dx.astype(BF16)
            dgq = dgq + _fold8(dgc)
        dgq_ref[...] += dgq
        dk = (dka_ref[0] + dka_ref[1] + dka_ref[2] + dka_ref[3]).T * LN2
        dy = _rope_bwd(dk, ca_ref[:, :LANES], sa_ref[:, :LANES], 16)
        dx, dgc = _head_norm_bwd(zf(Z_KA, Z_VA), gk_ref[...], dy)
        dz_ref[:, Z_KA:Z_VA] = dx.astype(BF16)
        dgk_ref[...] += _fold8(dgc)
        dz_ref[:, Z_VA:Z_CQ] = (dva_ref[0] + dva_ref[1] + dva_ref[2] + dva_ref[3]).T.astype(BF16)
        dx, dgc = _rms_bwd_rows(zf(Z_CQ, Z_CKV), gqa_ref[...], dcqn_ref[...].astype(F32))
        dz_ref[:, Z_CQ:Z_CKV] = dx.astype(BF16)
        dgqa_ref[...] += _fold8(dgc)
        dx, dgc = _rms_bwd_rows(zf(Z_CKV, Z_KR), gkva_ref[...], dckvn_ref[...].astype(F32))
        dz_ref[:, Z_CKV:Z_KR] = dx.astype(BF16)
        dgkva_ref[...] += _fold8(dgc)
        dz_ref[:, Z_KR:Z_GATE] = dkr_ref[...].astype(BF16)
        dz_ref[:, Z_GATE:Z_GATE + D_MODEL] = dzga_ref[...]
        dz_ref[:, Z_GATE + D_MODEL:Z_W] = dzgb_ref[...]

    part = pl.BlockSpec((4, LANES, tm), lambda i: (0, 0, i))
    return pl.pallas_call(
        body, name="prep_a_bwd", grid=(t // tm,),
        in_specs=[_row_spec(tm, Z_ATT_W), _row_spec(tm, 512), part, part, _row_spec(tm, MLA_Q_RANK),
                  _row_spec(tm, MLA_KV_RANK), _row_spec(tm, LANES), _row_spec(tm, D_MODEL), _row_spec(tm, D_MODEL),
                  _vec_spec(LANES),
                  _vec_spec(LANES), _vec_spec(MLA_Q_RANK), _vec_spec(MLA_KV_RANK), _row_spec(tm, 512),
                  _row_spec(tm, 512)],
        out_specs=[_row_spec(tm, Z_W), _acc_spec(LANES), _acc_spec(LANES), _acc_spec(MLA_Q_RANK),
                   _acc_spec(MLA_KV_RANK)],
        out_shape=[jax.ShapeDtypeStruct((t, Z_W), BF16), jax.ShapeDtypeStruct((SUBLANES, LANES), F32),
                   jax.ShapeDtypeStruct((SUBLANES, LANES), F32), jax.ShapeDtypeStruct((SUBLANES, MLA_Q_RANK), F32),
                   jax.ShapeDtypeStruct((SUBLANES, MLA_KV_RANK), F32)],
        compiler_params=_params("arbitrary"),
    )(z, dqa, dka4, dva4, dcqn, dckvn, dkr, dzga, dzgb, gq2, gk2, gqa, gkva, cos_a, sin_a)


def _prep_b_fwd(qb, kvb, krr, cos_b, sin_b):
    t = qb.shape[0]
    tm = _tile(t, 256)

    def body(qb_ref, kvb_ref, krr_ref, cb_ref, sb_ref, q_ref, k_ref, v_ref):
        for h in range(MLA_HEADS):
            cols = slice(LANES * h, LANES * (h + 1))
            qh = _rope(qb_ref[:, cols].astype(F32), cb_ref[...], sb_ref[...], 8)
            q_ref[:, cols] = (qh * (MLA_SCALE * LOG2E)).astype(BF16)
            k_ref[:, cols] = (kvb_ref[:, cols].astype(F32) + krr_ref[...]).astype(BF16)
        v_ref[...] = kvb_ref[:, 1024:1536].astype(BF16)

    return pl.pallas_call(
        body, name="prep_b_fwd", grid=(t // tm,),
        in_specs=[_row_spec(tm, 1024), _row_spec(tm, 1536), _row_spec(tm, LANES), _row_spec(tm, LANES),
                  _row_spec(tm, LANES)],
        out_specs=[_row_spec(tm, 1024), _row_spec(tm, 1024), _row_spec(tm, 512)],
        out_shape=[jax.ShapeDtypeStruct((t, 1024), BF16), jax.ShapeDtypeStruct((t, 1024), BF16),
                   jax.ShapeDtypeStruct((t, 512), BF16)],
        compiler_params=_params("parallel"),
    )(qb, kvb, krr, cos_b, sin_b)


def _prep_b_bwd(dq, dk, dv, cos_b, sin_b):
    t = dq.shape[0]
    tm = _tile(t, 256)

    def body(dq_ref, dk_ref, dv_ref, cb_ref, sb_ref, dqb_ref, dkvb_ref, dkr_ref):
        dkr = jnp.zeros((tm, LANES), F32)
        for h in range(MLA_HEADS):
            cols = slice(LANES * h, LANES * (h + 1))
            dqb_ref[:, cols] = _rope_bwd(dq_ref[:, cols] * MLA_SCALE, cb_ref[...], sb_ref[...], 8).astype(BF16)
            dkh = dk_ref[cols, :].T * LN2
            dkvb_ref[:, cols] = dkh.astype(BF16)
            dkr = dkr + dkh
        for j in range(MLA_HEADS // 2):
            dkvb_ref[:, 1024 + LANES * j:1024 + LANES * (j + 1)] = dv_ref[LANES * j:LANES * (j + 1), :].T.astype(BF16)
        dkr_ref[...] = _rope_bwd(dkr, cb_ref[...], sb_ref[...], 8)

    return pl.pallas_call(
        body, name="prep_b_bwd", grid=(t // tm,),
        in_specs=[_row_spec(tm, 1024), pl.BlockSpec((1024, tm), lambda i: (0, i)),
                  pl.BlockSpec((512, tm), lambda i: (0, i)), _row_spec(tm, LANES),
                  _row_spec(tm, LANES)],
        out_specs=[_row_spec(tm, 1024), _row_spec(tm, 1536), _row_spec(tm, LANES)],
        out_shape=[jax.ShapeDtypeStruct((t, 1024), BF16), jax.ShapeDtypeStruct((t, 1536), BF16),
                   jax.ShapeDtypeStruct((t, LANES), F32)],
        compiler_params=_params("parallel"),
    )(dq, dk, dv, cos_b, sin_b)


_NT = (((1,), (1,)), ((), ()))
_NN = (((1,), (0,)), ((), ()))
_TN = (((0,), (0,)), ((), ()))


def _head_operands(qv, kv, i, shared_k):
    if shared_k:
        lo = _lo_mask(qv.shape)
        keep = lo if i == 0 else jnp.logical_not(lo)
        return jnp.where(keep, qv, jnp.zeros_like(qv)), kv
    cols = slice(LANES * i, LANES * (i + 1))
    return qv[:, cols], kv[:, cols]


def _attn_specs(shared_k, tq, tk, q_of, k_of):
    wq = LANES if shared_k else 2 * LANES
    q_spec = pl.BlockSpec((tq, wq), lambda *g: (q_of(*g), g[0]))
    if shared_k:
        k_spec = pl.BlockSpec((tk, LANES), lambda *g: (k_of(*g), 0))
        v_spec = pl.BlockSpec((tk, LANES), lambda *g: (k_of(*g), 0))
    else:
        k_spec = pl.BlockSpec((tk, wq), lambda *g: (k_of(*g), g[0]))
        v_spec = pl.BlockSpec((tk, LANES), lambda *g: (k_of(*g), g[0]))
    return wq, q_spec, k_spec, v_spec


def _attn_fwd(q, k, v, shared_k, name):
    t = q.shape[0]
    tq, tk = _tile(t, ATTN_TQ), _tile(t, ATTN_TK)
    nq, nk = t // tq, t // tk
    wq, q_spec, k_spec, v_spec = _attn_specs(shared_k, tq, tk, lambda p, i, j: i, lambda p, i, j: j)
    groups = q.shape[1] // wq
    chunk = _tile(tq, 2 * LANES)

    def body(q_ref, k_ref, v_ref, o_ref, lse_ref, m_s, l_s, acc_s, alpha_s, s_s, p_s):
        kb = pl.program_id(2)

        @pl.when(kb == 0)
        def _():
            m_s[...] = jnp.full_like(m_s, -jnp.inf)
            l_s[...] = jnp.zeros_like(l_s)
            acc_s[...] = jnp.zeros_like(acc_s)

        qv, kv, vv = q_ref[...], k_ref[...], v_ref[...]
        for i in range(2):
            qi, ki = _head_operands(qv, kv, i, shared_k)
            s_s[i] = lax.dot_general(ki, qi, _NT, preferred_element_type=F32)
        for i in range(2):
            for c in range(tq // chunk):
                cols = slice(c * chunk, (c + 1) * chunk)
                m_prev = m_s[i, :, cols]
                m_new = jnp.maximum(m_prev, jnp.max(s_s[i, :, cols], axis=0, keepdims=True))
                alpha = jnp.exp2(m_prev - m_new)
                pt = jnp.exp2(s_s[i, :, cols] - m_new)
                l_s[i, :, cols] = alpha * l_s[i, :, cols] + jnp.sum(pt, axis=0, keepdims=True)
                m_s[i, :, cols] = m_new
                alpha_s[i, :, cols] = alpha
                p_s[i, :, cols] = pt.astype(BF16)
        for i in range(2):
            acc_s[i] = alpha_s[i] * acc_s[i] + lax.dot_general(vv, p_s[i], _TN, preferred_element_type=F32)

        @pl.when(kb == nk - 1)
        def _():
            o0 = acc_s[0] / l_s[0]
            o1 = acc_s[1] / l_s[1]
            row_lo = lax.broadcasted_iota(jnp.int32, o0.shape, 0) < HEAD_DIM
            o_ref[...] = jnp.where(row_lo, o0, o1).T.astype(BF16)
            lse_ref[0] = m_s[0] + jnp.log2(l_s[0])
            lse_ref[1] = m_s[1] + jnp.log2(l_s[1])

    return pl.pallas_call(
        body, name=name, grid=(groups, nq, nk),
        in_specs=[q_spec, k_spec, v_spec],
        out_specs=[pl.BlockSpec((tq, LANES), lambda p, i, j: (i, p)),
                   pl.BlockSpec((2, 1, tq), lambda p, i, j: (p, 0, i))],
        out_shape=[jax.ShapeDtypeStruct((t, LANES * groups), BF16),
                   jax.ShapeDtypeStruct((2 * groups, 1, t), F32)],
        scratch_shapes=[pltpu.VMEM((2, 1, tq), F32), pltpu.VMEM((2, 1, tq), F32), pltpu.VMEM((2, LANES, tq), F32),
                        pltpu.VMEM((2, 1, tq), F32), pltpu.VMEM((2, tk, tq), F32), pltpu.VMEM((2, tk, tq), BF16)],
        compiler_params=_params("parallel", "parallel", "arbitrary"),
    )(q, k, v)


def _attn_stats(do, o, lse):
    t, w = do.shape
    tm = _tile(t, 512)
    groups = w // LANES

    def body(do_ref, o_ref, lse_ref, delta_ref, lser_ref):
        prod = do_ref[...].astype(F32) * o_ref[...].astype(F32)
        for g in range(groups):
            x = prod[:, LANES * g:LANES * (g + 1)]
            lo = _lo_mask(x.shape)
            d0 = jnp.sum(jnp.where(lo, x, 0.0), axis=-1, keepdims=True)
            d1 = jnp.sum(jnp.where(lo, 0.0, x), axis=-1, keepdims=True)
            delta_ref[2 * g] = jnp.broadcast_to(d0, (tm, LANES))
            delta_ref[2 * g + 1] = jnp.broadcast_to(d1, (tm, LANES))
        for h in range(2 * groups):
            lser_ref[h] = jnp.broadcast_to(lse_ref[h], (LANES, tm)).T

    rep_spec = pl.BlockSpec((2 * groups, tm, LANES), lambda i: (0, i, 0))
    rep_shape = jax.ShapeDtypeStruct((2 * groups, t, LANES), F32)
    return pl.pallas_call(
        body, name="attn_stats", grid=(t // tm,),
        in_specs=[_row_spec(tm, w), _row_spec(tm, w), pl.BlockSpec((2 * groups, 1, tm), lambda i: (0, 0, i))],
        out_specs=[rep_spec, rep_spec],
        out_shape=[rep_shape, rep_shape],
        compiler_params=_params("parallel"),
    )(do, o, lse)


def _attn_bwd(q, k, v, do, lse, delta, shared_k, name):
    t = q.shape[0]
    tq, tk = _tile(t, ATTN_TQ), _tile(t, ATTN_TK)
    nq, nk = t // tq, t // tk
    wq, q_spec, k_spec, v_spec = _attn_specs(shared_k, tq, tk, lambda p, j, i: i, lambda p, j, i: j)
    groups = q.shape[1] // wq

    def body(q_ref, k_ref, v_ref, do_ref, lse_ref, delta_ref, dq_ref, dk_ref, dv_ref, dk_s, dv_s, s_s, dp_s, p_s,
             ds_s):
        kb, qb = pl.program_id(1), pl.program_id(2)

        @pl.when(qb == 0)
        def _():
            dk_s[...] = jnp.zeros_like(dk_s)
            dv_s[...] = jnp.zeros_like(dv_s)

        qv, kv, vv, dov = q_ref[...], k_ref[...], v_ref[...], do_ref[...]
        lo = _lo_mask(dov.shape)
        heads = []
        for i in range(2):
            qi, ki = _head_operands(qv, kv, i, shared_k)
            keep = lo if i == 0 else jnp.logical_not(lo)
            doi = jnp.where(keep, dov, jnp.zeros_like(dov))
            heads.append((qi, ki, doi))
            s_s[i] = lax.dot_general(qi, ki, _NT, preferred_element_type=F32)
            dp_s[i] = lax.dot_general(doi, vv, _NT, preferred_element_type=F32)
        for i in range(2):
            lse_i, delta_i = lse_ref[i], delta_ref[i]
            for c in range(tk // LANES):
                cols = slice(c * LANES, (c + 1) * LANES)
                p = jnp.exp2(s_s[i, :, cols] - lse_i)
                p_s[i, :, cols] = p.astype(BF16)
                ds_s[i, :, cols] = (p * (dp_s[i, :, cols] - delta_i)).astype(BF16)
        dq_parts = []
        for i in range(2):
            qi, ki, doi = heads[i]
            dv_s[...] += lax.dot_general(doi, p_s[i], _TN, preferred_element_type=F32)
            dk_i = lax.dot_general(qi, ds_s[i], _TN, preferred_element_type=F32)
            if shared_k:
                dk_s[...] += dk_i
            else:
                dk_s[LANES * i:LANES * (i + 1), :] += dk_i
            dq_parts.append(lax.dot_general(ds_s[i], ki, _NN, preferred_element_type=F32))
        rows = pl.ds(pl.multiple_of(qb * tq, tq), tq)
        if shared_k:
            tiles = [(slice(0, LANES), jnp.where(lo, dq_parts[0], dq_parts[1]))]
        else:
            tiles = [(slice(0, LANES), dq_parts[0]), (slice(LANES, 2 * LANES), dq_parts[1])]
        for cols, val in tiles:
            @pl.when(kb == 0)
            def _(cols=cols, val=val):
                dq_ref[rows, cols] = val

            @pl.when(kb > 0)
            def _(cols=cols, val=val):
                dq_ref[rows, cols] += val

        @pl.when(qb == nq - 1)
        def _():
            if shared_k:
                dk_ref[0] = dk_s[...]
                dv_ref[0] = dv_s[...]
            else:
                dk_ref[...] = dk_s[...]
                dv_ref[...] = dv_s[...]

    stat_spec = pl.BlockSpec((2, tq, LANES), lambda p, j, i: (p, i, 0))
    do_spec = pl.BlockSpec((tq, LANES), lambda p, j, i: (i, p))
    dq_spec = pl.BlockSpec((t, wq), lambda p, j, i: (0, p))
    if shared_k:
        dk_spec = pl.BlockSpec((1, LANES, tk), lambda p, j, i: (p, 0, j))
        dv_spec = dk_spec
        dk_shape = jax.ShapeDtypeStruct((groups, LANES, t), F32)
        dv_shape = dk_shape
    else:
        dk_spec = pl.BlockSpec((wq, tk), lambda p, j, i: (p, j))
        dv_spec = pl.BlockSpec((LANES, tk), lambda p, j, i: (p, j))
        dk_shape = jax.ShapeDtypeStruct((wq * groups, t), F32)
        dv_shape = jax.ShapeDtypeStruct((LANES * groups, t), F32)
    return pl.pallas_call(
        body, name=name, grid=(groups, nk, nq),
        in_specs=[q_spec, k_spec, v_spec, do_spec, stat_spec, stat_spec],
        out_specs=[dq_spec, dk_spec, dv_spec],
        out_shape=[jax.ShapeDtypeStruct((t, wq * groups), F32), dk_shape, dv_shape],
        scratch_shapes=[pltpu.VMEM((wq, tk), F32), pltpu.VMEM((LANES, tk), F32), pltpu.VMEM((2, tq, tk), F32),
                        pltpu.VMEM((2, tq, tk), F32), pltpu.VMEM((2, tq, tk), BF16), pltpu.VMEM((2, tq, tk), BF16)],
        compiler_params=_params("parallel", "arbitrary", "arbitrary"),
    )(q, k, v, do, lse, delta)


_MERGE_W = 512
_GATE_BLK0 = Z_GATE // _MERGE_W


def _merge_fwd(z, b_gate, ta, tb):
    t = z.shape[0]
    tm = _tile(t, 512)
    w = _MERGE_W
    nj = D_MODEL // w

    def body(za_ref, zb_ref, ba_ref, bb_ref, ta_ref, tb_ref, o_ref):
        ga = jax.nn.sigmoid(za_ref[...].astype(F32) + ba_ref[...])
        gb = jax.nn.sigmoid(zb_ref[...].astype(F32) + bb_ref[...])
        o_ref[...] = (ga * ta_ref[...].astype(F32) + gb * tb_ref[...].astype(F32)).astype(BF16)

    return pl.pallas_call(
        body, name="merge_fwd", grid=(t // tm, nj),
        in_specs=[pl.BlockSpec((tm, w), lambda i, j: (i, _GATE_BLK0 + j)),
                  pl.BlockSpec((tm, w), lambda i, j: (i, _GATE_BLK0 + nj + j)),
                  pl.BlockSpec((1, w), lambda i, j: (0, j)),
                  pl.BlockSpec((1, w), lambda i, j: (0, nj + j)),
                  pl.BlockSpec((tm, w), lambda i, j: (i, j)),
                  pl.BlockSpec((tm, w), lambda i, j: (i, j))],
        out_specs=pl.BlockSpec((tm, w), lambda i, j: (i, j)),
        out_shape=jax.ShapeDtypeStruct((t, D_MODEL), BF16),
        compiler_params=_params("parallel", "parallel"),
    )(z, z, b_gate, b_gate, ta, tb)


def _merge_bwd(dmg, z, b_gate, ta, tb):
    t = z.shape[0]
    tm = _tile(t, 512)
    w = _MERGE_W
    nj = D_MODEL // w

    def body(dm_ref, za_ref, zb_ref, ba_ref, bb_ref, ta_ref, tb_ref, dta_ref, dtb_ref, dza_ref, dzb_ref,
             dba_ref, dbb_ref):
        dm = dm_ref[...].astype(F32)
        ga = jax.nn.sigmoid(za_ref[...].astype(F32) + ba_ref[...])
        gb = jax.nn.sigmoid(zb_ref[...].astype(F32) + bb_ref[...])
        dta_ref[...] = (dm * ga).astype(BF16)
        dtb_ref[...] = (dm * gb).astype(BF16)
        dza = dm * ta_ref[...].astype(F32) * ga * (1.0 - ga)
        dzb = dm * tb_ref[...].astype(F32) * gb * (1.0 - gb)
        dza_ref[...] = dza.astype(BF16)
        dzb_ref[...] = dzb.astype(BF16)

        @pl.when(pl.program_id(1) == 0)
        def _():
            dba_ref[...] = jnp.zeros_like(dba_ref)
            dbb_ref[...] = jnp.zeros_like(dbb_ref)

        dba_ref[...] += _fold8(dza)
        dbb_ref[...] += _fold8(dzb)

    blk = pl.BlockSpec((tm, w), lambda j, i: (i, j))
    acc = pl.BlockSpec((SUBLANES, w), lambda j, i: (0, j))
    return pl.pallas_call(
        body, name="merge_bwd", grid=(nj, t // tm),
        in_specs=[blk,
                  pl.BlockSpec((tm, w), lambda j, i: (i, _GATE_BLK0 + j)),
                  pl.BlockSpec((tm, w), lambda j, i: (i, _GATE_BLK0 + nj + j)),
                  pl.BlockSpec((1, w), lambda j, i: (0, j)),
                  pl.BlockSpec((1, w), lambda j, i: (0, nj + j)),
                  blk, blk],
        out_specs=[blk, blk, blk, blk, acc, acc],
        out_shape=[jax.ShapeDtypeStruct((t, D_MODEL), BF16)] * 4 + [jax.ShapeDtypeStruct((SUBLANES, D_MODEL), F32)] * 2,
        compiler_params=_params("parallel", "arbitrary"),
    )(dmg, z, z, b_gate, b_gate, ta, tb)


def _loss_grad(y, target):
    t, d = y.shape
    tm = _tile(t, 512)

    def body(y_ref, t_ref, dy_ref, acc_ref):
        err = y_ref[...] - t_ref[...]
        dy_ref[...] = err * (1.0 / d)
        e8 = _fold8(err * err)
        part = e8[:, 0:LANES]
        for c in range(1, d // LANES):
            part = part + e8[:, LANES * c:LANES * (c + 1)]

        @pl.when(pl.program_id(0) == 0)
        def _():
            acc_ref[...] = jnp.zeros_like(acc_ref)

        acc_ref[...] += part

    return pl.pallas_call(
        body, name="loss_grad", grid=(t // tm,),
        in_specs=[_row_spec(tm, d), _row_spec(tm, d)],
        out_specs=[_row_spec(tm, d), _acc_spec(LANES)],
        out_shape=[jax.ShapeDtypeStruct((t, d), F32), jax.ShapeDtypeStruct((SUBLANES, LANES), F32)],
        compiler_params=_params("arbitrary"),
    )(y, target)


_MESH_ID = pl.DeviceIdType.MESH
_ANY = pl.BlockSpec(memory_space=pl.ANY)


def _all_gather(arrays):
    n = len(arrays)

    def body(*refs):
        x_refs, out_refs = refs[:n], refs[n:2 * n]
        send_sems, recv_sems, local_sems = refs[2 * n:]
        mx, my, mc = lax.axis_index("x"), lax.axis_index("y"), lax.axis_index("c")
        me, sibling = (mx, my, mc), (mx, my, 1 - mc)
        chips = [(1 - mx, my), (mx, 1 - my), (1 - mx, 1 - my)]

        def slot(a, px, py, pc):
            return out_refs[a].at[4 * px + 2 * py + pc]

        def copy(a, sem, block, to, src=None):
            return pltpu.make_async_remote_copy(
                src_ref=slot(a, *block) if src is None else src, dst_ref=slot(a, *block),
                send_sem=send_sems.at[a, sem], recv_sem=recv_sems.at[a, sem], device_id=to, device_id_type=_MESH_ID)

        mine = [pltpu.make_async_copy(x_refs[a], slot(a, *me), local_sems.at[a]) for a in range(n)]
        first = []
        for a in range(n):
            mine[a].start()
            first.append(copy(a, 0, me, sibling, src=x_refs[a]))
            first += [copy(a, 1 + j, me, (*chip, mc), src=x_refs[a]) for j, chip in enumerate(chips)]
        for cp in first:
            cp.start()
        passed = []
        for a in range(n):
            for j, chip in enumerate(chips):
                copy(a, 1 + j, (*chip, mc), me).wait_recv()
                passed.append(copy(a, 4 + j, (*chip, mc), sibling))
                passed[-1].start()
        for a in range(n):
            copy(a, 0, sibling, me).wait_recv()
            for j, chip in enumerate(chips):
                copy(a, 4 + j, (*chip, 1 - mc), me).wait_recv()
        for cp in first + passed:
            cp.wait_send()
        for cp in mine:
            cp.wait()

    return pl.pallas_call(
        body, name="weight_all_gather",
        out_shape=[jax.ShapeDtypeStruct((N_DEV,) + a.shape, a.dtype) for a in arrays],
        in_specs=[_ANY] * n, out_specs=[_ANY] * n,
        scratch_shapes=[pltpu.SemaphoreType.DMA((n, 7)), pltpu.SemaphoreType.DMA((n, 7)),
                        pltpu.SemaphoreType.DMA((n,))],
    )(*arrays)


def _pair_exchange(sends):
    n = len(sends)

    def body(*refs):
        s_refs, r_refs = refs[:n], refs[n:2 * n]
        send_sems, recv_sems = refs[2 * n:]
        mx, my, mc = lax.axis_index("x"), lax.axis_index("y"), lax.axis_index("c")
        copies = []
        for a in range(n):
            for ch in range(4):
                cp = pltpu.make_async_remote_copy(
                    src_ref=s_refs[a].at[2 * ch + (1 - mc)], dst_ref=r_refs[a].at[ch], send_sem=send_sems.at[a, ch],
                    recv_sem=recv_sems.at[a, ch], device_id=(mx, my, 1 - mc), device_id_type=_MESH_ID)
                cp.start()
                copies.append(cp)
        for cp in copies:
            cp.wait_send()
            cp.wait_recv()

    return pl.pallas_call(
        body, name="grad_pair_exchange",
        out_shape=[jax.ShapeDtypeStruct((4,) + s.shape[1:], s.dtype) for s in sends],
        in_specs=[_ANY] * n, out_specs=[_ANY] * n,
        scratch_shapes=[pltpu.SemaphoreType.DMA((n, 4)), pltpu.SemaphoreType.DMA((n, 4))],
    )(*sends)


def _pair_add(send, half, core):
    _, r, c_ = send.shape
    tr = _row_tile(r, c_)

    def body(core_ref, s_ref, h_ref, o_ref):
        del core_ref
        o_ref[...] = (s_ref[...] + h_ref[...]).astype(BF16)

    blk = pl.BlockSpec((1, tr, c_), lambda ch, i, core_ref: (ch, i, 0))
    return pl.pallas_call(
        body, name="grad_pair_add",
        grid_spec=pltpu.PrefetchScalarGridSpec(
            num_scalar_prefetch=1, grid=(4, r // tr),
            in_specs=[pl.BlockSpec((1, tr, c_), lambda ch, i, core_ref: (2 * ch + core_ref[0], i, 0)), blk],
            out_specs=blk),
        out_shape=jax.ShapeDtypeStruct((4, r, c_), BF16),
        compiler_params=_params("parallel", "parallel"),
    )(core, send, half)


def _chip_exchange(parts):
    n = len(parts)

    def body(*refs):
        p_refs, r_refs = refs[:n], refs[n:2 * n]
        send_sems, recv_sems, local_sems = refs[2 * n:]
        mx, my, mc = lax.axis_index("x"), lax.axis_index("y"), lax.axis_index("c")
        mine = 2 * mx + my
        local = [pltpu.make_async_copy(p_refs[a].at[mine], r_refs[a].at[mine], local_sems.at[a]) for a in range(n)]
        copies = []
        for a in range(n):
            local[a].start()
            for rel in range(1, 4):
                px = 1 - mx if rel & 2 else mx
                py = 1 - my if rel & 1 else my
                cp = pltpu.make_async_remote_copy(
                    src_ref=p_refs[a].at[2 * px + py], dst_ref=r_refs[a].at[mine], send_sem=send_sems.at[a, rel - 1],
                    recv_sem=recv_sems.at[a, rel - 1], device_id=(px, py, mc), device_id_type=_MESH_ID)
                cp.start()
                copies.append(cp)
        for cp in copies:
            cp.wait_send()
            cp.wait_recv()
        for cp in local:
            cp.wait()

    return pl.pallas_call(
        body, name="grad_chip_exchange",
        out_shape=[jax.ShapeDtypeStruct(p.shape, p.dtype) for p in parts],
        in_specs=[_ANY] * n, out_specs=[_ANY] * n,
        scratch_shapes=[pltpu.SemaphoreType.DMA((n, 3)), pltpu.SemaphoreType.DMA((n, 3)),
                        pltpu.SemaphoreType.DMA((n,))],
    )(*parts)


def _row_tile(r, c_):
    tr = min(r, ADAM_BLOCK_ELEMS // (pl.cdiv(c_, LANES) * LANES))
    while r % tr:
        tr -= SUBLANES
    return tr


def _adamw(recv, w, m, v):
    r, c_ = w.shape
    tr = _row_tile(r, c_)
    n_src = recv.shape[0]

    def body(g_ref, w_ref, m_ref, v_ref, go_ref, d_ref, mo_ref, vo_ref):
        g = g_ref[0].astype(F32)
        for s in range(1, n_src):
            g = g + g_ref[s].astype(F32)
        go_ref[...] = g
        mn = ADAM_B1 * m_ref[...] + (1.0 - ADAM_B1) * g
        vn = ADAM_B2 * v_ref[...] + (1.0 - ADAM_B2) * (g * g)
        mo_ref[...] = mn
        vo_ref[...] = vn
        m_hat = mn / (1.0 - ADAM_B1 ** ADAM_STEP)
        v_hat = vn / (1.0 - ADAM_B2 ** ADAM_STEP)
        d_ref[...] = -ADAM_LR * (m_hat / (jnp.sqrt(v_hat) + ADAM_EPS) + ADAM_WD * w_ref[...])

    spec = pl.BlockSpec((tr, c_), lambda i: (i, 0))
    out = jax.ShapeDtypeStruct((r, c_), F32)
    return pl.pallas_call(
        body, name="grad_sum_adamw", grid=(r // tr,),
        in_specs=[pl.BlockSpec((n_src, tr, c_), lambda i: (0, i, 0)), spec, spec, spec],
        out_specs=[spec, spec, spec, spec], out_shape=[out, out, out, out],
        compiler_params=_params("parallel"),
    )(recv, w, m, v)


def _pad_cols(a, before, after):
    parts = []
    if before:
        parts.append(jnp.zeros(a.shape[:-1] + (before,), a.dtype))
    parts.append(a)
    if after:
        parts.append(jnp.zeros(a.shape[:-1] + (after,), a.dtype))
    return jnp.concatenate(parts, axis=-1)


def _q_head_pairs(a, axis):
    shp = a.shape
    a = a.reshape(shp[:axis] + (GQA_KV_HEADS, GQA_GROUP, HEAD_DIM) + shp[axis + 1:])
    a = jnp.swapaxes(a, axis, axis + 1)
    return a.reshape(shp)


def _q_head_unpairs(a, axis):
    shp = a.shape
    a = a.reshape(shp[:axis] + (GQA_GROUP, GQA_KV_HEADS, HEAD_DIM) + shp[axis + 1:])
    a = jnp.swapaxes(a, axis, axis + 1)
    return a.reshape(shp)


def _layout_weights(w):
    w_in = w["w_in"]
    lead = w_in.shape[:-1]
    w_in_p = jnp.concatenate([
        _q_head_pairs(w_in[..., 0:512], w_in.ndim - 1),
        w_in[..., 512:1408],
        _pad_cols(w_in[..., 1408:1440], KR_LANE0, LANES - KR_LANE0 - MLA_ROPE_DIM),
        w_in[..., 1440:],
    ], axis=-1)
    wq = w["w_q_up"]
    wq_p = _pad_cols(wq.reshape(wq.shape[:-1] + (MLA_HEADS, MLA_QK_DIM)), 0, LANES - MLA_QK_DIM)
    wq_p = wq_p.reshape(wq.shape[:-1] + (MLA_HEADS * LANES,))
    wkv = w["w_kv_up"]
    wkv4 = wkv.reshape(wkv.shape[:-1] + (MLA_HEADS, 2 * HEAD_DIM))
    wk_p = _pad_cols(wkv4[..., :HEAD_DIM], 0, LANES - HEAD_DIM).reshape(wkv.shape[:-1] + (MLA_HEADS * LANES,))
    wv_p = wkv4[..., HEAD_DIM:].reshape(wkv.shape[:-1] + (MLA_HEADS * HEAD_DIM,))
    del lead
    return {
        "w_in": w_in_p, "w_q_up": wq_p, "w_kv_up": jnp.concatenate([wk_p, wv_p], axis=-1),
        "w_branch_a": _q_head_pairs(w["w_branch_a"], w["w_branch_a"].ndim - 2), "w_branch_b": w["w_branch_b"],
        "w_o": w["w_o"], "w_ffn_up": w["w_ffn_up"], "w_ffn_down": w["w_ffn_down"],
    }


def _unlayout_grads(g):
    gi = g["w_in"]
    kr0 = Z_KR + KR_LANE0
    g_in = jnp.concatenate([
        _q_head_unpairs(gi[..., 0:512], gi.ndim - 1), gi[..., 512:1408], gi[..., kr0:kr0 + MLA_ROPE_DIM],
        gi[..., Z_GATE:],
    ], axis=-1)
    gq = g["w_q_up"]
    gq = gq.reshape(gq.shape[:-1] + (MLA_HEADS, LANES))[..., :MLA_QK_DIM]
    gq = gq.reshape(gq.shape[:-2] + (MLA_HEADS * MLA_QK_DIM,))
    gkv = g["w_kv_up"]
    gk = gkv[..., :MLA_HEADS * LANES].reshape(gkv.shape[:-1] + (MLA_HEADS, LANES))[..., :HEAD_DIM]
    gv = gkv[..., MLA_HEADS * LANES:].reshape(gkv.shape[:-1] + (MLA_HEADS, HEAD_DIM))
    gkv = jnp.concatenate([gk, gv], axis=-1).reshape(gkv.shape[:-1] + (MLA_HEADS * 2 * HEAD_DIM,))
    return {
        "w_in": g_in, "w_q_up": gq, "w_kv_up": gkv,
        "w_branch_a": _q_head_unpairs(g["w_branch_a"], g["w_branch_a"].ndim - 2), "w_branch_b": g["w_branch_b"],
        "w_o": g["w_o"], "w_ffn_up": g["w_ffn_up"], "w_ffn_down": g["w_ffn_down"],
    }


def _pack_small(parts):
    flat = jnp.concatenate([p.reshape(-1) for p in parts])
    pad = (-flat.shape[0]) % (SUBLANES * LANES)
    if pad:
        flat = jnp.concatenate([flat, jnp.zeros((pad,), flat.dtype)])
    return flat.reshape(-1, LANES)


def _unpack_small(packed, shapes):
    flat = packed.reshape(-1)
    out, off = [], 0
    for shp in shapes:
        n = int(np.prod(shp))
        out.append(flat[off:off + n].reshape(shp))
        off += n
    return out


def _shards_of(full, axis):
    shp = full.shape
    cut = shp[:axis] + (N_DEV, shp[axis] // N_DEV) + shp[axis + 1:]
    return jnp.moveaxis(full.reshape(cut), axis, 0)


def _from_shards(shards, axis):
    full = list(shards.shape[1:])
    full[axis] *= N_DEV
    return jnp.moveaxis(shards, 0, axis).reshape(full)


def _rows2d(a):
    return a.reshape(-1, a.shape[-1])


def _layer_fwd(x, u, lw, tabs):
    cos_a, sin_a, cos_b, sin_b = tabs
    z = _matmul(u, lw["w_in"], "nn", "mm_in")
    qa, ka, va, cqn, ckvn, krr = _prep_a_fwd(z, lw["gq2"], lw["gk2"], lw["gqa"], lw["gkva"], cos_a, sin_a, cos_b, sin_b)
    qb = _matmul(cqn, lw["w_q_up"], "nn", "mm_q_up")
    kvb = _matmul(ckvn, lw["w_kv_up"], "nn", "mm_kv_up")
    q_b, k_b, v_b = _prep_b_fwd(qb, kvb, krr, cos_b, sin_b)
    ya, lse_a = _attn_fwd(qa, ka, va, True, "gqa_fwd")
    yb, lse_b = _attn_fwd(q_b, k_b, v_b, False, "mla_fwd")
    ta = _matmul(ya, lw["w_branch_a"], "nn", "mm_branch_a")
    tb = _matmul(yb, lw["w_branch_b"], "nn", "mm_branch_b")
    merged = _merge_fwd(z, lw["b_gate"], ta, tb)
    m = _matmul(merged, lw["w_o"], "nn", "mm_o")
    x2, u2 = _res_norm_fwd(x, m, lw["post_mix_g"], lw["pre_ffn_g"])
    h, a = _matmul(u2, lw["w_ffn_up"], "nn", "mm_ffn_up", post="relu2")
    f = _matmul(a, lw["w_ffn_down"], "nn", "mm_ffn_down")
    x3, u_next = _res_norm_fwd(x2, f, lw["post_ffn_g"], lw["next_pre_mix_g"])
    saved = dict(u=u, z=z, qa=qa, ka=ka, va=va, cqn=cqn, ckvn=ckvn, q_b=q_b, k_b=k_b, v_b=v_b, ya=ya, yb=yb,
                 lse_a=lse_a, lse_b=lse_b, ta=ta, tb=tb, merged=merged, m=m, x2=x2, u2=u2, h=h, a=a, f=f, x3=x3)
    return x3, u_next, saved


def _layer_bwd(dx3, du_next, lw, sv, tabs):
    cos_a, sin_a, cos_b, sin_b = tabs
    g = {}
    dx3, df, dg4, dg1n = _res_norm_bwd(sv["x3"], sv["f"], lw["post_ffn_g"], lw["next_pre_mix_g"], dx3, du_next)
    g["post_ffn_g"], g["next_pre_mix_g"] = dg4, dg1n
    dh = _matmul(df, lw["w_ffn_down"], "nt", "mm_d_h", post="relu2_bwd", h=sv["h"])
    g["w_ffn_down"] = _matmul(sv["a"], df, "tn", "mm_dw_ffn_down")
    du2 = _matmul(dh, lw["w_ffn_up"], "nt", "mm_d_u2")
    g["w_ffn_up"] = _matmul(sv["u2"], dh, "tn", "mm_dw_ffn_up")
    dx2, dm, dg2, dg3 = _res_norm_bwd(sv["x2"], sv["m"], lw["post_mix_g"], lw["pre_ffn_g"], dx3, du2)
    g["post_mix_g"], g["pre_ffn_g"] = dg2, dg3
    dmg = _matmul(dm, lw["w_o"], "nt", "mm_d_merged")
    g["w_o"] = _matmul(sv["merged"], dm, "tn", "mm_dw_o")
    dta, dtb, dzg_a, dzg_b, db_a, db_b = _merge_bwd(dmg, sv["z"], lw["b_gate"], sv["ta"], sv["tb"])
    g["b_gate"] = jnp.concatenate([db_a, db_b], axis=-1)
    dya = _matmul(dta, lw["w_branch_a"], "nt", "mm_d_ya")
    g["w_branch_a"] = _matmul(sv["ya"], dta, "tn", "mm_dw_branch_a")
    dyb = _matmul(dtb, lw["w_branch_b"], "nt", "mm_d_yb")
    g["w_branch_b"] = _matmul(sv["yb"], dtb, "tn", "mm_dw_branch_b")
    delta_a, lse_a = _attn_stats(dya, sv["ya"], sv["lse_a"])
    delta_b, lse_b = _attn_stats(dyb, sv["yb"], sv["lse_b"])
    dqa, dka4, dva4 = _attn_bwd(sv["qa"], sv["ka"], sv["va"], dya, lse_a, delta_a, True, "gqa_bwd")
    dq_b, dk_b, dv_b = _attn_bwd(sv["q_b"], sv["k_b"], sv["v_b"], dyb, lse_b, delta_b, False, "mla_bwd")
    dqb, dkvb, dkr = _prep_b_bwd(dq_b, dk_b, dv_b, cos_b, sin_b)
    dcqn = _matmul(dqb, lw["w_q_up"], "nt", "mm_d_cqn")
    g["w_q_up"] = _matmul(sv["cqn"], dqb, "tn", "mm_dw_q_up")
    dckvn = _matmul(dkvb, lw["w_kv_up"], "nt", "mm_d_ckvn")
    g["w_kv_up"] = _matmul(sv["ckvn"], dkvb, "tn", "mm_dw_kv_up")
    dz, dgq, dgk, dgqa, dgkva = _prep_a_bwd(sv["z"], dqa, dka4, dva4, dcqn, dckvn, dkr, dzg_a, dzg_b, lw["gq2"],
                                            lw["gk2"], lw["gqa"], lw["gkva"], cos_a, sin_a)
    g["q_norm_g"], g["k_norm_g"], g["q_a_norm_g"], g["kv_a_norm_g"] = dgq, dgk, dgqa, dgkva
    du = _matmul(dz, lw["w_in"], "nt", "mm_d_u")
    g["w_in"] = _matmul(sv["u"], dz, "tn", "mm_dw_in")
    return dx2, du, g


def kernel(x, w_in, b_gate, q_norm_g, k_norm_g, q_a_norm_g, kv_a_norm_g, w_q_up, w_kv_up, w_branch_a, w_branch_b, w_o, w_ffn_up, w_ffn_down, pre_mix_g, post_mix_g, pre_ffn_g, post_ffn_g, loss_target, m_w_in, m_b_gate, m_q_norm_g, m_k_norm_g, m_q_a_norm_g, m_kv_a_norm_g, m_w_q_up, m_w_kv_up, m_w_branch_a, m_w_branch_b, m_w_o, m_w_ffn_up, m_w_ffn_down, m_pre_mix_g, m_post_mix_g, m_pre_ffn_g, m_post_ffn_g, v_w_in, v_b_gate, v_q_norm_g, v_k_norm_g, v_q_a_norm_g, v_kv_a_norm_g, v_w_q_up, v_w_kv_up, v_w_branch_a, v_w_branch_b, v_w_o, v_w_ffn_up, v_w_ffn_down, v_pre_mix_g, v_post_mix_g, v_pre_ffn_g, v_post_ffn_g):
    weights = dict(zip(WEIGHT_NAMES, (w_in, b_gate, q_norm_g, k_norm_g, q_a_norm_g, kv_a_norm_g, w_q_up, w_kv_up,
                                      w_branch_a, w_branch_b, w_o, w_ffn_up, w_ffn_down, pre_mix_g, post_mix_g,
                                      pre_ffn_g, post_ffn_g)))
    mom_m = dict(zip(WEIGHT_NAMES, (m_w_in, m_b_gate, m_q_norm_g, m_k_norm_g, m_q_a_norm_g, m_kv_a_norm_g, m_w_q_up,
                                    m_w_kv_up, m_w_branch_a, m_w_branch_b, m_w_o, m_w_ffn_up, m_w_ffn_down,
                                    m_pre_mix_g, m_post_mix_g, m_pre_ffn_g, m_post_ffn_g)))
    mom_v = dict(zip(WEIGHT_NAMES, (v_w_in, v_b_gate, v_q_norm_g, v_k_norm_g, v_q_a_norm_g, v_kv_a_norm_g, v_w_q_up,
                                    v_w_kv_up, v_w_branch_a, v_w_branch_b, v_w_o, v_w_ffn_up, v_w_ffn_down,
                                    v_pre_mix_g, v_post_mix_g, v_pre_ffn_g, v_post_ffn_g)))
    assert x.shape[0] == 1 and x.shape[2] == D_MODEL, x.shape
    n_layers = w_in.shape[0]
    t = x.shape[1]
    x0 = x.reshape(t, D_MODEL)
    target = loss_target.reshape(t, D_MODEL)
    shard_shapes = {n: weights[n].shape for n in BIG_NAMES}
    small_shapes = [weights[n].shape for n in SMALL_NAMES]

    gathered = _all_gather([weights[n].astype(BF16) for n in BIG_NAMES])
    full = {n: _from_shards(g, SHARD_AXIS[n]) for n, g in zip(BIG_NAMES, gathered)}
    lw_all = _layout_weights(full)
    lw_all["b_gate"] = b_gate.reshape(n_layers, 1, 2 * D_MODEL)
    lw_all["gq2"] = jnp.tile(q_norm_g, (1, 2)).reshape(n_layers, 1, LANES)
    lw_all["gk2"] = jnp.tile(k_norm_g, (1, 2)).reshape(n_layers, 1, LANES)
    lw_all["gqa"] = q_a_norm_g.reshape(n_layers, 1, MLA_Q_RANK)
    lw_all["gkva"] = kv_a_norm_g.reshape(n_layers, 1, MLA_KV_RANK)
    for n in ("post_mix_g", "pre_ffn_g", "post_ffn_g"):
        lw_all[n] = weights[n]
    lw_all["next_pre_mix_g"] = jnp.roll(pre_mix_g, -1, axis=0)

    tabs = _rope_tables(t)
    u0 = _rms_fwd(x0, pre_mix_g[0])

    layer_w = [{n: a[l] for n, a in lw_all.items()} for l in range(n_layers)]
    xc, uc, saved = x0, u0, []
    for l in range(n_layers):
        xc, uc, sv = _layer_fwd(xc, uc, layer_w[l], tabs)
        saved.append(sv)
    dy, loss_acc = _loss_grad(xc, target)
    loss = lax.psum(0.5 * jnp.sum(loss_acc) / D_MODEL, ("x", "y", "c"))

    dx0, du0, layer_g = dy, jnp.zeros((t, D_MODEL), F32), [None] * n_layers
    for l in reversed(range(n_layers)):
        dx0, du0, layer_g[l] = _layer_bwd(dx0, du0, layer_w[l], saved[l], tabs)
    grads = {n: jnp.stack([g[n] for g in layer_g]) for n in layer_g[0]}
    grad_x, dg1_first = _rms_bwd(x0, pre_mix_g[0], dx0, du0)

    big_grads = _unlayout_grads({n: grads[n] for n in BIG_NAMES})
    fold = lambda a: a.sum(axis=1)
    dgq = fold(grads["q_norm_g"]).reshape(n_layers, 2, HEAD_DIM).sum(axis=1)
    dgk = fold(grads["k_norm_g"]).reshape(n_layers, 2, HEAD_DIM).sum(axis=1)
    dg1 = jnp.concatenate([fold(dg1_first[None]), fold(grads["next_pre_mix_g"])[:-1]], axis=0)
    small_grads = {
        "b_gate": fold(grads["b_gate"]), "q_norm_g": dgq, "k_norm_g": dgk, "q_a_norm_g": fold(grads["q_a_norm_g"]),
        "kv_a_norm_g": fold(grads["kv_a_norm_g"]), "pre_mix_g": dg1, "post_mix_g": fold(grads["post_mix_g"]),
        "pre_ffn_g": fold(grads["pre_ffn_g"]), "post_ffn_g": fold(grads["post_ffn_g"]),
    }
    small_packed = _pack_small([small_grads[n] for n in SMALL_NAMES])
    sends = [_shards_of(big_grads[n], SHARD_AXIS[n]).reshape((N_DEV,) + _rows2d(weights[n]).shape)
             for n in BIG_NAMES]
    sends.append(jnp.broadcast_to(small_packed[None], (N_DEV,) + small_packed.shape))
    halves = _pair_exchange(sends)
    core = lax.axis_index("c").astype(jnp.int32).reshape(1)
    recvs = _chip_exchange([_pair_add(s, h, core) for s, h in zip(sends, halves)])

    results = {}
    for n, recv in zip(BIG_NAMES, recvs):
        res = _adamw(recv, _rows2d(weights[n]), _rows2d(mom_m[n]), _rows2d(mom_v[n]))
        results[n] = [r.reshape(shard_shapes[n]) for r in res]
    res = _adamw(recvs[-1], *[_pack_small([d[n] for n in SMALL_NAMES]) for d in (weights, mom_m, mom_v)])
    for kind, packed_out in enumerate(res):
        for n, val in zip(SMALL_NAMES, _unpack_small(packed_out, small_shapes)):
            results.setdefault(n, [None] * 4)[kind] = val
    outs = [results[n][kind] for kind in range(4) for n in WEIGHT_NAMES]
    return (loss, grad_x.reshape(x.shape), *outs)
```

```python
import math

import jax
import jax.numpy as jnp
import numpy as np
from jax import lax
from jax.experimental import pallas as pl
from jax.experimental.pallas import tpu as pltpu

F32 = jnp.float32
BF16 = jnp.bfloat16

D_MODEL = 1024
GRID_W = 64
ROPE_THETA = 10000.0
EPS = 1e-6
GQA_HEADS = 8
GQA_KV_HEADS = 2
GQA_GROUP = GQA_HEADS // GQA_KV_HEADS
HEAD_DIM = 64
MLA_HEADS = 8
MLA_ROPE_DIM = 32
MLA_QK_DIM = 96
MLA_Q_RANK = 384
MLA_KV_RANK = 256
GQA_SCALE = 1.0 / math.sqrt(HEAD_DIM)
MLA_SCALE = 1.0 / math.sqrt(MLA_QK_DIM)
LOG2E = math.log2(math.e)
LN2 = math.log(2.0)

ADAM_LR = 0.001
ADAM_B1 = 0.9
ADAM_B2 = 0.999
ADAM_EPS = 1e-08
ADAM_WD = 0.01
ADAM_STEP = 10

N_DEV = 8
LANES = 128
SUBLANES = 8
VMEM_LIMIT = 48 * 1024 * 1024

Z_QA, Z_KA, Z_VA, Z_CQ, Z_CKV, Z_KR, Z_GATE = 0, 512, 640, 768, 1152, 1408, 1536
Z_ATT_W = 1536
Z_W = 3584
KR_LANE0 = 64

WEIGHT_NAMES = ("w_in", "b_gate", "q_norm_g", "k_norm_g", "q_a_norm_g", "kv_a_norm_g", "w_q_up", "w_kv_up",
                "w_branch_a", "w_branch_b", "w_o", "w_ffn_up", "w_ffn_down", "pre_mix_g", "post_mix_g",
                "pre_ffn_g", "post_ffn_g")
SHARD_AXIS = {"w_in": 2, "w_q_up": 2, "w_kv_up": 2, "w_branch_a": 2, "w_branch_b": 2, "w_o": 1, "w_ffn_up": 2,
              "w_ffn_down": 1}
BIG_NAMES = tuple(n for n in WEIGHT_NAMES if n in SHARD_AXIS)
SMALL_NAMES = tuple(n for n in WEIGHT_NAMES if n not in SHARD_AXIS)
ADAM_BLOCK_ELEMS = 256 * 1024
MM_TILE = 1024
MM_TILE_K = 2048
PREP_ROWS = 512
ATTN_TQ = 1024
ATTN_TK = 1024


def _params(*semantics):
    return pltpu.CompilerParams(dimension_semantics=semantics, vmem_limit_bytes=VMEM_LIMIT)


def _tile(n, pref):
    if n <= pref:
        return n
    t = (pref // LANES) * LANES
    while n % t:
        t -= LANES
    return t


def _fold8(t):
    return t.reshape(t.shape[0] // SUBLANES, SUBLANES, t.shape[1]).sum(axis=0)


_DIMS = {"nn": ((1,), (0,)), "nt": ((1,), (1,)), "tn": ((0,), (0,))}


def _matmul(a, b, mode, name, post=None, h=None):
    out_dt = F32 if mode == "tn" else BF16
    if mode == "nn":
        (m, k), n = a.shape, b.shape[1]
    elif mode == "nt":
        (m, k), n = a.shape, b.shape[0]
    else:
        (k, m), n = a.shape, b.shape[1]
    tm, tn, tk = _tile(m, MM_TILE), _tile(n, MM_TILE), _tile(k, MM_TILE_K)
    nk = k // tk
    dims = (_DIMS[mode], ((), ()))
    n_in = 3 if post == "relu2_bwd" else 2
    n_out = 2 if post == "relu2" else 1

    def body(*refs):
        a_ref, b_ref = refs[:2]
        o_refs, acc_ref = refs[n_in:n_in + n_out], refs[-1]

        def finish(val):
            if post == "relu2":
                o_refs[0][...] = val.astype(out_dt)
                r = jnp.maximum(val, 0.0)
                o_refs[1][...] = (r * r).astype(BF16)
            elif post == "relu2_bwd":
                o_refs[0][...] = (val * (2.0 * jnp.maximum(refs[2][...].astype(F32), 0.0))).astype(BF16)
            else:
                o_refs[0][...] = val.astype(out_dt)

        prod = lax.dot_general(a_ref[...], b_ref[...], dims, preferred_element_type=F32)
        if nk == 1:
            finish(prod)
        else:
            kk = pl.program_id(2)

            @pl.when(kk == 0)
            def _():
                acc_ref[...] = prod

            @pl.when(kk > 0)
            def _():
                acc_ref[...] += prod

            @pl.when(kk == nk - 1)
            def _():
                finish(acc_ref[...])

    if mode == "tn":
        a_spec = pl.BlockSpec((tk, tm), lambda i, j, kk: (kk, i))
    else:
        a_spec = pl.BlockSpec((tm, tk), lambda i, j, kk: (i, kk))
    if mode == "nt":
        b_spec = pl.BlockSpec((tn, tk), lambda i, j, kk: (j, kk))
    else:
        b_spec = pl.BlockSpec((tk, tn), lambda i, j, kk: (kk, j))
    o_spec = pl.BlockSpec((tm, tn), lambda i, j, kk: (i, j))
    main_out, bf16_out = jax.ShapeDtypeStruct((m, n), out_dt), jax.ShapeDtypeStruct((m, n), BF16)
    out_shape = {None: main_out, "relu2": [main_out, bf16_out], "relu2_bwd": bf16_out}[post]
    return pl.pallas_call(
        body,
        name=name,
        grid=(m // tm, n // tn, nk),
        in_specs=[a_spec, b_spec] + ([o_spec] if post == "relu2_bwd" else []),
        out_specs=[o_spec, o_spec] if post == "relu2" else o_spec,
        out_shape=out_shape,
        scratch_shapes=[pltpu.VMEM((tm, tn), F32)],
        compiler_params=_params("parallel", "parallel", "arbitrary"),
    )(*((a, b, h) if post == "relu2_bwd" else (a, b)))


def _rinv(x):
    return lax.rsqrt(jnp.mean(x * x, axis=-1, keepdims=True) + EPS)


def _rms_bwd_rows(x, g, dy):
    r = _rinv(x)
    xh = x * r
    dxh = dy * g
    dx = r * (dxh - xh * jnp.mean(dxh * xh, axis=-1, keepdims=True))
    return dx, dy * xh


def _row_spec(tm, c):
    return pl.BlockSpec((tm, c), lambda i: (i, 0))


def _vec_spec(c):
    return pl.BlockSpec((1, c), lambda i: (0, 0))


def _acc_spec(c):
    return pl.BlockSpec((SUBLANES, c), lambda i: (0, 0))


def _rms_fwd(x, g):
    t, d = x.shape
    tm = _tile(t, 512)

    def body(x_ref, g_ref, o_ref):
        xv = x_ref[...]
        o_ref[...] = (xv * _rinv(xv) * g_ref[...]).astype(BF16)

    return pl.pallas_call(
        body, name="rms_fwd", grid=(t // tm,),
        in_specs=[_row_spec(tm, d), _vec_spec(d)], out_specs=_row_spec(tm, d),
        out_shape=jax.ShapeDtypeStruct((t, d), BF16), compiler_params=_params("parallel"),
    )(x, g.reshape(1, d))


def _rms_bwd(x, g, dres, dy):
    t, d = x.shape
    tm = _tile(t, 512)

    def body(x_ref, g_ref, dres_ref, dy_ref, dx_ref, dg_ref):
        dx, dgc = _rms_bwd_rows(x_ref[...], g_ref[...], dy_ref[...].astype(F32))
        dx_ref[...] = dres_ref[...] + dx

        @pl.when(pl.program_id(0) == 0)
        def _():
            dg_ref[...] = jnp.zeros_like(dg_ref)

        dg_ref[...] += _fold8(dgc)

    return pl.pallas_call(
        body, name="rms_bwd", grid=(t // tm,),
        in_specs=[_row_spec(tm, d), _vec_spec(d), _row_spec(tm, d), _row_spec(tm, d)],
        out_specs=[_row_spec(tm, d), _acc_spec(d)],
        out_shape=[jax.ShapeDtypeStruct((t, d), F32), jax.ShapeDtypeStruct((SUBLANES, d), F32)],
        compiler_params=_params("arbitrary"),
    )(x, g.reshape(1, d), dres, dy)


def _res_norm_fwd(x, m, g_post, g_next):
    t, d = x.shape
    tm = _tile(t, 512)

    def body(x_ref, m_ref, gp_ref, gn_ref, x2_ref, u2_ref):
        mv = m_ref[...].astype(F32)
        x2 = x_ref[...] + mv * _rinv(mv) * gp_ref[...]
        x2_ref[...] = x2
        u2_ref[...] = (x2 * _rinv(x2) * gn_ref[...]).astype(BF16)

    return pl.pallas_call(
        body, name="res_norm_fwd", grid=(t // tm,),
        in_specs=[_row_spec(tm, d), _row_spec(tm, d), _vec_spec(d), _vec_spec(d)],
        out_specs=[_row_spec(tm, d), _row_spec(tm, d)],
        out_shape=[jax.ShapeDtypeStruct((t, d), F32), jax.ShapeDtypeStruct((t, d), BF16)],
        compiler_params=_params("parallel"),
    )(x, m, g_post.reshape(1, d), g_next.reshape(1, d))


def _res_norm_bwd(x2, m, g_post, g_next, dx2_in, du2):
    t, d = x2.shape
    tm = _tile(t, 512)

    def body(x2_ref, m_ref, gp_ref, gn_ref, dx2in_ref, du2_ref, dx2_ref, dm_ref, dgp_ref, dgn_ref):
        dxn, dgn_c = _rms_bwd_rows(x2_ref[...], gn_ref[...], du2_ref[...].astype(F32))
        dx2 = dx2in_ref[...] + dxn
        dx2_ref[...] = dx2
        dm, dgp_c = _rms_bwd_rows(m_ref[...].astype(F32), gp_ref[...], dx2)
        dm_ref[...] = dm.astype(BF16)

        @pl.when(pl.program_id(0) == 0)
        def _():
            dgp_ref[...] = jnp.zeros_like(dgp_ref)
            dgn_ref[...] = jnp.zeros_like(dgn_ref)

        dgp_ref[...] += _fold8(dgp_c)
        dgn_ref[...] += _fold8(dgn_c)

    return pl.pallas_call(
        body, name="res_norm_bwd", grid=(t // tm,),
        in_specs=[_row_spec(tm, d), _row_spec(tm, d), _vec_spec(d), _vec_spec(d), _row_spec(tm, d), _row_spec(tm, d)],
        out_specs=[_row_spec(tm, d), _row_spec(tm, d), _acc_spec(d), _acc_spec(d)],
        out_shape=[jax.ShapeDtypeStruct((t, d), F32), jax.ShapeDtypeStruct((t, d), BF16),
                   jax.ShapeDtypeStruct((SUBLANES, d), F32), jax.ShapeDtypeStruct((SUBLANES, d), F32)],
        compiler_params=_params("arbitrary"),
    )(x2, m, g_post.reshape(1, d), g_next.reshape(1, d), dx2_in, du2)


def _rope_tables(t):
    rows = t // GRID_W
    row = jnp.repeat(jnp.arange(rows, dtype=F32), GRID_W)
    col = jnp.tile(jnp.arange(GRID_W, dtype=F32), rows)

    def tab(rot_dim):
        half = rot_dim // 2
        inv = ROPE_THETA ** (-jnp.arange(0, half, 2, dtype=F32) / half)
        ar = row[:, None] * inv[None, :]
        ac = col[:, None] * inv[None, :]
        ang = jnp.concatenate([ar, ar, ac, ac], axis=-1)
        q = half // 2
        sign = np.tile(np.concatenate([-np.ones(q, np.float32), np.ones(q, np.float32)]), 2)
        return jnp.cos(ang), jnp.sin(ang) * sign[None, :]

    ca, sa = tab(HEAD_DIM)
    cb, sb = tab(MLA_ROPE_DIM)
    one = jnp.ones((t, 1), F32)
    cos_b = jnp.concatenate([one * jnp.ones((1, KR_LANE0), F32), cb, one * jnp.ones((1, 32), F32)], axis=-1)
    sin_b = jnp.concatenate([jnp.zeros((t, KR_LANE0), F32), sb, jnp.zeros((t, 32), F32)], axis=-1)
    return jnp.tile(ca, (1, GQA_HEADS)), jnp.tile(sa, (1, GQA_HEADS)), cos_b, sin_b


def _swap_halves(x, sh):
    lane = lax.broadcasted_iota(jnp.int32, x.shape, 1)
    up = pltpu.roll(x, LANES - sh, 1)
    dn = pltpu.roll(x, sh, 1)
    return jnp.where((lane & (2 * sh - 1)) < sh, up, dn)


def _rope(x, cos, sin_s, sh):
    return x * cos + _swap_halves(x, sh) * sin_s


def _rope_bwd(dy, cos, sin_s, sh):
    return dy * cos + _swap_halves(dy * sin_s, sh)


def _lo_mask(shape):
    return lax.broadcasted_iota(jnp.int32, shape, 1) < HEAD_DIM


def _half_mean(t, lo):
    s_lo = jnp.sum(jnp.where(lo, t, 0.0), axis=-1, keepdims=True)
    s_hi = jnp.sum(jnp.where(lo, 0.0, t), axis=-1, keepdims=True)
    return jnp.where(lo, s_lo, s_hi) * (1.0 / HEAD_DIM)


def _head_norm(x, g2):
    lo = _lo_mask(x.shape)
    r = lax.rsqrt(_half_mean(x * x, lo) + EPS)
    return x * r * g2


def _head_norm_bwd(x, g2, dy):
    lo = _lo_mask(x.shape)
    r = lax.rsqrt(_half_mean(x * x, lo) + EPS)
    xh = x * r
    dxh = dy * g2
    dx = r * (dxh - xh * _half_mean(dxh * xh, lo))
    return dx, dy * xh


def _prep_a_fwd(z, gq2, gk2, gqa, gkva, cos_a, sin_a, cos_b, sin_b):
    t = z.shape[0]
    tm = _tile(t, PREP_ROWS)

    def body(z_ref, gq_ref, gk_ref, gqa_ref, gkva_ref, ca_ref, sa_ref, cb_ref, sb_ref,
             qa_ref, ka_ref, va_ref, cqn_ref, ckvn_ref, krr_ref):
        def zf(lo, hi):
            return z_ref[:, lo:hi].astype(F32)

        for j in range(4):
            cols = slice(LANES * j, LANES * (j + 1))
            y = _rope(_head_norm(zf(LANES * j, LANES * (j + 1)), gq_ref[...]), ca_ref[:, cols], sa_ref[:, cols], 16)
            qa_ref[:, cols] = (y * (GQA_SCALE * LOG2E)).astype(BF16)
        y = _rope(_head_norm(zf(Z_KA, Z_VA), gk_ref[...]), ca_ref[:, :LANES], sa_ref[:, :LANES], 16)
        ka_ref[...] = y.astype(BF16)
        va_ref[...] = z_ref[:, Z_VA:Z_CQ].astype(BF16)
        cq = zf(Z_CQ, Z_CKV)
        cqn_ref[...] = (cq * _rinv(cq) * gqa_ref[...]).astype(BF16)
        ckv = zf(Z_CKV, Z_KR)
        ckvn_ref[...] = (ckv * _rinv(ckv) * gkva_ref[...]).astype(BF16)
        krr_ref[...] = _rope(zf(Z_KR, Z_GATE), cb_ref[...], sb_ref[...], 8)

    return pl.pallas_call(
        body, name="prep_a_fwd", grid=(t // tm,),
        in_specs=[_row_spec(tm, Z_ATT_W), _vec_spec(LANES), _vec_spec(LANES), _vec_spec(MLA_Q_RANK),
                  _vec_spec(MLA_KV_RANK), _row_spec(tm, 512), _row_spec(tm, 512), _row_spec(tm, LANES),
                  _row_spec(tm, LANES)],
        out_specs=[_row_spec(tm, 512), _row_spec(tm, LANES), _row_spec(tm, LANES), _row_spec(tm, MLA_Q_RANK),
                   _row_spec(tm, MLA_KV_RANK), _row_spec(tm, LANES)],
        out_shape=[jax.ShapeDtypeStruct((t, 512), BF16), jax.ShapeDtypeStruct((t, LANES), BF16),
                   jax.ShapeDtypeStruct((t, LANES), BF16), jax.ShapeDtypeStruct((t, MLA_Q_RANK), BF16),
                   jax.ShapeDtypeStruct((t, MLA_KV_RANK), BF16), jax.ShapeDtypeStruct((t, LANES), F32)],
        compiler_params=_params("parallel"),
    )(z, gq2, gk2, gqa, gkva, cos_a, sin_a, cos_b, sin_b)


def _prep_a_bwd(z, dqa, dka4, dva4, dcqn, dckvn, dkr, dzga, dzgb, gq2, gk2, gqa, gkva, cos_a, sin_a):
    t = z.shape[0]
    tm = _tile(t, PREP_ROWS)

    def body(z_ref, dqa_ref, dka_ref, dva_ref, dcqn_ref, dckvn_ref, dkr_ref, dzga_ref, dzgb_ref, gq_ref, gk_ref,
             gqa_ref, gkva_ref, ca_ref, sa_ref, dz_ref, dgq_ref, dgk_ref, dgqa_ref, dgkva_ref):
        @pl.when(pl.program_id(0) == 0)
        def _():
            dgq_ref[...] = jnp.zeros_like(dgq_ref)
            dgk_ref[...] = jnp.zeros_like(dgk_ref)
            dgqa_ref[...] = jnp.zeros_like(dgqa_ref)
            dgkva_ref[...] = jnp.zeros_like(dgkva_ref)

        def zf(lo, hi):
            return z_ref[:, lo:hi].astype(F32)

        dgq = jnp.zeros((SUBLANES, LANES), F32)
        for j in range(4):
            cols = slice(LANES * j, LANES * (j + 1))
            dy = _rope_bwd(dqa_ref[:, cols] * GQA_SCALE, ca_ref[:, cols], sa_ref[:, cols], 16)
            dx, dgc = _head_norm_bwd(zf(LANES * j, LANES * (j + 1)), gq_ref[...], dy)
            dz_ref[:, cols] = dx.astype(BF16)
            dgq = dgq + _fold8(dgc)
        dgq_ref[...] += dgq
        dk = (dka_ref[0] + dka_ref[1] + dka_ref[2] + dka_ref[3]).T * LN2
        dy = _rope_bwd(dk, ca_ref[:, :LANES], sa_ref[:, :LANES], 16)
        dx, dgc = _head_norm_bwd(zf(Z_KA, Z_VA), gk_ref[...], dy)
        dz_ref[:, Z_KA:Z_VA] = dx.astype(BF16)
        dgk_ref[...] += _fold8(dgc)
        dz_ref[:, Z_VA:Z_CQ] = (dva_ref[0] + dva_ref[1] + dva_ref[2] + dva_ref[3]).T.astype(BF16)
        dx, dgc = _rms_bwd_rows(zf(Z_CQ, Z_CKV), gqa_ref[...], dcqn_ref[...].astype(F32))
        dz_ref[:, Z_CQ:Z_CKV] = dx.astype(BF16)
        dgqa_ref[...] += _fold8(dgc)
        dx, dgc = _rms_bwd_rows(zf(Z_CKV, Z_KR), gkva_ref[...], dckvn_ref[...].astype(F32))
        dz_ref[:, Z_CKV:Z_KR] = dx.astype(BF16)
        dgkva_ref[...] += _fold8(dgc)
        dz_ref[:, Z_KR:Z_GATE] = dkr_ref[...].astype(BF16)
        dz_ref[:, Z_GATE:Z_GATE + D_MODEL] = dzga_ref[...]
        dz_ref[:, Z_GATE + D_MODEL:Z_W] = dzgb_ref[...]

    part = pl.BlockSpec((4, LANES, tm), lambda i: (0, 0, i))
    return pl.pallas_call(
        body, name="prep_a_bwd", grid=(t // tm,),
        in_specs=[_row_spec(tm, Z_ATT_W), _row_spec(tm, 512), part, part, _row_spec(tm, MLA_Q_RANK),
                  _row_spec(tm, MLA_KV_RANK), _row_spec(tm, LANES), _row_spec(tm, D_MODEL), _row_spec(tm, D_MODEL),
                  _vec_spec(LANES),
                  _vec_spec(LANES), _vec_spec(MLA_Q_RANK), _vec_spec(MLA_KV_RANK), _row_spec(tm, 512),
                  _row_spec(tm, 512)],
        out_specs=[_row_spec(tm, Z_W), _acc_spec(LANES), _acc_spec(LANES), _acc_spec(MLA_Q_RANK),
                   _acc_spec(MLA_KV_RANK)],
        out_shape=[jax.ShapeDtypeStruct((t, Z_W), BF16), jax.ShapeDtypeStruct((SUBLANES, LANES), F32),
                   jax.ShapeDtypeStruct((SUBLANES, LANES), F32), jax.ShapeDtypeStruct((SUBLANES, MLA_Q_RANK), F32),
                   jax.ShapeDtypeStruct((SUBLANES, MLA_KV_RANK), F32)],
        compiler_params=_params("arbitrary"),
    )(z, dqa, dka4, dva4, dcqn, dckvn, dkr, dzga, dzgb, gq2, gk2, gqa, gkva, cos_a, sin_a)


def _prep_b_fwd(qb, kvb, krr, cos_b, sin_b):
    t = qb.shape[0]
    tm = _tile(t, PREP_ROWS)

    def body(qb_ref, kvb_ref, krr_ref, cb_ref, sb_ref, q_ref, k_ref, v_ref):
        for h in range(MLA_HEADS):
            cols = slice(LANES * h, LANES * (h + 1))
            qh = _rope(qb_ref[:, cols].astype(F32), cb_ref[...], sb_ref[...], 8)
            q_ref[:, cols] = (qh * (MLA_SCALE * LOG2E)).astype(BF16)
            k_ref[:, cols] = (kvb_ref[:, cols].astype(F32) + krr_ref[...]).astype(BF16)
        v_ref[...] = kvb_ref[:, 1024:1536].astype(BF16)

    return pl.pallas_call(
        body, name="prep_b_fwd", grid=(t // tm,),
        in_specs=[_row_spec(tm, 1024), _row_spec(tm, 1536), _row_spec(tm, LANES), _row_spec(tm, LANES),
                  _row_spec(tm, LANES)],
        out_specs=[_row_spec(tm, 1024), _row_spec(tm, 1024), _row_spec(tm, 512)],
        out_shape=[jax.ShapeDtypeStruct((t, 1024), BF16), jax.ShapeDtypeStruct((t, 1024), BF16),
                   jax.ShapeDtypeStruct((t, 512), BF16)],
        compiler_params=_params("parallel"),
    )(qb, kvb, krr, cos_b, sin_b)


def _prep_b_bwd(dq, dk, dv, cos_b, sin_b):
    t = dq.shape[0]
    tm = _tile(t, PREP_ROWS)

    def body(dq_ref, dk_ref, dv_ref, cb_ref, sb_ref, dqb_ref, dkvb_ref, dkr_ref):
        dkr = jnp.zeros((tm, LANES), F32)
        for h in range(MLA_HEADS):
            cols = slice(LANES * h, LANES * (h + 1))
            dqb_ref[:, cols] = _rope_bwd(dq_ref[:, cols] * MLA_SCALE, cb_ref[...], sb_ref[...], 8).astype(BF16)
            dkh = dk_ref[cols, :].T * LN2
            dkvb_ref[:, cols] = dkh.astype(BF16)
            dkr = dkr + dkh
        for j in range(MLA_HEADS // 2):
            dkvb_ref[:, 1024 + LANES * j:1024 + LANES * (j + 1)] = dv_ref[LANES * j:LANES * (j + 1), :].T.astype(BF16)
        dkr_ref[...] = _rope_bwd(dkr, cb_ref[...], sb_ref[...], 8)

    return pl.pallas_call(
        body, name="prep_b_bwd", grid=(t // tm,),
        in_specs=[_row_spec(tm, 1024), pl.BlockSpec((1024, tm), lambda i: (0, i)),
                  pl.BlockSpec((512, tm), lambda i: (0, i)), _row_spec(tm, LANES),
                  _row_spec(tm, LANES)],
        out_specs=[_row_spec(tm, 1024), _row_spec(tm, 1536), _row_spec(tm, LANES)],
        out_shape=[jax.ShapeDtypeStruct((t, 1024), BF16), jax.ShapeDtypeStruct((t, 1536), BF16),
                   jax.ShapeDtypeStruct((t, LANES), F32)],
        compiler_params=_params("parallel"),
    )(dq, dk, dv, cos_b, sin_b)


_NT = (((1,), (1,)), ((), ()))
_NN = (((1,), (0,)), ((), ()))
_TN = (((0,), (0,)), ((), ()))


def _head_operands(qv, kv, i, shared_k):
    if shared_k:
        lo = _lo_mask(qv.shape)
        keep = lo if i == 0 else jnp.logical_not(lo)
        return jnp.where(keep, qv, jnp.zeros_like(qv)), kv
    cols = slice(LANES * i, LANES * (i + 1))
    return qv[:, cols], kv[:, cols]


def _attn_specs(shared_k, tq, tk, q_of, k_of):
    wq = LANES if shared_k else 2 * LANES
    q_spec = pl.BlockSpec((tq, wq), lambda *g: (q_of(*g), g[0]))
    if shared_k:
        k_spec = pl.BlockSpec((tk, LANES), lambda *g: (k_of(*g), 0))
        v_spec = pl.BlockSpec((tk, LANES), lambda *g: (k_of(*g), 0))
    else:
        k_spec = pl.BlockSpec((tk, wq), lambda *g: (k_of(*g), g[0]))
        v_spec = pl.BlockSpec((tk, LANES), lambda *g: (k_of(*g), g[0]))
    return wq, q_spec, k_spec, v_spec


def _attn_fwd(q, k, v, shared_k, name):
    t = q.shape[0]
    tq, tk = _tile(t, ATTN_TQ), _tile(t, ATTN_TK)
    nq, nk = t // tq, t // tk
    wq, q_spec, k_spec, v_spec = _attn_specs(shared_k, tq, tk, lambda p, i, j: i, lambda p, i, j: j)
    groups = q.shape[1] // wq
    chunk = _tile(tq, 2 * LANES)

    def body(q_ref, k_ref, v_ref, o_ref, lse_ref, m_s, acc_s, alpha_s, s_s, p_s):
        kb = pl.program_id(2)

        @pl.when(kb == 0)
        def _():
            m_s[...] = jnp.full_like(m_s, -jnp.inf)
            acc_s[...] = jnp.zeros_like(acc_s)

        qv, kv, vv = q_ref[...], k_ref[...], v_ref[...]
        lo = _lo_mask(vv.shape)
        for i in range(2):
            qi, ki = _head_operands(qv, kv, i, shared_k)
            s_s[i] = lax.dot_general(ki, qi, _NT, preferred_element_type=F32)
        for i in range(2):
            for c in range(tq // chunk):
                cols = slice(c * chunk, (c + 1) * chunk)
                m_prev = m_s[i, :, cols]
                m_new = jnp.maximum(m_prev, jnp.max(s_s[i, :, cols], axis=0, keepdims=True))
                alpha_s[i, :, cols] = jnp.exp2(m_prev - m_new)
                m_s[i, :, cols] = m_new
                p_s[i, :, cols] = jnp.exp2(s_s[i, :, cols] - m_new).astype(BF16)
        for i in range(2):
            keep = lo if i == 0 else jnp.logical_not(lo)
            vi = jnp.where(keep, vv, jnp.ones_like(vv))
            acc_s[i] = alpha_s[i] * acc_s[i] + lax.dot_general(vi, p_s[i], _TN, preferred_element_type=F32)

        @pl.when(kb == nk - 1)
        def _():
            a0, a1 = acc_s[0], acc_s[1]
            l0 = a0[LANES - SUBLANES:, :][0:1, :]
            l1 = a1[0:SUBLANES, :][0:1, :]
            row_lo = lax.broadcasted_iota(jnp.int32, a0.shape, 0) < HEAD_DIM
            o_ref[...] = jnp.where(row_lo, a0 / l0, a1 / l1).T.astype(BF16)
            lse_ref[0] = m_s[0] + jnp.log2(l0)
            lse_ref[1] = m_s[1] + jnp.log2(l1)

    return pl.pallas_call(
        body, name=name, grid=(groups, nq, nk),
        in_specs=[q_spec, k_spec, v_spec],
        out_specs=[pl.BlockSpec((tq, LANES), lambda p, i, j: (i, p)),
                   pl.BlockSpec((2, 1, tq), lambda p, i, j: (p, 0, i))],
        out_shape=[jax.ShapeDtypeStruct((t, LANES * groups), BF16),
                   jax.ShapeDtypeStruct((2 * groups, 1, t), F32)],
        scratch_shapes=[pltpu.VMEM((2, 1, tq), F32), pltpu.VMEM((2, LANES, tq), F32), pltpu.VMEM((2, 1, tq), F32),
                        pltpu.VMEM((2, tk, tq), F32), pltpu.VMEM((2, tk, tq), BF16)],
        compiler_params=_params("parallel", "parallel", "arbitrary"),
    )(q, k, v)


def _attn_stats(do, o, lse):
    t, w = do.shape
    tm = _tile(t, 512)
    groups = w // LANES

    def body(do_ref, o_ref, lse_ref, delta_ref, lser_ref):
        prod = do_ref[...].astype(F32) * o_ref[...].astype(F32)
        for g in range(groups):
            x = prod[:, LANES * g:LANES * (g + 1)]
            lo = _lo_mask(x.shape)
            d0 = jnp.sum(jnp.where(lo, x, 0.0), axis=-1, keepdims=True)
            d1 = jnp.sum(jnp.where(lo, 0.0, x), axis=-1, keepdims=True)
            delta_ref[2 * g] = jnp.broadcast_to(d0, (tm, LANES))
            delta_ref[2 * g + 1] = jnp.broadcast_to(d1, (tm, LANES))
        for h in range(2 * groups):
            lser_ref[h] = jnp.broadcast_to(lse_ref[h], (LANES, tm)).T

    rep_spec = pl.BlockSpec((2 * groups, tm, LANES), lambda i: (0, i, 0))
    rep_shape = jax.ShapeDtypeStruct((2 * groups, t, LANES), F32)
    return pl.pallas_call(
        body, name="attn_stats", grid=(t // tm,),
        in_specs=[_row_spec(tm, w), _row_spec(tm, w), pl.BlockSpec((2 * groups, 1, tm), lambda i: (0, 0, i))],
        out_specs=[rep_spec, rep_spec],
        out_shape=[rep_shape, rep_shape],
        compiler_params=_params("parallel"),
    )(do, o, lse)


def _attn_bwd(q, k, v, do, lse, delta, shared_k, name):
    t = q.shape[0]
    tq, tk = _tile(t, ATTN_TQ), _tile(t, ATTN_TK)
    nq, nk = t // tq, t // tk
    wq, q_spec, k_spec, v_spec = _attn_specs(shared_k, tq, tk, lambda p, j, i: i, lambda p, j, i: j)
    groups = q.shape[1] // wq

    def body(q_ref, k_ref, v_ref, do_ref, lse_ref, delta_ref, dq_ref, dk_ref, dv_ref, dk_s, dv_s, s_s, dp_s, p_s,
             ds_s):
        kb, qb = pl.program_id(1), pl.program_id(2)

        @pl.when(qb == 0)
        def _():
            dk_s[...] = jnp.zeros_like(dk_s)
            dv_s[...] = jnp.zeros_like(dv_s)

        qv, kv, vv, dov = q_ref[...], k_ref[...], v_ref[...], do_ref[...]
        lo = _lo_mask(dov.shape)
        heads = []
        for i in range(2):
            qi, ki = _head_operands(qv, kv, i, shared_k)
            keep = lo if i == 0 else jnp.logical_not(lo)
            doi = jnp.where(keep, dov, jnp.zeros_like(dov))
            heads.append((qi, ki, doi))
            s_s[i] = lax.dot_general(qi, ki, _NT, preferred_element_type=F32)
            dp_s[i] = lax.dot_general(doi, vv, _NT, preferred_element_type=F32)
        for i in range(2):
            lse_i, delta_i = lse_ref[i], delta_ref[i]
            for c in range(tk // LANES):
                cols = slice(c * LANES, (c + 1) * LANES)
                p = jnp.exp2(s_s[i, :, cols] - lse_i)
                p_s[i, :, cols] = p.astype(BF16)
                ds_s[i, :, cols] = (p * (dp_s[i, :, cols] - delta_i)).astype(BF16)
        dq_parts = []
        for i in range(2):
            qi, ki, doi = heads[i]
            dv_s[...] += lax.dot_general(doi, p_s[i], _TN, preferred_element_type=F32)
            dk_i = lax.dot_general(qi, ds_s[i], _TN, preferred_element_type=F32)
            if shared_k:
                dk_s[...] += dk_i
            else:
                dk_s[LANES * i:LANES * (i + 1), :] += dk_i
            dq_parts.append(lax.dot_general(ds_s[i], ki, _NN, preferred_element_type=F32))
        rows = pl.ds(pl.multiple_of(qb * tq, tq), tq)
        if shared_k:
            tiles = [(slice(0, LANES), jnp.where(lo, dq_parts[0], dq_parts[1]))]
        else:
            tiles = [(slice(0, LANES), dq_parts[0]), (slice(LANES, 2 * LANES), dq_parts[1])]
        for cols, val in tiles:
            @pl.when(kb == 0)
            def _(cols=cols, val=val):
                dq_ref[rows, cols] = val

            @pl.when(kb > 0)
            def _(cols=cols, val=val):
                dq_ref[rows, cols] += val

        @pl.when(qb == nq - 1)
        def _():
            if shared_k:
                dk_ref[0] = dk_s[...]
                dv_ref[0] = dv_s[...]
            else:
                dk_ref[...] = dk_s[...]
                dv_ref[...] = dv_s[...]

    stat_spec = pl.BlockSpec((2, tq, LANES), lambda p, j, i: (p, i, 0))
    do_spec = pl.BlockSpec((tq, LANES), lambda p, j, i: (i, p))
    dq_spec = pl.BlockSpec((t, wq), lambda p, j, i: (0, p))
    if shared_k:
        dk_spec = pl.BlockSpec((1, LANES, tk), lambda p, j, i: (p, 0, j))
        dv_spec = dk_spec
        dk_shape = jax.ShapeDtypeStruct((groups, LANES, t), F32)
        dv_shape = dk_shape
    else:
        dk_spec = pl.BlockSpec((wq, tk), lambda p, j, i: (p, j))
        dv_spec = pl.BlockSpec((LANES, tk), lambda p, j, i: (p, j))
        dk_shape = jax.ShapeDtypeStruct((wq * groups, t), F32)
        dv_shape = jax.ShapeDtypeStruct((LANES * groups, t), F32)
    return pl.pallas_call(
        body, name=name, grid=(groups, nk, nq),
        in_specs=[q_spec, k_spec, v_spec, do_spec, stat_spec, stat_spec],
        out_specs=[dq_spec, dk_spec, dv_spec],
        out_shape=[jax.ShapeDtypeStruct((t, wq * groups), F32), dk_shape, dv_shape],
        scratch_shapes=[pltpu.VMEM((wq, tk), F32), pltpu.VMEM((LANES, tk), F32), pltpu.VMEM((2, tq, tk), F32),
                        pltpu.VMEM((2, tq, tk), F32), pltpu.VMEM((2, tq, tk), BF16), pltpu.VMEM((2, tq, tk), BF16)],
        compiler_params=_params("parallel", "arbitrary", "arbitrary"),
    )(q, k, v, do, lse, delta)


_MERGE_W = 512
_GATE_BLK0 = Z_GATE // _MERGE_W


def _merge_fwd(z, b_gate, ta, tb):
    t = z.shape[0]
    tm = _tile(t, 512)
    w = _MERGE_W
    nj = D_MODEL // w

    def body(za_ref, zb_ref, ba_ref, bb_ref, ta_ref, tb_ref, o_ref):
        ga = jax.nn.sigmoid(za_ref[...].astype(F32) + ba_ref[...])
        gb = jax.nn.sigmoid(zb_ref[...].astype(F32) + bb_ref[...])
        o_ref[...] = (ga * ta_ref[...].astype(F32) + gb * tb_ref[...].astype(F32)).astype(BF16)

    return pl.pallas_call(
        body, name="merge_fwd", grid=(t // tm, nj),
        in_specs=[pl.BlockSpec((tm, w), lambda i, j: (i, _GATE_BLK0 + j)),
                  pl.BlockSpec((tm, w), lambda i, j: (i, _GATE_BLK0 + nj + j)),
                  pl.BlockSpec((1, w), lambda i, j: (0, j)),
                  pl.BlockSpec((1, w), lambda i, j: (0, nj + j)),
                  pl.BlockSpec((tm, w), lambda i, j: (i, j)),
                  pl.BlockSpec((tm, w), lambda i, j: (i, j))],
        out_specs=pl.BlockSpec((tm, w), lambda i, j: (i, j)),
        out_shape=jax.ShapeDtypeStruct((t, D_MODEL), BF16),
        compiler_params=_params("parallel", "parallel"),
    )(z, z, b_gate, b_gate, ta, tb)


def _merge_bwd(dmg, z, b_gate, ta, tb):
    t = z.shape[0]
    tm = _tile(t, 512)
    w = _MERGE_W
    nj = D_MODEL // w

    def body(dm_ref, za_ref, zb_ref, ba_ref, bb_ref, ta_ref, tb_ref, dta_ref, dtb_ref, dza_ref, dzb_ref,
             dba_ref, dbb_ref):
        dm = dm_ref[...].astype(F32)
        ga = jax.nn.sigmoid(za_ref[...].astype(F32) + ba_ref[...])
        gb = jax.nn.sigmoid(zb_ref[...].astype(F32) + bb_ref[...])
        dta_ref[...] = (dm * ga).astype(BF16)
        dtb_ref[...] = (dm * gb).astype(BF16)
        dza = dm * ta_ref[...].astype(F32) * ga * (1.0 - ga)
        dzb = dm * tb_ref[...].astype(F32) * gb * (1.0 - gb)
        dza_ref[...] = dza.astype(BF16)
        dzb_ref[...] = dzb.astype(BF16)

        @pl.when(pl.program_id(1) == 0)
        def _():
            dba_ref[...] = jnp.zeros_like(dba_ref)
            dbb_ref[...] = jnp.zeros_like(dbb_ref)

        dba_ref[...] += _fold8(dza)
        dbb_ref[...] += _fold8(dzb)

    blk = pl.BlockSpec((tm, w), lambda j, i: (i, j))
    acc = pl.BlockSpec((SUBLANES, w), lambda j, i: (0, j))
    return pl.pallas_call(
        body, name="merge_bwd", grid=(nj, t // tm),
        in_specs=[blk,
                  pl.BlockSpec((tm, w), lambda j, i: (i, _GATE_BLK0 + j)),
                  pl.BlockSpec((tm, w), lambda j, i: (i, _GATE_BLK0 + nj + j)),
                  pl.BlockSpec((1, w), lambda j, i: (0, j)),
                  pl.BlockSpec((1, w), lambda j, i: (0, nj + j)),
                  blk, blk],
        out_specs=[blk, blk, blk, blk, acc, acc],
        out_shape=[jax.ShapeDtypeStruct((t, D_MODEL), BF16)] * 4 + [jax.ShapeDtypeStruct((SUBLANES, D_MODEL), F32)] * 2,
        compiler_params=_params("parallel", "arbitrary"),
    )(dmg, z, z, b_gate, b_gate, ta, tb)


def _loss_grad(y, target):
    t, d = y.shape
    tm = _tile(t, 512)

    def body(y_ref, t_ref, dy_ref, acc_ref):
        err = y_ref[...] - t_ref[...]
        dy_ref[...] = err * (1.0 / d)
        e8 = _fold8(err * err)
        part = e8[:, 0:LANES]
        for c in range(1, d // LANES):
            part = part + e8[:, LANES * c:LANES * (c + 1)]

        @pl.when(pl.program_id(0) == 0)
        def _():
            acc_ref[...] = jnp.zeros_like(acc_ref)

        acc_ref[...] += part

    return pl.pallas_call(
        body, name="loss_grad", grid=(t // tm,),
        in_specs=[_row_spec(tm, d), _row_spec(tm, d)],
        out_specs=[_row_spec(tm, d), _acc_spec(LANES)],
        out_shape=[jax.ShapeDtypeStruct((t, d), F32), jax.ShapeDtypeStruct((SUBLANES, LANES), F32)],
        compiler_params=_params("arbitrary"),
    )(y, target)


_MESH_ID = pl.DeviceIdType.MESH
_ANY = pl.BlockSpec(memory_space=pl.ANY)


def _all_gather(arrays):
    n = len(arrays)

    def body(*refs):
        x_refs, out_refs = refs[:n], refs[n:2 * n]
        send_sems, recv_sems, local_sems = refs[2 * n:]
        mx, my, mc = lax.axis_index("x"), lax.axis_index("y"), lax.axis_index("c")
        me, sibling = (mx, my, mc), (mx, my, 1 - mc)
        chips = [(1 - mx, my), (mx, 1 - my), (1 - mx, 1 - my)]

        def slot(a, px, py, pc):
            return out_refs[a].at[4 * px + 2 * py + pc]

        def copy(a, sem, block, to, src=None):
            return pltpu.make_async_remote_copy(
                src_ref=slot(a, *block) if src is None else src, dst_ref=slot(a, *block),
                send_sem=send_sems.at[a, sem], recv_sem=recv_sems.at[a, sem], device_id=to, device_id_type=_MESH_ID)

        mine = [pltpu.make_async_copy(x_refs[a], slot(a, *me), local_sems.at[a]) for a in range(n)]
        first = []
        for a in range(n):
            mine[a].start()
            first.append(copy(a, 0, me, sibling, src=x_refs[a]))
            first += [copy(a, 1 + j, me, (*chip, mc), src=x_refs[a]) for j, chip in enumerate(chips)]
        for cp in first:
            cp.start()
        passed = []
        for a in range(n):
            for j, chip in enumerate(chips):
                copy(a, 1 + j, (*chip, mc), me).wait_recv()
                passed.append(copy(a, 4 + j, (*chip, mc), sibling))
                passed[-1].start()
        for a in range(n):
            copy(a, 0, sibling, me).wait_recv()
            for j, chip in enumerate(chips):
                copy(a, 4 + j, (*chip, 1 - mc), me).wait_recv()
        for cp in first + passed:
            cp.wait_send()
        for cp in mine:
            cp.wait()

    return pl.pallas_call(
        body, name="weight_all_gather",
        out_shape=[jax.ShapeDtypeStruct((N_DEV,) + a.shape, a.dtype) for a in arrays],
        in_specs=[_ANY] * n, out_specs=[_ANY] * n,
        scratch_shapes=[pltpu.SemaphoreType.DMA((n, 7)), pltpu.SemaphoreType.DMA((n, 7)),
                        pltpu.SemaphoreType.DMA((n,))],
    )(*arrays)


def _pair_exchange(sends):
    n = len(sends)

    def body(*refs):
        s_refs, r_refs = refs[:n], refs[n:2 * n]
        send_sems, recv_sems = refs[2 * n:]
        mx, my, mc = lax.axis_index("x"), lax.axis_index("y"), lax.axis_index("c")
        copies = []
        for a in range(n):
            for ch in range(4):
                cp = pltpu.make_async_remote_copy(
                    src_ref=s_refs[a].at[2 * ch + (1 - mc)], dst_ref=r_refs[a].at[ch], send_sem=send_sems.at[a, ch],
                    recv_sem=recv_sems.at[a, ch], device_id=(mx, my, 1 - mc), device_id_type=_MESH_ID)
                cp.start()
                copies.append(cp)
        for cp in copies:
            cp.wait_send()
            cp.wait_recv()

    return pl.pallas_call(
        body, name="grad_pair_exchange",
        out_shape=[jax.ShapeDtypeStruct((4,) + s.shape[1:], s.dtype) for s in sends],
        in_specs=[_ANY] * n, out_specs=[_ANY] * n,
        scratch_shapes=[pltpu.SemaphoreType.DMA((n, 4)), pltpu.SemaphoreType.DMA((n, 4))],
    )(*sends)


def _pair_add(send, half, core):
    _, r, c_ = send.shape
    tr = _row_tile(r, c_)

    def body(core_ref, s_ref, h_ref, o_ref):
        del core_ref
        o_ref[...] = (s_ref[...] + h_ref[...]).astype(BF16)

    blk = pl.BlockSpec((1, tr, c_), lambda ch, i, core_ref: (ch, i, 0))
    return pl.pallas_call(
        body, name="grad_pair_add",
        grid_spec=pltpu.PrefetchScalarGridSpec(
            num_scalar_prefetch=1, grid=(4, r // tr),
            in_specs=[pl.BlockSpec((1, tr, c_), lambda ch, i, core_ref: (2 * ch + core_ref[0], i, 0)), blk],
            out_specs=blk),
        out_shape=jax.ShapeDtypeStruct((4, r, c_), BF16),
        compiler_params=_params("parallel", "parallel"),
    )(core, send, half)


def _chip_exchange(parts):
    n = len(parts)

    def body(*refs):
        p_refs, r_refs = refs[:n], refs[n:2 * n]
        send_sems, recv_sems, local_sems = refs[2 * n:]
        mx, my, mc = lax.axis_index("x"), lax.axis_index("y"), lax.axis_index("c")
        mine = 2 * mx + my
        local = [pltpu.make_async_copy(p_refs[a].at[mine], r_refs[a].at[mine], local_sems.at[a]) for a in range(n)]
        copies = []
        for a in range(n):
            local[a].start()
            for rel in range(1, 4):
                px = 1 - mx if rel & 2 else mx
                py = 1 - my if rel & 1 else my
                cp = pltpu.make_async_remote_copy(
                    src_ref=p_refs[a].at[2 * px + py], dst_ref=r_refs[a].at[mine], send_sem=send_sems.at[a, rel - 1],
                    recv_sem=recv_sems.at[a, rel - 1], device_id=(px, py, mc), device_id_type=_MESH_ID)
                cp.start()
                copies.append(cp)
        for cp in copies:
            cp.wait_send()
            cp.wait_recv()
        for cp in local:
            cp.wait()

    return pl.pallas_call(
        body, name="grad_chip_exchange",
        out_shape=[jax.ShapeDtypeStruct(p.shape, p.dtype) for p in parts],
        in_specs=[_ANY] * n, out_specs=[_ANY] * n,
        scratch_shapes=[pltpu.SemaphoreType.DMA((n, 3)), pltpu.SemaphoreType.DMA((n, 3)),
                        pltpu.SemaphoreType.DMA((n,))],
    )(*parts)


def _row_tile(r, c_):
    tr = min(r, ADAM_BLOCK_ELEMS // (pl.cdiv(c_, LANES) * LANES))
    while r % tr:
        tr -= SUBLANES
    return tr


def _adamw(recv, w, m, v):
    r, c_ = w.shape
    tr = _row_tile(r, c_)
    n_src = recv.shape[0]

    def body(g_ref, w_ref, m_ref, v_ref, go_ref, d_ref, mo_ref, vo_ref):
        g = g_ref[0].astype(F32)
        for s in range(1, n_src):
            g = g + g_ref[s].astype(F32)
        go_ref[...] = g
        mn = ADAM_B1 * m_ref[...] + (1.0 - ADAM_B1) * g
        vn = ADAM_B2 * v_ref[...] + (1.0 - ADAM_B2) * (g * g)
        mo_ref[...] = mn
        vo_ref[...] = vn
        m_hat = mn / (1.0 - ADAM_B1 ** ADAM_STEP)
        v_hat = vn / (1.0 - ADAM_B2 ** ADAM_STEP)
        d_ref[...] = -ADAM_LR * (m_hat / (jnp.sqrt(v_hat) + ADAM_EPS) + ADAM_WD * w_ref[...])

    spec = pl.BlockSpec((tr, c_), lambda i: (i, 0))
    out = jax.ShapeDtypeStruct((r, c_), F32)
    return pl.pallas_call(
        body, name="grad_sum_adamw", grid=(r // tr,),
        in_specs=[pl.BlockSpec((n_src, tr, c_), lambda i: (0, i, 0)), spec, spec, spec],
        out_specs=[spec, spec, spec, spec], out_shape=[out, out, out, out],
        compiler_params=_params("parallel"),
    )(recv, w, m, v)


def _pad_cols(a, before, after):
    parts = []
    if before:
        parts.append(jnp.zeros(a.shape[:-1] + (before,), a.dtype))
    parts.append(a)
    if after:
        parts.append(jnp.zeros(a.shape[:-1] + (after,), a.dtype))
    return jnp.concatenate(parts, axis=-1)


def _q_head_pairs(a, axis):
    shp = a.shape
    a = a.reshape(shp[:axis] + (GQA_KV_HEADS, GQA_GROUP, HEAD_DIM) + shp[axis + 1:])
    a = jnp.swapaxes(a, axis, axis + 1)
    return a.reshape(shp)


def _q_head_unpairs(a, axis):
    shp = a.shape
    a = a.reshape(shp[:axis] + (GQA_GROUP, GQA_KV_HEADS, HEAD_DIM) + shp[axis + 1:])
    a = jnp.swapaxes(a, axis, axis + 1)
    return a.reshape(shp)


def _layout_weights(w):
    w_in = w["w_in"]
    lead = w_in.shape[:-1]
    w_in_p = jnp.concatenate([
        _q_head_pairs(w_in[..., 0:512], w_in.ndim - 1),
        w_in[..., 512:1408],
        _pad_cols(w_in[..., 1408:1440], KR_LANE0, LANES - KR_LANE0 - MLA_ROPE_DIM),
        w_in[..., 1440:],
    ], axis=-1)
    wq = w["w_q_up"]
    wq_p = _pad_cols(wq.reshape(wq.shape[:-1] + (MLA_HEADS, MLA_QK_DIM)), 0, LANES - MLA_QK_DIM)
    wq_p = wq_p.reshape(wq.shape[:-1] + (MLA_HEADS * LANES,))
    wkv = w["w_kv_up"]
    wkv4 = wkv.reshape(wkv.shape[:-1] + (MLA_HEADS, 2 * HEAD_DIM))
    wk_p = _pad_cols(wkv4[..., :HEAD_DIM], 0, LANES - HEAD_DIM).reshape(wkv.shape[:-1] + (MLA_HEADS * LANES,))
    wv_p = wkv4[..., HEAD_DIM:].reshape(wkv.shape[:-1] + (MLA_HEADS * HEAD_DIM,))
    del lead
    return {
        "w_in": w_in_p, "w_q_up": wq_p, "w_kv_up": jnp.concatenate([wk_p, wv_p], axis=-1),
        "w_branch_a": _q_head_pairs(w["w_branch_a"], w["w_branch_a"].ndim - 2), "w_branch_b": w["w_branch_b"],
        "w_o": w["w_o"], "w_ffn_up": w["w_ffn_up"], "w_ffn_down": w["w_ffn_down"],
    }


def _unlayout_grads(g):
    gi = g["w_in"]
    kr0 = Z_KR + KR_LANE0
    g_in = jnp.concatenate([
        _q_head_unpairs(gi[..., 0:512], gi.ndim - 1), gi[..., 512:1408], gi[..., kr0:kr0 + MLA_ROPE_DIM],
        gi[..., Z_GATE:],
    ], axis=-1)
    gq = g["w_q_up"]
    gq = gq.reshape(gq.shape[:-1] + (MLA_HEADS, LANES))[..., :MLA_QK_DIM]
    gq = gq.reshape(gq.shape[:-2] + (MLA_HEADS * MLA_QK_DIM,))
    gkv = g["w_kv_up"]
    gk = gkv[..., :MLA_HEADS * LANES].reshape(gkv.shape[:-1] + (MLA_HEADS, LANES))[..., :HEAD_DIM]
    gv = gkv[..., MLA_HEADS * LANES:].reshape(gkv.shape[:-1] + (MLA_HEADS, HEAD_DIM))
    gkv = jnp.concatenate([gk, gv], axis=-1).reshape(gkv.shape[:-1] + (MLA_HEADS * 2 * HEAD_DIM,))
    return {
        "w_in": g_in, "w_q_up": gq, "w_kv_up": gkv,
        "w_branch_a": _q_head_unpairs(g["w_branch_a"], g["w_branch_a"].ndim - 2), "w_branch_b": g["w_branch_b"],
        "w_o": g["w_o"], "w_ffn_up": g["w_ffn_up"], "w_ffn_down": g["w_ffn_down"],
    }


def _pack_small(parts):
    flat = jnp.concatenate([p.reshape(-1) for p in parts])
    pad = (-flat.shape[0]) % (SUBLANES * LANES)
    if pad:
        flat = jnp.concatenate([flat, jnp.zeros((pad,), flat.dtype)])
    return flat.reshape(-1, LANES)


def _unpack_small(packed, shapes):
    flat = packed.reshape(-1)
    out, off = [], 0
    for shp in shapes:
        n = int(np.prod(shp))
        out.append(flat[off:off + n].reshape(shp))
        off += n
    return out


def _shards_of(full, axis):
    shp = full.shape
    cut = shp[:axis] + (N_DEV, shp[axis] // N_DEV) + shp[axis + 1:]
    return jnp.moveaxis(full.reshape(cut), axis, 0)


def _from_shards(shards, axis):
    full = list(shards.shape[1:])
    full[axis] *= N_DEV
    return jnp.moveaxis(shards, 0, axis).reshape(full)


def _rows2d(a):
    return a.reshape(-1, a.shape[-1])


def _layer_fwd(x, u, lw, tabs):
    cos_a, sin_a, cos_b, sin_b = tabs
    z = _matmul(u, lw["w_in"], "nn", "mm_in")
    qa, ka, va, cqn, ckvn, krr = _prep_a_fwd(z, lw["gq2"], lw["gk2"], lw["gqa"], lw["gkva"], cos_a, sin_a, cos_b, sin_b)
    qb = _matmul(cqn, lw["w_q_up"], "nn", "mm_q_up")
    kvb = _matmul(ckvn, lw["w_kv_up"], "nn", "mm_kv_up")
    q_b, k_b, v_b = _prep_b_fwd(qb, kvb, krr, cos_b, sin_b)
    ya, lse_a = _attn_fwd(qa, ka, va, True, "gqa_fwd")
    yb, lse_b = _attn_fwd(q_b, k_b, v_b, False, "mla_fwd")
    ta = _matmul(ya, lw["w_branch_a"], "nn", "mm_branch_a")
    tb = _matmul(yb, lw["w_branch_b"], "nn", "mm_branch_b")
    merged = _merge_fwd(z, lw["b_gate"], ta, tb)
    m = _matmul(merged, lw["w_o"], "nn", "mm_o")
    x2, u2 = _res_norm_fwd(x, m, lw["post_mix_g"], lw["pre_ffn_g"])
    h, a = _matmul(u2, lw["w_ffn_up"], "nn", "mm_ffn_up", post="relu2")
    f = _matmul(a, lw["w_ffn_down"], "nn", "mm_ffn_down")
    x3, u_next = _res_norm_fwd(x2, f, lw["post_ffn_g"], lw["next_pre_mix_g"])
    saved = dict(u=u, z=z, qa=qa, ka=ka, va=va, cqn=cqn, ckvn=ckvn, q_b=q_b, k_b=k_b, v_b=v_b, ya=ya, yb=yb,
                 lse_a=lse_a, lse_b=lse_b, ta=ta, tb=tb, merged=merged, m=m, x2=x2, u2=u2, h=h, a=a, f=f, x3=x3)
    return x3, u_next, saved


def _layer_bwd(dx3, du_next, lw, sv, tabs):
    cos_a, sin_a, cos_b, sin_b = tabs
    g = {}
    dx3, df, dg4, dg1n = _res_norm_bwd(sv["x3"], sv["f"], lw["post_ffn_g"], lw["next_pre_mix_g"], dx3, du_next)
    g["post_ffn_g"], g["next_pre_mix_g"] = dg4, dg1n
    dh = _matmul(df, lw["w_ffn_down"], "nt", "mm_d_h", post="relu2_bwd", h=sv["h"])
    g["w_ffn_down"] = _matmul(sv["a"], df, "tn", "mm_dw_ffn_down")
    du2 = _matmul(dh, lw["w_ffn_up"], "nt", "mm_d_u2")
    g["w_ffn_up"] = _matmul(sv["u2"], dh, "tn", "mm_dw_ffn_up")
    dx2, dm, dg2, dg3 = _res_norm_bwd(sv["x2"], sv["m"], lw["post_mix_g"], lw["pre_ffn_g"], dx3, du2)
    g["post_mix_g"], g["pre_ffn_g"] = dg2, dg3
    dmg = _matmul(dm, lw["w_o"], "nt", "mm_d_merged")
    g["w_o"] = _matmul(sv["merged"], dm, "tn", "mm_dw_o")
    dta, dtb, dzg_a, dzg_b, db_a, db_b = _merge_bwd(dmg, sv["z"], lw["b_gate"], sv["ta"], sv["tb"])
    g["b_gate"] = jnp.concatenate([db_a, db_b], axis=-1)
    dya = _matmul(dta, lw["w_branch_a"], "nt", "mm_d_ya")
    g["w_branch_a"] = _matmul(sv["ya"], dta, "tn", "mm_dw_branch_a")
    dyb = _matmul(dtb, lw["w_branch_b"], "nt", "mm_d_yb")
    g["w_branch_b"] = _matmul(sv["yb"], dtb, "tn", "mm_dw_branch_b")
    delta_a, lse_a = _attn_stats(dya, sv["ya"], sv["lse_a"])
    delta_b, lse_b = _attn_stats(dyb, sv["yb"], sv["lse_b"])
    dqa, dka4, dva4 = _attn_bwd(sv["qa"], sv["ka"], sv["va"], dya, lse_a, delta_a, True, "gqa_bwd")
    dq_b, dk_b, dv_b = _attn_bwd(sv["q_b"], sv["k_b"], sv["v_b"], dyb, lse_b, delta_b, False, "mla_bwd")
    dqb, dkvb, dkr = _prep_b_bwd(dq_b, dk_b, dv_b, cos_b, sin_b)
    dcqn = _matmul(dqb, lw["w_q_up"], "nt", "mm_d_cqn")
    g["w_q_up"] = _matmul(sv["cqn"], dqb, "tn", "mm_dw_q_up")
    dckvn = _matmul(dkvb, lw["w_kv_up"], "nt", "mm_d_ckvn")
    g["w_kv_up"] = _matmul(sv["ckvn"], dkvb, "tn", "mm_dw_kv_up")
    dz, dgq, dgk, dgqa, dgkva = _prep_a_bwd(sv["z"], dqa, dka4, dva4, dcqn, dckvn, dkr, dzg_a, dzg_b, lw["gq2"],
                                            lw["gk2"], lw["gqa"], lw["gkva"], cos_a, sin_a)
    g["q_norm_g"], g["k_norm_g"], g["q_a_norm_g"], g["kv_a_norm_g"] = dgq, dgk, dgqa, dgkva
    du = _matmul(dz, lw["w_in"], "nt", "mm_d_u")
    g["w_in"] = _matmul(sv["u"], dz, "tn", "mm_dw_in")
    return dx2, du, g


def kernel(x, w_in, b_gate, q_norm_g, k_norm_g, q_a_norm_g, kv_a_norm_g, w_q_up, w_kv_up, w_branch_a, w_branch_b, w_o, w_ffn_up, w_ffn_down, pre_mix_g, post_mix_g, pre_ffn_g, post_ffn_g, loss_target, m_w_in, m_b_gate, m_q_norm_g, m_k_norm_g, m_q_a_norm_g, m_kv_a_norm_g, m_w_q_up, m_w_kv_up, m_w_branch_a, m_w_branch_b, m_w_o, m_w_ffn_up, m_w_ffn_down, m_pre_mix_g, m_post_mix_g, m_pre_ffn_g, m_post_ffn_g, v_w_in, v_b_gate, v_q_norm_g, v_k_norm_g, v_q_a_norm_g, v_kv_a_norm_g, v_w_q_up, v_w_kv_up, v_w_branch_a, v_w_branch_b, v_w_o, v_w_ffn_up, v_w_ffn_down, v_pre_mix_g, v_post_mix_g, v_pre_ffn_g, v_post_ffn_g):
    weights = dict(zip(WEIGHT_NAMES, (w_in, b_gate, q_norm_g, k_norm_g, q_a_norm_g, kv_a_norm_g, w_q_up, w_kv_up,
                                      w_branch_a, w_branch_b, w_o, w_ffn_up, w_ffn_down, pre_mix_g, post_mix_g,
                                      pre_ffn_g, post_ffn_g)))
    mom_m = dict(zip(WEIGHT_NAMES, (m_w_in, m_b_gate, m_q_norm_g, m_k_norm_g, m_q_a_norm_g, m_kv_a_norm_g, m_w_q_up,
                                    m_w_kv_up, m_w_branch_a, m_w_branch_b, m_w_o, m_w_ffn_up, m_w_ffn_down,
                                    m_pre_mix_g, m_post_mix_g, m_pre_ffn_g, m_post_ffn_g)))
    mom_v = dict(zip(WEIGHT_NAMES, (v_w_in, v_b_gate, v_q_norm_g, v_k_norm_g, v_q_a_norm_g, v_kv_a_norm_g, v_w_q_up,
                                    v_w_kv_up, v_w_branch_a, v_w_branch_b, v_w_o, v_w_ffn_up, v_w_ffn_down,
                                    v_pre_mix_g, v_post_mix_g, v_pre_ffn_g, v_post_ffn_g)))
    assert x.shape[0] == 1 and x.shape[2] == D_MODEL, x.shape
    n_layers = w_in.shape[0]
    t = x.shape[1]
    x0 = x.reshape(t, D_MODEL)
    target = loss_target.reshape(t, D_MODEL)
    shard_shapes = {n: weights[n].shape for n in BIG_NAMES}
    small_shapes = [weights[n].shape for n in SMALL_NAMES]

    gathered = _all_gather([weights[n].astype(BF16) for n in BIG_NAMES])
    full = {n: _from_shards(g, SHARD_AXIS[n]) for n, g in zip(BIG_NAMES, gathered)}
    lw_all = _layout_weights(full)
    lw_all["b_gate"] = b_gate.reshape(n_layers, 1, 2 * D_MODEL)
    lw_all["gq2"] = jnp.tile(q_norm_g, (1, 2)).reshape(n_layers, 1, LANES)
    lw_all["gk2"] = jnp.tile(k_norm_g, (1, 2)).reshape(n_layers, 1, LANES)
    lw_all["gqa"] = q_a_norm_g.reshape(n_layers, 1, MLA_Q_RANK)
    lw_all["gkva"] = kv_a_norm_g.reshape(n_layers, 1, MLA_KV_RANK)
    for n in ("post_mix_g", "pre_ffn_g", "post_ffn_g"):
        lw_all[n] = weights[n]
    lw_all["next_pre_mix_g"] = jnp.roll(pre_mix_g, -1, axis=0)

    tabs = _rope_tables(t)
    u0 = _rms_fwd(x0, pre_mix_g[0])

    layer_w = [{n: a[l] for n, a in lw_all.items()} for l in range(n_layers)]
    xc, uc, saved = x0, u0, []
    for l in range(n_layers):
        xc, uc, sv = _layer_fwd(xc, uc, layer_w[l], tabs)
        saved.append(sv)
    dy, loss_acc = _loss_grad(xc, target)
    loss = lax.psum(0.5 * jnp.sum(loss_acc) / D_MODEL, ("x", "y", "c"))

    dx0, du0, layer_g = dy, jnp.zeros((t, D_MODEL), F32), [None] * n_layers
    for l in reversed(range(n_layers)):
        dx0, du0, layer_g[l] = _layer_bwd(dx0, du0, layer_w[l], saved[l], tabs)
    grads = {n: jnp.stack([g[n] for g in layer_g]) for n in layer_g[0]}
    grad_x, dg1_first = _rms_bwd(x0, pre_mix_g[0], dx0, du0)

    big_grads = _unlayout_grads({n: grads[n] for n in BIG_NAMES})
    fold = lambda a: a.sum(axis=1)
    dgq = fold(grads["q_norm_g"]).reshape(n_layers, 2, HEAD_DIM).sum(axis=1)
    dgk = fold(grads["k_norm_g"]).reshape(n_layers, 2, HEAD_DIM).sum(axis=1)
    dg1 = jnp.concatenate([fold(dg1_first[None]), fold(grads["next_pre_mix_g"])[:-1]], axis=0)
    small_grads = {
        "b_gate": fold(grads["b_gate"]), "q_norm_g": dgq, "k_norm_g": dgk, "q_a_norm_g": fold(grads["q_a_norm_g"]),
        "kv_a_norm_g": fold(grads["kv_a_norm_g"]), "pre_mix_g": dg1, "post_mix_g": fold(grads["post_mix_g"]),
        "pre_ffn_g": fold(grads["pre_ffn_g"]), "post_ffn_g": fold(grads["post_ffn_g"]),
    }
    small_packed = _pack_small([small_grads[n] for n in SMALL_NAMES])
    sends = [_shards_of(big_grads[n], SHARD_AXIS[n]).reshape((N_DEV,) + _rows2d(weights[n]).shape)
             for n in BIG_NAMES]
    sends.append(jnp.broadcast_to(small_packed[None], (N_DEV,) + small_packed.shape))
    halves = _pair_exchange(sends)
    core = lax.axis_index("c").astype(jnp.int32).reshape(1)
    recvs = _chip_exchange([_pair_add(s, h, core) for s, h in zip(sends, halves)])

    results = {}
    for n, recv in zip(BIG_NAMES, recvs):
        res = _adamw(recv, _rows2d(weights[n]), _rows2d(mom_m[n]), _rows2d(mom_v[n]))
        results[n] = [r.reshape(shard_shapes[n]) for r in res]
    res = _adamw(recvs[-1], *[_pack_small([d[n] for n in SMALL_NAMES]) for d in (weights, mom_m, mom_v)])
    for kind, packed_out in enumerate(res):
        for n, val in zip(SMALL_NAMES, _unpack_small(packed_out, small_shapes)):
            results.setdefault(n, [None] * 4)[kind] = val
    outs = [results[n][kind] for kind in range(4) for n in WEIGHT_NAMES]
    return (loss, grad_x.reshape(x.shape), *outs)
```

```python
import math

import jax
import jax.numpy as jnp
import numpy as np
from jax import lax
from jax.experimental import pallas as pl
from jax.experimental.pallas import tpu as pltpu

F32 = jnp.float32
BF16 = jnp.bfloat16

D_MODEL = 1024
GRID_W = 64
ROPE_THETA = 10000.0
EPS = 1e-6
GQA_HEADS = 8
GQA_KV_HEADS = 2
GQA_GROUP = GQA_HEADS // GQA_KV_HEADS
HEAD_DIM = 64
MLA_HEADS = 8
MLA_ROPE_DIM = 32
MLA_QK_DIM = 96
MLA_Q_RANK = 384
MLA_KV_RANK = 256
GQA_SCALE = 1.0 / math.sqrt(HEAD_DIM)
MLA_SCALE = 1.0 / math.sqrt(MLA_QK_DIM)
LOG2E = math.log2(math.e)
LN2 = math.log(2.0)

ADAM_LR = 0.001
ADAM_B1 = 0.9
ADAM_B2 = 0.999
ADAM_EPS = 1e-08
ADAM_WD = 0.01
ADAM_STEP = 10

N_DEV = 8
LANES = 128
SUBLANES = 8
VMEM_LIMIT = 48 * 1024 * 1024

Z_QA, Z_KA, Z_VA, Z_CQ, Z_CKV, Z_KR, Z_GATE = 0, 512, 640, 768, 1152, 1408, 1536
Z_ATT_W = 1536
Z_W = 3584
KR_LANE0 = 64

WEIGHT_NAMES = ("w_in", "b_gate", "q_norm_g", "k_norm_g", "q_a_norm_g", "kv_a_norm_g", "w_q_up", "w_kv_up",
                "w_branch_a", "w_branch_b", "w_o", "w_ffn_up", "w_ffn_down", "pre_mix_g", "post_mix_g",
                "pre_ffn_g", "post_ffn_g")
SHARD_AXIS = {"w_in": 2, "w_q_up": 2, "w_kv_up": 2, "w_branch_a": 2, "w_branch_b": 2, "w_o": 1, "w_ffn_up": 2,
              "w_ffn_down": 1}
BIG_NAMES = tuple(n for n in WEIGHT_NAMES if n in SHARD_AXIS)
SMALL_NAMES = tuple(n for n in WEIGHT_NAMES if n not in SHARD_AXIS)
ADAM_BLOCK_ELEMS = 256 * 1024
MM_TILE = 1024
MM_TILE_TOKENS = 2048
MM_TILE_K = 2048
PREP_ROWS = 512
ATTN_TQ = 1024
ATTN_TK = 1024


def _params(*semantics):
    return pltpu.CompilerParams(dimension_semantics=semantics, vmem_limit_bytes=VMEM_LIMIT)


def _tile(n, pref):
    if n <= pref:
        return n
    t = (pref // LANES) * LANES
    while n % t:
        t -= LANES
    return t


def _fold8(t):
    return t.reshape(t.shape[0] // SUBLANES, SUBLANES, t.shape[1]).sum(axis=0)


_DIMS = {"nn": ((1,), (0,)), "nt": ((1,), (1,)), "tn": ((0,), (0,))}


def _matmul(a, b, mode, name, post=None, h=None):
    out_dt = F32 if mode == "tn" else BF16
    if mode == "nn":
        (m, k), n = a.shape, b.shape[1]
    elif mode == "nt":
        (m, k), n = a.shape, b.shape[0]
    else:
        (k, m), n = a.shape, b.shape[1]
    tm = _tile(m, MM_TILE_TOKENS if mode != "tn" and k <= MM_TILE else MM_TILE)
    tn, tk = _tile(n, MM_TILE), _tile(k, MM_TILE_K)
    nk = k // tk
    dims = (_DIMS[mode], ((), ()))
    n_in = 3 if post == "relu2_bwd" else 2
    n_out = 2 if post == "relu2" else 1

    def body(*refs):
        a_ref, b_ref = refs[:2]
        o_refs, acc_ref = refs[n_in:n_in + n_out], refs[-1]

        def finish(val):
            if post == "relu2":
                o_refs[0][...] = val.astype(out_dt)
                r = jnp.maximum(val, 0.0)
                o_refs[1][...] = (r * r).astype(BF16)
            elif post == "relu2_bwd":
                o_refs[0][...] = (val * (2.0 * jnp.maximum(refs[2][...].astype(F32), 0.0))).astype(BF16)
            else:
                o_refs[0][...] = val.astype(out_dt)

        prod = lax.dot_general(a_ref[...], b_ref[...], dims, preferred_element_type=F32)
        if nk == 1:
            finish(prod)
        else:
            kk = pl.program_id(2)

            @pl.when(kk == 0)
            def _():
                acc_ref[...] = prod

            @pl.when(kk > 0)
            def _():
                acc_ref[...] += prod

            @pl.when(kk == nk - 1)
            def _():
                finish(acc_ref[...])

    if mode == "tn":
        a_spec = pl.BlockSpec((tk, tm), lambda i, j, kk: (kk, i))
    else:
        a_spec = pl.BlockSpec((tm, tk), lambda i, j, kk: (i, kk))
    if mode == "nt":
        b_spec = pl.BlockSpec((tn, tk), lambda i, j, kk: (j, kk))
    else:
        b_spec = pl.BlockSpec((tk, tn), lambda i, j, kk: (kk, j))
    o_spec = pl.BlockSpec((tm, tn), lambda i, j, kk: (i, j))
    main_out, bf16_out = jax.ShapeDtypeStruct((m, n), out_dt), jax.ShapeDtypeStruct((m, n), BF16)
    out_shape = {None: main_out, "relu2": [main_out, bf16_out], "relu2_bwd": bf16_out}[post]
    return pl.pallas_call(
        body,
        name=name,
        grid=(m // tm, n // tn, nk),
        in_specs=[a_spec, b_spec] + ([o_spec] if post == "relu2_bwd" else []),
        out_specs=[o_spec, o_spec] if post == "relu2" else o_spec,
        out_shape=out_shape,
        scratch_shapes=[pltpu.VMEM((tm, tn), F32)],
        compiler_params=_params("parallel", "parallel", "arbitrary"),
    )(*((a, b, h) if post == "relu2_bwd" else (a, b)))


def _rinv(x):
    return lax.rsqrt(jnp.mean(x * x, axis=-1, keepdims=True) + EPS)


def _rms_bwd_rows(x, g, dy):
    r = _rinv(x)
    xh = x * r
    dxh = dy * g
    dx = r * (dxh - xh * jnp.mean(dxh * xh, axis=-1, keepdims=True))
    return dx, dy * xh


def _row_spec(tm, c):
    return pl.BlockSpec((tm, c), lambda i: (i, 0))


def _vec_spec(c):
    return pl.BlockSpec((1, c), lambda i: (0, 0))


def _acc_spec(c):
    return pl.BlockSpec((SUBLANES, c), lambda i: (0, 0))


def _rms_fwd(x, g):
    t, d = x.shape
    tm = _tile(t, 512)

    def body(x_ref, g_ref, o_ref):
        xv = x_ref[...]
        o_ref[...] = (xv * _rinv(xv) * g_ref[...]).astype(BF16)

    return pl.pallas_call(
        body, name="rms_fwd", grid=(t // tm,),
        in_specs=[_row_spec(tm, d), _vec_spec(d)], out_specs=_row_spec(tm, d),
        out_shape=jax.ShapeDtypeStruct((t, d), BF16), compiler_params=_params("parallel"),
    )(x, g.reshape(1, d))


def _rms_bwd(x, g, dres, dy):
    t, d = x.shape
    tm = _tile(t, 512)

    def body(x_ref, g_ref, dres_ref, dy_ref, dx_ref, dg_ref):
        dx, dgc = _rms_bwd_rows(x_ref[...], g_ref[...], dy_ref[...].astype(F32))
        dx_ref[...] = dres_ref[...] + dx

        @pl.when(pl.program_id(0) == 0)
        def _():
            dg_ref[...] = jnp.zeros_like(dg_ref)

        dg_ref[...] += _fold8(dgc)

    return pl.pallas_call(
        body, name="rms_bwd", grid=(t // tm,),
        in_specs=[_row_spec(tm, d), _vec_spec(d), _row_spec(tm, d), _row_spec(tm, d)],
        out_specs=[_row_spec(tm, d), _acc_spec(d)],
        out_shape=[jax.ShapeDtypeStruct((t, d), F32), jax.ShapeDtypeStruct((SUBLANES, d), F32)],
        compiler_params=_params("arbitrary"),
    )(x, g.reshape(1, d), dres, dy)


def _res_norm_fwd(x, m, g_post, g_next):
    t, d = x.shape
    tm = _tile(t, 512)

    def body(x_ref, m_ref, gp_ref, gn_ref, x2_ref, u2_ref):
        mv = m_ref[...].astype(F32)
        x2 = x_ref[...] + mv * _rinv(mv) * gp_ref[...]
        x2_ref[...] = x2
        u2_ref[...] = (x2 * _rinv(x2) * gn_ref[...]).astype(BF16)

    return pl.pallas_call(
        body, name="res_norm_fwd", grid=(t // tm,),
        in_specs=[_row_spec(tm, d), _row_spec(tm, d), _vec_spec(d), _vec_spec(d)],
        out_specs=[_row_spec(tm, d), _row_spec(tm, d)],
        out_shape=[jax.ShapeDtypeStruct((t, d), F32), jax.ShapeDtypeStruct((t, d), BF16)],
        compiler_params=_params("parallel"),
    )(x, m, g_post.reshape(1, d), g_next.reshape(1, d))


def _res_norm_bwd(x2, m, g_post, g_next, dx2_in, du2):
    t, d = x2.shape
    tm = _tile(t, 512)

    def body(x2_ref, m_ref, gp_ref, gn_ref, dx2in_ref, du2_ref, dx2_ref, dm_ref, dgp_ref, dgn_ref):
        dxn, dgn_c = _rms_bwd_rows(x2_ref[...], gn_ref[...], du2_ref[...].astype(F32))
        dx2 = dx2in_ref[...] + dxn
        dx2_ref[...] = dx2
        dm, dgp_c = _rms_bwd_rows(m_ref[...].astype(F32), gp_ref[...], dx2)
        dm_ref[...] = dm.astype(BF16)

        @pl.when(pl.program_id(0) == 0)
        def _():
            dgp_ref[...] = jnp.zeros_like(dgp_ref)
            dgn_ref[...] = jnp.zeros_like(dgn_ref)

        dgp_ref[...] += _fold8(dgp_c)
        dgn_ref[...] += _fold8(dgn_c)

    return pl.pallas_call(
        body, name="res_norm_bwd", grid=(t // tm,),
        in_specs=[_row_spec(tm, d), _row_spec(tm, d), _vec_spec(d), _vec_spec(d), _row_spec(tm, d), _row_spec(tm, d)],
        out_specs=[_row_spec(tm, d), _row_spec(tm, d), _acc_spec(d), _acc_spec(d)],
        out_shape=[jax.ShapeDtypeStruct((t, d), F32), jax.ShapeDtypeStruct((t, d), BF16),
                   jax.ShapeDtypeStruct((SUBLANES, d), F32), jax.ShapeDtypeStruct((SUBLANES, d), F32)],
        compiler_params=_params("arbitrary"),
    )(x2, m, g_post.reshape(1, d), g_next.reshape(1, d), dx2_in, du2)


def _rope_tables(t):
    rows = t // GRID_W
    row = jnp.repeat(jnp.arange(rows, dtype=F32), GRID_W)
    col = jnp.tile(jnp.arange(GRID_W, dtype=F32), rows)

    def tab(rot_dim):
        half = rot_dim // 2
        inv = ROPE_THETA ** (-jnp.arange(0, half, 2, dtype=F32) / half)
        ar = row[:, None] * inv[None, :]
        ac = col[:, None] * inv[None, :]
        ang = jnp.concatenate([ar, ar, ac, ac], axis=-1)
        q = half // 2
        sign = np.tile(np.concatenate([-np.ones(q, np.float32), np.ones(q, np.float32)]), 2)
        return jnp.cos(ang), jnp.sin(ang) * sign[None, :]

    ca, sa = tab(HEAD_DIM)
    cb, sb = tab(MLA_ROPE_DIM)
    one = jnp.ones((t, 1), F32)
    cos_b = jnp.concatenate([one * jnp.ones((1, KR_LANE0), F32), cb, one * jnp.ones((1, 32), F32)], axis=-1)
    sin_b = jnp.concatenate([jnp.zeros((t, KR_LANE0), F32), sb, jnp.zeros((t, 32), F32)], axis=-1)
    return jnp.tile(ca, (1, GQA_HEADS)), jnp.tile(sa, (1, GQA_HEADS)), cos_b, sin_b


def _swap_halves(x, sh):
    lane = lax.broadcasted_iota(jnp.int32, x.shape, 1)
    up = pltpu.roll(x, LANES - sh, 1)
    dn = pltpu.roll(x, sh, 1)
    return jnp.where((lane & (2 * sh - 1)) < sh, up, dn)


def _rope(x, cos, sin_s, sh):
    return x * cos + _swap_halves(x, sh) * sin_s


def _rope_bwd(dy, cos, sin_s, sh):
    return dy * cos + _swap_halves(dy * sin_s, sh)


def _lo_mask(shape):
    return lax.broadcasted_iota(jnp.int32, shape, 1) < HEAD_DIM


def _half_mean(t, lo):
    s_lo = jnp.sum(jnp.where(lo, t, 0.0), axis=-1, keepdims=True)
    s_hi = jnp.sum(jnp.where(lo, 0.0, t), axis=-1, keepdims=True)
    return jnp.where(lo, s_lo, s_hi) * (1.0 / HEAD_DIM)


def _head_norm(x, g2):
    lo = _lo_mask(x.shape)
    r = lax.rsqrt(_half_mean(x * x, lo) + EPS)
    return x * r * g2


def _head_norm_bwd(x, g2, dy):
    lo = _lo_mask(x.shape)
    r = lax.rsqrt(_half_mean(x * x, lo) + EPS)
    xh = x * r
    dxh = dy * g2
    dx = r * (dxh - xh * _half_mean(dxh * xh, lo))
    return dx, dy * xh


def _prep_a_fwd(z, gq2, gk2, gqa, gkva, cos_a, sin_a, cos_b, sin_b):
    t = z.shape[0]
    tm = _tile(t, PREP_ROWS)

    def body(z_ref, gq_ref, gk_ref, gqa_ref, gkva_ref, ca_ref, sa_ref, cb_ref, sb_ref,
             qa_ref, ka_ref, va_ref, cqn_ref, ckvn_ref, krr_ref):
        def zf(lo, hi):
            return z_ref[:, lo:hi].astype(F32)

        for j in range(4):
            cols = slice(LANES * j, LANES * (j + 1))
            y = _rope(_head_norm(zf(LANES * j, LANES * (j + 1)), gq_ref[...]), ca_ref[:, cols], sa_ref[:, cols], 16)
            qa_ref[:, cols] = (y * (GQA_SCALE * LOG2E)).astype(BF16)
        y = _rope(_head_norm(zf(Z_KA, Z_VA), gk_ref[...]), ca_ref[:, :LANES], sa_ref[:, :LANES], 16)
        ka_ref[...] = y.astype(BF16)
        va_ref[...] = z_ref[:, Z_VA:Z_CQ].astype(BF16)
        cq = zf(Z_CQ, Z_CKV)
        cqn_ref[...] = (cq * _rinv(cq) * gqa_ref[...]).astype(BF16)
        ckv = zf(Z_CKV, Z_KR)
        ckvn_ref[...] = (ckv * _rinv(ckv) * gkva_ref[...]).astype(BF16)
        krr_ref[...] = _rope(zf(Z_KR, Z_GATE), cb_ref[...], sb_ref[...], 8)

    return pl.pallas_call(
        body, name="prep_a_fwd", grid=(t // tm,),
        in_specs=[_row_spec(tm, Z_ATT_W), _vec_spec(LANES), _vec_spec(LANES), _vec_spec(MLA_Q_RANK),
                  _vec_spec(MLA_KV_RANK), _row_spec(tm, 512), _row_spec(tm, 512), _row_spec(tm, LANES),
                  _row_spec(tm, LANES)],
        out_specs=[_row_spec(tm, 512), _row_spec(tm, LANES), _row_spec(tm, LANES), _row_spec(tm, MLA_Q_RANK),
                   _row_spec(tm, MLA_KV_RANK), _row_spec(tm, LANES)],
        out_shape=[jax.ShapeDtypeStruct((t, 512), BF16), jax.ShapeDtypeStruct((t, LANES), BF16),
                   jax.ShapeDtypeStruct((t, LANES), BF16), jax.ShapeDtypeStruct((t, MLA_Q_RANK), BF16),
                   jax.ShapeDtypeStruct((t, MLA_KV_RANK), BF16), jax.ShapeDtypeStruct((t, LANES), F32)],
        compiler_params=_params("parallel"),
    )(z, gq2, gk2, gqa, gkva, cos_a, sin_a, cos_b, sin_b)


def _prep_a_bwd(z, dqa, dka4, dva4, dcqn, dckvn, dkr, dzga, dzgb, gq2, gk2, gqa, gkva, cos_a, sin_a):
    t = z.shape[0]
    tm = _tile(t, PREP_ROWS)

    def body(z_ref, dqa_ref, dka_ref, dva_ref, dcqn_ref, dckvn_ref, dkr_ref, dzga_ref, dzgb_ref, gq_ref, gk_ref,
             gqa_ref, gkva_ref, ca_ref, sa_ref, dz_ref, dgq_ref, dgk_ref, dgqa_ref, dgkva_ref):
        @pl.when(pl.program_id(0) == 0)
        def _():
            dgq_ref[...] = jnp.zeros_like(dgq_ref)
            dgk_ref[...] = jnp.zeros_like(dgk_ref)
            dgqa_ref[...] = jnp.zeros_like(dgqa_ref)
            dgkva_ref[...] = jnp.zeros_like(dgkva_ref)

        def zf(lo, hi):
            return z_ref[:, lo:hi].astype(F32)

        dgq = jnp.zeros((SUBLANES, LANES), F32)
        for j in range(4):
            cols = slice(LANES * j, LANES * (j + 1))
            dy = _rope_bwd(dqa_ref[:, cols] * GQA_SCALE, ca_ref[:, cols], sa_ref[:, cols], 16)
            dx, dgc = _head_norm_bwd(zf(LANES * j, LANES * (j + 1)), gq_ref[...], dy)
            dz_ref[:, cols] = dx.astype(BF16)
            dgq = dgq + _fold8(dgc)
        dgq_ref[...] += dgq
        dk = (dka_ref[0] + dka_ref[1] + dka_ref[2] + dka_ref[3]).T * LN2
        dy = _rope_bwd(dk, ca_ref[:, :LANES], sa_ref[:, :LANES], 16)
        dx, dgc = _head_norm_bwd(zf(Z_KA, Z_VA), gk_ref[...], dy)
        dz_ref[:, Z_KA:Z_VA] = dx.astype(BF16)
        dgk_ref[...] += _fold8(dgc)
        dz_ref[:, Z_VA:Z_CQ] = (dva_ref[0] + dva_ref[1] + dva_ref[2] + dva_ref[3]).T.astype(BF16)
        dx, dgc = _rms_bwd_rows(zf(Z_CQ, Z_CKV), gqa_ref[...], dcqn_ref[...].astype(F32))
        dz_ref[:, Z_CQ:Z_CKV] = dx.astype(BF16)
        dgqa_ref[...] += _fold8(dgc)
        dx, dgc = _rms_bwd_rows(zf(Z_CKV, Z_KR), gkva_ref[...], dckvn_ref[...].astype(F32))
        dz_ref[:, Z_CKV:Z_KR] = dx.astype(BF16)
        dgkva_ref[...] += _fold8(dgc)
        dz_ref[:, Z_KR:Z_GATE] = dkr_ref[...].astype(BF16)
        dz_ref[:, Z_GATE:Z_GATE + D_MODEL] = dzga_ref[...]
        dz_ref[:, Z_GATE + D_MODEL:Z_W] = dzgb_ref[...]

    part = pl.BlockSpec((4, LANES, tm), lambda i: (0, 0, i))
    return pl.pallas_call(
        body, name="prep_a_bwd", grid=(t // tm,),
        in_specs=[_row_spec(tm, Z_ATT_W), _row_spec(tm, 512), part, part, _row_spec(tm, MLA_Q_RANK),
                  _row_spec(tm, MLA_KV_RANK), _row_spec(tm, LANES), _row_spec(tm, D_MODEL), _row_spec(tm, D_MODEL),
                  _vec_spec(LANES),
                  _vec_spec(LANES), _vec_spec(MLA_Q_RANK), _vec_spec(MLA_KV_RANK), _row_spec(tm, 512),
                  _row_spec(tm, 512)],
        out_specs=[_row_spec(tm, Z_W), _acc_spec(LANES), _acc_spec(LANES), _acc_spec(MLA_Q_RANK),
                   _acc_spec(MLA_KV_RANK)],
        out_shape=[jax.ShapeDtypeStruct((t, Z_W), BF16), jax.ShapeDtypeStruct((SUBLANES, LANES), F32),
                   jax.ShapeDtypeStruct((SUBLANES, LANES), F32), jax.ShapeDtypeStruct((SUBLANES, MLA_Q_RANK), F32),
                   jax.ShapeDtypeStruct((SUBLANES, MLA_KV_RANK), F32)],
        compiler_params=_params("arbitrary"),
    )(z, dqa, dka4, dva4, dcqn, dckvn, dkr, dzga, dzgb, gq2, gk2, gqa, gkva, cos_a, sin_a)


def _prep_b_fwd(qb, kvb, krr, cos_b, sin_b):
    t = qb.shape[0]
    tm = _tile(t, PREP_ROWS)

    def body(qb_ref, kvb_ref, krr_ref, cb_ref, sb_ref, q_ref, k_ref, v_ref):
        for h in range(MLA_HEADS):
            cols = slice(LANES * h, LANES * (h + 1))
            qh = _rope(qb_ref[:, cols].astype(F32), cb_ref[...], sb_ref[...], 8)
            q_ref[:, cols] = (qh * (MLA_SCALE * LOG2E)).astype(BF16)
            k_ref[:, cols] = (kvb_ref[:, cols].astype(F32) + krr_ref[...]).astype(BF16)
        v_ref[...] = kvb_ref[:, 1024:1536].astype(BF16)

    return pl.pallas_call(
        body, name="prep_b_fwd", grid=(t // tm,),
        in_specs=[_row_spec(tm, 1024), _row_spec(tm, 1536), _row_spec(tm, LANES), _row_spec(tm, LANES),
                  _row_spec(tm, LANES)],
        out_specs=[_row_spec(tm, 1024), _row_spec(tm, 1024), _row_spec(tm, 512)],
        out_shape=[jax.ShapeDtypeStruct((t, 1024), BF16), jax.ShapeDtypeStruct((t, 1024), BF16),
                   jax.ShapeDtypeStruct((t, 512), BF16)],
        compiler_params=_params("parallel"),
    )(qb, kvb, krr, cos_b, sin_b)


def _prep_b_bwd(dq, dk, dv, cos_b, sin_b):
    t = dq.shape[0]
    tm = _tile(t, PREP_ROWS)

    def body(dq_ref, dk_ref, dv_ref, cb_ref, sb_ref, dqb_ref, dkvb_ref, dkr_ref):
        dkr = jnp.zeros((tm, LANES), F32)
        for h in range(MLA_HEADS):
            cols = slice(LANES * h, LANES * (h + 1))
            dqb_ref[:, cols] = _rope_bwd(dq_ref[:, cols] * MLA_SCALE, cb_ref[...], sb_ref[...], 8).astype(BF16)
            dkh = dk_ref[cols, :].T * LN2
            dkvb_ref[:, cols] = dkh.astype(BF16)
            dkr = dkr + dkh
        for j in range(MLA_HEADS // 2):
            dkvb_ref[:, 1024 + LANES * j:1024 + LANES * (j + 1)] = dv_ref[LANES * j:LANES * (j + 1), :].T.astype(BF16)
        dkr_ref[...] = _rope_bwd(dkr, cb_ref[...], sb_ref[...], 8)

    return pl.pallas_call(
        body, name="prep_b_bwd", grid=(t // tm,),
        in_specs=[_row_spec(tm, 1024), pl.BlockSpec((1024, tm), lambda i: (0, i)),
                  pl.BlockSpec((512, tm), lambda i: (0, i)), _row_spec(tm, LANES),
                  _row_spec(tm, LANES)],
        out_specs=[_row_spec(tm, 1024), _row_spec(tm, 1536), _row_spec(tm, LANES)],
        out_shape=[jax.ShapeDtypeStruct((t, 1024), BF16), jax.ShapeDtypeStruct((t, 1536), BF16),
                   jax.ShapeDtypeStruct((t, LANES), F32)],
        compiler_params=_params("parallel"),
    )(dq, dk, dv, cos_b, sin_b)


_NT = (((1,), (1,)), ((), ()))
_NN = (((1,), (0,)), ((), ()))
_TN = (((0,), (0,)), ((), ()))


def _head_operands(qv, kv, i, shared_k):
    if shared_k:
        lo = _lo_mask(qv.shape)
        keep = lo if i == 0 else jnp.logical_not(lo)
        return jnp.where(keep, qv, jnp.zeros_like(qv)), kv
    cols = slice(LANES * i, LANES * (i + 1))
    return qv[:, cols], kv[:, cols]


def _attn_specs(shared_k, tq, tk, q_of, k_of):
    wq = LANES if shared_k else 2 * LANES
    q_spec = pl.BlockSpec((tq, wq), lambda *g: (q_of(*g), g[0]))
    if shared_k:
        k_spec = pl.BlockSpec((tk, LANES), lambda *g: (k_of(*g), 0))
        v_spec = pl.BlockSpec((tk, LANES), lambda *g: (k_of(*g), 0))
    else:
        k_spec = pl.BlockSpec((tk, wq), lambda *g: (k_of(*g), g[0]))
        v_spec = pl.BlockSpec((tk, LANES), lambda *g: (k_of(*g), g[0]))
    return wq, q_spec, k_spec, v_spec


def _attn_fwd(q, k, v, shared_k, name):
    t = q.shape[0]
    tq, tk = _tile(t, ATTN_TQ), _tile(t, ATTN_TK)
    nq, nk = t // tq, t // tk
    wq, q_spec, k_spec, v_spec = _attn_specs(shared_k, tq, tk, lambda p, i, j: i, lambda p, i, j: j)
    groups = q.shape[1] // wq
    chunk = _tile(tq, 2 * LANES)

    def body(q_ref, k_ref, v_ref, o_ref, lse_ref, m_s, acc_s, alpha_s, s_s, p_s):
        kb = pl.program_id(2)

        @pl.when(kb == 0)
        def _():
            m_s[...] = jnp.full_like(m_s, -jnp.inf)
            acc_s[...] = jnp.zeros_like(acc_s)

        qv, kv, vv = q_ref[...], k_ref[...], v_ref[...]
        lo = _lo_mask(vv.shape)
        for i in range(2):
            qi, ki = _head_operands(qv, kv, i, shared_k)
            s_s[i] = lax.dot_general(ki, qi, _NT, preferred_element_type=F32)
        for i in range(2):
            for c in range(tq // chunk):
                cols = slice(c * chunk, (c + 1) * chunk)
                m_prev = m_s[i, :, cols]
                m_new = jnp.maximum(m_prev, jnp.max(s_s[i, :, cols], axis=0, keepdims=True))
                alpha_s[i, :, cols] = jnp.exp2(m_prev - m_new)
                m_s[i, :, cols] = m_new
                p_s[i, :, cols] = jnp.exp2(s_s[i, :, cols] - m_new).astype(BF16)
        for i in range(2):
            keep = lo if i == 0 else jnp.logical_not(lo)
            vi = jnp.where(keep, vv, jnp.ones_like(vv))
            acc_s[i] = alpha_s[i] * acc_s[i] + lax.dot_general(vi, p_s[i], _TN, preferred_element_type=F32)

        @pl.when(kb == nk - 1)
        def _():
            a0, a1 = acc_s[0], acc_s[1]
            l0 = a0[LANES - SUBLANES:, :][0:1, :]
            l1 = a1[0:SUBLANES, :][0:1, :]
            row_lo = lax.broadcasted_iota(jnp.int32, a0.shape, 0) < HEAD_DIM
            o_ref[...] = jnp.where(row_lo, a0 / l0, a1 / l1).T.astype(BF16)
            lse_ref[0] = jnp.broadcast_to(m_s[0] + jnp.log2(l0), (LANES, tq)).T
            lse_ref[1] = jnp.broadcast_to(m_s[1] + jnp.log2(l1), (LANES, tq)).T

    return pl.pallas_call(
        body, name=name, grid=(groups, nq, nk),
        in_specs=[q_spec, k_spec, v_spec],
        out_specs=[pl.BlockSpec((tq, LANES), lambda p, i, j: (i, p)),
                   pl.BlockSpec((2, tq, LANES), lambda p, i, j: (p, i, 0))],
        out_shape=[jax.ShapeDtypeStruct((t, LANES * groups), BF16),
                   jax.ShapeDtypeStruct((2 * groups, t, LANES), F32)],
        scratch_shapes=[pltpu.VMEM((2, 1, tq), F32), pltpu.VMEM((2, LANES, tq), F32), pltpu.VMEM((2, 1, tq), F32),
                        pltpu.VMEM((2, tk, tq), F32), pltpu.VMEM((2, tk, tq), BF16)],
        compiler_params=_params("parallel", "parallel", "arbitrary"),
    )(q, k, v)


def _attn_delta(do, o):
    t, w = do.shape
    tm = _tile(t, 512)
    groups = w // LANES

    def body(do_ref, o_ref, delta_ref):
        prod = do_ref[...].astype(F32) * o_ref[...].astype(F32)
        for g in range(groups):
            x = prod[:, LANES * g:LANES * (g + 1)]
            lo = _lo_mask(x.shape)
            d0 = jnp.sum(jnp.where(lo, x, 0.0), axis=-1, keepdims=True)
            d1 = jnp.sum(jnp.where(lo, 0.0, x), axis=-1, keepdims=True)
            delta_ref[2 * g] = jnp.broadcast_to(d0, (tm, LANES))
            delta_ref[2 * g + 1] = jnp.broadcast_to(d1, (tm, LANES))

    return pl.pallas_call(
        body, name="attn_delta", grid=(t // tm,),
        in_specs=[_row_spec(tm, w), _row_spec(tm, w)],
        out_specs=pl.BlockSpec((2 * groups, tm, LANES), lambda i: (0, i, 0)),
        out_shape=jax.ShapeDtypeStruct((2 * groups, t, LANES), F32),
        compiler_params=_params("parallel"),
    )(do, o)


def _attn_bwd(q, k, v, do, lse, delta, shared_k, name):
    t = q.shape[0]
    tq, tk = _tile(t, ATTN_TQ), _tile(t, ATTN_TK)
    nq, nk = t // tq, t // tk
    wq, q_spec, k_spec, v_spec = _attn_specs(shared_k, tq, tk, lambda p, j, i: i, lambda p, j, i: j)
    groups = q.shape[1] // wq

    def body(q_ref, k_ref, v_ref, do_ref, lse_ref, delta_ref, dq_ref, dk_ref, dv_ref, dk_s, dv_s, s_s, dp_s, p_s,
             ds_s):
        kb, qb = pl.program_id(1), pl.program_id(2)

        @pl.when(qb == 0)
        def _():
            dk_s[...] = jnp.zeros_like(dk_s)
            dv_s[...] = jnp.zeros_like(dv_s)

        qv, kv, vv, dov = q_ref[...], k_ref[...], v_ref[...], do_ref[...]
        lo = _lo_mask(dov.shape)
        heads = []
        for i in range(2):
            qi, ki = _head_operands(qv, kv, i, shared_k)
            keep = lo if i == 0 else jnp.logical_not(lo)
            doi = jnp.where(keep, dov, jnp.zeros_like(dov))
            heads.append((qi, ki, doi))
            s_s[i] = lax.dot_general(qi, ki, _NT, preferred_element_type=F32)
            dp_s[i] = lax.dot_general(doi, vv, _NT, preferred_element_type=F32)
        for i in range(2):
            lse_i, delta_i = lse_ref[i], delta_ref[i]
            for c in range(tk // LANES):
                cols = slice(c * LANES, (c + 1) * LANES)
                p = jnp.exp2(s_s[i, :, cols] - lse_i)
                p_s[i, :, cols] = p.astype(BF16)
                ds_s[i, :, cols] = (p * (dp_s[i, :, cols] - delta_i)).astype(BF16)
        dq_parts = []
        for i in range(2):
            qi, ki, doi = heads[i]
            dv_s[...] += lax.dot_general(doi, p_s[i], _TN, preferred_element_type=F32)
            dk_i = lax.dot_general(qi, ds_s[i], _TN, preferred_element_type=F32)
            if shared_k:
                dk_s[...] += dk_i
            else:
                dk_s[LANES * i:LANES * (i + 1), :] += dk_i
            dq_parts.append(lax.dot_general(ds_s[i], ki, _NN, preferred_element_type=F32))
        rows = pl.ds(pl.multiple_of(qb * tq, tq), tq)
        if shared_k:
            tiles = [(slice(0, LANES), jnp.where(lo, dq_parts[0], dq_parts[1]))]
        else:
            tiles = [(slice(0, LANES), dq_parts[0]), (slice(LANES, 2 * LANES), dq_parts[1])]
        for cols, val in tiles:
            @pl.when(kb == 0)
            def _(cols=cols, val=val):
                dq_ref[rows, cols] = val

            @pl.when(kb > 0)
            def _(cols=cols, val=val):
                dq_ref[rows, cols] += val

        @pl.when(qb == nq - 1)
        def _():
            if shared_k:
                dk_ref[0] = dk_s[...]
                dv_ref[0] = dv_s[...]
            else:
                dk_ref[...] = dk_s[...]
                dv_ref[...] = dv_s[...]

    stat_spec = pl.BlockSpec((2, tq, LANES), lambda p, j, i: (p, i, 0))
    do_spec = pl.BlockSpec((tq, LANES), lambda p, j, i: (i, p))
    dq_spec = pl.BlockSpec((t, wq), lambda p, j, i: (0, p))
    if shared_k:
        dk_spec = pl.BlockSpec((1, LANES, tk), lambda p, j, i: (p, 0, j))
        dv_spec = dk_spec
        dk_shape = jax.ShapeDtypeStruct((groups, LANES, t), F32)
        dv_shape = dk_shape
    else:
        dk_spec = pl.BlockSpec((wq, tk), lambda p, j, i: (p, j))
        dv_spec = pl.BlockSpec((LANES, tk), lambda p, j, i: (p, j))
        dk_shape = jax.ShapeDtypeStruct((wq * groups, t), F32)
        dv_shape = jax.ShapeDtypeStruct((LANES * groups, t), F32)
    return pl.pallas_call(
        body, name=name, grid=(groups, nk, nq),
        in_specs=[q_spec, k_spec, v_spec, do_spec, stat_spec, stat_spec],
        out_specs=[dq_spec, dk_spec, dv_spec],
        out_shape=[jax.ShapeDtypeStruct((t, wq * groups), F32), dk_shape, dv_shape],
        scratch_shapes=[pltpu.VMEM((wq, tk), F32), pltpu.VMEM((LANES, tk), F32), pltpu.VMEM((2, tq, tk), F32),
                        pltpu.VMEM((2, tq, tk), F32), pltpu.VMEM((2, tq, tk), BF16), pltpu.VMEM((2, tq, tk), BF16)],
        compiler_params=_params("parallel", "arbitrary", "arbitrary"),
    )(q, k, v, do, lse, delta)


_MERGE_W = 512
_GATE_BLK0 = Z_GATE // _MERGE_W


def _merge_fwd(z, b_gate, ta, tb):
    t = z.shape[0]
    tm = _tile(t, 512)
    w = _MERGE_W
    nj = D_MODEL // w

    def body(za_ref, zb_ref, ba_ref, bb_ref, ta_ref, tb_ref, o_ref):
        ga = jax.nn.sigmoid(za_ref[...].astype(F32) + ba_ref[...])
        gb = jax.nn.sigmoid(zb_ref[...].astype(F32) + bb_ref[...])
        o_ref[...] = (ga * ta_ref[...].astype(F32) + gb * tb_ref[...].astype(F32)).astype(BF16)

    return pl.pallas_call(
        body, name="merge_fwd", grid=(t // tm, nj),
        in_specs=[pl.BlockSpec((tm, w), lambda i, j: (i, _GATE_BLK0 + j)),
                  pl.BlockSpec((tm, w), lambda i, j: (i, _GATE_BLK0 + nj + j)),
                  pl.BlockSpec((1, w), lambda i, j: (0, j)),
                  pl.BlockSpec((1, w), lambda i, j: (0, nj + j)),
                  pl.BlockSpec((tm, w), lambda i, j: (i, j)),
                  pl.BlockSpec((tm, w), lambda i, j: (i, j))],
        out_specs=pl.BlockSpec((tm, w), lambda i, j: (i, j)),
        out_shape=jax.ShapeDtypeStruct((t, D_MODEL), BF16),
        compiler_params=_params("parallel", "parallel"),
    )(z, z, b_gate, b_gate, ta, tb)


def _merge_bwd(dmg, z, b_gate, ta, tb):
    t = z.shape[0]
    tm = _tile(t, 512)
    w = _MERGE_W
    nj = D_MODEL // w

    def body(dm_ref, za_ref, zb_ref, ba_ref, bb_ref, ta_ref, tb_ref, dta_ref, dtb_ref, dza_ref, dzb_ref,
             dba_ref, dbb_ref):
        dm = dm_ref[...].astype(F32)
        ga = jax.nn.sigmoid(za_ref[...].astype(F32) + ba_ref[...])
        gb = jax.nn.sigmoid(zb_ref[...].astype(F32) + bb_ref[...])
        dta_ref[...] = (dm * ga).astype(BF16)
        dtb_ref[...] = (dm * gb).astype(BF16)
        dza = dm * ta_ref[...].astype(F32) * ga * (1.0 - ga)
        dzb = dm * tb_ref[...].astype(F32) * gb * (1.0 - gb)
        dza_ref[...] = dza.astype(BF16)
        dzb_ref[...] = dzb.astype(BF16)

        @pl.when(pl.program_id(1) == 0)
        def _():
            dba_ref[...] = jnp.zeros_like(dba_ref)
            dbb_ref[...] = jnp.zeros_like(dbb_ref)

        dba_ref[...] += _fold8(dza)
        dbb_ref[...] += _fold8(dzb)

    blk = pl.BlockSpec((tm, w), lambda j, i: (i, j))
    acc = pl.BlockSpec((SUBLANES, w), lambda j, i: (0, j))
    return pl.pallas_call(
        body, name="merge_bwd", grid=(nj, t // tm),
        in_specs=[blk,
                  pl.BlockSpec((tm, w), lambda j, i: (i, _GATE_BLK0 + j)),
                  pl.BlockSpec((tm, w), lambda j, i: (i, _GATE_BLK0 + nj + j)),
                  pl.BlockSpec((1, w), lambda j, i: (0, j)),
                  pl.BlockSpec((1, w), lambda j, i: (0, nj + j)),
                  blk, blk],
        out_specs=[blk, blk, blk, blk, acc, acc],
        out_shape=[jax.ShapeDtypeStruct((t, D_MODEL), BF16)] * 4 + [jax.ShapeDtypeStruct((SUBLANES, D_MODEL), F32)] * 2,
        compiler_params=_params("parallel", "arbitrary"),
    )(dmg, z, z, b_gate, b_gate, ta, tb)


def _loss_grad(y, target):
    t, d = y.shape
    tm = _tile(t, 512)

    def body(y_ref, t_ref, dy_ref, acc_ref):
        err = y_ref[...] - t_ref[...]
        dy_ref[...] = err * (1.0 / d)
        e8 = _fold8(err * err)
        part = e8[:, 0:LANES]
        for c in range(1, d // LANES):
            part = part + e8[:, LANES * c:LANES * (c + 1)]

        @pl.when(pl.program_id(0) == 0)
        def _():
            acc_ref[...] = jnp.zeros_like(acc_ref)

        acc_ref[...] += part

    return pl.pallas_call(
        body, name="loss_grad", grid=(t // tm,),
        in_specs=[_row_spec(tm, d), _row_spec(tm, d)],
        out_specs=[_row_spec(tm, d), _acc_spec(LANES)],
        out_shape=[jax.ShapeDtypeStruct((t, d), F32), jax.ShapeDtypeStruct((SUBLANES, LANES), F32)],
        compiler_params=_params("arbitrary"),
    )(y, target)


_MESH_ID = pl.DeviceIdType.MESH
_ANY = pl.BlockSpec(memory_space=pl.ANY)


def _all_gather(arrays):
    n = len(arrays)

    def body(*refs):
        x_refs, out_refs = refs[:n], refs[n:2 * n]
        send_sems, recv_sems, local_sems = refs[2 * n:]
        mx, my, mc = lax.axis_index("x"), lax.axis_index("y"), lax.axis_index("c")
        me, sibling = (mx, my, mc), (mx, my, 1 - mc)
        chips = [(1 - mx, my), (mx, 1 - my), (1 - mx, 1 - my)]

        def slot(a, px, py, pc):
            return out_refs[a].at[4 * px + 2 * py + pc]

        def copy(a, sem, block, to, src=None):
            return pltpu.make_async_remote_copy(
                src_ref=slot(a, *block) if src is None else src, dst_ref=slot(a, *block),
                send_sem=send_sems.at[a, sem], recv_sem=recv_sems.at[a, sem], device_id=to, device_id_type=_MESH_ID)

        mine = [pltpu.make_async_copy(x_refs[a], slot(a, *me), local_sems.at[a]) for a in range(n)]
        first = []
        for a in range(n):
            mine[a].start()
            first.append(copy(a, 0, me, sibling, src=x_refs[a]))
            first += [copy(a, 1 + j, me, (*chip, mc), src=x_refs[a]) for j, chip in enumerate(chips)]
        for cp in first:
            cp.start()
        passed = []
        for a in range(n):
            for j, chip in enumerate(chips):
                copy(a, 1 + j, (*chip, mc), me).wait_recv()
                passed.append(copy(a, 4 + j, (*chip, mc), sibling))
                passed[-1].start()
        for a in range(n):
            copy(a, 0, sibling, me).wait_recv()
            for j, chip in enumerate(chips):
                copy(a, 4 + j, (*chip, 1 - mc), me).wait_recv()
        for cp in first + passed:
            cp.wait_send()
        for cp in mine:
            cp.wait()

    return pl.pallas_call(
        body, name="weight_all_gather",
        out_shape=[jax.ShapeDtypeStruct((N_DEV,) + a.shape, a.dtype) for a in arrays],
        in_specs=[_ANY] * n, out_specs=[_ANY] * n,
        scratch_shapes=[pltpu.SemaphoreType.DMA((n, 7)), pltpu.SemaphoreType.DMA((n, 7)),
                        pltpu.SemaphoreType.DMA((n,))],
    )(*arrays)


def _pair_exchange(sends):
    n = len(sends)

    def body(*refs):
        s_refs, r_refs = refs[:n], refs[n:2 * n]
        send_sems, recv_sems = refs[2 * n:]
        mx, my, mc = lax.axis_index("x"), lax.axis_index("y"), lax.axis_index("c")
        copies = []
        for a in range(n):
            for ch in range(4):
                cp = pltpu.make_async_remote_copy(
                    src_ref=s_refs[a].at[2 * ch + (1 - mc)], dst_ref=r_refs[a].at[ch], send_sem=send_sems.at[a, ch],
                    recv_sem=recv_sems.at[a, ch], device_id=(mx, my, 1 - mc), device_id_type=_MESH_ID)
                cp.start()
                copies.append(cp)
        for cp in copies:
            cp.wait_send()
            cp.wait_recv()

    return pl.pallas_call(
        body, name="grad_pair_exchange",
        out_shape=[jax.ShapeDtypeStruct((4,) + s.shape[1:], s.dtype) for s in sends],
        in_specs=[_ANY] * n, out_specs=[_ANY] * n,
        scratch_shapes=[pltpu.SemaphoreType.DMA((n, 4)), pltpu.SemaphoreType.DMA((n, 4))],
    )(*sends)


def _pair_add(send, half, core):
    _, r, c_ = send.shape
    tr = _row_tile(r, c_)

    def body(core_ref, s_ref, h_ref, o_ref):
        del core_ref
        o_ref[...] = (s_ref[...] + h_ref[...]).astype(BF16)

    blk = pl.BlockSpec((1, tr, c_), lambda ch, i, core_ref: (ch, i, 0))
    return pl.pallas_call(
        body, name="grad_pair_add",
        grid_spec=pltpu.PrefetchScalarGridSpec(
            num_scalar_prefetch=1, grid=(4, r // tr),
            in_specs=[pl.BlockSpec((1, tr, c_), lambda ch, i, core_ref: (2 * ch + core_ref[0], i, 0)), blk],
            out_specs=blk),
        out_shape=jax.ShapeDtypeStruct((4, r, c_), BF16),
        compiler_params=_params("parallel", "parallel"),
    )(core, send, half)


def _chip_exchange(parts):
    n = len(parts)

    def body(*refs):
        p_refs, r_refs = refs[:n], refs[n:2 * n]
        send_sems, recv_sems, local_sems = refs[2 * n:]
        mx, my, mc = lax.axis_index("x"), lax.axis_index("y"), lax.axis_index("c")
        mine = 2 * mx + my
        local = [pltpu.make_async_copy(p_refs[a].at[mine], r_refs[a].at[mine], local_sems.at[a]) for a in range(n)]
        copies = []
        for a in range(n):
            local[a].start()
            for rel in range(1, 4):
                px = 1 - mx if rel & 2 else mx
                py = 1 - my if rel & 1 else my
                cp = pltpu.make_async_remote_copy(
                    src_ref=p_refs[a].at[2 * px + py], dst_ref=r_refs[a].at[mine], send_sem=send_sems.at[a, rel - 1],
                    recv_sem=recv_sems.at[a, rel - 1], device_id=(px, py, mc), device_id_type=_MESH_ID)
                cp.start()
                copies.append(cp)
        for cp in copies:
            cp.wait_send()
            cp.wait_recv()
        for cp in local:
            cp.wait()

    return pl.pallas_call(
        body, name="grad_chip_exchange",
        out_shape=[jax.ShapeDtypeStruct(p.shape, p.dtype) for p in parts],
        in_specs=[_ANY] * n, out_specs=[_ANY] * n,
        scratch_shapes=[pltpu.SemaphoreType.DMA((n, 3)), pltpu.SemaphoreType.DMA((n, 3)),
                        pltpu.SemaphoreType.DMA((n,))],
    )(*parts)


def _row_tile(r, c_):
    tr = min(r, ADAM_BLOCK_ELEMS // (pl.cdiv(c_, LANES) * LANES))
    while r % tr:
        tr -= SUBLANES
    return tr


def _adamw(recv, w, m, v):
    r, c_ = w.shape
    tr = _row_tile(r, c_)
    n_src = recv.shape[0]

    def body(g_ref, w_ref, m_ref, v_ref, go_ref, d_ref, mo_ref, vo_ref):
        g = g_ref[0].astype(F32)
        for s in range(1, n_src):
            g = g + g_ref[s].astype(F32)
        go_ref[...] = g
        mn = ADAM_B1 * m_ref[...] + (1.0 - ADAM_B1) * g
        vn = ADAM_B2 * v_ref[...] + (1.0 - ADAM_B2) * (g * g)
        mo_ref[...] = mn
        vo_ref[...] = vn
        m_hat = mn / (1.0 - ADAM_B1 ** ADAM_STEP)
        v_hat = vn / (1.0 - ADAM_B2 ** ADAM_STEP)
        d_ref[...] = -ADAM_LR * (m_hat / (jnp.sqrt(v_hat) + ADAM_EPS) + ADAM_WD * w_ref[...])

    spec = pl.BlockSpec((tr, c_), lambda i: (i, 0))
    out = jax.ShapeDtypeStruct((r, c_), F32)
    return pl.pallas_call(
        body, name="grad_sum_adamw", grid=(r // tr,),
        in_specs=[pl.BlockSpec((n_src, tr, c_), lambda i: (0, i, 0)), spec, spec, spec],
        out_specs=[spec, spec, spec, spec], out_shape=[out, out, out, out],
        compiler_params=_params("parallel"),
    )(recv, w, m, v)


def _pad_cols(a, before, after):
    parts = []
    if before:
        parts.append(jnp.zeros(a.shape[:-1] + (before,), a.dtype))
    parts.append(a)
    if after:
        parts.append(jnp.zeros(a.shape[:-1] + (after,), a.dtype))
    return jnp.concatenate(parts, axis=-1)


def _q_head_pairs(a, axis):
    shp = a.shape
    a = a.reshape(shp[:axis] + (GQA_KV_HEADS, GQA_GROUP, HEAD_DIM) + shp[axis + 1:])
    a = jnp.swapaxes(a, axis, axis + 1)
    return a.reshape(shp)


def _q_head_unpairs(a, axis):
    shp = a.shape
    a = a.reshape(shp[:axis] + (GQA_GROUP, GQA_KV_HEADS, HEAD_DIM) + shp[axis + 1:])
    a = jnp.swapaxes(a, axis, axis + 1)
    return a.reshape(shp)


def _layout_weights(w):
    w_in = w["w_in"]
    lead = w_in.shape[:-1]
    w_in_p = jnp.concatenate([
        _q_head_pairs(w_in[..., 0:512], w_in.ndim - 1),
        w_in[..., 512:1408],
        _pad_cols(w_in[..., 1408:1440], KR_LANE0, LANES - KR_LANE0 - MLA_ROPE_DIM),
        w_in[..., 1440:],
    ], axis=-1)
    wq = w["w_q_up"]
    wq_p = _pad_cols(wq.reshape(wq.shape[:-1] + (MLA_HEADS, MLA_QK_DIM)), 0, LANES - MLA_QK_DIM)
    wq_p = wq_p.reshape(wq.shape[:-1] + (MLA_HEADS * LANES,))
    wkv = w["w_kv_up"]
    wkv4 = wkv.reshape(wkv.shape[:-1] + (MLA_HEADS, 2 * HEAD_DIM))
    wk_p = _pad_cols(wkv4[..., :HEAD_DIM], 0, LANES - HEAD_DIM).reshape(wkv.shape[:-1] + (MLA_HEADS * LANES,))
    wv_p = wkv4[..., HEAD_DIM:].reshape(wkv.shape[:-1] + (MLA_HEADS * HEAD_DIM,))
    del lead
    return {
        "w_in": w_in_p, "w_q_up": wq_p, "w_kv_up": jnp.concatenate([wk_p, wv_p], axis=-1),
        "w_branch_a": _q_head_pairs(w["w_branch_a"], w["w_branch_a"].ndim - 2), "w_branch_b": w["w_branch_b"],
        "w_o": w["w_o"], "w_ffn_up": w["w_ffn_up"], "w_ffn_down": w["w_ffn_down"],
    }


def _unlayout_grads(g):
    gi = g["w_in"]
    kr0 = Z_KR + KR_LANE0
    g_in = jnp.concatenate([
        _q_head_unpairs(gi[..., 0:512], gi.ndim - 1), gi[..., 512:1408], gi[..., kr0:kr0 + MLA_ROPE_DIM],
        gi[..., Z_GATE:],
    ], axis=-1)
    gq = g["w_q_up"]
    gq = gq.reshape(gq.shape[:-1] + (MLA_HEADS, LANES))[..., :MLA_QK_DIM]
    gq = gq.reshape(gq.shape[:-2] + (MLA_HEADS * MLA_QK_DIM,))
    gkv = g["w_kv_up"]
    gk = gkv[..., :MLA_HEADS * LANES].reshape(gkv.shape[:-1] + (MLA_HEADS, LANES))[..., :HEAD_DIM]
    gv = gkv[..., MLA_HEADS * LANES:].reshape(gkv.shape[:-1] + (MLA_HEADS, HEAD_DIM))
    gkv = jnp.concatenate([gk, gv], axis=-1).reshape(gkv.shape[:-1] + (MLA_HEADS * 2 * HEAD_DIM,))
    return {
        "w_in": g_in, "w_q_up": gq, "w_kv_up": gkv,
        "w_branch_a": _q_head_unpairs(g["w_branch_a"], g["w_branch_a"].ndim - 2), "w_branch_b": g["w_branch_b"],
        "w_o": g["w_o"], "w_ffn_up": g["w_ffn_up"], "w_ffn_down": g["w_ffn_down"],
    }


def _pack_small(parts):
    flat = jnp.concatenate([p.reshape(-1) for p in parts])
    pad = (-flat.shape[0]) % (SUBLANES * LANES)
    if pad:
        flat = jnp.concatenate([flat, jnp.zeros((pad,), flat.dtype)])
    return flat.reshape(-1, LANES)


def _unpack_small(packed, shapes):
    flat = packed.reshape(-1)
    out, off = [], 0
    for shp in shapes:
        n = int(np.prod(shp))
        out.append(flat[off:off + n].reshape(shp))
        off += n
    return out


def _shards_of(full, axis):
    shp = full.shape
    cut = shp[:axis] + (N_DEV, shp[axis] // N_DEV) + shp[axis + 1:]
    return jnp.moveaxis(full.reshape(cut), axis, 0)


def _from_shards(shards, axis):
    full = list(shards.shape[1:])
    full[axis] *= N_DEV
    return jnp.moveaxis(shards, 0, axis).reshape(full)


def _rows2d(a):
    return a.reshape(-1, a.shape[-1])


def _layer_fwd(x, u, lw, tabs):
    cos_a, sin_a, cos_b, sin_b = tabs
    z = _matmul(u, lw["w_in"], "nn", "mm_in")
    qa, ka, va, cqn, ckvn, krr = _prep_a_fwd(z, lw["gq2"], lw["gk2"], lw["gqa"], lw["gkva"], cos_a, sin_a, cos_b, sin_b)
    qb = _matmul(cqn, lw["w_q_up"], "nn", "mm_q_up")
    kvb = _matmul(ckvn, lw["w_kv_up"], "nn", "mm_kv_up")
    q_b, k_b, v_b = _prep_b_fwd(qb, kvb, krr, cos_b, sin_b)
    ya, lse_a = _attn_fwd(qa, ka, va, True, "gqa_fwd")
    yb, lse_b = _attn_fwd(q_b, k_b, v_b, False, "mla_fwd")
    ta = _matmul(ya, lw["w_branch_a"], "nn", "mm_branch_a")
    tb = _matmul(yb, lw["w_branch_b"], "nn", "mm_branch_b")
    merged = _merge_fwd(z, lw["b_gate"], ta, tb)
    m = _matmul(merged, lw["w_o"], "nn", "mm_o")
    x2, u2 = _res_norm_fwd(x, m, lw["post_mix_g"], lw["pre_ffn_g"])
    h, a = _matmul(u2, lw["w_ffn_up"], "nn", "mm_ffn_up", post="relu2")
    f = _matmul(a, lw["w_ffn_down"], "nn", "mm_ffn_down")
    x3, u_next = _res_norm_fwd(x2, f, lw["post_ffn_g"], lw["next_pre_mix_g"])
    saved = dict(u=u, z=z, qa=qa, ka=ka, va=va, cqn=cqn, ckvn=ckvn, q_b=q_b, k_b=k_b, v_b=v_b, ya=ya, yb=yb,
                 lse_a=lse_a, lse_b=lse_b, ta=ta, tb=tb, merged=merged, m=m, x2=x2, u2=u2, h=h, a=a, f=f, x3=x3)
    return x3, u_next, saved


def _layer_bwd(dx3, du_next, lw, sv, tabs):
    cos_a, sin_a, cos_b, sin_b = tabs
    g = {}
    dx3, df, dg4, dg1n = _res_norm_bwd(sv["x3"], sv["f"], lw["post_ffn_g"], lw["next_pre_mix_g"], dx3, du_next)
    g["post_ffn_g"], g["next_pre_mix_g"] = dg4, dg1n
    dh = _matmul(df, lw["w_ffn_down"], "nt", "mm_d_h", post="relu2_bwd", h=sv["h"])
    g["w_ffn_down"] = _matmul(sv["a"], df, "tn", "mm_dw_ffn_down")
    du2 = _matmul(dh, lw["w_ffn_up"], "nt", "mm_d_u2")
    g["w_ffn_up"] = _matmul(sv["u2"], dh, "tn", "mm_dw_ffn_up")
    dx2, dm, dg2, dg3 = _res_norm_bwd(sv["x2"], sv["m"], lw["post_mix_g"], lw["pre_ffn_g"], dx3, du2)
    g["post_mix_g"], g["pre_ffn_g"] = dg2, dg3
    dmg = _matmul(dm, lw["w_o"], "nt", "mm_d_merged")
    g["w_o"] = _matmul(sv["merged"], dm, "tn", "mm_dw_o")
    dta, dtb, dzg_a, dzg_b, db_a, db_b = _merge_bwd(dmg, sv["z"], lw["b_gate"], sv["ta"], sv["tb"])
    g["b_gate"] = jnp.concatenate([db_a, db_b], axis=-1)
    dya = _matmul(dta, lw["w_branch_a"], "nt", "mm_d_ya")
    g["w_branch_a"] = _matmul(sv["ya"], dta, "tn", "mm_dw_branch_a")
    dyb = _matmul(dtb, lw["w_branch_b"], "nt", "mm_d_yb")
    g["w_branch_b"] = _matmul(sv["yb"], dtb, "tn", "mm_dw_branch_b")
    delta_a = _attn_delta(dya, sv["ya"])
    delta_b = _attn_delta(dyb, sv["yb"])
    dqa, dka4, dva4 = _attn_bwd(sv["qa"], sv["ka"], sv["va"], dya, sv["lse_a"], delta_a, True, "gqa_bwd")
    dq_b, dk_b, dv_b = _attn_bwd(sv["q_b"], sv["k_b"], sv["v_b"], dyb, sv["lse_b"], delta_b, False, "mla_bwd")
    dqb, dkvb, dkr = _prep_b_bwd(dq_b, dk_b, dv_b, cos_b, sin_b)
    dcqn = _matmul(dqb, lw["w_q_up"], "nt", "mm_d_cqn")
    g["w_q_up"] = _matmul(sv["cqn"], dqb, "tn", "mm_dw_q_up")
    dckvn = _matmul(dkvb, lw["w_kv_up"], "nt", "mm_d_ckvn")
    g["w_kv_up"] = _matmul(sv["ckvn"], dkvb, "tn", "mm_dw_kv_up")
    dz, dgq, dgk, dgqa, dgkva = _prep_a_bwd(sv["z"], dqa, dka4, dva4, dcqn, dckvn, dkr, dzg_a, dzg_b, lw["gq2"],
                                            lw["gk2"], lw["gqa"], lw["gkva"], cos_a, sin_a)
    g["q_norm_g"], g["k_norm_g"], g["q_a_norm_g"], g["kv_a_norm_g"] = dgq, dgk, dgqa, dgkva
    du = _matmul(dz, lw["w_in"], "nt", "mm_d_u")
    g["w_in"] = _matmul(sv["u"], dz, "tn", "mm_dw_in")
    return dx2, du, g


def kernel(x, w_in, b_gate, q_norm_g, k_norm_g, q_a_norm_g, kv_a_norm_g, w_q_up, w_kv_up, w_branch_a, w_branch_b, w_o, w_ffn_up, w_ffn_down, pre_mix_g, post_mix_g, pre_ffn_g, post_ffn_g, loss_target, m_w_in, m_b_gate, m_q_norm_g, m_k_norm_g, m_q_a_norm_g, m_kv_a_norm_g, m_w_q_up, m_w_kv_up, m_w_branch_a, m_w_branch_b, m_w_o, m_w_ffn_up, m_w_ffn_down, m_pre_mix_g, m_post_mix_g, m_pre_ffn_g, m_post_ffn_g, v_w_in, v_b_gate, v_q_norm_g, v_k_norm_g, v_q_a_norm_g, v_kv_a_norm_g, v_w_q_up, v_w_kv_up, v_w_branch_a, v_w_branch_b, v_w_o, v_w_ffn_up, v_w_ffn_down, v_pre_mix_g, v_post_mix_g, v_pre_ffn_g, v_post_ffn_g):
    weights = dict(zip(WEIGHT_NAMES, (w_in, b_gate, q_norm_g, k_norm_g, q_a_norm_g, kv_a_norm_g, w_q_up, w_kv_up,
                                      w_branch_a, w_branch_b, w_o, w_ffn_up, w_ffn_down, pre_mix_g, post_mix_g,
                                      pre_ffn_g, post_ffn_g)))
    mom_m = dict(zip(WEIGHT_NAMES, (m_w_in, m_b_gate, m_q_norm_g, m_k_norm_g, m_q_a_norm_g, m_kv_a_norm_g, m_w_q_up,
                                    m_w_kv_up, m_w_branch_a, m_w_branch_b, m_w_o, m_w_ffn_up, m_w_ffn_down,
                                    m_pre_mix_g, m_post_mix_g, m_pre_ffn_g, m_post_ffn_g)))
    mom_v = dict(zip(WEIGHT_NAMES, (v_w_in, v_b_gate, v_q_norm_g, v_k_norm_g, v_q_a_norm_g, v_kv_a_norm_g, v_w_q_up,
                                    v_w_kv_up, v_w_branch_a, v_w_branch_b, v_w_o, v_w_ffn_up, v_w_ffn_down,
                                    v_pre_mix_g, v_post_mix_g, v_pre_ffn_g, v_post_ffn_g)))
    assert x.shape[0] == 1 and x.shape[2] == D_MODEL, x.shape
    n_layers = w_in.shape[0]
    t = x.shape[1]
    x0 = x.reshape(t, D_MODEL)
    target = loss_target.reshape(t, D_MODEL)
    shard_shapes = {n: weights[n].shape for n in BIG_NAMES}
    small_shapes = [weights[n].shape for n in SMALL_NAMES]

    gathered = _all_gather([weights[n].astype(BF16) for n in BIG_NAMES])
    full = {n: _from_shards(g, SHARD_AXIS[n]) for n, g in zip(BIG_NAMES, gathered)}
    lw_all = _layout_weights(full)
    lw_all["b_gate"] = b_gate.reshape(n_layers, 1, 2 * D_MODEL)
    lw_all["gq2"] = jnp.tile(q_norm_g, (1, 2)).reshape(n_layers, 1, LANES)
    lw_all["gk2"] = jnp.tile(k_norm_g, (1, 2)).reshape(n_layers, 1, LANES)
    lw_all["gqa"] = q_a_norm_g.reshape(n_layers, 1, MLA_Q_RANK)
    lw_all["gkva"] = kv_a_norm_g.reshape(n_layers, 1, MLA_KV_RANK)
    for n in ("post_mix_g", "pre_ffn_g", "post_ffn_g"):
        lw_all[n] = weights[n]
    lw_all["next_pre_mix_g"] = jnp.roll(pre_mix_g, -1, axis=0)

    tabs = _rope_tables(t)
    u0 = _rms_fwd(x0, pre_mix_g[0])

    layer_w = [{n: a[l] for n, a in lw_all.items()} for l in range(n_layers)]
    xc, uc, saved = x0, u0, []
    for l in range(n_layers):
        xc, uc, sv = _layer_fwd(xc, uc, layer_w[l], tabs)
        saved.append(sv)
    dy, loss_acc = _loss_grad(xc, target)
    loss = lax.psum(0.5 * jnp.sum(loss_acc) / D_MODEL, ("x", "y", "c"))

    dx0, du0, layer_g = dy, jnp.zeros((t, D_MODEL), F32), [None] * n_layers
    for l in reversed(range(n_layers)):
        dx0, du0, layer_g[l] = _layer_bwd(dx0, du0, layer_w[l], saved[l], tabs)
    grads = {n: jnp.stack([g[n] for g in layer_g]) for n in layer_g[0]}
    grad_x, dg1_first = _rms_bwd(x0, pre_mix_g[0], dx0, du0)

    big_grads = _unlayout_grads({n: grads[n] for n in BIG_NAMES})
    fold = lambda a: a.sum(axis=1)
    dgq = fold(grads["q_norm_g"]).reshape(n_layers, 2, HEAD_DIM).sum(axis=1)
    dgk = fold(grads["k_norm_g"]).reshape(n_layers, 2, HEAD_DIM).sum(axis=1)
    dg1 = jnp.concatenate([fold(dg1_first[None]), fold(grads["next_pre_mix_g"])[:-1]], axis=0)
    small_grads = {
        "b_gate": fold(grads["b_gate"]), "q_norm_g": dgq, "k_norm_g": dgk, "q_a_norm_g": fold(grads["q_a_norm_g"]),
        "kv_a_norm_g": fold(grads["kv_a_norm_g"]), "pre_mix_g": dg1, "post_mix_g": fold(grads["post_mix_g"]),
        "pre_ffn_g": fold(grads["pre_ffn_g"]), "post_ffn_g": fold(grads["post_ffn_g"]),
    }
    small_packed = _pack_small([small_grads[n] for n in SMALL_NAMES])
    sends = [_shards_of(big_grads[n], SHARD_AXIS[n]).reshape((N_DEV,) + _rows2d(weights[n]).shape)
             for n in BIG_NAMES]
    sends.append(jnp.broadcast_to(small_packed[None], (N_DEV,) + small_packed.shape))
    halves = _pair_exchange(sends)
    core = lax.axis_index("c").astype(jnp.int32).reshape(1)
    recvs = _chip_exchange([_pair_add(s, h, core) for s, h in zip(sends, halves)])

    results = {}
    for n, recv in zip(BIG_NAMES, recvs):
        res = _adamw(recv, _rows2d(weights[n]), _rows2d(mom_m[n]), _rows2d(mom_v[n]))
        results[n] = [r.reshape(shard_shapes[n]) for r in res]
    res = _adamw(recvs[-1], *[_pack_small([d[n] for n in SMALL_NAMES]) for d in (weights, mom_m, mom_v)])
    for kind, packed_out in enumerate(res):
        for n, val in zip(SMALL_NAMES, _unpack_small(packed_out, small_shapes)):
            results.setdefault(n, [None] * 4)[kind] = val
    outs = [results[n][kind] for kind in range(4) for n in WEIGHT_NAMES]
    return (loss, grad_x.reshape(x.shape), *outs)
```

```python
import math

import jax
import jax.numpy as jnp
import numpy as np
from jax import lax
from jax.experimental import pallas as pl
from jax.experimental.pallas import tpu as pltpu

F32 = jnp.float32
BF16 = jnp.bfloat16

D_MODEL = 1024
GRID_W = 64
ROPE_THETA = 10000.0
EPS = 1e-6
GQA_HEADS = 8
GQA_KV_HEADS = 2
GQA_GROUP = GQA_HEADS // GQA_KV_HEADS
HEAD_DIM = 64
MLA_HEADS = 8
MLA_ROPE_DIM = 32
MLA_QK_DIM = 96
MLA_Q_RANK = 384
MLA_KV_RANK = 256
GQA_SCALE = 1.0 / math.sqrt(HEAD_DIM)
MLA_SCALE = 1.0 / math.sqrt(MLA_QK_DIM)
LOG2E = math.log2(math.e)
LN2 = math.log(2.0)

ADAM_LR = 0.001
ADAM_B1 = 0.9
ADAM_B2 = 0.999
ADAM_EPS = 1e-08
ADAM_WD = 0.01
ADAM_STEP = 10

N_DEV = 8
LANES = 128
SUBLANES = 8
VMEM_LIMIT = 48 * 1024 * 1024

Z_QA, Z_KA, Z_VA, Z_CQ, Z_CKV, Z_KR, Z_GATE = 0, 512, 640, 768, 1152, 1408, 1536
Z_ATT_W = 1536
Z_W = 3584
KR_LANE0 = 64

WEIGHT_NAMES = ("w_in", "b_gate", "q_norm_g", "k_norm_g", "q_a_norm_g", "kv_a_norm_g", "w_q_up", "w_kv_up",
                "w_branch_a", "w_branch_b", "w_o", "w_ffn_up", "w_ffn_down", "pre_mix_g", "post_mix_g",
                "pre_ffn_g", "post_ffn_g")
SHARD_AXIS = {"w_in": 2, "w_q_up": 2, "w_kv_up": 2, "w_branch_a": 2, "w_branch_b": 2, "w_o": 1, "w_ffn_up": 2,
              "w_ffn_down": 1}
BIG_NAMES = tuple(n for n in WEIGHT_NAMES if n in SHARD_AXIS)
SMALL_NAMES = tuple(n for n in WEIGHT_NAMES if n not in SHARD_AXIS)
ADAM_BLOCK_ELEMS = 256 * 1024
MM_TILE = 1024
MM_TILE_TOKENS = 2048
MM_TILE_K = 2048
PREP_ROWS = 512
ATTN_TQ = 1024
ATTN_TK = 1024


def _params(*semantics):
    return pltpu.CompilerParams(dimension_semantics=semantics, vmem_limit_bytes=VMEM_LIMIT)


def _tile(n, pref):
    if n <= pref:
        return n
    t = (pref // LANES) * LANES
    while n % t:
        t -= LANES
    return t


def _fold8(t):
    return t.reshape(t.shape[0] // SUBLANES, SUBLANES, t.shape[1]).sum(axis=0)


_DIMS = {"nn": ((1,), (0,)), "nt": ((1,), (1,)), "tn": ((0,), (0,))}


def _matmul(a, b, mode, name, post=None, h=None, stack=None):
    out_dt = F32 if mode == "tn" else BF16
    if mode == "nn":
        (m, k), n = a.shape, b.shape[1]
    elif mode == "nt":
        (m, k), n = a.shape, b.shape[0]
    else:
        (k, m), n = a.shape, b.shape[1]
    tm = _tile(m, MM_TILE_TOKENS if mode != "tn" and k <= MM_TILE else MM_TILE)
    tn, tk = _tile(n, MM_TILE), _tile(k, MM_TILE_K)
    nk = k // tk
    dims = (_DIMS[mode], ((), ()))
    operands = [a, b] + ([h] if post == "relu2_bwd" else []) + ([stack[0]] if stack else [])
    n_in = len(operands)
    n_out = 2 if post == "relu2" else 1

    def body(*refs):
        a_ref, b_ref = refs[:2]
        o_refs, acc_ref = refs[n_in:n_in + n_out], refs[-1]

        def finish(val):
            if post == "relu2":
                o_refs[0][...] = val.astype(out_dt)
                r = jnp.maximum(val, 0.0)
                o_refs[1][...] = (r * r).astype(BF16)
            elif post == "relu2_bwd":
                o_refs[0][...] = (val * (2.0 * jnp.maximum(refs[2][...].astype(F32), 0.0))).astype(BF16)
            else:
                o_refs[0][...] = val.astype(out_dt)

        prod = lax.dot_general(a_ref[...], b_ref[...], dims, preferred_element_type=F32)
        if nk == 1:
            finish(prod)
        else:
            kk = pl.program_id(2)

            @pl.when(kk == 0)
            def _():
                acc_ref[...] = prod

            @pl.when(kk > 0)
            def _():
                acc_ref[...] += prod

            @pl.when(kk == nk - 1)
            def _():
                finish(acc_ref[...])

    if mode == "tn":
        a_spec = pl.BlockSpec((tk, tm), lambda i, j, kk: (kk, i))
    else:
        a_spec = pl.BlockSpec((tm, tk), lambda i, j, kk: (i, kk))
    if mode == "nt":
        b_spec = pl.BlockSpec((tn, tk), lambda i, j, kk: (j, kk))
    else:
        b_spec = pl.BlockSpec((tk, tn), lambda i, j, kk: (kk, j))
    o_spec = pl.BlockSpec((tm, tn), lambda i, j, kk: (i, j))
    main_out, bf16_out = jax.ShapeDtypeStruct((m, n), out_dt), jax.ShapeDtypeStruct((m, n), BF16)
    out_shape = {None: main_out, "relu2": [main_out, bf16_out], "relu2_bwd": bf16_out}[post]
    in_specs = [a_spec, b_spec] + ([o_spec] if post == "relu2_bwd" else [])
    out_specs = [o_spec, o_spec] if post == "relu2" else o_spec
    aliases = {}
    if stack:
        buf, layer = stack
        assert post is None and buf.shape[1:] == (m, n) and buf.dtype == out_dt, (buf.shape, buf.dtype)
        in_specs.append(pl.BlockSpec(memory_space=pl.ANY))
        out_specs = pl.BlockSpec((None, tm, tn), lambda i, j, kk: (layer, i, j))
        out_shape = jax.ShapeDtypeStruct(buf.shape, buf.dtype)
        aliases = {n_in - 1: 0}
    return pl.pallas_call(
        body,
        name=name,
        grid=(m // tm, n // tn, nk),
        in_specs=in_specs,
        out_specs=out_specs,
        out_shape=out_shape,
        scratch_shapes=[pltpu.VMEM((tm, tn), F32)],
        input_output_aliases=aliases,
        compiler_params=_params("parallel", "parallel", "arbitrary"),
    )(*operands)


def _rinv(x):
    return lax.rsqrt(jnp.mean(x * x, axis=-1, keepdims=True) + EPS)


def _rms_bwd_rows(x, g, dy):
    r = _rinv(x)
    xh = x * r
    dxh = dy * g
    dx = r * (dxh - xh * jnp.mean(dxh * xh, axis=-1, keepdims=True))
    return dx, dy * xh


def _row_spec(tm, c):
    return pl.BlockSpec((tm, c), lambda i: (i, 0))


def _vec_spec(c):
    return pl.BlockSpec((1, c), lambda i: (0, 0))


def _acc_spec(c):
    return pl.BlockSpec((SUBLANES, c), lambda i: (0, 0))


def _rms_fwd(x, g):
    t, d = x.shape
    tm = _tile(t, 512)

    def body(x_ref, g_ref, o_ref):
        xv = x_ref[...]
        o_ref[...] = (xv * _rinv(xv) * g_ref[...]).astype(BF16)

    return pl.pallas_call(
        body, name="rms_fwd", grid=(t // tm,),
        in_specs=[_row_spec(tm, d), _vec_spec(d)], out_specs=_row_spec(tm, d),
        out_shape=jax.ShapeDtypeStruct((t, d), BF16), compiler_params=_params("parallel"),
    )(x, g.reshape(1, d))


def _rms_bwd(x, g, dres, dy):
    t, d = x.shape
    tm = _tile(t, 512)

    def body(x_ref, g_ref, dres_ref, dy_ref, dx_ref, dg_ref):
        dx, dgc = _rms_bwd_rows(x_ref[...], g_ref[...], dy_ref[...].astype(F32))
        dx_ref[...] = dres_ref[...] + dx

        @pl.when(pl.program_id(0) == 0)
        def _():
            dg_ref[...] = jnp.zeros_like(dg_ref)

        dg_ref[...] += _fold8(dgc)

    return pl.pallas_call(
        body, name="rms_bwd", grid=(t // tm,),
        in_specs=[_row_spec(tm, d), _vec_spec(d), _row_spec(tm, d), _row_spec(tm, d)],
        out_specs=[_row_spec(tm, d), _acc_spec(d)],
        out_shape=[jax.ShapeDtypeStruct((t, d), F32), jax.ShapeDtypeStruct((SUBLANES, d), F32)],
        compiler_params=_params("arbitrary"),
    )(x, g.reshape(1, d), dres, dy)


def _res_norm_fwd(x, m, g_post, g_next):
    t, d = x.shape
    tm = _tile(t, 512)

    def body(x_ref, m_ref, gp_ref, gn_ref, x2_ref, u2_ref):
        mv = m_ref[...].astype(F32)
        x2 = x_ref[...] + mv * _rinv(mv) * gp_ref[...]
        x2_ref[...] = x2
        u2_ref[...] = (x2 * _rinv(x2) * gn_ref[...]).astype(BF16)

    return pl.pallas_call(
        body, name="res_norm_fwd", grid=(t // tm,),
        in_specs=[_row_spec(tm, d), _row_spec(tm, d), _vec_spec(d), _vec_spec(d)],
        out_specs=[_row_spec(tm, d), _row_spec(tm, d)],
        out_shape=[jax.ShapeDtypeStruct((t, d), F32), jax.ShapeDtypeStruct((t, d), BF16)],
        compiler_params=_params("parallel"),
    )(x, m, g_post.reshape(1, d), g_next.reshape(1, d))


def _res_norm_bwd(x2, m, g_post, g_next, dx2_in, du2):
    t, d = x2.shape
    tm = _tile(t, 512)

    def body(x2_ref, m_ref, gp_ref, gn_ref, dx2in_ref, du2_ref, dx2_ref, dm_ref, dgp_ref, dgn_ref):
        dxn, dgn_c = _rms_bwd_rows(x2_ref[...], gn_ref[...], du2_ref[...].astype(F32))
        dx2 = dx2in_ref[...] + dxn
        dx2_ref[...] = dx2
        dm, dgp_c = _rms_bwd_rows(m_ref[...].astype(F32), gp_ref[...], dx2)
        dm_ref[...] = dm.astype(BF16)

        @pl.when(pl.program_id(0) == 0)
        def _():
            dgp_ref[...] = jnp.zeros_like(dgp_ref)
            dgn_ref[...] = jnp.zeros_like(dgn_ref)

        dgp_ref[...] += _fold8(dgp_c)
        dgn_ref[...] += _fold8(dgn_c)

    return pl.pallas_call(
        body, name="res_norm_bwd", grid=(t // tm,),
        in_specs=[_row_spec(tm, d), _row_spec(tm, d), _vec_spec(d), _vec_spec(d), _row_spec(tm, d), _row_spec(tm, d)],
        out_specs=[_row_spec(tm, d), _row_spec(tm, d), _acc_spec(d), _acc_spec(d)],
        out_shape=[jax.ShapeDtypeStruct((t, d), F32), jax.ShapeDtypeStruct((t, d), BF16),
                   jax.ShapeDtypeStruct((SUBLANES, d), F32), jax.ShapeDtypeStruct((SUBLANES, d), F32)],
        compiler_params=_params("arbitrary"),
    )(x2, m, g_post.reshape(1, d), g_next.reshape(1, d), dx2_in, du2)


def _rope_tables(t):
    rows = t // GRID_W
    row = jnp.repeat(jnp.arange(rows, dtype=F32), GRID_W)
    col = jnp.tile(jnp.arange(GRID_W, dtype=F32), rows)

    def tab(rot_dim):
        half = rot_dim // 2
        inv = ROPE_THETA ** (-jnp.arange(0, half, 2, dtype=F32) / half)
        ar = row[:, None] * inv[None, :]
        ac = col[:, None] * inv[None, :]
        ang = jnp.concatenate([ar, ar, ac, ac], axis=-1)
        q = half // 2
        sign = np.tile(np.concatenate([-np.ones(q, np.float32), np.ones(q, np.float32)]), 2)
        return jnp.cos(ang), jnp.sin(ang) * sign[None, :]

    ca, sa = tab(HEAD_DIM)
    cb, sb = tab(MLA_ROPE_DIM)
    one = jnp.ones((t, 1), F32)
    cos_b = jnp.concatenate([one * jnp.ones((1, KR_LANE0), F32), cb, one * jnp.ones((1, 32), F32)], axis=-1)
    sin_b = jnp.concatenate([jnp.zeros((t, KR_LANE0), F32), sb, jnp.zeros((t, 32), F32)], axis=-1)
    return jnp.tile(ca, (1, GQA_HEADS)), jnp.tile(sa, (1, GQA_HEADS)), cos_b, sin_b


def _swap_halves(x, sh):
    lane = lax.broadcasted_iota(jnp.int32, x.shape, 1)
    up = pltpu.roll(x, LANES - sh, 1)
    dn = pltpu.roll(x, sh, 1)
    return jnp.where((lane & (2 * sh - 1)) < sh, up, dn)


def _rope(x, cos, sin_s, sh):
    return x * cos + _swap_halves(x, sh) * sin_s


def _rope_bwd(dy, cos, sin_s, sh):
    return dy * cos + _swap_halves(dy * sin_s, sh)


def _lo_mask(shape):
    return lax.broadcasted_iota(jnp.int32, shape, 1) < HEAD_DIM


def _half_mean(t, lo):
    s_lo = jnp.sum(jnp.where(lo, t, 0.0), axis=-1, keepdims=True)
    s_hi = jnp.sum(jnp.where(lo, 0.0, t), axis=-1, keepdims=True)
    return jnp.where(lo, s_lo, s_hi) * (1.0 / HEAD_DIM)


def _head_norm(x, g2):
    lo = _lo_mask(x.shape)
    r = lax.rsqrt(_half_mean(x * x, lo) + EPS)
    return x * r * g2


def _head_norm_bwd(x, g2, dy):
    lo = _lo_mask(x.shape)
    r = lax.rsqrt(_half_mean(x * x, lo) + EPS)
    xh = x * r
    dxh = dy * g2
    dx = r * (dxh - xh * _half_mean(dxh * xh, lo))
    return dx, dy * xh


def _prep_a_fwd(z, gq2, gk2, gqa, gkva, cos_a, sin_a, cos_b, sin_b):
    t = z.shape[0]
    tm = _tile(t, PREP_ROWS)

    def body(z_ref, gq_ref, gk_ref, gqa_ref, gkva_ref, ca_ref, sa_ref, cb_ref, sb_ref,
             qa_ref, ka_ref, va_ref, cqn_ref, ckvn_ref, krr_ref):
        def zf(lo, hi):
            return z_ref[:, lo:hi].astype(F32)

        for j in range(4):
            cols = slice(LANES * j, LANES * (j + 1))
            y = _rope(_head_norm(zf(LANES * j, LANES * (j + 1)), gq_ref[...]), ca_ref[:, cols], sa_ref[:, cols], 16)
            qa_ref[:, cols] = (y * (GQA_SCALE * LOG2E)).astype(BF16)
        y = _rope(_head_norm(zf(Z_KA, Z_VA), gk_ref[...]), ca_ref[:, :LANES], sa_ref[:, :LANES], 16)
        ka_ref[...] = y.astype(BF16)
        va_ref[...] = z_ref[:, Z_VA:Z_CQ].astype(BF16)
        cq = zf(Z_CQ, Z_CKV)
        cqn_ref[...] = (cq * _rinv(cq) * gqa_ref[...]).astype(BF16)
        ckv = zf(Z_CKV, Z_KR)
        ckvn_ref[...] = (ckv * _rinv(ckv) * gkva_ref[...]).astype(BF16)
        krr_ref[...] = _rope(zf(Z_KR, Z_GATE), cb_ref[...], sb_ref[...], 8)

    return pl.pallas_call(
        body, name="prep_a_fwd", grid=(t // tm,),
        in_specs=[_row_spec(tm, Z_ATT_W), _vec_spec(LANES), _vec_spec(LANES), _vec_spec(MLA_Q_RANK),
                  _vec_spec(MLA_KV_RANK), _row_spec(tm, 512), _row_spec(tm, 512), _row_spec(tm, LANES),
                  _row_spec(tm, LANES)],
        out_specs=[_row_spec(tm, 512), _row_spec(tm, LANES), _row_spec(tm, LANES), _row_spec(tm, MLA_Q_RANK),
                   _row_spec(tm, MLA_KV_RANK), _row_spec(tm, LANES)],
        out_shape=[jax.ShapeDtypeStruct((t, 512), BF16), jax.ShapeDtypeStruct((t, LANES), BF16),
                   jax.ShapeDtypeStruct((t, LANES), BF16), jax.ShapeDtypeStruct((t, MLA_Q_RANK), BF16),
                   jax.ShapeDtypeStruct((t, MLA_KV_RANK), BF16), jax.ShapeDtypeStruct((t, LANES), F32)],
        compiler_params=_params("parallel"),
    )(z, gq2, gk2, gqa, gkva, cos_a, sin_a, cos_b, sin_b)


def _prep_a_bwd(z, dqa, dka4, dva4, dcqn, dckvn, dkr, dzga, dzgb, gq2, gk2, gqa, gkva, cos_a, sin_a):
    t = z.shape[0]
    tm = _tile(t, PREP_ROWS)

    def body(z_ref, dqa_ref, dka_ref, dva_ref, dcqn_ref, dckvn_ref, dkr_ref, dzga_ref, dzgb_ref, gq_ref, gk_ref,
             gqa_ref, gkva_ref, ca_ref, sa_ref, dz_ref, dgq_ref, dgk_ref, dgqa_ref, dgkva_ref):
        @pl.when(pl.program_id(0) == 0)
        def _():
            dgq_ref[...] = jnp.zeros_like(dgq_ref)
            dgk_ref[...] = jnp.zeros_like(dgk_ref)
            dgqa_ref[...] = jnp.zeros_like(dgqa_ref)
            dgkva_ref[...] = jnp.zeros_like(dgkva_ref)

        def zf(lo, hi):
            return z_ref[:, lo:hi].astype(F32)

        dgq = jnp.zeros((SUBLANES, LANES), F32)
        for j in range(4):
            cols = slice(LANES * j, LANES * (j + 1))
            dy = _rope_bwd(dqa_ref[:, cols] * GQA_SCALE, ca_ref[:, cols], sa_ref[:, cols], 16)
            dx, dgc = _head_norm_bwd(zf(LANES * j, LANES * (j + 1)), gq_ref[...], dy)
            dz_ref[:, cols] = dx.astype(BF16)
            dgq = dgq + _fold8(dgc)
        dgq_ref[...] += dgq
        dk = (dka_ref[0] + dka_ref[1] + dka_ref[2] + dka_ref[3]).T * LN2
        dy = _rope_bwd(dk, ca_ref[:, :LANES], sa_ref[:, :LANES], 16)
        dx, dgc = _head_norm_bwd(zf(Z_KA, Z_VA), gk_ref[...], dy)
        dz_ref[:, Z_KA:Z_VA] = dx.astype(BF16)
        dgk_ref[...] += _fold8(dgc)
        dz_ref[:, Z_VA:Z_CQ] = (dva_ref[0] + dva_ref[1] + dva_ref[2] + dva_ref[3]).T.astype(BF16)
        dx, dgc = _rms_bwd_rows(zf(Z_CQ, Z_CKV), gqa_ref[...], dcqn_ref[...].astype(F32))
        dz_ref[:, Z_CQ:Z_CKV] = dx.astype(BF16)
        dgqa_ref[...] += _fold8(dgc)
        dx, dgc = _rms_bwd_rows(zf(Z_CKV, Z_KR), gkva_ref[...], dckvn_ref[...].astype(F32))
        dz_ref[:, Z_CKV:Z_KR] = dx.astype(BF16)
        dgkva_ref[...] += _fold8(dgc)
        dz_ref[:, Z_KR:Z_GATE] = dkr_ref[...].astype(BF16)
        dz_ref[:, Z_GATE:Z_GATE + D_MODEL] = dzga_ref[...]
        dz_ref[:, Z_GATE + D_MODEL:Z_W] = dzgb_ref[...]

    part = pl.BlockSpec((4, LANES, tm), lambda i: (0, 0, i))
    return pl.pallas_call(
        body, name="prep_a_bwd", grid=(t // tm,),
        in_specs=[_row_spec(tm, Z_ATT_W), _row_spec(tm, 512), part, part, _row_spec(tm, MLA_Q_RANK),
                  _row_spec(tm, MLA_KV_RANK), _row_spec(tm, LANES), _row_spec(tm, D_MODEL), _row_spec(tm, D_MODEL),
                  _vec_spec(LANES),
                  _vec_spec(LANES), _vec_spec(MLA_Q_RANK), _vec_spec(MLA_KV_RANK), _row_spec(tm, 512),
                  _row_spec(tm, 512)],
        out_specs=[_row_spec(tm, Z_W), _acc_spec(LANES), _acc_spec(LANES), _acc_spec(MLA_Q_RANK),
                   _acc_spec(MLA_KV_RANK)],
        out_shape=[jax.ShapeDtypeStruct((t, Z_W), BF16), jax.ShapeDtypeStruct((SUBLANES, LANES), F32),
                   jax.ShapeDtypeStruct((SUBLANES, LANES), F32), jax.ShapeDtypeStruct((SUBLANES, MLA_Q_RANK), F32),
                   jax.ShapeDtypeStruct((SUBLANES, MLA_KV_RANK), F32)],
        compiler_params=_params("arbitrary"),
    )(z, dqa, dka4, dva4, dcqn, dckvn, dkr, dzga, dzgb, gq2, gk2, gqa, gkva, cos_a, sin_a)


def _prep_b_fwd(qb, kvb, krr, cos_b, sin_b):
    t = qb.shape[0]
    tm = _tile(t, PREP_ROWS)

    def body(qb_ref, kvb_ref, krr_ref, cb_ref, sb_ref, q_ref, k_ref, v_ref):
        for h in range(MLA_HEADS):
            cols = slice(LANES * h, LANES * (h + 1))
            qh = _rope(qb_ref[:, cols].astype(F32), cb_ref[...], sb_ref[...], 8)
            q_ref[:, cols] = (qh * (MLA_SCALE * LOG2E)).astype(BF16)
            k_ref[:, cols] = (kvb_ref[:, cols].astype(F32) + krr_ref[...]).astype(BF16)
        v_ref[...] = kvb_ref[:, 1024:1536].astype(BF16)

    return pl.pallas_call(
        body, name="prep_b_fwd", grid=(t // tm,),
        in_specs=[_row_spec(tm, 1024), _row_spec(tm, 1536), _row_spec(tm, LANES), _row_spec(tm, LANES),
                  _row_spec(tm, LANES)],
        out_specs=[_row_spec(tm, 1024), _row_spec(tm, 1024), _row_spec(tm, 512)],
        out_shape=[jax.ShapeDtypeStruct((t, 1024), BF16), jax.ShapeDtypeStruct((t, 1024), BF16),
                   jax.ShapeDtypeStruct((t, 512), BF16)],
        compiler_params=_params("parallel"),
    )(qb, kvb, krr, cos_b, sin_b)


def _prep_b_bwd(dq, dk, dv, cos_b, sin_b):
    t = dq.shape[0]
    tm = _tile(t, PREP_ROWS)

    def body(dq_ref, dk_ref, dv_ref, cb_ref, sb_ref, dqb_ref, dkvb_ref, dkr_ref):
        dkr = jnp.zeros((tm, LANES), F32)
        for h in range(MLA_HEADS):
            cols = slice(LANES * h, LANES * (h + 1))
            dqb_ref[:, cols] = _rope_bwd(dq_ref[:, cols] * MLA_SCALE, cb_ref[...], sb_ref[...], 8).astype(BF16)
            dkh = dk_ref[cols, :].T * LN2
            dkvb_ref[:, cols] = dkh.astype(BF16)
            dkr = dkr + dkh
        for j in range(MLA_HEADS // 2):
            dkvb_ref[:, 1024 + LANES * j:1024 + LANES * (j + 1)] = dv_ref[LANES * j:LANES * (j + 1), :].T.astype(BF16)
        dkr_ref[...] = _rope_bwd(dkr, cb_ref[...], sb_ref[...], 8)

    return pl.pallas_call(
        body, name="prep_b_bwd", grid=(t // tm,),
        in_specs=[_row_spec(tm, 1024), pl.BlockSpec((1024, tm), lambda i: (0, i)),
                  pl.BlockSpec((512, tm), lambda i: (0, i)), _row_spec(tm, LANES),
                  _row_spec(tm, LANES)],
        out_specs=[_row_spec(tm, 1024), _row_spec(tm, 1536), _row_spec(tm, LANES)],
        out_shape=[jax.ShapeDtypeStruct((t, 1024), BF16), jax.ShapeDtypeStruct((t, 1536), BF16),
                   jax.ShapeDtypeStruct((t, LANES), F32)],
        compiler_params=_params("parallel"),
    )(dq, dk, dv, cos_b, sin_b)


_NT = (((1,), (1,)), ((), ()))
_NN = (((1,), (0,)), ((), ()))
_TN = (((0,), (0,)), ((), ()))


def _head_operands(qv, kv, i, shared_k):
    if shared_k:
        lo = _lo_mask(qv.shape)
        keep = lo if i == 0 else jnp.logical_not(lo)
        return jnp.where(keep, qv, jnp.zeros_like(qv)), kv
    cols = slice(LANES * i, LANES * (i + 1))
    return qv[:, cols], kv[:, cols]


def _attn_specs(shared_k, tq, tk, q_of, k_of):
    wq = LANES if shared_k else 2 * LANES
    q_spec = pl.BlockSpec((tq, wq), lambda *g: (q_of(*g), g[0]))
    if shared_k:
        k_spec = pl.BlockSpec((tk, LANES), lambda *g: (k_of(*g), 0))
        v_spec = pl.BlockSpec((tk, LANES), lambda *g: (k_of(*g), 0))
    else:
        k_spec = pl.BlockSpec((tk, wq), lambda *g: (k_of(*g), g[0]))
        v_spec = pl.BlockSpec((tk, LANES), lambda *g: (k_of(*g), g[0]))
    return wq, q_spec, k_spec, v_spec


def _attn_fwd(q, k, v, shared_k, name):
    t = q.shape[0]
    tq, tk = _tile(t, ATTN_TQ), _tile(t, ATTN_TK)
    nq, nk = t // tq, t // tk
    wq, q_spec, k_spec, v_spec = _attn_specs(shared_k, tq, tk, lambda p, i, j: i, lambda p, i, j: j)
    groups = q.shape[1] // wq
    chunk = _tile(tq, 2 * LANES)

    def body(q_ref, k_ref, v_ref, o_ref, lse_ref, m_s, acc_s, alpha_s, s_s, p_s):
        kb = pl.program_id(2)

        @pl.when(kb == 0)
        def _():
            m_s[...] = jnp.full_like(m_s, -jnp.inf)
            acc_s[...] = jnp.zeros_like(acc_s)

        qv, kv, vv = q_ref[...], k_ref[...], v_ref[...]
        lo = _lo_mask(vv.shape)
        for i in range(2):
            qi, ki = _head_operands(qv, kv, i, shared_k)
            s_s[i] = lax.dot_general(ki, qi, _NT, preferred_element_type=F32)
        for i in range(2):
            for c in range(tq // chunk):
                cols = slice(c * chunk, (c + 1) * chunk)
                m_prev = m_s[i, :, cols]
                m_new = jnp.maximum(m_prev, jnp.max(s_s[i, :, cols], axis=0, keepdims=True))
                alpha_s[i, :, cols] = jnp.exp2(m_prev - m_new)
                m_s[i, :, cols] = m_new
                p_s[i, :, cols] = jnp.exp2(s_s[i, :, cols] - m_new).astype(BF16)
        for i in range(2):
            keep = lo if i == 0 else jnp.logical_not(lo)
            vi = jnp.where(keep, vv, jnp.ones_like(vv))
            acc_s[i] = alpha_s[i] * acc_s[i] + lax.dot_general(vi, p_s[i], _TN, preferred_element_type=F32)

        @pl.when(kb == nk - 1)
        def _():
            a0, a1 = acc_s[0], acc_s[1]
            l0 = a0[LANES - SUBLANES:, :][0:1, :]
            l1 = a1[0:SUBLANES, :][0:1, :]
            row_lo = lax.broadcasted_iota(jnp.int32, a0.shape, 0) < HEAD_DIM
            o_ref[...] = jnp.where(row_lo, a0 / l0, a1 / l1).T.astype(BF16)
            lse_ref[0] = jnp.broadcast_to(m_s[0] + jnp.log2(l0), (LANES, tq)).T
            lse_ref[1] = jnp.broadcast_to(m_s[1] + jnp.log2(l1), (LANES, tq)).T

    return pl.pallas_call(
        body, name=name, grid=(groups, nq, nk),
        in_specs=[q_spec, k_spec, v_spec],
        out_specs=[pl.BlockSpec((tq, LANES), lambda p, i, j: (i, p)),
                   pl.BlockSpec((2, tq, LANES), lambda p, i, j: (p, i, 0))],
        out_shape=[jax.ShapeDtypeStruct((t, LANES * groups), BF16),
                   jax.ShapeDtypeStruct((2 * groups, t, LANES), F32)],
        scratch_shapes=[pltpu.VMEM((2, 1, tq), F32), pltpu.VMEM((2, LANES, tq), F32), pltpu.VMEM((2, 1, tq), F32),
                        pltpu.VMEM((2, tk, tq), F32), pltpu.VMEM((2, tk, tq), BF16)],
        compiler_params=_params("parallel", "parallel", "arbitrary"),
    )(q, k, v)


def _attn_delta(do, o):
    t, w = do.shape
    tm = _tile(t, 512)
    groups = w // LANES

    def body(do_ref, o_ref, delta_ref):
        prod = do_ref[...].astype(F32) * o_ref[...].astype(F32)
        for g in range(groups):
            x = prod[:, LANES * g:LANES * (g + 1)]
            lo = _lo_mask(x.shape)
            d0 = jnp.sum(jnp.where(lo, x, 0.0), axis=-1, keepdims=True)
            d1 = jnp.sum(jnp.where(lo, 0.0, x), axis=-1, keepdims=True)
            delta_ref[2 * g] = jnp.broadcast_to(d0, (tm, LANES))
            delta_ref[2 * g + 1] = jnp.broadcast_to(d1, (tm, LANES))

    return pl.pallas_call(
        body, name="attn_delta", grid=(t // tm,),
        in_specs=[_row_spec(tm, w), _row_spec(tm, w)],
        out_specs=pl.BlockSpec((2 * groups, tm, LANES), lambda i: (0, i, 0)),
        out_shape=jax.ShapeDtypeStruct((2 * groups, t, LANES), F32),
        compiler_params=_params("parallel"),
    )(do, o)


def _attn_bwd(q, k, v, do, lse, delta, shared_k, name):
    t = q.shape[0]
    tq, tk = _tile(t, ATTN_TQ), _tile(t, ATTN_TK)
    nq, nk = t // tq, t // tk
    wq, q_spec, k_spec, v_spec = _attn_specs(shared_k, tq, tk, lambda p, j, i: i, lambda p, j, i: j)
    groups = q.shape[1] // wq

    def body(q_ref, k_ref, v_ref, do_ref, lse_ref, delta_ref, dq_ref, dk_ref, dv_ref, dk_s, dv_s, s_s, dp_s, p_s,
             ds_s):
        kb, qb = pl.program_id(1), pl.program_id(2)

        @pl.when(qb == 0)
        def _():
            dk_s[...] = jnp.zeros_like(dk_s)
            dv_s[...] = jnp.zeros_like(dv_s)

        qv, kv, vv, dov = q_ref[...], k_ref[...], v_ref[...], do_ref[...]
        lo = _lo_mask(dov.shape)
        heads = []
        for i in range(2):
            qi, ki = _head_operands(qv, kv, i, shared_k)
            keep = lo if i == 0 else jnp.logical_not(lo)
            doi = jnp.where(keep, dov, jnp.zeros_like(dov))
            heads.append((qi, ki, doi))
            s_s[i] = lax.dot_general(qi, ki, _NT, preferred_element_type=F32)
            dp_s[i] = lax.dot_general(doi, vv, _NT, preferred_element_type=F32)
        for i in range(2):
            lse_i, delta_i = lse_ref[i], delta_ref[i]
            for c in range(tk // LANES):
                cols = slice(c * LANES, (c + 1) * LANES)
                p = jnp.exp2(s_s[i, :, cols] - lse_i)
                p_s[i, :, cols] = p.astype(BF16)
                ds_s[i, :, cols] = (p * (dp_s[i, :, cols] - delta_i)).astype(BF16)
        dq_parts = []
        for i in range(2):
            qi, ki, doi = heads[i]
            dv_s[...] += lax.dot_general(doi, p_s[i], _TN, preferred_element_type=F32)
            dk_i = lax.dot_general(qi, ds_s[i], _TN, preferred_element_type=F32)
            if shared_k:
                dk_s[...] += dk_i
            else:
                dk_s[LANES * i:LANES * (i + 1), :] += dk_i
            dq_parts.append(lax.dot_general(ds_s[i], ki, _NN, preferred_element_type=F32))
        rows = pl.ds(pl.multiple_of(qb * tq, tq), tq)
        if shared_k:
            tiles = [(slice(0, LANES), jnp.where(lo, dq_parts[0], dq_parts[1]))]
        else:
            tiles = [(slice(0, LANES), dq_parts[0]), (slice(LANES, 2 * LANES), dq_parts[1])]
        for cols, val in tiles:
            @pl.when(kb == 0)
            def _(cols=cols, val=val):
                dq_ref[rows, cols] = val

            @pl.when(kb > 0)
            def _(cols=cols, val=val):
                dq_ref[rows, cols] += val

        @pl.when(qb == nq - 1)
        def _():
            if shared_k:
                dk_ref[0] = dk_s[...]
                dv_ref[0] = dv_s[...]
            else:
                dk_ref[...] = dk_s[...]
                dv_ref[...] = dv_s[...]

    stat_spec = pl.BlockSpec((2, tq, LANES), lambda p, j, i: (p, i, 0))
    do_spec = pl.BlockSpec((tq, LANES), lambda p, j, i: (i, p))
    dq_spec = pl.BlockSpec((t, wq), lambda p, j, i: (0, p))
    if shared_k:
        dk_spec = pl.BlockSpec((1, LANES, tk), lambda p, j, i: (p, 0, j))
        dv_spec = dk_spec
        dk_shape = jax.ShapeDtypeStruct((groups, LANES, t), F32)
        dv_shape = dk_shape
    else:
        dk_spec = pl.BlockSpec((wq, tk), lambda p, j, i: (p, j))
        dv_spec = pl.BlockSpec((LANES, tk), lambda p, j, i: (p, j))
        dk_shape = jax.ShapeDtypeStruct((wq * groups, t), F32)
        dv_shape = jax.ShapeDtypeStruct((LANES * groups, t), F32)
    return pl.pallas_call(
        body, name=name, grid=(groups, nk, nq),
        in_specs=[q_spec, k_spec, v_spec, do_spec, stat_spec, stat_spec],
        out_specs=[dq_spec, dk_spec, dv_spec],
        out_shape=[jax.ShapeDtypeStruct((t, wq * groups), F32), dk_shape, dv_shape],
        scratch_shapes=[pltpu.VMEM((wq, tk), F32), pltpu.VMEM((LANES, tk), F32), pltpu.VMEM((2, tq, tk), F32),
                        pltpu.VMEM((2, tq, tk), F32), pltpu.VMEM((2, tq, tk), BF16), pltpu.VMEM((2, tq, tk), BF16)],
        compiler_params=_params("parallel", "arbitrary", "arbitrary"),
    )(q, k, v, do, lse, delta)


_MERGE_W = 512
_GATE_BLK0 = Z_GATE // _MERGE_W


def _merge_fwd(z, b_gate, ta, tb):
    t = z.shape[0]
    tm = _tile(t, 512)
    w = _MERGE_W
    nj = D_MODEL // w

    def body(za_ref, zb_ref, ba_ref, bb_ref, ta_ref, tb_ref, o_ref):
        ga = jax.nn.sigmoid(za_ref[...].astype(F32) + ba_ref[...])
        gb = jax.nn.sigmoid(zb_ref[...].astype(F32) + bb_ref[...])
        o_ref[...] = (ga * ta_ref[...].astype(F32) + gb * tb_ref[...].astype(F32)).astype(BF16)

    return pl.pallas_call(
        body, name="merge_fwd", grid=(t // tm, nj),
        in_specs=[pl.BlockSpec((tm, w), lambda i, j: (i, _GATE_BLK0 + j)),
                  pl.BlockSpec((tm, w), lambda i, j: (i, _GATE_BLK0 + nj + j)),
                  pl.BlockSpec((1, w), lambda i, j: (0, j)),
                  pl.BlockSpec((1, w), lambda i, j: (0, nj + j)),
                  pl.BlockSpec((tm, w), lambda i, j: (i, j)),
                  pl.BlockSpec((tm, w), lambda i, j: (i, j))],
        out_specs=pl.BlockSpec((tm, w), lambda i, j: (i, j)),
        out_shape=jax.ShapeDtypeStruct((t, D_MODEL), BF16),
        compiler_params=_params("parallel", "parallel"),
    )(z, z, b_gate, b_gate, ta, tb)


def _merge_bwd(dmg, z, b_gate, ta, tb):
    t = z.shape[0]
    tm = _tile(t, 512)
    w = _MERGE_W
    nj = D_MODEL // w

    def body(dm_ref, za_ref, zb_ref, ba_ref, bb_ref, ta_ref, tb_ref, dta_ref, dtb_ref, dza_ref, dzb_ref,
             dba_ref, dbb_ref):
        dm = dm_ref[...].astype(F32)
        ga = jax.nn.sigmoid(za_ref[...].astype(F32) + ba_ref[...])
        gb = jax.nn.sigmoid(zb_ref[...].astype(F32) + bb_ref[...])
        dta_ref[...] = (dm * ga).astype(BF16)
        dtb_ref[...] = (dm * gb).astype(BF16)
        dza = dm * ta_ref[...].astype(F32) * ga * (1.0 - ga)
        dzb = dm * tb_ref[...].astype(F32) * gb * (1.0 - gb)
        dza_ref[...] = dza.astype(BF16)
        dzb_ref[...] = dzb.astype(BF16)

        @pl.when(pl.program_id(1) == 0)
        def _():
            dba_ref[...] = jnp.zeros_like(dba_ref)
            dbb_ref[...] = jnp.zeros_like(dbb_ref)

        dba_ref[...] += _fold8(dza)
        dbb_ref[...] += _fold8(dzb)

    blk = pl.BlockSpec((tm, w), lambda j, i: (i, j))
    acc = pl.BlockSpec((SUBLANES, w), lambda j, i: (0, j))
    return pl.pallas_call(
        body, name="merge_bwd", grid=(nj, t // tm),
        in_specs=[blk,
                  pl.BlockSpec((tm, w), lambda j, i: (i, _GATE_BLK0 + j)),
                  pl.BlockSpec((tm, w), lambda j, i: (i, _GATE_BLK0 + nj + j)),
                  pl.BlockSpec((1, w), lambda j, i: (0, j)),
                  pl.BlockSpec((1, w), lambda j, i: (0, nj + j)),
                  blk, blk],
        out_specs=[blk, blk, blk, blk, acc, acc],
        out_shape=[jax.ShapeDtypeStruct((t, D_MODEL), BF16)] * 4 + [jax.ShapeDtypeStruct((SUBLANES, D_MODEL), F32)] * 2,
        compiler_params=_params("parallel", "arbitrary"),
    )(dmg, z, z, b_gate, b_gate, ta, tb)


def _loss_grad(y, target):
    t, d = y.shape
    tm = _tile(t, 512)

    def body(y_ref, t_ref, dy_ref, acc_ref):
        err = y_ref[...] - t_ref[...]
        dy_ref[...] = err * (1.0 / d)
        e8 = _fold8(err * err)
        part = e8[:, 0:LANES]
        for c in range(1, d // LANES):
            part = part + e8[:, LANES * c:LANES * (c + 1)]

        @pl.when(pl.program_id(0) == 0)
        def _():
            acc_ref[...] = jnp.zeros_like(acc_ref)

        acc_ref[...] += part

    return pl.pallas_call(
        body, name="loss_grad", grid=(t // tm,),
        in_specs=[_row_spec(tm, d), _row_spec(tm, d)],
        out_specs=[_row_spec(tm, d), _acc_spec(LANES)],
        out_shape=[jax.ShapeDtypeStruct((t, d), F32), jax.ShapeDtypeStruct((SUBLANES, LANES), F32)],
        compiler_params=_params("arbitrary"),
    )(y, target)


_MESH_ID = pl.DeviceIdType.MESH
_ANY = pl.BlockSpec(memory_space=pl.ANY)


def _all_gather(arrays):
    n = len(arrays)

    def body(*refs):
        x_refs, out_refs = refs[:n], refs[n:2 * n]
        send_sems, recv_sems, local_sems = refs[2 * n:]
        mx, my, mc = lax.axis_index("x"), lax.axis_index("y"), lax.axis_index("c")
        me, sibling = (mx, my, mc), (mx, my, 1 - mc)
        chips = [(1 - mx, my), (mx, 1 - my), (1 - mx, 1 - my)]

        def slot(a, px, py, pc):
            return out_refs[a].at[4 * px + 2 * py + pc]

        def copy(a, sem, block, to, src=None):
            return pltpu.make_async_remote_copy(
                src_ref=slot(a, *block) if src is None else src, dst_ref=slot(a, *block),
                send_sem=send_sems.at[a, sem], recv_sem=recv_sems.at[a, sem], device_id=to, device_id_type=_MESH_ID)

        mine = [pltpu.make_async_copy(x_refs[a], slot(a, *me), local_sems.at[a]) for a in range(n)]
        first = []
        for a in range(n):
            mine[a].start()
            first.append(copy(a, 0, me, sibling, src=x_refs[a]))
            first += [copy(a, 1 + j, me, (*chip, mc), src=x_refs[a]) for j, chip in enumerate(chips)]
        for cp in first:
            cp.start()
        passed = []
        for a in range(n):
            for j, chip in enumerate(chips):
                copy(a, 1 + j, (*chip, mc), me).wait_recv()
                passed.append(copy(a, 4 + j, (*chip, mc), sibling))
                passed[-1].start()
        for a in range(n):
            copy(a, 0, sibling, me).wait_recv()
            for j, chip in enumerate(chips):
                copy(a, 4 + j, (*chip, 1 - mc), me).wait_recv()
        for cp in first + passed:
            cp.wait_send()
        for cp in mine:
            cp.wait()

    return pl.pallas_call(
        body, name="weight_all_gather",
        out_shape=[jax.ShapeDtypeStruct((N_DEV,) + a.shape, a.dtype) for a in arrays],
        in_specs=[_ANY] * n, out_specs=[_ANY] * n,
        scratch_shapes=[pltpu.SemaphoreType.DMA((n, 7)), pltpu.SemaphoreType.DMA((n, 7)),
                        pltpu.SemaphoreType.DMA((n,))],
    )(*arrays)


def _pair_exchange(sends):
    n = len(sends)

    def body(*refs):
        s_refs, r_refs = refs[:n], refs[n:2 * n]
        send_sems, recv_sems = refs[2 * n:]
        mx, my, mc = lax.axis_index("x"), lax.axis_index("y"), lax.axis_index("c")
        copies = []
        for a in range(n):
            for ch in range(4):
                cp = pltpu.make_async_remote_copy(
                    src_ref=s_refs[a].at[2 * ch + (1 - mc)], dst_ref=r_refs[a].at[ch], send_sem=send_sems.at[a, ch],
                    recv_sem=recv_sems.at[a, ch], device_id=(mx, my, 1 - mc), device_id_type=_MESH_ID)
                cp.start()
                copies.append(cp)
        for cp in copies:
            cp.wait_send()
            cp.wait_recv()

    return pl.pallas_call(
        body, name="grad_pair_exchange",
        out_shape=[jax.ShapeDtypeStruct((4,) + s.shape[1:], s.dtype) for s in sends],
        in_specs=[_ANY] * n, out_specs=[_ANY] * n,
        scratch_shapes=[pltpu.SemaphoreType.DMA((n, 4)), pltpu.SemaphoreType.DMA((n, 4))],
    )(*sends)


def _pair_add(send, half, core):
    _, r, c_ = send.shape
    tr = _row_tile(r, c_)

    def body(core_ref, s_ref, h_ref, o_ref):
        del core_ref
        o_ref[...] = (s_ref[...] + h_ref[...]).astype(BF16)

    blk = pl.BlockSpec((1, tr, c_), lambda ch, i, core_ref: (ch, i, 0))
    return pl.pallas_call(
        body, name="grad_pair_add",
        grid_spec=pltpu.PrefetchScalarGridSpec(
            num_scalar_prefetch=1, grid=(4, r // tr),
            in_specs=[pl.BlockSpec((1, tr, c_), lambda ch, i, core_ref: (2 * ch + core_ref[0], i, 0)), blk],
            out_specs=blk),
        out_shape=jax.ShapeDtypeStruct((4, r, c_), BF16),
        compiler_params=_params("parallel", "parallel"),
    )(core, send, half)


def _chip_exchange(parts):
    n = len(parts)

    def body(*refs):
        p_refs, r_refs = refs[:n], refs[n:2 * n]
        send_sems, recv_sems, local_sems = refs[2 * n:]
        mx, my, mc = lax.axis_index("x"), lax.axis_index("y"), lax.axis_index("c")
        mine = 2 * mx + my
        local = [pltpu.make_async_copy(p_refs[a].at[mine], r_refs[a].at[mine], local_sems.at[a]) for a in range(n)]
        copies = []
        for a in range(n):
            local[a].start()
            for rel in range(1, 4):
                px = 1 - mx if rel & 2 else mx
                py = 1 - my if rel & 1 else my
                cp = pltpu.make_async_remote_copy(
                    src_ref=p_refs[a].at[2 * px + py], dst_ref=r_refs[a].at[mine], send_sem=send_sems.at[a, rel - 1],
                    recv_sem=recv_sems.at[a, rel - 1], device_id=(px, py, mc), device_id_type=_MESH_ID)
                cp.start()
                copies.append(cp)
        for cp in copies:
            cp.wait_send()
            cp.wait_recv()
        for cp in local:
            cp.wait()

    return pl.pallas_call(
        body, name="grad_chip_exchange",
        out_shape=[jax.ShapeDtypeStruct(p.shape, p.dtype) for p in parts],
        in_specs=[_ANY] * n, out_specs=[_ANY] * n,
        scratch_shapes=[pltpu.SemaphoreType.DMA((n, 3)), pltpu.SemaphoreType.DMA((n, 3)),
                        pltpu.SemaphoreType.DMA((n,))],
    )(*parts)


def _row_tile(r, c_):
    tr = min(r, ADAM_BLOCK_ELEMS // (pl.cdiv(c_, LANES) * LANES))
    while r % tr:
        tr -= SUBLANES
    return tr


def _adamw(recv, w, m, v):
    r, c_ = w.shape
    tr = _row_tile(r, c_)
    n_src = recv.shape[0]

    def body(g_ref, w_ref, m_ref, v_ref, go_ref, d_ref, mo_ref, vo_ref):
        g = g_ref[0].astype(F32)
        for s in range(1, n_src):
            g = g + g_ref[s].astype(F32)
        go_ref[...] = g
        mn = ADAM_B1 * m_ref[...] + (1.0 - ADAM_B1) * g
        vn = ADAM_B2 * v_ref[...] + (1.0 - ADAM_B2) * (g * g)
        mo_ref[...] = mn
        vo_ref[...] = vn
        m_hat = mn / (1.0 - ADAM_B1 ** ADAM_STEP)
        v_hat = vn / (1.0 - ADAM_B2 ** ADAM_STEP)
        d_ref[...] = -ADAM_LR * (m_hat / (jnp.sqrt(v_hat) + ADAM_EPS) + ADAM_WD * w_ref[...])

    spec = pl.BlockSpec((tr, c_), lambda i: (i, 0))
    out = jax.ShapeDtypeStruct((r, c_), F32)
    return pl.pallas_call(
        body, name="grad_sum_adamw", grid=(r // tr,),
        in_specs=[pl.BlockSpec((n_src, tr, c_), lambda i: (0, i, 0)), spec, spec, spec],
        out_specs=[spec, spec, spec, spec], out_shape=[out, out, out, out],
        compiler_params=_params("parallel"),
    )(recv, w, m, v)


def _pad_cols(a, before, after):
    parts = []
    if before:
        parts.append(jnp.zeros(a.shape[:-1] + (before,), a.dtype))
    parts.append(a)
    if after:
        parts.append(jnp.zeros(a.shape[:-1] + (after,), a.dtype))
    return jnp.concatenate(parts, axis=-1)


def _q_head_pairs(a, axis):
    shp = a.shape
    a = a.reshape(shp[:axis] + (GQA_KV_HEADS, GQA_GROUP, HEAD_DIM) + shp[axis + 1:])
    a = jnp.swapaxes(a, axis, axis + 1)
    return a.reshape(shp)


def _q_head_unpairs(a, axis):
    shp = a.shape
    a = a.reshape(shp[:axis] + (GQA_GROUP, GQA_KV_HEADS, HEAD_DIM) + shp[axis + 1:])
    a = jnp.swapaxes(a, axis, axis + 1)
    return a.reshape(shp)


def _layout_weights(w):
    w_in = w["w_in"]
    lead = w_in.shape[:-1]
    w_in_p = jnp.concatenate([
        _q_head_pairs(w_in[..., 0:512], w_in.ndim - 1),
        w_in[..., 512:1408],
        _pad_cols(w_in[..., 1408:1440], KR_LANE0, LANES - KR_LANE0 - MLA_ROPE_DIM),
        w_in[..., 1440:],
    ], axis=-1)
    wq = w["w_q_up"]
    wq_p = _pad_cols(wq.reshape(wq.shape[:-1] + (MLA_HEADS, MLA_QK_DIM)), 0, LANES - MLA_QK_DIM)
    wq_p = wq_p.reshape(wq.shape[:-1] + (MLA_HEADS * LANES,))
    wkv = w["w_kv_up"]
    wkv4 = wkv.reshape(wkv.shape[:-1] + (MLA_HEADS, 2 * HEAD_DIM))
    wk_p = _pad_cols(wkv4[..., :HEAD_DIM], 0, LANES - HEAD_DIM).reshape(wkv.shape[:-1] + (MLA_HEADS * LANES,))
    wv_p = wkv4[..., HEAD_DIM:].reshape(wkv.shape[:-1] + (MLA_HEADS * HEAD_DIM,))
    del lead
    return {
        "w_in": w_in_p, "w_q_up": wq_p, "w_kv_up": jnp.concatenate([wk_p, wv_p], axis=-1),
        "w_branch_a": _q_head_pairs(w["w_branch_a"], w["w_branch_a"].ndim - 2), "w_branch_b": w["w_branch_b"],
        "w_o": w["w_o"], "w_ffn_up": w["w_ffn_up"], "w_ffn_down": w["w_ffn_down"],
    }


def _unlayout_grads(g):
    gi = g["w_in"]
    kr0 = Z_KR + KR_LANE0
    g_in = jnp.concatenate([
        _q_head_unpairs(gi[..., 0:512], gi.ndim - 1), gi[..., 512:1408], gi[..., kr0:kr0 + MLA_ROPE_DIM],
        gi[..., Z_GATE:],
    ], axis=-1)
    gq = g["w_q_up"]
    gq = gq.reshape(gq.shape[:-1] + (MLA_HEADS, LANES))[..., :MLA_QK_DIM]
    gq = gq.reshape(gq.shape[:-2] + (MLA_HEADS * MLA_QK_DIM,))
    gkv = g["w_kv_up"]
    gk = gkv[..., :MLA_HEADS * LANES].reshape(gkv.shape[:-1] + (MLA_HEADS, LANES))[..., :HEAD_DIM]
    gv = gkv[..., MLA_HEADS * LANES:].reshape(gkv.shape[:-1] + (MLA_HEADS, HEAD_DIM))
    gkv = jnp.concatenate([gk, gv], axis=-1).reshape(gkv.shape[:-1] + (MLA_HEADS * 2 * HEAD_DIM,))
    return {
        "w_in": g_in, "w_q_up": gq, "w_kv_up": gkv,
        "w_branch_a": _q_head_unpairs(g["w_branch_a"], g["w_branch_a"].ndim - 2), "w_branch_b": g["w_branch_b"],
        "w_o": g["w_o"], "w_ffn_up": g["w_ffn_up"], "w_ffn_down": g["w_ffn_down"],
    }


def _pack_small(parts):
    flat = jnp.concatenate([p.reshape(-1) for p in parts])
    pad = (-flat.shape[0]) % (SUBLANES * LANES)
    if pad:
        flat = jnp.concatenate([flat, jnp.zeros((pad,), flat.dtype)])
    return flat.reshape(-1, LANES)


def _unpack_small(packed, shapes):
    flat = packed.reshape(-1)
    out, off = [], 0
    for shp in shapes:
        n = int(np.prod(shp))
        out.append(flat[off:off + n].reshape(shp))
        off += n
    return out


def _shards_of(full, axis):
    shp = full.shape
    cut = shp[:axis] + (N_DEV, shp[axis] // N_DEV) + shp[axis + 1:]
    return jnp.moveaxis(full.reshape(cut), axis, 0)


def _from_shards(shards, axis):
    full = list(shards.shape[1:])
    full[axis] *= N_DEV
    return jnp.moveaxis(shards, 0, axis).reshape(full)


def _rows2d(a):
    return a.reshape(-1, a.shape[-1])


def _layer_fwd(x, u, lw, tabs):
    cos_a, sin_a, cos_b, sin_b = tabs
    z = _matmul(u, lw["w_in"], "nn", "mm_in")
    qa, ka, va, cqn, ckvn, krr = _prep_a_fwd(z, lw["gq2"], lw["gk2"], lw["gqa"], lw["gkva"], cos_a, sin_a, cos_b, sin_b)
    qb = _matmul(cqn, lw["w_q_up"], "nn", "mm_q_up")
    kvb = _matmul(ckvn, lw["w_kv_up"], "nn", "mm_kv_up")
    q_b, k_b, v_b = _prep_b_fwd(qb, kvb, krr, cos_b, sin_b)
    ya, lse_a = _attn_fwd(qa, ka, va, True, "gqa_fwd")
    yb, lse_b = _attn_fwd(q_b, k_b, v_b, False, "mla_fwd")
    ta = _matmul(ya, lw["w_branch_a"], "nn", "mm_branch_a")
    tb = _matmul(yb, lw["w_branch_b"], "nn", "mm_branch_b")
    merged = _merge_fwd(z, lw["b_gate"], ta, tb)
    m = _matmul(merged, lw["w_o"], "nn", "mm_o")
    x2, u2 = _res_norm_fwd(x, m, lw["post_mix_g"], lw["pre_ffn_g"])
    h, a = _matmul(u2, lw["w_ffn_up"], "nn", "mm_ffn_up", post="relu2")
    f = _matmul(a, lw["w_ffn_down"], "nn", "mm_ffn_down")
    x3, u_next = _res_norm_fwd(x2, f, lw["post_ffn_g"], lw["next_pre_mix_g"])
    saved = dict(u=u, z=z, qa=qa, ka=ka, va=va, cqn=cqn, ckvn=ckvn, q_b=q_b, k_b=k_b, v_b=v_b, ya=ya, yb=yb,
                 lse_a=lse_a, lse_b=lse_b, ta=ta, tb=tb, merged=merged, m=m, x2=x2, u2=u2, h=h, a=a, f=f, x3=x3)
    return x3, u_next, saved


def _layer_bwd(dx3, du_next, lw, sv, tabs, gbuf, layer):
    cos_a, sin_a, cos_b, sin_b = tabs
    g = {}
    dx3, df, dg4, dg1n = _res_norm_bwd(sv["x3"], sv["f"], lw["post_ffn_g"], lw["next_pre_mix_g"], dx3, du_next)
    g["post_ffn_g"], g["next_pre_mix_g"] = dg4, dg1n
    dh = _matmul(df, lw["w_ffn_down"], "nt", "mm_d_h", post="relu2_bwd", h=sv["h"])
    g["w_ffn_down"] = _matmul(sv["a"], df, "tn", "mm_dw_ffn_down", stack=(gbuf["w_ffn_down"], layer))
    du2 = _matmul(dh, lw["w_ffn_up"], "nt", "mm_d_u2")
    g["w_ffn_up"] = _matmul(sv["u2"], dh, "tn", "mm_dw_ffn_up", stack=(gbuf["w_ffn_up"], layer))
    dx2, dm, dg2, dg3 = _res_norm_bwd(sv["x2"], sv["m"], lw["post_mix_g"], lw["pre_ffn_g"], dx3, du2)
    g["post_mix_g"], g["pre_ffn_g"] = dg2, dg3
    dmg = _matmul(dm, lw["w_o"], "nt", "mm_d_merged")
    g["w_o"] = _matmul(sv["merged"], dm, "tn", "mm_dw_o", stack=(gbuf["w_o"], layer))
    dta, dtb, dzg_a, dzg_b, db_a, db_b = _merge_bwd(dmg, sv["z"], lw["b_gate"], sv["ta"], sv["tb"])
    g["b_gate"] = jnp.concatenate([db_a, db_b], axis=-1)
    dya = _matmul(dta, lw["w_branch_a"], "nt", "mm_d_ya")
    g["w_branch_a"] = _matmul(sv["ya"], dta, "tn", "mm_dw_branch_a", stack=(gbuf["w_branch_a"], layer))
    dyb = _matmul(dtb, lw["w_branch_b"], "nt", "mm_d_yb")
    g["w_branch_b"] = _matmul(sv["yb"], dtb, "tn", "mm_dw_branch_b", stack=(gbuf["w_branch_b"], layer))
    delta_a = _attn_delta(dya, sv["ya"])
    delta_b = _attn_delta(dyb, sv["yb"])
    dqa, dka4, dva4 = _attn_bwd(sv["qa"], sv["ka"], sv["va"], dya, sv["lse_a"], delta_a, True, "gqa_bwd")
    dq_b, dk_b, dv_b = _attn_bwd(sv["q_b"], sv["k_b"], sv["v_b"], dyb, sv["lse_b"], delta_b, False, "mla_bwd")
    dqb, dkvb, dkr = _prep_b_bwd(dq_b, dk_b, dv_b, cos_b, sin_b)
    dcqn = _matmul(dqb, lw["w_q_up"], "nt", "mm_d_cqn")
    g["w_q_up"] = _matmul(sv["cqn"], dqb, "tn", "mm_dw_q_up", stack=(gbuf["w_q_up"], layer))
    dckvn = _matmul(dkvb, lw["w_kv_up"], "nt", "mm_d_ckvn")
    g["w_kv_up"] = _matmul(sv["ckvn"], dkvb, "tn", "mm_dw_kv_up", stack=(gbuf["w_kv_up"], layer))
    dz, dgq, dgk, dgqa, dgkva = _prep_a_bwd(sv["z"], dqa, dka4, dva4, dcqn, dckvn, dkr, dzg_a, dzg_b, lw["gq2"],
                                            lw["gk2"], lw["gqa"], lw["gkva"], cos_a, sin_a)
    g["q_norm_g"], g["k_norm_g"], g["q_a_norm_g"], g["kv_a_norm_g"] = dgq, dgk, dgqa, dgkva
    du = _matmul(dz, lw["w_in"], "nt", "mm_d_u")
    g["w_in"] = _matmul(sv["u"], dz, "tn", "mm_dw_in", stack=(gbuf["w_in"], layer))
    return dx2, du, g


def kernel(x, w_in, b_gate, q_norm_g, k_norm_g, q_a_norm_g, kv_a_norm_g, w_q_up, w_kv_up, w_branch_a, w_branch_b, w_o, w_ffn_up, w_ffn_down, pre_mix_g, post_mix_g, pre_ffn_g, post_ffn_g, loss_target, m_w_in, m_b_gate, m_q_norm_g, m_k_norm_g, m_q_a_norm_g, m_kv_a_norm_g, m_w_q_up, m_w_kv_up, m_w_branch_a, m_w_branch_b, m_w_o, m_w_ffn_up, m_w_ffn_down, m_pre_mix_g, m_post_mix_g, m_pre_ffn_g, m_post_ffn_g, v_w_in, v_b_gate, v_q_norm_g, v_k_norm_g, v_q_a_norm_g, v_kv_a_norm_g, v_w_q_up, v_w_kv_up, v_w_branch_a, v_w_branch_b, v_w_o, v_w_ffn_up, v_w_ffn_down, v_pre_mix_g, v_post_mix_g, v_pre_ffn_g, v_post_ffn_g):
    weights = dict(zip(WEIGHT_NAMES, (w_in, b_gate, q_norm_g, k_norm_g, q_a_norm_g, kv_a_norm_g, w_q_up, w_kv_up,
                                      w_branch_a, w_branch_b, w_o, w_ffn_up, w_ffn_down, pre_mix_g, post_mix_g,
                                      pre_ffn_g, post_ffn_g)))
    mom_m = dict(zip(WEIGHT_NAMES, (m_w_in, m_b_gate, m_q_norm_g, m_k_norm_g, m_q_a_norm_g, m_kv_a_norm_g, m_w_q_up,
                                    m_w_kv_up, m_w_branch_a, m_w_branch_b, m_w_o, m_w_ffn_up, m_w_ffn_down,
                                    m_pre_mix_g, m_post_mix_g, m_pre_ffn_g, m_post_ffn_g)))
    mom_v = dict(zip(WEIGHT_NAMES, (v_w_in, v_b_gate, v_q_norm_g, v_k_norm_g, v_q_a_norm_g, v_kv_a_norm_g, v_w_q_up,
                                    v_w_kv_up, v_w_branch_a, v_w_branch_b, v_w_o, v_w_ffn_up, v_w_ffn_down,
                                    v_pre_mix_g, v_post_mix_g, v_pre_ffn_g, v_post_ffn_g)))
    assert x.shape[0] == 1 and x.shape[2] == D_MODEL, x.shape
    n_layers = w_in.shape[0]
    t = x.shape[1]
    x0 = x.reshape(t, D_MODEL)
    target = loss_target.reshape(t, D_MODEL)
    shard_shapes = {n: weights[n].shape for n in BIG_NAMES}
    small_shapes = [weights[n].shape for n in SMALL_NAMES]

    gathered = _all_gather([weights[n].astype(BF16) for n in BIG_NAMES])
    full = {n: _from_shards(g, SHARD_AXIS[n]) for n, g in zip(BIG_NAMES, gathered)}
    lw_all = _layout_weights(full)
    lw_all["b_gate"] = b_gate.reshape(n_layers, 1, 2 * D_MODEL)
    lw_all["gq2"] = jnp.tile(q_norm_g, (1, 2)).reshape(n_layers, 1, LANES)
    lw_all["gk2"] = jnp.tile(k_norm_g, (1, 2)).reshape(n_layers, 1, LANES)
    lw_all["gqa"] = q_a_norm_g.reshape(n_layers, 1, MLA_Q_RANK)
    lw_all["gkva"] = kv_a_norm_g.reshape(n_layers, 1, MLA_KV_RANK)
    for n in ("post_mix_g", "pre_ffn_g", "post_ffn_g"):
        lw_all[n] = weights[n]
    lw_all["next_pre_mix_g"] = jnp.roll(pre_mix_g, -1, axis=0)

    tabs = _rope_tables(t)
    u0 = _rms_fwd(x0, pre_mix_g[0])

    layer_w = [{n: a[l] for n, a in lw_all.items()} for l in range(n_layers)]
    xc, uc, saved = x0, u0, []
    for l in range(n_layers):
        xc, uc, sv = _layer_fwd(xc, uc, layer_w[l], tabs)
        saved.append(sv)
    dy, loss_acc = _loss_grad(xc, target)
    loss = lax.psum(0.5 * jnp.sum(loss_acc) / D_MODEL, ("x", "y", "c"))

    dx0, du0, layer_g = dy, jnp.zeros((t, D_MODEL), F32), [None] * n_layers
    gbuf = {n: lax.empty(lw_all[n].shape, F32) for n in BIG_NAMES}
    for l in reversed(range(n_layers)):
        dx0, du0, layer_g[l] = _layer_bwd(dx0, du0, layer_w[l], saved[l], tabs, gbuf, l)
        gbuf = {n: layer_g[l][n] for n in BIG_NAMES}
    grads = {n: jnp.stack([g[n] for g in layer_g]) for n in layer_g[0] if n not in BIG_NAMES}
    grads.update(gbuf)
    grad_x, dg1_first = _rms_bwd(x0, pre_mix_g[0], dx0, du0)

    big_grads = _unlayout_grads({n: grads[n] for n in BIG_NAMES})
    fold = lambda a: a.sum(axis=1)
    dgq = fold(grads["q_norm_g"]).reshape(n_layers, 2, HEAD_DIM).sum(axis=1)
    dgk = fold(grads["k_norm_g"]).reshape(n_layers, 2, HEAD_DIM).sum(axis=1)
    dg1 = jnp.concatenate([fold(dg1_first[None]), fold(grads["next_pre_mix_g"])[:-1]], axis=0)
    small_grads = {
        "b_gate": fold(grads["b_gate"]), "q_norm_g": dgq, "k_norm_g": dgk, "q_a_norm_g": fold(grads["q_a_norm_g"]),
        "kv_a_norm_g": fold(grads["kv_a_norm_g"]), "pre_mix_g": dg1, "post_mix_g": fold(grads["post_mix_g"]),
        "pre_ffn_g": fold(grads["pre_ffn_g"]), "post_ffn_g": fold(grads["post_ffn_g"]),
    }
    small_packed = _pack_small([small_grads[n] for n in SMALL_NAMES])
    sends = [_shards_of(big_grads[n], SHARD_AXIS[n]).reshape((N_DEV,) + _rows2d(weights[n]).shape)
             for n in BIG_NAMES]
    sends.append(jnp.broadcast_to(small_packed[None], (N_DEV,) + small_packed.shape))
    halves = _pair_exchange(sends)
    core = lax.axis_index("c").astype(jnp.int32).reshape(1)
    recvs = _chip_exchange([_pair_add(s, h, core) for s, h in zip(sends, halves)])

    results = {}
    for n, recv in zip(BIG_NAMES, recvs):
        res = _adamw(recv, _rows2d(weights[n]), _rows2d(mom_m[n]), _rows2d(mom_v[n]))
        results[n] = [r.reshape(shard_shapes[n]) for r in res]
    res = _adamw(recvs[-1], *[_pack_small([d[n] for n in SMALL_NAMES]) for d in (weights, mom_m, mom_v)])
    for kind, packed_out in enumerate(res):
        for n, val in zip(SMALL_NAMES, _unpack_small(packed_out, small_shapes)):
            results.setdefault(n, [None] * 4)[kind] = val
    outs = [results[n][kind] for kind in range(4) for n in WEIGHT_NAMES]
    return (loss, grad_x.reshape(x.shape), *outs)
```

```python
import math

import jax
import jax.numpy as jnp
import numpy as np
from jax import lax
from jax.experimental import pallas as pl
from jax.experimental.pallas import tpu as pltpu

F32 = jnp.float32
BF16 = jnp.bfloat16

D_MODEL = 1024
GRID_W = 64
ROPE_THETA = 10000.0
EPS = 1e-6
GQA_HEADS = 8
GQA_KV_HEADS = 2
GQA_GROUP = GQA_HEADS // GQA_KV_HEADS
HEAD_DIM = 64
MLA_HEADS = 8
MLA_ROPE_DIM = 32
MLA_QK_DIM = 96
MLA_Q_RANK = 384
MLA_KV_RANK = 256
GQA_SCALE = 1.0 / math.sqrt(HEAD_DIM)
MLA_SCALE = 1.0 / math.sqrt(MLA_QK_DIM)
LOG2E = math.log2(math.e)
LN2 = math.log(2.0)

ADAM_LR = 0.001
ADAM_B1 = 0.9
ADAM_B2 = 0.999
ADAM_EPS = 1e-08
ADAM_WD = 0.01
ADAM_STEP = 10

N_DEV = 8
LANES = 128
SUBLANES = 8
VMEM_LIMIT = 48 * 1024 * 1024

Z_QA, Z_KA, Z_VA, Z_CQ, Z_CKV, Z_KR, Z_GATE = 0, 512, 640, 768, 1152, 1408, 1536
Z_ATT_W = 1536
Z_W = 3584
KR_LANE0 = 64

WEIGHT_NAMES = ("w_in", "b_gate", "q_norm_g", "k_norm_g", "q_a_norm_g", "kv_a_norm_g", "w_q_up", "w_kv_up",
                "w_branch_a", "w_branch_b", "w_o", "w_ffn_up", "w_ffn_down", "pre_mix_g", "post_mix_g",
                "pre_ffn_g", "post_ffn_g")
SHARD_AXIS = {"w_in": 2, "w_q_up": 2, "w_kv_up": 2, "w_branch_a": 2, "w_branch_b": 2, "w_o": 1, "w_ffn_up": 2,
              "w_ffn_down": 1}
BIG_NAMES = tuple(n for n in WEIGHT_NAMES if n in SHARD_AXIS)
SMALL_NAMES = tuple(n for n in WEIGHT_NAMES if n not in SHARD_AXIS)
ADAM_BLOCK_ELEMS = 256 * 1024
MM_TILE = 1024
MM_TILE_TOKENS = 2048
MM_TILE_K = 2048
PREP_ROWS = 512
ATTN_TQ = 1024
ATTN_TK = 1024


def _params(*semantics):
    return pltpu.CompilerParams(dimension_semantics=semantics, vmem_limit_bytes=VMEM_LIMIT)


def _tile(n, pref):
    if n <= pref:
        return n
    t = (pref // LANES) * LANES
    while n % t:
        t -= LANES
    return t


def _fold8(t):
    return t.reshape(t.shape[0] // SUBLANES, SUBLANES, t.shape[1]).sum(axis=0)


_DIMS = {"nn": ((1,), (0,)), "nt": ((1,), (1,)), "tn": ((0,), (0,))}


def _matmul(a, b, mode, name, post=None, h=None, stack=None):
    out_dt = F32 if mode == "tn" else BF16
    if mode == "nn":
        (m, k), n = a.shape, b.shape[1]
    elif mode == "nt":
        (m, k), n = a.shape, b.shape[0]
    else:
        (k, m), n = a.shape, b.shape[1]
    tm = _tile(m, MM_TILE_TOKENS if mode != "tn" and k <= MM_TILE else MM_TILE)
    tn, tk = _tile(n, MM_TILE), _tile(k, MM_TILE_K)
    nk = k // tk
    dims = (_DIMS[mode], ((), ()))
    operands = [a, b] + ([h] if post == "relu2_bwd" else []) + ([stack[0]] if stack else [])
    n_in = len(operands)
    n_out = 2 if post == "relu2" else 1

    def body(*refs):
        a_ref, b_ref = refs[:2]
        o_refs, acc_ref = refs[n_in:n_in + n_out], refs[-1]

        def finish(val):
            if post == "relu2":
                o_refs[0][...] = val.astype(out_dt)
                r = jnp.maximum(val, 0.0)
                o_refs[1][...] = (r * r).astype(BF16)
            elif post == "relu2_bwd":
                o_refs[0][...] = (val * (2.0 * jnp.maximum(refs[2][...].astype(F32), 0.0))).astype(BF16)
            else:
                o_refs[0][...] = val.astype(out_dt)

        prod = lax.dot_general(a_ref[...], b_ref[...], dims, preferred_element_type=F32)
        if nk == 1:
            finish(prod)
        else:
            kk = pl.program_id(2)

            @pl.when(kk == 0)
            def _():
                acc_ref[...] = prod

            @pl.when(kk > 0)
            def _():
                acc_ref[...] += prod

            @pl.when(kk == nk - 1)
            def _():
                finish(acc_ref[...])

    if mode == "tn":
        a_spec = pl.BlockSpec((tk, tm), lambda i, j, kk: (kk, i))
    else:
        a_spec = pl.BlockSpec((tm, tk), lambda i, j, kk: (i, kk))
    if mode == "nt":
        b_spec = pl.BlockSpec((tn, tk), lambda i, j, kk: (j, kk))
    else:
        b_spec = pl.BlockSpec((tk, tn), lambda i, j, kk: (kk, j))
    o_spec = pl.BlockSpec((tm, tn), lambda i, j, kk: (i, j))
    main_out, bf16_out = jax.ShapeDtypeStruct((m, n), out_dt), jax.ShapeDtypeStruct((m, n), BF16)
    out_shape = {None: main_out, "relu2": [main_out, bf16_out], "relu2_bwd": bf16_out}[post]
    in_specs = [a_spec, b_spec] + ([o_spec] if post == "relu2_bwd" else [])
    out_specs = [o_spec, o_spec] if post == "relu2" else o_spec
    aliases = {}
    if stack:
        buf, layer = stack
        assert post is None and buf.shape[1:] == (m, n) and buf.dtype == out_dt, (buf.shape, buf.dtype)
        in_specs.append(pl.BlockSpec(memory_space=pl.ANY))
        out_specs = pl.BlockSpec((None, tm, tn), lambda i, j, kk: (layer, i, j))
        out_shape = jax.ShapeDtypeStruct(buf.shape, buf.dtype)
        aliases = {n_in - 1: 0}
    return pl.pallas_call(
        body,
        name=name,
        grid=(m // tm, n // tn, nk),
        in_specs=in_specs,
        out_specs=out_specs,
        out_shape=out_shape,
        scratch_shapes=[pltpu.VMEM((tm, tn), F32)],
        input_output_aliases=aliases,
        compiler_params=_params("parallel", "parallel", "arbitrary"),
    )(*operands)


def _rinv(x):
    return lax.rsqrt(jnp.mean(x * x, axis=-1, keepdims=True) + EPS)


def _rms_bwd_rows(x, g, dy):
    r = _rinv(x)
    xh = x * r
    dxh = dy * g
    dx = r * (dxh - xh * jnp.mean(dxh * xh, axis=-1, keepdims=True))
    return dx, dy * xh


def _row_spec(tm, c):
    return pl.BlockSpec((tm, c), lambda i: (i, 0))


def _vec_spec(c):
    return pl.BlockSpec((1, c), lambda i: (0, 0))


def _acc_spec(c):
    return pl.BlockSpec((SUBLANES, c), lambda i: (0, 0))


def _rms_fwd(x, g):
    t, d = x.shape
    tm = _tile(t, 512)

    def body(x_ref, g_ref, o_ref):
        xv = x_ref[...]
        o_ref[...] = (xv * _rinv(xv) * g_ref[...]).astype(BF16)

    return pl.pallas_call(
        body, name="rms_fwd", grid=(t // tm,),
        in_specs=[_row_spec(tm, d), _vec_spec(d)], out_specs=_row_spec(tm, d),
        out_shape=jax.ShapeDtypeStruct((t, d), BF16), compiler_params=_params("parallel"),
    )(x, g.reshape(1, d))


def _rms_bwd(x, g, dres, dy):
    t, d = x.shape
    tm = _tile(t, 512)

    def body(x_ref, g_ref, dres_ref, dy_ref, dx_ref, dg_ref):
        dx, dgc = _rms_bwd_rows(x_ref[...], g_ref[...], dy_ref[...].astype(F32))
        dx_ref[...] = dres_ref[...] + dx

        @pl.when(pl.program_id(0) == 0)
        def _():
            dg_ref[...] = jnp.zeros_like(dg_ref)

        dg_ref[...] += _fold8(dgc)

    return pl.pallas_call(
        body, name="rms_bwd", grid=(t // tm,),
        in_specs=[_row_spec(tm, d), _vec_spec(d), _row_spec(tm, d), _row_spec(tm, d)],
        out_specs=[_row_spec(tm, d), _acc_spec(d)],
        out_shape=[jax.ShapeDtypeStruct((t, d), F32), jax.ShapeDtypeStruct((SUBLANES, d), F32)],
        compiler_params=_params("arbitrary"),
    )(x, g.reshape(1, d), dres, dy)


def _res_norm_fwd(x, m, g_post, g_next):
    t, d = x.shape
    tm = _tile(t, 512)

    def body(x_ref, m_ref, gp_ref, gn_ref, x2_ref, u2_ref):
        mv = m_ref[...].astype(F32)
        x2 = x_ref[...] + mv * _rinv(mv) * gp_ref[...]
        x2_ref[...] = x2
        u2_ref[...] = (x2 * _rinv(x2) * gn_ref[...]).astype(BF16)

    return pl.pallas_call(
        body, name="res_norm_fwd", grid=(t // tm,),
        in_specs=[_row_spec(tm, d), _row_spec(tm, d), _vec_spec(d), _vec_spec(d)],
        out_specs=[_row_spec(tm, d), _row_spec(tm, d)],
        out_shape=[jax.ShapeDtypeStruct((t, d), F32), jax.ShapeDtypeStruct((t, d), BF16)],
        compiler_params=_params("parallel"),
    )(x, m, g_post.reshape(1, d), g_next.reshape(1, d))


def _res_norm_bwd(x2, m, g_post, g_next, dx2_in, du2):
    t, d = x2.shape
    tm = _tile(t, 512)

    def body(x2_ref, m_ref, gp_ref, gn_ref, dx2in_ref, du2_ref, dx2_ref, dm_ref, dgp_ref, dgn_ref):
        dxn, dgn_c = _rms_bwd_rows(x2_ref[...], gn_ref[...], du2_ref[...].astype(F32))
        dx2 = dx2in_ref[...] + dxn
        dx2_ref[...] = dx2
        dm, dgp_c = _rms_bwd_rows(m_ref[...].astype(F32), gp_ref[...], dx2)
        dm_ref[...] = dm.astype(BF16)

        @pl.when(pl.program_id(0) == 0)
        def _():
            dgp_ref[...] = jnp.zeros_like(dgp_ref)
            dgn_ref[...] = jnp.zeros_like(dgn_ref)

        dgp_ref[...] += _fold8(dgp_c)
        dgn_ref[...] += _fold8(dgn_c)

    return pl.pallas_call(
        body, name="res_norm_bwd", grid=(t // tm,),
        in_specs=[_row_spec(tm, d), _row_spec(tm, d), _vec_spec(d), _vec_spec(d), _row_spec(tm, d), _row_spec(tm, d)],
        out_specs=[_row_spec(tm, d), _row_spec(tm, d), _acc_spec(d), _acc_spec(d)],
        out_shape=[jax.ShapeDtypeStruct((t, d), F32), jax.ShapeDtypeStruct((t, d), BF16),
                   jax.ShapeDtypeStruct((SUBLANES, d), F32), jax.ShapeDtypeStruct((SUBLANES, d), F32)],
        compiler_params=_params("arbitrary"),
    )(x2, m, g_post.reshape(1, d), g_next.reshape(1, d), dx2_in, du2)


def _rope_tables(t):
    rows = t // GRID_W
    row = jnp.repeat(jnp.arange(rows, dtype=F32), GRID_W)
    col = jnp.tile(jnp.arange(GRID_W, dtype=F32), rows)

    def tab(rot_dim):
        half = rot_dim // 2
        inv = ROPE_THETA ** (-jnp.arange(0, half, 2, dtype=F32) / half)
        ar = row[:, None] * inv[None, :]
        ac = col[:, None] * inv[None, :]
        ang = jnp.concatenate([ar, ar, ac, ac], axis=-1)
        q = half // 2
        sign = np.tile(np.concatenate([-np.ones(q, np.float32), np.ones(q, np.float32)]), 2)
        return jnp.cos(ang), jnp.sin(ang) * sign[None, :]

    ca, sa = tab(HEAD_DIM)
    cb, sb = tab(MLA_ROPE_DIM)
    one = jnp.ones((t, 1), F32)
    cos_b = jnp.concatenate([one * jnp.ones((1, KR_LANE0), F32), cb, one * jnp.ones((1, 32), F32)], axis=-1)
    sin_b = jnp.concatenate([jnp.zeros((t, KR_LANE0), F32), sb, jnp.zeros((t, 32), F32)], axis=-1)
    return jnp.tile(ca, (1, GQA_HEADS)), jnp.tile(sa, (1, GQA_HEADS)), cos_b, sin_b


def _swap_halves(x, sh):
    lane = lax.broadcasted_iota(jnp.int32, x.shape, 1)
    up = pltpu.roll(x, LANES - sh, 1)
    dn = pltpu.roll(x, sh, 1)
    return jnp.where((lane & (2 * sh - 1)) < sh, up, dn)


def _rope(x, cos, sin_s, sh):
    return x * cos + _swap_halves(x, sh) * sin_s


def _rope_bwd(dy, cos, sin_s, sh):
    return dy * cos + _swap_halves(dy * sin_s, sh)


def _lo_mask(shape):
    return lax.broadcasted_iota(jnp.int32, shape, 1) < HEAD_DIM


def _half_mean(t, lo):
    s_lo = jnp.sum(jnp.where(lo, t, 0.0), axis=-1, keepdims=True)
    s_hi = jnp.sum(jnp.where(lo, 0.0, t), axis=-1, keepdims=True)
    return jnp.where(lo, s_lo, s_hi) * (1.0 / HEAD_DIM)


def _head_norm(x, g2):
    lo = _lo_mask(x.shape)
    r = lax.rsqrt(_half_mean(x * x, lo) + EPS)
    return x * r * g2


def _head_norm_bwd(x, g2, dy):
    lo = _lo_mask(x.shape)
    r = lax.rsqrt(_half_mean(x * x, lo) + EPS)
    xh = x * r
    dxh = dy * g2
    dx = r * (dxh - xh * _half_mean(dxh * xh, lo))
    return dx, dy * xh


def _prep_a_fwd(z, gq2, gk2, gqa, gkva, cos_a, sin_a, cos_b, sin_b):
    t = z.shape[0]
    tm = _tile(t, PREP_ROWS)

    def body(z_ref, gq_ref, gk_ref, gqa_ref, gkva_ref, ca_ref, sa_ref, cb_ref, sb_ref,
             qa_ref, ka_ref, va_ref, cqn_ref, ckvn_ref, krr_ref):
        def zf(lo, hi):
            return z_ref[:, lo:hi].astype(F32)

        for j in range(4):
            cols = slice(LANES * j, LANES * (j + 1))
            y = _rope(_head_norm(zf(LANES * j, LANES * (j + 1)), gq_ref[...]), ca_ref[:, cols], sa_ref[:, cols], 16)
            qa_ref[:, cols] = (y * (GQA_SCALE * LOG2E)).astype(BF16)
        y = _rope(_head_norm(zf(Z_KA, Z_VA), gk_ref[...]), ca_ref[:, :LANES], sa_ref[:, :LANES], 16)
        ka_ref[...] = y.astype(BF16)
        va_ref[...] = z_ref[:, Z_VA:Z_CQ].astype(BF16)
        cq = zf(Z_CQ, Z_CKV)
        cqn_ref[...] = (cq * _rinv(cq) * gqa_ref[...]).astype(BF16)
        ckv = zf(Z_CKV, Z_KR)
        ckvn_ref[...] = (ckv * _rinv(ckv) * gkva_ref[...]).astype(BF16)
        krr_ref[...] = _rope(zf(Z_KR, Z_GATE), cb_ref[...], sb_ref[...], 8)

    return pl.pallas_call(
        body, name="prep_a_fwd", grid=(t // tm,),
        in_specs=[_row_spec(tm, Z_ATT_W), _vec_spec(LANES), _vec_spec(LANES), _vec_spec(MLA_Q_RANK),
                  _vec_spec(MLA_KV_RANK), _row_spec(tm, 512), _row_spec(tm, 512), _row_spec(tm, LANES),
                  _row_spec(tm, LANES)],
        out_specs=[_row_spec(tm, 512), _row_spec(tm, LANES), _row_spec(tm, LANES), _row_spec(tm, MLA_Q_RANK),
                   _row_spec(tm, MLA_KV_RANK), _row_spec(tm, LANES)],
        out_shape=[jax.ShapeDtypeStruct((t, 512), BF16), jax.ShapeDtypeStruct((t, LANES), BF16),
                   jax.ShapeDtypeStruct((t, LANES), BF16), jax.ShapeDtypeStruct((t, MLA_Q_RANK), BF16),
                   jax.ShapeDtypeStruct((t, MLA_KV_RANK), BF16), jax.ShapeDtypeStruct((t, LANES), F32)],
        compiler_params=_params("parallel"),
    )(z, gq2, gk2, gqa, gkva, cos_a, sin_a, cos_b, sin_b)


def _prep_a_bwd(z, dqa, dka4, dva4, dcqn, dckvn, dkr, dzga, dzgb, gq2, gk2, gqa, gkva, cos_a, sin_a):
    t = z.shape[0]
    tm = _tile(t, PREP_ROWS)

    def body(z_ref, dqa_ref, dka_ref, dva_ref, dcqn_ref, dckvn_ref, dkr_ref, dzga_ref, dzgb_ref, gq_ref, gk_ref,
             gqa_ref, gkva_ref, ca_ref, sa_ref, dz_ref, dgq_ref, dgk_ref, dgqa_ref, dgkva_ref):
        @pl.when(pl.program_id(0) == 0)
        def _():
            dgq_ref[...] = jnp.zeros_like(dgq_ref)
            dgk_ref[...] = jnp.zeros_like(dgk_ref)
            dgqa_ref[...] = jnp.zeros_like(dgqa_ref)
            dgkva_ref[...] = jnp.zeros_like(dgkva_ref)

        def zf(lo, hi):
            return z_ref[:, lo:hi].astype(F32)

        dgq = jnp.zeros((SUBLANES, LANES), F32)
        for j in range(4):
            cols = slice(LANES * j, LANES * (j + 1))
            dy = _rope_bwd(dqa_ref[:, cols] * GQA_SCALE, ca_ref[:, cols], sa_ref[:, cols], 16)
            dx, dgc = _head_norm_bwd(zf(LANES * j, LANES * (j + 1)), gq_ref[...], dy)
            dz_ref[:, cols] = dx.astype(BF16)
            dgq = dgq + _fold8(dgc)
        dgq_ref[...] += dgq
        dk = (dka_ref[0] + dka_ref[1] + dka_ref[2] + dka_ref[3]).T * LN2
        dy = _rope_bwd(dk, ca_ref[:, :LANES], sa_ref[:, :LANES], 16)
        dx, dgc = _head_norm_bwd(zf(Z_KA, Z_VA), gk_ref[...], dy)
        dz_ref[:, Z_KA:Z_VA] = dx.astype(BF16)
        dgk_ref[...] += _fold8(dgc)
        dz_ref[:, Z_VA:Z_CQ] = (dva_ref[0] + dva_ref[1] + dva_ref[2] + dva_ref[3]).T.astype(BF16)
        dx, dgc = _rms_bwd_rows(zf(Z_CQ, Z_CKV), gqa_ref[...], dcqn_ref[...].astype(F32))
        dz_ref[:, Z_CQ:Z_CKV] = dx.astype(BF16)
        dgqa_ref[...] += _fold8(dgc)
        dx, dgc = _rms_bwd_rows(zf(Z_CKV, Z_KR), gkva_ref[...], dckvn_ref[...].astype(F32))
        dz_ref[:, Z_CKV:Z_KR] = dx.astype(BF16)
        dgkva_ref[...] += _fold8(dgc)
        dz_ref[:, Z_KR:Z_GATE] = dkr_ref[...].astype(BF16)
        dz_ref[:, Z_GATE:Z_GATE + D_MODEL] = dzga_ref[...]
        dz_ref[:, Z_GATE + D_MODEL:Z_W] = dzgb_ref[...]

    part = pl.BlockSpec((4, LANES, tm), lambda i: (0, 0, i))
    return pl.pallas_call(
        body, name="prep_a_bwd", grid=(t // tm,),
        in_specs=[_row_spec(tm, Z_ATT_W), _row_spec(tm, 512), part, part, _row_spec(tm, MLA_Q_RANK),
                  _row_spec(tm, MLA_KV_RANK), _row_spec(tm, LANES), _row_spec(tm, D_MODEL), _row_spec(tm, D_MODEL),
                  _vec_spec(LANES),
                  _vec_spec(LANES), _vec_spec(MLA_Q_RANK), _vec_spec(MLA_KV_RANK), _row_spec(tm, 512),
                  _row_spec(tm, 512)],
        out_specs=[_row_spec(tm, Z_W), _acc_spec(LANES), _acc_spec(LANES), _acc_spec(MLA_Q_RANK),
                   _acc_spec(MLA_KV_RANK)],
        out_shape=[jax.ShapeDtypeStruct((t, Z_W), BF16), jax.ShapeDtypeStruct((SUBLANES, LANES), F32),
                   jax.ShapeDtypeStruct((SUBLANES, LANES), F32), jax.ShapeDtypeStruct((SUBLANES, MLA_Q_RANK), F32),
                   jax.ShapeDtypeStruct((SUBLANES, MLA_KV_RANK), F32)],
        compiler_params=_params("arbitrary"),
    )(z, dqa, dka4, dva4, dcqn, dckvn, dkr, dzga, dzgb, gq2, gk2, gqa, gkva, cos_a, sin_a)


def _prep_b_fwd(qb, kvb, krr, cos_b, sin_b):
    t = qb.shape[0]
    tm = _tile(t, PREP_ROWS)

    def body(qb_ref, kvb_ref, krr_ref, cb_ref, sb_ref, q_ref, k_ref, v_ref):
        for h in range(MLA_HEADS):
            cols = slice(LANES * h, LANES * (h + 1))
            qh = _rope(qb_ref[:, cols].astype(F32), cb_ref[...], sb_ref[...], 8)
            q_ref[:, cols] = (qh * (MLA_SCALE * LOG2E)).astype(BF16)
            k_ref[:, cols] = (kvb_ref[:, cols].astype(F32) + krr_ref[...]).astype(BF16)
        v_ref[...] = kvb_ref[:, 1024:1536].astype(BF16)

    return pl.pallas_call(
        body, name="prep_b_fwd", grid=(t // tm,),
        in_specs=[_row_spec(tm, 1024), _row_spec(tm, 1536), _row_spec(tm, LANES), _row_spec(tm, LANES),
                  _row_spec(tm, LANES)],
        out_specs=[_row_spec(tm, 1024), _row_spec(tm, 1024), _row_spec(tm, 512)],
        out_shape=[jax.ShapeDtypeStruct((t, 1024), BF16), jax.ShapeDtypeStruct((t, 1024), BF16),
                   jax.ShapeDtypeStruct((t, 512), BF16)],
        compiler_params=_params("parallel"),
    )(qb, kvb, krr, cos_b, sin_b)


def _prep_b_bwd(dq, dk, dv, cos_b, sin_b):
    t = dq.shape[0]
    tm = _tile(t, PREP_ROWS)

    def body(dq_ref, dk_ref, dv_ref, cb_ref, sb_ref, dqb_ref, dkvb_ref, dkr_ref):
        dkr = jnp.zeros((tm, LANES), F32)
        for h in range(MLA_HEADS):
            cols = slice(LANES * h, LANES * (h + 1))
            dqb_ref[:, cols] = _rope_bwd(dq_ref[:, cols] * MLA_SCALE, cb_ref[...], sb_ref[...], 8).astype(BF16)
            dkh = dk_ref[cols, :].T * LN2
            dkvb_ref[:, cols] = dkh.astype(BF16)
            dkr = dkr + dkh
        for j in range(MLA_HEADS // 2):
            dkvb_ref[:, 1024 + LANES * j:1024 + LANES * (j + 1)] = dv_ref[LANES * j:LANES * (j + 1), :].T.astype(BF16)
        dkr_ref[...] = _rope_bwd(dkr, cb_ref[...], sb_ref[...], 8)

    return pl.pallas_call(
        body, name="prep_b_bwd", grid=(t // tm,),
        in_specs=[_row_spec(tm, 1024), pl.BlockSpec((1024, tm), lambda i: (0, i)),
                  pl.BlockSpec((512, tm), lambda i: (0, i)), _row_spec(tm, LANES),
                  _row_spec(tm, LANES)],
        out_specs=[_row_spec(tm, 1024), _row_spec(tm, 1536), _row_spec(tm, LANES)],
        out_shape=[jax.ShapeDtypeStruct((t, 1024), BF16), jax.ShapeDtypeStruct((t, 1536), BF16),
                   jax.ShapeDtypeStruct((t, LANES), F32)],
        compiler_params=_params("parallel"),
    )(dq, dk, dv, cos_b, sin_b)


_NT = (((1,), (1,)), ((), ()))
_NN = (((1,), (0,)), ((), ()))
_TN = (((0,), (0,)), ((), ()))


def _head_operands(qv, kv, i, shared_k):
    if shared_k:
        lo = _lo_mask(qv.shape)
        keep = lo if i == 0 else jnp.logical_not(lo)
        return jnp.where(keep, qv, jnp.zeros_like(qv)), kv
    cols = slice(LANES * i, LANES * (i + 1))
    return qv[:, cols], kv[:, cols]


def _attn_specs(shared_k, tq, tk, q_of, k_of):
    wq = LANES if shared_k else 2 * LANES
    q_spec = pl.BlockSpec((tq, wq), lambda *g: (q_of(*g), g[0]))
    if shared_k:
        k_spec = pl.BlockSpec((tk, LANES), lambda *g: (k_of(*g), 0))
        v_spec = pl.BlockSpec((tk, LANES), lambda *g: (k_of(*g), 0))
    else:
        k_spec = pl.BlockSpec((tk, wq), lambda *g: (k_of(*g), g[0]))
        v_spec = pl.BlockSpec((tk, LANES), lambda *g: (k_of(*g), g[0]))
    return wq, q_spec, k_spec, v_spec


def _attn_fwd(q, k, v, shared_k, name):
    t = q.shape[0]
    tq, tk = _tile(t, ATTN_TQ), _tile(t, ATTN_TK)
    nq, nk = t // tq, t // tk
    wq, q_spec, k_spec, v_spec = _attn_specs(shared_k, tq, tk, lambda p, i, j: i, lambda p, i, j: j)
    groups = q.shape[1] // wq
    chunk = _tile(tq, 2 * LANES)

    def body(q_ref, k_ref, v_ref, o_ref, lse_ref, m_s, acc_s, alpha_s, s_s, p_s):
        kb = pl.program_id(2)

        @pl.when(kb == 0)
        def _():
            m_s[...] = jnp.full_like(m_s, -jnp.inf)
            acc_s[...] = jnp.zeros_like(acc_s)

        qv, kv, vv = q_ref[...], k_ref[...], v_ref[...]
        lo = _lo_mask(vv.shape)
        for i in range(2):
            qi, ki = _head_operands(qv, kv, i, shared_k)
            s_s[i] = lax.dot_general(ki, qi, _NT, preferred_element_type=F32)
        for i in range(2):
            for c in range(tq // chunk):
                cols = slice(c * chunk, (c + 1) * chunk)
                m_prev = m_s[i, :, cols]
                m_new = jnp.maximum(m_prev, jnp.max(s_s[i, :, cols], axis=0, keepdims=True))
                alpha_s[i, :, cols] = jnp.exp2(m_prev - m_new)
                m_s[i, :, cols] = m_new
                p_s[i, :, cols] = jnp.exp2(s_s[i, :, cols] - m_new).astype(BF16)
        for i in range(2):
            keep = lo if i == 0 else jnp.logical_not(lo)
            vi = jnp.where(keep, vv, jnp.ones_like(vv))
            acc_s[i] = alpha_s[i] * acc_s[i] + lax.dot_general(vi, p_s[i], _TN, preferred_element_type=F32)

        @pl.when(kb == nk - 1)
        def _():
            a0, a1 = acc_s[0], acc_s[1]
            l0 = a0[LANES - SUBLANES:, :][0:1, :]
            l1 = a1[0:SUBLANES, :][0:1, :]
            row_lo = lax.broadcasted_iota(jnp.int32, a0.shape, 0) < HEAD_DIM
            o_ref[...] = jnp.where(row_lo, a0 / l0, a1 / l1).T.astype(BF16)
            lse_ref[0] = jnp.broadcast_to(m_s[0] + jnp.log2(l0), (LANES, tq)).T
            lse_ref[1] = jnp.broadcast_to(m_s[1] + jnp.log2(l1), (LANES, tq)).T

    return pl.pallas_call(
        body, name=name, grid=(groups, nq, nk),
        in_specs=[q_spec, k_spec, v_spec],
        out_specs=[pl.BlockSpec((tq, LANES), lambda p, i, j: (i, p)),
                   pl.BlockSpec((2, tq, LANES), lambda p, i, j: (p, i, 0))],
        out_shape=[jax.ShapeDtypeStruct((t, LANES * groups), BF16),
                   jax.ShapeDtypeStruct((2 * groups, t, LANES), F32)],
        scratch_shapes=[pltpu.VMEM((2, 1, tq), F32), pltpu.VMEM((2, LANES, tq), F32), pltpu.VMEM((2, 1, tq), F32),
                        pltpu.VMEM((2, tk, tq), F32), pltpu.VMEM((2, tk, tq), BF16)],
        compiler_params=_params("parallel", "parallel", "arbitrary"),
    )(q, k, v)


def _attn_delta(do, o):
    t, w = do.shape
    tm = _tile(t, 512)
    groups = w // LANES

    def body(do_ref, o_ref, delta_ref):
        prod = do_ref[...].astype(F32) * o_ref[...].astype(F32)
        for g in range(groups):
            x = prod[:, LANES * g:LANES * (g + 1)]
            lo = _lo_mask(x.shape)
            d0 = jnp.sum(jnp.where(lo, x, 0.0), axis=-1, keepdims=True)
            d1 = jnp.sum(jnp.where(lo, 0.0, x), axis=-1, keepdims=True)
            delta_ref[2 * g] = jnp.broadcast_to(d0, (tm, LANES))
            delta_ref[2 * g + 1] = jnp.broadcast_to(d1, (tm, LANES))

    return pl.pallas_call(
        body, name="attn_delta", grid=(t // tm,),
        in_specs=[_row_spec(tm, w), _row_spec(tm, w)],
        out_specs=pl.BlockSpec((2 * groups, tm, LANES), lambda i: (0, i, 0)),
        out_shape=jax.ShapeDtypeStruct((2 * groups, t, LANES), F32),
        compiler_params=_params("parallel"),
    )(do, o)


def _attn_bwd(q, k, v, do, lse, delta, shared_k, name):
    t = q.shape[0]
    tq, tk = _tile(t, ATTN_TQ), _tile(t, ATTN_TK)
    nq, nk = t // tq, t // tk
    wq, q_spec, k_spec, v_spec = _attn_specs(shared_k, tq, tk, lambda p, j, i: i, lambda p, j, i: j)
    groups = q.shape[1] // wq

    def body(q_ref, k_ref, v_ref, do_ref, lse_ref, delta_ref, dq_ref, dk_ref, dv_ref, dk_s, dv_s, s_s, dp_s, p_s,
             ds_s):
        kb, qb = pl.program_id(1), pl.program_id(2)

        @pl.when(qb == 0)
        def _():
            dk_s[...] = jnp.zeros_like(dk_s)
            dv_s[...] = jnp.zeros_like(dv_s)

        qv, kv, vv, dov = q_ref[...], k_ref[...], v_ref[...], do_ref[...]
        lo = _lo_mask(dov.shape)
        heads = []
        for i in range(2):
            qi, ki = _head_operands(qv, kv, i, shared_k)
            keep = lo if i == 0 else jnp.logical_not(lo)
            doi = jnp.where(keep, dov, jnp.zeros_like(dov))
            heads.append((qi, ki, doi))
            s_s[i] = lax.dot_general(qi, ki, _NT, preferred_element_type=F32)
            dp_s[i] = lax.dot_general(doi, vv, _NT, preferred_element_type=F32)
        for i in range(2):
            lse_i, delta_i = lse_ref[i], delta_ref[i]
            for c in range(tk // LANES):
                cols = slice(c * LANES, (c + 1) * LANES)
                p = jnp.exp2(s_s[i, :, cols] - lse_i)
                p_s[i, :, cols] = p.astype(BF16)
                ds_s[i, :, cols] = (p * (dp_s[i, :, cols] - delta_i)).astype(BF16)
        dq_parts = []
        for i in range(2):
            qi, ki, doi = heads[i]
            dv_s[...] += lax.dot_general(doi, p_s[i], _TN, preferred_element_type=F32)
            dk_i = lax.dot_general(qi, ds_s[i], _TN, preferred_element_type=F32)
            if shared_k:
                dk_s[...] += dk_i
            else:
                dk_s[LANES * i:LANES * (i + 1), :] += dk_i
            dq_parts.append(lax.dot_general(ds_s[i], ki, _NN, preferred_element_type=F32))
        rows = pl.ds(pl.multiple_of(qb * tq, tq), tq)
        if shared_k:
            tiles = [(slice(0, LANES), jnp.where(lo, dq_parts[0], dq_parts[1]))]
        else:
            tiles = [(slice(0, LANES), dq_parts[0]), (slice(LANES, 2 * LANES), dq_parts[1])]
        for cols, val in tiles:
            @pl.when(kb == 0)
            def _(cols=cols, val=val):
                dq_ref[rows, cols] = val

            @pl.when(kb > 0)
            def _(cols=cols, val=val):
                dq_ref[rows, cols] += val

        @pl.when(qb == nq - 1)
        def _():
            if shared_k:
                dk_ref[0] = dk_s[...]
                dv_ref[0] = dv_s[...]
            else:
                dk_ref[...] = dk_s[...]
                dv_ref[...] = dv_s[...]

    stat_spec = pl.BlockSpec((2, tq, LANES), lambda p, j, i: (p, i, 0))
    do_spec = pl.BlockSpec((tq, LANES), lambda p, j, i: (i, p))
    dq_spec = pl.BlockSpec((t, wq), lambda p, j, i: (0, p))
    if shared_k:
        dk_spec = pl.BlockSpec((1, LANES, tk), lambda p, j, i: (p, 0, j))
        dv_spec = dk_spec
        dk_shape = jax.ShapeDtypeStruct((groups, LANES, t), F32)
        dv_shape = dk_shape
    else:
        dk_spec = pl.BlockSpec((wq, tk), lambda p, j, i: (p, j))
        dv_spec = pl.BlockSpec((LANES, tk), lambda p, j, i: (p, j))
        dk_shape = jax.ShapeDtypeStruct((wq * groups, t), F32)
        dv_shape = jax.ShapeDtypeStruct((LANES * groups, t), F32)
    return pl.pallas_call(
        body, name=name, grid=(groups, nk, nq),
        in_specs=[q_spec, k_spec, v_spec, do_spec, stat_spec, stat_spec],
        out_specs=[dq_spec, dk_spec, dv_spec],
        out_shape=[jax.ShapeDtypeStruct((t, wq * groups), F32), dk_shape, dv_shape],
        scratch_shapes=[pltpu.VMEM((wq, tk), F32), pltpu.VMEM((LANES, tk), F32), pltpu.VMEM((2, tq, tk), F32),
                        pltpu.VMEM((2, tq, tk), F32), pltpu.VMEM((2, tq, tk), BF16), pltpu.VMEM((2, tq, tk), BF16)],
        compiler_params=_params("parallel", "arbitrary", "arbitrary"),
    )(q, k, v, do, lse, delta)


_MERGE_W = 512
_GATE_BLK0 = Z_GATE // _MERGE_W


def _merge_fwd(z, b_gate, ta, tb):
    t = z.shape[0]
    tm = _tile(t, 512)
    w = _MERGE_W
    nj = D_MODEL // w

    def body(za_ref, zb_ref, ba_ref, bb_ref, ta_ref, tb_ref, o_ref):
        ga = jax.nn.sigmoid(za_ref[...].astype(F32) + ba_ref[...])
        gb = jax.nn.sigmoid(zb_ref[...].astype(F32) + bb_ref[...])
        o_ref[...] = (ga * ta_ref[...].astype(F32) + gb * tb_ref[...].astype(F32)).astype(BF16)

    return pl.pallas_call(
        body, name="merge_fwd", grid=(t // tm, nj),
        in_specs=[pl.BlockSpec((tm, w), lambda i, j: (i, _GATE_BLK0 + j)),
                  pl.BlockSpec((tm, w), lambda i, j: (i, _GATE_BLK0 + nj + j)),
                  pl.BlockSpec((1, w), lambda i, j: (0, j)),
                  pl.BlockSpec((1, w), lambda i, j: (0, nj + j)),
                  pl.BlockSpec((tm, w), lambda i, j: (i, j)),
                  pl.BlockSpec((tm, w), lambda i, j: (i, j))],
        out_specs=pl.BlockSpec((tm, w), lambda i, j: (i, j)),
        out_shape=jax.ShapeDtypeStruct((t, D_MODEL), BF16),
        compiler_params=_params("parallel", "parallel"),
    )(z, z, b_gate, b_gate, ta, tb)


def _merge_bwd(dmg, z, b_gate, ta, tb):
    t = z.shape[0]
    tm = _tile(t, 512)
    w = _MERGE_W
    nj = D_MODEL // w

    def body(dm_ref, za_ref, zb_ref, ba_ref, bb_ref, ta_ref, tb_ref, dta_ref, dtb_ref, dza_ref, dzb_ref,
             dba_ref, dbb_ref):
        dm = dm_ref[...].astype(F32)
        ga = jax.nn.sigmoid(za_ref[...].astype(F32) + ba_ref[...])
        gb = jax.nn.sigmoid(zb_ref[...].astype(F32) + bb_ref[...])
        dta_ref[...] = (dm * ga).astype(BF16)
        dtb_ref[...] = (dm * gb).astype(BF16)
        dza = dm * ta_ref[...].astype(F32) * ga * (1.0 - ga)
        dzb = dm * tb_ref[...].astype(F32) * gb * (1.0 - gb)
        dza_ref[...] = dza.astype(BF16)
        dzb_ref[...] = dzb.astype(BF16)

        @pl.when(pl.program_id(1) == 0)
        def _():
            dba_ref[...] = jnp.zeros_like(dba_ref)
            dbb_ref[...] = jnp.zeros_like(dbb_ref)

        dba_ref[...] += _fold8(dza)
        dbb_ref[...] += _fold8(dzb)

    blk = pl.BlockSpec((tm, w), lambda j, i: (i, j))
    acc = pl.BlockSpec((SUBLANES, w), lambda j, i: (0, j))
    return pl.pallas_call(
        body, name="merge_bwd", grid=(nj, t // tm),
        in_specs=[blk,
                  pl.BlockSpec((tm, w), lambda j, i: (i, _GATE_BLK0 + j)),
                  pl.BlockSpec((tm, w), lambda j, i: (i, _GATE_BLK0 + nj + j)),
                  pl.BlockSpec((1, w), lambda j, i: (0, j)),
                  pl.BlockSpec((1, w), lambda j, i: (0, nj + j)),
                  blk, blk],
        out_specs=[blk, blk, blk, blk, acc, acc],
        out_shape=[jax.ShapeDtypeStruct((t, D_MODEL), BF16)] * 4 + [jax.ShapeDtypeStruct((SUBLANES, D_MODEL), F32)] * 2,
        compiler_params=_params("parallel", "arbitrary"),
    )(dmg, z, z, b_gate, b_gate, ta, tb)


def _loss_grad(y, target):
    t, d = y.shape
    tm = _tile(t, 512)

    def body(y_ref, t_ref, dy_ref, acc_ref):
        err = y_ref[...] - t_ref[...]
        dy_ref[...] = err * (1.0 / d)
        e8 = _fold8(err * err)
        part = e8[:, 0:LANES]
        for c in range(1, d // LANES):
            part = part + e8[:, LANES * c:LANES * (c + 1)]

        @pl.when(pl.program_id(0) == 0)
        def _():
            acc_ref[...] = jnp.zeros_like(acc_ref)

        acc_ref[...] += part

    return pl.pallas_call(
        body, name="loss_grad", grid=(t // tm,),
        in_specs=[_row_spec(tm, d), _row_spec(tm, d)],
        out_specs=[_row_spec(tm, d), _acc_spec(LANES)],
        out_shape=[jax.ShapeDtypeStruct((t, d), F32), jax.ShapeDtypeStruct((SUBLANES, LANES), F32)],
        compiler_params=_params("arbitrary"),
    )(y, target)


_MESH_ID = pl.DeviceIdType.MESH
_ANY = pl.BlockSpec(memory_space=pl.ANY)


def _all_gather(arrays):
    n = len(arrays)
    halves = []
    for a in arrays:
        assert a.shape[0] % 2 == 0, a.shape
        halves.append((pl.ds(0, a.shape[0] // 2), pl.ds(a.shape[0] // 2, a.shape[0] // 2)))
    OWN_SIB, OWN_X, OWN_Y, FWD_X, FWD_Y, SIB_X, SIB_Y, SIB_DA, SIB_DB = range(9)

    def body(*refs):
        x_refs, out_refs = refs[:n], refs[n:2 * n]
        send_sems, recv_sems, local_sems = refs[2 * n:]
        mx, my, mc = lax.axis_index("x"), lax.axis_index("y"), lax.axis_index("c")
        me, sibling = (mx, my, mc), (mx, my, 1 - mc)
        x_nbr, y_nbr, diag = (1 - mx, my, mc), (mx, 1 - my, mc), (1 - mx, 1 - my, mc)

        def slot(a, dev, rows=None):
            px, py, pc = dev
            ref = out_refs[a].at[4 * px + 2 * py + pc]
            return ref if rows is None else ref.at[rows]

        def other_core(dev):
            return (dev[0], dev[1], 1 - dev[2])

        def copy(a, sem, block, to, rows=None, src=None):
            return pltpu.make_async_remote_copy(
                src_ref=slot(a, block, rows) if src is None else src, dst_ref=slot(a, block, rows),
                send_sem=send_sems.at[a, sem], recv_sem=recv_sems.at[a, sem], device_id=to, device_id_type=_MESH_ID)

        mine = [pltpu.make_async_copy(x_refs[a], slot(a, me), local_sems.at[a]) for a in range(n)]
        sent = []
        for a in range(n):
            mine[a].start()
            sent += [copy(a, OWN_SIB, me, sibling, src=x_refs[a]), copy(a, OWN_X, me, x_nbr, src=x_refs[a]),
                     copy(a, OWN_Y, me, y_nbr, src=x_refs[a])]
        for cp in sent:
            cp.start()
        for a in range(n):
            first, second = halves[a]
            copy(a, OWN_Y, y_nbr, me).wait_recv()
            sent += [copy(a, FWD_X, y_nbr, x_nbr, rows=first), copy(a, SIB_Y, y_nbr, sibling)]
            sent[-2].start()
            sent[-1].start()
            copy(a, OWN_X, x_nbr, me).wait_recv()
            sent += [copy(a, FWD_Y, x_nbr, y_nbr, rows=second), copy(a, SIB_X, x_nbr, sibling)]
            sent[-2].start()
            sent[-1].start()
        for a in range(n):
            first, second = halves[a]
            copy(a, FWD_X, diag, me, rows=first).wait_recv()
            sent.append(copy(a, SIB_DA, diag, sibling, rows=first))
            sent[-1].start()
            copy(a, FWD_Y, diag, me, rows=second).wait_recv()
            sent.append(copy(a, SIB_DB, diag, sibling, rows=second))
            sent[-1].start()
        for a in range(n):
            first, second = halves[a]
            copy(a, OWN_SIB, sibling, me).wait_recv()
            copy(a, SIB_X, other_core(x_nbr), me).wait_recv()
            copy(a, SIB_Y, other_core(y_nbr), me).wait_recv()
            copy(a, SIB_DA, other_core(diag), me, rows=first).wait_recv()
            copy(a, SIB_DB, other_core(diag), me, rows=second).wait_recv()
        for cp in sent:
            cp.wait_send()
        for cp in mine:
            cp.wait()

    return pl.pallas_call(
        body, name="weight_all_gather",
        out_shape=[jax.ShapeDtypeStruct((N_DEV,) + a.shape, a.dtype) for a in arrays],
        in_specs=[_ANY] * n, out_specs=[_ANY] * n,
        scratch_shapes=[pltpu.SemaphoreType.DMA((n, 9)), pltpu.SemaphoreType.DMA((n, 9)),
                        pltpu.SemaphoreType.DMA((n,))],
    )(*arrays)


def _pair_exchange(sends):
    n = len(sends)

    def body(*refs):
        s_refs, r_refs = refs[:n], refs[n:2 * n]
        send_sems, recv_sems = refs[2 * n:]
        mx, my, mc = lax.axis_index("x"), lax.axis_index("y"), lax.axis_index("c")
        copies = []
        for a in range(n):
            for ch in range(4):
                cp = pltpu.make_async_remote_copy(
                    src_ref=s_refs[a].at[2 * ch + (1 - mc)], dst_ref=r_refs[a].at[ch], send_sem=send_sems.at[a, ch],
                    recv_sem=recv_sems.at[a, ch], device_id=(mx, my, 1 - mc), device_id_type=_MESH_ID)
                cp.start()
                copies.append(cp)
        for cp in copies:
            cp.wait_send()
            cp.wait_recv()

    return pl.pallas_call(
        body, name="grad_pair_exchange",
        out_shape=[jax.ShapeDtypeStruct((4,) + s.shape[1:], s.dtype) for s in sends],
        in_specs=[_ANY] * n, out_specs=[_ANY] * n,
        scratch_shapes=[pltpu.SemaphoreType.DMA((n, 4)), pltpu.SemaphoreType.DMA((n, 4))],
    )(*sends)


def _pair_add(send, half, core):
    _, r, c_ = send.shape
    tr = _row_tile(r, c_)

    def body(core_ref, s_ref, h_ref, o_ref):
        del core_ref
        o_ref[...] = (s_ref[...] + h_ref[...]).astype(BF16)

    blk = pl.BlockSpec((1, tr, c_), lambda ch, i, core_ref: (ch, i, 0))
    return pl.pallas_call(
        body, name="grad_pair_add",
        grid_spec=pltpu.PrefetchScalarGridSpec(
            num_scalar_prefetch=1, grid=(4, r // tr),
            in_specs=[pl.BlockSpec((1, tr, c_), lambda ch, i, core_ref: (2 * ch + core_ref[0], i, 0)), blk],
            out_specs=blk),
        out_shape=jax.ShapeDtypeStruct((4, r, c_), BF16),
        compiler_params=_params("parallel", "parallel"),
    )(core, send, half)


def _chip_exchange(parts):
    n = len(parts)

    def body(*refs):
        p_refs, r_refs = refs[:n], refs[n:2 * n]
        send_sems, recv_sems, local_sems = refs[2 * n:]
        mx, my, mc = lax.axis_index("x"), lax.axis_index("y"), lax.axis_index("c")
        mine = 2 * mx + my
        local = [pltpu.make_async_copy(p_refs[a].at[mine], r_refs[a].at[mine], local_sems.at[a]) for a in range(n)]
        copies = []
        for a in range(n):
            local[a].start()
            for rel in range(1, 4):
                px = 1 - mx if rel & 2 else mx
                py = 1 - my if rel & 1 else my
                cp = pltpu.make_async_remote_copy(
                    src_ref=p_refs[a].at[2 * px + py], dst_ref=r_refs[a].at[mine], send_sem=send_sems.at[a, rel - 1],
                    recv_sem=recv_sems.at[a, rel - 1], device_id=(px, py, mc), device_id_type=_MESH_ID)
                cp.start()
                copies.append(cp)
        for cp in copies:
            cp.wait_send()
            cp.wait_recv()
        for cp in local:
            cp.wait()

    return pl.pallas_call(
        body, name="grad_chip_exchange",
        out_shape=[jax.ShapeDtypeStruct(p.shape, p.dtype) for p in parts],
        in_specs=[_ANY] * n, out_specs=[_ANY] * n,
        scratch_shapes=[pltpu.SemaphoreType.DMA((n, 3)), pltpu.SemaphoreType.DMA((n, 3)),
                        pltpu.SemaphoreType.DMA((n,))],
    )(*parts)


def _row_tile(r, c_):
    tr = min(r, ADAM_BLOCK_ELEMS // (pl.cdiv(c_, LANES) * LANES))
    while r % tr:
        tr -= SUBLANES
    return tr


def _adamw(recv, w, m, v):
    r, c_ = w.shape
    tr = _row_tile(r, c_)
    n_src = recv.shape[0]

    def body(g_ref, w_ref, m_ref, v_ref, go_ref, d_ref, mo_ref, vo_ref):
        g = g_ref[0].astype(F32)
        for s in range(1, n_src):
            g = g + g_ref[s].astype(F32)
        go_ref[...] = g
        mn = ADAM_B1 * m_ref[...] + (1.0 - ADAM_B1) * g
        vn = ADAM_B2 * v_ref[...] + (1.0 - ADAM_B2) * (g * g)
        mo_ref[...] = mn
        vo_ref[...] = vn
        m_hat = mn / (1.0 - ADAM_B1 ** ADAM_STEP)
        v_hat = vn / (1.0 - ADAM_B2 ** ADAM_STEP)
        d_ref[...] = -ADAM_LR * (m_hat / (jnp.sqrt(v_hat) + ADAM_EPS) + ADAM_WD * w_ref[...])

    spec = pl.BlockSpec((tr, c_), lambda i: (i, 0))
    out = jax.ShapeDtypeStruct((r, c_), F32)
    return pl.pallas_call(
        body, name="grad_sum_adamw", grid=(r // tr,),
        in_specs=[pl.BlockSpec((n_src, tr, c_), lambda i: (0, i, 0)), spec, spec, spec],
        out_specs=[spec, spec, spec, spec], out_shape=[out, out, out, out],
        compiler_params=_params("parallel"),
    )(recv, w, m, v)


def _pad_cols(a, before, after):
    parts = []
    if before:
        parts.append(jnp.zeros(a.shape[:-1] + (before,), a.dtype))
    parts.append(a)
    if after:
        parts.append(jnp.zeros(a.shape[:-1] + (after,), a.dtype))
    return jnp.concatenate(parts, axis=-1)


def _q_head_pairs(a, axis):
    shp = a.shape
    a = a.reshape(shp[:axis] + (GQA_KV_HEADS, GQA_GROUP, HEAD_DIM) + shp[axis + 1:])
    a = jnp.swapaxes(a, axis, axis + 1)
    return a.reshape(shp)


def _q_head_unpairs(a, axis):
    shp = a.shape
    a = a.reshape(shp[:axis] + (GQA_GROUP, GQA_KV_HEADS, HEAD_DIM) + shp[axis + 1:])
    a = jnp.swapaxes(a, axis, axis + 1)
    return a.reshape(shp)


def _layout_weights(w):
    w_in = w["w_in"]
    lead = w_in.shape[:-1]
    w_in_p = jnp.concatenate([
        _q_head_pairs(w_in[..., 0:512], w_in.ndim - 1),
        w_in[..., 512:1408],
        _pad_cols(w_in[..., 1408:1440], KR_LANE0, LANES - KR_LANE0 - MLA_ROPE_DIM),
        w_in[..., 1440:],
    ], axis=-1)
    wq = w["w_q_up"]
    wq_p = _pad_cols(wq.reshape(wq.shape[:-1] + (MLA_HEADS, MLA_QK_DIM)), 0, LANES - MLA_QK_DIM)
    wq_p = wq_p.reshape(wq.shape[:-1] + (MLA_HEADS * LANES,))
    wkv = w["w_kv_up"]
    wkv4 = wkv.reshape(wkv.shape[:-1] + (MLA_HEADS, 2 * HEAD_DIM))
    wk_p = _pad_cols(wkv4[..., :HEAD_DIM], 0, LANES - HEAD_DIM).reshape(wkv.shape[:-1] + (MLA_HEADS * LANES,))
    wv_p = wkv4[..., HEAD_DIM:].reshape(wkv.shape[:-1] + (MLA_HEADS * HEAD_DIM,))
    del lead
    return {
        "w_in": w_in_p, "w_q_up": wq_p, "w_kv_up": jnp.concatenate([wk_p, wv_p], axis=-1),
        "w_branch_a": _q_head_pairs(w["w_branch_a"], w["w_branch_a"].ndim - 2), "w_branch_b": w["w_branch_b"],
        "w_o": w["w_o"], "w_ffn_up": w["w_ffn_up"], "w_ffn_down": w["w_ffn_down"],
    }


def _unlayout_grads(g):
    gi = g["w_in"]
    kr0 = Z_KR + KR_LANE0
    g_in = jnp.concatenate([
        _q_head_unpairs(gi[..., 0:512], gi.ndim - 1), gi[..., 512:1408], gi[..., kr0:kr0 + MLA_ROPE_DIM],
        gi[..., Z_GATE:],
    ], axis=-1)
    gq = g["w_q_up"]
    gq = gq.reshape(gq.shape[:-1] + (MLA_HEADS, LANES))[..., :MLA_QK_DIM]
    gq = gq.reshape(gq.shape[:-2] + (MLA_HEADS * MLA_QK_DIM,))
    gkv = g["w_kv_up"]
    gk = gkv[..., :MLA_HEADS * LANES].reshape(gkv.shape[:-1] + (MLA_HEADS, LANES))[..., :HEAD_DIM]
    gv = gkv[..., MLA_HEADS * LANES:].reshape(gkv.shape[:-1] + (MLA_HEADS, HEAD_DIM))
    gkv = jnp.concatenate([gk, gv], axis=-1).reshape(gkv.shape[:-1] + (MLA_HEADS * 2 * HEAD_DIM,))
    return {
        "w_in": g_in, "w_q_up": gq, "w_kv_up": gkv,
        "w_branch_a": _q_head_unpairs(g["w_branch_a"], g["w_branch_a"].ndim - 2), "w_branch_b": g["w_branch_b"],
        "w_o": g["w_o"], "w_ffn_up": g["w_ffn_up"], "w_ffn_down": g["w_ffn_down"],
    }


def _pack_small(parts):
    flat = jnp.concatenate([p.reshape(-1) for p in parts])
    pad = (-flat.shape[0]) % (SUBLANES * LANES)
    if pad:
        flat = jnp.concatenate([flat, jnp.zeros((pad,), flat.dtype)])
    return flat.reshape(-1, LANES)


def _unpack_small(packed, shapes):
    flat = packed.reshape(-1)
    out, off = [], 0
    for shp in shapes:
        n = int(np.prod(shp))
        out.append(flat[off:off + n].reshape(shp))
        off += n
    return out


def _shards_of(full, axis):
    shp = full.shape
    cut = shp[:axis] + (N_DEV, shp[axis] // N_DEV) + shp[axis + 1:]
    return jnp.moveaxis(full.reshape(cut), axis, 0)


def _from_shards(shards, axis):
    full = list(shards.shape[1:])
    full[axis] *= N_DEV
    return jnp.moveaxis(shards, 0, axis).reshape(full)


def _rows2d(a):
    return a.reshape(-1, a.shape[-1])


def _layer_fwd(x, u, lw, tabs):
    cos_a, sin_a, cos_b, sin_b = tabs
    z = _matmul(u, lw["w_in"], "nn", "mm_in")
    qa, ka, va, cqn, ckvn, krr = _prep_a_fwd(z, lw["gq2"], lw["gk2"], lw["gqa"], lw["gkva"], cos_a, sin_a, cos_b, sin_b)
    qb = _matmul(cqn, lw["w_q_up"], "nn", "mm_q_up")
    kvb = _matmul(ckvn, lw["w_kv_up"], "nn", "mm_kv_up")
    q_b, k_b, v_b = _prep_b_fwd(qb, kvb, krr, cos_b, sin_b)
    ya, lse_a = _attn_fwd(qa, ka, va, True, "gqa_fwd")
    yb, lse_b = _attn_fwd(q_b, k_b, v_b, False, "mla_fwd")
    ta = _matmul(ya, lw["w_branch_a"], "nn", "mm_branch_a")
    tb = _matmul(yb, lw["w_branch_b"], "nn", "mm_branch_b")
    merged = _merge_fwd(z, lw["b_gate"], ta, tb)
    m = _matmul(merged, lw["w_o"], "nn", "mm_o")
    x2, u2 = _res_norm_fwd(x, m, lw["post_mix_g"], lw["pre_ffn_g"])
    h, a = _matmul(u2, lw["w_ffn_up"], "nn", "mm_ffn_up", post="relu2")
    f = _matmul(a, lw["w_ffn_down"], "nn", "mm_ffn_down")
    x3, u_next = _res_norm_fwd(x2, f, lw["post_ffn_g"], lw["next_pre_mix_g"])
    saved = dict(u=u, z=z, qa=qa, ka=ka, va=va, cqn=cqn, ckvn=ckvn, q_b=q_b, k_b=k_b, v_b=v_b, ya=ya, yb=yb,
                 lse_a=lse_a, lse_b=lse_b, ta=ta, tb=tb, merged=merged, m=m, x2=x2, u2=u2, h=h, a=a, f=f, x3=x3)
    return x3, u_next, saved


def _layer_bwd(dx3, du_next, lw, sv, tabs, gbuf, layer):
    cos_a, sin_a, cos_b, sin_b = tabs
    g = {}
    dx3, df, dg4, dg1n = _res_norm_bwd(sv["x3"], sv["f"], lw["post_ffn_g"], lw["next_pre_mix_g"], dx3, du_next)
    g["post_ffn_g"], g["next_pre_mix_g"] = dg4, dg1n
    dh = _matmul(df, lw["w_ffn_down"], "nt", "mm_d_h", post="relu2_bwd", h=sv["h"])
    g["w_ffn_down"] = _matmul(sv["a"], df, "tn", "mm_dw_ffn_down", stack=(gbuf["w_ffn_down"], layer))
    du2 = _matmul(dh, lw["w_ffn_up"], "nt", "mm_d_u2")
    g["w_ffn_up"] = _matmul(sv["u2"], dh, "tn", "mm_dw_ffn_up", stack=(gbuf["w_ffn_up"], layer))
    dx2, dm, dg2, dg3 = _res_norm_bwd(sv["x2"], sv["m"], lw["post_mix_g"], lw["pre_ffn_g"], dx3, du2)
    g["post_mix_g"], g["pre_ffn_g"] = dg2, dg3
    dmg = _matmul(dm, lw["w_o"], "nt", "mm_d_merged")
    g["w_o"] = _matmul(sv["merged"], dm, "tn", "mm_dw_o", stack=(gbuf["w_o"], layer))
    dta, dtb, dzg_a, dzg_b, db_a, db_b = _merge_bwd(dmg, sv["z"], lw["b_gate"], sv["ta"], sv["tb"])
    g["b_gate"] = jnp.concatenate([db_a, db_b], axis=-1)
    dya = _matmul(dta, lw["w_branch_a"], "nt", "mm_d_ya")
    g["w_branch_a"] = _matmul(sv["ya"], dta, "tn", "mm_dw_branch_a", stack=(gbuf["w_branch_a"], layer))
    dyb = _matmul(dtb, lw["w_branch_b"], "nt", "mm_d_yb")
    g["w_branch_b"] = _matmul(sv["yb"], dtb, "tn", "mm_dw_branch_b", stack=(gbuf["w_branch_b"], layer))
    delta_a = _attn_delta(dya, sv["ya"])
    delta_b = _attn_delta(dyb, sv["yb"])
    dqa, dka4, dva4 = _attn_bwd(sv["qa"], sv["ka"], sv["va"], dya, sv["lse_a"], delta_a, True, "gqa_bwd")
    dq_b, dk_b, dv_b = _attn_bwd(sv["q_b"], sv["k_b"], sv["v_b"], dyb, sv["lse_b"], delta_b, False, "mla_bwd")
    dqb, dkvb, dkr = _prep_b_bwd(dq_b, dk_b, dv_b, cos_b, sin_b)
    dcqn = _matmul(dqb, lw["w_q_up"], "nt", "mm_d_cqn")
    g["w_q_up"] = _matmul(sv["cqn"], dqb, "tn", "mm_dw_q_up", stack=(gbuf["w_q_up"], layer))
    dckvn = _matmul(dkvb, lw["w_kv_up"], "nt", "mm_d_ckvn")
    g["w_kv_up"] = _matmul(sv["ckvn"], dkvb, "tn", "mm_dw_kv_up", stack=(gbuf["w_kv_up"], layer))
    dz, dgq, dgk, dgqa, dgkva = _prep_a_bwd(sv["z"], dqa, dka4, dva4, dcqn, dckvn, dkr, dzg_a, dzg_b, lw["gq2"],
                                            lw["gk2"], lw["gqa"], lw["gkva"], cos_a, sin_a)
    g["q_norm_g"], g["k_norm_g"], g["q_a_norm_g"], g["kv_a_norm_g"] = dgq, dgk, dgqa, dgkva
    du = _matmul(dz, lw["w_in"], "nt", "mm_d_u")
    g["w_in"] = _matmul(sv["u"], dz, "tn", "mm_dw_in", stack=(gbuf["w_in"], layer))
    return dx2, du, g


def kernel(x, w_in, b_gate, q_norm_g, k_norm_g, q_a_norm_g, kv_a_norm_g, w_q_up, w_kv_up, w_branch_a, w_branch_b, w_o, w_ffn_up, w_ffn_down, pre_mix_g, post_mix_g, pre_ffn_g, post_ffn_g, loss_target, m_w_in, m_b_gate, m_q_norm_g, m_k_norm_g, m_q_a_norm_g, m_kv_a_norm_g, m_w_q_up, m_w_kv_up, m_w_branch_a, m_w_branch_b, m_w_o, m_w_ffn_up, m_w_ffn_down, m_pre_mix_g, m_post_mix_g, m_pre_ffn_g, m_post_ffn_g, v_w_in, v_b_gate, v_q_norm_g, v_k_norm_g, v_q_a_norm_g, v_kv_a_norm_g, v_w_q_up, v_w_kv_up, v_w_branch_a, v_w_branch_b, v_w_o, v_w_ffn_up, v_w_ffn_down, v_pre_mix_g, v_post_mix_g, v_pre_ffn_g, v_post_ffn_g):
    weights = dict(zip(WEIGHT_NAMES, (w_in, b_gate, q_norm_g, k_norm_g, q_a_norm_g, kv_a_norm_g, w_q_up, w_kv_up,
                                      w_branch_a, w_branch_b, w_o, w_ffn_up, w_ffn_down, pre_mix_g, post_mix_g,
                                      pre_ffn_g, post_ffn_g)))
    mom_m = dict(zip(WEIGHT_NAMES, (m_w_in, m_b_gate, m_q_norm_g, m_k_norm_g, m_q_a_norm_g, m_kv_a_norm_g, m_w_q_up,
                                    m_w_kv_up, m_w_branch_a, m_w_branch_b, m_w_o, m_w_ffn_up, m_w_ffn_down,
                                    m_pre_mix_g, m_post_mix_g, m_pre_ffn_g, m_post_ffn_g)))
    mom_v = dict(zip(WEIGHT_NAMES, (v_w_in, v_b_gate, v_q_norm_g, v_k_norm_g, v_q_a_norm_g, v_kv_a_norm_g, v_w_q_up,
                                    v_w_kv_up, v_w_branch_a, v_w_branch_b, v_w_o, v_w_ffn_up, v_w_ffn_down,
                                    v_pre_mix_g, v_post_mix_g, v_pre_ffn_g, v_post_ffn_g)))
    assert x.shape[0] == 1 and x.shape[2] == D_MODEL, x.shape
    n_layers = w_in.shape[0]
    t = x.shape[1]
    x0 = x.reshape(t, D_MODEL)
    target = loss_target.reshape(t, D_MODEL)
    shard_shapes = {n: weights[n].shape for n in BIG_NAMES}
    small_shapes = [weights[n].shape for n in SMALL_NAMES]

    gathered = _all_gather([weights[n].astype(BF16) for n in BIG_NAMES])
    full = {n: _from_shards(g, SHARD_AXIS[n]) for n, g in zip(BIG_NAMES, gathered)}
    lw_all = _layout_weights(full)
    lw_all["b_gate"] = b_gate.reshape(n_layers, 1, 2 * D_MODEL)
    lw_all["gq2"] = jnp.tile(q_norm_g, (1, 2)).reshape(n_layers, 1, LANES)
    lw_all["gk2"] = jnp.tile(k_norm_g, (1, 2)).reshape(n_layers, 1, LANES)
    lw_all["gqa"] = q_a_norm_g.reshape(n_layers, 1, MLA_Q_RANK)
    lw_all["gkva"] = kv_a_norm_g.reshape(n_layers, 1, MLA_KV_RANK)
    for n in ("post_mix_g", "pre_ffn_g", "post_ffn_g"):
        lw_all[n] = weights[n]
    lw_all["next_pre_mix_g"] = jnp.roll(pre_mix_g, -1, axis=0)

    tabs = _rope_tables(t)
    u0 = _rms_fwd(x0, pre_mix_g[0])

    layer_w = [{n: a[l] for n, a in lw_all.items()} for l in range(n_layers)]
    xc, uc, saved = x0, u0, []
    for l in range(n_layers):
        xc, uc, sv = _layer_fwd(xc, uc, layer_w[l], tabs)
        saved.append(sv)
    dy, loss_acc = _loss_grad(xc, target)
    loss = lax.psum(0.5 * jnp.sum(loss_acc) / D_MODEL, ("x", "y", "c"))

    dx0, du0, layer_g = dy, jnp.zeros((t, D_MODEL), F32), [None] * n_layers
    gbuf = {n: lax.empty(lw_all[n].shape, F32) for n in BIG_NAMES}
    for l in reversed(range(n_layers)):
        dx0, du0, layer_g[l] = _layer_bwd(dx0, du0, layer_w[l], saved[l], tabs, gbuf, l)
        gbuf = {n: layer_g[l][n] for n in BIG_NAMES}
    grads = {n: jnp.stack([g[n] for g in layer_g]) for n in layer_g[0] if n not in BIG_NAMES}
    grads.update(gbuf)
    grad_x, dg1_first = _rms_bwd(x0, pre_mix_g[0], dx0, du0)

    big_grads = _unlayout_grads({n: grads[n] for n in BIG_NAMES})
    fold = lambda a: a.sum(axis=1)
    dgq = fold(grads["q_norm_g"]).reshape(n_layers, 2, HEAD_DIM).sum(axis=1)
    dgk = fold(grads["k_norm_g"]).reshape(n_layers, 2, HEAD_DIM).sum(axis=1)
    dg1 = jnp.concatenate([fold(dg1_first[None]), fold(grads["next_pre_mix_g"])[:-1]], axis=0)
    small_grads = {
        "b_gate": fold(grads["b_gate"]), "q_norm_g": dgq, "k_norm_g": dgk, "q_a_norm_g": fold(grads["q_a_norm_g"]),
        "kv_a_norm_g": fold(grads["kv_a_norm_g"]), "pre_mix_g": dg1, "post_mix_g": fold(grads["post_mix_g"]),
        "pre_ffn_g": fold(grads["pre_ffn_g"]), "post_ffn_g": fold(grads["post_ffn_g"]),
    }
    small_packed = _pack_small([small_grads[n] for n in SMALL_NAMES])
    sends = [_shards_of(big_grads[n], SHARD_AXIS[n]).reshape((N_DEV,) + _rows2d(weights[n]).shape)
             for n in BIG_NAMES]
    sends.append(jnp.broadcast_to(small_packed[None], (N_DEV,) + small_packed.shape))
    halves = _pair_exchange(sends)
    core = lax.axis_index("c").astype(jnp.int32).reshape(1)
    recvs = _chip_exchange([_pair_add(s, h, core) for s, h in zip(sends, halves)])

    results = {}
    for n, recv in zip(BIG_NAMES, recvs):
        res = _adamw(recv, _rows2d(weights[n]), _rows2d(mom_m[n]), _rows2d(mom_v[n]))
        results[n] = [r.reshape(shard_shapes[n]) for r in res]
    res = _adamw(recvs[-1], *[_pack_small([d[n] for n in SMALL_NAMES]) for d in (weights, mom_m, mom_v)])
    for kind, packed_out in enumerate(res):
        for n, val in zip(SMALL_NAMES, _unpack_small(packed_out, small_shapes)):
            results.setdefault(n, [None] * 4)[kind] = val
    outs = [results[n][kind] for kind in range(4) for n in WEIGHT_NAMES]
    return (loss, grad_x.reshape(x.shape), *outs)
```

```python
import math

import jax
import jax.numpy as jnp
import numpy as np
from jax import lax
from jax.experimental import pallas as pl
from jax.experimental.pallas import tpu as pltpu

F32 = jnp.float32
BF16 = jnp.bfloat16

D_MODEL = 1024
GRID_W = 64
ROPE_THETA = 10000.0
EPS = 1e-6
GQA_HEADS = 8
GQA_KV_HEADS = 2
GQA_GROUP = GQA_HEADS // GQA_KV_HEADS
HEAD_DIM = 64
MLA_HEADS = 8
MLA_ROPE_DIM = 32
MLA_QK_DIM = 96
MLA_Q_RANK = 384
MLA_KV_RANK = 256
GQA_SCALE = 1.0 / math.sqrt(HEAD_DIM)
MLA_SCALE = 1.0 / math.sqrt(MLA_QK_DIM)
LOG2E = math.log2(math.e)
LN2 = math.log(2.0)

ADAM_LR = 0.001
ADAM_B1 = 0.9
ADAM_B2 = 0.999
ADAM_EPS = 1e-08
ADAM_WD = 0.01
ADAM_STEP = 10

N_DEV = 8
LANES = 128
SUBLANES = 8
VMEM_LIMIT = 48 * 1024 * 1024

Z_QA, Z_KA, Z_VA, Z_CQ, Z_CKV, Z_KR, Z_GATE = 0, 512, 640, 768, 1152, 1408, 1536
Z_ATT_W = 1536
Z_W = 3584
KR_LANE0 = 64

WEIGHT_NAMES = ("w_in", "b_gate", "q_norm_g", "k_norm_g", "q_a_norm_g", "kv_a_norm_g", "w_q_up", "w_kv_up",
                "w_branch_a", "w_branch_b", "w_o", "w_ffn_up", "w_ffn_down", "pre_mix_g", "post_mix_g",
                "pre_ffn_g", "post_ffn_g")
SHARD_AXIS = {"w_in": 2, "w_q_up": 2, "w_kv_up": 2, "w_branch_a": 2, "w_branch_b": 2, "w_o": 1, "w_ffn_up": 2,
              "w_ffn_down": 1}
BIG_NAMES = tuple(n for n in WEIGHT_NAMES if n in SHARD_AXIS)
SMALL_NAMES = tuple(n for n in WEIGHT_NAMES if n not in SHARD_AXIS)
ADAM_BLOCK_ELEMS = 256 * 1024
MM_TILE = 1024
MM_TILE_TOKENS = 2048
MM_TILE_K = 2048
PREP_ROWS = 512
ATTN_TQ = 1024
ATTN_TK = 1024


def _params(*semantics):
    return pltpu.CompilerParams(dimension_semantics=semantics, vmem_limit_bytes=VMEM_LIMIT)


def _tile(n, pref):
    if n <= pref:
        return n
    t = (pref // LANES) * LANES
    while n % t:
        t -= LANES
    return t


def _fold8(t):
    return t.reshape(t.shape[0] // SUBLANES, SUBLANES, t.shape[1]).sum(axis=0)


_DIMS = {"nn": ((1,), (0,)), "nt": ((1,), (1,)), "tn": ((0,), (0,))}


def _matmul(a, b, mode, name, post=None, h=None, stack=None):
    out_dt = F32 if mode == "tn" else BF16
    if mode == "nn":
        (m, k), n = a.shape, b.shape[1]
    elif mode == "nt":
        (m, k), n = a.shape, b.shape[0]
    else:
        (k, m), n = a.shape, b.shape[1]
    tm = _tile(m, MM_TILE_TOKENS if mode != "tn" and k <= MM_TILE else MM_TILE)
    tn, tk = _tile(n, MM_TILE), _tile(k, MM_TILE_K)
    nk = k // tk
    dims = (_DIMS[mode], ((), ()))
    operands = [a, b] + ([h] if post in ("relu2_bwd", "delta") else []) + ([stack[0]] if stack else [])
    n_in = len(operands)
    n_out = 2 if post in ("relu2", "delta") else 1
    assert post != "delta" or tn == n, (n, tn)

    def body(*refs):
        a_ref, b_ref = refs[:2]
        o_refs, acc_ref = refs[n_in:n_in + n_out], refs[-1]

        def finish(val):
            if post == "relu2":
                o_refs[0][...] = val.astype(out_dt)
                r = jnp.maximum(val, 0.0)
                o_refs[1][...] = (r * r).astype(BF16)
            elif post == "relu2_bwd":
                o_refs[0][...] = (val * (2.0 * jnp.maximum(refs[2][...].astype(F32), 0.0))).astype(BF16)
            elif post == "delta":
                do = val.astype(BF16)
                o_refs[0][...] = do
                prod = do.astype(F32) * refs[2][...].astype(F32)
                for g in range(n // LANES):
                    x = prod[:, LANES * g:LANES * (g + 1)]
                    lo = _lo_mask(x.shape)
                    d0 = jnp.sum(jnp.where(lo, x, 0.0), axis=-1, keepdims=True)
                    d1 = jnp.sum(jnp.where(lo, 0.0, x), axis=-1, keepdims=True)
                    o_refs[1][2 * g] = jnp.broadcast_to(d0, (tm, LANES))
                    o_refs[1][2 * g + 1] = jnp.broadcast_to(d1, (tm, LANES))
            else:
                o_refs[0][...] = val.astype(out_dt)

        prod = lax.dot_general(a_ref[...], b_ref[...], dims, preferred_element_type=F32)
        if nk == 1:
            finish(prod)
        else:
            kk = pl.program_id(2)

            @pl.when(kk == 0)
            def _():
                acc_ref[...] = prod

            @pl.when(kk > 0)
            def _():
                acc_ref[...] += prod

            @pl.when(kk == nk - 1)
            def _():
                finish(acc_ref[...])

    if mode == "tn":
        a_spec = pl.BlockSpec((tk, tm), lambda i, j, kk: (kk, i))
    else:
        a_spec = pl.BlockSpec((tm, tk), lambda i, j, kk: (i, kk))
    if mode == "nt":
        b_spec = pl.BlockSpec((tn, tk), lambda i, j, kk: (j, kk))
    else:
        b_spec = pl.BlockSpec((tk, tn), lambda i, j, kk: (kk, j))
    o_spec = pl.BlockSpec((tm, tn), lambda i, j, kk: (i, j))
    main_out, bf16_out = jax.ShapeDtypeStruct((m, n), out_dt), jax.ShapeDtypeStruct((m, n), BF16)
    heads = n // HEAD_DIM
    delta_out = jax.ShapeDtypeStruct((heads, m, LANES), F32)
    out_shape = {None: main_out, "relu2": [main_out, bf16_out], "relu2_bwd": bf16_out,
                 "delta": [bf16_out, delta_out]}[post]
    in_specs = [a_spec, b_spec] + ([o_spec] if post in ("relu2_bwd", "delta") else [])
    out_specs = {None: o_spec, "relu2": [o_spec, o_spec], "relu2_bwd": o_spec,
                 "delta": [o_spec, pl.BlockSpec((heads, tm, LANES), lambda i, j, kk: (0, i, 0))]}[post]
    aliases = {}
    if stack:
        buf, layer = stack
        assert post is None and buf.shape[1:] == (m, n) and buf.dtype == out_dt, (buf.shape, buf.dtype)
        in_specs.append(pl.BlockSpec(memory_space=pl.ANY))
        out_specs = pl.BlockSpec((None, tm, tn), lambda i, j, kk: (layer, i, j))
        out_shape = jax.ShapeDtypeStruct(buf.shape, buf.dtype)
        aliases = {n_in - 1: 0}
    return pl.pallas_call(
        body,
        name=name,
        grid=(m // tm, n // tn, nk),
        in_specs=in_specs,
        out_specs=out_specs,
        out_shape=out_shape,
        scratch_shapes=[pltpu.VMEM((tm, tn), F32)],
        input_output_aliases=aliases,
        compiler_params=_params("parallel", "parallel", "arbitrary"),
    )(*operands)


def _rinv(x):
    return lax.rsqrt(jnp.mean(x * x, axis=-1, keepdims=True) + EPS)


def _rms_bwd_rows(x, g, dy):
    r = _rinv(x)
    xh = x * r
    dxh = dy * g
    dx = r * (dxh - xh * jnp.mean(dxh * xh, axis=-1, keepdims=True))
    return dx, dy * xh


def _row_spec(tm, c):
    return pl.BlockSpec((tm, c), lambda i: (i, 0))


def _vec_spec(c):
    return pl.BlockSpec((1, c), lambda i: (0, 0))


def _acc_spec(c):
    return pl.BlockSpec((SUBLANES, c), lambda i: (0, 0))


def _rms_fwd(x, g):
    t, d = x.shape
    tm = _tile(t, 512)

    def body(x_ref, g_ref, o_ref):
        xv = x_ref[...]
        o_ref[...] = (xv * _rinv(xv) * g_ref[...]).astype(BF16)

    return pl.pallas_call(
        body, name="rms_fwd", grid=(t // tm,),
        in_specs=[_row_spec(tm, d), _vec_spec(d)], out_specs=_row_spec(tm, d),
        out_shape=jax.ShapeDtypeStruct((t, d), BF16), compiler_params=_params("parallel"),
    )(x, g.reshape(1, d))


def _rms_bwd(x, g, dres, dy):
    t, d = x.shape
    tm = _tile(t, 512)

    def body(x_ref, g_ref, dres_ref, dy_ref, dx_ref, dg_ref):
        dx, dgc = _rms_bwd_rows(x_ref[...], g_ref[...], dy_ref[...].astype(F32))
        dx_ref[...] = dres_ref[...] + dx

        @pl.when(pl.program_id(0) == 0)
        def _():
            dg_ref[...] = jnp.zeros_like(dg_ref)

        dg_ref[...] += _fold8(dgc)

    return pl.pallas_call(
        body, name="rms_bwd", grid=(t // tm,),
        in_specs=[_row_spec(tm, d), _vec_spec(d), _row_spec(tm, d), _row_spec(tm, d)],
        out_specs=[_row_spec(tm, d), _acc_spec(d)],
        out_shape=[jax.ShapeDtypeStruct((t, d), F32), jax.ShapeDtypeStruct((SUBLANES, d), F32)],
        compiler_params=_params("arbitrary"),
    )(x, g.reshape(1, d), dres, dy)


def _res_norm_fwd(x, m, g_post, g_next):
    t, d = x.shape
    tm = _tile(t, 512)

    def body(x_ref, m_ref, gp_ref, gn_ref, x2_ref, u2_ref):
        mv = m_ref[...].astype(F32)
        x2 = x_ref[...] + mv * _rinv(mv) * gp_ref[...]
        x2_ref[...] = x2
        u2_ref[...] = (x2 * _rinv(x2) * gn_ref[...]).astype(BF16)

    return pl.pallas_call(
        body, name="res_norm_fwd", grid=(t // tm,),
        in_specs=[_row_spec(tm, d), _row_spec(tm, d), _vec_spec(d), _vec_spec(d)],
        out_specs=[_row_spec(tm, d), _row_spec(tm, d)],
        out_shape=[jax.ShapeDtypeStruct((t, d), F32), jax.ShapeDtypeStruct((t, d), BF16)],
        compiler_params=_params("parallel"),
    )(x, m, g_post.reshape(1, d), g_next.reshape(1, d))


def _res_norm_bwd(x2, m, g_post, g_next, dx2_in, du2):
    t, d = x2.shape
    tm = _tile(t, 512)

    def body(x2_ref, m_ref, gp_ref, gn_ref, dx2in_ref, du2_ref, dx2_ref, dm_ref, dgp_ref, dgn_ref):
        dxn, dgn_c = _rms_bwd_rows(x2_ref[...], gn_ref[...], du2_ref[...].astype(F32))
        dx2 = dx2in_ref[...] + dxn
        dx2_ref[...] = dx2
        dm, dgp_c = _rms_bwd_rows(m_ref[...].astype(F32), gp_ref[...], dx2)
        dm_ref[...] = dm.astype(BF16)

        @pl.when(pl.program_id(0) == 0)
        def _():
            dgp_ref[...] = jnp.zeros_like(dgp_ref)
            dgn_ref[...] = jnp.zeros_like(dgn_ref)

        dgp_ref[...] += _fold8(dgp_c)
        dgn_ref[...] += _fold8(dgn_c)

    return pl.pallas_call(
        body, name="res_norm_bwd", grid=(t // tm,),
        in_specs=[_row_spec(tm, d), _row_spec(tm, d), _vec_spec(d), _vec_spec(d), _row_spec(tm, d), _row_spec(tm, d)],
        out_specs=[_row_spec(tm, d), _row_spec(tm, d), _acc_spec(d), _acc_spec(d)],
        out_shape=[jax.ShapeDtypeStruct((t, d), F32), jax.ShapeDtypeStruct((t, d), BF16),
                   jax.ShapeDtypeStruct((SUBLANES, d), F32), jax.ShapeDtypeStruct((SUBLANES, d), F32)],
        compiler_params=_params("arbitrary"),
    )(x2, m, g_post.reshape(1, d), g_next.reshape(1, d), dx2_in, du2)


def _rope_tables(t):
    rows = t // GRID_W
    row = jnp.repeat(jnp.arange(rows, dtype=F32), GRID_W)
    col = jnp.tile(jnp.arange(GRID_W, dtype=F32), rows)

    def tab(rot_dim):
        half = rot_dim // 2
        inv = ROPE_THETA ** (-jnp.arange(0, half, 2, dtype=F32) / half)
        ar = row[:, None] * inv[None, :]
        ac = col[:, None] * inv[None, :]
        ang = jnp.concatenate([ar, ar, ac, ac], axis=-1)
        q = half // 2
        sign = np.tile(np.concatenate([-np.ones(q, np.float32), np.ones(q, np.float32)]), 2)
        return jnp.cos(ang), jnp.sin(ang) * sign[None, :]

    ca, sa = tab(HEAD_DIM)
    cb, sb = tab(MLA_ROPE_DIM)
    one = jnp.ones((t, 1), F32)
    cos_b = jnp.concatenate([one * jnp.ones((1, KR_LANE0), F32), cb, one * jnp.ones((1, 32), F32)], axis=-1)
    sin_b = jnp.concatenate([jnp.zeros((t, KR_LANE0), F32), sb, jnp.zeros((t, 32), F32)], axis=-1)
    return jnp.tile(ca, (1, GQA_HEADS)), jnp.tile(sa, (1, GQA_HEADS)), cos_b, sin_b


def _swap_halves(x, sh):
    lane = lax.broadcasted_iota(jnp.int32, x.shape, 1)
    up = pltpu.roll(x, LANES - sh, 1)
    dn = pltpu.roll(x, sh, 1)
    return jnp.where((lane & (2 * sh - 1)) < sh, up, dn)


def _rope(x, cos, sin_s, sh):
    return x * cos + _swap_halves(x, sh) * sin_s


def _rope_bwd(dy, cos, sin_s, sh):
    return dy * cos + _swap_halves(dy * sin_s, sh)


def _lo_mask(shape):
    return lax.broadcasted_iota(jnp.int32, shape, 1) < HEAD_DIM


def _half_mean(t, lo):
    s_lo = jnp.sum(jnp.where(lo, t, 0.0), axis=-1, keepdims=True)
    s_hi = jnp.sum(jnp.where(lo, 0.0, t), axis=-1, keepdims=True)
    return jnp.where(lo, s_lo, s_hi) * (1.0 / HEAD_DIM)


def _head_norm(x, g2):
    lo = _lo_mask(x.shape)
    r = lax.rsqrt(_half_mean(x * x, lo) + EPS)
    return x * r * g2


def _head_norm_bwd(x, g2, dy):
    lo = _lo_mask(x.shape)
    r = lax.rsqrt(_half_mean(x * x, lo) + EPS)
    xh = x * r
    dxh = dy * g2
    dx = r * (dxh - xh * _half_mean(dxh * xh, lo))
    return dx, dy * xh


def _prep_a_fwd(z, gq2, gk2, gqa, gkva, cos_a, sin_a, cos_b, sin_b):
    t = z.shape[0]
    tm = _tile(t, PREP_ROWS)

    def body(z_ref, gq_ref, gk_ref, gqa_ref, gkva_ref, ca_ref, sa_ref, cb_ref, sb_ref,
             qa_ref, ka_ref, va_ref, cqn_ref, ckvn_ref, krr_ref):
        def zf(lo, hi):
            return z_ref[:, lo:hi].astype(F32)

        for j in range(4):
            cols = slice(LANES * j, LANES * (j + 1))
            y = _rope(_head_norm(zf(LANES * j, LANES * (j + 1)), gq_ref[...]), ca_ref[:, cols], sa_ref[:, cols], 16)
            qa_ref[:, cols] = (y * (GQA_SCALE * LOG2E)).astype(BF16)
        y = _rope(_head_norm(zf(Z_KA, Z_VA), gk_ref[...]), ca_ref[:, :LANES], sa_ref[:, :LANES], 16)
        ka_ref[...] = y.astype(BF16)
        va_ref[...] = z_ref[:, Z_VA:Z_CQ].astype(BF16)
        cq = zf(Z_CQ, Z_CKV)
        cqn_ref[...] = (cq * _rinv(cq) * gqa_ref[...]).astype(BF16)
        ckv = zf(Z_CKV, Z_KR)
        ckvn_ref[...] = (ckv * _rinv(ckv) * gkva_ref[...]).astype(BF16)
        krr_ref[...] = _rope(zf(Z_KR, Z_GATE), cb_ref[...], sb_ref[...], 8)

    return pl.pallas_call(
        body, name="prep_a_fwd", grid=(t // tm,),
        in_specs=[_row_spec(tm, Z_ATT_W), _vec_spec(LANES), _vec_spec(LANES), _vec_spec(MLA_Q_RANK),
                  _vec_spec(MLA_KV_RANK), _row_spec(tm, 512), _row_spec(tm, 512), _row_spec(tm, LANES),
                  _row_spec(tm, LANES)],
        out_specs=[_row_spec(tm, 512), _row_spec(tm, LANES), _row_spec(tm, LANES), _row_spec(tm, MLA_Q_RANK),
                   _row_spec(tm, MLA_KV_RANK), _row_spec(tm, LANES)],
        out_shape=[jax.ShapeDtypeStruct((t, 512), BF16), jax.ShapeDtypeStruct((t, LANES), BF16),
                   jax.ShapeDtypeStruct((t, LANES), BF16), jax.ShapeDtypeStruct((t, MLA_Q_RANK), BF16),
                   jax.ShapeDtypeStruct((t, MLA_KV_RANK), BF16), jax.ShapeDtypeStruct((t, LANES), F32)],
        compiler_params=_params("parallel"),
    )(z, gq2, gk2, gqa, gkva, cos_a, sin_a, cos_b, sin_b)


def _prep_a_bwd(z, dqa, dka4, dva4, dcqn, dckvn, dkr, dzga, dzgb, gq2, gk2, gqa, gkva, cos_a, sin_a):
    t = z.shape[0]
    tm = _tile(t, PREP_ROWS)

    def body(z_ref, dqa_ref, dka_ref, dva_ref, dcqn_ref, dckvn_ref, dkr_ref, dzga_ref, dzgb_ref, gq_ref, gk_ref,
             gqa_ref, gkva_ref, ca_ref, sa_ref, dz_ref, dgq_ref, dgk_ref, dgqa_ref, dgkva_ref):
        @pl.when(pl.program_id(0) == 0)
        def _():
            dgq_ref[...] = jnp.zeros_like(dgq_ref)
            dgk_ref[...] = jnp.zeros_like(dgk_ref)
            dgqa_ref[...] = jnp.zeros_like(dgqa_ref)
            dgkva_ref[...] = jnp.zeros_like(dgkva_ref)

        def zf(lo, hi):
            return z_ref[:, lo:hi].astype(F32)

        dgq = jnp.zeros((SUBLANES, LANES), F32)
        for j in range(4):
            cols = slice(LANES * j, LANES * (j + 1))
            dy = _rope_bwd(dqa_ref[:, cols] * GQA_SCALE, ca_ref[:, cols], sa_ref[:, cols], 16)
            dx, dgc = _head_norm_bwd(zf(LANES * j, LANES * (j + 1)), gq_ref[...], dy)
            dz_ref[:, cols] = dx.astype(BF16)
            dgq = dgq + _fold8(dgc)
        dgq_ref[...] += dgq
        dk = (dka_ref[0] + dka_ref[1] + dka_ref[2] + dka_ref[3]).T * LN2
        dy = _rope_bwd(dk, ca_ref[:, :LANES], sa_ref[:, :LANES], 16)
        dx, dgc = _head_norm_bwd(zf(Z_KA, Z_VA), gk_ref[...], dy)
        dz_ref[:, Z_KA:Z_VA] = dx.astype(BF16)
        dgk_ref[...] += _fold8(dgc)
        dz_ref[:, Z_VA:Z_CQ] = (dva_ref[0] + dva_ref[1] + dva_ref[2] + dva_ref[3]).T.astype(BF16)
        dx, dgc = _rms_bwd_rows(zf(Z_CQ, Z_CKV), gqa_ref[...], dcqn_ref[...].astype(F32))
        dz_ref[:, Z_CQ:Z_CKV] = dx.astype(BF16)
        dgqa_ref[...] += _fold8(dgc)
        dx, dgc = _rms_bwd_rows(zf(Z_CKV, Z_KR), gkva_ref[...], dckvn_ref[...].astype(F32))
        dz_ref[:, Z_CKV:Z_KR] = dx.astype(BF16)
        dgkva_ref[...] += _fold8(dgc)
        dz_ref[:, Z_KR:Z_GATE] = dkr_ref[...].astype(BF16)
        dz_ref[:, Z_GATE:Z_GATE + D_MODEL] = dzga_ref[...]
        dz_ref[:, Z_GATE + D_MODEL:Z_W] = dzgb_ref[...]

    part = pl.BlockSpec((4, LANES, tm), lambda i: (0, 0, i))
    return pl.pallas_call(
        body, name="prep_a_bwd", grid=(t // tm,),
        in_specs=[_row_spec(tm, Z_ATT_W), _row_spec(tm, 512), part, part, _row_spec(tm, MLA_Q_RANK),
                  _row_spec(tm, MLA_KV_RANK), _row_spec(tm, LANES), _row_spec(tm, D_MODEL), _row_spec(tm, D_MODEL),
                  _vec_spec(LANES),
                  _vec_spec(LANES), _vec_spec(MLA_Q_RANK), _vec_spec(MLA_KV_RANK), _row_spec(tm, 512),
                  _row_spec(tm, 512)],
        out_specs=[_row_spec(tm, Z_W), _acc_spec(LANES), _acc_spec(LANES), _acc_spec(MLA_Q_RANK),
                   _acc_spec(MLA_KV_RANK)],
        out_shape=[jax.ShapeDtypeStruct((t, Z_W), BF16), jax.ShapeDtypeStruct((SUBLANES, LANES), F32),
                   jax.ShapeDtypeStruct((SUBLANES, LANES), F32), jax.ShapeDtypeStruct((SUBLANES, MLA_Q_RANK), F32),
                   jax.ShapeDtypeStruct((SUBLANES, MLA_KV_RANK), F32)],
        compiler_params=_params("arbitrary"),
    )(z, dqa, dka4, dva4, dcqn, dckvn, dkr, dzga, dzgb, gq2, gk2, gqa, gkva, cos_a, sin_a)


def _prep_b_fwd(qb, kvb, krr, cos_b, sin_b):
    t = qb.shape[0]
    tm = _tile(t, PREP_ROWS)

    def body(qb_ref, kvb_ref, krr_ref, cb_ref, sb_ref, q_ref, k_ref, v_ref):
        for h in range(MLA_HEADS):
            cols = slice(LANES * h, LANES * (h + 1))
            qh = _rope(qb_ref[:, cols].astype(F32), cb_ref[...], sb_ref[...], 8)
            q_ref[:, cols] = (qh * (MLA_SCALE * LOG2E)).astype(BF16)
            k_ref[:, cols] = (kvb_ref[:, cols].astype(F32) + krr_ref[...]).astype(BF16)
        v_ref[...] = kvb_ref[:, 1024:1536].astype(BF16)

    return pl.pallas_call(
        body, name="prep_b_fwd", grid=(t // tm,),
        in_specs=[_row_spec(tm, 1024), _row_spec(tm, 1536), _row_spec(tm, LANES), _row_spec(tm, LANES),
                  _row_spec(tm, LANES)],
        out_specs=[_row_spec(tm, 1024), _row_spec(tm, 1024), _row_spec(tm, 512)],
        out_shape=[jax.ShapeDtypeStruct((t, 1024), BF16), jax.ShapeDtypeStruct((t, 1024), BF16),
                   jax.ShapeDtypeStruct((t, 512), BF16)],
        compiler_params=_params("parallel"),
    )(qb, kvb, krr, cos_b, sin_b)


def _prep_b_bwd(dq, dk, dv, cos_b, sin_b):
    t = dq.shape[0]
    tm = _tile(t, PREP_ROWS)

    def body(dq_ref, dk_ref, dv_ref, cb_ref, sb_ref, dqb_ref, dkvb_ref, dkr_ref):
        dkr = jnp.zeros((tm, LANES), F32)
        for h in range(MLA_HEADS):
            cols = slice(LANES * h, LANES * (h + 1))
            dqb_ref[:, cols] = _rope_bwd(dq_ref[:, cols] * MLA_SCALE, cb_ref[...], sb_ref[...], 8).astype(BF16)
            dkh = dk_ref[cols, :].T * LN2
            dkvb_ref[:, cols] = dkh.astype(BF16)
            dkr = dkr + dkh
        for j in range(MLA_HEADS // 2):
            dkvb_ref[:, 1024 + LANES * j:1024 + LANES * (j + 1)] = dv_ref[LANES * j:LANES * (j + 1), :].T.astype(BF16)
        dkr_ref[...] = _rope_bwd(dkr, cb_ref[...], sb_ref[...], 8)

    return pl.pallas_call(
        body, name="prep_b_bwd", grid=(t // tm,),
        in_specs=[_row_spec(tm, 1024), pl.BlockSpec((1024, tm), lambda i: (0, i)),
                  pl.BlockSpec((512, tm), lambda i: (0, i)), _row_spec(tm, LANES),
                  _row_spec(tm, LANES)],
        out_specs=[_row_spec(tm, 1024), _row_spec(tm, 1536), _row_spec(tm, LANES)],
        out_shape=[jax.ShapeDtypeStruct((t, 1024), BF16), jax.ShapeDtypeStruct((t, 1536), BF16),
                   jax.ShapeDtypeStruct((t, LANES), F32)],
        compiler_params=_params("parallel"),
    )(dq, dk, dv, cos_b, sin_b)


_NT = (((1,), (1,)), ((), ()))
_NN = (((1,), (0,)), ((), ()))
_TN = (((0,), (0,)), ((), ()))


def _head_operands(qv, kv, i, shared_k):
    if shared_k:
        lo = _lo_mask(qv.shape)
        keep = lo if i == 0 else jnp.logical_not(lo)
        return jnp.where(keep, qv, jnp.zeros_like(qv)), kv
    cols = slice(LANES * i, LANES * (i + 1))
    return qv[:, cols], kv[:, cols]


def _attn_specs(shared_k, tq, tk, q_of, k_of):
    wq = LANES if shared_k else 2 * LANES
    q_spec = pl.BlockSpec((tq, wq), lambda *g: (q_of(*g), g[0]))
    if shared_k:
        k_spec = pl.BlockSpec((tk, LANES), lambda *g: (k_of(*g), 0))
        v_spec = pl.BlockSpec((tk, LANES), lambda *g: (k_of(*g), 0))
    else:
        k_spec = pl.BlockSpec((tk, wq), lambda *g: (k_of(*g), g[0]))
        v_spec = pl.BlockSpec((tk, LANES), lambda *g: (k_of(*g), g[0]))
    return wq, q_spec, k_spec, v_spec


def _attn_fwd(q, k, v, shared_k, name):
    t = q.shape[0]
    tq, tk = _tile(t, ATTN_TQ), _tile(t, ATTN_TK)
    nq, nk = t // tq, t // tk
    wq, q_spec, k_spec, v_spec = _attn_specs(shared_k, tq, tk, lambda p, i, j: i, lambda p, i, j: j)
    groups = q.shape[1] // wq
    chunk = _tile(tq, 2 * LANES)

    def body(q_ref, k_ref, v_ref, o_ref, lse_ref, m_s, acc_s, alpha_s, s_s, p_s):
        kb = pl.program_id(2)

        @pl.when(kb == 0)
        def _():
            m_s[...] = jnp.full_like(m_s, -jnp.inf)
            acc_s[...] = jnp.zeros_like(acc_s)

        qv, kv, vv = q_ref[...], k_ref[...], v_ref[...]
        lo = _lo_mask(vv.shape)
        for i in range(2):
            qi, ki = _head_operands(qv, kv, i, shared_k)
            s_s[i] = lax.dot_general(ki, qi, _NT, preferred_element_type=F32)
        for i in range(2):
            for c in range(tq // chunk):
                cols = slice(c * chunk, (c + 1) * chunk)
                m_prev = m_s[i, :, cols]
                m_new = jnp.maximum(m_prev, jnp.max(s_s[i, :, cols], axis=0, keepdims=True))
                alpha_s[i, :, cols] = jnp.exp2(m_prev - m_new)
                m_s[i, :, cols] = m_new
                p_s[i, :, cols] = jnp.exp2(s_s[i, :, cols] - m_new).astype(BF16)
        for i in range(2):
            keep = lo if i == 0 else jnp.logical_not(lo)
            vi = jnp.where(keep, vv, jnp.ones_like(vv))
            acc_s[i] = alpha_s[i] * acc_s[i] + lax.dot_general(vi, p_s[i], _TN, preferred_element_type=F32)

        @pl.when(kb == nk - 1)
        def _():
            a0, a1 = acc_s[0], acc_s[1]
            l0 = a0[LANES - SUBLANES:, :][0:1, :]
            l1 = a1[0:SUBLANES, :][0:1, :]
            row_lo = lax.broadcasted_iota(jnp.int32, a0.shape, 0) < HEAD_DIM
            o_ref[...] = jnp.where(row_lo, a0 / l0, a1 / l1).T.astype(BF16)
            lse_ref[0] = jnp.broadcast_to(m_s[0] + jnp.log2(l0), (LANES, tq)).T
            lse_ref[1] = jnp.broadcast_to(m_s[1] + jnp.log2(l1), (LANES, tq)).T

    return pl.pallas_call(
        body, name=name, grid=(groups, nq, nk),
        in_specs=[q_spec, k_spec, v_spec],
        out_specs=[pl.BlockSpec((tq, LANES), lambda p, i, j: (i, p)),
                   pl.BlockSpec((2, tq, LANES), lambda p, i, j: (p, i, 0))],
        out_shape=[jax.ShapeDtypeStruct((t, LANES * groups), BF16),
                   jax.ShapeDtypeStruct((2 * groups, t, LANES), F32)],
        scratch_shapes=[pltpu.VMEM((2, 1, tq), F32), pltpu.VMEM((2, LANES, tq), F32), pltpu.VMEM((2, 1, tq), F32),
                        pltpu.VMEM((2, tk, tq), F32), pltpu.VMEM((2, tk, tq), BF16)],
        compiler_params=_params("parallel", "parallel", "arbitrary"),
    )(q, k, v)


def _attn_bwd(q, k, v, do, lse, delta, shared_k, name):
    t = q.shape[0]
    tq, tk = _tile(t, ATTN_TQ), _tile(t, ATTN_TK)
    nq, nk = t // tq, t // tk
    wq, q_spec, k_spec, v_spec = _attn_specs(shared_k, tq, tk, lambda p, j, i: i, lambda p, j, i: j)
    groups = q.shape[1] // wq

    def body(q_ref, k_ref, v_ref, do_ref, lse_ref, delta_ref, dq_ref, dk_ref, dv_ref, dk_s, dv_s, s_s, dp_s, p_s,
             ds_s):
        kb, qb = pl.program_id(1), pl.program_id(2)

        @pl.when(qb == 0)
        def _():
            dk_s[...] = jnp.zeros_like(dk_s)
            dv_s[...] = jnp.zeros_like(dv_s)

        qv, kv, vv, dov = q_ref[...], k_ref[...], v_ref[...], do_ref[...]
        lo = _lo_mask(dov.shape)
        heads = []
        for i in range(2):
            qi, ki = _head_operands(qv, kv, i, shared_k)
            keep = lo if i == 0 else jnp.logical_not(lo)
            doi = jnp.where(keep, dov, jnp.zeros_like(dov))
            heads.append((qi, ki, doi))
            s_s[i] = lax.dot_general(qi, ki, _NT, preferred_element_type=F32)
            dp_s[i] = lax.dot_general(doi, vv, _NT, preferred_element_type=F32)
        for i in range(2):
            lse_i, delta_i = lse_ref[i], delta_ref[i]
            for c in range(tk // LANES):
                cols = slice(c * LANES, (c + 1) * LANES)
                p = jnp.exp2(s_s[i, :, cols] - lse_i)
                p_s[i, :, cols] = p.astype(BF16)
                ds_s[i, :, cols] = (p * (dp_s[i, :, cols] - delta_i)).astype(BF16)
        dq_parts = []
        for i in range(2):
            qi, ki, doi = heads[i]
            dv_s[...] += lax.dot_general(doi, p_s[i], _TN, preferred_element_type=F32)
            dk_i = lax.dot_general(qi, ds_s[i], _TN, preferred_element_type=F32)
            if shared_k:
                dk_s[...] += dk_i
            else:
                dk_s[LANES * i:LANES * (i + 1), :] += dk_i
            dq_parts.append(lax.dot_general(ds_s[i], ki, _NN, preferred_element_type=F32))
        rows = pl.ds(pl.multiple_of(qb * tq, tq), tq)
        if shared_k:
            tiles = [(slice(0, LANES), jnp.where(lo, dq_parts[0], dq_parts[1]))]
        else:
            tiles = [(slice(0, LANES), dq_parts[0]), (slice(LANES, 2 * LANES), dq_parts[1])]
        for cols, val in tiles:
            @pl.when(kb == 0)
            def _(cols=cols, val=val):
                dq_ref[rows, cols] = val

            @pl.when(kb > 0)
            def _(cols=cols, val=val):
                dq_ref[rows, cols] += val

        @pl.when(qb == nq - 1)
        def _():
            if shared_k:
                dk_ref[0] = dk_s[...]
                dv_ref[0] = dv_s[...]
            else:
                dk_ref[...] = dk_s[...]
                dv_ref[...] = dv_s[...]

    stat_spec = pl.BlockSpec((2, tq, LANES), lambda p, j, i: (p, i, 0))
    do_spec = pl.BlockSpec((tq, LANES), lambda p, j, i: (i, p))
    dq_spec = pl.BlockSpec((t, wq), lambda p, j, i: (0, p))
    if shared_k:
        dk_spec = pl.BlockSpec((1, LANES, tk), lambda p, j, i: (p, 0, j))
        dv_spec = dk_spec
        dk_shape = jax.ShapeDtypeStruct((groups, LANES, t), F32)
        dv_shape = dk_shape
    else:
        dk_spec = pl.BlockSpec((wq, tk), lambda p, j, i: (p, j))
        dv_spec = pl.BlockSpec((LANES, tk), lambda p, j, i: (p, j))
        dk_shape = jax.ShapeDtypeStruct((wq * groups, t), F32)
        dv_shape = jax.ShapeDtypeStruct((LANES * groups, t), F32)
    return pl.pallas_call(
        body, name=name, grid=(groups, nk, nq),
        in_specs=[q_spec, k_spec, v_spec, do_spec, stat_spec, stat_spec],
        out_specs=[dq_spec, dk_spec, dv_spec],
        out_shape=[jax.ShapeDtypeStruct((t, wq * groups), F32), dk_shape, dv_shape],
        scratch_shapes=[pltpu.VMEM((wq, tk), F32), pltpu.VMEM((LANES, tk), F32), pltpu.VMEM((2, tq, tk), F32),
                        pltpu.VMEM((2, tq, tk), F32), pltpu.VMEM((2, tq, tk), BF16), pltpu.VMEM((2, tq, tk), BF16)],
        compiler_params=_params("parallel", "arbitrary", "arbitrary"),
    )(q, k, v, do, lse, delta)


_MERGE_W = 512
_GATE_BLK0 = Z_GATE // _MERGE_W


def _merge_fwd(z, b_gate, ta, tb):
    t = z.shape[0]
    tm = _tile(t, 512)
    w = _MERGE_W
    nj = D_MODEL // w

    def body(za_ref, zb_ref, ba_ref, bb_ref, ta_ref, tb_ref, o_ref):
        ga = jax.nn.sigmoid(za_ref[...].astype(F32) + ba_ref[...])
        gb = jax.nn.sigmoid(zb_ref[...].astype(F32) + bb_ref[...])
        o_ref[...] = (ga * ta_ref[...].astype(F32) + gb * tb_ref[...].astype(F32)).astype(BF16)

    return pl.pallas_call(
        body, name="merge_fwd", grid=(t // tm, nj),
        in_specs=[pl.BlockSpec((tm, w), lambda i, j: (i, _GATE_BLK0 + j)),
                  pl.BlockSpec((tm, w), lambda i, j: (i, _GATE_BLK0 + nj + j)),
                  pl.BlockSpec((1, w), lambda i, j: (0, j)),
                  pl.BlockSpec((1, w), lambda i, j: (0, nj + j)),
                  pl.BlockSpec((tm, w), lambda i, j: (i, j)),
                  pl.BlockSpec((tm, w), lambda i, j: (i, j))],
        out_specs=pl.BlockSpec((tm, w), lambda i, j: (i, j)),
        out_shape=jax.ShapeDtypeStruct((t, D_MODEL), BF16),
        compiler_params=_params("parallel", "parallel"),
    )(z, z, b_gate, b_gate, ta, tb)


def _merge_bwd(dmg, z, b_gate, ta, tb):
    t = z.shape[0]
    tm = _tile(t, 512)
    w = _MERGE_W
    nj = D_MODEL // w

    def body(dm_ref, za_ref, zb_ref, ba_ref, bb_ref, ta_ref, tb_ref, dta_ref, dtb_ref, dza_ref, dzb_ref,
             dba_ref, dbb_ref):
        dm = dm_ref[...].astype(F32)
        ga = jax.nn.sigmoid(za_ref[...].astype(F32) + ba_ref[...])
        gb = jax.nn.sigmoid(zb_ref[...].astype(F32) + bb_ref[...])
        dta_ref[...] = (dm * ga).astype(BF16)
        dtb_ref[...] = (dm * gb).astype(BF16)
        dza = dm * ta_ref[...].astype(F32) * ga * (1.0 - ga)
        dzb = dm * tb_ref[...].astype(F32) * gb * (1.0 - gb)
        dza_ref[...] = dza.astype(BF16)
        dzb_ref[...] = dzb.astype(BF16)

        @pl.when(pl.program_id(1) == 0)
        def _():
            dba_ref[...] = jnp.zeros_like(dba_ref)
            dbb_ref[...] = jnp.zeros_like(dbb_ref)

        dba_ref[...] += _fold8(dza)
        dbb_ref[...] += _fold8(dzb)

    blk = pl.BlockSpec((tm, w), lambda j, i: (i, j))
    acc = pl.BlockSpec((SUBLANES, w), lambda j, i: (0, j))
    return pl.pallas_call(
        body, name="merge_bwd", grid=(nj, t // tm),
        in_specs=[blk,
                  pl.BlockSpec((tm, w), lambda j, i: (i, _GATE_BLK0 + j)),
                  pl.BlockSpec((tm, w), lambda j, i: (i, _GATE_BLK0 + nj + j)),
                  pl.BlockSpec((1, w), lambda j, i: (0, j)),
                  pl.BlockSpec((1, w), lambda j, i: (0, nj + j)),
                  blk, blk],
        out_specs=[blk, blk, blk, blk, acc, acc],
        out_shape=[jax.ShapeDtypeStruct((t, D_MODEL), BF16)] * 4 + [jax.ShapeDtypeStruct((SUBLANES, D_MODEL), F32)] * 2,
        compiler_params=_params("parallel", "arbitrary"),
    )(dmg, z, z, b_gate, b_gate, ta, tb)


def _loss_grad(y, target):
    t, d = y.shape
    tm = _tile(t, 512)

    def body(y_ref, t_ref, dy_ref, acc_ref):
        err = y_ref[...] - t_ref[...]
        dy_ref[...] = err * (1.0 / d)
        e8 = _fold8(err * err)
        part = e8[:, 0:LANES]
        for c in range(1, d // LANES):
            part = part + e8[:, LANES * c:LANES * (c + 1)]

        @pl.when(pl.program_id(0) == 0)
        def _():
            acc_ref[...] = jnp.zeros_like(acc_ref)

        acc_ref[...] += part

    return pl.pallas_call(
        body, name="loss_grad", grid=(t // tm,),
        in_specs=[_row_spec(tm, d), _row_spec(tm, d)],
        out_specs=[_row_spec(tm, d), _acc_spec(LANES)],
        out_shape=[jax.ShapeDtypeStruct((t, d), F32), jax.ShapeDtypeStruct((SUBLANES, LANES), F32)],
        compiler_params=_params("arbitrary"),
    )(y, target)


_MESH_ID = pl.DeviceIdType.MESH
_ANY = pl.BlockSpec(memory_space=pl.ANY)


def _all_gather(arrays):
    n = len(arrays)
    halves = []
    for a in arrays:
        assert a.shape[0] % 2 == 0, a.shape
        halves.append((pl.ds(0, a.shape[0] // 2), pl.ds(a.shape[0] // 2, a.shape[0] // 2)))
    OWN_SIB, OWN_X, OWN_Y, FWD_X, FWD_Y, SIB_X, SIB_Y, SIB_DA, SIB_DB = range(9)

    def body(*refs):
        x_refs, out_refs = refs[:n], refs[n:2 * n]
        send_sems, recv_sems, local_sems = refs[2 * n:]
        mx, my, mc = lax.axis_index("x"), lax.axis_index("y"), lax.axis_index("c")
        me, sibling = (mx, my, mc), (mx, my, 1 - mc)
        x_nbr, y_nbr, diag = (1 - mx, my, mc), (mx, 1 - my, mc), (1 - mx, 1 - my, mc)

        def slot(a, dev, rows=None):
            px, py, pc = dev
            ref = out_refs[a].at[4 * px + 2 * py + pc]
            return ref if rows is None else ref.at[rows]

        def other_core(dev):
            return (dev[0], dev[1], 1 - dev[2])

        def copy(a, sem, block, to, rows=None, src=None):
            return pltpu.make_async_remote_copy(
                src_ref=slot(a, block, rows) if src is None else src, dst_ref=slot(a, block, rows),
                send_sem=send_sems.at[a, sem], recv_sem=recv_sems.at[a, sem], device_id=to, device_id_type=_MESH_ID)

        mine = [pltpu.make_async_copy(x_refs[a], slot(a, me), local_sems.at[a]) for a in range(n)]
        sent = []
        for a in range(n):
            mine[a].start()
            sent += [copy(a, OWN_SIB, me, sibling, src=x_refs[a]), copy(a, OWN_X, me, x_nbr, src=x_refs[a]),
                     copy(a, OWN_Y, me, y_nbr, src=x_refs[a])]
        for cp in sent:
            cp.start()
        for a in range(n):
            first, second = halves[a]
            copy(a, OWN_Y, y_nbr, me).wait_recv()
            sent += [copy(a, FWD_X, y_nbr, x_nbr, rows=first), copy(a, SIB_Y, y_nbr, sibling)]
            sent[-2].start()
            sent[-1].start()
            copy(a, OWN_X, x_nbr, me).wait_recv()
            sent += [copy(a, FWD_Y, x_nbr, y_nbr, rows=second), copy(a, SIB_X, x_nbr, sibling)]
            sent[-2].start()
            sent[-1].start()
        for a in range(n):
            first, second = halves[a]
            copy(a, FWD_X, diag, me, rows=first).wait_recv()
            sent.append(copy(a, SIB_DA, diag, sibling, rows=first))
            sent[-1].start()
            copy(a, FWD_Y, diag, me, rows=second).wait_recv()
            sent.append(copy(a, SIB_DB, diag, sibling, rows=second))
            sent[-1].start()
        for a in range(n):
            first, second = halves[a]
            copy(a, OWN_SIB, sibling, me).wait_recv()
            copy(a, SIB_X, other_core(x_nbr), me).wait_recv()
            copy(a, SIB_Y, other_core(y_nbr), me).wait_recv()
            copy(a, SIB_DA, other_core(diag), me, rows=first).wait_recv()
            copy(a, SIB_DB, other_core(diag), me, rows=second).wait_recv()
        for cp in sent:
            cp.wait_send()
        for cp in mine:
            cp.wait()

    return pl.pallas_call(
        body, name="weight_all_gather",
        out_shape=[jax.ShapeDtypeStruct((N_DEV,) + a.shape, a.dtype) for a in arrays],
        in_specs=[_ANY] * n, out_specs=[_ANY] * n,
        scratch_shapes=[pltpu.SemaphoreType.DMA((n, 9)), pltpu.SemaphoreType.DMA((n, 9)),
                        pltpu.SemaphoreType.DMA((n,))],
    )(*arrays)


def _pair_exchange(sends):
    n = len(sends)

    def body(*refs):
        s_refs, r_refs = refs[:n], refs[n:2 * n]
        send_sems, recv_sems = refs[2 * n:]
        mx, my, mc = lax.axis_index("x"), lax.axis_index("y"), lax.axis_index("c")
        copies = []
        for a in range(n):
            for ch in range(4):
                cp = pltpu.make_async_remote_copy(
                    src_ref=s_refs[a].at[2 * ch + (1 - mc)], dst_ref=r_refs[a].at[ch], send_sem=send_sems.at[a, ch],
                    recv_sem=recv_sems.at[a, ch], device_id=(mx, my, 1 - mc), device_id_type=_MESH_ID)
                cp.start()
                copies.append(cp)
        for cp in copies:
            cp.wait_send()
            cp.wait_recv()

    return pl.pallas_call(
        body, name="grad_pair_exchange",
        out_shape=[jax.ShapeDtypeStruct((4,) + s.shape[1:], s.dtype) for s in sends],
        in_specs=[_ANY] * n, out_specs=[_ANY] * n,
        scratch_shapes=[pltpu.SemaphoreType.DMA((n, 4)), pltpu.SemaphoreType.DMA((n, 4))],
    )(*sends)


def _pair_add(send, half, core):
    _, r, c_ = send.shape
    tr = _row_tile(r, c_)

    def body(core_ref, s_ref, h_ref, o_ref):
        del core_ref
        o_ref[...] = (s_ref[...] + h_ref[...]).astype(BF16)

    blk = pl.BlockSpec((1, tr, c_), lambda ch, i, core_ref: (ch, i, 0))
    return pl.pallas_call(
        body, name="grad_pair_add",
        grid_spec=pltpu.PrefetchScalarGridSpec(
            num_scalar_prefetch=1, grid=(4, r // tr),
            in_specs=[pl.BlockSpec((1, tr, c_), lambda ch, i, core_ref: (2 * ch + core_ref[0], i, 0)), blk],
            out_specs=blk),
        out_shape=jax.ShapeDtypeStruct((4, r, c_), BF16),
        compiler_params=_params("parallel", "parallel"),
    )(core, send, half)


def _chip_exchange(parts):
    n = len(parts)

    def body(*refs):
        p_refs, r_refs = refs[:n], refs[n:2 * n]
        send_sems, recv_sems, local_sems = refs[2 * n:]
        mx, my, mc = lax.axis_index("x"), lax.axis_index("y"), lax.axis_index("c")
        mine = 2 * mx + my
        local = [pltpu.make_async_copy(p_refs[a].at[mine], r_refs[a].at[mine], local_sems.at[a]) for a in range(n)]
        copies = []
        for a in range(n):
            local[a].start()
            for rel in range(1, 4):
                px = 1 - mx if rel & 2 else mx
                py = 1 - my if rel & 1 else my
                cp = pltpu.make_async_remote_copy(
                    src_ref=p_refs[a].at[2 * px + py], dst_ref=r_refs[a].at[mine], send_sem=send_sems.at[a, rel - 1],
                    recv_sem=recv_sems.at[a, rel - 1], device_id=(px, py, mc), device_id_type=_MESH_ID)
                cp.start()
                copies.append(cp)
        for cp in copies:
            cp.wait_send()
            cp.wait_recv()
        for cp in local:
            cp.wait()

    return pl.pallas_call(
        body, name="grad_chip_exchange",
        out_shape=[jax.ShapeDtypeStruct(p.shape, p.dtype) for p in parts],
        in_specs=[_ANY] * n, out_specs=[_ANY] * n,
        scratch_shapes=[pltpu.SemaphoreType.DMA((n, 3)), pltpu.SemaphoreType.DMA((n, 3)),
                        pltpu.SemaphoreType.DMA((n,))],
    )(*parts)


def _row_tile(r, c_):
    tr = min(r, ADAM_BLOCK_ELEMS // (pl.cdiv(c_, LANES) * LANES))
    while r % tr:
        tr -= SUBLANES
    return tr


def _adamw(recv, w, m, v):
    r, c_ = w.shape
    tr = _row_tile(r, c_)
    n_src = recv.shape[0]

    def body(g_ref, w_ref, m_ref, v_ref, go_ref, d_ref, mo_ref, vo_ref):
        g = g_ref[0].astype(F32)
        for s in range(1, n_src):
            g = g + g_ref[s].astype(F32)
        go_ref[...] = g
        mn = ADAM_B1 * m_ref[...] + (1.0 - ADAM_B1) * g
        vn = ADAM_B2 * v_ref[...] + (1.0 - ADAM_B2) * (g * g)
        mo_ref[...] = mn
        vo_ref[...] = vn
        m_hat = mn / (1.0 - ADAM_B1 ** ADAM_STEP)
        v_hat = vn / (1.0 - ADAM_B2 ** ADAM_STEP)
        d_ref[...] = -ADAM_LR * (m_hat / (jnp.sqrt(v_hat) + ADAM_EPS) + ADAM_WD * w_ref[...])

    spec = pl.BlockSpec((tr, c_), lambda i: (i, 0))
    out = jax.ShapeDtypeStruct((r, c_), F32)
    return pl.pallas_call(
        body, name="grad_sum_adamw", grid=(r // tr,),
        in_specs=[pl.BlockSpec((n_src, tr, c_), lambda i: (0, i, 0)), spec, spec, spec],
        out_specs=[spec, spec, spec, spec], out_shape=[out, out, out, out],
        compiler_params=_params("parallel"),
    )(recv, w, m, v)


def _pad_cols(a, before, after):
    parts = []
    if before:
        parts.append(jnp.zeros(a.shape[:-1] + (before,), a.dtype))
    parts.append(a)
    if after:
        parts.append(jnp.zeros(a.shape[:-1] + (after,), a.dtype))
    return jnp.concatenate(parts, axis=-1)


def _q_head_pairs(a, axis):
    shp = a.shape
    a = a.reshape(shp[:axis] + (GQA_KV_HEADS, GQA_GROUP, HEAD_DIM) + shp[axis + 1:])
    a = jnp.swapaxes(a, axis, axis + 1)
    return a.reshape(shp)


def _q_head_unpairs(a, axis):
    shp = a.shape
    a = a.reshape(shp[:axis] + (GQA_GROUP, GQA_KV_HEADS, HEAD_DIM) + shp[axis + 1:])
    a = jnp.swapaxes(a, axis, axis + 1)
    return a.reshape(shp)


def _layout_weights(w):
    w_in = w["w_in"]
    lead = w_in.shape[:-1]
    w_in_p = jnp.concatenate([
        _q_head_pairs(w_in[..., 0:512], w_in.ndim - 1),
        w_in[..., 512:1408],
        _pad_cols(w_in[..., 1408:1440], KR_LANE0, LANES - KR_LANE0 - MLA_ROPE_DIM),
        w_in[..., 1440:],
    ], axis=-1)
    wq = w["w_q_up"]
    wq_p = _pad_cols(wq.reshape(wq.shape[:-1] + (MLA_HEADS, MLA_QK_DIM)), 0, LANES - MLA_QK_DIM)
    wq_p = wq_p.reshape(wq.shape[:-1] + (MLA_HEADS * LANES,))
    wkv = w["w_kv_up"]
    wkv4 = wkv.reshape(wkv.shape[:-1] + (MLA_HEADS, 2 * HEAD_DIM))
    wk_p = _pad_cols(wkv4[..., :HEAD_DIM], 0, LANES - HEAD_DIM).reshape(wkv.shape[:-1] + (MLA_HEADS * LANES,))
    wv_p = wkv4[..., HEAD_DIM:].reshape(wkv.shape[:-1] + (MLA_HEADS * HEAD_DIM,))
    del lead
    return {
        "w_in": w_in_p, "w_q_up": wq_p, "w_kv_up": jnp.concatenate([wk_p, wv_p], axis=-1),
        "w_branch_a": _q_head_pairs(w["w_branch_a"], w["w_branch_a"].ndim - 2), "w_branch_b": w["w_branch_b"],
        "w_o": w["w_o"], "w_ffn_up": w["w_ffn_up"], "w_ffn_down": w["w_ffn_down"],
    }


def _unlayout_grads(g):
    gi = g["w_in"]
    kr0 = Z_KR + KR_LANE0
    g_in = jnp.concatenate([
        _q_head_unpairs(gi[..., 0:512], gi.ndim - 1), gi[..., 512:1408], gi[..., kr0:kr0 + MLA_ROPE_DIM],
        gi[..., Z_GATE:],
    ], axis=-1)
    gq = g["w_q_up"]
    gq = gq.reshape(gq.shape[:-1] + (MLA_HEADS, LANES))[..., :MLA_QK_DIM]
    gq = gq.reshape(gq.shape[:-2] + (MLA_HEADS * MLA_QK_DIM,))
    gkv = g["w_kv_up"]
    gk = gkv[..., :MLA_HEADS * LANES].reshape(gkv.shape[:-1] + (MLA_HEADS, LANES))[..., :HEAD_DIM]
    gv = gkv[..., MLA_HEADS * LANES:].reshape(gkv.shape[:-1] + (MLA_HEADS, HEAD_DIM))
    gkv = jnp.concatenate([gk, gv], axis=-1).reshape(gkv.shape[:-1] + (MLA_HEADS * 2 * HEAD_DIM,))
    return {
        "w_in": g_in, "w_q_up": gq, "w_kv_up": gkv,
        "w_branch_a": _q_head_unpairs(g["w_branch_a"], g["w_branch_a"].ndim - 2), "w_branch_b": g["w_branch_b"],
        "w_o": g["w_o"], "w_ffn_up": g["w_ffn_up"], "w_ffn_down": g["w_ffn_down"],
    }


def _pack_small(parts):
    flat = jnp.concatenate([p.reshape(-1) for p in parts])
    pad = (-flat.shape[0]) % (SUBLANES * LANES)
    if pad:
        flat = jnp.concatenate([flat, jnp.zeros((pad,), flat.dtype)])
    return flat.reshape(-1, LANES)


def _unpack_small(packed, shapes):
    flat = packed.reshape(-1)
    out, off = [], 0
    for shp in shapes:
        n = int(np.prod(shp))
        out.append(flat[off:off + n].reshape(shp))
        off += n
    return out


def _shards_of(full, axis):
    shp = full.shape
    cut = shp[:axis] + (N_DEV, shp[axis] // N_DEV) + shp[axis + 1:]
    return jnp.moveaxis(full.reshape(cut), axis, 0)


def _from_shards(shards, axis):
    full = list(shards.shape[1:])
    full[axis] *= N_DEV
    return jnp.moveaxis(shards, 0, axis).reshape(full)


def _rows2d(a):
    return a.reshape(-1, a.shape[-1])


def _layer_fwd(x, u, lw, tabs):
    cos_a, sin_a, cos_b, sin_b = tabs
    z = _matmul(u, lw["w_in"], "nn", "mm_in")
    qa, ka, va, cqn, ckvn, krr = _prep_a_fwd(z, lw["gq2"], lw["gk2"], lw["gqa"], lw["gkva"], cos_a, sin_a, cos_b, sin_b)
    qb = _matmul(cqn, lw["w_q_up"], "nn", "mm_q_up")
    kvb = _matmul(ckvn, lw["w_kv_up"], "nn", "mm_kv_up")
    q_b, k_b, v_b = _prep_b_fwd(qb, kvb, krr, cos_b, sin_b)
    ya, lse_a = _attn_fwd(qa, ka, va, True, "gqa_fwd")
    yb, lse_b = _attn_fwd(q_b, k_b, v_b, False, "mla_fwd")
    ta = _matmul(ya, lw["w_branch_a"], "nn", "mm_branch_a")
    tb = _matmul(yb, lw["w_branch_b"], "nn", "mm_branch_b")
    merged = _merge_fwd(z, lw["b_gate"], ta, tb)
    m = _matmul(merged, lw["w_o"], "nn", "mm_o")
    x2, u2 = _res_norm_fwd(x, m, lw["post_mix_g"], lw["pre_ffn_g"])
    h, a = _matmul(u2, lw["w_ffn_up"], "nn", "mm_ffn_up", post="relu2")
    f = _matmul(a, lw["w_ffn_down"], "nn", "mm_ffn_down")
    x3, u_next = _res_norm_fwd(x2, f, lw["post_ffn_g"], lw["next_pre_mix_g"])
    saved = dict(u=u, z=z, qa=qa, ka=ka, va=va, cqn=cqn, ckvn=ckvn, q_b=q_b, k_b=k_b, v_b=v_b, ya=ya, yb=yb,
                 lse_a=lse_a, lse_b=lse_b, ta=ta, tb=tb, merged=merged, m=m, x2=x2, u2=u2, h=h, a=a, f=f, x3=x3)
    return x3, u_next, saved


def _layer_bwd(dx3, du_next, lw, sv, tabs, gbuf, layer):
    cos_a, sin_a, cos_b, sin_b = tabs
    g = {}
    dx3, df, dg4, dg1n = _res_norm_bwd(sv["x3"], sv["f"], lw["post_ffn_g"], lw["next_pre_mix_g"], dx3, du_next)
    g["post_ffn_g"], g["next_pre_mix_g"] = dg4, dg1n
    dh = _matmul(df, lw["w_ffn_down"], "nt", "mm_d_h", post="relu2_bwd", h=sv["h"])
    g["w_ffn_down"] = _matmul(sv["a"], df, "tn", "mm_dw_ffn_down", stack=(gbuf["w_ffn_down"], layer))
    du2 = _matmul(dh, lw["w_ffn_up"], "nt", "mm_d_u2")
    g["w_ffn_up"] = _matmul(sv["u2"], dh, "tn", "mm_dw_ffn_up", stack=(gbuf["w_ffn_up"], layer))
    dx2, dm, dg2, dg3 = _res_norm_bwd(sv["x2"], sv["m"], lw["post_mix_g"], lw["pre_ffn_g"], dx3, du2)
    g["post_mix_g"], g["pre_ffn_g"] = dg2, dg3
    dmg = _matmul(dm, lw["w_o"], "nt", "mm_d_merged")
    g["w_o"] = _matmul(sv["merged"], dm, "tn", "mm_dw_o", stack=(gbuf["w_o"], layer))
    dta, dtb, dzg_a, dzg_b, db_a, db_b = _merge_bwd(dmg, sv["z"], lw["b_gate"], sv["ta"], sv["tb"])
    g["b_gate"] = jnp.concatenate([db_a, db_b], axis=-1)
    dya, delta_a = _matmul(dta, lw["w_branch_a"], "nt", "mm_d_ya", post="delta", h=sv["ya"])
    g["w_branch_a"] = _matmul(sv["ya"], dta, "tn", "mm_dw_branch_a", stack=(gbuf["w_branch_a"], layer))
    dyb, delta_b = _matmul(dtb, lw["w_branch_b"], "nt", "mm_d_yb", post="delta", h=sv["yb"])
    g["w_branch_b"] = _matmul(sv["yb"], dtb, "tn", "mm_dw_branch_b", stack=(gbuf["w_branch_b"], layer))
    dqa, dka4, dva4 = _attn_bwd(sv["qa"], sv["ka"], sv["va"], dya, sv["lse_a"], delta_a, True, "gqa_bwd")
    dq_b, dk_b, dv_b = _attn_bwd(sv["q_b"], sv["k_b"], sv["v_b"], dyb, sv["lse_b"], delta_b, False, "mla_bwd")
    dqb, dkvb, dkr = _prep_b_bwd(dq_b, dk_b, dv_b, cos_b, sin_b)
    dcqn = _matmul(dqb, lw["w_q_up"], "nt", "mm_d_cqn")
    g["w_q_up"] = _matmul(sv["cqn"], dqb, "tn", "mm_dw_q_up", stack=(gbuf["w_q_up"], layer))
    dckvn = _matmul(dkvb, lw["w_kv_up"], "nt", "mm_d_ckvn")
    g["w_kv_up"] = _matmul(sv["ckvn"], dkvb, "tn", "mm_dw_kv_up", stack=(gbuf["w_kv_up"], layer))
    dz, dgq, dgk, dgqa, dgkva = _prep_a_bwd(sv["z"], dqa, dka4, dva4, dcqn, dckvn, dkr, dzg_a, dzg_b, lw["gq2"],
                                            lw["gk2"], lw["gqa"], lw["gkva"], cos_a, sin_a)
    g["q_norm_g"], g["k_norm_g"], g["q_a_norm_g"], g["kv_a_norm_g"] = dgq, dgk, dgqa, dgkva
    du = _matmul(dz, lw["w_in"], "nt", "mm_d_u")
    g["w_in"] = _matmul(sv["u"], dz, "tn", "mm_dw_in", stack=(gbuf["w_in"], layer))
    return dx2, du, g


def kernel(x, w_in, b_gate, q_norm_g, k_norm_g, q_a_norm_g, kv_a_norm_g, w_q_up, w_kv_up, w_branch_a, w_branch_b, w_o, w_ffn_up, w_ffn_down, pre_mix_g, post_mix_g, pre_ffn_g, post_ffn_g, loss_target, m_w_in, m_b_gate, m_q_norm_g, m_k_norm_g, m_q_a_norm_g, m_kv_a_norm_g, m_w_q_up, m_w_kv_up, m_w_branch_a, m_w_branch_b, m_w_o, m_w_ffn_up, m_w_ffn_down, m_pre_mix_g, m_post_mix_g, m_pre_ffn_g, m_post_ffn_g, v_w_in, v_b_gate, v_q_norm_g, v_k_norm_g, v_q_a_norm_g, v_kv_a_norm_g, v_w_q_up, v_w_kv_up, v_w_branch_a, v_w_branch_b, v_w_o, v_w_ffn_up, v_w_ffn_down, v_pre_mix_g, v_post_mix_g, v_pre_ffn_g, v_post_ffn_g):
    weights = dict(zip(WEIGHT_NAMES, (w_in, b_gate, q_norm_g, k_norm_g, q_a_norm_g, kv_a_norm_g, w_q_up, w_kv_up,
                                      w_branch_a, w_branch_b, w_o, w_ffn_up, w_ffn_down, pre_mix_g, post_mix_g,
                                      pre_ffn_g, post_ffn_g)))
    mom_m = dict(zip(WEIGHT_NAMES, (m_w_in, m_b_gate, m_q_norm_g, m_k_norm_g, m_q_a_norm_g, m_kv_a_norm_g, m_w_q_up,
                                    m_w_kv_up, m_w_branch_a, m_w_branch_b, m_w_o, m_w_ffn_up, m_w_ffn_down,
                                    m_pre_mix_g, m_post_mix_g, m_pre_ffn_g, m_post_ffn_g)))
    mom_v = dict(zip(WEIGHT_NAMES, (v_w_in, v_b_gate, v_q_norm_g, v_k_norm_g, v_q_a_norm_g, v_kv_a_norm_g, v_w_q_up,
                                    v_w_kv_up, v_w_branch_a, v_w_branch_b, v_w_o, v_w_ffn_up, v_w_ffn_down,
                                    v_pre_mix_g, v_post_mix_g, v_pre_ffn_g, v_post_ffn_g)))
    assert x.shape[0] == 1 and x.shape[2] == D_MODEL, x.shape
    n_layers = w_in.shape[0]
    t = x.shape[1]
    x0 = x.reshape(t, D_MODEL)
    target = loss_target.reshape(t, D_MODEL)
    shard_shapes = {n: weights[n].shape for n in BIG_NAMES}
    small_shapes = [weights[n].shape for n in SMALL_NAMES]

    gathered = _all_gather([weights[n].astype(BF16) for n in BIG_NAMES])
    full = {n: _from_shards(g, SHARD_AXIS[n]) for n, g in zip(BIG_NAMES, gathered)}
    lw_all = _layout_weights(full)
    lw_all["b_gate"] = b_gate.reshape(n_layers, 1, 2 * D_MODEL)
    lw_all["gq2"] = jnp.tile(q_norm_g, (1, 2)).reshape(n_layers, 1, LANES)
    lw_all["gk2"] = jnp.tile(k_norm_g, (1, 2)).reshape(n_layers, 1, LANES)
    lw_all["gqa"] = q_a_norm_g.reshape(n_layers, 1, MLA_Q_RANK)
    lw_all["gkva"] = kv_a_norm_g.reshape(n_layers, 1, MLA_KV_RANK)
    for n in ("post_mix_g", "pre_ffn_g", "post_ffn_g"):
        lw_all[n] = weights[n]
    lw_all["next_pre_mix_g"] = jnp.roll(pre_mix_g, -1, axis=0)

    tabs = _rope_tables(t)
    u0 = _rms_fwd(x0, pre_mix_g[0])

    layer_w = [{n: a[l] for n, a in lw_all.items()} for l in range(n_layers)]
    xc, uc, saved = x0, u0, []
    for l in range(n_layers):
        xc, uc, sv = _layer_fwd(xc, uc, layer_w[l], tabs)
        saved.append(sv)
    dy, loss_acc = _loss_grad(xc, target)
    loss = lax.psum(0.5 * jnp.sum(loss_acc) / D_MODEL, ("x", "y", "c"))

    dx0, du0, layer_g = dy, jnp.zeros((t, D_MODEL), F32), [None] * n_layers
    gbuf = {n: lax.empty(lw_all[n].shape, F32) for n in BIG_NAMES}
    for l in reversed(range(n_layers)):
        dx0, du0, layer_g[l] = _layer_bwd(dx0, du0, layer_w[l], saved[l], tabs, gbuf, l)
        gbuf = {n: layer_g[l][n] for n in BIG_NAMES}
    grads = {n: jnp.stack([g[n] for g in layer_g]) for n in layer_g[0] if n not in BIG_NAMES}
    grads.update(gbuf)
    grad_x, dg1_first = _rms_bwd(x0, pre_mix_g[0], dx0, du0)

    big_grads = _unlayout_grads({n: grads[n] for n in BIG_NAMES})
    fold = lambda a: a.sum(axis=1)
    dgq = fold(grads["q_norm_g"]).reshape(n_layers, 2, HEAD_DIM).sum(axis=1)
    dgk = fold(grads["k_norm_g"]).reshape(n_layers, 2, HEAD_DIM).sum(axis=1)
    dg1 = jnp.concatenate([fold(dg1_first[None]), fold(grads["next_pre_mix_g"])[:-1]], axis=0)
    small_grads = {
        "b_gate": fold(grads["b_gate"]), "q_norm_g": dgq, "k_norm_g": dgk, "q_a_norm_g": fold(grads["q_a_norm_g"]),
        "kv_a_norm_g": fold(grads["kv_a_norm_g"]), "pre_mix_g": dg1, "post_mix_g": fold(grads["post_mix_g"]),
        "pre_ffn_g": fold(grads["pre_ffn_g"]), "post_ffn_g": fold(grads["post_ffn_g"]),
    }
    small_packed = _pack_small([small_grads[n] for n in SMALL_NAMES])
    sends = [_shards_of(big_grads[n], SHARD_AXIS[n]).reshape((N_DEV,) + _rows2d(weights[n]).shape)
             for n in BIG_NAMES]
    sends.append(jnp.broadcast_to(small_packed[None], (N_DEV,) + small_packed.shape))
    halves = _pair_exchange(sends)
    core = lax.axis_index("c").astype(jnp.int32).reshape(1)
    recvs = _chip_exchange([_pair_add(s, h, core) for s, h in zip(sends, halves)])

    results = {}
    for n, recv in zip(BIG_NAMES, recvs):
        res = _adamw(recv, _rows2d(weights[n]), _rows2d(mom_m[n]), _rows2d(mom_v[n]))
        results[n] = [r.reshape(shard_shapes[n]) for r in res]
    res = _adamw(recvs[-1], *[_pack_small([d[n] for n in SMALL_NAMES]) for d in (weights, mom_m, mom_v)])
    for kind, packed_out in enumerate(res):
        for n, val in zip(SMALL_NAMES, _unpack_small(packed_out, small_shapes)):
            results.setdefault(n, [None] * 4)[kind] = val
    outs = [results[n][kind] for kind in range(4) for n in WEIGHT_NAMES]
    return (loss, grad_x.reshape(x.shape), *outs)
```

```python
import math

import jax
import jax.numpy as jnp
import numpy as np
from jax import lax
from jax.experimental import pallas as pl
from jax.experimental.pallas import tpu as pltpu

F32 = jnp.float32
BF16 = jnp.bfloat16

D_MODEL = 1024
GRID_W = 64
ROPE_THETA = 10000.0
EPS = 1e-6
GQA_HEADS = 8
GQA_KV_HEADS = 2
GQA_GROUP = GQA_HEADS // GQA_KV_HEADS
HEAD_DIM = 64
MLA_HEADS = 8
MLA_ROPE_DIM = 32
MLA_QK_DIM = 96
MLA_Q_RANK = 384
MLA_KV_RANK = 256
GQA_SCALE = 1.0 / math.sqrt(HEAD_DIM)
MLA_SCALE = 1.0 / math.sqrt(MLA_QK_DIM)
LOG2E = math.log2(math.e)
LN2 = math.log(2.0)

ADAM_LR = 0.001
ADAM_B1 = 0.9
ADAM_B2 = 0.999
ADAM_EPS = 1e-08
ADAM_WD = 0.01
ADAM_STEP = 10

N_DEV = 8
LANES = 128
SUBLANES = 8
VMEM_LIMIT = 48 * 1024 * 1024

Z_QA, Z_KA, Z_VA, Z_CQ, Z_CKV, Z_KR, Z_GATE = 0, 512, 640, 768, 1152, 1408, 1536
Z_ATT_W = 1536
Z_W = 3584
KR_LANE0 = 64

WEIGHT_NAMES = ("w_in", "b_gate", "q_norm_g", "k_norm_g", "q_a_norm_g", "kv_a_norm_g", "w_q_up", "w_kv_up",
                "w_branch_a", "w_branch_b", "w_o", "w_ffn_up", "w_ffn_down", "pre_mix_g", "post_mix_g",
                "pre_ffn_g", "post_ffn_g")
SHARD_AXIS = {"w_in": 2, "w_q_up": 2, "w_kv_up": 2, "w_branch_a": 2, "w_branch_b": 2, "w_o": 1, "w_ffn_up": 2,
              "w_ffn_down": 1}
BIG_NAMES = tuple(n for n in WEIGHT_NAMES if n in SHARD_AXIS)
SMALL_NAMES = tuple(n for n in WEIGHT_NAMES if n not in SHARD_AXIS)
ADAM_BLOCK_ELEMS = 256 * 1024
MM_TILE = 1024
MM_TILE_TOKENS = 2048
MM_TILE_K = 2048
PREP_ROWS = 512
ROW_TILE = 1024
ATTN_TQ = 1024
ATTN_TK = 1024


def _params(*semantics):
    return pltpu.CompilerParams(dimension_semantics=semantics, vmem_limit_bytes=VMEM_LIMIT)


def _tile(n, pref):
    if n <= pref:
        return n
    t = (pref // LANES) * LANES
    while n % t:
        t -= LANES
    return t


def _fold8(t):
    return t.reshape(t.shape[0] // SUBLANES, SUBLANES, t.shape[1]).sum(axis=0)


_DIMS = {"nn": ((1,), (0,)), "nt": ((1,), (1,)), "tn": ((0,), (0,))}


def _matmul(a, b, mode, name, post=None, h=None, stack=None):
    out_dt = F32 if mode == "tn" else BF16
    if mode == "nn":
        (m, k), n = a.shape, b.shape[1]
    elif mode == "nt":
        (m, k), n = a.shape, b.shape[0]
    else:
        (k, m), n = a.shape, b.shape[1]
    tm = _tile(m, MM_TILE_TOKENS if mode != "tn" and k <= MM_TILE else MM_TILE)
    tn, tk = _tile(n, MM_TILE), _tile(k, MM_TILE_K)
    nk = k // tk
    dims = (_DIMS[mode], ((), ()))
    operands = [a, b] + ([h] if post in ("relu2_bwd", "delta") else []) + ([stack[0]] if stack else [])
    n_in = len(operands)
    n_out = 2 if post in ("relu2", "delta") else 1
    assert post != "delta" or tn == n, (n, tn)

    def body(*refs):
        a_ref, b_ref = refs[:2]
        o_refs, acc_ref = refs[n_in:n_in + n_out], refs[-1]

        def finish(val):
            if post == "relu2":
                o_refs[0][...] = val.astype(out_dt)
                r = jnp.maximum(val, 0.0)
                o_refs[1][...] = (r * r).astype(BF16)
            elif post == "relu2_bwd":
                o_refs[0][...] = (val * (2.0 * jnp.maximum(refs[2][...].astype(F32), 0.0))).astype(BF16)
            elif post == "delta":
                do = val.astype(BF16)
                o_refs[0][...] = do
                prod = do.astype(F32) * refs[2][...].astype(F32)
                for g in range(n // LANES):
                    x = prod[:, LANES * g:LANES * (g + 1)]
                    lo = _lo_mask(x.shape)
                    d0 = jnp.sum(jnp.where(lo, x, 0.0), axis=-1, keepdims=True)
                    d1 = jnp.sum(jnp.where(lo, 0.0, x), axis=-1, keepdims=True)
                    o_refs[1][2 * g] = jnp.broadcast_to(d0, (tm, LANES))
                    o_refs[1][2 * g + 1] = jnp.broadcast_to(d1, (tm, LANES))
            else:
                o_refs[0][...] = val.astype(out_dt)

        prod = lax.dot_general(a_ref[...], b_ref[...], dims, preferred_element_type=F32)
        if nk == 1:
            finish(prod)
        else:
            kk = pl.program_id(2)

            @pl.when(kk == 0)
            def _():
                acc_ref[...] = prod

            @pl.when(kk > 0)
            def _():
                acc_ref[...] += prod

            @pl.when(kk == nk - 1)
            def _():
                finish(acc_ref[...])

    if mode == "tn":
        a_spec = pl.BlockSpec((tk, tm), lambda i, j, kk: (kk, i))
    else:
        a_spec = pl.BlockSpec((tm, tk), lambda i, j, kk: (i, kk))
    if mode == "nt":
        b_spec = pl.BlockSpec((tn, tk), lambda i, j, kk: (j, kk))
    else:
        b_spec = pl.BlockSpec((tk, tn), lambda i, j, kk: (kk, j))
    o_spec = pl.BlockSpec((tm, tn), lambda i, j, kk: (i, j))
    main_out, bf16_out = jax.ShapeDtypeStruct((m, n), out_dt), jax.ShapeDtypeStruct((m, n), BF16)
    heads = n // HEAD_DIM
    delta_out = jax.ShapeDtypeStruct((heads, m, LANES), F32)
    out_shape = {None: main_out, "relu2": [main_out, bf16_out], "relu2_bwd": bf16_out,
                 "delta": [bf16_out, delta_out]}[post]
    in_specs = [a_spec, b_spec] + ([o_spec] if post in ("relu2_bwd", "delta") else [])
    out_specs = {None: o_spec, "relu2": [o_spec, o_spec], "relu2_bwd": o_spec,
                 "delta": [o_spec, pl.BlockSpec((heads, tm, LANES), lambda i, j, kk: (0, i, 0))]}[post]
    aliases = {}
    if stack:
        buf, layer = stack
        assert post is None and buf.shape[1:] == (m, n) and buf.dtype == out_dt, (buf.shape, buf.dtype)
        in_specs.append(pl.BlockSpec(memory_space=pl.ANY))
        out_specs = pl.BlockSpec((None, tm, tn), lambda i, j, kk: (layer, i, j))
        out_shape = jax.ShapeDtypeStruct(buf.shape, buf.dtype)
        aliases = {n_in - 1: 0}
    return pl.pallas_call(
        body,
        name=name,
        grid=(m // tm, n // tn, nk),
        in_specs=in_specs,
        out_specs=out_specs,
        out_shape=out_shape,
        scratch_shapes=[pltpu.VMEM((tm, tn), F32)],
        input_output_aliases=aliases,
        compiler_params=_params("parallel", "parallel", "arbitrary"),
    )(*operands)


def _rinv(x):
    return lax.rsqrt(jnp.mean(x * x, axis=-1, keepdims=True) + EPS)


def _rms_bwd_rows(x, g, dy):
    r = _rinv(x)
    xh = x * r
    dxh = dy * g
    dx = r * (dxh - xh * jnp.mean(dxh * xh, axis=-1, keepdims=True))
    return dx, dy * xh


def _row_spec(tm, c):
    return pl.BlockSpec((tm, c), lambda i: (i, 0))


def _vec_spec(c):
    return pl.BlockSpec((1, c), lambda i: (0, 0))


def _acc_spec(c):
    return pl.BlockSpec((SUBLANES, c), lambda i: (0, 0))


def _rms_fwd(x, g):
    t, d = x.shape
    tm = _tile(t, ROW_TILE)

    def body(x_ref, g_ref, o_ref):
        xv = x_ref[...]
        o_ref[...] = (xv * _rinv(xv) * g_ref[...]).astype(BF16)

    return pl.pallas_call(
        body, name="rms_fwd", grid=(t // tm,),
        in_specs=[_row_spec(tm, d), _vec_spec(d)], out_specs=_row_spec(tm, d),
        out_shape=jax.ShapeDtypeStruct((t, d), BF16), compiler_params=_params("parallel"),
    )(x, g.reshape(1, d))


def _rms_bwd(x, g, dres, dy):
    t, d = x.shape
    tm = _tile(t, ROW_TILE)

    def body(x_ref, g_ref, dres_ref, dy_ref, dx_ref, dg_ref):
        dx, dgc = _rms_bwd_rows(x_ref[...], g_ref[...], dy_ref[...].astype(F32))
        dx_ref[...] = dres_ref[...] + dx

        @pl.when(pl.program_id(0) == 0)
        def _():
            dg_ref[...] = jnp.zeros_like(dg_ref)

        dg_ref[...] += _fold8(dgc)

    return pl.pallas_call(
        body, name="rms_bwd", grid=(t // tm,),
        in_specs=[_row_spec(tm, d), _vec_spec(d), _row_spec(tm, d), _row_spec(tm, d)],
        out_specs=[_row_spec(tm, d), _acc_spec(d)],
        out_shape=[jax.ShapeDtypeStruct((t, d), F32), jax.ShapeDtypeStruct((SUBLANES, d), F32)],
        compiler_params=_params("arbitrary"),
    )(x, g.reshape(1, d), dres, dy)


def _res_norm_fwd(x, m, g_post, g_next):
    t, d = x.shape
    tm = _tile(t, ROW_TILE)

    def body(x_ref, m_ref, gp_ref, gn_ref, x2_ref, u2_ref):
        mv = m_ref[...].astype(F32)
        x2 = x_ref[...] + mv * _rinv(mv) * gp_ref[...]
        x2_ref[...] = x2
        u2_ref[...] = (x2 * _rinv(x2) * gn_ref[...]).astype(BF16)

    return pl.pallas_call(
        body, name="res_norm_fwd", grid=(t // tm,),
        in_specs=[_row_spec(tm, d), _row_spec(tm, d), _vec_spec(d), _vec_spec(d)],
        out_specs=[_row_spec(tm, d), _row_spec(tm, d)],
        out_shape=[jax.ShapeDtypeStruct((t, d), F32), jax.ShapeDtypeStruct((t, d), BF16)],
        compiler_params=_params("parallel"),
    )(x, m, g_post.reshape(1, d), g_next.reshape(1, d))


def _res_norm_bwd(x2, m, g_post, g_next, dx2_in, du2):
    t, d = x2.shape
    tm = _tile(t, ROW_TILE // 2)

    def body(x2_ref, m_ref, gp_ref, gn_ref, dx2in_ref, du2_ref, dx2_ref, dm_ref, dgp_ref, dgn_ref):
        dxn, dgn_c = _rms_bwd_rows(x2_ref[...], gn_ref[...], du2_ref[...].astype(F32))
        dx2 = dx2in_ref[...] + dxn
        dx2_ref[...] = dx2
        dm, dgp_c = _rms_bwd_rows(m_ref[...].astype(F32), gp_ref[...], dx2)
        dm_ref[...] = dm.astype(BF16)

        @pl.when(pl.program_id(0) == 0)
        def _():
            dgp_ref[...] = jnp.zeros_like(dgp_ref)
            dgn_ref[...] = jnp.zeros_like(dgn_ref)

        dgp_ref[...] += _fold8(dgp_c)
        dgn_ref[...] += _fold8(dgn_c)

    return pl.pallas_call(
        body, name="res_norm_bwd", grid=(t // tm,),
        in_specs=[_row_spec(tm, d), _row_spec(tm, d), _vec_spec(d), _vec_spec(d), _row_spec(tm, d), _row_spec(tm, d)],
        out_specs=[_row_spec(tm, d), _row_spec(tm, d), _acc_spec(d), _acc_spec(d)],
        out_shape=[jax.ShapeDtypeStruct((t, d), F32), jax.ShapeDtypeStruct((t, d), BF16),
                   jax.ShapeDtypeStruct((SUBLANES, d), F32), jax.ShapeDtypeStruct((SUBLANES, d), F32)],
        compiler_params=_params("arbitrary"),
    )(x2, m, g_post.reshape(1, d), g_next.reshape(1, d), dx2_in, du2)


def _rope_tables(t):
    rows = t // GRID_W
    row = jnp.repeat(jnp.arange(rows, dtype=F32), GRID_W)
    col = jnp.tile(jnp.arange(GRID_W, dtype=F32), rows)

    def tab(rot_dim):
        half = rot_dim // 2
        inv = ROPE_THETA ** (-jnp.arange(0, half, 2, dtype=F32) / half)
        ar = row[:, None] * inv[None, :]
        ac = col[:, None] * inv[None, :]
        ang = jnp.concatenate([ar, ar, ac, ac], axis=-1)
        q = half // 2
        sign = np.tile(np.concatenate([-np.ones(q, np.float32), np.ones(q, np.float32)]), 2)
        return jnp.cos(ang), jnp.sin(ang) * sign[None, :]

    ca, sa = tab(HEAD_DIM)
    cb, sb = tab(MLA_ROPE_DIM)
    one = jnp.ones((t, 1), F32)
    cos_b = jnp.concatenate([one * jnp.ones((1, KR_LANE0), F32), cb, one * jnp.ones((1, 32), F32)], axis=-1)
    sin_b = jnp.concatenate([jnp.zeros((t, KR_LANE0), F32), sb, jnp.zeros((t, 32), F32)], axis=-1)
    return jnp.tile(ca, (1, GQA_HEADS)), jnp.tile(sa, (1, GQA_HEADS)), cos_b, sin_b


def _swap_halves(x, sh):
    lane = lax.broadcasted_iota(jnp.int32, x.shape, 1)
    up = pltpu.roll(x, LANES - sh, 1)
    dn = pltpu.roll(x, sh, 1)
    return jnp.where((lane & (2 * sh - 1)) < sh, up, dn)


def _rope(x, cos, sin_s, sh):
    return x * cos + _swap_halves(x, sh) * sin_s


def _rope_bwd(dy, cos, sin_s, sh):
    return dy * cos + _swap_halves(dy * sin_s, sh)


def _lo_mask(shape):
    return lax.broadcasted_iota(jnp.int32, shape, 1) < HEAD_DIM


def _half_mean(t, lo):
    s_lo = jnp.sum(jnp.where(lo, t, 0.0), axis=-1, keepdims=True)
    s_hi = jnp.sum(jnp.where(lo, 0.0, t), axis=-1, keepdims=True)
    return jnp.where(lo, s_lo, s_hi) * (1.0 / HEAD_DIM)


def _head_norm(x, g2):
    lo = _lo_mask(x.shape)
    r = lax.rsqrt(_half_mean(x * x, lo) + EPS)
    return x * r * g2


def _head_norm_bwd(x, g2, dy):
    lo = _lo_mask(x.shape)
    r = lax.rsqrt(_half_mean(x * x, lo) + EPS)
    xh = x * r
    dxh = dy * g2
    dx = r * (dxh - xh * _half_mean(dxh * xh, lo))
    return dx, dy * xh


def _prep_a_fwd(z, gq2, gk2, gqa, gkva, cos_a, sin_a, cos_b, sin_b):
    t = z.shape[0]
    tm = _tile(t, PREP_ROWS)

    def body(z_ref, gq_ref, gk_ref, gqa_ref, gkva_ref, ca_ref, sa_ref, cb_ref, sb_ref,
             qa_ref, ka_ref, va_ref, cqn_ref, ckvn_ref, krr_ref):
        def zf(lo, hi):
            return z_ref[:, lo:hi].astype(F32)

        for j in range(4):
            cols = slice(LANES * j, LANES * (j + 1))
            y = _rope(_head_norm(zf(LANES * j, LANES * (j + 1)), gq_ref[...]), ca_ref[:, cols], sa_ref[:, cols], 16)
            qa_ref[:, cols] = (y * (GQA_SCALE * LOG2E)).astype(BF16)
        y = _rope(_head_norm(zf(Z_KA, Z_VA), gk_ref[...]), ca_ref[:, :LANES], sa_ref[:, :LANES], 16)
        ka_ref[...] = y.astype(BF16)
        va_ref[...] = z_ref[:, Z_VA:Z_CQ].astype(BF16)
        cq = zf(Z_CQ, Z_CKV)
        cqn_ref[...] = (cq * _rinv(cq) * gqa_ref[...]).astype(BF16)
        ckv = zf(Z_CKV, Z_KR)
        ckvn_ref[...] = (ckv * _rinv(ckv) * gkva_ref[...]).astype(BF16)
        krr_ref[...] = _rope(zf(Z_KR, Z_GATE), cb_ref[...], sb_ref[...], 8)

    return pl.pallas_call(
        body, name="prep_a_fwd", grid=(t // tm,),
        in_specs=[_row_spec(tm, Z_ATT_W), _vec_spec(LANES), _vec_spec(LANES), _vec_spec(MLA_Q_RANK),
                  _vec_spec(MLA_KV_RANK), _row_spec(tm, 512), _row_spec(tm, 512), _row_spec(tm, LANES),
                  _row_spec(tm, LANES)],
        out_specs=[_row_spec(tm, 512), _row_spec(tm, LANES), _row_spec(tm, LANES), _row_spec(tm, MLA_Q_RANK),
                   _row_spec(tm, MLA_KV_RANK), _row_spec(tm, LANES)],
        out_shape=[jax.ShapeDtypeStruct((t, 512), BF16), jax.ShapeDtypeStruct((t, LANES), BF16),
                   jax.ShapeDtypeStruct((t, LANES), BF16), jax.ShapeDtypeStruct((t, MLA_Q_RANK), BF16),
                   jax.ShapeDtypeStruct((t, MLA_KV_RANK), BF16), jax.ShapeDtypeStruct((t, LANES), F32)],
        compiler_params=_params("parallel"),
    )(z, gq2, gk2, gqa, gkva, cos_a, sin_a, cos_b, sin_b)


def _prep_a_bwd(z, dqa, dka4, dva4, dcqn, dckvn, dkr, dzga, dzgb, gq2, gk2, gqa, gkva, cos_a, sin_a):
    t = z.shape[0]
    tm = _tile(t, PREP_ROWS)

    def body(z_ref, dqa_ref, dka_ref, dva_ref, dcqn_ref, dckvn_ref, dkr_ref, dzga_ref, dzgb_ref, gq_ref, gk_ref,
             gqa_ref, gkva_ref, ca_ref, sa_ref, dz_ref, dgq_ref, dgk_ref, dgqa_ref, dgkva_ref):
        @pl.when(pl.program_id(0) == 0)
        def _():
            dgq_ref[...] = jnp.zeros_like(dgq_ref)
            dgk_ref[...] = jnp.zeros_like(dgk_ref)
            dgqa_ref[...] = jnp.zeros_like(dgqa_ref)
            dgkva_ref[...] = jnp.zeros_like(dgkva_ref)

        def zf(lo, hi):
            return z_ref[:, lo:hi].astype(F32)

        dgq = jnp.zeros((SUBLANES, LANES), F32)
        for j in range(4):
            cols = slice(LANES * j, LANES * (j + 1))
            dy = _rope_bwd(dqa_ref[:, cols] * GQA_SCALE, ca_ref[:, cols], sa_ref[:, cols], 16)
            dx, dgc = _head_norm_bwd(zf(LANES * j, LANES * (j + 1)), gq_ref[...], dy)
            dz_ref[:, cols] = dx.astype(BF16)
            dgq = dgq + _fold8(dgc)
        dgq_ref[...] += dgq
        dk = (dka_ref[0] + dka_ref[1] + dka_ref[2] + dka_ref[3]).T * LN2
        dy = _rope_bwd(dk, ca_ref[:, :LANES], sa_ref[:, :LANES], 16)
        dx, dgc = _head_norm_bwd(zf(Z_KA, Z_VA), gk_ref[...], dy)
        dz_ref[:, Z_KA:Z_VA] = dx.astype(BF16)
        dgk_ref[...] += _fold8(dgc)
        dz_ref[:, Z_VA:Z_CQ] = (dva_ref[0] + dva_ref[1] + dva_ref[2] + dva_ref[3]).T.astype(BF16)
        dx, dgc = _rms_bwd_rows(zf(Z_CQ, Z_CKV), gqa_ref[...], dcqn_ref[...].astype(F32))
        dz_ref[:, Z_CQ:Z_CKV] = dx.astype(BF16)
        dgqa_ref[...] += _fold8(dgc)
        dx, dgc = _rms_bwd_rows(zf(Z_CKV, Z_KR), gkva_ref[...], dckvn_ref[...].astype(F32))
        dz_ref[:, Z_CKV:Z_KR] = dx.astype(BF16)
        dgkva_ref[...] += _fold8(dgc)
        dz_ref[:, Z_KR:Z_GATE] = dkr_ref[...].astype(BF16)
        dz_ref[:, Z_GATE:Z_GATE + D_MODEL] = dzga_ref[...]
        dz_ref[:, Z_GATE + D_MODEL:Z_W] = dzgb_ref[...]

    part = pl.BlockSpec((4, LANES, tm), lambda i: (0, 0, i))
    return pl.pallas_call(
        body, name="prep_a_bwd", grid=(t // tm,),
        in_specs=[_row_spec(tm, Z_ATT_W), _row_spec(tm, 512), part, part, _row_spec(tm, MLA_Q_RANK),
                  _row_spec(tm, MLA_KV_RANK), _row_spec(tm, LANES), _row_spec(tm, D_MODEL), _row_spec(tm, D_MODEL),
                  _vec_spec(LANES),
                  _vec_spec(LANES), _vec_spec(MLA_Q_RANK), _vec_spec(MLA_KV_RANK), _row_spec(tm, 512),
                  _row_spec(tm, 512)],
        out_specs=[_row_spec(tm, Z_W), _acc_spec(LANES), _acc_spec(LANES), _acc_spec(MLA_Q_RANK),
                   _acc_spec(MLA_KV_RANK)],
        out_shape=[jax.ShapeDtypeStruct((t, Z_W), BF16), jax.ShapeDtypeStruct((SUBLANES, LANES), F32),
                   jax.ShapeDtypeStruct((SUBLANES, LANES), F32), jax.ShapeDtypeStruct((SUBLANES, MLA_Q_RANK), F32),
                   jax.ShapeDtypeStruct((SUBLANES, MLA_KV_RANK), F32)],
        compiler_params=_params("arbitrary"),
    )(z, dqa, dka4, dva4, dcqn, dckvn, dkr, dzga, dzgb, gq2, gk2, gqa, gkva, cos_a, sin_a)


def _prep_b_fwd(qb, kvb, krr, cos_b, sin_b):
    t = qb.shape[0]
    tm = _tile(t, PREP_ROWS)

    def body(qb_ref, kvb_ref, krr_ref, cb_ref, sb_ref, q_ref, k_ref, v_ref):
        for h in range(MLA_HEADS):
            cols = slice(LANES * h, LANES * (h + 1))
            qh = _rope(qb_ref[:, cols].astype(F32), cb_ref[...], sb_ref[...], 8)
            q_ref[:, cols] = (qh * (MLA_SCALE * LOG2E)).astype(BF16)
            k_ref[:, cols] = (kvb_ref[:, cols].astype(F32) + krr_ref[...]).astype(BF16)
        v_ref[...] = kvb_ref[:, 1024:1536].astype(BF16)

    return pl.pallas_call(
        body, name="prep_b_fwd", grid=(t // tm,),
        in_specs=[_row_spec(tm, 1024), _row_spec(tm, 1536), _row_spec(tm, LANES), _row_spec(tm, LANES),
                  _row_spec(tm, LANES)],
        out_specs=[_row_spec(tm, 1024), _row_spec(tm, 1024), _row_spec(tm, 512)],
        out_shape=[jax.ShapeDtypeStruct((t, 1024), BF16), jax.ShapeDtypeStruct((t, 1024), BF16),
                   jax.ShapeDtypeStruct((t, 512), BF16)],
        compiler_params=_params("parallel"),
    )(qb, kvb, krr, cos_b, sin_b)


def _prep_b_bwd(dq, dk, dv, cos_b, sin_b):
    t = dq.shape[0]
    tm = _tile(t, PREP_ROWS)

    def body(dq_ref, dk_ref, dv_ref, cb_ref, sb_ref, dqb_ref, dkvb_ref, dkr_ref):
        dkr = jnp.zeros((tm, LANES), F32)
        for h in range(MLA_HEADS):
            cols = slice(LANES * h, LANES * (h + 1))
            dqb_ref[:, cols] = _rope_bwd(dq_ref[:, cols] * MLA_SCALE, cb_ref[...], sb_ref[...], 8).astype(BF16)
            dkh = dk_ref[cols, :].T * LN2
            dkvb_ref[:, cols] = dkh.astype(BF16)
            dkr = dkr + dkh
        for j in range(MLA_HEADS // 2):
            dkvb_ref[:, 1024 + LANES * j:1024 + LANES * (j + 1)] = dv_ref[LANES * j:LANES * (j + 1), :].T.astype(BF16)
        dkr_ref[...] = _rope_bwd(dkr, cb_ref[...], sb_ref[...], 8)

    return pl.pallas_call(
        body, name="prep_b_bwd", grid=(t // tm,),
        in_specs=[_row_spec(tm, 1024), pl.BlockSpec((1024, tm), lambda i: (0, i)),
                  pl.BlockSpec((512, tm), lambda i: (0, i)), _row_spec(tm, LANES),
                  _row_spec(tm, LANES)],
        out_specs=[_row_spec(tm, 1024), _row_spec(tm, 1536), _row_spec(tm, LANES)],
        out_shape=[jax.ShapeDtypeStruct((t, 1024), BF16), jax.ShapeDtypeStruct((t, 1536), BF16),
                   jax.ShapeDtypeStruct((t, LANES), F32)],
        compiler_params=_params("parallel"),
    )(dq, dk, dv, cos_b, sin_b)


_NT = (((1,), (1,)), ((), ()))
_NN = (((1,), (0,)), ((), ()))
_TN = (((0,), (0,)), ((), ()))


def _head_operands(qv, kv, i, shared_k):
    if shared_k:
        lo = _lo_mask(qv.shape)
        keep = lo if i == 0 else jnp.logical_not(lo)
        return jnp.where(keep, qv, jnp.zeros_like(qv)), kv
    cols = slice(LANES * i, LANES * (i + 1))
    return qv[:, cols], kv[:, cols]


def _attn_specs(shared_k, tq, tk, q_of, k_of):
    wq = LANES if shared_k else 2 * LANES
    q_spec = pl.BlockSpec((tq, wq), lambda *g: (q_of(*g), g[0]))
    if shared_k:
        k_spec = pl.BlockSpec((tk, LANES), lambda *g: (k_of(*g), 0))
        v_spec = pl.BlockSpec((tk, LANES), lambda *g: (k_of(*g), 0))
    else:
        k_spec = pl.BlockSpec((tk, wq), lambda *g: (k_of(*g), g[0]))
        v_spec = pl.BlockSpec((tk, LANES), lambda *g: (k_of(*g), g[0]))
    return wq, q_spec, k_spec, v_spec


def _attn_fwd(q, k, v, shared_k, name):
    t = q.shape[0]
    tq, tk = _tile(t, ATTN_TQ), _tile(t, ATTN_TK)
    nq, nk = t // tq, t // tk
    wq, q_spec, k_spec, v_spec = _attn_specs(shared_k, tq, tk, lambda p, i, j: i, lambda p, i, j: j)
    groups = q.shape[1] // wq
    chunk = _tile(tq, 2 * LANES)

    def body(q_ref, k_ref, v_ref, o_ref, lse_ref, m_s, acc_s, alpha_s, s_s, p_s):
        kb = pl.program_id(2)

        @pl.when(kb == 0)
        def _():
            m_s[...] = jnp.full_like(m_s, -jnp.inf)
            acc_s[...] = jnp.zeros_like(acc_s)

        qv, kv, vv = q_ref[...], k_ref[...], v_ref[...]
        lo = _lo_mask(vv.shape)
        for i in range(2):
            qi, ki = _head_operands(qv, kv, i, shared_k)
            s_s[i] = lax.dot_general(ki, qi, _NT, preferred_element_type=F32)
        for i in range(2):
            for c in range(tq // chunk):
                cols = slice(c * chunk, (c + 1) * chunk)
                m_prev = m_s[i, :, cols]
                m_new = jnp.maximum(m_prev, jnp.max(s_s[i, :, cols], axis=0, keepdims=True))
                alpha_s[i, :, cols] = jnp.exp2(m_prev - m_new)
                m_s[i, :, cols] = m_new
                p_s[i, :, cols] = jnp.exp2(s_s[i, :, cols] - m_new).astype(BF16)
        for i in range(2):
            keep = lo if i == 0 else jnp.logical_not(lo)
            vi = jnp.where(keep, vv, jnp.ones_like(vv))
            acc_s[i] = alpha_s[i] * acc_s[i] + lax.dot_general(vi, p_s[i], _TN, preferred_element_type=F32)

        @pl.when(kb == nk - 1)
        def _():
            a0, a1 = acc_s[0], acc_s[1]
            l0 = a0[LANES - SUBLANES:, :][0:1, :]
            l1 = a1[0:SUBLANES, :][0:1, :]
            row_lo = lax.broadcasted_iota(jnp.int32, a0.shape, 0) < HEAD_DIM
            o_ref[...] = jnp.where(row_lo, a0 / l0, a1 / l1).T.astype(BF16)
            lse_ref[0] = jnp.broadcast_to(m_s[0] + jnp.log2(l0), (LANES, tq)).T
            lse_ref[1] = jnp.broadcast_to(m_s[1] + jnp.log2(l1), (LANES, tq)).T

    return pl.pallas_call(
        body, name=name, grid=(groups, nq, nk),
        in_specs=[q_spec, k_spec, v_spec],
        out_specs=[pl.BlockSpec((tq, LANES), lambda p, i, j: (i, p)),
                   pl.BlockSpec((2, tq, LANES), lambda p, i, j: (p, i, 0))],
        out_shape=[jax.ShapeDtypeStruct((t, LANES * groups), BF16),
                   jax.ShapeDtypeStruct((2 * groups, t, LANES), F32)],
        scratch_shapes=[pltpu.VMEM((2, 1, tq), F32), pltpu.VMEM((2, LANES, tq), F32), pltpu.VMEM((2, 1, tq), F32),
                        pltpu.VMEM((2, tk, tq), F32), pltpu.VMEM((2, tk, tq), BF16)],
        compiler_params=_params("parallel", "parallel", "arbitrary"),
    )(q, k, v)


def _attn_bwd(q, k, v, do, lse, delta, shared_k, name):
    t = q.shape[0]
    tq, tk = _tile(t, ATTN_TQ), _tile(t, ATTN_TK)
    nq, nk = t // tq, t // tk
    wq, q_spec, k_spec, v_spec = _attn_specs(shared_k, tq, tk, lambda p, j, i: i, lambda p, j, i: j)
    groups = q.shape[1] // wq

    def body(q_ref, k_ref, v_ref, do_ref, lse_ref, delta_ref, dq_ref, dk_ref, dv_ref, dk_s, dv_s, s_s, dp_s, p_s,
             ds_s):
        kb, qb = pl.program_id(1), pl.program_id(2)

        @pl.when(qb == 0)
        def _():
            dk_s[...] = jnp.zeros_like(dk_s)
            dv_s[...] = jnp.zeros_like(dv_s)

        qv, kv, vv, dov = q_ref[...], k_ref[...], v_ref[...], do_ref[...]
        lo = _lo_mask(dov.shape)
        heads = []
        for i in range(2):
            qi, ki = _head_operands(qv, kv, i, shared_k)
            keep = lo if i == 0 else jnp.logical_not(lo)
            doi = jnp.where(keep, dov, jnp.zeros_like(dov))
            heads.append((qi, ki, doi))
            s_s[i] = lax.dot_general(qi, ki, _NT, preferred_element_type=F32)
            dp_s[i] = lax.dot_general(doi, vv, _NT, preferred_element_type=F32)
        for i in range(2):
            lse_i, delta_i = lse_ref[i], delta_ref[i]
            for c in range(tk // LANES):
                cols = slice(c * LANES, (c + 1) * LANES)
                p = jnp.exp2(s_s[i, :, cols] - lse_i)
                p_s[i, :, cols] = p.astype(BF16)
                ds_s[i, :, cols] = (p * (dp_s[i, :, cols] - delta_i)).astype(BF16)
        dq_parts = []
        for i in range(2):
            qi, ki, doi = heads[i]
            dv_s[...] += lax.dot_general(doi, p_s[i], _TN, preferred_element_type=F32)
            dk_i = lax.dot_general(qi, ds_s[i], _TN, preferred_element_type=F32)
            if shared_k:
                dk_s[...] += dk_i
            else:
                dk_s[LANES * i:LANES * (i + 1), :] += dk_i
            dq_parts.append(lax.dot_general(ds_s[i], ki, _NN, preferred_element_type=F32))
        rows = pl.ds(pl.multiple_of(qb * tq, tq), tq)
        if shared_k:
            tiles = [(slice(0, LANES), jnp.where(lo, dq_parts[0], dq_parts[1]))]
        else:
            tiles = [(slice(0, LANES), dq_parts[0]), (slice(LANES, 2 * LANES), dq_parts[1])]
        for cols, val in tiles:
            @pl.when(kb == 0)
            def _(cols=cols, val=val):
                dq_ref[rows, cols] = val

            @pl.when(kb > 0)
            def _(cols=cols, val=val):
                dq_ref[rows, cols] += val

        @pl.when(qb == nq - 1)
        def _():
            if shared_k:
                dk_ref[0] = dk_s[...]
                dv_ref[0] = dv_s[...]
            else:
                dk_ref[...] = dk_s[...]
                dv_ref[...] = dv_s[...]

    stat_spec = pl.BlockSpec((2, tq, LANES), lambda p, j, i: (p, i, 0))
    do_spec = pl.BlockSpec((tq, LANES), lambda p, j, i: (i, p))
    dq_spec = pl.BlockSpec((t, wq), lambda p, j, i: (0, p))
    if shared_k:
        dk_spec = pl.BlockSpec((1, LANES, tk), lambda p, j, i: (p, 0, j))
        dv_spec = dk_spec
        dk_shape = jax.ShapeDtypeStruct((groups, LANES, t), F32)
        dv_shape = dk_shape
    else:
        dk_spec = pl.BlockSpec((wq, tk), lambda p, j, i: (p, j))
        dv_spec = pl.BlockSpec((LANES, tk), lambda p, j, i: (p, j))
        dk_shape = jax.ShapeDtypeStruct((wq * groups, t), F32)
        dv_shape = jax.ShapeDtypeStruct((LANES * groups, t), F32)
    return pl.pallas_call(
        body, name=name, grid=(groups, nk, nq),
        in_specs=[q_spec, k_spec, v_spec, do_spec, stat_spec, stat_spec],
        out_specs=[dq_spec, dk_spec, dv_spec],
        out_shape=[jax.ShapeDtypeStruct((t, wq * groups), F32), dk_shape, dv_shape],
        scratch_shapes=[pltpu.VMEM((wq, tk), F32), pltpu.VMEM((LANES, tk), F32), pltpu.VMEM((2, tq, tk), F32),
                        pltpu.VMEM((2, tq, tk), F32), pltpu.VMEM((2, tq, tk), BF16), pltpu.VMEM((2, tq, tk), BF16)],
        compiler_params=_params("parallel", "arbitrary", "arbitrary"),
    )(q, k, v, do, lse, delta)


_MERGE_W = 512
_GATE_BLK0 = Z_GATE // _MERGE_W


def _merge_fwd(z, b_gate, ta, tb):
    t = z.shape[0]
    tm = _tile(t, ROW_TILE)
    w = _MERGE_W
    nj = D_MODEL // w

    def body(za_ref, zb_ref, ba_ref, bb_ref, ta_ref, tb_ref, o_ref):
        ga = jax.nn.sigmoid(za_ref[...].astype(F32) + ba_ref[...])
        gb = jax.nn.sigmoid(zb_ref[...].astype(F32) + bb_ref[...])
        o_ref[...] = (ga * ta_ref[...].astype(F32) + gb * tb_ref[...].astype(F32)).astype(BF16)

    return pl.pallas_call(
        body, name="merge_fwd", grid=(t // tm, nj),
        in_specs=[pl.BlockSpec((tm, w), lambda i, j: (i, _GATE_BLK0 + j)),
                  pl.BlockSpec((tm, w), lambda i, j: (i, _GATE_BLK0 + nj + j)),
                  pl.BlockSpec((1, w), lambda i, j: (0, j)),
                  pl.BlockSpec((1, w), lambda i, j: (0, nj + j)),
                  pl.BlockSpec((tm, w), lambda i, j: (i, j)),
                  pl.BlockSpec((tm, w), lambda i, j: (i, j))],
        out_specs=pl.BlockSpec((tm, w), lambda i, j: (i, j)),
        out_shape=jax.ShapeDtypeStruct((t, D_MODEL), BF16),
        compiler_params=_params("parallel", "parallel"),
    )(z, z, b_gate, b_gate, ta, tb)


def _merge_bwd(dmg, z, b_gate, ta, tb):
    t = z.shape[0]
    tm = _tile(t, ROW_TILE)
    w = _MERGE_W
    nj = D_MODEL // w

    def body(dm_ref, za_ref, zb_ref, ba_ref, bb_ref, ta_ref, tb_ref, dta_ref, dtb_ref, dza_ref, dzb_ref,
             dba_ref, dbb_ref):
        dm = dm_ref[...].astype(F32)
        ga = jax.nn.sigmoid(za_ref[...].astype(F32) + ba_ref[...])
        gb = jax.nn.sigmoid(zb_ref[...].astype(F32) + bb_ref[...])
        dta_ref[...] = (dm * ga).astype(BF16)
        dtb_ref[...] = (dm * gb).astype(BF16)
        dza = dm * ta_ref[...].astype(F32) * ga * (1.0 - ga)
        dzb = dm * tb_ref[...].astype(F32) * gb * (1.0 - gb)
        dza_ref[...] = dza.astype(BF16)
        dzb_ref[...] = dzb.astype(BF16)

        @pl.when(pl.program_id(1) == 0)
        def _():
            dba_ref[...] = jnp.zeros_like(dba_ref)
            dbb_ref[...] = jnp.zeros_like(dbb_ref)

        dba_ref[...] += _fold8(dza)
        dbb_ref[...] += _fold8(dzb)

    blk = pl.BlockSpec((tm, w), lambda j, i: (i, j))
    acc = pl.BlockSpec((SUBLANES, w), lambda j, i: (0, j))
    return pl.pallas_call(
        body, name="merge_bwd", grid=(nj, t // tm),
        in_specs=[blk,
                  pl.BlockSpec((tm, w), lambda j, i: (i, _GATE_BLK0 + j)),
                  pl.BlockSpec((tm, w), lambda j, i: (i, _GATE_BLK0 + nj + j)),
                  pl.BlockSpec((1, w), lambda j, i: (0, j)),
                  pl.BlockSpec((1, w), lambda j, i: (0, nj + j)),
                  blk, blk],
        out_specs=[blk, blk, blk, blk, acc, acc],
        out_shape=[jax.ShapeDtypeStruct((t, D_MODEL), BF16)] * 4 + [jax.ShapeDtypeStruct((SUBLANES, D_MODEL), F32)] * 2,
        compiler_params=_params("parallel", "arbitrary"),
    )(dmg, z, z, b_gate, b_gate, ta, tb)


def _loss_grad(y, target):
    t, d = y.shape
    tm = _tile(t, ROW_TILE)

    def body(y_ref, t_ref, dy_ref, acc_ref):
        err = y_ref[...] - t_ref[...]
        dy_ref[...] = err * (1.0 / d)
        e8 = _fold8(err * err)
        part = e8[:, 0:LANES]
        for c in range(1, d // LANES):
            part = part + e8[:, LANES * c:LANES * (c + 1)]

        @pl.when(pl.program_id(0) == 0)
        def _():
            acc_ref[...] = jnp.zeros_like(acc_ref)

        acc_ref[...] += part

    return pl.pallas_call(
        body, name="loss_grad", grid=(t // tm,),
        in_specs=[_row_spec(tm, d), _row_spec(tm, d)],
        out_specs=[_row_spec(tm, d), _acc_spec(LANES)],
        out_shape=[jax.ShapeDtypeStruct((t, d), F32), jax.ShapeDtypeStruct((SUBLANES, LANES), F32)],
        compiler_params=_params("arbitrary"),
    )(y, target)


_MESH_ID = pl.DeviceIdType.MESH
_ANY = pl.BlockSpec(memory_space=pl.ANY)


def _all_gather(arrays):
    n = len(arrays)
    halves = []
    for a in arrays:
        assert a.shape[0] % 2 == 0, a.shape
        halves.append((pl.ds(0, a.shape[0] // 2), pl.ds(a.shape[0] // 2, a.shape[0] // 2)))
    OWN_SIB, OWN_X, OWN_Y, FWD_X, FWD_Y, SIB_X, SIB_Y, SIB_DA, SIB_DB = range(9)

    def body(*refs):
        x_refs, out_refs = refs[:n], refs[n:2 * n]
        send_sems, recv_sems, local_sems = refs[2 * n:]
        mx, my, mc = lax.axis_index("x"), lax.axis_index("y"), lax.axis_index("c")
        me, sibling = (mx, my, mc), (mx, my, 1 - mc)
        x_nbr, y_nbr, diag = (1 - mx, my, mc), (mx, 1 - my, mc), (1 - mx, 1 - my, mc)

        def slot(a, dev, rows=None):
            px, py, pc = dev
            ref = out_refs[a].at[4 * px + 2 * py + pc]
            return ref if rows is None else ref.at[rows]

        def other_core(dev):
            return (dev[0], dev[1], 1 - dev[2])

        def copy(a, sem, block, to, rows=None, src=None):
            return pltpu.make_async_remote_copy(
                src_ref=slot(a, block, rows) if src is None else src, dst_ref=slot(a, block, rows),
                send_sem=send_sems.at[a, sem], recv_sem=recv_sems.at[a, sem], device_id=to, device_id_type=_MESH_ID)

        mine = [pltpu.make_async_copy(x_refs[a], slot(a, me), local_sems.at[a]) for a in range(n)]
        sent = []
        for a in range(n):
            mine[a].start()
            sent += [copy(a, OWN_SIB, me, sibling, src=x_refs[a]), copy(a, OWN_X, me, x_nbr, src=x_refs[a]),
                     copy(a, OWN_Y, me, y_nbr, src=x_refs[a])]
        for cp in sent:
            cp.start()
        for a in range(n):
            first, second = halves[a]
            copy(a, OWN_Y, y_nbr, me).wait_recv()
            sent += [copy(a, FWD_X, y_nbr, x_nbr, rows=first), copy(a, SIB_Y, y_nbr, sibling)]
            sent[-2].start()
            sent[-1].start()
            copy(a, OWN_X, x_nbr, me).wait_recv()
            sent += [copy(a, FWD_Y, x_nbr, y_nbr, rows=second), copy(a, SIB_X, x_nbr, sibling)]
            sent[-2].start()
            sent[-1].start()
        for a in range(n):
            first, second = halves[a]
            copy(a, FWD_X, diag, me, rows=first).wait_recv()
            sent.append(copy(a, SIB_DA, diag, sibling, rows=first))
            sent[-1].start()
            copy(a, FWD_Y, diag, me, rows=second).wait_recv()
            sent.append(copy(a, SIB_DB, diag, sibling, rows=second))
            sent[-1].start()
        for a in range(n):
            first, second = halves[a]
            copy(a, OWN_SIB, sibling, me).wait_recv()
            copy(a, SIB_X, other_core(x_nbr), me).wait_recv()
            copy(a, SIB_Y, other_core(y_nbr), me).wait_recv()
            copy(a, SIB_DA, other_core(diag), me, rows=first).wait_recv()
            copy(a, SIB_DB, other_core(diag), me, rows=second).wait_recv()
        for cp in sent:
            cp.wait_send()
        for cp in mine:
            cp.wait()

    return pl.pallas_call(
        body, name="weight_all_gather",
        out_shape=[jax.ShapeDtypeStruct((N_DEV,) + a.shape, a.dtype) for a in arrays],
        in_specs=[_ANY] * n, out_specs=[_ANY] * n,
        scratch_shapes=[pltpu.SemaphoreType.DMA((n, 9)), pltpu.SemaphoreType.DMA((n, 9)),
                        pltpu.SemaphoreType.DMA((n,))],
    )(*arrays)


def _pair_exchange(sends):
    n = len(sends)

    def body(*refs):
        s_refs, r_refs = refs[:n], refs[n:2 * n]
        send_sems, recv_sems = refs[2 * n:]
        mx, my, mc = lax.axis_index("x"), lax.axis_index("y"), lax.axis_index("c")
        copies = []
        for a in range(n):
            for ch in range(4):
                cp = pltpu.make_async_remote_copy(
                    src_ref=s_refs[a].at[2 * ch + (1 - mc)], dst_ref=r_refs[a].at[ch], send_sem=send_sems.at[a, ch],
                    recv_sem=recv_sems.at[a, ch], device_id=(mx, my, 1 - mc), device_id_type=_MESH_ID)
                cp.start()
                copies.append(cp)
        for cp in copies:
            cp.wait_send()
            cp.wait_recv()

    return pl.pallas_call(
        body, name="grad_pair_exchange",
        out_shape=[jax.ShapeDtypeStruct((4,) + s.shape[1:], s.dtype) for s in sends],
        in_specs=[_ANY] * n, out_specs=[_ANY] * n,
        scratch_shapes=[pltpu.SemaphoreType.DMA((n, 4)), pltpu.SemaphoreType.DMA((n, 4))],
    )(*sends)


def _pair_add(send, half, core):
    _, r, c_ = send.shape
    tr = _row_tile(r, c_)

    def body(core_ref, s_ref, h_ref, o_ref):
        del core_ref
        o_ref[...] = (s_ref[...] + h_ref[...]).astype(BF16)

    blk = pl.BlockSpec((1, tr, c_), lambda ch, i, core_ref: (ch, i, 0))
    return pl.pallas_call(
        body, name="grad_pair_add",
        grid_spec=pltpu.PrefetchScalarGridSpec(
            num_scalar_prefetch=1, grid=(4, r // tr),
            in_specs=[pl.BlockSpec((1, tr, c_), lambda ch, i, core_ref: (2 * ch + core_ref[0], i, 0)), blk],
            out_specs=blk),
        out_shape=jax.ShapeDtypeStruct((4, r, c_), BF16),
        compiler_params=_params("parallel", "parallel"),
    )(core, send, half)


def _chip_exchange(parts):
    n = len(parts)

    def body(*refs):
        p_refs, r_refs = refs[:n], refs[n:2 * n]
        send_sems, recv_sems, local_sems = refs[2 * n:]
        mx, my, mc = lax.axis_index("x"), lax.axis_index("y"), lax.axis_index("c")
        mine = 2 * mx + my
        local = [pltpu.make_async_copy(p_refs[a].at[mine], r_refs[a].at[mine], local_sems.at[a]) for a in range(n)]
        copies = []
        for a in range(n):
            local[a].start()
            for rel in range(1, 4):
                px = 1 - mx if rel & 2 else mx
                py = 1 - my if rel & 1 else my
                cp = pltpu.make_async_remote_copy(
                    src_ref=p_refs[a].at[2 * px + py], dst_ref=r_refs[a].at[mine], send_sem=send_sems.at[a, rel - 1],
                    recv_sem=recv_sems.at[a, rel - 1], device_id=(px, py, mc), device_id_type=_MESH_ID)
                cp.start()
                copies.append(cp)
        for cp in copies:
            cp.wait_send()
            cp.wait_recv()
        for cp in local:
            cp.wait()

    return pl.pallas_call(
        body, name="grad_chip_exchange",
        out_shape=[jax.ShapeDtypeStruct(p.shape, p.dtype) for p in parts],
        in_specs=[_ANY] * n, out_specs=[_ANY] * n,
        scratch_shapes=[pltpu.SemaphoreType.DMA((n, 3)), pltpu.SemaphoreType.DMA((n, 3)),
                        pltpu.SemaphoreType.DMA((n,))],
    )(*parts)


def _row_tile(r, c_):
    tr = min(r, ADAM_BLOCK_ELEMS // (pl.cdiv(c_, LANES) * LANES))
    while r % tr:
        tr -= SUBLANES
    return tr


def _adamw(recv, w, m, v):
    r, c_ = w.shape
    tr = _row_tile(r, c_)
    n_src = recv.shape[0]

    def body(g_ref, w_ref, m_ref, v_ref, go_ref, d_ref, mo_ref, vo_ref):
        g = g_ref[0].astype(F32)
        for s in range(1, n_src):
            g = g + g_ref[s].astype(F32)
        go_ref[...] = g
        mn = ADAM_B1 * m_ref[...] + (1.0 - ADAM_B1) * g
        vn = ADAM_B2 * v_ref[...] + (1.0 - ADAM_B2) * (g * g)
        mo_ref[...] = mn
        vo_ref[...] = vn
        m_hat = mn / (1.0 - ADAM_B1 ** ADAM_STEP)
        v_hat = vn / (1.0 - ADAM_B2 ** ADAM_STEP)
        d_ref[...] = -ADAM_LR * (m_hat / (jnp.sqrt(v_hat) + ADAM_EPS) + ADAM_WD * w_ref[...])

    spec = pl.BlockSpec((tr, c_), lambda i: (i, 0))
    out = jax.ShapeDtypeStruct((r, c_), F32)
    return pl.pallas_call(
        body, name="grad_sum_adamw", grid=(r // tr,),
        in_specs=[pl.BlockSpec((n_src, tr, c_), lambda i: (0, i, 0)), spec, spec, spec],
        out_specs=[spec, spec, spec, spec], out_shape=[out, out, out, out],
        compiler_params=_params("parallel"),
    )(recv, w, m, v)


def _pad_cols(a, before, after):
    parts = []
    if before:
        parts.append(jnp.zeros(a.shape[:-1] + (before,), a.dtype))
    parts.append(a)
    if after:
        parts.append(jnp.zeros(a.shape[:-1] + (after,), a.dtype))
    return jnp.concatenate(parts, axis=-1)


def _q_head_pairs(a, axis):
    shp = a.shape
    a = a.reshape(shp[:axis] + (GQA_KV_HEADS, GQA_GROUP, HEAD_DIM) + shp[axis + 1:])
    a = jnp.swapaxes(a, axis, axis + 1)
    return a.reshape(shp)


def _q_head_unpairs(a, axis):
    shp = a.shape
    a = a.reshape(shp[:axis] + (GQA_GROUP, GQA_KV_HEADS, HEAD_DIM) + shp[axis + 1:])
    a = jnp.swapaxes(a, axis, axis + 1)
    return a.reshape(shp)


def _layout_weights(w):
    w_in = w["w_in"]
    lead = w_in.shape[:-1]
    w_in_p = jnp.concatenate([
        _q_head_pairs(w_in[..., 0:512], w_in.ndim - 1),
        w_in[..., 512:1408],
        _pad_cols(w_in[..., 1408:1440], KR_LANE0, LANES - KR_LANE0 - MLA_ROPE_DIM),
        w_in[..., 1440:],
    ], axis=-1)
    wq = w["w_q_up"]
    wq_p = _pad_cols(wq.reshape(wq.shape[:-1] + (MLA_HEADS, MLA_QK_DIM)), 0, LANES - MLA_QK_DIM)
    wq_p = wq_p.reshape(wq.shape[:-1] + (MLA_HEADS * LANES,))
    wkv = w["w_kv_up"]
    wkv4 = wkv.reshape(wkv.shape[:-1] + (MLA_HEADS, 2 * HEAD_DIM))
    wk_p = _pad_cols(wkv4[..., :HEAD_DIM], 0, LANES - HEAD_DIM).reshape(wkv.shape[:-1] + (MLA_HEADS * LANES,))
    wv_p = wkv4[..., HEAD_DIM:].reshape(wkv.shape[:-1] + (MLA_HEADS * HEAD_DIM,))
    del lead
    return {
        "w_in": w_in_p, "w_q_up": wq_p, "w_kv_up": jnp.concatenate([wk_p, wv_p], axis=-1),
        "w_branch_a": _q_head_pairs(w["w_branch_a"], w["w_branch_a"].ndim - 2), "w_branch_b": w["w_branch_b"],
        "w_o": w["w_o"], "w_ffn_up": w["w_ffn_up"], "w_ffn_down": w["w_ffn_down"],
    }


def _unlayout_grads(g):
    gi = g["w_in"]
    kr0 = Z_KR + KR_LANE0
    g_in = jnp.concatenate([
        _q_head_unpairs(gi[..., 0:512], gi.ndim - 1), gi[..., 512:1408], gi[..., kr0:kr0 + MLA_ROPE_DIM],
        gi[..., Z_GATE:],
    ], axis=-1)
    gq = g["w_q_up"]
    gq = gq.reshape(gq.shape[:-1] + (MLA_HEADS, LANES))[..., :MLA_QK_DIM]
    gq = gq.reshape(gq.shape[:-2] + (MLA_HEADS * MLA_QK_DIM,))
    gkv = g["w_kv_up"]
    gk = gkv[..., :MLA_HEADS * LANES].reshape(gkv.shape[:-1] + (MLA_HEADS, LANES))[..., :HEAD_DIM]
    gv = gkv[..., MLA_HEADS * LANES:].reshape(gkv.shape[:-1] + (MLA_HEADS, HEAD_DIM))
    gkv = jnp.concatenate([gk, gv], axis=-1).reshape(gkv.shape[:-1] + (MLA_HEADS * 2 * HEAD_DIM,))
    return {
        "w_in": g_in, "w_q_up": gq, "w_kv_up": gkv,
        "w_branch_a": _q_head_unpairs(g["w_branch_a"], g["w_branch_a"].ndim - 2), "w_branch_b": g["w_branch_b"],
        "w_o": g["w_o"], "w_ffn_up": g["w_ffn_up"], "w_ffn_down": g["w_ffn_down"],
    }


def _pack_small(parts):
    flat = jnp.concatenate([p.reshape(-1) for p in parts])
    pad = (-flat.shape[0]) % (SUBLANES * LANES)
    if pad:
        flat = jnp.concatenate([flat, jnp.zeros((pad,), flat.dtype)])
    return flat.reshape(-1, LANES)


def _unpack_small(packed, shapes):
    flat = packed.reshape(-1)
    out, off = [], 0
    for shp in shapes:
        n = int(np.prod(shp))
        out.append(flat[off:off + n].reshape(shp))
        off += n
    return out


def _shards_of(full, axis):
    shp = full.shape
    cut = shp[:axis] + (N_DEV, shp[axis] // N_DEV) + shp[axis + 1:]
    return jnp.moveaxis(full.reshape(cut), axis, 0)


def _from_shards(shards, axis):
    full = list(shards.shape[1:])
    full[axis] *= N_DEV
    return jnp.moveaxis(shards, 0, axis).reshape(full)


def _rows2d(a):
    return a.reshape(-1, a.shape[-1])


def _layer_fwd(x, u, lw, tabs):
    cos_a, sin_a, cos_b, sin_b = tabs
    z = _matmul(u, lw["w_in"], "nn", "mm_in")
    qa, ka, va, cqn, ckvn, krr = _prep_a_fwd(z, lw["gq2"], lw["gk2"], lw["gqa"], lw["gkva"], cos_a, sin_a, cos_b, sin_b)
    qb = _matmul(cqn, lw["w_q_up"], "nn", "mm_q_up")
    kvb = _matmul(ckvn, lw["w_kv_up"], "nn", "mm_kv_up")
    q_b, k_b, v_b = _prep_b_fwd(qb, kvb, krr, cos_b, sin_b)
    ya, lse_a = _attn_fwd(qa, ka, va, True, "gqa_fwd")
    yb, lse_b = _attn_fwd(q_b, k_b, v_b, False, "mla_fwd")
    ta = _matmul(ya, lw["w_branch_a"], "nn", "mm_branch_a")
    tb = _matmul(yb, lw["w_branch_b"], "nn", "mm_branch_b")
    merged = _merge_fwd(z, lw["b_gate"], ta, tb)
    m = _matmul(merged, lw["w_o"], "nn", "mm_o")
    x2, u2 = _res_norm_fwd(x, m, lw["post_mix_g"], lw["pre_ffn_g"])
    h, a = _matmul(u2, lw["w_ffn_up"], "nn", "mm_ffn_up", post="relu2")
    f = _matmul(a, lw["w_ffn_down"], "nn", "mm_ffn_down")
    x3, u_next = _res_norm_fwd(x2, f, lw["post_ffn_g"], lw["next_pre_mix_g"])
    saved = dict(u=u, z=z, qa=qa, ka=ka, va=va, cqn=cqn, ckvn=ckvn, q_b=q_b, k_b=k_b, v_b=v_b, ya=ya, yb=yb,
                 lse_a=lse_a, lse_b=lse_b, ta=ta, tb=tb, merged=merged, m=m, x2=x2, u2=u2, h=h, a=a, f=f, x3=x3)
    return x3, u_next, saved


def _layer_bwd(dx3, du_next, lw, sv, tabs, gbuf, layer):
    cos_a, sin_a, cos_b, sin_b = tabs
    g = {}
    dx3, df, dg4, dg1n = _res_norm_bwd(sv["x3"], sv["f"], lw["post_ffn_g"], lw["next_pre_mix_g"], dx3, du_next)
    g["post_ffn_g"], g["next_pre_mix_g"] = dg4, dg1n
    dh = _matmul(df, lw["w_ffn_down"], "nt", "mm_d_h", post="relu2_bwd", h=sv["h"])
    g["w_ffn_down"] = _matmul(sv["a"], df, "tn", "mm_dw_ffn_down", stack=(gbuf["w_ffn_down"], layer))
    du2 = _matmul(dh, lw["w_ffn_up"], "nt", "mm_d_u2")
    g["w_ffn_up"] = _matmul(sv["u2"], dh, "tn", "mm_dw_ffn_up", stack=(gbuf["w_ffn_up"], layer))
    dx2, dm, dg2, dg3 = _res_norm_bwd(sv["x2"], sv["m"], lw["post_mix_g"], lw["pre_ffn_g"], dx3, du2)
    g["post_mix_g"], g["pre_ffn_g"] = dg2, dg3
    dmg = _matmul(dm, lw["w_o"], "nt", "mm_d_merged")
    g["w_o"] = _matmul(sv["merged"], dm, "tn", "mm_dw_o", stack=(gbuf["w_o"], layer))
    dta, dtb, dzg_a, dzg_b, db_a, db_b = _merge_bwd(dmg, sv["z"], lw["b_gate"], sv["ta"], sv["tb"])
    g["b_gate"] = jnp.concatenate([db_a, db_b], axis=-1)
    dya, delta_a = _matmul(dta, lw["w_branch_a"], "nt", "mm_d_ya", post="delta", h=sv["ya"])
    g["w_branch_a"] = _matmul(sv["ya"], dta, "tn", "mm_dw_branch_a", stack=(gbuf["w_branch_a"], layer))
    dyb, delta_b = _matmul(dtb, lw["w_branch_b"], "nt", "mm_d_yb", post="delta", h=sv["yb"])
    g["w_branch_b"] = _matmul(sv["yb"], dtb, "tn", "mm_dw_branch_b", stack=(gbuf["w_branch_b"], layer))
    dqa, dka4, dva4 = _attn_bwd(sv["qa"], sv["ka"], sv["va"], dya, sv["lse_a"], delta_a, True, "gqa_bwd")
    dq_b, dk_b, dv_b = _attn_bwd(sv["q_b"], sv["k_b"], sv["v_b"], dyb, sv["lse_b"], delta_b, False, "mla_bwd")
    dqb, dkvb, dkr = _prep_b_bwd(dq_b, dk_b, dv_b, cos_b, sin_b)
    dcqn = _matmul(dqb, lw["w_q_up"], "nt", "mm_d_cqn")
    g["w_q_up"] = _matmul(sv["cqn"], dqb, "tn", "mm_dw_q_up", stack=(gbuf["w_q_up"], layer))
    dckvn = _matmul(dkvb, lw["w_kv_up"], "nt", "mm_d_ckvn")
    g["w_kv_up"] = _matmul(sv["ckvn"], dkvb, "tn", "mm_dw_kv_up", stack=(gbuf["w_kv_up"], layer))
    dz, dgq, dgk, dgqa, dgkva = _prep_a_bwd(sv["z"], dqa, dka4, dva4, dcqn, dckvn, dkr, dzg_a, dzg_b, lw["gq2"],
                                            lw["gk2"], lw["gqa"], lw["gkva"], cos_a, sin_a)
    g["q_norm_g"], g["k_norm_g"], g["q_a_norm_g"], g["kv_a_norm_g"] = dgq, dgk, dgqa, dgkva
    du = _matmul(dz, lw["w_in"], "nt", "mm_d_u")
    g["w_in"] = _matmul(sv["u"], dz, "tn", "mm_dw_in", stack=(gbuf["w_in"], layer))
    return dx2, du, g


def kernel(x, w_in, b_gate, q_norm_g, k_norm_g, q_a_norm_g, kv_a_norm_g, w_q_up, w_kv_up, w_branch_a, w_branch_b, w_o, w_ffn_up, w_ffn_down, pre_mix_g, post_mix_g, pre_ffn_g, post_ffn_g, loss_target, m_w_in, m_b_gate, m_q_norm_g, m_k_norm_g, m_q_a_norm_g, m_kv_a_norm_g, m_w_q_up, m_w_kv_up, m_w_branch_a, m_w_branch_b, m_w_o, m_w_ffn_up, m_w_ffn_down, m_pre_mix_g, m_post_mix_g, m_pre_ffn_g, m_post_ffn_g, v_w_in, v_b_gate, v_q_norm_g, v_k_norm_g, v_q_a_norm_g, v_kv_a_norm_g, v_w_q_up, v_w_kv_up, v_w_branch_a, v_w_branch_b, v_w_o, v_w_ffn_up, v_w_ffn_down, v_pre_mix_g, v_post_mix_g, v_pre_ffn_g, v_post_ffn_g):
    weights = dict(zip(WEIGHT_NAMES, (w_in, b_gate, q_norm_g, k_norm_g, q_a_norm_g, kv_a_norm_g, w_q_up, w_kv_up,
                                      w_branch_a, w_branch_b, w_o, w_ffn_up, w_ffn_down, pre_mix_g, post_mix_g,
                                      pre_ffn_g, post_ffn_g)))
    mom_m = dict(zip(WEIGHT_NAMES, (m_w_in, m_b_gate, m_q_norm_g, m_k_norm_g, m_q_a_norm_g, m_kv_a_norm_g, m_w_q_up,
                                    m_w_kv_up, m_w_branch_a, m_w_branch_b, m_w_o, m_w_ffn_up, m_w_ffn_down,
                                    m_pre_mix_g, m_post_mix_g, m_pre_ffn_g, m_post_ffn_g)))
    mom_v = dict(zip(WEIGHT_NAMES, (v_w_in, v_b_gate, v_q_norm_g, v_k_norm_g, v_q_a_norm_g, v_kv_a_norm_g, v_w_q_up,
                                    v_w_kv_up, v_w_branch_a, v_w_branch_b, v_w_o, v_w_ffn_up, v_w_ffn_down,
                                    v_pre_mix_g, v_post_mix_g, v_pre_ffn_g, v_post_ffn_g)))
    assert x.shape[0] == 1 and x.shape[2] == D_MODEL, x.shape
    n_layers = w_in.shape[0]
    t = x.shape[1]
    x0 = x.reshape(t, D_MODEL)
    target = loss_target.reshape(t, D_MODEL)
    shard_shapes = {n: weights[n].shape for n in BIG_NAMES}
    small_shapes = [weights[n].shape for n in SMALL_NAMES]

    gathered = _all_gather([weights[n].astype(BF16) for n in BIG_NAMES])
    full = {n: _from_shards(g, SHARD_AXIS[n]) for n, g in zip(BIG_NAMES, gathered)}
    lw_all = _layout_weights(full)
    lw_all["b_gate"] = b_gate.reshape(n_layers, 1, 2 * D_MODEL)
    lw_all["gq2"] = jnp.tile(q_norm_g, (1, 2)).reshape(n_layers, 1, LANES)
    lw_all["gk2"] = jnp.tile(k_norm_g, (1, 2)).reshape(n_layers, 1, LANES)
    lw_all["gqa"] = q_a_norm_g.reshape(n_layers, 1, MLA_Q_RANK)
    lw_all["gkva"] = kv_a_norm_g.reshape(n_layers, 1, MLA_KV_RANK)
    for n in ("post_mix_g", "pre_ffn_g", "post_ffn_g"):
        lw_all[n] = weights[n]
    lw_all["next_pre_mix_g"] = jnp.roll(pre_mix_g, -1, axis=0)

    tabs = _rope_tables(t)
    u0 = _rms_fwd(x0, pre_mix_g[0])

    layer_w = [{n: a[l] for n, a in lw_all.items()} for l in range(n_layers)]
    xc, uc, saved = x0, u0, []
    for l in range(n_layers):
        xc, uc, sv = _layer_fwd(xc, uc, layer_w[l], tabs)
        saved.append(sv)
    dy, loss_acc = _loss_grad(xc, target)
    loss = lax.psum(0.5 * jnp.sum(loss_acc) / D_MODEL, ("x", "y", "c"))

    dx0, du0, layer_g = dy, jnp.zeros((t, D_MODEL), F32), [None] * n_layers
    gbuf = {n: lax.empty(lw_all[n].shape, F32) for n in BIG_NAMES}
    for l in reversed(range(n_layers)):
        dx0, du0, layer_g[l] = _layer_bwd(dx0, du0, layer_w[l], saved[l], tabs, gbuf, l)
        gbuf = {n: layer_g[l][n] for n in BIG_NAMES}
    grads = {n: jnp.stack([g[n] for g in layer_g]) for n in layer_g[0] if n not in BIG_NAMES}
    grads.update(gbuf)
    grad_x, dg1_first = _rms_bwd(x0, pre_mix_g[0], dx0, du0)

    big_grads = _unlayout_grads({n: grads[n] for n in BIG_NAMES})
    fold = lambda a: a.sum(axis=1)
    dgq = fold(grads["q_norm_g"]).reshape(n_layers, 2, HEAD_DIM).sum(axis=1)
    dgk = fold(grads["k_norm_g"]).reshape(n_layers, 2, HEAD_DIM).sum(axis=1)
    dg1 = jnp.concatenate([fold(dg1_first[None]), fold(grads["next_pre_mix_g"])[:-1]], axis=0)
    small_grads = {
        "b_gate": fold(grads["b_gate"]), "q_norm_g": dgq, "k_norm_g": dgk, "q_a_norm_g": fold(grads["q_a_norm_g"]),
        "kv_a_norm_g": fold(grads["kv_a_norm_g"]), "pre_mix_g": dg1, "post_mix_g": fold(grads["post_mix_g"]),
        "pre_ffn_g": fold(grads["pre_ffn_g"]), "post_ffn_g": fold(grads["post_ffn_g"]),
    }
    small_packed = _pack_small([small_grads[n] for n in SMALL_NAMES])
    sends = [_shards_of(big_grads[n], SHARD_AXIS[n]).reshape((N_DEV,) + _rows2d(weights[n]).shape)
             for n in BIG_NAMES]
    sends.append(jnp.broadcast_to(small_packed[None], (N_DEV,) + small_packed.shape))
    halves = _pair_exchange(sends)
    core = lax.axis_index("c").astype(jnp.int32).reshape(1)
    recvs = _chip_exchange([_pair_add(s, h, core) for s, h in zip(sends, halves)])

    results = {}
    for n, recv in zip(BIG_NAMES, recvs):
        res = _adamw(recv, _rows2d(weights[n]), _rows2d(mom_m[n]), _rows2d(mom_v[n]))
        results[n] = [r.reshape(shard_shapes[n]) for r in res]
    res = _adamw(recvs[-1], *[_pack_small([d[n] for n in SMALL_NAMES]) for d in (weights, mom_m, mom_v)])
    for kind, packed_out in enumerate(res):
        for n, val in zip(SMALL_NAMES, _unpack_small(packed_out, small_shapes)):
            results.setdefault(n, [None] * 4)[kind] = val
    outs = [results[n][kind] for kind in range(4) for n in WEIGHT_NAMES]
    return (loss, grad_x.reshape(x.shape), *outs)
```

```python
import math

import jax
import jax.numpy as jnp
import numpy as np
from jax import lax
from jax.experimental import pallas as pl
from jax.experimental.pallas import tpu as pltpu

F32 = jnp.float32
BF16 = jnp.bfloat16

D_MODEL = 1024
GRID_W = 64
ROPE_THETA = 10000.0
EPS = 1e-6
GQA_HEADS = 8
GQA_KV_HEADS = 2
GQA_GROUP = GQA_HEADS // GQA_KV_HEADS
HEAD_DIM = 64
MLA_HEADS = 8
MLA_ROPE_DIM = 32
MLA_QK_DIM = 96
MLA_Q_RANK = 384
MLA_KV_RANK = 256
GQA_SCALE = 1.0 / math.sqrt(HEAD_DIM)
MLA_SCALE = 1.0 / math.sqrt(MLA_QK_DIM)
LOG2E = math.log2(math.e)
LN2 = math.log(2.0)

ADAM_LR = 0.001
ADAM_B1 = 0.9
ADAM_B2 = 0.999
ADAM_EPS = 1e-08
ADAM_WD = 0.01
ADAM_STEP = 10

N_DEV = 8
LANES = 128
SUBLANES = 8
VMEM_LIMIT = 48 * 1024 * 1024

Z_QA, Z_KA, Z_VA, Z_CQ, Z_CKV, Z_KR, Z_GATE = 0, 512, 640, 768, 1152, 1408, 1536
Z_ATT_W = 1536
Z_W = 3584
KR_LANE0 = 64

WEIGHT_NAMES = ("w_in", "b_gate", "q_norm_g", "k_norm_g", "q_a_norm_g", "kv_a_norm_g", "w_q_up", "w_kv_up",
                "w_branch_a", "w_branch_b", "w_o", "w_ffn_up", "w_ffn_down", "pre_mix_g", "post_mix_g",
                "pre_ffn_g", "post_ffn_g")
SHARD_AXIS = {"w_in": 2, "w_q_up": 2, "w_kv_up": 2, "w_branch_a": 2, "w_branch_b": 2, "w_o": 1, "w_ffn_up": 2,
              "w_ffn_down": 1}
BIG_NAMES = tuple(n for n in WEIGHT_NAMES if n in SHARD_AXIS)
SMALL_NAMES = tuple(n for n in WEIGHT_NAMES if n not in SHARD_AXIS)
ADAM_BLOCK_ELEMS = 256 * 1024
MM_TILE = 1024
MM_TILE_TOKENS = 2048
MM_TILE_K = 2048
PREP_ROWS = 512
ROW_TILE = 1024
ATTN_TQ = 1024
ATTN_TK = 1024


def _params(*semantics):
    return pltpu.CompilerParams(dimension_semantics=semantics, vmem_limit_bytes=VMEM_LIMIT)


def _tile(n, pref):
    if n <= pref:
        return n
    t = (pref // LANES) * LANES
    while n % t:
        t -= LANES
    return t


def _fold8(t):
    return t.reshape(t.shape[0] // SUBLANES, SUBLANES, t.shape[1]).sum(axis=0)


_DIMS = {"nn": ((1,), (0,)), "nt": ((1,), (1,)), "tn": ((0,), (0,))}


def _matmul(a, b, mode, name, post=None, h=None, stack=None):
    out_dt = F32 if mode == "tn" else BF16
    if mode == "nn":
        (m, k), n = a.shape, b.shape[1]
    elif mode == "nt":
        (m, k), n = a.shape, b.shape[0]
    else:
        (k, m), n = a.shape, b.shape[1]
    tm = _tile(m, MM_TILE_TOKENS if mode != "tn" and k <= MM_TILE else MM_TILE)
    tn, tk = _tile(n, MM_TILE), _tile(k, MM_TILE_K)
    nk = k // tk
    dims = (_DIMS[mode], ((), ()))
    operands = [a, b] + ([h] if post in ("relu2_bwd", "delta") else []) + ([stack[0]] if stack else [])
    n_in = len(operands)
    n_out = 2 if post in ("relu2", "delta") else 1
    assert post != "delta" or tn == n, (n, tn)

    def body(*refs):
        a_ref, b_ref = refs[:2]
        o_refs, acc_ref = refs[n_in:n_in + n_out], refs[-1]

        def finish(val):
            if post == "relu2":
                o_refs[0][...] = val.astype(out_dt)
                r = jnp.maximum(val, 0.0)
                o_refs[1][...] = (r * r).astype(BF16)
            elif post == "relu2_bwd":
                o_refs[0][...] = (val * (2.0 * jnp.maximum(refs[2][...].astype(F32), 0.0))).astype(BF16)
            elif post == "delta":
                do = val.astype(BF16)
                o_refs[0][...] = do
                prod = do.astype(F32) * refs[2][...].astype(F32)
                for g in range(n // LANES):
                    x = prod[:, LANES * g:LANES * (g + 1)]
                    lo = _lo_mask(x.shape)
                    d0 = jnp.sum(jnp.where(lo, x, 0.0), axis=-1, keepdims=True)
                    d1 = jnp.sum(jnp.where(lo, 0.0, x), axis=-1, keepdims=True)
                    o_refs[1][2 * g] = jnp.broadcast_to(d0, (tm, LANES))
                    o_refs[1][2 * g + 1] = jnp.broadcast_to(d1, (tm, LANES))
            else:
                o_refs[0][...] = val.astype(out_dt)

        prod = lax.dot_general(a_ref[...], b_ref[...], dims, preferred_element_type=F32)
        if nk == 1:
            finish(prod)
        else:
            kk = pl.program_id(2)

            @pl.when(kk == 0)
            def _():
                acc_ref[...] = prod

            @pl.when(kk > 0)
            def _():
                acc_ref[...] += prod

            @pl.when(kk == nk - 1)
            def _():
                finish(acc_ref[...])

    if mode == "tn":
        a_spec = pl.BlockSpec((tk, tm), lambda i, j, kk: (kk, i))
    else:
        a_spec = pl.BlockSpec((tm, tk), lambda i, j, kk: (i, kk))
    if mode == "nt":
        b_spec = pl.BlockSpec((tn, tk), lambda i, j, kk: (j, kk))
    else:
        b_spec = pl.BlockSpec((tk, tn), lambda i, j, kk: (kk, j))
    o_spec = pl.BlockSpec((tm, tn), lambda i, j, kk: (i, j))
    main_out, bf16_out = jax.ShapeDtypeStruct((m, n), out_dt), jax.ShapeDtypeStruct((m, n), BF16)
    heads = n // HEAD_DIM
    delta_out = jax.ShapeDtypeStruct((heads, m, LANES), F32)
    out_shape = {None: main_out, "relu2": [main_out, bf16_out], "relu2_bwd": bf16_out,
                 "delta": [bf16_out, delta_out]}[post]
    in_specs = [a_spec, b_spec] + ([o_spec] if post in ("relu2_bwd", "delta") else [])
    out_specs = {None: o_spec, "relu2": [o_spec, o_spec], "relu2_bwd": o_spec,
                 "delta": [o_spec, pl.BlockSpec((heads, tm, LANES), lambda i, j, kk: (0, i, 0))]}[post]
    aliases = {}
    if stack:
        buf, layer = stack
        assert post is None and buf.shape[1:] == (m, n) and buf.dtype == out_dt, (buf.shape, buf.dtype)
        in_specs.append(pl.BlockSpec(memory_space=pl.ANY))
        out_specs = pl.BlockSpec((None, tm, tn), lambda i, j, kk: (layer, i, j))
        out_shape = jax.ShapeDtypeStruct(buf.shape, buf.dtype)
        aliases = {n_in - 1: 0}
    return pl.pallas_call(
        body,
        name=name,
        grid=(m // tm, n // tn, nk),
        in_specs=in_specs,
        out_specs=out_specs,
        out_shape=out_shape,
        scratch_shapes=[pltpu.VMEM((tm, tn), F32)],
        input_output_aliases=aliases,
        compiler_params=_params("parallel", "parallel", "arbitrary"),
    )(*operands)


def _rinv(x):
    return lax.rsqrt(jnp.mean(x * x, axis=-1, keepdims=True) + EPS)


def _rms_bwd_rows(x, g, dy):
    r = _rinv(x)
    xh = x * r
    dxh = dy * g
    dx = r * (dxh - xh * jnp.mean(dxh * xh, axis=-1, keepdims=True))
    return dx, dy * xh


def _row_spec(tm, c):
    return pl.BlockSpec((tm, c), lambda i: (i, 0))


def _vec_spec(c):
    return pl.BlockSpec((1, c), lambda i: (0, 0))


def _acc_spec(c):
    return pl.BlockSpec((SUBLANES, c), lambda i: (0, 0))


def _rms_fwd(x, g):
    t, d = x.shape
    tm = _tile(t, ROW_TILE)

    def body(x_ref, g_ref, o_ref):
        xv = x_ref[...]
        o_ref[...] = (xv * _rinv(xv) * g_ref[...]).astype(BF16)

    return pl.pallas_call(
        body, name="rms_fwd", grid=(t // tm,),
        in_specs=[_row_spec(tm, d), _vec_spec(d)], out_specs=_row_spec(tm, d),
        out_shape=jax.ShapeDtypeStruct((t, d), BF16), compiler_params=_params("parallel"),
    )(x, g.reshape(1, d))


def _rms_bwd(x, g, dres, dy):
    t, d = x.shape
    tm = _tile(t, ROW_TILE)

    def body(x_ref, g_ref, dres_ref, dy_ref, dx_ref, dg_ref):
        dx, dgc = _rms_bwd_rows(x_ref[...], g_ref[...], dy_ref[...].astype(F32))
        dx_ref[...] = dres_ref[...] + dx

        @pl.when(pl.program_id(0) == 0)
        def _():
            dg_ref[...] = jnp.zeros_like(dg_ref)

        dg_ref[...] += _fold8(dgc)

    return pl.pallas_call(
        body, name="rms_bwd", grid=(t // tm,),
        in_specs=[_row_spec(tm, d), _vec_spec(d), _row_spec(tm, d), _row_spec(tm, d)],
        out_specs=[_row_spec(tm, d), _acc_spec(d)],
        out_shape=[jax.ShapeDtypeStruct((t, d), F32), jax.ShapeDtypeStruct((SUBLANES, d), F32)],
        compiler_params=_params("arbitrary"),
    )(x, g.reshape(1, d), dres, dy)


def _res_norm_fwd(x, m, g_post, g_next):
    t, d = x.shape
    tm = _tile(t, ROW_TILE)

    def body(x_ref, m_ref, gp_ref, gn_ref, x2_ref, u2_ref):
        mv = m_ref[...].astype(F32)
        x2 = x_ref[...] + mv * _rinv(mv) * gp_ref[...]
        x2_ref[...] = x2
        u2_ref[...] = (x2 * _rinv(x2) * gn_ref[...]).astype(BF16)

    return pl.pallas_call(
        body, name="res_norm_fwd", grid=(t // tm,),
        in_specs=[_row_spec(tm, d), _row_spec(tm, d), _vec_spec(d), _vec_spec(d)],
        out_specs=[_row_spec(tm, d), _row_spec(tm, d)],
        out_shape=[jax.ShapeDtypeStruct((t, d), F32), jax.ShapeDtypeStruct((t, d), BF16)],
        compiler_params=_params("parallel"),
    )(x, m, g_post.reshape(1, d), g_next.reshape(1, d))


def _res_norm_bwd(x2, m, g_post, g_next, dx2_in, du2):
    t, d = x2.shape
    tm = _tile(t, ROW_TILE // 2)

    def body(x2_ref, m_ref, gp_ref, gn_ref, dx2in_ref, du2_ref, dx2_ref, dm_ref, dgp_ref, dgn_ref):
        dxn, dgn_c = _rms_bwd_rows(x2_ref[...], gn_ref[...], du2_ref[...].astype(F32))
        dx2 = dx2in_ref[...] + dxn
        dx2_ref[...] = dx2
        dm, dgp_c = _rms_bwd_rows(m_ref[...].astype(F32), gp_ref[...], dx2)
        dm_ref[...] = dm.astype(BF16)

        @pl.when(pl.program_id(0) == 0)
        def _():
            dgp_ref[...] = jnp.zeros_like(dgp_ref)
            dgn_ref[...] = jnp.zeros_like(dgn_ref)

        dgp_ref[...] += _fold8(dgp_c)
        dgn_ref[...] += _fold8(dgn_c)

    return pl.pallas_call(
        body, name="res_norm_bwd", grid=(t // tm,),
        in_specs=[_row_spec(tm, d), _row_spec(tm, d), _vec_spec(d), _vec_spec(d), _row_spec(tm, d), _row_spec(tm, d)],
        out_specs=[_row_spec(tm, d), _row_spec(tm, d), _acc_spec(d), _acc_spec(d)],
        out_shape=[jax.ShapeDtypeStruct((t, d), F32), jax.ShapeDtypeStruct((t, d), BF16),
                   jax.ShapeDtypeStruct((SUBLANES, d), F32), jax.ShapeDtypeStruct((SUBLANES, d), F32)],
        compiler_params=_params("arbitrary"),
    )(x2, m, g_post.reshape(1, d), g_next.reshape(1, d), dx2_in, du2)


def _rope_tables(t):
    rows = t // GRID_W
    row = jnp.repeat(jnp.arange(rows, dtype=F32), GRID_W)
    col = jnp.tile(jnp.arange(GRID_W, dtype=F32), rows)

    def tab(rot_dim):
        half = rot_dim // 2
        inv = ROPE_THETA ** (-jnp.arange(0, half, 2, dtype=F32) / half)
        ar = row[:, None] * inv[None, :]
        ac = col[:, None] * inv[None, :]
        ang = jnp.concatenate([ar, ar, ac, ac], axis=-1)
        q = half // 2
        sign = np.tile(np.concatenate([-np.ones(q, np.float32), np.ones(q, np.float32)]), 2)
        return jnp.cos(ang), jnp.sin(ang) * sign[None, :]

    ca, sa = tab(HEAD_DIM)
    cb, sb = tab(MLA_ROPE_DIM)
    one = jnp.ones((t, 1), F32)
    cos_b = jnp.concatenate([one * jnp.ones((1, KR_LANE0), F32), cb, one * jnp.ones((1, 32), F32)], axis=-1)
    sin_b = jnp.concatenate([jnp.zeros((t, KR_LANE0), F32), sb, jnp.zeros((t, 32), F32)], axis=-1)
    return jnp.tile(ca, (1, GQA_HEADS)), jnp.tile(sa, (1, GQA_HEADS)), cos_b, sin_b


def _swap_halves(x, sh):
    lane = lax.broadcasted_iota(jnp.int32, x.shape, 1)
    up = pltpu.roll(x, LANES - sh, 1)
    dn = pltpu.roll(x, sh, 1)
    return jnp.where((lane & (2 * sh - 1)) < sh, up, dn)


def _rope(x, cos, sin_s, sh):
    return x * cos + _swap_halves(x, sh) * sin_s


def _rope_bwd(dy, cos, sin_s, sh):
    return dy * cos + _swap_halves(dy * sin_s, sh)


def _lo_mask(shape):
    return lax.broadcasted_iota(jnp.int32, shape, 1) < HEAD_DIM


def _half_mean(t, lo):
    s_lo = jnp.sum(jnp.where(lo, t, 0.0), axis=-1, keepdims=True)
    s_hi = jnp.sum(jnp.where(lo, 0.0, t), axis=-1, keepdims=True)
    return jnp.where(lo, s_lo, s_hi) * (1.0 / HEAD_DIM)


def _head_norm(x, g2):
    lo = _lo_mask(x.shape)
    r = lax.rsqrt(_half_mean(x * x, lo) + EPS)
    return x * r * g2


def _head_norm_bwd(x, g2, dy):
    lo = _lo_mask(x.shape)
    r = lax.rsqrt(_half_mean(x * x, lo) + EPS)
    xh = x * r
    dxh = dy * g2
    dx = r * (dxh - xh * _half_mean(dxh * xh, lo))
    return dx, dy * xh


def _prep_a_fwd(z, gq2, gk2, gqa, gkva, cos_a, sin_a, cos_b, sin_b):
    t = z.shape[0]
    tm = _tile(t, ROW_TILE)

    def body(z_ref, gq_ref, gk_ref, gqa_ref, gkva_ref, ca_ref, sa_ref, cb_ref, sb_ref,
             qa_ref, ka_ref, va_ref, cqn_ref, ckvn_ref, krr_ref):
        def zf(lo, hi):
            return z_ref[:, lo:hi].astype(F32)

        for j in range(4):
            cols = slice(LANES * j, LANES * (j + 1))
            y = _rope(_head_norm(zf(LANES * j, LANES * (j + 1)), gq_ref[...]), ca_ref[:, cols], sa_ref[:, cols], 16)
            qa_ref[:, cols] = (y * (GQA_SCALE * LOG2E)).astype(BF16)
        y = _rope(_head_norm(zf(Z_KA, Z_VA), gk_ref[...]), ca_ref[:, :LANES], sa_ref[:, :LANES], 16)
        ka_ref[...] = y.astype(BF16)
        va_ref[...] = z_ref[:, Z_VA:Z_CQ].astype(BF16)
        cq = zf(Z_CQ, Z_CKV)
        cqn_ref[...] = (cq * _rinv(cq) * gqa_ref[...]).astype(BF16)
        ckv = zf(Z_CKV, Z_KR)
        ckvn_ref[...] = (ckv * _rinv(ckv) * gkva_ref[...]).astype(BF16)
        krr_ref[...] = _rope(zf(Z_KR, Z_GATE), cb_ref[...], sb_ref[...], 8)

    return pl.pallas_call(
        body, name="prep_a_fwd", grid=(t // tm,),
        in_specs=[_row_spec(tm, Z_ATT_W), _vec_spec(LANES), _vec_spec(LANES), _vec_spec(MLA_Q_RANK),
                  _vec_spec(MLA_KV_RANK), _row_spec(tm, 512), _row_spec(tm, 512), _row_spec(tm, LANES),
                  _row_spec(tm, LANES)],
        out_specs=[_row_spec(tm, 512), _row_spec(tm, LANES), _row_spec(tm, LANES), _row_spec(tm, MLA_Q_RANK),
                   _row_spec(tm, MLA_KV_RANK), _row_spec(tm, LANES)],
        out_shape=[jax.ShapeDtypeStruct((t, 512), BF16), jax.ShapeDtypeStruct((t, LANES), BF16),
                   jax.ShapeDtypeStruct((t, LANES), BF16), jax.ShapeDtypeStruct((t, MLA_Q_RANK), BF16),
                   jax.ShapeDtypeStruct((t, MLA_KV_RANK), BF16), jax.ShapeDtypeStruct((t, LANES), F32)],
        compiler_params=_params("parallel"),
    )(z, gq2, gk2, gqa, gkva, cos_a, sin_a, cos_b, sin_b)


def _prep_a_bwd(z, dqa, dka4, dva4, dcqn, dckvn, dkr, dzga, dzgb, gq2, gk2, gqa, gkva, cos_a, sin_a):
    t = z.shape[0]
    tm = _tile(t, PREP_ROWS)

    def body(z_ref, dqa_ref, dka_ref, dva_ref, dcqn_ref, dckvn_ref, dkr_ref, dzga_ref, dzgb_ref, gq_ref, gk_ref,
             gqa_ref, gkva_ref, ca_ref, sa_ref, dz_ref, dgq_ref, dgk_ref, dgqa_ref, dgkva_ref):
        @pl.when(pl.program_id(0) == 0)
        def _():
            dgq_ref[...] = jnp.zeros_like(dgq_ref)
            dgk_ref[...] = jnp.zeros_like(dgk_ref)
            dgqa_ref[...] = jnp.zeros_like(dgqa_ref)
            dgkva_ref[...] = jnp.zeros_like(dgkva_ref)

        def zf(lo, hi):
            return z_ref[:, lo:hi].astype(F32)

        dgq = jnp.zeros((SUBLANES, LANES), F32)
        for j in range(4):
            cols = slice(LANES * j, LANES * (j + 1))
            dy = _rope_bwd(dqa_ref[:, cols] * GQA_SCALE, ca_ref[:, cols], sa_ref[:, cols], 16)
            dx, dgc = _head_norm_bwd(zf(LANES * j, LANES * (j + 1)), gq_ref[...], dy)
            dz_ref[:, cols] = dx.astype(BF16)
            dgq = dgq + _fold8(dgc)
        dgq_ref[...] += dgq
        dk = (dka_ref[0] + dka_ref[1] + dka_ref[2] + dka_ref[3]).T * LN2
        dy = _rope_bwd(dk, ca_ref[:, :LANES], sa_ref[:, :LANES], 16)
        dx, dgc = _head_norm_bwd(zf(Z_KA, Z_VA), gk_ref[...], dy)
        dz_ref[:, Z_KA:Z_VA] = dx.astype(BF16)
        dgk_ref[...] += _fold8(dgc)
        dz_ref[:, Z_VA:Z_CQ] = (dva_ref[0] + dva_ref[1] + dva_ref[2] + dva_ref[3]).T.astype(BF16)
        dx, dgc = _rms_bwd_rows(zf(Z_CQ, Z_CKV), gqa_ref[...], dcqn_ref[...].astype(F32))
        dz_ref[:, Z_CQ:Z_CKV] = dx.astype(BF16)
        dgqa_ref[...] += _fold8(dgc)
        dx, dgc = _rms_bwd_rows(zf(Z_CKV, Z_KR), gkva_ref[...], dckvn_ref[...].astype(F32))
        dz_ref[:, Z_CKV:Z_KR] = dx.astype(BF16)
        dgkva_ref[...] += _fold8(dgc)
        dz_ref[:, Z_KR:Z_GATE] = dkr_ref[...].astype(BF16)
        dz_ref[:, Z_GATE:Z_GATE + D_MODEL] = dzga_ref[...]
        dz_ref[:, Z_GATE + D_MODEL:Z_W] = dzgb_ref[...]

    part = pl.BlockSpec((4, LANES, tm), lambda i: (0, 0, i))
    return pl.pallas_call(
        body, name="prep_a_bwd", grid=(t // tm,),
        in_specs=[_row_spec(tm, Z_ATT_W), _row_spec(tm, 512), part, part, _row_spec(tm, MLA_Q_RANK),
                  _row_spec(tm, MLA_KV_RANK), _row_spec(tm, LANES), _row_spec(tm, D_MODEL), _row_spec(tm, D_MODEL),
                  _vec_spec(LANES),
                  _vec_spec(LANES), _vec_spec(MLA_Q_RANK), _vec_spec(MLA_KV_RANK), _row_spec(tm, 512),
                  _row_spec(tm, 512)],
        out_specs=[_row_spec(tm, Z_W), _acc_spec(LANES), _acc_spec(LANES), _acc_spec(MLA_Q_RANK),
                   _acc_spec(MLA_KV_RANK)],
        out_shape=[jax.ShapeDtypeStruct((t, Z_W), BF16), jax.ShapeDtypeStruct((SUBLANES, LANES), F32),
                   jax.ShapeDtypeStruct((SUBLANES, LANES), F32), jax.ShapeDtypeStruct((SUBLANES, MLA_Q_RANK), F32),
                   jax.ShapeDtypeStruct((SUBLANES, MLA_KV_RANK), F32)],
        compiler_params=_params("arbitrary"),
    )(z, dqa, dka4, dva4, dcqn, dckvn, dkr, dzga, dzgb, gq2, gk2, gqa, gkva, cos_a, sin_a)


def _prep_b_fwd(qb, kvb, krr, cos_b, sin_b):
    t = qb.shape[0]
    tm = _tile(t, ROW_TILE)

    def body(qb_ref, kvb_ref, krr_ref, cb_ref, sb_ref, q_ref, k_ref, v_ref):
        for h in range(MLA_HEADS):
            cols = slice(LANES * h, LANES * (h + 1))
            qh = _rope(qb_ref[:, cols].astype(F32), cb_ref[...], sb_ref[...], 8)
            q_ref[:, cols] = (qh * (MLA_SCALE * LOG2E)).astype(BF16)
            k_ref[:, cols] = (kvb_ref[:, cols].astype(F32) + krr_ref[...]).astype(BF16)
        v_ref[...] = kvb_ref[:, 1024:1536].astype(BF16)

    return pl.pallas_call(
        body, name="prep_b_fwd", grid=(t // tm,),
        in_specs=[_row_spec(tm, 1024), _row_spec(tm, 1536), _row_spec(tm, LANES), _row_spec(tm, LANES),
                  _row_spec(tm, LANES)],
        out_specs=[_row_spec(tm, 1024), _row_spec(tm, 1024), _row_spec(tm, 512)],
        out_shape=[jax.ShapeDtypeStruct((t, 1024), BF16), jax.ShapeDtypeStruct((t, 1024), BF16),
                   jax.ShapeDtypeStruct((t, 512), BF16)],
        compiler_params=_params("parallel"),
    )(qb, kvb, krr, cos_b, sin_b)


def _prep_b_bwd(dq, dk, dv, cos_b, sin_b):
    t = dq.shape[0]
    tm = _tile(t, ROW_TILE)

    def body(dq_ref, dk_ref, dv_ref, cb_ref, sb_ref, dqb_ref, dkvb_ref, dkr_ref):
        dkr = jnp.zeros((tm, LANES), F32)
        for h in range(MLA_HEADS):
            cols = slice(LANES * h, LANES * (h + 1))
            dqb_ref[:, cols] = _rope_bwd(dq_ref[:, cols] * MLA_SCALE, cb_ref[...], sb_ref[...], 8).astype(BF16)
            dkh = dk_ref[cols, :].T * LN2
            dkvb_ref[:, cols] = dkh.astype(BF16)
            dkr = dkr + dkh
        for j in range(MLA_HEADS // 2):
            dkvb_ref[:, 1024 + LANES * j:1024 + LANES * (j + 1)] = dv_ref[LANES * j:LANES * (j + 1), :].T.astype(BF16)
        dkr_ref[...] = _rope_bwd(dkr, cb_ref[...], sb_ref[...], 8)

    return pl.pallas_call(
        body, name="prep_b_bwd", grid=(t // tm,),
        in_specs=[_row_spec(tm, 1024), pl.BlockSpec((1024, tm), lambda i: (0, i)),
                  pl.BlockSpec((512, tm), lambda i: (0, i)), _row_spec(tm, LANES),
                  _row_spec(tm, LANES)],
        out_specs=[_row_spec(tm, 1024), _row_spec(tm, 1536), _row_spec(tm, LANES)],
        out_shape=[jax.ShapeDtypeStruct((t, 1024), BF16), jax.ShapeDtypeStruct((t, 1536), BF16),
                   jax.ShapeDtypeStruct((t, LANES), F32)],
        compiler_params=_params("parallel"),
    )(dq, dk, dv, cos_b, sin_b)


_NT = (((1,), (1,)), ((), ()))
_NN = (((1,), (0,)), ((), ()))
_TN = (((0,), (0,)), ((), ()))


def _head_operands(qv, kv, i, shared_k):
    if shared_k:
        lo = _lo_mask(qv.shape)
        keep = lo if i == 0 else jnp.logical_not(lo)
        return jnp.where(keep, qv, jnp.zeros_like(qv)), kv
    cols = slice(LANES * i, LANES * (i + 1))
    return qv[:, cols], kv[:, cols]


def _attn_specs(shared_k, tq, tk, q_of, k_of):
    wq = LANES if shared_k else 2 * LANES
    q_spec = pl.BlockSpec((tq, wq), lambda *g: (q_of(*g), g[0]))
    if shared_k:
        k_spec = pl.BlockSpec((tk, LANES), lambda *g: (k_of(*g), 0))
        v_spec = pl.BlockSpec((tk, LANES), lambda *g: (k_of(*g), 0))
    else:
        k_spec = pl.BlockSpec((tk, wq), lambda *g: (k_of(*g), g[0]))
        v_spec = pl.BlockSpec((tk, LANES), lambda *g: (k_of(*g), g[0]))
    return wq, q_spec, k_spec, v_spec


def _attn_fwd(q, k, v, shared_k, name):
    t = q.shape[0]
    tq, tk = _tile(t, ATTN_TQ), _tile(t, ATTN_TK)
    nq, nk = t // tq, t // tk
    wq, q_spec, k_spec, v_spec = _attn_specs(shared_k, tq, tk, lambda p, i, j: i, lambda p, i, j: j)
    groups = q.shape[1] // wq
    chunk = _tile(tq, 2 * LANES)

    def body(q_ref, k_ref, v_ref, o_ref, lse_ref, m_s, acc_s, alpha_s, s_s, p_s):
        kb = pl.program_id(2)

        @pl.when(kb == 0)
        def _():
            m_s[...] = jnp.full_like(m_s, -jnp.inf)
            acc_s[...] = jnp.zeros_like(acc_s)

        qv, kv, vv = q_ref[...], k_ref[...], v_ref[...]
        lo = _lo_mask(vv.shape)
        for i in range(2):
            qi, ki = _head_operands(qv, kv, i, shared_k)
            s_s[i] = lax.dot_general(ki, qi, _NT, preferred_element_type=F32)
        for i in range(2):
            for c in range(tq // chunk):
                cols = slice(c * chunk, (c + 1) * chunk)
                m_prev = m_s[i, :, cols]
                m_new = jnp.maximum(m_prev, jnp.max(s_s[i, :, cols], axis=0, keepdims=True))
                alpha_s[i, :, cols] = jnp.exp2(m_prev - m_new)
                m_s[i, :, cols] = m_new
                p_s[i, :, cols] = jnp.exp2(s_s[i, :, cols] - m_new).astype(BF16)
        for i in range(2):
            keep = lo if i == 0 else jnp.logical_not(lo)
            vi = jnp.where(keep, vv, jnp.ones_like(vv))
            acc_s[i] = alpha_s[i] * acc_s[i] + lax.dot_general(vi, p_s[i], _TN, preferred_element_type=F32)

        @pl.when(kb == nk - 1)
        def _():
            a0, a1 = acc_s[0], acc_s[1]
            l0 = a0[LANES - SUBLANES:, :][0:1, :]
            l1 = a1[0:SUBLANES, :][0:1, :]
            row_lo = lax.broadcasted_iota(jnp.int32, a0.shape, 0) < HEAD_DIM
            o_ref[...] = jnp.where(row_lo, a0 / l0, a1 / l1).T.astype(BF16)
            lse_ref[0] = jnp.broadcast_to(m_s[0] + jnp.log2(l0), (LANES, tq)).T
            lse_ref[1] = jnp.broadcast_to(m_s[1] + jnp.log2(l1), (LANES, tq)).T

    return pl.pallas_call(
        body, name=name, grid=(groups, nq, nk),
        in_specs=[q_spec, k_spec, v_spec],
        out_specs=[pl.BlockSpec((tq, LANES), lambda p, i, j: (i, p)),
                   pl.BlockSpec((2, tq, LANES), lambda p, i, j: (p, i, 0))],
        out_shape=[jax.ShapeDtypeStruct((t, LANES * groups), BF16),
                   jax.ShapeDtypeStruct((2 * groups, t, LANES), F32)],
        scratch_shapes=[pltpu.VMEM((2, 1, tq), F32), pltpu.VMEM((2, LANES, tq), F32), pltpu.VMEM((2, 1, tq), F32),
                        pltpu.VMEM((2, tk, tq), F32), pltpu.VMEM((2, tk, tq), BF16)],
        compiler_params=_params("parallel", "parallel", "arbitrary"),
    )(q, k, v)


def _attn_bwd(q, k, v, do, lse, delta, shared_k, name):
    t = q.shape[0]
    tq, tk = _tile(t, ATTN_TQ), _tile(t, ATTN_TK)
    nq, nk = t // tq, t // tk
    wq, q_spec, k_spec, v_spec = _attn_specs(shared_k, tq, tk, lambda p, j, i: i, lambda p, j, i: j)
    groups = q.shape[1] // wq

    def body(q_ref, k_ref, v_ref, do_ref, lse_ref, delta_ref, dq_ref, dk_ref, dv_ref, dk_s, dv_s, s_s, dp_s, p_s,
             ds_s):
        kb, qb = pl.program_id(1), pl.program_id(2)

        @pl.when(qb == 0)
        def _():
            dk_s[...] = jnp.zeros_like(dk_s)
            dv_s[...] = jnp.zeros_like(dv_s)

        qv, kv, vv, dov = q_ref[...], k_ref[...], v_ref[...], do_ref[...]
        lo = _lo_mask(dov.shape)
        heads = []
        for i in range(2):
            qi, ki = _head_operands(qv, kv, i, shared_k)
            keep = lo if i == 0 else jnp.logical_not(lo)
            doi = jnp.where(keep, dov, jnp.zeros_like(dov))
            heads.append((qi, ki, doi))
            s_s[i] = lax.dot_general(qi, ki, _NT, preferred_element_type=F32)
            dp_s[i] = lax.dot_general(doi, vv, _NT, preferred_element_type=F32)
        for i in range(2):
            lse_i, delta_i = lse_ref[i], delta_ref[i]
            for c in range(tk // LANES):
                cols = slice(c * LANES, (c + 1) * LANES)
                p = jnp.exp2(s_s[i, :, cols] - lse_i)
                p_s[i, :, cols] = p.astype(BF16)
                ds_s[i, :, cols] = (p * (dp_s[i, :, cols] - delta_i)).astype(BF16)
        dq_parts = []
        for i in range(2):
            qi, ki, doi = heads[i]
            dv_s[...] += lax.dot_general(doi, p_s[i], _TN, preferred_element_type=F32)
            dk_i = lax.dot_general(qi, ds_s[i], _TN, preferred_element_type=F32)
            if shared_k:
                dk_s[...] += dk_i
            else:
                dk_s[LANES * i:LANES * (i + 1), :] += dk_i
            dq_parts.append(lax.dot_general(ds_s[i], ki, _NN, preferred_element_type=F32))
        rows = pl.ds(pl.multiple_of(qb * tq, tq), tq)
        if shared_k:
            tiles = [(slice(0, LANES), jnp.where(lo, dq_parts[0], dq_parts[1]))]
        else:
            tiles = [(slice(0, LANES), dq_parts[0]), (slice(LANES, 2 * LANES), dq_parts[1])]
        for cols, val in tiles:
            @pl.when(kb == 0)
            def _(cols=cols, val=val):
                dq_ref[rows, cols] = val

            @pl.when(kb > 0)
            def _(cols=cols, val=val):
                dq_ref[rows, cols] += val

        @pl.when(qb == nq - 1)
        def _():
            if shared_k:
                dk_ref[0] = dk_s[...]
                dv_ref[0] = dv_s[...]
            else:
                dk_ref[...] = dk_s[...]
                dv_ref[...] = dv_s[...]

    stat_spec = pl.BlockSpec((2, tq, LANES), lambda p, j, i: (p, i, 0))
    do_spec = pl.BlockSpec((tq, LANES), lambda p, j, i: (i, p))
    dq_spec = pl.BlockSpec((t, wq), lambda p, j, i: (0, p))
    if shared_k:
        dk_spec = pl.BlockSpec((1, LANES, tk), lambda p, j, i: (p, 0, j))
        dv_spec = dk_spec
        dk_shape = jax.ShapeDtypeStruct((groups, LANES, t), F32)
        dv_shape = dk_shape
    else:
        dk_spec = pl.BlockSpec((wq, tk), lambda p, j, i: (p, j))
        dv_spec = pl.BlockSpec((LANES, tk), lambda p, j, i: (p, j))
        dk_shape = jax.ShapeDtypeStruct((wq * groups, t), F32)
        dv_shape = jax.ShapeDtypeStruct((LANES * groups, t), F32)
    return pl.pallas_call(
        body, name=name, grid=(groups, nk, nq),
        in_specs=[q_spec, k_spec, v_spec, do_spec, stat_spec, stat_spec],
        out_specs=[dq_spec, dk_spec, dv_spec],
        out_shape=[jax.ShapeDtypeStruct((t, wq * groups), F32), dk_shape, dv_shape],
        scratch_shapes=[pltpu.VMEM((wq, tk), F32), pltpu.VMEM((LANES, tk), F32), pltpu.VMEM((2, tq, tk), F32),
                        pltpu.VMEM((2, tq, tk), F32), pltpu.VMEM((2, tq, tk), BF16), pltpu.VMEM((2, tq, tk), BF16)],
        compiler_params=_params("parallel", "arbitrary", "arbitrary"),
    )(q, k, v, do, lse, delta)


_MERGE_W = 512
_GATE_BLK0 = Z_GATE // _MERGE_W


def _merge_fwd(z, b_gate, ta, tb):
    t = z.shape[0]
    tm = _tile(t, ROW_TILE)
    w = _MERGE_W
    nj = D_MODEL // w

    def body(za_ref, zb_ref, ba_ref, bb_ref, ta_ref, tb_ref, o_ref):
        ga = jax.nn.sigmoid(za_ref[...].astype(F32) + ba_ref[...])
        gb = jax.nn.sigmoid(zb_ref[...].astype(F32) + bb_ref[...])
        o_ref[...] = (ga * ta_ref[...].astype(F32) + gb * tb_ref[...].astype(F32)).astype(BF16)

    return pl.pallas_call(
        body, name="merge_fwd", grid=(t // tm, nj),
        in_specs=[pl.BlockSpec((tm, w), lambda i, j: (i, _GATE_BLK0 + j)),
                  pl.BlockSpec((tm, w), lambda i, j: (i, _GATE_BLK0 + nj + j)),
                  pl.BlockSpec((1, w), lambda i, j: (0, j)),
                  pl.BlockSpec((1, w), lambda i, j: (0, nj + j)),
                  pl.BlockSpec((tm, w), lambda i, j: (i, j)),
                  pl.BlockSpec((tm, w), lambda i, j: (i, j))],
        out_specs=pl.BlockSpec((tm, w), lambda i, j: (i, j)),
        out_shape=jax.ShapeDtypeStruct((t, D_MODEL), BF16),
        compiler_params=_params("parallel", "parallel"),
    )(z, z, b_gate, b_gate, ta, tb)


def _merge_bwd(dmg, z, b_gate, ta, tb):
    t = z.shape[0]
    tm = _tile(t, ROW_TILE)
    w = _MERGE_W
    nj = D_MODEL // w

    def body(dm_ref, za_ref, zb_ref, ba_ref, bb_ref, ta_ref, tb_ref, dta_ref, dtb_ref, dza_ref, dzb_ref,
             dba_ref, dbb_ref):
        dm = dm_ref[...].astype(F32)
        ga = jax.nn.sigmoid(za_ref[...].astype(F32) + ba_ref[...])
        gb = jax.nn.sigmoid(zb_ref[...].astype(F32) + bb_ref[...])
        dta_ref[...] = (dm * ga).astype(BF16)
        dtb_ref[...] = (dm * gb).astype(BF16)
        dza = dm * ta_ref[...].astype(F32) * ga * (1.0 - ga)
        dzb = dm * tb_ref[...].astype(F32) * gb * (1.0 - gb)
        dza_ref[...] = dza.astype(BF16)
        dzb_ref[...] = dzb.astype(BF16)

        @pl.when(pl.program_id(1) == 0)
        def _():
            dba_ref[...] = jnp.zeros_like(dba_ref)
            dbb_ref[...] = jnp.zeros_like(dbb_ref)

        dba_ref[...] += _fold8(dza)
        dbb_ref[...] += _fold8(dzb)

    blk = pl.BlockSpec((tm, w), lambda j, i: (i, j))
    acc = pl.BlockSpec((SUBLANES, w), lambda j, i: (0, j))
    return pl.pallas_call(
        body, name="merge_bwd", grid=(nj, t // tm),
        in_specs=[blk,
                  pl.BlockSpec((tm, w), lambda j, i: (i, _GATE_BLK0 + j)),
                  pl.BlockSpec((tm, w), lambda j, i: (i, _GATE_BLK0 + nj + j)),
                  pl.BlockSpec((1, w), lambda j, i: (0, j)),
                  pl.BlockSpec((1, w), lambda j, i: (0, nj + j)),
                  blk, blk],
        out_specs=[blk, blk, blk, blk, acc, acc],
        out_shape=[jax.ShapeDtypeStruct((t, D_MODEL), BF16)] * 4 + [jax.ShapeDtypeStruct((SUBLANES, D_MODEL), F32)] * 2,
        compiler_params=_params("parallel", "arbitrary"),
    )(dmg, z, z, b_gate, b_gate, ta, tb)


def _loss_grad(y, target):
    t, d = y.shape
    tm = _tile(t, ROW_TILE)

    def body(y_ref, t_ref, dy_ref, acc_ref):
        err = y_ref[...] - t_ref[...]
        dy_ref[...] = err * (1.0 / d)
        e8 = _fold8(err * err)
        part = e8[:, 0:LANES]
        for c in range(1, d // LANES):
            part = part + e8[:, LANES * c:LANES * (c + 1)]

        @pl.when(pl.program_id(0) == 0)
        def _():
            acc_ref[...] = jnp.zeros_like(acc_ref)

        acc_ref[...] += part

    return pl.pallas_call(
        body, name="loss_grad", grid=(t // tm,),
        in_specs=[_row_spec(tm, d), _row_spec(tm, d)],
        out_specs=[_row_spec(tm, d), _acc_spec(LANES)],
        out_shape=[jax.ShapeDtypeStruct((t, d), F32), jax.ShapeDtypeStruct((SUBLANES, LANES), F32)],
        compiler_params=_params("arbitrary"),
    )(y, target)


_MESH_ID = pl.DeviceIdType.MESH
_ANY = pl.BlockSpec(memory_space=pl.ANY)


def _all_gather(arrays):
    n = len(arrays)
    halves = []
    for a in arrays:
        assert a.shape[0] % 2 == 0, a.shape
        halves.append((pl.ds(0, a.shape[0] // 2), pl.ds(a.shape[0] // 2, a.shape[0] // 2)))
    OWN_SIB, OWN_X, OWN_Y, FWD_X, FWD_Y, SIB_X, SIB_Y, SIB_DA, SIB_DB = range(9)

    def body(*refs):
        x_refs, out_refs = refs[:n], refs[n:2 * n]
        send_sems, recv_sems, local_sems = refs[2 * n:]
        mx, my, mc = lax.axis_index("x"), lax.axis_index("y"), lax.axis_index("c")
        me, sibling = (mx, my, mc), (mx, my, 1 - mc)
        x_nbr, y_nbr, diag = (1 - mx, my, mc), (mx, 1 - my, mc), (1 - mx, 1 - my, mc)

        def slot(a, dev, rows=None):
            px, py, pc = dev
            ref = out_refs[a].at[4 * px + 2 * py + pc]
            return ref if rows is None else ref.at[rows]

        def other_core(dev):
            return (dev[0], dev[1], 1 - dev[2])

        def copy(a, sem, block, to, rows=None, src=None):
            return pltpu.make_async_remote_copy(
                src_ref=slot(a, block, rows) if src is None else src, dst_ref=slot(a, block, rows),
                send_sem=send_sems.at[a, sem], recv_sem=recv_sems.at[a, sem], device_id=to, device_id_type=_MESH_ID)

        mine = [pltpu.make_async_copy(x_refs[a], slot(a, me), local_sems.at[a]) for a in range(n)]
        sent = []
        for a in range(n):
            mine[a].start()
            sent += [copy(a, OWN_SIB, me, sibling, src=x_refs[a]), copy(a, OWN_X, me, x_nbr, src=x_refs[a]),
                     copy(a, OWN_Y, me, y_nbr, src=x_refs[a])]
        for cp in sent:
            cp.start()
        for a in range(n):
            first, second = halves[a]
            copy(a, OWN_Y, y_nbr, me).wait_recv()
            sent += [copy(a, FWD_X, y_nbr, x_nbr, rows=first), copy(a, SIB_Y, y_nbr, sibling)]
            sent[-2].start()
            sent[-1].start()
            copy(a, OWN_X, x_nbr, me).wait_recv()
            sent += [copy(a, FWD_Y, x_nbr, y_nbr, rows=second), copy(a, SIB_X, x_nbr, sibling)]
            sent[-2].start()
            sent[-1].start()
        for a in range(n):
            first, second = halves[a]
            copy(a, FWD_X, diag, me, rows=first).wait_recv()
            sent.append(copy(a, SIB_DA, diag, sibling, rows=first))
            sent[-1].start()
            copy(a, FWD_Y, diag, me, rows=second).wait_recv()
            sent.append(copy(a, SIB_DB, diag, sibling, rows=second))
            sent[-1].start()
        for a in range(n):
            first, second = halves[a]
            copy(a, OWN_SIB, sibling, me).wait_recv()
            copy(a, SIB_X, other_core(x_nbr), me).wait_recv()
            copy(a, SIB_Y, other_core(y_nbr), me).wait_recv()
            copy(a, SIB_DA, other_core(diag), me, rows=first).wait_recv()
            copy(a, SIB_DB, other_core(diag), me, rows=second).wait_recv()
        for cp in sent:
            cp.wait_send()
        for cp in mine:
            cp.wait()

    return pl.pallas_call(
        body, name="weight_all_gather",
        out_shape=[jax.ShapeDtypeStruct((N_DEV,) + a.shape, a.dtype) for a in arrays],
        in_specs=[_ANY] * n, out_specs=[_ANY] * n,
        scratch_shapes=[pltpu.SemaphoreType.DMA((n, 9)), pltpu.SemaphoreType.DMA((n, 9)),
                        pltpu.SemaphoreType.DMA((n,))],
    )(*arrays)


def _pair_exchange(sends):
    n = len(sends)

    def body(*refs):
        s_refs, r_refs = refs[:n], refs[n:2 * n]
        send_sems, recv_sems = refs[2 * n:]
        mx, my, mc = lax.axis_index("x"), lax.axis_index("y"), lax.axis_index("c")
        copies = []
        for a in range(n):
            for ch in range(4):
                cp = pltpu.make_async_remote_copy(
                    src_ref=s_refs[a].at[2 * ch + (1 - mc)], dst_ref=r_refs[a].at[ch], send_sem=send_sems.at[a, ch],
                    recv_sem=recv_sems.at[a, ch], device_id=(mx, my, 1 - mc), device_id_type=_MESH_ID)
                cp.start()
                copies.append(cp)
        for cp in copies:
            cp.wait_send()
            cp.wait_recv()

    return pl.pallas_call(
        body, name="grad_pair_exchange",
        out_shape=[jax.ShapeDtypeStruct((4,) + s.shape[1:], s.dtype) for s in sends],
        in_specs=[_ANY] * n, out_specs=[_ANY] * n,
        scratch_shapes=[pltpu.SemaphoreType.DMA((n, 4)), pltpu.SemaphoreType.DMA((n, 4))],
    )(*sends)


def _pair_add(send, half, core):
    _, r, c_ = send.shape
    tr = _row_tile(r, c_)

    def body(core_ref, s_ref, h_ref, o_ref):
        del core_ref
        o_ref[...] = (s_ref[...] + h_ref[...]).astype(BF16)

    blk = pl.BlockSpec((1, tr, c_), lambda ch, i, core_ref: (ch, i, 0))
    return pl.pallas_call(
        body, name="grad_pair_add",
        grid_spec=pltpu.PrefetchScalarGridSpec(
            num_scalar_prefetch=1, grid=(4, r // tr),
            in_specs=[pl.BlockSpec((1, tr, c_), lambda ch, i, core_ref: (2 * ch + core_ref[0], i, 0)), blk],
            out_specs=blk),
        out_shape=jax.ShapeDtypeStruct((4, r, c_), BF16),
        compiler_params=_params("parallel", "parallel"),
    )(core, send, half)


def _chip_exchange(parts):
    n = len(parts)

    def body(*refs):
        p_refs, r_refs = refs[:n], refs[n:2 * n]
        send_sems, recv_sems, local_sems = refs[2 * n:]
        mx, my, mc = lax.axis_index("x"), lax.axis_index("y"), lax.axis_index("c")
        mine = 2 * mx + my
        local = [pltpu.make_async_copy(p_refs[a].at[mine], r_refs[a].at[mine], local_sems.at[a]) for a in range(n)]
        copies = []
        for a in range(n):
            local[a].start()
            for rel in range(1, 4):
                px = 1 - mx if rel & 2 else mx
                py = 1 - my if rel & 1 else my
                cp = pltpu.make_async_remote_copy(
                    src_ref=p_refs[a].at[2 * px + py], dst_ref=r_refs[a].at[mine], send_sem=send_sems.at[a, rel - 1],
                    recv_sem=recv_sems.at[a, rel - 1], device_id=(px, py, mc), device_id_type=_MESH_ID)
                cp.start()
                copies.append(cp)
        for cp in copies:
            cp.wait_send()
            cp.wait_recv()
        for cp in local:
            cp.wait()

    return pl.pallas_call(
        body, name="grad_chip_exchange",
        out_shape=[jax.ShapeDtypeStruct(p.shape, p.dtype) for p in parts],
        in_specs=[_ANY] * n, out_specs=[_ANY] * n,
        scratch_shapes=[pltpu.SemaphoreType.DMA((n, 3)), pltpu.SemaphoreType.DMA((n, 3)),
                        pltpu.SemaphoreType.DMA((n,))],
    )(*parts)


def _row_tile(r, c_):
    tr = min(r, ADAM_BLOCK_ELEMS // (pl.cdiv(c_, LANES) * LANES))
    while r % tr:
        tr -= SUBLANES
    return tr


def _adamw(recv, w, m, v):
    r, c_ = w.shape
    tr = _row_tile(r, c_)
    n_src = recv.shape[0]

    def body(g_ref, w_ref, m_ref, v_ref, go_ref, d_ref, mo_ref, vo_ref):
        g = g_ref[0].astype(F32)
        for s in range(1, n_src):
            g = g + g_ref[s].astype(F32)
        go_ref[...] = g
        mn = ADAM_B1 * m_ref[...] + (1.0 - ADAM_B1) * g
        vn = ADAM_B2 * v_ref[...] + (1.0 - ADAM_B2) * (g * g)
        mo_ref[...] = mn
        vo_ref[...] = vn
        m_hat = mn / (1.0 - ADAM_B1 ** ADAM_STEP)
        v_hat = vn / (1.0 - ADAM_B2 ** ADAM_STEP)
        d_ref[...] = -ADAM_LR * (m_hat / (jnp.sqrt(v_hat) + ADAM_EPS) + ADAM_WD * w_ref[...])

    spec = pl.BlockSpec((tr, c_), lambda i: (i, 0))
    out = jax.ShapeDtypeStruct((r, c_), F32)
    return pl.pallas_call(
        body, name="grad_sum_adamw", grid=(r // tr,),
        in_specs=[pl.BlockSpec((n_src, tr, c_), lambda i: (0, i, 0)), spec, spec, spec],
        out_specs=[spec, spec, spec, spec], out_shape=[out, out, out, out],
        compiler_params=_params("parallel"),
    )(recv, w, m, v)


def _pad_cols(a, before, after):
    parts = []
    if before:
        parts.append(jnp.zeros(a.shape[:-1] + (before,), a.dtype))
    parts.append(a)
    if after:
        parts.append(jnp.zeros(a.shape[:-1] + (after,), a.dtype))
    return jnp.concatenate(parts, axis=-1)


def _q_head_pairs(a, axis):
    shp = a.shape
    a = a.reshape(shp[:axis] + (GQA_KV_HEADS, GQA_GROUP, HEAD_DIM) + shp[axis + 1:])
    a = jnp.swapaxes(a, axis, axis + 1)
    return a.reshape(shp)


def _q_head_unpairs(a, axis):
    shp = a.shape
    a = a.reshape(shp[:axis] + (GQA_GROUP, GQA_KV_HEADS, HEAD_DIM) + shp[axis + 1:])
    a = jnp.swapaxes(a, axis, axis + 1)
    return a.reshape(shp)


def _layout_weights(w):
    w_in = w["w_in"]
    lead = w_in.shape[:-1]
    w_in_p = jnp.concatenate([
        _q_head_pairs(w_in[..., 0:512], w_in.ndim - 1),
        w_in[..., 512:1408],
        _pad_cols(w_in[..., 1408:1440], KR_LANE0, LANES - KR_LANE0 - MLA_ROPE_DIM),
        w_in[..., 1440:],
    ], axis=-1)
    wq = w["w_q_up"]
    wq_p = _pad_cols(wq.reshape(wq.shape[:-1] + (MLA_HEADS, MLA_QK_DIM)), 0, LANES - MLA_QK_DIM)
    wq_p = wq_p.reshape(wq.shape[:-1] + (MLA_HEADS * LANES,))
    wkv = w["w_kv_up"]
    wkv4 = wkv.reshape(wkv.shape[:-1] + (MLA_HEADS, 2 * HEAD_DIM))
    wk_p = _pad_cols(wkv4[..., :HEAD_DIM], 0, LANES - HEAD_DIM).reshape(wkv.shape[:-1] + (MLA_HEADS * LANES,))
    wv_p = wkv4[..., HEAD_DIM:].reshape(wkv.shape[:-1] + (MLA_HEADS * HEAD_DIM,))
    del lead
    return {
        "w_in": w_in_p, "w_q_up": wq_p, "w_kv_up": jnp.concatenate([wk_p, wv_p], axis=-1),
        "w_branch_a": _q_head_pairs(w["w_branch_a"], w["w_branch_a"].ndim - 2), "w_branch_b": w["w_branch_b"],
        "w_o": w["w_o"], "w_ffn_up": w["w_ffn_up"], "w_ffn_down": w["w_ffn_down"],
    }


def _unlayout_grads(g):
    gi = g["w_in"]
    kr0 = Z_KR + KR_LANE0
    g_in = jnp.concatenate([
        _q_head_unpairs(gi[..., 0:512], gi.ndim - 1), gi[..., 512:1408], gi[..., kr0:kr0 + MLA_ROPE_DIM],
        gi[..., Z_GATE:],
    ], axis=-1)
    gq = g["w_q_up"]
    gq = gq.reshape(gq.shape[:-1] + (MLA_HEADS, LANES))[..., :MLA_QK_DIM]
    gq = gq.reshape(gq.shape[:-2] + (MLA_HEADS * MLA_QK_DIM,))
    gkv = g["w_kv_up"]
    gk = gkv[..., :MLA_HEADS * LANES].reshape(gkv.shape[:-1] + (MLA_HEADS, LANES))[..., :HEAD_DIM]
    gv = gkv[..., MLA_HEADS * LANES:].reshape(gkv.shape[:-1] + (MLA_HEADS, HEAD_DIM))
    gkv = jnp.concatenate([gk, gv], axis=-1).reshape(gkv.shape[:-1] + (MLA_HEADS * 2 * HEAD_DIM,))
    return {
        "w_in": g_in, "w_q_up": gq, "w_kv_up": gkv,
        "w_branch_a": _q_head_unpairs(g["w_branch_a"], g["w_branch_a"].ndim - 2), "w_branch_b": g["w_branch_b"],
        "w_o": g["w_o"], "w_ffn_up": g["w_ffn_up"], "w_ffn_down": g["w_ffn_down"],
    }


def _pack_small(parts):
    flat = jnp.concatenate([p.reshape(-1) for p in parts])
    pad = (-flat.shape[0]) % (SUBLANES * LANES)
    if pad:
        flat = jnp.concatenate([flat, jnp.zeros((pad,), flat.dtype)])
    return flat.reshape(-1, LANES)


def _unpack_small(packed, shapes):
    flat = packed.reshape(-1)
    out, off = [], 0
    for shp in shapes:
        n = int(np.prod(shp))
        out.append(flat[off:off + n].reshape(shp))
        off += n
    return out


def _shards_of(full, axis):
    shp = full.shape
    cut = shp[:axis] + (N_DEV, shp[axis] // N_DEV) + shp[axis + 1:]
    return jnp.moveaxis(full.reshape(cut), axis, 0)


def _from_shards(shards, axis):
    full = list(shards.shape[1:])
    full[axis] *= N_DEV
    return jnp.moveaxis(shards, 0, axis).reshape(full)


def _rows2d(a):
    return a.reshape(-1, a.shape[-1])


def _layer_fwd(x, u, lw, tabs):
    cos_a, sin_a, cos_b, sin_b = tabs
    z = _matmul(u, lw["w_in"], "nn", "mm_in")
    qa, ka, va, cqn, ckvn, krr = _prep_a_fwd(z, lw["gq2"], lw["gk2"], lw["gqa"], lw["gkva"], cos_a, sin_a, cos_b, sin_b)
    qb = _matmul(cqn, lw["w_q_up"], "nn", "mm_q_up")
    kvb = _matmul(ckvn, lw["w_kv_up"], "nn", "mm_kv_up")
    q_b, k_b, v_b = _prep_b_fwd(qb, kvb, krr, cos_b, sin_b)
    ya, lse_a = _attn_fwd(qa, ka, va, True, "gqa_fwd")
    yb, lse_b = _attn_fwd(q_b, k_b, v_b, False, "mla_fwd")
    ta = _matmul(ya, lw["w_branch_a"], "nn", "mm_branch_a")
    tb = _matmul(yb, lw["w_branch_b"], "nn", "mm_branch_b")
    merged = _merge_fwd(z, lw["b_gate"], ta, tb)
    m = _matmul(merged, lw["w_o"], "nn", "mm_o")
    x2, u2 = _res_norm_fwd(x, m, lw["post_mix_g"], lw["pre_ffn_g"])
    h, a = _matmul(u2, lw["w_ffn_up"], "nn", "mm_ffn_up", post="relu2")
    f = _matmul(a, lw["w_ffn_down"], "nn", "mm_ffn_down")
    x3, u_next = _res_norm_fwd(x2, f, lw["post_ffn_g"], lw["next_pre_mix_g"])
    saved = dict(u=u, z=z, qa=qa, ka=ka, va=va, cqn=cqn, ckvn=ckvn, q_b=q_b, k_b=k_b, v_b=v_b, ya=ya, yb=yb,
                 lse_a=lse_a, lse_b=lse_b, ta=ta, tb=tb, merged=merged, m=m, x2=x2, u2=u2, h=h, a=a, f=f, x3=x3)
    return x3, u_next, saved


def _layer_bwd(dx3, du_next, lw, sv, tabs, gbuf, layer):
    cos_a, sin_a, cos_b, sin_b = tabs
    g = {}
    dx3, df, dg4, dg1n = _res_norm_bwd(sv["x3"], sv["f"], lw["post_ffn_g"], lw["next_pre_mix_g"], dx3, du_next)
    g["post_ffn_g"], g["next_pre_mix_g"] = dg4, dg1n
    dh = _matmul(df, lw["w_ffn_down"], "nt", "mm_d_h", post="relu2_bwd", h=sv["h"])
    g["w_ffn_down"] = _matmul(sv["a"], df, "tn", "mm_dw_ffn_down", stack=(gbuf["w_ffn_down"], layer))
    du2 = _matmul(dh, lw["w_ffn_up"], "nt", "mm_d_u2")
    g["w_ffn_up"] = _matmul(sv["u2"], dh, "tn", "mm_dw_ffn_up", stack=(gbuf["w_ffn_up"], layer))
    dx2, dm, dg2, dg3 = _res_norm_bwd(sv["x2"], sv["m"], lw["post_mix_g"], lw["pre_ffn_g"], dx3, du2)
    g["post_mix_g"], g["pre_ffn_g"] = dg2, dg3
    dmg = _matmul(dm, lw["w_o"], "nt", "mm_d_merged")
    g["w_o"] = _matmul(sv["merged"], dm, "tn", "mm_dw_o", stack=(gbuf["w_o"], layer))
    dta, dtb, dzg_a, dzg_b, db_a, db_b = _merge_bwd(dmg, sv["z"], lw["b_gate"], sv["ta"], sv["tb"])
    g["b_gate"] = jnp.concatenate([db_a, db_b], axis=-1)
    dya, delta_a = _matmul(dta, lw["w_branch_a"], "nt", "mm_d_ya", post="delta", h=sv["ya"])
    g["w_branch_a"] = _matmul(sv["ya"], dta, "tn", "mm_dw_branch_a", stack=(gbuf["w_branch_a"], layer))
    dyb, delta_b = _matmul(dtb, lw["w_branch_b"], "nt", "mm_d_yb", post="delta", h=sv["yb"])
    g["w_branch_b"] = _matmul(sv["yb"], dtb, "tn", "mm_dw_branch_b", stack=(gbuf["w_branch_b"], layer))
    dqa, dka4, dva4 = _attn_bwd(sv["qa"], sv["ka"], sv["va"], dya, sv["lse_a"], delta_a, True, "gqa_bwd")
    dq_b, dk_b, dv_b = _attn_bwd(sv["q_b"], sv["k_b"], sv["v_b"], dyb, sv["lse_b"], delta_b, False, "mla_bwd")
    dqb, dkvb, dkr = _prep_b_bwd(dq_b, dk_b, dv_b, cos_b, sin_b)
    dcqn = _matmul(dqb, lw["w_q_up"], "nt", "mm_d_cqn")
    g["w_q_up"] = _matmul(sv["cqn"], dqb, "tn", "mm_dw_q_up", stack=(gbuf["w_q_up"], layer))
    dckvn = _matmul(dkvb, lw["w_kv_up"], "nt", "mm_d_ckvn")
    g["w_kv_up"] = _matmul(sv["ckvn"], dkvb, "tn", "mm_dw_kv_up", stack=(gbuf["w_kv_up"], layer))
    dz, dgq, dgk, dgqa, dgkva = _prep_a_bwd(sv["z"], dqa, dka4, dva4, dcqn, dckvn, dkr, dzg_a, dzg_b, lw["gq2"],
                                            lw["gk2"], lw["gqa"], lw["gkva"], cos_a, sin_a)
    g["q_norm_g"], g["k_norm_g"], g["q_a_norm_g"], g["kv_a_norm_g"] = dgq, dgk, dgqa, dgkva
    du = _matmul(dz, lw["w_in"], "nt", "mm_d_u")
    g["w_in"] = _matmul(sv["u"], dz, "tn", "mm_dw_in", stack=(gbuf["w_in"], layer))
    return dx2, du, g


def kernel(x, w_in, b_gate, q_norm_g, k_norm_g, q_a_norm_g, kv_a_norm_g, w_q_up, w_kv_up, w_branch_a, w_branch_b, w_o, w_ffn_up, w_ffn_down, pre_mix_g, post_mix_g, pre_ffn_g, post_ffn_g, loss_target, m_w_in, m_b_gate, m_q_norm_g, m_k_norm_g, m_q_a_norm_g, m_kv_a_norm_g, m_w_q_up, m_w_kv_up, m_w_branch_a, m_w_branch_b, m_w_o, m_w_ffn_up, m_w_ffn_down, m_pre_mix_g, m_post_mix_g, m_pre_ffn_g, m_post_ffn_g, v_w_in, v_b_gate, v_q_norm_g, v_k_norm_g, v_q_a_norm_g, v_kv_a_norm_g, v_w_q_up, v_w_kv_up, v_w_branch_a, v_w_branch_b, v_w_o, v_w_ffn_up, v_w_ffn_down, v_pre_mix_g, v_post_mix_g, v_pre_ffn_g, v_post_ffn_g):
    weights = dict(zip(WEIGHT_NAMES, (w_in, b_gate, q_norm_g, k_norm_g, q_a_norm_g, kv_a_norm_g, w_q_up, w_kv_up,
                                      w_branch_a, w_branch_b, w_o, w_ffn_up, w_ffn_down, pre_mix_g, post_mix_g,
                                      pre_ffn_g, post_ffn_g)))
    mom_m = dict(zip(WEIGHT_NAMES, (m_w_in, m_b_gate, m_q_norm_g, m_k_norm_g, m_q_a_norm_g, m_kv_a_norm_g, m_w_q_up,
                                    m_w_kv_up, m_w_branch_a, m_w_branch_b, m_w_o, m_w_ffn_up, m_w_ffn_down,
                                    m_pre_mix_g, m_post_mix_g, m_pre_ffn_g, m_post_ffn_g)))
    mom_v = dict(zip(WEIGHT_NAMES, (v_w_in, v_b_gate, v_q_norm_g, v_k_norm_g, v_q_a_norm_g, v_kv_a_norm_g, v_w_q_up,
                                    v_w_kv_up, v_w_branch_a, v_w_branch_b, v_w_o, v_w_ffn_up, v_w_ffn_down,
                                    v_pre_mix_g, v_post_mix_g, v_pre_ffn_g, v_post_ffn_g)))
    assert x.shape[0] == 1 and x.shape[2] == D_MODEL, x.shape
    n_layers = w_in.shape[0]
    t = x.shape[1]
    x0 = x.reshape(t, D_MODEL)
    target = loss_target.reshape(t, D_MODEL)
    shard_shapes = {n: weights[n].shape for n in BIG_NAMES}
    small_shapes = [weights[n].shape for n in SMALL_NAMES]

    gathered = _all_gather([weights[n].astype(BF16) for n in BIG_NAMES])
    full = {n: _from_shards(g, SHARD_AXIS[n]) for n, g in zip(BIG_NAMES, gathered)}
    lw_all = _layout_weights(full)
    lw_all["b_gate"] = b_gate.reshape(n_layers, 1, 2 * D_MODEL)
    lw_all["gq2"] = jnp.tile(q_norm_g, (1, 2)).reshape(n_layers, 1, LANES)
    lw_all["gk2"] = jnp.tile(k_norm_g, (1, 2)).reshape(n_layers, 1, LANES)
    lw_all["gqa"] = q_a_norm_g.reshape(n_layers, 1, MLA_Q_RANK)
    lw_all["gkva"] = kv_a_norm_g.reshape(n_layers, 1, MLA_KV_RANK)
    for n in ("post_mix_g", "pre_ffn_g", "post_ffn_g"):
        lw_all[n] = weights[n]
    lw_all["next_pre_mix_g"] = jnp.roll(pre_mix_g, -1, axis=0)

    tabs = _rope_tables(t)
    u0 = _rms_fwd(x0, pre_mix_g[0])

    layer_w = [{n: a[l] for n, a in lw_all.items()} for l in range(n_layers)]
    xc, uc, saved = x0, u0, []
    for l in range(n_layers):
        xc, uc, sv = _layer_fwd(xc, uc, layer_w[l], tabs)
        saved.append(sv)
    dy, loss_acc = _loss_grad(xc, target)
    loss = lax.psum(0.5 * jnp.sum(loss_acc) / D_MODEL, ("x", "y", "c"))

    dx0, du0, layer_g = dy, jnp.zeros((t, D_MODEL), F32), [None] * n_layers
    gbuf = {n: lax.empty(lw_all[n].shape, F32) for n in BIG_NAMES}
    for l in reversed(range(n_layers)):
        dx0, du0, layer_g[l] = _layer_bwd(dx0, du0, layer_w[l], saved[l], tabs, gbuf, l)
        gbuf = {n: layer_g[l][n] for n in BIG_NAMES}
    grads = {n: jnp.stack([g[n] for g in layer_g]) for n in layer_g[0] if n not in BIG_NAMES}
    grads.update(gbuf)
    grad_x, dg1_first = _rms_bwd(x0, pre_mix_g[0], dx0, du0)

    big_grads = _unlayout_grads({n: grads[n] for n in BIG_NAMES})
    fold = lambda a: a.sum(axis=1)
    dgq = fold(grads["q_norm_g"]).reshape(n_layers, 2, HEAD_DIM).sum(axis=1)
    dgk = fold(grads["k_norm_g"]).reshape(n_layers, 2, HEAD_DIM).sum(axis=1)
    dg1 = jnp.concatenate([fold(dg1_first[None]), fold(grads["next_pre_mix_g"])[:-1]], axis=0)
    small_grads = {
        "b_gate": fold(grads["b_gate"]), "q_norm_g": dgq, "k_norm_g": dgk, "q_a_norm_g": fold(grads["q_a_norm_g"]),
        "kv_a_norm_g": fold(grads["kv_a_norm_g"]), "pre_mix_g": dg1, "post_mix_g": fold(grads["post_mix_g"]),
        "pre_ffn_g": fold(grads["pre_ffn_g"]), "post_ffn_g": fold(grads["post_ffn_g"]),
    }
    small_packed = _pack_small([small_grads[n] for n in SMALL_NAMES])
    sends = [_shards_of(big_grads[n], SHARD_AXIS[n]).reshape((N_DEV,) + _rows2d(weights[n]).shape)
             for n in BIG_NAMES]
    sends.append(jnp.broadcast_to(small_packed[None], (N_DEV,) + small_packed.shape))
    halves = _pair_exchange(sends)
    core = lax.axis_index("c").astype(jnp.int32).reshape(1)
    recvs = _chip_exchange([_pair_add(s, h, core) for s, h in zip(sends, halves)])

    results = {}
    for n, recv in zip(BIG_NAMES, recvs):
        res = _adamw(recv, _rows2d(weights[n]), _rows2d(mom_m[n]), _rows2d(mom_v[n]))
        results[n] = [r.reshape(shard_shapes[n]) for r in res]
    res = _adamw(recvs[-1], *[_pack_small([d[n] for n in SMALL_NAMES]) for d in (weights, mom_m, mom_v)])
    for kind, packed_out in enumerate(res):
        for n, val in zip(SMALL_NAMES, _unpack_small(packed_out, small_shapes)):
            results.setdefault(n, [None] * 4)[kind] = val
    outs = [results[n][kind] for kind in range(4) for n in WEIGHT_NAMES]
    return (loss, grad_x.reshape(x.shape), *outs)
```

```python
import math

import jax
import jax.numpy as jnp
import numpy as np
from jax import lax
from jax.experimental import pallas as pl
from jax.experimental.pallas import tpu as pltpu

F32 = jnp.float32
BF16 = jnp.bfloat16

D_MODEL = 1024
GRID_W = 64
ROPE_THETA = 10000.0
EPS = 1e-6
GQA_HEADS = 8
GQA_KV_HEADS = 2
GQA_GROUP = GQA_HEADS // GQA_KV_HEADS
HEAD_DIM = 64
MLA_HEADS = 8
MLA_ROPE_DIM = 32
MLA_QK_DIM = 96
MLA_Q_RANK = 384
MLA_KV_RANK = 256
GQA_SCALE = 1.0 / math.sqrt(HEAD_DIM)
MLA_SCALE = 1.0 / math.sqrt(MLA_QK_DIM)
LOG2E = math.log2(math.e)
LN2 = math.log(2.0)

ADAM_LR = 0.001
ADAM_B1 = 0.9
ADAM_B2 = 0.999
ADAM_EPS = 1e-08
ADAM_WD = 0.01
ADAM_STEP = 10

N_DEV = 8
LANES = 128
SUBLANES = 8
VMEM_LIMIT = 48 * 1024 * 1024

Z_QA, Z_KA, Z_VA, Z_CQ, Z_CKV, Z_KR, Z_GATE = 0, 512, 640, 768, 1152, 1408, 1536
Z_ATT_W = 1536
Z_W = 3584
KR_LANE0 = 64

WEIGHT_NAMES = ("w_in", "b_gate", "q_norm_g", "k_norm_g", "q_a_norm_g", "kv_a_norm_g", "w_q_up", "w_kv_up",
                "w_branch_a", "w_branch_b", "w_o", "w_ffn_up", "w_ffn_down", "pre_mix_g", "post_mix_g",
                "pre_ffn_g", "post_ffn_g")
SHARD_AXIS = {"w_in": 2, "w_q_up": 2, "w_kv_up": 2, "w_branch_a": 2, "w_branch_b": 2, "w_o": 1, "w_ffn_up": 2,
              "w_ffn_down": 1}
BIG_NAMES = tuple(n for n in WEIGHT_NAMES if n in SHARD_AXIS)
SMALL_NAMES = tuple(n for n in WEIGHT_NAMES if n not in SHARD_AXIS)
ADAM_BLOCK_ELEMS = 256 * 1024
MM_TILE = 1024
MM_TILE_TOKENS = 2048
MM_TILE_K = 2048
PREP_ROWS = 512
ROW_TILE = 1024
ATTN_TQ = 1024
ATTN_TK = 1024


def _params(*semantics):
    return pltpu.CompilerParams(dimension_semantics=semantics, vmem_limit_bytes=VMEM_LIMIT)


def _tile(n, pref):
    if n <= pref:
        return n
    t = (pref // LANES) * LANES
    while n % t:
        t -= LANES
    return t


def _fold8(t):
    return t.reshape(t.shape[0] // SUBLANES, SUBLANES, t.shape[1]).sum(axis=0)


_DIMS = {"nn": ((1,), (0,)), "nt": ((1,), (1,)), "tn": ((0,), (0,))}


def _matmul(a, b, mode, name, post=None, h=None, stack=None):
    out_dt = BF16
    if mode == "nn":
        (m, k), n = a.shape, b.shape[1]
    elif mode == "nt":
        (m, k), n = a.shape, b.shape[0]
    else:
        (k, m), n = a.shape, b.shape[1]
    tm = _tile(m, MM_TILE_TOKENS if mode != "tn" and k <= MM_TILE else MM_TILE)
    tn, tk = _tile(n, MM_TILE), _tile(k, MM_TILE_K)
    nk = k // tk
    dims = (_DIMS[mode], ((), ()))
    operands = [a, b] + ([h] if post in ("relu2_bwd", "delta") else []) + ([stack[0]] if stack else [])
    n_in = len(operands)
    n_out = 2 if post in ("relu2", "delta") else 1
    assert post != "delta" or tn == n, (n, tn)

    def body(*refs):
        a_ref, b_ref = refs[:2]
        o_refs, acc_ref = refs[n_in:n_in + n_out], refs[-1]

        def finish(val):
            if post == "relu2":
                o_refs[0][...] = val.astype(out_dt)
                r = jnp.maximum(val, 0.0)
                o_refs[1][...] = (r * r).astype(BF16)
            elif post == "relu2_bwd":
                o_refs[0][...] = (val * (2.0 * jnp.maximum(refs[2][...].astype(F32), 0.0))).astype(BF16)
            elif post == "delta":
                do = val.astype(BF16)
                o_refs[0][...] = do
                prod = do.astype(F32) * refs[2][...].astype(F32)
                for g in range(n // LANES):
                    x = prod[:, LANES * g:LANES * (g + 1)]
                    lo = _lo_mask(x.shape)
                    d0 = jnp.sum(jnp.where(lo, x, 0.0), axis=-1, keepdims=True)
                    d1 = jnp.sum(jnp.where(lo, 0.0, x), axis=-1, keepdims=True)
                    o_refs[1][2 * g] = jnp.broadcast_to(d0, (tm, LANES))
                    o_refs[1][2 * g + 1] = jnp.broadcast_to(d1, (tm, LANES))
            else:
                o_refs[0][...] = val.astype(out_dt)

        prod = lax.dot_general(a_ref[...], b_ref[...], dims, preferred_element_type=F32)
        if nk == 1:
            finish(prod)
        else:
            kk = pl.program_id(2)

            @pl.when(kk == 0)
            def _():
                acc_ref[...] = prod

            @pl.when(kk > 0)
            def _():
                acc_ref[...] += prod

            @pl.when(kk == nk - 1)
            def _():
                finish(acc_ref[...])

    if mode == "tn":
        a_spec = pl.BlockSpec((tk, tm), lambda i, j, kk: (kk, i))
    else:
        a_spec = pl.BlockSpec((tm, tk), lambda i, j, kk: (i, kk))
    if mode == "nt":
        b_spec = pl.BlockSpec((tn, tk), lambda i, j, kk: (j, kk))
    else:
        b_spec = pl.BlockSpec((tk, tn), lambda i, j, kk: (kk, j))
    o_spec = pl.BlockSpec((tm, tn), lambda i, j, kk: (i, j))
    main_out, bf16_out = jax.ShapeDtypeStruct((m, n), out_dt), jax.ShapeDtypeStruct((m, n), BF16)
    heads = n // HEAD_DIM
    delta_out = jax.ShapeDtypeStruct((heads, m, LANES), F32)
    out_shape = {None: main_out, "relu2": [main_out, bf16_out], "relu2_bwd": bf16_out,
                 "delta": [bf16_out, delta_out]}[post]
    in_specs = [a_spec, b_spec] + ([o_spec] if post in ("relu2_bwd", "delta") else [])
    out_specs = {None: o_spec, "relu2": [o_spec, o_spec], "relu2_bwd": o_spec,
                 "delta": [o_spec, pl.BlockSpec((heads, tm, LANES), lambda i, j, kk: (0, i, 0))]}[post]
    aliases = {}
    if stack:
        buf, layer = stack
        assert post is None and buf.shape[1:] == (m, n) and buf.dtype == out_dt, (buf.shape, buf.dtype)
        in_specs.append(pl.BlockSpec(memory_space=pl.ANY))
        out_specs = pl.BlockSpec((None, tm, tn), lambda i, j, kk: (layer, i, j))
        out_shape = jax.ShapeDtypeStruct(buf.shape, buf.dtype)
        aliases = {n_in - 1: 0}
    return pl.pallas_call(
        body,
        name=name,
        grid=(m // tm, n // tn, nk),
        in_specs=in_specs,
        out_specs=out_specs,
        out_shape=out_shape,
        scratch_shapes=[pltpu.VMEM((tm, tn), F32)],
        input_output_aliases=aliases,
        compiler_params=_params("parallel", "parallel", "arbitrary"),
    )(*operands)


def _rinv(x):
    return lax.rsqrt(jnp.mean(x * x, axis=-1, keepdims=True) + EPS)


def _rms_bwd_rows(x, g, dy):
    r = _rinv(x)
    xh = x * r
    dxh = dy * g
    dx = r * (dxh - xh * jnp.mean(dxh * xh, axis=-1, keepdims=True))
    return dx, dy * xh


def _row_spec(tm, c):
    return pl.BlockSpec((tm, c), lambda i: (i, 0))


def _vec_spec(c):
    return pl.BlockSpec((1, c), lambda i: (0, 0))


def _acc_spec(c):
    return pl.BlockSpec((SUBLANES, c), lambda i: (0, 0))


def _rms_fwd(x, g):
    t, d = x.shape
    tm = _tile(t, ROW_TILE)

    def body(x_ref, g_ref, o_ref):
        xv = x_ref[...]
        o_ref[...] = (xv * _rinv(xv) * g_ref[...]).astype(BF16)

    return pl.pallas_call(
        body, name="rms_fwd", grid=(t // tm,),
        in_specs=[_row_spec(tm, d), _vec_spec(d)], out_specs=_row_spec(tm, d),
        out_shape=jax.ShapeDtypeStruct((t, d), BF16), compiler_params=_params("parallel"),
    )(x, g.reshape(1, d))


def _rms_bwd(x, g, dres, dy):
    t, d = x.shape
    tm = _tile(t, ROW_TILE)

    def body(x_ref, g_ref, dres_ref, dy_ref, dx_ref, dg_ref):
        dx, dgc = _rms_bwd_rows(x_ref[...], g_ref[...], dy_ref[...].astype(F32))
        dx_ref[...] = dres_ref[...] + dx

        @pl.when(pl.program_id(0) == 0)
        def _():
            dg_ref[...] = jnp.zeros_like(dg_ref)

        dg_ref[...] += _fold8(dgc)

    return pl.pallas_call(
        body, name="rms_bwd", grid=(t // tm,),
        in_specs=[_row_spec(tm, d), _vec_spec(d), _row_spec(tm, d), _row_spec(tm, d)],
        out_specs=[_row_spec(tm, d), _acc_spec(d)],
        out_shape=[jax.ShapeDtypeStruct((t, d), F32), jax.ShapeDtypeStruct((SUBLANES, d), F32)],
        compiler_params=_params("arbitrary"),
    )(x, g.reshape(1, d), dres, dy)


def _res_norm_fwd(x, m, g_post, g_next):
    t, d = x.shape
    tm = _tile(t, ROW_TILE)

    def body(x_ref, m_ref, gp_ref, gn_ref, x2_ref, u2_ref):
        mv = m_ref[...].astype(F32)
        x2 = x_ref[...] + mv * _rinv(mv) * gp_ref[...]
        x2_ref[...] = x2
        u2_ref[...] = (x2 * _rinv(x2) * gn_ref[...]).astype(BF16)

    return pl.pallas_call(
        body, name="res_norm_fwd", grid=(t // tm,),
        in_specs=[_row_spec(tm, d), _row_spec(tm, d), _vec_spec(d), _vec_spec(d)],
        out_specs=[_row_spec(tm, d), _row_spec(tm, d)],
        out_shape=[jax.ShapeDtypeStruct((t, d), F32), jax.ShapeDtypeStruct((t, d), BF16)],
        compiler_params=_params("parallel"),
    )(x, m, g_post.reshape(1, d), g_next.reshape(1, d))


def _res_norm_bwd(x2, m, g_post, g_next, dx2_in, du2):
    t, d = x2.shape
    tm = _tile(t, ROW_TILE // 2)

    def body(x2_ref, m_ref, gp_ref, gn_ref, dx2in_ref, du2_ref, dx2_ref, dm_ref, dgp_ref, dgn_ref):
        dxn, dgn_c = _rms_bwd_rows(x2_ref[...], gn_ref[...], du2_ref[...].astype(F32))
        dx2 = dx2in_ref[...] + dxn
        dx2_ref[...] = dx2
        dm, dgp_c = _rms_bwd_rows(m_ref[...].astype(F32), gp_ref[...], dx2)
        dm_ref[...] = dm.astype(BF16)

        @pl.when(pl.program_id(0) == 0)
        def _():
            dgp_ref[...] = jnp.zeros_like(dgp_ref)
            dgn_ref[...] = jnp.zeros_like(dgn_ref)

        dgp_ref[...] += _fold8(dgp_c)
        dgn_ref[...] += _fold8(dgn_c)

    return pl.pallas_call(
        body, name="res_norm_bwd", grid=(t // tm,),
        in_specs=[_row_spec(tm, d), _row_spec(tm, d), _vec_spec(d), _vec_spec(d), _row_spec(tm, d), _row_spec(tm, d)],
        out_specs=[_row_spec(tm, d), _row_spec(tm, d), _acc_spec(d), _acc_spec(d)],
        out_shape=[jax.ShapeDtypeStruct((t, d), F32), jax.ShapeDtypeStruct((t, d), BF16),
                   jax.ShapeDtypeStruct((SUBLANES, d), F32), jax.ShapeDtypeStruct((SUBLANES, d), F32)],
        compiler_params=_params("arbitrary"),
    )(x2, m, g_post.reshape(1, d), g_next.reshape(1, d), dx2_in, du2)


def _rope_tables(t):
    rows = t // GRID_W
    row = jnp.repeat(jnp.arange(rows, dtype=F32), GRID_W)
    col = jnp.tile(jnp.arange(GRID_W, dtype=F32), rows)

    def tab(rot_dim):
        half = rot_dim // 2
        inv = ROPE_THETA ** (-jnp.arange(0, half, 2, dtype=F32) / half)
        ar = row[:, None] * inv[None, :]
        ac = col[:, None] * inv[None, :]
        ang = jnp.concatenate([ar, ar, ac, ac], axis=-1)
        q = half // 2
        sign = np.tile(np.concatenate([-np.ones(q, np.float32), np.ones(q, np.float32)]), 2)
        return jnp.cos(ang), jnp.sin(ang) * sign[None, :]

    ca, sa = tab(HEAD_DIM)
    cb, sb = tab(MLA_ROPE_DIM)
    one = jnp.ones((t, 1), F32)
    cos_b = jnp.concatenate([one * jnp.ones((1, KR_LANE0), F32), cb, one * jnp.ones((1, 32), F32)], axis=-1)
    sin_b = jnp.concatenate([jnp.zeros((t, KR_LANE0), F32), sb, jnp.zeros((t, 32), F32)], axis=-1)
    return jnp.tile(ca, (1, GQA_HEADS)), jnp.tile(sa, (1, GQA_HEADS)), cos_b, sin_b


def _swap_halves(x, sh):
    lane = lax.broadcasted_iota(jnp.int32, x.shape, 1)
    up = pltpu.roll(x, LANES - sh, 1)
    dn = pltpu.roll(x, sh, 1)
    return jnp.where((lane & (2 * sh - 1)) < sh, up, dn)


def _rope(x, cos, sin_s, sh):
    return x * cos + _swap_halves(x, sh) * sin_s


def _rope_bwd(dy, cos, sin_s, sh):
    return dy * cos + _swap_halves(dy * sin_s, sh)


def _lo_mask(shape):
    return lax.broadcasted_iota(jnp.int32, shape, 1) < HEAD_DIM


def _half_mean(t, lo):
    s_lo = jnp.sum(jnp.where(lo, t, 0.0), axis=-1, keepdims=True)
    s_hi = jnp.sum(jnp.where(lo, 0.0, t), axis=-1, keepdims=True)
    return jnp.where(lo, s_lo, s_hi) * (1.0 / HEAD_DIM)


def _head_norm(x, g2):
    lo = _lo_mask(x.shape)
    r = lax.rsqrt(_half_mean(x * x, lo) + EPS)
    return x * r * g2


def _head_norm_bwd(x, g2, dy):
    lo = _lo_mask(x.shape)
    r = lax.rsqrt(_half_mean(x * x, lo) + EPS)
    xh = x * r
    dxh = dy * g2
    dx = r * (dxh - xh * _half_mean(dxh * xh, lo))
    return dx, dy * xh


def _prep_a_fwd(z, gq2, gk2, gqa, gkva, cos_a, sin_a, cos_b, sin_b):
    t = z.shape[0]
    tm = _tile(t, ROW_TILE)

    def body(z_ref, gq_ref, gk_ref, gqa_ref, gkva_ref, ca_ref, sa_ref, cb_ref, sb_ref,
             qa_ref, ka_ref, va_ref, cqn_ref, ckvn_ref, krr_ref):
        def zf(lo, hi):
            return z_ref[:, lo:hi].astype(F32)

        for j in range(4):
            cols = slice(LANES * j, LANES * (j + 1))
            y = _rope(_head_norm(zf(LANES * j, LANES * (j + 1)), gq_ref[...]), ca_ref[:, cols], sa_ref[:, cols], 16)
            qa_ref[:, cols] = (y * (GQA_SCALE * LOG2E)).astype(BF16)
        y = _rope(_head_norm(zf(Z_KA, Z_VA), gk_ref[...]), ca_ref[:, :LANES], sa_ref[:, :LANES], 16)
        ka_ref[...] = y.astype(BF16)
        va_ref[...] = z_ref[:, Z_VA:Z_CQ].astype(BF16)
        cq = zf(Z_CQ, Z_CKV)
        cqn_ref[...] = (cq * _rinv(cq) * gqa_ref[...]).astype(BF16)
        ckv = zf(Z_CKV, Z_KR)
        ckvn_ref[...] = (ckv * _rinv(ckv) * gkva_ref[...]).astype(BF16)
        krr_ref[...] = _rope(zf(Z_KR, Z_GATE), cb_ref[...], sb_ref[...], 8)

    return pl.pallas_call(
        body, name="prep_a_fwd", grid=(t // tm,),
        in_specs=[_row_spec(tm, Z_ATT_W), _vec_spec(LANES), _vec_spec(LANES), _vec_spec(MLA_Q_RANK),
                  _vec_spec(MLA_KV_RANK), _row_spec(tm, 512), _row_spec(tm, 512), _row_spec(tm, LANES),
                  _row_spec(tm, LANES)],
        out_specs=[_row_spec(tm, 512), _row_spec(tm, LANES), _row_spec(tm, LANES), _row_spec(tm, MLA_Q_RANK),
                   _row_spec(tm, MLA_KV_RANK), _row_spec(tm, LANES)],
        out_shape=[jax.ShapeDtypeStruct((t, 512), BF16), jax.ShapeDtypeStruct((t, LANES), BF16),
                   jax.ShapeDtypeStruct((t, LANES), BF16), jax.ShapeDtypeStruct((t, MLA_Q_RANK), BF16),
                   jax.ShapeDtypeStruct((t, MLA_KV_RANK), BF16), jax.ShapeDtypeStruct((t, LANES), F32)],
        compiler_params=_params("parallel"),
    )(z, gq2, gk2, gqa, gkva, cos_a, sin_a, cos_b, sin_b)


def _prep_a_bwd(z, dqa, dka4, dva4, dcqn, dckvn, dkr, dzga, dzgb, gq2, gk2, gqa, gkva, cos_a, sin_a):
    t = z.shape[0]
    tm = _tile(t, PREP_ROWS)

    def body(z_ref, dqa_ref, dka_ref, dva_ref, dcqn_ref, dckvn_ref, dkr_ref, dzga_ref, dzgb_ref, gq_ref, gk_ref,
             gqa_ref, gkva_ref, ca_ref, sa_ref, dz_ref, dgq_ref, dgk_ref, dgqa_ref, dgkva_ref):
        @pl.when(pl.program_id(0) == 0)
        def _():
            dgq_ref[...] = jnp.zeros_like(dgq_ref)
            dgk_ref[...] = jnp.zeros_like(dgk_ref)
            dgqa_ref[...] = jnp.zeros_like(dgqa_ref)
            dgkva_ref[...] = jnp.zeros_like(dgkva_ref)

        def zf(lo, hi):
            return z_ref[:, lo:hi].astype(F32)

        dgq = jnp.zeros((SUBLANES, LANES), F32)
        for j in range(4):
            cols = slice(LANES * j, LANES * (j + 1))
            dy = _rope_bwd(dqa_ref[:, cols] * GQA_SCALE, ca_ref[:, cols], sa_ref[:, cols], 16)
            dx, dgc = _head_norm_bwd(zf(LANES * j, LANES * (j + 1)), gq_ref[...], dy)
            dz_ref[:, cols] = dx.astype(BF16)
            dgq = dgq + _fold8(dgc)
        dgq_ref[...] += dgq
        dk = (dka_ref[0] + dka_ref[1] + dka_ref[2] + dka_ref[3]).T * LN2
        dy = _rope_bwd(dk, ca_ref[:, :LANES], sa_ref[:, :LANES], 16)
        dx, dgc = _head_norm_bwd(zf(Z_KA, Z_VA), gk_ref[...], dy)
        dz_ref[:, Z_KA:Z_VA] = dx.astype(BF16)
        dgk_ref[...] += _fold8(dgc)
        dz_ref[:, Z_VA:Z_CQ] = (dva_ref[0] + dva_ref[1] + dva_ref[2] + dva_ref[3]).T.astype(BF16)
        dx, dgc = _rms_bwd_rows(zf(Z_CQ, Z_CKV), gqa_ref[...], dcqn_ref[...].astype(F32))
        dz_ref[:, Z_CQ:Z_CKV] = dx.astype(BF16)
        dgqa_ref[...] += _fold8(dgc)
        dx, dgc = _rms_bwd_rows(zf(Z_CKV, Z_KR), gkva_ref[...], dckvn_ref[...].astype(F32))
        dz_ref[:, Z_CKV:Z_KR] = dx.astype(BF16)
        dgkva_ref[...] += _fold8(dgc)
        dz_ref[:, Z_KR:Z_GATE] = dkr_ref[...].astype(BF16)
        dz_ref[:, Z_GATE:Z_GATE + D_MODEL] = dzga_ref[...]
        dz_ref[:, Z_GATE + D_MODEL:Z_W] = dzgb_ref[...]

    part = pl.BlockSpec((4, LANES, tm), lambda i: (0, 0, i))
    return pl.pallas_call(
        body, name="prep_a_bwd", grid=(t // tm,),
        in_specs=[_row_spec(tm, Z_ATT_W), _row_spec(tm, 512), part, part, _row_spec(tm, MLA_Q_RANK),
                  _row_spec(tm, MLA_KV_RANK), _row_spec(tm, LANES), _row_spec(tm, D_MODEL), _row_spec(tm, D_MODEL),
                  _vec_spec(LANES),
                  _vec_spec(LANES), _vec_spec(MLA_Q_RANK), _vec_spec(MLA_KV_RANK), _row_spec(tm, 512),
                  _row_spec(tm, 512)],
        out_specs=[_row_spec(tm, Z_W), _acc_spec(LANES), _acc_spec(LANES), _acc_spec(MLA_Q_RANK),
                   _acc_spec(MLA_KV_RANK)],
        out_shape=[jax.ShapeDtypeStruct((t, Z_W), BF16), jax.ShapeDtypeStruct((SUBLANES, LANES), F32),
                   jax.ShapeDtypeStruct((SUBLANES, LANES), F32), jax.ShapeDtypeStruct((SUBLANES, MLA_Q_RANK), F32),
                   jax.ShapeDtypeStruct((SUBLANES, MLA_KV_RANK), F32)],
        compiler_params=_params("arbitrary"),
    )(z, dqa, dka4, dva4, dcqn, dckvn, dkr, dzga, dzgb, gq2, gk2, gqa, gkva, cos_a, sin_a)


def _prep_b_fwd(qb, kvb, krr, cos_b, sin_b):
    t = qb.shape[0]
    tm = _tile(t, ROW_TILE)

    def body(qb_ref, kvb_ref, krr_ref, cb_ref, sb_ref, q_ref, k_ref, v_ref):
        for h in range(MLA_HEADS):
            cols = slice(LANES * h, LANES * (h + 1))
            qh = _rope(qb_ref[:, cols].astype(F32), cb_ref[...], sb_ref[...], 8)
            q_ref[:, cols] = (qh * (MLA_SCALE * LOG2E)).astype(BF16)
            k_ref[:, cols] = (kvb_ref[:, cols].astype(F32) + krr_ref[...]).astype(BF16)
        v_ref[...] = kvb_ref[:, 1024:1536].astype(BF16)

    return pl.pallas_call(
        body, name="prep_b_fwd", grid=(t // tm,),
        in_specs=[_row_spec(tm, 1024), _row_spec(tm, 1536), _row_spec(tm, LANES), _row_spec(tm, LANES),
                  _row_spec(tm, LANES)],
        out_specs=[_row_spec(tm, 1024), _row_spec(tm, 1024), _row_spec(tm, 512)],
        out_shape=[jax.ShapeDtypeStruct((t, 1024), BF16), jax.ShapeDtypeStruct((t, 1024), BF16),
                   jax.ShapeDtypeStruct((t, 512), BF16)],
        compiler_params=_params("parallel"),
    )(qb, kvb, krr, cos_b, sin_b)


def _prep_b_bwd(dq, dk, dv, cos_b, sin_b):
    t = dq.shape[0]
    tm = _tile(t, ROW_TILE)

    def body(dq_ref, dk_ref, dv_ref, cb_ref, sb_ref, dqb_ref, dkvb_ref, dkr_ref):
        dkr = jnp.zeros((tm, LANES), F32)
        for h in range(MLA_HEADS):
            cols = slice(LANES * h, LANES * (h + 1))
            dqb_ref[:, cols] = _rope_bwd(dq_ref[:, cols] * MLA_SCALE, cb_ref[...], sb_ref[...], 8).astype(BF16)
            dkh = dk_ref[cols, :].T * LN2
            dkvb_ref[:, cols] = dkh.astype(BF16)
            dkr = dkr + dkh
        for j in range(MLA_HEADS // 2):
            dkvb_ref[:, 1024 + LANES * j:1024 + LANES * (j + 1)] = dv_ref[LANES * j:LANES * (j + 1), :].T.astype(BF16)
        dkr_ref[...] = _rope_bwd(dkr, cb_ref[...], sb_ref[...], 8)

    return pl.pallas_call(
        body, name="prep_b_bwd", grid=(t // tm,),
        in_specs=[_row_spec(tm, 1024), pl.BlockSpec((1024, tm), lambda i: (0, i)),
                  pl.BlockSpec((512, tm), lambda i: (0, i)), _row_spec(tm, LANES),
                  _row_spec(tm, LANES)],
        out_specs=[_row_spec(tm, 1024), _row_spec(tm, 1536), _row_spec(tm, LANES)],
        out_shape=[jax.ShapeDtypeStruct((t, 1024), BF16), jax.ShapeDtypeStruct((t, 1536), BF16),
                   jax.ShapeDtypeStruct((t, LANES), F32)],
        compiler_params=_params("parallel"),
    )(dq, dk, dv, cos_b, sin_b)


_NT = (((1,), (1,)), ((), ()))
_NN = (((1,), (0,)), ((), ()))
_TN = (((0,), (0,)), ((), ()))


def _head_operands(qv, kv, i, shared_k):
    if shared_k:
        lo = _lo_mask(qv.shape)
        keep = lo if i == 0 else jnp.logical_not(lo)
        return jnp.where(keep, qv, jnp.zeros_like(qv)), kv
    cols = slice(LANES * i, LANES * (i + 1))
    return qv[:, cols], kv[:, cols]


def _attn_specs(shared_k, tq, tk, q_of, k_of):
    wq = LANES if shared_k else 2 * LANES
    q_spec = pl.BlockSpec((tq, wq), lambda *g: (q_of(*g), g[0]))
    if shared_k:
        k_spec = pl.BlockSpec((tk, LANES), lambda *g: (k_of(*g), 0))
        v_spec = pl.BlockSpec((tk, LANES), lambda *g: (k_of(*g), 0))
    else:
        k_spec = pl.BlockSpec((tk, wq), lambda *g: (k_of(*g), g[0]))
        v_spec = pl.BlockSpec((tk, LANES), lambda *g: (k_of(*g), g[0]))
    return wq, q_spec, k_spec, v_spec


def _attn_fwd(q, k, v, shared_k, name):
    t = q.shape[0]
    tq, tk = _tile(t, ATTN_TQ), _tile(t, ATTN_TK)
    nq, nk = t // tq, t // tk
    wq, q_spec, k_spec, v_spec = _attn_specs(shared_k, tq, tk, lambda p, i, j: i, lambda p, i, j: j)
    groups = q.shape[1] // wq
    chunk = _tile(tq, 2 * LANES)

    def body(q_ref, k_ref, v_ref, o_ref, lse_ref, m_s, acc_s, alpha_s, s_s, p_s):
        kb = pl.program_id(2)

        @pl.when(kb == 0)
        def _():
            m_s[...] = jnp.full_like(m_s, -jnp.inf)
            acc_s[...] = jnp.zeros_like(acc_s)

        qv, kv, vv = q_ref[...], k_ref[...], v_ref[...]
        lo = _lo_mask(vv.shape)
        for i in range(2):
            qi, ki = _head_operands(qv, kv, i, shared_k)
            s_s[i] = lax.dot_general(ki, qi, _NT, preferred_element_type=F32)
        for i in range(2):
            for c in range(tq // chunk):
                cols = slice(c * chunk, (c + 1) * chunk)
                m_prev = m_s[i, :, cols]
                m_new = jnp.maximum(m_prev, jnp.max(s_s[i, :, cols], axis=0, keepdims=True))
                alpha_s[i, :, cols] = jnp.exp2(m_prev - m_new)
                m_s[i, :, cols] = m_new
                p_s[i, :, cols] = jnp.exp2(s_s[i, :, cols] - m_new).astype(BF16)
        for i in range(2):
            keep = lo if i == 0 else jnp.logical_not(lo)
            vi = jnp.where(keep, vv, jnp.ones_like(vv))
            acc_s[i] = alpha_s[i] * acc_s[i] + lax.dot_general(vi, p_s[i], _TN, preferred_element_type=F32)

        @pl.when(kb == nk - 1)
        def _():
            a0, a1 = acc_s[0], acc_s[1]
            l0 = a0[LANES - SUBLANES:, :][0:1, :]
            l1 = a1[0:SUBLANES, :][0:1, :]
            row_lo = lax.broadcasted_iota(jnp.int32, a0.shape, 0) < HEAD_DIM
            o_ref[...] = jnp.where(row_lo, a0 / l0, a1 / l1).T.astype(BF16)
            lse_ref[0] = jnp.broadcast_to(m_s[0] + jnp.log2(l0), (LANES, tq)).T
            lse_ref[1] = jnp.broadcast_to(m_s[1] + jnp.log2(l1), (LANES, tq)).T

    return pl.pallas_call(
        body, name=name, grid=(groups, nq, nk),
        in_specs=[q_spec, k_spec, v_spec],
        out_specs=[pl.BlockSpec((tq, LANES), lambda p, i, j: (i, p)),
                   pl.BlockSpec((2, tq, LANES), lambda p, i, j: (p, i, 0))],
        out_shape=[jax.ShapeDtypeStruct((t, LANES * groups), BF16),
                   jax.ShapeDtypeStruct((2 * groups, t, LANES), F32)],
        scratch_shapes=[pltpu.VMEM((2, 1, tq), F32), pltpu.VMEM((2, LANES, tq), F32), pltpu.VMEM((2, 1, tq), F32),
                        pltpu.VMEM((2, tk, tq), F32), pltpu.VMEM((2, tk, tq), BF16)],
        compiler_params=_params("parallel", "parallel", "arbitrary"),
    )(q, k, v)


def _attn_bwd(q, k, v, do, lse, delta, shared_k, name):
    t = q.shape[0]
    tq, tk = _tile(t, ATTN_TQ), _tile(t, ATTN_TK)
    nq, nk = t // tq, t // tk
    wq, q_spec, k_spec, v_spec = _attn_specs(shared_k, tq, tk, lambda p, j, i: i, lambda p, j, i: j)
    groups = q.shape[1] // wq

    def body(q_ref, k_ref, v_ref, do_ref, lse_ref, delta_ref, dq_ref, dk_ref, dv_ref, dk_s, dv_s, s_s, dp_s, p_s,
             ds_s):
        kb, qb = pl.program_id(1), pl.program_id(2)

        @pl.when(qb == 0)
        def _():
            dk_s[...] = jnp.zeros_like(dk_s)
            dv_s[...] = jnp.zeros_like(dv_s)

        qv, kv, vv, dov = q_ref[...], k_ref[...], v_ref[...], do_ref[...]
        lo = _lo_mask(dov.shape)
        heads = []
        for i in range(2):
            qi, ki = _head_operands(qv, kv, i, shared_k)
            keep = lo if i == 0 else jnp.logical_not(lo)
            doi = jnp.where(keep, dov, jnp.zeros_like(dov))
            heads.append((qi, ki, doi))
            s_s[i] = lax.dot_general(qi, ki, _NT, preferred_element_type=F32)
            dp_s[i] = lax.dot_general(doi, vv, _NT, preferred_element_type=F32)
        for i in range(2):
            lse_i, delta_i = lse_ref[i], delta_ref[i]
            for c in range(tk // LANES):
                cols = slice(c * LANES, (c + 1) * LANES)
                p = jnp.exp2(s_s[i, :, cols] - lse_i)
                p_s[i, :, cols] = p.astype(BF16)
                ds_s[i, :, cols] = (p * (dp_s[i, :, cols] - delta_i)).astype(BF16)
        dq_parts = []
        for i in range(2):
            qi, ki, doi = heads[i]
            dv_s[...] += lax.dot_general(doi, p_s[i], _TN, preferred_element_type=F32)
            dk_i = lax.dot_general(qi, ds_s[i], _TN, preferred_element_type=F32)
            if shared_k:
                dk_s[...] += dk_i
            else:
                dk_s[LANES * i:LANES * (i + 1), :] += dk_i
            dq_parts.append(lax.dot_general(ds_s[i], ki, _NN, preferred_element_type=F32))
        rows = pl.ds(pl.multiple_of(qb * tq, tq), tq)
        if shared_k:
            tiles = [(slice(0, LANES), jnp.where(lo, dq_parts[0], dq_parts[1]))]
        else:
            tiles = [(slice(0, LANES), dq_parts[0]), (slice(LANES, 2 * LANES), dq_parts[1])]
        for cols, val in tiles:
            @pl.when(kb == 0)
            def _(cols=cols, val=val):
                dq_ref[rows, cols] = val

            @pl.when(kb > 0)
            def _(cols=cols, val=val):
                dq_ref[rows, cols] += val

        @pl.when(qb == nq - 1)
        def _():
            if shared_k:
                dk_ref[0] = dk_s[...]
                dv_ref[0] = dv_s[...]
            else:
                dk_ref[...] = dk_s[...]
                dv_ref[...] = dv_s[...]

    stat_spec = pl.BlockSpec((2, tq, LANES), lambda p, j, i: (p, i, 0))
    do_spec = pl.BlockSpec((tq, LANES), lambda p, j, i: (i, p))
    dq_spec = pl.BlockSpec((t, wq), lambda p, j, i: (0, p))
    if shared_k:
        dk_spec = pl.BlockSpec((1, LANES, tk), lambda p, j, i: (p, 0, j))
        dv_spec = dk_spec
        dk_shape = jax.ShapeDtypeStruct((groups, LANES, t), F32)
        dv_shape = dk_shape
    else:
        dk_spec = pl.BlockSpec((wq, tk), lambda p, j, i: (p, j))
        dv_spec = pl.BlockSpec((LANES, tk), lambda p, j, i: (p, j))
        dk_shape = jax.ShapeDtypeStruct((wq * groups, t), F32)
        dv_shape = jax.ShapeDtypeStruct((LANES * groups, t), F32)
    return pl.pallas_call(
        body, name=name, grid=(groups, nk, nq),
        in_specs=[q_spec, k_spec, v_spec, do_spec, stat_spec, stat_spec],
        out_specs=[dq_spec, dk_spec, dv_spec],
        out_shape=[jax.ShapeDtypeStruct((t, wq * groups), F32), dk_shape, dv_shape],
        scratch_shapes=[pltpu.VMEM((wq, tk), F32), pltpu.VMEM((LANES, tk), F32), pltpu.VMEM((2, tq, tk), F32),
                        pltpu.VMEM((2, tq, tk), F32), pltpu.VMEM((2, tq, tk), BF16), pltpu.VMEM((2, tq, tk), BF16)],
        compiler_params=_params("parallel", "arbitrary", "arbitrary"),
    )(q, k, v, do, lse, delta)


_MERGE_W = 512
_GATE_BLK0 = Z_GATE // _MERGE_W


def _merge_fwd(z, b_gate, ta, tb):
    t = z.shape[0]
    tm = _tile(t, ROW_TILE)
    w = _MERGE_W
    nj = D_MODEL // w

    def body(za_ref, zb_ref, ba_ref, bb_ref, ta_ref, tb_ref, o_ref):
        ga = jax.nn.sigmoid(za_ref[...].astype(F32) + ba_ref[...])
        gb = jax.nn.sigmoid(zb_ref[...].astype(F32) + bb_ref[...])
        o_ref[...] = (ga * ta_ref[...].astype(F32) + gb * tb_ref[...].astype(F32)).astype(BF16)

    return pl.pallas_call(
        body, name="merge_fwd", grid=(t // tm, nj),
        in_specs=[pl.BlockSpec((tm, w), lambda i, j: (i, _GATE_BLK0 + j)),
                  pl.BlockSpec((tm, w), lambda i, j: (i, _GATE_BLK0 + nj + j)),
                  pl.BlockSpec((1, w), lambda i, j: (0, j)),
                  pl.BlockSpec((1, w), lambda i, j: (0, nj + j)),
                  pl.BlockSpec((tm, w), lambda i, j: (i, j)),
                  pl.BlockSpec((tm, w), lambda i, j: (i, j))],
        out_specs=pl.BlockSpec((tm, w), lambda i, j: (i, j)),
        out_shape=jax.ShapeDtypeStruct((t, D_MODEL), BF16),
        compiler_params=_params("parallel", "parallel"),
    )(z, z, b_gate, b_gate, ta, tb)


def _merge_bwd(dmg, z, b_gate, ta, tb):
    t = z.shape[0]
    tm = _tile(t, ROW_TILE)
    w = _MERGE_W
    nj = D_MODEL // w

    def body(dm_ref, za_ref, zb_ref, ba_ref, bb_ref, ta_ref, tb_ref, dta_ref, dtb_ref, dza_ref, dzb_ref,
             dba_ref, dbb_ref):
        dm = dm_ref[...].astype(F32)
        ga = jax.nn.sigmoid(za_ref[...].astype(F32) + ba_ref[...])
        gb = jax.nn.sigmoid(zb_ref[...].astype(F32) + bb_ref[...])
        dta_ref[...] = (dm * ga).astype(BF16)
        dtb_ref[...] = (dm * gb).astype(BF16)
        dza = dm * ta_ref[...].astype(F32) * ga * (1.0 - ga)
        dzb = dm * tb_ref[...].astype(F32) * gb * (1.0 - gb)
        dza_ref[...] = dza.astype(BF16)
        dzb_ref[...] = dzb.astype(BF16)

        @pl.when(pl.program_id(1) == 0)
        def _():
            dba_ref[...] = jnp.zeros_like(dba_ref)
            dbb_ref[...] = jnp.zeros_like(dbb_ref)

        dba_ref[...] += _fold8(dza)
        dbb_ref[...] += _fold8(dzb)

    blk = pl.BlockSpec((tm, w), lambda j, i: (i, j))
    acc = pl.BlockSpec((SUBLANES, w), lambda j, i: (0, j))
    return pl.pallas_call(
        body, name="merge_bwd", grid=(nj, t // tm),
        in_specs=[blk,
                  pl.BlockSpec((tm, w), lambda j, i: (i, _GATE_BLK0 + j)),
                  pl.BlockSpec((tm, w), lambda j, i: (i, _GATE_BLK0 + nj + j)),
                  pl.BlockSpec((1, w), lambda j, i: (0, j)),
                  pl.BlockSpec((1, w), lambda j, i: (0, nj + j)),
                  blk, blk],
        out_specs=[blk, blk, blk, blk, acc, acc],
        out_shape=[jax.ShapeDtypeStruct((t, D_MODEL), BF16)] * 4 + [jax.ShapeDtypeStruct((SUBLANES, D_MODEL), F32)] * 2,
        compiler_params=_params("parallel", "arbitrary"),
    )(dmg, z, z, b_gate, b_gate, ta, tb)


def _loss_grad(y, target):
    t, d = y.shape
    tm = _tile(t, ROW_TILE)

    def body(y_ref, t_ref, dy_ref, acc_ref):
        err = y_ref[...] - t_ref[...]
        dy_ref[...] = err * (1.0 / d)
        e8 = _fold8(err * err)
        part = e8[:, 0:LANES]
        for c in range(1, d // LANES):
            part = part + e8[:, LANES * c:LANES * (c + 1)]

        @pl.when(pl.program_id(0) == 0)
        def _():
            acc_ref[...] = jnp.zeros_like(acc_ref)

        acc_ref[...] += part

    return pl.pallas_call(
        body, name="loss_grad", grid=(t // tm,),
        in_specs=[_row_spec(tm, d), _row_spec(tm, d)],
        out_specs=[_row_spec(tm, d), _acc_spec(LANES)],
        out_shape=[jax.ShapeDtypeStruct((t, d), F32), jax.ShapeDtypeStruct((SUBLANES, LANES), F32)],
        compiler_params=_params("arbitrary"),
    )(y, target)


_MESH_ID = pl.DeviceIdType.MESH
_ANY = pl.BlockSpec(memory_space=pl.ANY)


def _all_gather(arrays):
    n = len(arrays)
    halves = []
    for a in arrays:
        assert a.shape[0] % 2 == 0, a.shape
        halves.append((pl.ds(0, a.shape[0] // 2), pl.ds(a.shape[0] // 2, a.shape[0] // 2)))
    OWN_SIB, OWN_X, OWN_Y, FWD_X, FWD_Y, SIB_X, SIB_Y, SIB_DA, SIB_DB = range(9)

    def body(*refs):
        x_refs, out_refs = refs[:n], refs[n:2 * n]
        send_sems, recv_sems, local_sems = refs[2 * n:]
        mx, my, mc = lax.axis_index("x"), lax.axis_index("y"), lax.axis_index("c")
        me, sibling = (mx, my, mc), (mx, my, 1 - mc)
        x_nbr, y_nbr, diag = (1 - mx, my, mc), (mx, 1 - my, mc), (1 - mx, 1 - my, mc)

        def slot(a, dev, rows=None):
            px, py, pc = dev
            ref = out_refs[a].at[4 * px + 2 * py + pc]
            return ref if rows is None else ref.at[rows]

        def other_core(dev):
            return (dev[0], dev[1], 1 - dev[2])

        def copy(a, sem, block, to, rows=None, src=None):
            return pltpu.make_async_remote_copy(
                src_ref=slot(a, block, rows) if src is None else src, dst_ref=slot(a, block, rows),
                send_sem=send_sems.at[a, sem], recv_sem=recv_sems.at[a, sem], device_id=to, device_id_type=_MESH_ID)

        mine = [pltpu.make_async_copy(x_refs[a], slot(a, me), local_sems.at[a]) for a in range(n)]
        sent = []
        for a in range(n):
            mine[a].start()
            sent += [copy(a, OWN_SIB, me, sibling, src=x_refs[a]), copy(a, OWN_X, me, x_nbr, src=x_refs[a]),
                     copy(a, OWN_Y, me, y_nbr, src=x_refs[a])]
        for cp in sent:
            cp.start()
        for a in range(n):
            first, second = halves[a]
            copy(a, OWN_Y, y_nbr, me).wait_recv()
            sent += [copy(a, FWD_X, y_nbr, x_nbr, rows=first), copy(a, SIB_Y, y_nbr, sibling)]
            sent[-2].start()
            sent[-1].start()
            copy(a, OWN_X, x_nbr, me).wait_recv()
            sent += [copy(a, FWD_Y, x_nbr, y_nbr, rows=second), copy(a, SIB_X, x_nbr, sibling)]
            sent[-2].start()
            sent[-1].start()
        for a in range(n):
            first, second = halves[a]
            copy(a, FWD_X, diag, me, rows=first).wait_recv()
            sent.append(copy(a, SIB_DA, diag, sibling, rows=first))
            sent[-1].start()
            copy(a, FWD_Y, diag, me, rows=second).wait_recv()
            sent.append(copy(a, SIB_DB, diag, sibling, rows=second))
            sent[-1].start()
        for a in range(n):
            first, second = halves[a]
            copy(a, OWN_SIB, sibling, me).wait_recv()
            copy(a, SIB_X, other_core(x_nbr), me).wait_recv()
            copy(a, SIB_Y, other_core(y_nbr), me).wait_recv()
            copy(a, SIB_DA, other_core(diag), me, rows=first).wait_recv()
            copy(a, SIB_DB, other_core(diag), me, rows=second).wait_recv()
        for cp in sent:
            cp.wait_send()
        for cp in mine:
            cp.wait()

    return pl.pallas_call(
        body, name="weight_all_gather",
        out_shape=[jax.ShapeDtypeStruct((N_DEV,) + a.shape, a.dtype) for a in arrays],
        in_specs=[_ANY] * n, out_specs=[_ANY] * n,
        scratch_shapes=[pltpu.SemaphoreType.DMA((n, 9)), pltpu.SemaphoreType.DMA((n, 9)),
                        pltpu.SemaphoreType.DMA((n,))],
    )(*arrays)


def _pair_exchange(sends):
    n = len(sends)

    def body(*refs):
        s_refs, r_refs = refs[:n], refs[n:2 * n]
        send_sems, recv_sems = refs[2 * n:]
        mx, my, mc = lax.axis_index("x"), lax.axis_index("y"), lax.axis_index("c")
        copies = []
        for a in range(n):
            for ch in range(4):
                cp = pltpu.make_async_remote_copy(
                    src_ref=s_refs[a].at[2 * ch + (1 - mc)], dst_ref=r_refs[a].at[ch], send_sem=send_sems.at[a, ch],
                    recv_sem=recv_sems.at[a, ch], device_id=(mx, my, 1 - mc), device_id_type=_MESH_ID)
                cp.start()
                copies.append(cp)
        for cp in copies:
            cp.wait_send()
            cp.wait_recv()

    return pl.pallas_call(
        body, name="grad_pair_exchange",
        out_shape=[jax.ShapeDtypeStruct((4,) + s.shape[1:], s.dtype) for s in sends],
        in_specs=[_ANY] * n, out_specs=[_ANY] * n,
        scratch_shapes=[pltpu.SemaphoreType.DMA((n, 4)), pltpu.SemaphoreType.DMA((n, 4))],
    )(*sends)


def _pair_add(send, half, core):
    _, r, c_ = send.shape
    tr = _row_tile(r, c_)

    def body(core_ref, s_ref, h_ref, o_ref):
        del core_ref
        o_ref[...] = (s_ref[...].astype(F32) + h_ref[...].astype(F32)).astype(BF16)

    blk = pl.BlockSpec((1, tr, c_), lambda ch, i, core_ref: (ch, i, 0))
    return pl.pallas_call(
        body, name="grad_pair_add",
        grid_spec=pltpu.PrefetchScalarGridSpec(
            num_scalar_prefetch=1, grid=(4, r // tr),
            in_specs=[pl.BlockSpec((1, tr, c_), lambda ch, i, core_ref: (2 * ch + core_ref[0], i, 0)), blk],
            out_specs=blk),
        out_shape=jax.ShapeDtypeStruct((4, r, c_), BF16),
        compiler_params=_params("parallel", "parallel"),
    )(core, send, half)


def _chip_exchange(parts):
    n = len(parts)

    def body(*refs):
        p_refs, r_refs = refs[:n], refs[n:2 * n]
        send_sems, recv_sems, local_sems = refs[2 * n:]
        mx, my, mc = lax.axis_index("x"), lax.axis_index("y"), lax.axis_index("c")
        mine = 2 * mx + my
        local = [pltpu.make_async_copy(p_refs[a].at[mine], r_refs[a].at[mine], local_sems.at[a]) for a in range(n)]
        copies = []
        for a in range(n):
            local[a].start()
            for rel in range(1, 4):
                px = 1 - mx if rel & 2 else mx
                py = 1 - my if rel & 1 else my
                cp = pltpu.make_async_remote_copy(
                    src_ref=p_refs[a].at[2 * px + py], dst_ref=r_refs[a].at[mine], send_sem=send_sems.at[a, rel - 1],
                    recv_sem=recv_sems.at[a, rel - 1], device_id=(px, py, mc), device_id_type=_MESH_ID)
                cp.start()
                copies.append(cp)
        for cp in copies:
            cp.wait_send()
            cp.wait_recv()
        for cp in local:
            cp.wait()

    return pl.pallas_call(
        body, name="grad_chip_exchange",
        out_shape=[jax.ShapeDtypeStruct(p.shape, p.dtype) for p in parts],
        in_specs=[_ANY] * n, out_specs=[_ANY] * n,
        scratch_shapes=[pltpu.SemaphoreType.DMA((n, 3)), pltpu.SemaphoreType.DMA((n, 3)),
                        pltpu.SemaphoreType.DMA((n,))],
    )(*parts)


def _row_tile(r, c_):
    tr = min(r, ADAM_BLOCK_ELEMS // (pl.cdiv(c_, LANES) * LANES))
    while r % tr:
        tr -= SUBLANES
    return tr


def _adamw(recv, w, m, v):
    r, c_ = w.shape
    tr = _row_tile(r, c_)
    n_src = recv.shape[0]

    def body(g_ref, w_ref, m_ref, v_ref, go_ref, d_ref, mo_ref, vo_ref):
        g = g_ref[0].astype(F32)
        for s in range(1, n_src):
            g = g + g_ref[s].astype(F32)
        go_ref[...] = g
        mn = ADAM_B1 * m_ref[...] + (1.0 - ADAM_B1) * g
        vn = ADAM_B2 * v_ref[...] + (1.0 - ADAM_B2) * (g * g)
        mo_ref[...] = mn
        vo_ref[...] = vn
        m_hat = mn / (1.0 - ADAM_B1 ** ADAM_STEP)
        v_hat = vn / (1.0 - ADAM_B2 ** ADAM_STEP)
        d_ref[...] = -ADAM_LR * (m_hat / (jnp.sqrt(v_hat) + ADAM_EPS) + ADAM_WD * w_ref[...])

    spec = pl.BlockSpec((tr, c_), lambda i: (i, 0))
    out = jax.ShapeDtypeStruct((r, c_), F32)
    return pl.pallas_call(
        body, name="grad_sum_adamw", grid=(r // tr,),
        in_specs=[pl.BlockSpec((n_src, tr, c_), lambda i: (0, i, 0)), spec, spec, spec],
        out_specs=[spec, spec, spec, spec], out_shape=[out, out, out, out],
        compiler_params=_params("parallel"),
    )(recv, w, m, v)


def _pad_cols(a, before, after):
    parts = []
    if before:
        parts.append(jnp.zeros(a.shape[:-1] + (before,), a.dtype))
    parts.append(a)
    if after:
        parts.append(jnp.zeros(a.shape[:-1] + (after,), a.dtype))
    return jnp.concatenate(parts, axis=-1)


def _q_head_pairs(a, axis):
    shp = a.shape
    a = a.reshape(shp[:axis] + (GQA_KV_HEADS, GQA_GROUP, HEAD_DIM) + shp[axis + 1:])
    a = jnp.swapaxes(a, axis, axis + 1)
    return a.reshape(shp)


def _q_head_unpairs(a, axis):
    shp = a.shape
    a = a.reshape(shp[:axis] + (GQA_GROUP, GQA_KV_HEADS, HEAD_DIM) + shp[axis + 1:])
    a = jnp.swapaxes(a, axis, axis + 1)
    return a.reshape(shp)


def _layout_weights(w):
    w_in = w["w_in"]
    lead = w_in.shape[:-1]
    w_in_p = jnp.concatenate([
        _q_head_pairs(w_in[..., 0:512], w_in.ndim - 1),
        w_in[..., 512:1408],
        _pad_cols(w_in[..., 1408:1440], KR_LANE0, LANES - KR_LANE0 - MLA_ROPE_DIM),
        w_in[..., 1440:],
    ], axis=-1)
    wq = w["w_q_up"]
    wq_p = _pad_cols(wq.reshape(wq.shape[:-1] + (MLA_HEADS, MLA_QK_DIM)), 0, LANES - MLA_QK_DIM)
    wq_p = wq_p.reshape(wq.shape[:-1] + (MLA_HEADS * LANES,))
    wkv = w["w_kv_up"]
    wkv4 = wkv.reshape(wkv.shape[:-1] + (MLA_HEADS, 2 * HEAD_DIM))
    wk_p = _pad_cols(wkv4[..., :HEAD_DIM], 0, LANES - HEAD_DIM).reshape(wkv.shape[:-1] + (MLA_HEADS * LANES,))
    wv_p = wkv4[..., HEAD_DIM:].reshape(wkv.shape[:-1] + (MLA_HEADS * HEAD_DIM,))
    del lead
    return {
        "w_in": w_in_p, "w_q_up": wq_p, "w_kv_up": jnp.concatenate([wk_p, wv_p], axis=-1),
        "w_branch_a": _q_head_pairs(w["w_branch_a"], w["w_branch_a"].ndim - 2), "w_branch_b": w["w_branch_b"],
        "w_o": w["w_o"], "w_ffn_up": w["w_ffn_up"], "w_ffn_down": w["w_ffn_down"],
    }


def _unlayout_grads(g):
    gi = g["w_in"]
    kr0 = Z_KR + KR_LANE0
    g_in = jnp.concatenate([
        _q_head_unpairs(gi[..., 0:512], gi.ndim - 1), gi[..., 512:1408], gi[..., kr0:kr0 + MLA_ROPE_DIM],
        gi[..., Z_GATE:],
    ], axis=-1)
    gq = g["w_q_up"]
    gq = gq.reshape(gq.shape[:-1] + (MLA_HEADS, LANES))[..., :MLA_QK_DIM]
    gq = gq.reshape(gq.shape[:-2] + (MLA_HEADS * MLA_QK_DIM,))
    gkv = g["w_kv_up"]
    gk = gkv[..., :MLA_HEADS * LANES].reshape(gkv.shape[:-1] + (MLA_HEADS, LANES))[..., :HEAD_DIM]
    gv = gkv[..., MLA_HEADS * LANES:].reshape(gkv.shape[:-1] + (MLA_HEADS, HEAD_DIM))
    gkv = jnp.concatenate([gk, gv], axis=-1).reshape(gkv.shape[:-1] + (MLA_HEADS * 2 * HEAD_DIM,))
    return {
        "w_in": g_in, "w_q_up": gq, "w_kv_up": gkv,
        "w_branch_a": _q_head_unpairs(g["w_branch_a"], g["w_branch_a"].ndim - 2), "w_branch_b": g["w_branch_b"],
        "w_o": g["w_o"], "w_ffn_up": g["w_ffn_up"], "w_ffn_down": g["w_ffn_down"],
    }


def _pack_small(parts):
    flat = jnp.concatenate([p.reshape(-1) for p in parts])
    pad = (-flat.shape[0]) % (SUBLANES * LANES)
    if pad:
        flat = jnp.concatenate([flat, jnp.zeros((pad,), flat.dtype)])
    return flat.reshape(-1, LANES)


def _unpack_small(packed, shapes):
    flat = packed.reshape(-1)
    out, off = [], 0
    for shp in shapes:
        n = int(np.prod(shp))
        out.append(flat[off:off + n].reshape(shp))
        off += n
    return out


def _shards_of(full, axis):
    shp = full.shape
    cut = shp[:axis] + (N_DEV, shp[axis] // N_DEV) + shp[axis + 1:]
    return jnp.moveaxis(full.reshape(cut), axis, 0)


def _from_shards(shards, axis):
    full = list(shards.shape[1:])
    full[axis] *= N_DEV
    return jnp.moveaxis(shards, 0, axis).reshape(full)


def _rows2d(a):
    return a.reshape(-1, a.shape[-1])


def _layer_fwd(x, u, lw, tabs):
    cos_a, sin_a, cos_b, sin_b = tabs
    z = _matmul(u, lw["w_in"], "nn", "mm_in")
    qa, ka, va, cqn, ckvn, krr = _prep_a_fwd(z, lw["gq2"], lw["gk2"], lw["gqa"], lw["gkva"], cos_a, sin_a, cos_b, sin_b)
    qb = _matmul(cqn, lw["w_q_up"], "nn", "mm_q_up")
    kvb = _matmul(ckvn, lw["w_kv_up"], "nn", "mm_kv_up")
    q_b, k_b, v_b = _prep_b_fwd(qb, kvb, krr, cos_b, sin_b)
    ya, lse_a = _attn_fwd(qa, ka, va, True, "gqa_fwd")
    yb, lse_b = _attn_fwd(q_b, k_b, v_b, False, "mla_fwd")
    ta = _matmul(ya, lw["w_branch_a"], "nn", "mm_branch_a")
    tb = _matmul(yb, lw["w_branch_b"], "nn", "mm_branch_b")
    merged = _merge_fwd(z, lw["b_gate"], ta, tb)
    m = _matmul(merged, lw["w_o"], "nn", "mm_o")
    x2, u2 = _res_norm_fwd(x, m, lw["post_mix_g"], lw["pre_ffn_g"])
    h, a = _matmul(u2, lw["w_ffn_up"], "nn", "mm_ffn_up", post="relu2")
    f = _matmul(a, lw["w_ffn_down"], "nn", "mm_ffn_down")
    x3, u_next = _res_norm_fwd(x2, f, lw["post_ffn_g"], lw["next_pre_mix_g"])
    saved = dict(u=u, z=z, qa=qa, ka=ka, va=va, cqn=cqn, ckvn=ckvn, q_b=q_b, k_b=k_b, v_b=v_b, ya=ya, yb=yb,
                 lse_a=lse_a, lse_b=lse_b, ta=ta, tb=tb, merged=merged, m=m, x2=x2, u2=u2, h=h, a=a, f=f, x3=x3)
    return x3, u_next, saved


def _layer_bwd(dx3, du_next, lw, sv, tabs, gbuf, layer):
    cos_a, sin_a, cos_b, sin_b = tabs
    g = {}
    dx3, df, dg4, dg1n = _res_norm_bwd(sv["x3"], sv["f"], lw["post_ffn_g"], lw["next_pre_mix_g"], dx3, du_next)
    g["post_ffn_g"], g["next_pre_mix_g"] = dg4, dg1n
    dh = _matmul(df, lw["w_ffn_down"], "nt", "mm_d_h", post="relu2_bwd", h=sv["h"])
    g["w_ffn_down"] = _matmul(sv["a"], df, "tn", "mm_dw_ffn_down", stack=(gbuf["w_ffn_down"], layer))
    du2 = _matmul(dh, lw["w_ffn_up"], "nt", "mm_d_u2")
    g["w_ffn_up"] = _matmul(sv["u2"], dh, "tn", "mm_dw_ffn_up", stack=(gbuf["w_ffn_up"], layer))
    dx2, dm, dg2, dg3 = _res_norm_bwd(sv["x2"], sv["m"], lw["post_mix_g"], lw["pre_ffn_g"], dx3, du2)
    g["post_mix_g"], g["pre_ffn_g"] = dg2, dg3
    dmg = _matmul(dm, lw["w_o"], "nt", "mm_d_merged")
    g["w_o"] = _matmul(sv["merged"], dm, "tn", "mm_dw_o", stack=(gbuf["w_o"], layer))
    dta, dtb, dzg_a, dzg_b, db_a, db_b = _merge_bwd(dmg, sv["z"], lw["b_gate"], sv["ta"], sv["tb"])
    g["b_gate"] = jnp.concatenate([db_a, db_b], axis=-1)
    dya, delta_a = _matmul(dta, lw["w_branch_a"], "nt", "mm_d_ya", post="delta", h=sv["ya"])
    g["w_branch_a"] = _matmul(sv["ya"], dta, "tn", "mm_dw_branch_a", stack=(gbuf["w_branch_a"], layer))
    dyb, delta_b = _matmul(dtb, lw["w_branch_b"], "nt", "mm_d_yb", post="delta", h=sv["yb"])
    g["w_branch_b"] = _matmul(sv["yb"], dtb, "tn", "mm_dw_branch_b", stack=(gbuf["w_branch_b"], layer))
    dqa, dka4, dva4 = _attn_bwd(sv["qa"], sv["ka"], sv["va"], dya, sv["lse_a"], delta_a, True, "gqa_bwd")
    dq_b, dk_b, dv_b = _attn_bwd(sv["q_b"], sv["k_b"], sv["v_b"], dyb, sv["lse_b"], delta_b, False, "mla_bwd")
    dqb, dkvb, dkr = _prep_b_bwd(dq_b, dk_b, dv_b, cos_b, sin_b)
    dcqn = _matmul(dqb, lw["w_q_up"], "nt", "mm_d_cqn")
    g["w_q_up"] = _matmul(sv["cqn"], dqb, "tn", "mm_dw_q_up", stack=(gbuf["w_q_up"], layer))
    dckvn = _matmul(dkvb, lw["w_kv_up"], "nt", "mm_d_ckvn")
    g["w_kv_up"] = _matmul(sv["ckvn"], dkvb, "tn", "mm_dw_kv_up", stack=(gbuf["w_kv_up"], layer))
    dz, dgq, dgk, dgqa, dgkva = _prep_a_bwd(sv["z"], dqa, dka4, dva4, dcqn, dckvn, dkr, dzg_a, dzg_b, lw["gq2"],
                                            lw["gk2"], lw["gqa"], lw["gkva"], cos_a, sin_a)
    g["q_norm_g"], g["k_norm_g"], g["q_a_norm_g"], g["kv_a_norm_g"] = dgq, dgk, dgqa, dgkva
    du = _matmul(dz, lw["w_in"], "nt", "mm_d_u")
    g["w_in"] = _matmul(sv["u"], dz, "tn", "mm_dw_in", stack=(gbuf["w_in"], layer))
    return dx2, du, g


def kernel(x, w_in, b_gate, q_norm_g, k_norm_g, q_a_norm_g, kv_a_norm_g, w_q_up, w_kv_up, w_branch_a, w_branch_b, w_o, w_ffn_up, w_ffn_down, pre_mix_g, post_mix_g, pre_ffn_g, post_ffn_g, loss_target, m_w_in, m_b_gate, m_q_norm_g, m_k_norm_g, m_q_a_norm_g, m_kv_a_norm_g, m_w_q_up, m_w_kv_up, m_w_branch_a, m_w_branch_b, m_w_o, m_w_ffn_up, m_w_ffn_down, m_pre_mix_g, m_post_mix_g, m_pre_ffn_g, m_post_ffn_g, v_w_in, v_b_gate, v_q_norm_g, v_k_norm_g, v_q_a_norm_g, v_kv_a_norm_g, v_w_q_up, v_w_kv_up, v_w_branch_a, v_w_branch_b, v_w_o, v_w_ffn_up, v_w_ffn_down, v_pre_mix_g, v_post_mix_g, v_pre_ffn_g, v_post_ffn_g):
    weights = dict(zip(WEIGHT_NAMES, (w_in, b_gate, q_norm_g, k_norm_g, q_a_norm_g, kv_a_norm_g, w_q_up, w_kv_up,
                                      w_branch_a, w_branch_b, w_o, w_ffn_up, w_ffn_down, pre_mix_g, post_mix_g,
                                      pre_ffn_g, post_ffn_g)))
    mom_m = dict(zip(WEIGHT_NAMES, (m_w_in, m_b_gate, m_q_norm_g, m_k_norm_g, m_q_a_norm_g, m_kv_a_norm_g, m_w_q_up,
                                    m_w_kv_up, m_w_branch_a, m_w_branch_b, m_w_o, m_w_ffn_up, m_w_ffn_down,
                                    m_pre_mix_g, m_post_mix_g, m_pre_ffn_g, m_post_ffn_g)))
    mom_v = dict(zip(WEIGHT_NAMES, (v_w_in, v_b_gate, v_q_norm_g, v_k_norm_g, v_q_a_norm_g, v_kv_a_norm_g, v_w_q_up,
                                    v_w_kv_up, v_w_branch_a, v_w_branch_b, v_w_o, v_w_ffn_up, v_w_ffn_down,
                                    v_pre_mix_g, v_post_mix_g, v_pre_ffn_g, v_post_ffn_g)))
    assert x.shape[0] == 1 and x.shape[2] == D_MODEL, x.shape
    n_layers = w_in.shape[0]
    t = x.shape[1]
    x0 = x.reshape(t, D_MODEL)
    target = loss_target.reshape(t, D_MODEL)
    shard_shapes = {n: weights[n].shape for n in BIG_NAMES}
    small_shapes = [weights[n].shape for n in SMALL_NAMES]

    gathered = _all_gather([weights[n].astype(BF16) for n in BIG_NAMES])
    full = {n: _from_shards(g, SHARD_AXIS[n]) for n, g in zip(BIG_NAMES, gathered)}
    lw_all = _layout_weights(full)
    lw_all["b_gate"] = b_gate.reshape(n_layers, 1, 2 * D_MODEL)
    lw_all["gq2"] = jnp.tile(q_norm_g, (1, 2)).reshape(n_layers, 1, LANES)
    lw_all["gk2"] = jnp.tile(k_norm_g, (1, 2)).reshape(n_layers, 1, LANES)
    lw_all["gqa"] = q_a_norm_g.reshape(n_layers, 1, MLA_Q_RANK)
    lw_all["gkva"] = kv_a_norm_g.reshape(n_layers, 1, MLA_KV_RANK)
    for n in ("post_mix_g", "pre_ffn_g", "post_ffn_g"):
        lw_all[n] = weights[n]
    lw_all["next_pre_mix_g"] = jnp.roll(pre_mix_g, -1, axis=0)

    tabs = _rope_tables(t)
    u0 = _rms_fwd(x0, pre_mix_g[0])

    layer_w = [{n: a[l] for n, a in lw_all.items()} for l in range(n_layers)]
    xc, uc, saved = x0, u0, []
    for l in range(n_layers):
        xc, uc, sv = _layer_fwd(xc, uc, layer_w[l], tabs)
        saved.append(sv)
    dy, loss_acc = _loss_grad(xc, target)
    loss = lax.psum(0.5 * jnp.sum(loss_acc) / D_MODEL, ("x", "y", "c"))

    dx0, du0, layer_g = dy, jnp.zeros((t, D_MODEL), F32), [None] * n_layers
    gbuf = {n: lax.empty(lw_all[n].shape, BF16) for n in BIG_NAMES}
    for l in reversed(range(n_layers)):
        dx0, du0, layer_g[l] = _layer_bwd(dx0, du0, layer_w[l], saved[l], tabs, gbuf, l)
        gbuf = {n: layer_g[l][n] for n in BIG_NAMES}
    grads = {n: jnp.stack([g[n] for g in layer_g]) for n in layer_g[0] if n not in BIG_NAMES}
    grads.update(gbuf)
    grad_x, dg1_first = _rms_bwd(x0, pre_mix_g[0], dx0, du0)

    big_grads = _unlayout_grads({n: grads[n] for n in BIG_NAMES})
    fold = lambda a: a.sum(axis=1)
    dgq = fold(grads["q_norm_g"]).reshape(n_layers, 2, HEAD_DIM).sum(axis=1)
    dgk = fold(grads["k_norm_g"]).reshape(n_layers, 2, HEAD_DIM).sum(axis=1)
    dg1 = jnp.concatenate([fold(dg1_first[None]), fold(grads["next_pre_mix_g"])[:-1]], axis=0)
    small_grads = {
        "b_gate": fold(grads["b_gate"]), "q_norm_g": dgq, "k_norm_g": dgk, "q_a_norm_g": fold(grads["q_a_norm_g"]),
        "kv_a_norm_g": fold(grads["kv_a_norm_g"]), "pre_mix_g": dg1, "post_mix_g": fold(grads["post_mix_g"]),
        "pre_ffn_g": fold(grads["pre_ffn_g"]), "post_ffn_g": fold(grads["post_ffn_g"]),
    }
    small_packed = _pack_small([small_grads[n] for n in SMALL_NAMES])
    sends = [_shards_of(big_grads[n], SHARD_AXIS[n]).reshape((N_DEV,) + _rows2d(weights[n]).shape)
             for n in BIG_NAMES]
    sends.append(jnp.broadcast_to(small_packed[None], (N_DEV,) + small_packed.shape))
    halves = _pair_exchange(sends)
    core = lax.axis_index("c").astype(jnp.int32).reshape(1)
    recvs = _chip_exchange([_pair_add(s, h, core) for s, h in zip(sends, halves)])

    results = {}
    for n, recv in zip(BIG_NAMES, recvs):
        res = _adamw(recv, _rows2d(weights[n]), _rows2d(mom_m[n]), _rows2d(mom_v[n]))
        results[n] = [r.reshape(shard_shapes[n]) for r in res]
    res = _adamw(recvs[-1], *[_pack_small([d[n] for n in SMALL_NAMES]) for d in (weights, mom_m, mom_v)])
    for kind, packed_out in enumerate(res):
        for n, val in zip(SMALL_NAMES, _unpack_small(packed_out, small_shapes)):
            results.setdefault(n, [None] * 4)[kind] = val
    outs = [results[n][kind] for kind in range(4) for n in WEIGHT_NAMES]
    return (loss, grad_x.reshape(x.shape), *outs)
```

```python
import math

import jax
import jax.numpy as jnp
import numpy as np
from jax import lax
from jax.experimental import pallas as pl
from jax.experimental.pallas import tpu as pltpu

F32 = jnp.float32
BF16 = jnp.bfloat16

D_MODEL = 1024
GRID_W = 64
ROPE_THETA = 10000.0
EPS = 1e-6
GQA_HEADS = 8
GQA_KV_HEADS = 2
GQA_GROUP = GQA_HEADS // GQA_KV_HEADS
HEAD_DIM = 64
MLA_HEADS = 8
MLA_ROPE_DIM = 32
MLA_QK_DIM = 96
MLA_Q_RANK = 384
MLA_KV_RANK = 256
GQA_SCALE = 1.0 / math.sqrt(HEAD_DIM)
MLA_SCALE = 1.0 / math.sqrt(MLA_QK_DIM)
LOG2E = math.log2(math.e)
LN2 = math.log(2.0)

ADAM_LR = 0.001
ADAM_B1 = 0.9
ADAM_B2 = 0.999
ADAM_EPS = 1e-08
ADAM_WD = 0.01
ADAM_STEP = 10

N_DEV = 8
LANES = 128
SUBLANES = 8
VMEM_LIMIT = 48 * 1024 * 1024

Z_QA, Z_KA, Z_VA, Z_CQ, Z_CKV, Z_KR, Z_GATE = 0, 512, 640, 768, 1152, 1408, 1536
Z_ATT_W = 1536
Z_W = 3584
KR_LANE0 = 64

WEIGHT_NAMES = ("w_in", "b_gate", "q_norm_g", "k_norm_g", "q_a_norm_g", "kv_a_norm_g", "w_q_up", "w_kv_up",
                "w_branch_a", "w_branch_b", "w_o", "w_ffn_up", "w_ffn_down", "pre_mix_g", "post_mix_g",
                "pre_ffn_g", "post_ffn_g")
SHARD_AXIS = {"w_in": 2, "w_q_up": 2, "w_kv_up": 2, "w_branch_a": 2, "w_branch_b": 2, "w_o": 1, "w_ffn_up": 2,
              "w_ffn_down": 1}
BIG_NAMES = tuple(n for n in WEIGHT_NAMES if n in SHARD_AXIS)
SMALL_NAMES = tuple(n for n in WEIGHT_NAMES if n not in SHARD_AXIS)
ADAM_BLOCK_ELEMS = 256 * 1024
MM_TILE = 1024
MM_TILE_TOKENS = 2048
MM_TILE_K = 2048
PREP_ROWS = 512
ROW_TILE = 1024
ATTN_TQ = 1024
ATTN_TK = 1024


def _params(*semantics):
    return pltpu.CompilerParams(dimension_semantics=semantics, vmem_limit_bytes=VMEM_LIMIT)


def _tile(n, pref):
    if n <= pref:
        return n
    t = (pref // LANES) * LANES
    while n % t:
        t -= LANES
    return t


def _fold8(t):
    return t.reshape(t.shape[0] // SUBLANES, SUBLANES, t.shape[1]).sum(axis=0)


_DIMS = {"nn": ((1,), (0,)), "nt": ((1,), (1,)), "tn": ((0,), (0,))}


def _matmul(a, b, mode, name, post=None, h=None, stack=None):
    out_dt = BF16
    if mode == "nn":
        (m, k), n = a.shape, b.shape[1]
    elif mode == "nt":
        (m, k), n = a.shape, b.shape[0]
    else:
        (k, m), n = a.shape, b.shape[1]
    tm = _tile(m, MM_TILE_TOKENS if mode != "tn" and k <= MM_TILE else MM_TILE)
    tn, tk = _tile(n, MM_TILE), _tile(k, MM_TILE_K)
    nk = k // tk
    dims = (_DIMS[mode], ((), ()))
    operands = [a, b] + ([h] if post in ("relu2_bwd", "delta") else []) + ([stack[0]] if stack else [])
    n_in = len(operands)
    n_out = 2 if post in ("relu2", "delta") else 1
    assert post != "delta" or tn == n, (n, tn)

    def body(*refs):
        a_ref, b_ref = refs[:2]
        o_refs, acc_ref = refs[n_in:n_in + n_out], refs[-1]

        def finish(val):
            if post == "relu2":
                o_refs[0][...] = val.astype(out_dt)
                r = jnp.maximum(val, 0.0)
                o_refs[1][...] = (r * r).astype(BF16)
            elif post == "relu2_bwd":
                o_refs[0][...] = (val * (2.0 * jnp.maximum(refs[2][...].astype(F32), 0.0))).astype(BF16)
            elif post == "delta":
                do = val.astype(BF16)
                o_refs[0][...] = do
                prod = do.astype(F32) * refs[2][...].astype(F32)
                for g in range(n // LANES):
                    x = prod[:, LANES * g:LANES * (g + 1)]
                    lo = _lo_mask(x.shape)
                    d0 = jnp.sum(jnp.where(lo, x, 0.0), axis=-1, keepdims=True)
                    d1 = jnp.sum(jnp.where(lo, 0.0, x), axis=-1, keepdims=True)
                    o_refs[1][2 * g] = jnp.broadcast_to(d0, (tm, LANES))
                    o_refs[1][2 * g + 1] = jnp.broadcast_to(d1, (tm, LANES))
            else:
                o_refs[0][...] = val.astype(out_dt)

        prod = lax.dot_general(a_ref[...], b_ref[...], dims, preferred_element_type=F32)
        if nk == 1:
            finish(prod)
        else:
            kk = pl.program_id(2)

            @pl.when(kk == 0)
            def _():
                acc_ref[...] = prod

            @pl.when(kk > 0)
            def _():
                acc_ref[...] += prod

            @pl.when(kk == nk - 1)
            def _():
                finish(acc_ref[...])

    if mode == "tn":
        a_spec = pl.BlockSpec((tk, tm), lambda i, j, kk: (kk, i))
    else:
        a_spec = pl.BlockSpec((tm, tk), lambda i, j, kk: (i, kk))
    if mode == "nt":
        b_spec = pl.BlockSpec((tn, tk), lambda i, j, kk: (j, kk))
    else:
        b_spec = pl.BlockSpec((tk, tn), lambda i, j, kk: (kk, j))
    o_spec = pl.BlockSpec((tm, tn), lambda i, j, kk: (i, j))
    main_out, bf16_out = jax.ShapeDtypeStruct((m, n), out_dt), jax.ShapeDtypeStruct((m, n), BF16)
    heads = n // HEAD_DIM
    delta_out = jax.ShapeDtypeStruct((heads, m, LANES), F32)
    out_shape = {None: main_out, "relu2": [main_out, bf16_out], "relu2_bwd": bf16_out,
                 "delta": [bf16_out, delta_out]}[post]
    in_specs = [a_spec, b_spec] + ([o_spec] if post in ("relu2_bwd", "delta") else [])
    out_specs = {None: o_spec, "relu2": [o_spec, o_spec], "relu2_bwd": o_spec,
                 "delta": [o_spec, pl.BlockSpec((heads, tm, LANES), lambda i, j, kk: (0, i, 0))]}[post]
    aliases = {}
    if stack:
        buf, layer = stack
        assert post is None and buf.shape[1:] == (m, n) and buf.dtype == out_dt, (buf.shape, buf.dtype)
        in_specs.append(pl.BlockSpec(memory_space=pl.ANY))
        out_specs = pl.BlockSpec((None, tm, tn), lambda i, j, kk: (layer, i, j))
        out_shape = jax.ShapeDtypeStruct(buf.shape, buf.dtype)
        aliases = {n_in - 1: 0}
    return pl.pallas_call(
        body,
        name=name,
        grid=(m // tm, n // tn, nk),
        in_specs=in_specs,
        out_specs=out_specs,
        out_shape=out_shape,
        scratch_shapes=[pltpu.VMEM((tm, tn), F32)],
        input_output_aliases=aliases,
        compiler_params=_params("parallel", "parallel", "arbitrary"),
    )(*operands)


def _rinv(x):
    return lax.rsqrt(jnp.mean(x * x, axis=-1, keepdims=True) + EPS)


def _rms_bwd_rows(x, g, dy):
    r = _rinv(x)
    xh = x * r
    dxh = dy * g
    dx = r * (dxh - xh * jnp.mean(dxh * xh, axis=-1, keepdims=True))
    return dx, dy * xh


def _row_spec(tm, c):
    return pl.BlockSpec((tm, c), lambda i: (i, 0))


def _vec_spec(c):
    return pl.BlockSpec((1, c), lambda i: (0, 0))


def _acc_spec(c):
    return pl.BlockSpec((SUBLANES, c), lambda i: (0, 0))


def _rms_fwd(x, g):
    t, d = x.shape
    tm = _tile(t, ROW_TILE)

    def body(x_ref, g_ref, o_ref):
        xv = x_ref[...]
        o_ref[...] = (xv * _rinv(xv) * g_ref[...]).astype(BF16)

    return pl.pallas_call(
        body, name="rms_fwd", grid=(t // tm,),
        in_specs=[_row_spec(tm, d), _vec_spec(d)], out_specs=_row_spec(tm, d),
        out_shape=jax.ShapeDtypeStruct((t, d), BF16), compiler_params=_params("parallel"),
    )(x, g.reshape(1, d))


def _rms_bwd(x, g, dres, dy):
    t, d = x.shape
    tm = _tile(t, ROW_TILE)

    def body(x_ref, g_ref, dres_ref, dy_ref, dx_ref, dg_ref):
        dx, dgc = _rms_bwd_rows(x_ref[...], g_ref[...], dy_ref[...].astype(F32))
        dx_ref[...] = dres_ref[...] + dx

        @pl.when(pl.program_id(0) == 0)
        def _():
            dg_ref[...] = jnp.zeros_like(dg_ref)

        dg_ref[...] += _fold8(dgc)

    return pl.pallas_call(
        body, name="rms_bwd", grid=(t // tm,),
        in_specs=[_row_spec(tm, d), _vec_spec(d), _row_spec(tm, d), _row_spec(tm, d)],
        out_specs=[_row_spec(tm, d), _acc_spec(d)],
        out_shape=[jax.ShapeDtypeStruct((t, d), F32), jax.ShapeDtypeStruct((SUBLANES, d), F32)],
        compiler_params=_params("arbitrary"),
    )(x, g.reshape(1, d), dres, dy)


def _res_norm_fwd(x, m, g_post, g_next):
    t, d = x.shape
    tm = _tile(t, ROW_TILE)

    def body(x_ref, m_ref, gp_ref, gn_ref, x2_ref, u2_ref):
        mv = m_ref[...].astype(F32)
        x2 = x_ref[...] + mv * _rinv(mv) * gp_ref[...]
        x2_ref[...] = x2
        u2_ref[...] = (x2 * _rinv(x2) * gn_ref[...]).astype(BF16)

    return pl.pallas_call(
        body, name="res_norm_fwd", grid=(t // tm,),
        in_specs=[_row_spec(tm, d), _row_spec(tm, d), _vec_spec(d), _vec_spec(d)],
        out_specs=[_row_spec(tm, d), _row_spec(tm, d)],
        out_shape=[jax.ShapeDtypeStruct((t, d), F32), jax.ShapeDtypeStruct((t, d), BF16)],
        compiler_params=_params("parallel"),
    )(x, m, g_post.reshape(1, d), g_next.reshape(1, d))


def _res_norm_bwd(x2, m, g_post, g_next, dx2_in, du2):
    t, d = x2.shape
    tm = _tile(t, ROW_TILE // 2)

    def body(x2_ref, m_ref, gp_ref, gn_ref, dx2in_ref, du2_ref, dx2_ref, dm_ref, dgp_ref, dgn_ref):
        dxn, dgn_c = _rms_bwd_rows(x2_ref[...], gn_ref[...], du2_ref[...].astype(F32))
        dx2 = dx2in_ref[...] + dxn
        dx2_ref[...] = dx2
        dm, dgp_c = _rms_bwd_rows(m_ref[...].astype(F32), gp_ref[...], dx2)
        dm_ref[...] = dm.astype(BF16)

        @pl.when(pl.program_id(0) == 0)
        def _():
            dgp_ref[...] = jnp.zeros_like(dgp_ref)
            dgn_ref[...] = jnp.zeros_like(dgn_ref)

        dgp_ref[...] += _fold8(dgp_c)
        dgn_ref[...] += _fold8(dgn_c)

    return pl.pallas_call(
        body, name="res_norm_bwd", grid=(t // tm,),
        in_specs=[_row_spec(tm, d), _row_spec(tm, d), _vec_spec(d), _vec_spec(d), _row_spec(tm, d), _row_spec(tm, d)],
        out_specs=[_row_spec(tm, d), _row_spec(tm, d), _acc_spec(d), _acc_spec(d)],
        out_shape=[jax.ShapeDtypeStruct((t, d), F32), jax.ShapeDtypeStruct((t, d), BF16),
                   jax.ShapeDtypeStruct((SUBLANES, d), F32), jax.ShapeDtypeStruct((SUBLANES, d), F32)],
        compiler_params=_params("arbitrary"),
    )(x2, m, g_post.reshape(1, d), g_next.reshape(1, d), dx2_in, du2)


def _rope_tables(t):
    rows = t // GRID_W
    row = jnp.repeat(jnp.arange(rows, dtype=F32), GRID_W)
    col = jnp.tile(jnp.arange(GRID_W, dtype=F32), rows)

    def tab(rot_dim):
        half = rot_dim // 2
        inv = ROPE_THETA ** (-jnp.arange(0, half, 2, dtype=F32) / half)
        ar = row[:, None] * inv[None, :]
        ac = col[:, None] * inv[None, :]
        ang = jnp.concatenate([ar, ar, ac, ac], axis=-1)
        q = half // 2
        sign = np.tile(np.concatenate([-np.ones(q, np.float32), np.ones(q, np.float32)]), 2)
        return jnp.cos(ang), jnp.sin(ang) * sign[None, :]

    ca, sa = tab(HEAD_DIM)
    cb, sb = tab(MLA_ROPE_DIM)
    one = jnp.ones((t, 1), F32)
    cos_b = jnp.concatenate([one * jnp.ones((1, KR_LANE0), F32), cb, one * jnp.ones((1, 32), F32)], axis=-1)
    sin_b = jnp.concatenate([jnp.zeros((t, KR_LANE0), F32), sb, jnp.zeros((t, 32), F32)], axis=-1)
    return jnp.tile(ca, (1, GQA_HEADS)), jnp.tile(sa, (1, GQA_HEADS)), cos_b, sin_b


def _swap_halves(x, sh):
    lane = lax.broadcasted_iota(jnp.int32, x.shape, 1)
    up = pltpu.roll(x, LANES - sh, 1)
    dn = pltpu.roll(x, sh, 1)
    return jnp.where((lane & (2 * sh - 1)) < sh, up, dn)


def _rope(x, cos, sin_s, sh):
    return x * cos + _swap_halves(x, sh) * sin_s


def _rope_bwd(dy, cos, sin_s, sh):
    return dy * cos + _swap_halves(dy * sin_s, sh)


def _lo_mask(shape):
    return lax.broadcasted_iota(jnp.int32, shape, 1) < HEAD_DIM


def _half_mean(t, lo):
    s_lo = jnp.sum(jnp.where(lo, t, 0.0), axis=-1, keepdims=True)
    s_hi = jnp.sum(jnp.where(lo, 0.0, t), axis=-1, keepdims=True)
    return jnp.where(lo, s_lo, s_hi) * (1.0 / HEAD_DIM)


def _head_norm(x, g2):
    lo = _lo_mask(x.shape)
    r = lax.rsqrt(_half_mean(x * x, lo) + EPS)
    return x * r * g2


def _head_norm_bwd(x, g2, dy):
    lo = _lo_mask(x.shape)
    r = lax.rsqrt(_half_mean(x * x, lo) + EPS)
    xh = x * r
    dxh = dy * g2
    dx = r * (dxh - xh * _half_mean(dxh * xh, lo))
    return dx, dy * xh


def _prep_a_fwd(z, gq2, gk2, gqa, gkva, cos_a, sin_a, cos_b, sin_b):
    t = z.shape[0]
    tm = _tile(t, ROW_TILE)

    def body(z_ref, gq_ref, gk_ref, gqa_ref, gkva_ref, ca_ref, sa_ref, cb_ref, sb_ref,
             qa_ref, ka_ref, va_ref, cqn_ref, ckvn_ref, krr_ref):
        def zf(lo, hi):
            return z_ref[:, lo:hi].astype(F32)

        for j in range(4):
            cols = slice(LANES * j, LANES * (j + 1))
            y = _rope(_head_norm(zf(LANES * j, LANES * (j + 1)), gq_ref[...]), ca_ref[:, cols], sa_ref[:, cols], 16)
            qa_ref[:, cols] = (y * (GQA_SCALE * LOG2E)).astype(BF16)
        y = _rope(_head_norm(zf(Z_KA, Z_VA), gk_ref[...]), ca_ref[:, :LANES], sa_ref[:, :LANES], 16)
        ka_ref[...] = y.astype(BF16)
        va_ref[...] = z_ref[:, Z_VA:Z_CQ].astype(BF16)
        cq = zf(Z_CQ, Z_CKV)
        cqn_ref[...] = (cq * _rinv(cq) * gqa_ref[...]).astype(BF16)
        ckv = zf(Z_CKV, Z_KR)
        ckvn_ref[...] = (ckv * _rinv(ckv) * gkva_ref[...]).astype(BF16)
        krr_ref[...] = _rope(zf(Z_KR, Z_GATE), cb_ref[...], sb_ref[...], 8)

    return pl.pallas_call(
        body, name="prep_a_fwd", grid=(t // tm,),
        in_specs=[_row_spec(tm, Z_ATT_W), _vec_spec(LANES), _vec_spec(LANES), _vec_spec(MLA_Q_RANK),
                  _vec_spec(MLA_KV_RANK), _row_spec(tm, 512), _row_spec(tm, 512), _row_spec(tm, LANES),
                  _row_spec(tm, LANES)],
        out_specs=[_row_spec(tm, 512), _row_spec(tm, LANES), _row_spec(tm, LANES), _row_spec(tm, MLA_Q_RANK),
                   _row_spec(tm, MLA_KV_RANK), _row_spec(tm, LANES)],
        out_shape=[jax.ShapeDtypeStruct((t, 512), BF16), jax.ShapeDtypeStruct((t, LANES), BF16),
                   jax.ShapeDtypeStruct((t, LANES), BF16), jax.ShapeDtypeStruct((t, MLA_Q_RANK), BF16),
                   jax.ShapeDtypeStruct((t, MLA_KV_RANK), BF16), jax.ShapeDtypeStruct((t, LANES), F32)],
        compiler_params=_params("parallel"),
    )(z, gq2, gk2, gqa, gkva, cos_a, sin_a, cos_b, sin_b)


def _prep_a_bwd(z, dqa, dka4, dva4, dcqn, dckvn, dkr, dzga, dzgb, gq2, gk2, gqa, gkva, cos_a, sin_a):
    t = z.shape[0]
    tm = _tile(t, PREP_ROWS)

    def body(z_ref, dqa_ref, dka_ref, dva_ref, dcqn_ref, dckvn_ref, dkr_ref, dzga_ref, dzgb_ref, gq_ref, gk_ref,
             gqa_ref, gkva_ref, ca_ref, sa_ref, dz_ref, dgq_ref, dgk_ref, dgqa_ref, dgkva_ref):
        @pl.when(pl.program_id(0) == 0)
        def _():
            dgq_ref[...] = jnp.zeros_like(dgq_ref)
            dgk_ref[...] = jnp.zeros_like(dgk_ref)
            dgqa_ref[...] = jnp.zeros_like(dgqa_ref)
            dgkva_ref[...] = jnp.zeros_like(dgkva_ref)

        def zf(lo, hi):
            return z_ref[:, lo:hi].astype(F32)

        dgq = jnp.zeros((SUBLANES, LANES), F32)
        for j in range(4):
            cols = slice(LANES * j, LANES * (j + 1))
            dy = _rope_bwd(dqa_ref[:, cols] * GQA_SCALE, ca_ref[:, cols], sa_ref[:, cols], 16)
            dx, dgc = _head_norm_bwd(zf(LANES * j, LANES * (j + 1)), gq_ref[...], dy)
            dz_ref[:, cols] = dx.astype(BF16)
            dgq = dgq + _fold8(dgc)
        dgq_ref[...] += dgq
        dk = (dka_ref[0].astype(F32) + dka_ref[1].astype(F32) + dka_ref[2].astype(F32)
              + dka_ref[3].astype(F32)).T * LN2
        dy = _rope_bwd(dk, ca_ref[:, :LANES], sa_ref[:, :LANES], 16)
        dx, dgc = _head_norm_bwd(zf(Z_KA, Z_VA), gk_ref[...], dy)
        dz_ref[:, Z_KA:Z_VA] = dx.astype(BF16)
        dgk_ref[...] += _fold8(dgc)
        dz_ref[:, Z_VA:Z_CQ] = (dva_ref[0].astype(F32) + dva_ref[1].astype(F32) + dva_ref[2].astype(F32)
                                + dva_ref[3].astype(F32)).T.astype(BF16)
        dx, dgc = _rms_bwd_rows(zf(Z_CQ, Z_CKV), gqa_ref[...], dcqn_ref[...].astype(F32))
        dz_ref[:, Z_CQ:Z_CKV] = dx.astype(BF16)
        dgqa_ref[...] += _fold8(dgc)
        dx, dgc = _rms_bwd_rows(zf(Z_CKV, Z_KR), gkva_ref[...], dckvn_ref[...].astype(F32))
        dz_ref[:, Z_CKV:Z_KR] = dx.astype(BF16)
        dgkva_ref[...] += _fold8(dgc)
        dz_ref[:, Z_KR:Z_GATE] = dkr_ref[...].astype(BF16)
        dz_ref[:, Z_GATE:Z_GATE + D_MODEL] = dzga_ref[...]
        dz_ref[:, Z_GATE + D_MODEL:Z_W] = dzgb_ref[...]

    part = pl.BlockSpec((4, LANES, tm), lambda i: (0, 0, i))
    return pl.pallas_call(
        body, name="prep_a_bwd", grid=(t // tm,),
        in_specs=[_row_spec(tm, Z_ATT_W), _row_spec(tm, 512), part, part, _row_spec(tm, MLA_Q_RANK),
                  _row_spec(tm, MLA_KV_RANK), _row_spec(tm, LANES), _row_spec(tm, D_MODEL), _row_spec(tm, D_MODEL),
                  _vec_spec(LANES),
                  _vec_spec(LANES), _vec_spec(MLA_Q_RANK), _vec_spec(MLA_KV_RANK), _row_spec(tm, 512),
                  _row_spec(tm, 512)],
        out_specs=[_row_spec(tm, Z_W), _acc_spec(LANES), _acc_spec(LANES), _acc_spec(MLA_Q_RANK),
                   _acc_spec(MLA_KV_RANK)],
        out_shape=[jax.ShapeDtypeStruct((t, Z_W), BF16), jax.ShapeDtypeStruct((SUBLANES, LANES), F32),
                   jax.ShapeDtypeStruct((SUBLANES, LANES), F32), jax.ShapeDtypeStruct((SUBLANES, MLA_Q_RANK), F32),
                   jax.ShapeDtypeStruct((SUBLANES, MLA_KV_RANK), F32)],
        compiler_params=_params("arbitrary"),
    )(z, dqa, dka4, dva4, dcqn, dckvn, dkr, dzga, dzgb, gq2, gk2, gqa, gkva, cos_a, sin_a)


def _prep_b_fwd(qb, kvb, krr, cos_b, sin_b):
    t = qb.shape[0]
    tm = _tile(t, ROW_TILE)

    def body(qb_ref, kvb_ref, krr_ref, cb_ref, sb_ref, q_ref, k_ref, v_ref):
        for h in range(MLA_HEADS):
            cols = slice(LANES * h, LANES * (h + 1))
            qh = _rope(qb_ref[:, cols].astype(F32), cb_ref[...], sb_ref[...], 8)
            q_ref[:, cols] = (qh * (MLA_SCALE * LOG2E)).astype(BF16)
            k_ref[:, cols] = (kvb_ref[:, cols].astype(F32) + krr_ref[...]).astype(BF16)
        v_ref[...] = kvb_ref[:, 1024:1536].astype(BF16)

    return pl.pallas_call(
        body, name="prep_b_fwd", grid=(t // tm,),
        in_specs=[_row_spec(tm, 1024), _row_spec(tm, 1536), _row_spec(tm, LANES), _row_spec(tm, LANES),
                  _row_spec(tm, LANES)],
        out_specs=[_row_spec(tm, 1024), _row_spec(tm, 1024), _row_spec(tm, 512)],
        out_shape=[jax.ShapeDtypeStruct((t, 1024), BF16), jax.ShapeDtypeStruct((t, 1024), BF16),
                   jax.ShapeDtypeStruct((t, 512), BF16)],
        compiler_params=_params("parallel"),
    )(qb, kvb, krr, cos_b, sin_b)


def _prep_b_bwd(dq, dk, dv, cos_b, sin_b):
    t = dq.shape[0]
    tm = _tile(t, ROW_TILE)

    def body(dq_ref, dk_ref, dv_ref, cb_ref, sb_ref, dqb_ref, dkvb_ref, dkr_ref):
        dkr = jnp.zeros((tm, LANES), F32)
        for h in range(MLA_HEADS):
            cols = slice(LANES * h, LANES * (h + 1))
            dqb_ref[:, cols] = _rope_bwd(dq_ref[:, cols] * MLA_SCALE, cb_ref[...], sb_ref[...], 8).astype(BF16)
            dkh = dk_ref[cols, :].astype(F32).T * LN2
            dkvb_ref[:, cols] = dkh.astype(BF16)
            dkr = dkr + dkh
        for j in range(MLA_HEADS // 2):
            dv_j = dv_ref[LANES * j:LANES * (j + 1), :].astype(F32).T
            dkvb_ref[:, 1024 + LANES * j:1024 + LANES * (j + 1)] = dv_j.astype(BF16)
        dkr_ref[...] = _rope_bwd(dkr, cb_ref[...], sb_ref[...], 8)

    return pl.pallas_call(
        body, name="prep_b_bwd", grid=(t // tm,),
        in_specs=[_row_spec(tm, 1024), pl.BlockSpec((1024, tm), lambda i: (0, i)),
                  pl.BlockSpec((512, tm), lambda i: (0, i)), _row_spec(tm, LANES),
                  _row_spec(tm, LANES)],
        out_specs=[_row_spec(tm, 1024), _row_spec(tm, 1536), _row_spec(tm, LANES)],
        out_shape=[jax.ShapeDtypeStruct((t, 1024), BF16), jax.ShapeDtypeStruct((t, 1536), BF16),
                   jax.ShapeDtypeStruct((t, LANES), F32)],
        compiler_params=_params("parallel"),
    )(dq, dk, dv, cos_b, sin_b)


_NT = (((1,), (1,)), ((), ()))
_NN = (((1,), (0,)), ((), ()))
_TN = (((0,), (0,)), ((), ()))


def _head_operands(qv, kv, i, shared_k):
    if shared_k:
        lo = _lo_mask(qv.shape)
        keep = lo if i == 0 else jnp.logical_not(lo)
        return jnp.where(keep, qv, jnp.zeros_like(qv)), kv
    cols = slice(LANES * i, LANES * (i + 1))
    return qv[:, cols], kv[:, cols]


def _attn_specs(shared_k, tq, tk, q_of, k_of):
    wq = LANES if shared_k else 2 * LANES
    q_spec = pl.BlockSpec((tq, wq), lambda *g: (q_of(*g), g[0]))
    if shared_k:
        k_spec = pl.BlockSpec((tk, LANES), lambda *g: (k_of(*g), 0))
        v_spec = pl.BlockSpec((tk, LANES), lambda *g: (k_of(*g), 0))
    else:
        k_spec = pl.BlockSpec((tk, wq), lambda *g: (k_of(*g), g[0]))
        v_spec = pl.BlockSpec((tk, LANES), lambda *g: (k_of(*g), g[0]))
    return wq, q_spec, k_spec, v_spec


def _attn_fwd(q, k, v, shared_k, name):
    t = q.shape[0]
    tq, tk = _tile(t, ATTN_TQ), _tile(t, ATTN_TK)
    nq, nk = t // tq, t // tk
    wq, q_spec, k_spec, v_spec = _attn_specs(shared_k, tq, tk, lambda p, i, j: i, lambda p, i, j: j)
    groups = q.shape[1] // wq
    chunk = _tile(tq, 2 * LANES)

    def body(q_ref, k_ref, v_ref, o_ref, lse_ref, m_s, acc_s, alpha_s, s_s, p_s):
        kb = pl.program_id(2)

        @pl.when(kb == 0)
        def _():
            m_s[...] = jnp.full_like(m_s, -jnp.inf)
            acc_s[...] = jnp.zeros_like(acc_s)

        qv, kv, vv = q_ref[...], k_ref[...], v_ref[...]
        lo = _lo_mask(vv.shape)
        for i in range(2):
            qi, ki = _head_operands(qv, kv, i, shared_k)
            s_s[i] = lax.dot_general(ki, qi, _NT, preferred_element_type=F32)
        for i in range(2):
            for c in range(tq // chunk):
                cols = slice(c * chunk, (c + 1) * chunk)
                m_prev = m_s[i, :, cols]
                m_new = jnp.maximum(m_prev, jnp.max(s_s[i, :, cols], axis=0, keepdims=True))
                alpha_s[i, :, cols] = jnp.exp2(m_prev - m_new)
                m_s[i, :, cols] = m_new
                p_s[i, :, cols] = jnp.exp2(s_s[i, :, cols] - m_new).astype(BF16)
        for i in range(2):
            keep = lo if i == 0 else jnp.logical_not(lo)
            vi = jnp.where(keep, vv, jnp.ones_like(vv))
            acc_s[i] = alpha_s[i] * acc_s[i] + lax.dot_general(vi, p_s[i], _TN, preferred_element_type=F32)

        @pl.when(kb == nk - 1)
        def _():
            a0, a1 = acc_s[0], acc_s[1]
            l0 = a0[LANES - SUBLANES:, :][0:1, :]
            l1 = a1[0:SUBLANES, :][0:1, :]
            row_lo = lax.broadcasted_iota(jnp.int32, a0.shape, 0) < HEAD_DIM
            o_ref[...] = jnp.where(row_lo, a0 / l0, a1 / l1).T.astype(BF16)
            lse_ref[0] = jnp.broadcast_to(m_s[0] + jnp.log2(l0), (LANES, tq)).T
            lse_ref[1] = jnp.broadcast_to(m_s[1] + jnp.log2(l1), (LANES, tq)).T

    return pl.pallas_call(
        body, name=name, grid=(groups, nq, nk),
        in_specs=[q_spec, k_spec, v_spec],
        out_specs=[pl.BlockSpec((tq, LANES), lambda p, i, j: (i, p)),
                   pl.BlockSpec((2, tq, LANES), lambda p, i, j: (p, i, 0))],
        out_shape=[jax.ShapeDtypeStruct((t, LANES * groups), BF16),
                   jax.ShapeDtypeStruct((2 * groups, t, LANES), F32)],
        scratch_shapes=[pltpu.VMEM((2, 1, tq), F32), pltpu.VMEM((2, LANES, tq), F32), pltpu.VMEM((2, 1, tq), F32),
                        pltpu.VMEM((2, tk, tq), F32), pltpu.VMEM((2, tk, tq), BF16)],
        compiler_params=_params("parallel", "parallel", "arbitrary"),
    )(q, k, v)


def _attn_bwd(q, k, v, do, lse, delta, shared_k, name):
    t = q.shape[0]
    tq, tk = _tile(t, ATTN_TQ), _tile(t, ATTN_TK)
    nq, nk = t // tq, t // tk
    wq, q_spec, k_spec, v_spec = _attn_specs(shared_k, tq, tk, lambda p, j, i: i, lambda p, j, i: j)
    groups = q.shape[1] // wq

    def body(q_ref, k_ref, v_ref, do_ref, lse_ref, delta_ref, dq_ref, dk_ref, dv_ref, dk_s, dv_s, s_s, dp_s, p_s,
             ds_s):
        kb, qb = pl.program_id(1), pl.program_id(2)

        @pl.when(qb == 0)
        def _():
            dk_s[...] = jnp.zeros_like(dk_s)
            dv_s[...] = jnp.zeros_like(dv_s)

        qv, kv, vv, dov = q_ref[...], k_ref[...], v_ref[...], do_ref[...]
        lo = _lo_mask(dov.shape)
        heads = []
        for i in range(2):
            qi, ki = _head_operands(qv, kv, i, shared_k)
            keep = lo if i == 0 else jnp.logical_not(lo)
            doi = jnp.where(keep, dov, jnp.zeros_like(dov))
            heads.append((qi, ki, doi))
            s_s[i] = lax.dot_general(qi, ki, _NT, preferred_element_type=F32)
            dp_s[i] = lax.dot_general(doi, vv, _NT, preferred_element_type=F32)
        for i in range(2):
            lse_i, delta_i = lse_ref[i], delta_ref[i]
            for c in range(tk // LANES):
                cols = slice(c * LANES, (c + 1) * LANES)
                p = jnp.exp2(s_s[i, :, cols] - lse_i)
                p_s[i, :, cols] = p.astype(BF16)
                ds_s[i, :, cols] = (p * (dp_s[i, :, cols] - delta_i)).astype(BF16)
        dq_parts = []
        for i in range(2):
            qi, ki, doi = heads[i]
            dv_s[...] += lax.dot_general(doi, p_s[i], _TN, preferred_element_type=F32)
            dk_i = lax.dot_general(qi, ds_s[i], _TN, preferred_element_type=F32)
            if shared_k:
                dk_s[...] += dk_i
            else:
                dk_s[LANES * i:LANES * (i + 1), :] += dk_i
            dq_parts.append(lax.dot_general(ds_s[i], ki, _NN, preferred_element_type=F32))
        rows = pl.ds(pl.multiple_of(qb * tq, tq), tq)
        if shared_k:
            tiles = [(slice(0, LANES), jnp.where(lo, dq_parts[0], dq_parts[1]))]
        else:
            tiles = [(slice(0, LANES), dq_parts[0]), (slice(LANES, 2 * LANES), dq_parts[1])]
        for cols, val in tiles:
            @pl.when(kb == 0)
            def _(cols=cols, val=val):
                dq_ref[rows, cols] = val

            @pl.when(kb > 0)
            def _(cols=cols, val=val):
                dq_ref[rows, cols] += val

        @pl.when(qb == nq - 1)
        def _():
            if shared_k:
                dk_ref[0] = dk_s[...].astype(BF16)
                dv_ref[0] = dv_s[...].astype(BF16)
            else:
                dk_ref[...] = dk_s[...].astype(BF16)
                dv_ref[...] = dv_s[...].astype(BF16)

    stat_spec = pl.BlockSpec((2, tq, LANES), lambda p, j, i: (p, i, 0))
    do_spec = pl.BlockSpec((tq, LANES), lambda p, j, i: (i, p))
    dq_spec = pl.BlockSpec((t, wq), lambda p, j, i: (0, p))
    if shared_k:
        dk_spec = pl.BlockSpec((1, LANES, tk), lambda p, j, i: (p, 0, j))
        dv_spec = dk_spec
        dk_shape = jax.ShapeDtypeStruct((groups, LANES, t), BF16)
        dv_shape = dk_shape
    else:
        dk_spec = pl.BlockSpec((wq, tk), lambda p, j, i: (p, j))
        dv_spec = pl.BlockSpec((LANES, tk), lambda p, j, i: (p, j))
        dk_shape = jax.ShapeDtypeStruct((wq * groups, t), BF16)
        dv_shape = jax.ShapeDtypeStruct((LANES * groups, t), BF16)
    return pl.pallas_call(
        body, name=name, grid=(groups, nk, nq),
        in_specs=[q_spec, k_spec, v_spec, do_spec, stat_spec, stat_spec],
        out_specs=[dq_spec, dk_spec, dv_spec],
        out_shape=[jax.ShapeDtypeStruct((t, wq * groups), F32), dk_shape, dv_shape],
        scratch_shapes=[pltpu.VMEM((wq, tk), F32), pltpu.VMEM((LANES, tk), F32), pltpu.VMEM((2, tq, tk), F32),
                        pltpu.VMEM((2, tq, tk), F32), pltpu.VMEM((2, tq, tk), BF16), pltpu.VMEM((2, tq, tk), BF16)],
        compiler_params=_params("parallel", "arbitrary", "arbitrary"),
    )(q, k, v, do, lse, delta)


_MERGE_W = 512
_GATE_BLK0 = Z_GATE // _MERGE_W


def _merge_fwd(z, b_gate, ta, tb):
    t = z.shape[0]
    tm = _tile(t, ROW_TILE)
    w = _MERGE_W
    nj = D_MODEL // w

    def body(za_ref, zb_ref, ba_ref, bb_ref, ta_ref, tb_ref, o_ref):
        ga = jax.nn.sigmoid(za_ref[...].astype(F32) + ba_ref[...])
        gb = jax.nn.sigmoid(zb_ref[...].astype(F32) + bb_ref[...])
        o_ref[...] = (ga * ta_ref[...].astype(F32) + gb * tb_ref[...].astype(F32)).astype(BF16)

    return pl.pallas_call(
        body, name="merge_fwd", grid=(t // tm, nj),
        in_specs=[pl.BlockSpec((tm, w), lambda i, j: (i, _GATE_BLK0 + j)),
                  pl.BlockSpec((tm, w), lambda i, j: (i, _GATE_BLK0 + nj + j)),
                  pl.BlockSpec((1, w), lambda i, j: (0, j)),
                  pl.BlockSpec((1, w), lambda i, j: (0, nj + j)),
                  pl.BlockSpec((tm, w), lambda i, j: (i, j)),
                  pl.BlockSpec((tm, w), lambda i, j: (i, j))],
        out_specs=pl.BlockSpec((tm, w), lambda i, j: (i, j)),
        out_shape=jax.ShapeDtypeStruct((t, D_MODEL), BF16),
        compiler_params=_params("parallel", "parallel"),
    )(z, z, b_gate, b_gate, ta, tb)


def _merge_bwd(dmg, z, b_gate, ta, tb):
    t = z.shape[0]
    tm = _tile(t, ROW_TILE)
    w = _MERGE_W
    nj = D_MODEL // w

    def body(dm_ref, za_ref, zb_ref, ba_ref, bb_ref, ta_ref, tb_ref, dta_ref, dtb_ref, dza_ref, dzb_ref,
             dba_ref, dbb_ref):
        dm = dm_ref[...].astype(F32)
        ga = jax.nn.sigmoid(za_ref[...].astype(F32) + ba_ref[...])
        gb = jax.nn.sigmoid(zb_ref[...].astype(F32) + bb_ref[...])
        dta_ref[...] = (dm * ga).astype(BF16)
        dtb_ref[...] = (dm * gb).astype(BF16)
        dza = dm * ta_ref[...].astype(F32) * ga * (1.0 - ga)
        dzb = dm * tb_ref[...].astype(F32) * gb * (1.0 - gb)
        dza_ref[...] = dza.astype(BF16)
        dzb_ref[...] = dzb.astype(BF16)

        @pl.when(pl.program_id(1) == 0)
        def _():
            dba_ref[...] = jnp.zeros_like(dba_ref)
            dbb_ref[...] = jnp.zeros_like(dbb_ref)

        dba_ref[...] += _fold8(dza)
        dbb_ref[...] += _fold8(dzb)

    blk = pl.BlockSpec((tm, w), lambda j, i: (i, j))
    acc = pl.BlockSpec((SUBLANES, w), lambda j, i: (0, j))
    return pl.pallas_call(
        body, name="merge_bwd", grid=(nj, t // tm),
        in_specs=[blk,
                  pl.BlockSpec((tm, w), lambda j, i: (i, _GATE_BLK0 + j)),
                  pl.BlockSpec((tm, w), lambda j, i: (i, _GATE_BLK0 + nj + j)),
                  pl.BlockSpec((1, w), lambda j, i: (0, j)),
                  pl.BlockSpec((1, w), lambda j, i: (0, nj + j)),
                  blk, blk],
        out_specs=[blk, blk, blk, blk, acc, acc],
        out_shape=[jax.ShapeDtypeStruct((t, D_MODEL), BF16)] * 4 + [jax.ShapeDtypeStruct((SUBLANES, D_MODEL), F32)] * 2,
        compiler_params=_params("parallel", "arbitrary"),
    )(dmg, z, z, b_gate, b_gate, ta, tb)


def _loss_grad(y, target):
    t, d = y.shape
    tm = _tile(t, ROW_TILE)

    def body(y_ref, t_ref, dy_ref, acc_ref):
        err = y_ref[...] - t_ref[...]
        dy_ref[...] = err * (1.0 / d)
        e8 = _fold8(err * err)
        part = e8[:, 0:LANES]
        for c in range(1, d // LANES):
            part = part + e8[:, LANES * c:LANES * (c + 1)]

        @pl.when(pl.program_id(0) == 0)
        def _():
            acc_ref[...] = jnp.zeros_like(acc_ref)

        acc_ref[...] += part

    return pl.pallas_call(
        body, name="loss_grad", grid=(t // tm,),
        in_specs=[_row_spec(tm, d), _row_spec(tm, d)],
        out_specs=[_row_spec(tm, d), _acc_spec(LANES)],
        out_shape=[jax.ShapeDtypeStruct((t, d), F32), jax.ShapeDtypeStruct((SUBLANES, LANES), F32)],
        compiler_params=_params("arbitrary"),
    )(y, target)


_MESH_ID = pl.DeviceIdType.MESH
_ANY = pl.BlockSpec(memory_space=pl.ANY)


def _all_gather(arrays):
    n = len(arrays)
    halves = []
    for a in arrays:
        assert a.shape[0] % 2 == 0, a.shape
        halves.append((pl.ds(0, a.shape[0] // 2), pl.ds(a.shape[0] // 2, a.shape[0] // 2)))
    OWN_SIB, OWN_X, OWN_Y, FWD_X, FWD_Y, SIB_X, SIB_Y, SIB_DA, SIB_DB = range(9)

    def body(*refs):
        x_refs, out_refs = refs[:n], refs[n:2 * n]
        send_sems, recv_sems, local_sems = refs[2 * n:]
        mx, my, mc = lax.axis_index("x"), lax.axis_index("y"), lax.axis_index("c")
        me, sibling = (mx, my, mc), (mx, my, 1 - mc)
        x_nbr, y_nbr, diag = (1 - mx, my, mc), (mx, 1 - my, mc), (1 - mx, 1 - my, mc)

        def slot(a, dev, rows=None):
            px, py, pc = dev
            ref = out_refs[a].at[4 * px + 2 * py + pc]
            return ref if rows is None else ref.at[rows]

        def other_core(dev):
            return (dev[0], dev[1], 1 - dev[2])

        def copy(a, sem, block, to, rows=None, src=None):
            return pltpu.make_async_remote_copy(
                src_ref=slot(a, block, rows) if src is None else src, dst_ref=slot(a, block, rows),
                send_sem=send_sems.at[a, sem], recv_sem=recv_sems.at[a, sem], device_id=to, device_id_type=_MESH_ID)

        mine = [pltpu.make_async_copy(x_refs[a], slot(a, me), local_sems.at[a]) for a in range(n)]
        sent = []
        for a in range(n):
            mine[a].start()
            sent += [copy(a, OWN_SIB, me, sibling, src=x_refs[a]), copy(a, OWN_X, me, x_nbr, src=x_refs[a]),
                     copy(a, OWN_Y, me, y_nbr, src=x_refs[a])]
        for cp in sent:
            cp.start()
        for a in range(n):
            first, second = halves[a]
            copy(a, OWN_Y, y_nbr, me).wait_recv()
            sent += [copy(a, FWD_X, y_nbr, x_nbr, rows=first), copy(a, SIB_Y, y_nbr, sibling)]
            sent[-2].start()
            sent[-1].start()
            copy(a, OWN_X, x_nbr, me).wait_recv()
            sent += [copy(a, FWD_Y, x_nbr, y_nbr, rows=second), copy(a, SIB_X, x_nbr, sibling)]
            sent[-2].start()
            sent[-1].start()
        for a in range(n):
            first, second = halves[a]
            copy(a, FWD_X, diag, me, rows=first).wait_recv()
            sent.append(copy(a, SIB_DA, diag, sibling, rows=first))
            sent[-1].start()
            copy(a, FWD_Y, diag, me, rows=second).wait_recv()
            sent.append(copy(a, SIB_DB, diag, sibling, rows=second))
            sent[-1].start()
        for a in range(n):
            first, second = halves[a]
            copy(a, OWN_SIB, sibling, me).wait_recv()
            copy(a, SIB_X, other_core(x_nbr), me).wait_recv()
            copy(a, SIB_Y, other_core(y_nbr), me).wait_recv()
            copy(a, SIB_DA, other_core(diag), me, rows=first).wait_recv()
            copy(a, SIB_DB, other_core(diag), me, rows=second).wait_recv()
        for cp in sent:
            cp.wait_send()
        for cp in mine:
            cp.wait()

    return pl.pallas_call(
        body, name="weight_all_gather",
        out_shape=[jax.ShapeDtypeStruct((N_DEV,) + a.shape, a.dtype) for a in arrays],
        in_specs=[_ANY] * n, out_specs=[_ANY] * n,
        scratch_shapes=[pltpu.SemaphoreType.DMA((n, 9)), pltpu.SemaphoreType.DMA((n, 9)),
                        pltpu.SemaphoreType.DMA((n,))],
    )(*arrays)


def _pair_exchange(sends):
    n = len(sends)

    def body(*refs):
        s_refs, r_refs = refs[:n], refs[n:2 * n]
        send_sems, recv_sems = refs[2 * n:]
        mx, my, mc = lax.axis_index("x"), lax.axis_index("y"), lax.axis_index("c")
        copies = []
        for a in range(n):
            for ch in range(4):
                cp = pltpu.make_async_remote_copy(
                    src_ref=s_refs[a].at[2 * ch + (1 - mc)], dst_ref=r_refs[a].at[ch], send_sem=send_sems.at[a, ch],
                    recv_sem=recv_sems.at[a, ch], device_id=(mx, my, 1 - mc), device_id_type=_MESH_ID)
                cp.start()
                copies.append(cp)
        for cp in copies:
            cp.wait_send()
            cp.wait_recv()

    return pl.pallas_call(
        body, name="grad_pair_exchange",
        out_shape=[jax.ShapeDtypeStruct((4,) + s.shape[1:], s.dtype) for s in sends],
        in_specs=[_ANY] * n, out_specs=[_ANY] * n,
        scratch_shapes=[pltpu.SemaphoreType.DMA((n, 4)), pltpu.SemaphoreType.DMA((n, 4))],
    )(*sends)


def _pair_add(send, half, core):
    _, r, c_ = send.shape
    tr = _row_tile(r, c_)

    def body(core_ref, s_ref, h_ref, o_ref):
        del core_ref
        o_ref[...] = (s_ref[...].astype(F32) + h_ref[...].astype(F32)).astype(BF16)

    blk = pl.BlockSpec((1, tr, c_), lambda ch, i, core_ref: (ch, i, 0))
    return pl.pallas_call(
        body, name="grad_pair_add",
        grid_spec=pltpu.PrefetchScalarGridSpec(
            num_scalar_prefetch=1, grid=(4, r // tr),
            in_specs=[pl.BlockSpec((1, tr, c_), lambda ch, i, core_ref: (2 * ch + core_ref[0], i, 0)), blk],
            out_specs=blk),
        out_shape=jax.ShapeDtypeStruct((4, r, c_), BF16),
        compiler_params=_params("parallel", "parallel"),
    )(core, send, half)


def _chip_exchange(parts):
    n = len(parts)

    def body(*refs):
        p_refs, r_refs = refs[:n], refs[n:2 * n]
        send_sems, recv_sems, local_sems = refs[2 * n:]
        mx, my, mc = lax.axis_index("x"), lax.axis_index("y"), lax.axis_index("c")
        mine = 2 * mx + my
        local = [pltpu.make_async_copy(p_refs[a].at[mine], r_refs[a].at[mine], local_sems.at[a]) for a in range(n)]
        copies = []
        for a in range(n):
            local[a].start()
            for rel in range(1, 4):
                px = 1 - mx if rel & 2 else mx
                py = 1 - my if rel & 1 else my
                cp = pltpu.make_async_remote_copy(
                    src_ref=p_refs[a].at[2 * px + py], dst_ref=r_refs[a].at[mine], send_sem=send_sems.at[a, rel - 1],
                    recv_sem=recv_sems.at[a, rel - 1], device_id=(px, py, mc), device_id_type=_MESH_ID)
                cp.start()
                copies.append(cp)
        for cp in copies:
            cp.wait_send()
            cp.wait_recv()
        for cp in local:
            cp.wait()

    return pl.pallas_call(
        body, name="grad_chip_exchange",
        out_shape=[jax.ShapeDtypeStruct(p.shape, p.dtype) for p in parts],
        in_specs=[_ANY] * n, out_specs=[_ANY] * n,
        scratch_shapes=[pltpu.SemaphoreType.DMA((n, 3)), pltpu.SemaphoreType.DMA((n, 3)),
                        pltpu.SemaphoreType.DMA((n,))],
    )(*parts)


def _row_tile(r, c_):
    tr = min(r, ADAM_BLOCK_ELEMS // (pl.cdiv(c_, LANES) * LANES))
    while r % tr:
        tr -= SUBLANES
    return tr


def _adamw(recv, w, m, v):
    r, c_ = w.shape
    tr = _row_tile(r, c_)
    n_src = recv.shape[0]

    def body(g_ref, w_ref, m_ref, v_ref, go_ref, d_ref, mo_ref, vo_ref):
        g = g_ref[0].astype(F32)
        for s in range(1, n_src):
            g = g + g_ref[s].astype(F32)
        go_ref[...] = g
        mn = ADAM_B1 * m_ref[...] + (1.0 - ADAM_B1) * g
        vn = ADAM_B2 * v_ref[...] + (1.0 - ADAM_B2) * (g * g)
        mo_ref[...] = mn
        vo_ref[...] = vn
        m_hat = mn / (1.0 - ADAM_B1 ** ADAM_STEP)
        v_hat = vn / (1.0 - ADAM_B2 ** ADAM_STEP)
        d_ref[...] = -ADAM_LR * (m_hat / (jnp.sqrt(v_hat) + ADAM_EPS) + ADAM_WD * w_ref[...])

    spec = pl.BlockSpec((tr, c_), lambda i: (i, 0))
    out = jax.ShapeDtypeStruct((r, c_), F32)
    return pl.pallas_call(
        body, name="grad_sum_adamw", grid=(r // tr,),
        in_specs=[pl.BlockSpec((n_src, tr, c_), lambda i: (0, i, 0)), spec, spec, spec],
        out_specs=[spec, spec, spec, spec], out_shape=[out, out, out, out],
        compiler_params=_params("parallel"),
    )(recv, w, m, v)


def _pad_cols(a, before, after):
    parts = []
    if before:
        parts.append(jnp.zeros(a.shape[:-1] + (before,), a.dtype))
    parts.append(a)
    if after:
        parts.append(jnp.zeros(a.shape[:-1] + (after,), a.dtype))
    return jnp.concatenate(parts, axis=-1)


def _q_head_pairs(a, axis):
    shp = a.shape
    a = a.reshape(shp[:axis] + (GQA_KV_HEADS, GQA_GROUP, HEAD_DIM) + shp[axis + 1:])
    a = jnp.swapaxes(a, axis, axis + 1)
    return a.reshape(shp)


def _q_head_unpairs(a, axis):
    shp = a.shape
    a = a.reshape(shp[:axis] + (GQA_GROUP, GQA_KV_HEADS, HEAD_DIM) + shp[axis + 1:])
    a = jnp.swapaxes(a, axis, axis + 1)
    return a.reshape(shp)


def _layout_weights(w):
    w_in = w["w_in"]
    lead = w_in.shape[:-1]
    w_in_p = jnp.concatenate([
        _q_head_pairs(w_in[..., 0:512], w_in.ndim - 1),
        w_in[..., 512:1408],
        _pad_cols(w_in[..., 1408:1440], KR_LANE0, LANES - KR_LANE0 - MLA_ROPE_DIM),
        w_in[..., 1440:],
    ], axis=-1)
    wq = w["w_q_up"]
    wq_p = _pad_cols(wq.reshape(wq.shape[:-1] + (MLA_HEADS, MLA_QK_DIM)), 0, LANES - MLA_QK_DIM)
    wq_p = wq_p.reshape(wq.shape[:-1] + (MLA_HEADS * LANES,))
    wkv = w["w_kv_up"]
    wkv4 = wkv.reshape(wkv.shape[:-1] + (MLA_HEADS, 2 * HEAD_DIM))
    wk_p = _pad_cols(wkv4[..., :HEAD_DIM], 0, LANES - HEAD_DIM).reshape(wkv.shape[:-1] + (MLA_HEADS * LANES,))
    wv_p = wkv4[..., HEAD_DIM:].reshape(wkv.shape[:-1] + (MLA_HEADS * HEAD_DIM,))
    del lead
    return {
        "w_in": w_in_p, "w_q_up": wq_p, "w_kv_up": jnp.concatenate([wk_p, wv_p], axis=-1),
        "w_branch_a": _q_head_pairs(w["w_branch_a"], w["w_branch_a"].ndim - 2), "w_branch_b": w["w_branch_b"],
        "w_o": w["w_o"], "w_ffn_up": w["w_ffn_up"], "w_ffn_down": w["w_ffn_down"],
    }


def _unlayout_grads(g):
    gi = g["w_in"]
    kr0 = Z_KR + KR_LANE0
    g_in = jnp.concatenate([
        _q_head_unpairs(gi[..., 0:512], gi.ndim - 1), gi[..., 512:1408], gi[..., kr0:kr0 + MLA_ROPE_DIM],
        gi[..., Z_GATE:],
    ], axis=-1)
    gq = g["w_q_up"]
    gq = gq.reshape(gq.shape[:-1] + (MLA_HEADS, LANES))[..., :MLA_QK_DIM]
    gq = gq.reshape(gq.shape[:-2] + (MLA_HEADS * MLA_QK_DIM,))
    gkv = g["w_kv_up"]
    gk = gkv[..., :MLA_HEADS * LANES].reshape(gkv.shape[:-1] + (MLA_HEADS, LANES))[..., :HEAD_DIM]
    gv = gkv[..., MLA_HEADS * LANES:].reshape(gkv.shape[:-1] + (MLA_HEADS, HEAD_DIM))
    gkv = jnp.concatenate([gk, gv], axis=-1).reshape(gkv.shape[:-1] + (MLA_HEADS * 2 * HEAD_DIM,))
    return {
        "w_in": g_in, "w_q_up": gq, "w_kv_up": gkv,
        "w_branch_a": _q_head_unpairs(g["w_branch_a"], g["w_branch_a"].ndim - 2), "w_branch_b": g["w_branch_b"],
        "w_o": g["w_o"], "w_ffn_up": g["w_ffn_up"], "w_ffn_down": g["w_ffn_down"],
    }


def _pack_small(parts):
    flat = jnp.concatenate([p.reshape(-1) for p in parts])
    pad = (-flat.shape[0]) % (SUBLANES * LANES)
    if pad:
        flat = jnp.concatenate([flat, jnp.zeros((pad,), flat.dtype)])
    return flat.reshape(-1, LANES)


def _unpack_small(packed, shapes):
    flat = packed.reshape(-1)
    out, off = [], 0
    for shp in shapes:
        n = int(np.prod(shp))
        out.append(flat[off:off + n].reshape(shp))
        off += n
    return out


def _shards_of(full, axis):
    shp = full.shape
    cut = shp[:axis] + (N_DEV, shp[axis] // N_DEV) + shp[axis + 1:]
    return jnp.moveaxis(full.reshape(cut), axis, 0)


def _from_shards(shards, axis):
    full = list(shards.shape[1:])
    full[axis] *= N_DEV
    return jnp.moveaxis(shards, 0, axis).reshape(full)


def _rows2d(a):
    return a.reshape(-1, a.shape[-1])


def _layer_fwd(x, u, lw, tabs):
    cos_a, sin_a, cos_b, sin_b = tabs
    z = _matmul(u, lw["w_in"], "nn", "mm_in")
    qa, ka, va, cqn, ckvn, krr = _prep_a_fwd(z, lw["gq2"], lw["gk2"], lw["gqa"], lw["gkva"], cos_a, sin_a, cos_b, sin_b)
    qb = _matmul(cqn, lw["w_q_up"], "nn", "mm_q_up")
    kvb = _matmul(ckvn, lw["w_kv_up"], "nn", "mm_kv_up")
    q_b, k_b, v_b = _prep_b_fwd(qb, kvb, krr, cos_b, sin_b)
    ya, lse_a = _attn_fwd(qa, ka, va, True, "gqa_fwd")
    yb, lse_b = _attn_fwd(q_b, k_b, v_b, False, "mla_fwd")
    ta = _matmul(ya, lw["w_branch_a"], "nn", "mm_branch_a")
    tb = _matmul(yb, lw["w_branch_b"], "nn", "mm_branch_b")
    merged = _merge_fwd(z, lw["b_gate"], ta, tb)
    m = _matmul(merged, lw["w_o"], "nn", "mm_o")
    x2, u2 = _res_norm_fwd(x, m, lw["post_mix_g"], lw["pre_ffn_g"])
    h, a = _matmul(u2, lw["w_ffn_up"], "nn", "mm_ffn_up", post="relu2")
    f = _matmul(a, lw["w_ffn_down"], "nn", "mm_ffn_down")
    x3, u_next = _res_norm_fwd(x2, f, lw["post_ffn_g"], lw["next_pre_mix_g"])
    saved = dict(u=u, z=z, qa=qa, ka=ka, va=va, cqn=cqn, ckvn=ckvn, q_b=q_b, k_b=k_b, v_b=v_b, ya=ya, yb=yb,
                 lse_a=lse_a, lse_b=lse_b, ta=ta, tb=tb, merged=merged, m=m, x2=x2, u2=u2, h=h, a=a, f=f, x3=x3)
    return x3, u_next, saved


def _layer_bwd(dx3, du_next, lw, sv, tabs, gbuf, layer):
    cos_a, sin_a, cos_b, sin_b = tabs
    g = {}
    dx3, df, dg4, dg1n = _res_norm_bwd(sv["x3"], sv["f"], lw["post_ffn_g"], lw["next_pre_mix_g"], dx3, du_next)
    g["post_ffn_g"], g["next_pre_mix_g"] = dg4, dg1n
    dh = _matmul(df, lw["w_ffn_down"], "nt", "mm_d_h", post="relu2_bwd", h=sv["h"])
    g["w_ffn_down"] = _matmul(sv["a"], df, "tn", "mm_dw_ffn_down", stack=(gbuf["w_ffn_down"], layer))
    du2 = _matmul(dh, lw["w_ffn_up"], "nt", "mm_d_u2")
    g["w_ffn_up"] = _matmul(sv["u2"], dh, "tn", "mm_dw_ffn_up", stack=(gbuf["w_ffn_up"], layer))
    dx2, dm, dg2, dg3 = _res_norm_bwd(sv["x2"], sv["m"], lw["post_mix_g"], lw["pre_ffn_g"], dx3, du2)
    g["post_mix_g"], g["pre_ffn_g"] = dg2, dg3
    dmg = _matmul(dm, lw["w_o"], "nt", "mm_d_merged")
    g["w_o"] = _matmul(sv["merged"], dm, "tn", "mm_dw_o", stack=(gbuf["w_o"], layer))
    dta, dtb, dzg_a, dzg_b, db_a, db_b = _merge_bwd(dmg, sv["z"], lw["b_gate"], sv["ta"], sv["tb"])
    g["b_gate"] = jnp.concatenate([db_a, db_b], axis=-1)
    dya, delta_a = _matmul(dta, lw["w_branch_a"], "nt", "mm_d_ya", post="delta", h=sv["ya"])
    g["w_branch_a"] = _matmul(sv["ya"], dta, "tn", "mm_dw_branch_a", stack=(gbuf["w_branch_a"], layer))
    dyb, delta_b = _matmul(dtb, lw["w_branch_b"], "nt", "mm_d_yb", post="delta", h=sv["yb"])
    g["w_branch_b"] = _matmul(sv["yb"], dtb, "tn", "mm_dw_branch_b", stack=(gbuf["w_branch_b"], layer))
    dqa, dka4, dva4 = _attn_bwd(sv["qa"], sv["ka"], sv["va"], dya, sv["lse_a"], delta_a, True, "gqa_bwd")
    dq_b, dk_b, dv_b = _attn_bwd(sv["q_b"], sv["k_b"], sv["v_b"], dyb, sv["lse_b"], delta_b, False, "mla_bwd")
    dqb, dkvb, dkr = _prep_b_bwd(dq_b, dk_b, dv_b, cos_b, sin_b)
    dcqn = _matmul(dqb, lw["w_q_up"], "nt", "mm_d_cqn")
    g["w_q_up"] = _matmul(sv["cqn"], dqb, "tn", "mm_dw_q_up", stack=(gbuf["w_q_up"], layer))
    dckvn = _matmul(dkvb, lw["w_kv_up"], "nt", "mm_d_ckvn")
    g["w_kv_up"] = _matmul(sv["ckvn"], dkvb, "tn", "mm_dw_kv_up", stack=(gbuf["w_kv_up"], layer))
    dz, dgq, dgk, dgqa, dgkva = _prep_a_bwd(sv["z"], dqa, dka4, dva4, dcqn, dckvn, dkr, dzg_a, dzg_b, lw["gq2"],
                                            lw["gk2"], lw["gqa"], lw["gkva"], cos_a, sin_a)
    g["q_norm_g"], g["k_norm_g"], g["q_a_norm_g"], g["kv_a_norm_g"] = dgq, dgk, dgqa, dgkva
    du = _matmul(dz, lw["w_in"], "nt", "mm_d_u")
    g["w_in"] = _matmul(sv["u"], dz, "tn", "mm_dw_in", stack=(gbuf["w_in"], layer))
    return dx2, du, g


def kernel(x, w_in, b_gate, q_norm_g, k_norm_g, q_a_norm_g, kv_a_norm_g, w_q_up, w_kv_up, w_branch_a, w_branch_b, w_o, w_ffn_up, w_ffn_down, pre_mix_g, post_mix_g, pre_ffn_g, post_ffn_g, loss_target, m_w_in, m_b_gate, m_q_norm_g, m_k_norm_g, m_q_a_norm_g, m_kv_a_norm_g, m_w_q_up, m_w_kv_up, m_w_branch_a, m_w_branch_b, m_w_o, m_w_ffn_up, m_w_ffn_down, m_pre_mix_g, m_post_mix_g, m_pre_ffn_g, m_post_ffn_g, v_w_in, v_b_gate, v_q_norm_g, v_k_norm_g, v_q_a_norm_g, v_kv_a_norm_g, v_w_q_up, v_w_kv_up, v_w_branch_a, v_w_branch_b, v_w_o, v_w_ffn_up, v_w_ffn_down, v_pre_mix_g, v_post_mix_g, v_pre_ffn_g, v_post_ffn_g):
    weights = dict(zip(WEIGHT_NAMES, (w_in, b_gate, q_norm_g, k_norm_g, q_a_norm_g, kv_a_norm_g, w_q_up, w_kv_up,
                                      w_branch_a, w_branch_b, w_o, w_ffn_up, w_ffn_down, pre_mix_g, post_mix_g,
                                      pre_ffn_g, post_ffn_g)))
    mom_m = dict(zip(WEIGHT_NAMES, (m_w_in, m_b_gate, m_q_norm_g, m_k_norm_g, m_q_a_norm_g, m_kv_a_norm_g, m_w_q_up,
                                    m_w_kv_up, m_w_branch_a, m_w_branch_b, m_w_o, m_w_ffn_up, m_w_ffn_down,
                                    m_pre_mix_g, m_post_mix_g, m_pre_ffn_g, m_post_ffn_g)))
    mom_v = dict(zip(WEIGHT_NAMES, (v_w_in, v_b_gate, v_q_norm_g, v_k_norm_g, v_q_a_norm_g, v_kv_a_norm_g, v_w_q_up,
                                    v_w_kv_up, v_w_branch_a, v_w_branch_b, v_w_o, v_w_ffn_up, v_w_ffn_down,
                                    v_pre_mix_g, v_post_mix_g, v_pre_ffn_g, v_post_ffn_g)))
    assert x.shape[0] == 1 and x.shape[2] == D_MODEL, x.shape
    n_layers = w_in.shape[0]
    t = x.shape[1]
    x0 = x.reshape(t, D_MODEL)
    target = loss_target.reshape(t, D_MODEL)
    shard_shapes = {n: weights[n].shape for n in BIG_NAMES}
    small_shapes = [weights[n].shape for n in SMALL_NAMES]

    gathered = _all_gather([weights[n].astype(BF16) for n in BIG_NAMES])
    full = {n: _from_shards(g, SHARD_AXIS[n]) for n, g in zip(BIG_NAMES, gathered)}
    lw_all = _layout_weights(full)
    lw_all["b_gate"] = b_gate.reshape(n_layers, 1, 2 * D_MODEL)
    lw_all["gq2"] = jnp.tile(q_norm_g, (1, 2)).reshape(n_layers, 1, LANES)
    lw_all["gk2"] = jnp.tile(k_norm_g, (1, 2)).reshape(n_layers, 1, LANES)
    lw_all["gqa"] = q_a_norm_g.reshape(n_layers, 1, MLA_Q_RANK)
    lw_all["gkva"] = kv_a_norm_g.reshape(n_layers, 1, MLA_KV_RANK)
    for n in ("post_mix_g", "pre_ffn_g", "post_ffn_g"):
        lw_all[n] = weights[n]
    lw_all["next_pre_mix_g"] = jnp.roll(pre_mix_g, -1, axis=0)

    tabs = _rope_tables(t)
    u0 = _rms_fwd(x0, pre_mix_g[0])

    layer_w = [{n: a[l] for n, a in lw_all.items()} for l in range(n_layers)]
    xc, uc, saved = x0, u0, []
    for l in range(n_layers):
        xc, uc, sv = _layer_fwd(xc, uc, layer_w[l], tabs)
        saved.append(sv)
    dy, loss_acc = _loss_grad(xc, target)
    loss = lax.psum(0.5 * jnp.sum(loss_acc) / D_MODEL, ("x", "y", "c"))

    dx0, du0, layer_g = dy, jnp.zeros((t, D_MODEL), F32), [None] * n_layers
    gbuf = {n: lax.empty(lw_all[n].shape, BF16) for n in BIG_NAMES}
    for l in reversed(range(n_layers)):
        dx0, du0, layer_g[l] = _layer_bwd(dx0, du0, layer_w[l], saved[l], tabs, gbuf, l)
        gbuf = {n: layer_g[l][n] for n in BIG_NAMES}
    grads = {n: jnp.stack([g[n] for g in layer_g]) for n in layer_g[0] if n not in BIG_NAMES}
    grads.update(gbuf)
    grad_x, dg1_first = _rms_bwd(x0, pre_mix_g[0], dx0, du0)

    big_grads = _unlayout_grads({n: grads[n] for n in BIG_NAMES})
    fold = lambda a: a.sum(axis=1)
    dgq = fold(grads["q_norm_g"]).reshape(n_layers, 2, HEAD_DIM).sum(axis=1)
    dgk = fold(grads["k_norm_g"]).reshape(n_layers, 2, HEAD_DIM).sum(axis=1)
    dg1 = jnp.concatenate([fold(dg1_first[None]), fold(grads["next_pre_mix_g"])[:-1]], axis=0)
    small_grads = {
        "b_gate": fold(grads["b_gate"]), "q_norm_g": dgq, "k_norm_g": dgk, "q_a_norm_g": fold(grads["q_a_norm_g"]),
        "kv_a_norm_g": fold(grads["kv_a_norm_g"]), "pre_mix_g": dg1, "post_mix_g": fold(grads["post_mix_g"]),
        "pre_ffn_g": fold(grads["pre_ffn_g"]), "post_ffn_g": fold(grads["post_ffn_g"]),
    }
    small_packed = _pack_small([small_grads[n] for n in SMALL_NAMES])
    sends = [_shards_of(big_grads[n], SHARD_AXIS[n]).reshape((N_DEV,) + _rows2d(weights[n]).shape)
             for n in BIG_NAMES]
    sends.append(jnp.broadcast_to(small_packed[None], (N_DEV,) + small_packed.shape))
    halves = _pair_exchange(sends)
    core = lax.axis_index("c").astype(jnp.int32).reshape(1)
    recvs = _chip_exchange([_pair_add(s, h, core) for s, h in zip(sends, halves)])

    results = {}
    for n, recv in zip(BIG_NAMES, recvs):
        res = _adamw(recv, _rows2d(weights[n]), _rows2d(mom_m[n]), _rows2d(mom_v[n]))
        results[n] = [r.reshape(shard_shapes[n]) for r in res]
    res = _adamw(recvs[-1], *[_pack_small([d[n] for n in SMALL_NAMES]) for d in (weights, mom_m, mom_v)])
    for kind, packed_out in enumerate(res):
        for n, val in zip(SMALL_NAMES, _unpack_small(packed_out, small_shapes)):
            results.setdefault(n, [None] * 4)[kind] = val
    outs = [results[n][kind] for kind in range(4) for n in WEIGHT_NAMES]
    return (loss, grad_x.reshape(x.shape), *outs)
```
